```python
import jax
import jax.numpy as jnp
from jax import lax
import numpy as np

D_MODEL = 1024
BATCH = 32
SEQ = 256
DEPTH = 1
DEC_BATCH = 4
DEC_SEQ = 1024
PAST_LEN = 512

GRID_W = 64
EPS = 1e-6
D_RNN = 1024
LRU_BLOCKS = 16
LRU_BLOCK = D_RNN // LRU_BLOCKS
CONV_WIDTH = 4
LRU_C = 8.0
N_HEADS = 8
Q_LORA = 384
KV_LORA = 256
NOPE_DIM = 64
ROPE_DIM = 32
QK_DIM = NOPE_DIM + ROPE_DIM
V_DIM = 64
ROPE_BASE = 10000.0
Q_BLOCK = 128
N_BRANCHES = 2
IN_WIDTH = 2 * D_RNN + Q_LORA + KV_LORA + ROPE_DIM + N_BRANCHES * D_MODEL
SPLITS = (D_RNN, 2 * D_RNN, 2 * D_RNN + Q_LORA, 2 * D_RNN + Q_LORA + KV_LORA, 2 * D_RNN + Q_LORA + KV_LORA + ROPE_DIM)
N_GROUPS = 4
EXPERTS_PER_GROUP = 4
N_EXPERTS = N_GROUPS * EXPERTS_PER_GROUP
TOP_K_IN_GROUP = 2
D_EXPERT = 512

kernel_name = 'hybrid_rglru_mla_hmoe_diffusion_step'


def rms_norm(x, g):
    xf = x.astype(jnp.float32)
    y = xf * lax.rsqrt(jnp.mean(xf * xf, axis=-1, keepdims=True) + EPS)
    return (y * g.astype(jnp.float32)).astype(x.dtype)


def modulation(cond, w_mod, b_mod):
    m = jax.nn.silu(cond) @ w_mod + b_mod
    return [t[:, None, :] for t in jnp.split(m, 6, axis=-1)]


def centred_depthwise_conv(x, w, b):
    t_len = x.shape[1]
    left = (CONV_WIDTH - 1) // 2
    right = CONV_WIDTH - 1 - left
    xp = jnp.pad(x, ((0, 0), (left, right), (0, 0)))
    y = b
    for k in range(CONV_WIDTH):
        y = y + xp[:, k:k + t_len, :] * w[k]
    return y


def linear_scan(a, b, h0, reverse):
    if reverse:
        a, b = jnp.flip(a, axis=1), jnp.flip(b, axis=1)

    def combine(left, right):
        a_l, b_l = left
        a_r, b_r = right
        return a_l * a_r, a_r * b_l + b_r

    a_cum, b_cum = lax.associative_scan(combine, (a, b), axis=1)
    h = a_cum * h0[:, None, :] + b_cum
    return jnp.flip(h, axis=1) if reverse else h


def rglru_direction(xc, h0, wa, ba, wx, bx, lam, reverse):
    bsz, t_len, _ = xc.shape
    xb = xc.reshape(bsz, t_len, LRU_BLOCKS, LRU_BLOCK)
    r = jax.nn.sigmoid(jnp.einsum('btnk,nkj->btnj', xb, wa).reshape(bsz, t_len, D_RNN) + ba)
    i = jax.nn.sigmoid(jnp.einsum('btnk,nkj->btnj', xb, wx).reshape(bsz, t_len, D_RNN) + bx)
    log_a = -LRU_C * r.astype(jnp.float32) * jax.nn.softplus(-lam.astype(jnp.float32))
    a = jnp.exp(log_a)
    b = jnp.sqrt(-jnp.expm1(2.0 * log_a)) * (i * xc).astype(jnp.float32)
    return linear_scan(a, b, h0.astype(jnp.float32), reverse)


def rglru_branch(x_rnn, g_rnn, h0, lw):
    xc = centred_depthwise_conv(x_rnn, lw['conv_w'], lw['conv_b'])
    h_f = rglru_direction(xc, h0[:, 0], lw['lru_wa'][0], lw['lru_ba'][0], lw['lru_wx'][0], lw['lru_bx'][0], lw['lru_lam'][0], False)
    h_b = rglru_direction(xc, h0[:, 1], lw['lru_wa'][1], lw['lru_ba'][1], lw['lru_wx'][1], lw['lru_bx'][1], lw['lru_lam'][1], True)
    y = (h_f + h_b).astype(x_rnn.dtype) * jax.nn.gelu(g_rnn)
    h_final = jnp.stack([h_f[:, -1], h_b[:, 0]], axis=1)
    return y, h_final


def axial_rope_tables(n_tokens):
    rows = n_tokens // GRID_W
    row = jnp.repeat(jnp.arange(rows), GRID_W).astype(jnp.float32)
    col = jnp.tile(jnp.arange(GRID_W), rows).astype(jnp.float32)
    axis_dim = ROPE_DIM // 2
    inv = ROPE_BASE ** (-jnp.arange(0, axis_dim, 2, dtype=jnp.float32) / axis_dim)
    ang = jnp.concatenate([row[:, None] * inv, col[:, None] * inv], axis=-1)
    return jnp.cos(ang), jnp.sin(ang)


def apply_rope(x, cos, sin):
    x1 = x[..., 0::2].astype(jnp.float32)
    x2 = x[..., 1::2].astype(jnp.float32)
    cs = cos[None, :, None, :]
    sn = sin[None, :, None, :]
    out = jnp.stack([x1 * cs - x2 * sn, x1 * sn + x2 * cs], axis=-1)
    return out.reshape(x.shape).astype(x.dtype)


def mla_keys_values(ckv, k_rope, lw):
    bsz, t_len, _ = ckv.shape
    k_nope = (ckv @ lw['w_uk']).reshape(bsz, t_len, N_HEADS, NOPE_DIM)
    k_r = jnp.broadcast_to(k_rope[:, :, None, :], (bsz, t_len, N_HEADS, ROPE_DIM))
    k = rms_norm(jnp.concatenate([k_nope, k_r], axis=-1), lw['k_norm'])
    v = (ckv @ lw['w_uv']).reshape(bsz, t_len, N_HEADS, V_DIM)
    return k, v


def block_attention(q, k, v):
    bsz, t_len, n_h, dk = q.shape
    n_blk = t_len // Q_BLOCK
    qb = jnp.moveaxis(q.reshape(bsz, n_blk, Q_BLOCK, n_h, dk), 1, 0)

    def one_block(qi):
        s = jnp.einsum('bqhd,bkhd->bhqk', qi, k).astype(jnp.float32) * (QK_DIM ** -0.5)
        p = jax.nn.softmax(s, axis=-1).astype(v.dtype)
        return jnp.einsum('bhqk,bkhd->bqhd', p, v)

    o = lax.map(one_block, qb)
    return jnp.moveaxis(o, 0, 1).reshape(bsz, t_len, n_h * V_DIM)


def token_mixer(h, rnn_h0, ctx_ckv, ctx_krope, rope, lw):
    bsz, t_len, _ = h.shape
    proj = h @ lw['w_in']
    x_rnn, g_rnn, q_lat, kv_lat, k_rope, gate_logits = jnp.split(proj, list(SPLITS), axis=-1)
    y_rnn, h_final = rglru_branch(x_rnn, g_rnn, rnn_h0, lw)
    q = (rms_norm(q_lat, lw['q_a_norm']) @ lw['w_uq']).reshape(bsz, t_len, N_HEADS, QK_DIM)
    q = rms_norm(q, lw['q_norm'])
    ckv = rms_norm(kv_lat, lw['kv_a_norm'])
    k, v = mla_keys_values(ckv, k_rope, lw)
    if rope is not None:
        cos, sin = rope
        q = jnp.concatenate([q[..., :NOPE_DIM], apply_rope(q[..., NOPE_DIM:], cos, sin)], axis=-1)
        k = jnp.concatenate([k[..., :NOPE_DIM], apply_rope(k[..., NOPE_DIM:], cos, sin)], axis=-1)
    if ctx_ckv is not None:
        k_ctx, v_ctx = mla_keys_values(ctx_ckv, ctx_krope, lw)
        k = jnp.concatenate([k_ctx, k], axis=1)
        v = jnp.concatenate([v_ctx, v], axis=1)
    y_att = block_attention(q, k, v)
    gates = jax.nn.sigmoid(gate_logits).reshape(bsz, t_len, N_BRANCHES, D_MODEL)
    merged = gates[:, :, 0] * (y_rnn @ lw['w_o_rnn']) + gates[:, :, 1] * (y_att @ lw['w_o_mla'])
    return merged @ lw['w_out'], h_final, ckv, k_rope


def hier_moe(h, lw):
    bsz, t_len, d = h.shape
    xt = h.reshape(bsz * t_len, d)
    n_tok = xt.shape[0]
    g_logits = (xt @ lw['router_wg'] + lw['router_bg']).astype(jnp.float32)
    g_prob = jax.nn.softmax(g_logits, axis=-1)
    g_idx = jnp.argmax(g_logits, axis=-1)
    g_w = jnp.take_along_axis(g_prob, g_idx[:, None], axis=-1)
    e_logits = (xt @ lw['router_we'] + lw['router_be']).astype(jnp.float32).reshape(n_tok, N_GROUPS, EXPERTS_PER_GROUP)
    sel = jnp.broadcast_to(g_idx[:, None, None], (n_tok, 1, EXPERTS_PER_GROUP))
    e_in = jnp.take_along_axis(e_logits, sel, axis=1)[:, 0]
    top_v, top_i = lax.top_k(e_in, TOP_K_IN_GROUP)
    w = jax.nn.softmax(top_v, axis=-1) * g_w
    expert_id = g_idx[:, None] * EXPERTS_PER_GROUP + top_i
    combine = jnp.sum(jax.nn.one_hot(expert_id, N_EXPERTS, dtype=jnp.float32) * w[..., None], axis=1)
    out = jnp.zeros((n_tok, d), jnp.float32)
    for e in range(N_EXPERTS):
        he = jax.nn.silu(xt @ lw['exp_w1'][e]) * (xt @ lw['exp_w3'][e])
        out = out + combine[:, e:e + 1] * (he @ lw['exp_w2'][e]).astype(jnp.float32)
    return out.astype(h.dtype).reshape(bsz, t_len, d)


def context_layer(x, c_ctx, lw):
    bsz = x.shape[0]
    sh1, sc1, g1, sh2, sc2, g2 = modulation(c_ctx[None, :], lw['w_mod'], lw['b_mod'])
    h = rms_norm(x, lw['norm1_g']) * (1.0 + sc1) + sh1
    h0 = jnp.zeros((bsz, 2, D_RNN), jnp.float32)
    mix, h_final, ckv, k_rope = token_mixer(h, h0, None, None, None, lw)
    x = x + g1 * mix
    h = rms_norm(x, lw['norm2_g']) * (1.0 + sc2) + sh2
    x = x + g2 * hier_moe(h, lw)
    return x, ckv, k_rope, h_final.astype(x.dtype)


def latent_layer(x, cond, rnn_h0, ctx_ckv, ctx_krope, rope, lw):
    sh1, sc1, g1, sh2, sc2, g2 = modulation(cond, lw['w_mod'], lw['b_mod'])
    h = rms_norm(x, lw['norm1_g']) * (1.0 + sc1) + sh1
    mix = token_mixer(h, rnn_h0, ctx_ckv, ctx_krope, rope, lw)[0]
    x = x + g1 * mix
    h = rms_norm(x, lw['norm2_g']) * (1.0 + sc2) + sh2
    return x + g2 * hier_moe(h, lw)


def setup_inputs(seed: int = 0) -> dict:
    key = jax.random.key(seed)
    ks = iter(jax.random.split(key, 48))
    f32 = jnp.float32

    def nrm(shape, scale):
        return jax.random.normal(next(ks), shape, f32) * scale

    L = DEPTH
    u = jax.random.uniform(next(ks), (L, 2, D_RNN), f32, minval=0.9, maxval=0.999)
    s = u ** (1.0 / LRU_C)
    return {
        'x_prompt': nrm((BATCH, SEQ, D_MODEL), 1.0),
        'x_sample': nrm((DEC_BATCH, DEC_SEQ, D_MODEL), 1.0),
        'cache_mla_ckv': nrm((DEC_BATCH, L, PAST_LEN, KV_LORA), 1.0),
        'cache_mla_krope': nrm((DEC_BATCH, L, PAST_LEN, ROPE_DIM), 1.0),
        'state_rglru': nrm((DEC_BATCH, L, 2, D_RNN), 0.5),
        'c': nrm((DEC_BATCH, D_MODEL), 1.0),
        'c_ctx': nrm((D_MODEL,), 1.0),
        'norm1_g': 1.0 + nrm((L, D_MODEL), 0.05),
        'norm2_g': 1.0 + nrm((L, D_MODEL), 0.05),
        'w_mod': nrm((L, D_MODEL, 6 * D_MODEL), 0.5 * D_MODEL ** -0.5),
        'b_mod': nrm((L, 6 * D_MODEL), 0.01),
        'w_in': nrm((L, D_MODEL, IN_WIDTH), D_MODEL ** -0.5),
        'conv_w': nrm((L, CONV_WIDTH, D_RNN), CONV_WIDTH ** -0.5),
        'conv_b': nrm((L, D_RNN), 0.01),
        'lru_wa': nrm((L, 2, LRU_BLOCKS, LRU_BLOCK, LRU_BLOCK), LRU_BLOCK ** -0.5),
        'lru_ba': nrm((L, 2, D_RNN), 0.01),
        'lru_wx': nrm((L, 2, LRU_BLOCKS, LRU_BLOCK, LRU_BLOCK), LRU_BLOCK ** -0.5),
        'lru_bx': nrm((L, 2, D_RNN), 0.01),
        'lru_lam': jnp.log(s) - jnp.log1p(-s),
        'q_a_norm': 1.0 + nrm((L, Q_LORA), 0.05),
        'kv_a_norm': 1.0 + nrm((L, KV_LORA), 0.05),
        'w_uq': nrm((L, Q_LORA, N_HEADS * QK_DIM), Q_LORA ** -0.5),
        'w_uk': nrm((L, KV_LORA, N_HEADS * NOPE_DIM), KV_LORA ** -0.5),
        'w_uv': nrm((L, KV_LORA, N_HEADS * V_DIM), KV_LORA ** -0.5),
        'q_norm': 1.0 + nrm((L, QK_DIM), 0.05),
        'k_norm': 1.0 + nrm((L, QK_DIM), 0.05),
        'w_o_rnn': nrm((L, D_RNN, D_MODEL), D_RNN ** -0.5),
        'w_o_mla': nrm((L, N_HEADS * V_DIM, D_MODEL), (N_HEADS * V_DIM) ** -0.5),
        'w_out': nrm((L, D_MODEL, D_MODEL), D_MODEL ** -0.5),
        'router_wg': nrm((L, D_MODEL, N_GROUPS), D_MODEL ** -0.5),
        'router_bg': nrm((L, N_GROUPS), 0.01),
        'router_we': nrm((L, D_MODEL, N_EXPERTS), D_MODEL ** -0.5),
        'router_be': nrm((L, N_EXPERTS), 0.01),
        'exp_w1': nrm((L, N_EXPERTS, D_MODEL, D_EXPERT), D_MODEL ** -0.5),
        'exp_w3': nrm((L, N_EXPERTS, D_MODEL, D_EXPERT), D_MODEL ** -0.5),
        'exp_w2': nrm((L, N_EXPERTS, D_EXPERT, D_MODEL), D_EXPERT ** -0.5),
    }


def reference(x_prompt, x_sample, cache_mla_ckv, cache_mla_krope, state_rglru, c, c_ctx,
              norm1_g, norm2_g, w_mod, b_mod, w_in, conv_w, conv_b,
              lru_wa, lru_ba, lru_wx, lru_bx, lru_lam,
              q_a_norm, kv_a_norm, w_uq, w_uk, w_uv, q_norm, k_norm,
              w_o_rnn, w_o_mla, w_out,
              router_wg, router_bg, router_we, router_be,
              exp_w1, exp_w3, exp_w2):
    rope = axial_rope_tables(x_sample.shape[1])
    y_prompt = x_prompt
    y_sample = x_sample
    ckv_list, krope_list, rnn_list = [], [], []
    for l in range(DEPTH):
        lw = {
            'norm1_g': norm1_g[l], 'norm2_g': norm2_g[l], 'w_mod': w_mod[l], 'b_mod': b_mod[l],
            'w_in': w_in[l], 'conv_w': conv_w[l], 'conv_b': conv_b[l],
            'lru_wa': lru_wa[l], 'lru_ba': lru_ba[l], 'lru_wx': lru_wx[l], 'lru_bx': lru_bx[l], 'lru_lam': lru_lam[l],
            'q_a_norm': q_a_norm[l], 'kv_a_norm': kv_a_norm[l], 'w_uq': w_uq[l], 'w_uk': w_uk[l], 'w_uv': w_uv[l],
            'q_norm': q_norm[l], 'k_norm': k_norm[l],
            'w_o_rnn': w_o_rnn[l], 'w_o_mla': w_o_mla[l], 'w_out': w_out[l],
            'router_wg': router_wg[l], 'router_bg': router_bg[l], 'router_we': router_we[l], 'router_be': router_be[l],
            'exp_w1': exp_w1[l], 'exp_w3': exp_w3[l], 'exp_w2': exp_w2[l],
        }
        y_prompt, ckv, krope, h_fin = context_layer(y_prompt, c_ctx, lw)
        ckv_list.append(ckv)
        krope_list.append(krope)
        rnn_list.append(h_fin)
        y_sample = latent_layer(y_sample, c, state_rglru[:, l], cache_mla_ckv[:, l], cache_mla_krope[:, l], rope, lw)
    new_mla_ckv = jnp.stack(ckv_list, axis=1)
    new_mla_krope = jnp.stack(krope_list, axis=1)
    new_state_rglru = jnp.stack(rnn_list, axis=1)
    return (y_prompt, y_sample, new_mla_ckv, new_mla_krope, new_state_rglru)
```

```python
import functools
import math

import numpy as np
import jax
import jax.numpy as jnp
from jax import lax
from jax.experimental import pallas as pl
from jax.experimental.pallas import tpu as pltpu

D = 1024
DR = 1024
QL = 384
KVL = 256
NH = 8
NOPE = 64
ROPE = 32
QK = NOPE + ROPE
VD = 64
HP = 128
DH = NH * HP
GRID_W = 64
ROPE_BASE = 10000.0
EPS = 1e-6
LRU_C = 8.0
LRU_BLOCK = 64
CH = 256
NLT = CH // 128
NG = 4
EPG = 4
NE = NG * EPG
DE = 512
LANES = 128
SUBLANES = 8
TM = 256
TMOE = 1024
VMEM_LIMIT = 52 * 1024 * 1024
BF = jnp.bfloat16
F32 = jnp.float32


def _cparams(sem):
    return pltpu.CompilerParams(dimension_semantics=sem, vmem_limit_bytes=VMEM_LIMIT)


def _dot(a, b):
    return jnp.dot(a, b, preferred_element_type=F32)


def _dot_nt(a, b):
    return lax.dot_general(a, b, (((1,), (1,)), ((), ())), preferred_element_type=F32)


def _rms(x, g, width):
    ms = jnp.sum(x * x, axis=-1, keepdims=True) * (1.0 / width)
    return x * lax.rsqrt(ms + EPS) * g


def _mod_kernel(c_ref, w_ref, b_ref, o_ref):
    c = c_ref[...]
    s = c * jax.nn.sigmoid(c)
    o_ref[...] = _dot(s, w_ref[...]) + b_ref[...]


def _modulation(cond8, w_mod, b_mod):
    n = w_mod.shape[1]
    return pl.pallas_call(
        _mod_kernel,
        grid=(n // D,),
        in_specs=[
            pl.BlockSpec((SUBLANES, D), lambda j: (0, 0)),
            pl.BlockSpec((D, D), lambda j: (0, j)),
            pl.BlockSpec((1, D), lambda j: (0, j)),
        ],
        out_specs=pl.BlockSpec((SUBLANES, D), lambda j: (0, j)),
        out_shape=jax.ShapeDtypeStruct((SUBLANES, n), F32),
        compiler_params=_cparams(("arbitrary",)),
        name="modulation",
    )(cond8, w_mod, b_mod.reshape(1, n))


def _head_norm(xh, gain, cos, sins):
    ms = jnp.sum(xh * xh, axis=-1, keepdims=True) * (1.0 / QK)
    y = xh * lax.rsqrt(ms + EPS) * gain
    if cos is None:
        return y
    lane = lax.broadcasted_iota(jnp.int32, y.shape, 1)
    first = (lane >= NOPE) & (lane < NOPE + ROPE // 2)
    partner = jnp.where(first, pltpu.roll(y, HP - ROPE // 2, 1), pltpu.roll(y, ROPE // 2, 1))
    return y * cos + partner * sins


def _keys_values(ckv, krp, wuk_ref, wuv_ref, gk, cos, sins, k_ref, v_ref):
    cb = ckv.astype(BF)
    kn = _dot(cb, wuk_ref[...])
    v_ref[0] = _dot(cb, wuv_ref[...]).astype(BF)
    for h in range(NH):
        kh = kn[:, h * HP:(h + 1) * HP] + krp
        k_ref[0, :, h * HP:(h + 1) * HP] = _head_norm(kh, gk, cos, sins).astype(BF)


def _proj_kernel(*refs, rope, emit_cache):
    it = iter(refs)
    x_ref, mod_ref, n1_ref = next(it), next(it), next(it)
    wx_ref, wg_ref, wq_ref, wkvr_ref = next(it), next(it), next(it), next(it)
    qan_ref, kvan_ref, wuq_ref, gq_ref = next(it), next(it), next(it), next(it)
    wuk_ref, gk_ref, wuv_ref = next(it), next(it), next(it)
    cos = sins = None
    if rope:
        cos, sins = next(it)[...], next(it)[...]
    h_ref, xr_ref, gg_ref, q_ref, k_ref, v_ref = (next(it) for _ in range(6))
    if emit_cache:
        ckv_ref, kro_ref = next(it), next(it)

    x = x_ref[0]
    sh1 = mod_ref[0, 0:1, :]
    sc1 = mod_ref[0, 1:2, :]
    h = _rms(x, n1_ref[...], D) * (1.0 + sc1) + sh1
    hb = h.astype(BF)
    h_ref[0] = hb
    xr_ref[0] = _dot(hb, wx_ref[...])
    gg_ref[0] = jax.nn.gelu(_dot(hb, wg_ref[...])).astype(BF)

    qn = _rms(_dot(hb, wq_ref[...]), qan_ref[...], QL)
    q = _dot(qn.astype(BF), wuq_ref[...])
    gq = gq_ref[...]
    for hd in range(NH):
        qh = q[:, hd * HP:(hd + 1) * HP]
        q_ref[0, :, hd * HP:(hd + 1) * HP] = _head_norm(qh, gq, cos, sins).astype(BF)

    kvr = _dot(hb, wkvr_ref[...])
    ckv = _rms(kvr[:, :KVL], kvan_ref[...], KVL)
    krp = kvr[:, KVL:KVL + HP]
    if emit_cache:
        ckv_ref[0] = ckv
        kro_ref[0] = krp[:, NOPE:NOPE + ROPE]
    _keys_values(ckv, krp, wuk_ref, wuv_ref, gk_ref[...], cos, sins, k_ref, v_ref)


def _projections(x, mod, mod_row, wts, rope_tabs, emit_cache):
    b, t, _ = x.shape
    nt = t // TM
    rope = rope_tabs is not None
    full = lambda shape: pl.BlockSpec(shape, lambda i, j: (0,) * len(shape))
    tile = lambda w: pl.BlockSpec((1, TM, w), lambda i, j: (i, j, 0))
    in_specs = [
        tile(D),
        pl.BlockSpec((1, 6, D), lambda i, j: (mod_row(i), 0, 0)),
        full((1, D)),
        full((D, DR)), full((D, DR)), full((D, QL)), full((D, KVL + HP)),
        full((1, QL)), full((1, KVL)), full((QL, DH)), full((1, HP)),
        full((KVL, DH)), full((1, HP)), full((KVL, DH)),
    ]
    args = [x, mod, wts["n1"], wts["w_x"], wts["w_g"], wts["w_q"], wts["w_kvr"],
            wts["qan"], wts["kvan"], wts["w_uq"], wts["gq"], wts["w_uk"], wts["gk"], wts["w_uv"]]
    if rope:
        in_specs += [pl.BlockSpec((TM, HP), lambda i, j: (j, 0))] * 2
        args += list(rope_tabs)
    out_specs = [tile(D), tile(DR), tile(DR), tile(DH), tile(DH), tile(DH)]
    out_shape = [jax.ShapeDtypeStruct((b, t, D), BF), jax.ShapeDtypeStruct((b, t, DR), F32),
                 jax.ShapeDtypeStruct((b, t, DR), BF), jax.ShapeDtypeStruct((b, t, DH), BF),
                 jax.ShapeDtypeStruct((b, t, DH), BF), jax.ShapeDtypeStruct((b, t, DH), BF)]
    if emit_cache:
        out_specs += [tile(KVL), tile(ROPE)]
        out_shape += [jax.ShapeDtypeStruct((b, t, KVL), F32), jax.ShapeDtypeStruct((b, t, ROPE), F32)]
    return pl.pallas_call(
        functools.partial(_proj_kernel, rope=rope, emit_cache=emit_cache),
        grid=(b, nt),
        in_specs=in_specs,
        out_specs=out_specs,
        out_shape=out_shape,
        compiler_params=_cparams(("arbitrary", "arbitrary")),
        name="projections",
    )(*args)


def _cache_kv_kernel(ckv_ref, krp_ref, wuk_ref, gk_ref, wuv_ref, k_ref, v_ref):
    _keys_values(ckv_ref[0], krp_ref[0], wuk_ref, wuv_ref, gk_ref[...], None, None, k_ref, v_ref)


def _cache_keys_values(ckv, krp, wts):
    b, s, _ = ckv.shape
    full = lambda shape: pl.BlockSpec(shape, lambda i: (0,) * len(shape))
    return pl.pallas_call(
        _cache_kv_kernel,
        grid=(b,),
        in_specs=[pl.BlockSpec((1, s, KVL), lambda i: (i, 0, 0)),
                  pl.BlockSpec((1, s, HP), lambda i: (i, 0, 0)),
                  full((KVL, DH)), full((1, HP)), full((KVL, DH))],
        out_specs=[pl.BlockSpec((1, s, DH), lambda i: (i, 0, 0))] * 2,
        out_shape=[jax.ShapeDtypeStruct((b, s, DH), BF)] * 2,
        compiler_params=_cparams(("arbitrary",)),
        name="cache_keys_values",
    )(ckv, krp, wts["w_uk"], wts["gk"], wts["w_uv"])


def _rglru_kernel(*refs, t, has_h0, emit_state):
    it = iter(refs)
    xr_ref, gg_ref, cw_ref, cb_ref, bd_ref, ba_ref, bx_ref, lam_ref = (next(it) for _ in range(8))
    h0_ref = next(it) if has_h0 else None
    y_ref = next(it)
    hf_ref = next(it) if emit_state else None
    a_scr, b_scr, hl_scr, pc_scr = (next(it) for _ in range(4))

    seg = t // SUBLANES
    x = xr_ref[0]
    row = lax.broadcasted_iota(jnp.int32, x.shape, 0)
    xm1 = jnp.where(row >= 1, pltpu.roll(x, 1, 0), 0.0)
    xp1 = jnp.where(row < t - 1, pltpu.roll(x, t - 1, 0), 0.0)
    xp2 = jnp.where(row < t - 2, pltpu.roll(x, t - 2, 0), 0.0)
    xc = (cb_ref[...] + xm1 * cw_ref[0:1, :] + x * cw_ref[1:2, :]
          + xp1 * cw_ref[2:3, :] + xp2 * cw_ref[3:4, :])
    xcb = xc.astype(BF)
    for d in range(2):
        r = jax.nn.sigmoid(_dot(xcb, bd_ref[2 * d, 0]) + ba_ref[d:d + 1, :])
        gi = jax.nn.sigmoid(_dot(xcb, bd_ref[2 * d + 1, 0]) + bx_ref[d:d + 1, :])
        nl = -lam_ref[d:d + 1, :]
        softplus = jnp.maximum(nl, 0.0) + jnp.log(1.0 + jnp.exp(-jnp.abs(nl)))
        log_a = (-LRU_C) * r * softplus
        a = jnp.exp(log_a)
        b = jnp.sqrt(1.0 - jnp.exp(2.0 * log_a)) * (gi * xc)
        for c in range(NLT):
            a_scr[d * NLT + c] = a[:, c * LANES:(c + 1) * LANES]
            b_scr[d * NLT + c] = b[:, c * LANES:(c + 1) * LANES]

    def step(i, carry):
        out = []
        for k in range(2 * NLT):
            hk, pk = carry[k]
            pos = i if k < NLT else seg - 1 - i
            rows = pl.ds(pos, SUBLANES, stride=seg)
            ak = a_scr[k, rows, :]
            hk = ak * hk + b_scr[k, rows, :]
            pk = ak * pk
            hl_scr[k, rows, :] = hk
            pc_scr[k, rows, :] = pk
            out.append((hk, pk))
        return tuple(out)

    init = tuple((jnp.zeros((SUBLANES, LANES), F32), jnp.ones((SUBLANES, LANES), F32))
                 for _ in range(2 * NLT))
    final = lax.fori_loop(0, seg, step, init)

    for c in range(NLT):
        lanes = slice(c * LANES, (c + 1) * LANES)
        hf, pf = final[c]
        hb, pb = final[NLT + c]
        if has_h0:
            cf = h0_ref[0, 0:1, lanes]
            cbk = h0_ref[0, 1:2, lanes]
        else:
            cf = jnp.zeros((1, LANES), F32)
            cbk = jnp.zeros((1, LANES), F32)
        carry_f = [None] * SUBLANES
        carry_b = [None] * SUBLANES
        for s in range(SUBLANES):
            carry_f[s] = cf
            cf = hf[s:s + 1, :] + pf[s:s + 1, :] * cf
        for s in reversed(range(SUBLANES)):
            carry_b[s] = cbk
            cbk = hb[s:s + 1, :] + pb[s:s + 1, :] * cbk
        if emit_state:
            hf_ref[0, 0:1, lanes] = cf
            hf_ref[0, 1:2, lanes] = cbk
        for s in range(SUBLANES):
            rows = pl.ds(s * seg, seg)
            hsum = (hl_scr[c, rows, :] + pc_scr[c, rows, :] * carry_f[s]
                    + hl_scr[NLT + c, rows, :] + pc_scr[NLT + c, rows, :] * carry_b[s])
            y_ref[0, rows, lanes] = (hsum * gg_ref[0, rows, lanes].astype(F32)).astype(BF)


def _rglru(xr, gg, wts, h0, emit_state):
    b, t, _ = xr.shape
    nc = DR // CH
    has_h0 = h0 is not None
    chunk = lambda r: pl.BlockSpec((r, CH), lambda i, j: (0, j))
    seq = pl.BlockSpec((1, t, CH), lambda i, j: (i, 0, j))
    state = pl.BlockSpec((1, 2, CH), lambda i, j: (i, 0, j))
    in_specs = [seq, seq, chunk(4), chunk(1),
                pl.BlockSpec((4, 1, CH, CH), lambda i, j: (0, j, 0, 0)),
                chunk(2), chunk(2), chunk(2)]
    args = [xr, gg, wts["conv_w"], wts["conv_b"], wts["bd"], wts["lru_ba"], wts["lru_bx"], wts["lru_lam"]]
    if has_h0:
        in_specs.append(state)
        args.append(h0)
    out_specs = [seq]
    out_shape = [jax.ShapeDtypeStruct((b, t, DR), BF)]
    if emit_state:
        out_specs.append(state)
        out_shape.append(jax.ShapeDtypeStruct((b, 2, DR), F32))
    res = pl.pallas_call(
        functools.partial(_rglru_kernel, t=t, has_h0=has_h0, emit_state=emit_state),
        grid=(b, nc),
        in_specs=in_specs,
        out_specs=out_specs,
        out_shape=out_shape,
        scratch_shapes=[pltpu.VMEM((2 * NLT, t, LANES), F32)] * 4,
        compiler_params=_cparams(("arbitrary", "arbitrary")),
        name="rglru",
    )(*args)
    return res if emit_state else (res[0], None)


def _attn_kernel(*refs, t, n_heads, has_ctx, q_block):
    it = iter(refs)
    q_ref, k_ref, v_ref = next(it), next(it), next(it)
    kc_ref = vc_ref = None
    if has_ctx:
        kc_ref, vc_ref = next(it), next(it)
    o_ref = next(it)
    scale = QK ** -0.5
    for hd in range(n_heads):
        cols = slice(hd * HP, (hd + 1) * HP)
        k = k_ref[0, :, cols]
        v = v_ref[0, :, cols]
        if has_ctx:
            kc = kc_ref[0, :, cols]
            vc = vc_ref[0, :, cols]
        for qb in range(t // q_block):
            rows = slice(qb * q_block, (qb + 1) * q_block)
            q = q_ref[0, rows, cols]
            s = _dot_nt(q, k) * scale
            m = jnp.max(s, axis=-1, keepdims=True)
            if has_ctx:
                sc = _dot_nt(q, kc) * scale
                m = jnp.maximum(m, jnp.max(sc, axis=-1, keepdims=True))
            p = jnp.exp(s - m)
            den = jnp.sum(p, axis=-1, keepdims=True)
            o = _dot(p.astype(BF), v)
            if has_ctx:
                pc = jnp.exp(sc - m)
                den = den + jnp.sum(pc, axis=-1, keepdims=True)
                o = o + _dot(pc.astype(BF), vc)
            o_ref[0, rows, cols] = (o / den).astype(BF)


def _attention(q, k, v, kc, vc, heads_per_step):
    b, t, _ = q.shape
    has_ctx = kc is not None
    w = heads_per_step * HP
    blk = lambda n: pl.BlockSpec((1, n, w), lambda i, j: (i, 0, j))
    in_specs = [blk(t), blk(t), blk(t)]
    args = [q, k, v]
    if has_ctx:
        in_specs += [blk(kc.shape[1])] * 2
        args += [kc, vc]
    return pl.pallas_call(
        functools.partial(_attn_kernel, t=t, n_heads=heads_per_step, has_ctx=has_ctx, q_block=min(t, 256)),
        grid=(b, NH // heads_per_step),
        in_specs=in_specs,
        out_specs=blk(t),
        out_shape=jax.ShapeDtypeStruct((b, t, DH), BF),
        compiler_params=_cparams(("arbitrary", "arbitrary")),
        name="attention",
    )(*args)


def _route(logits):
    lane = lax.broadcasted_iota(jnp.int32, logits.shape, 1)
    lanef = lane.astype(F32)
    neg = -jnp.inf
    big = float(LANES)
    gl = jnp.where((lane >= NE) & (lane < NE + NG), logits, neg)
    gmax = jnp.max(gl, axis=-1, keepdims=True)
    gidx = jnp.min(jnp.where(gl == gmax, lanef, big), axis=-1, keepdims=True) - float(NE)
    gw = 1.0 / jnp.sum(jnp.exp(gl - gmax), axis=-1, keepdims=True)
    lo = gidx * float(EPG)
    el = jnp.where((lanef >= lo) & (lanef < lo + float(EPG)), logits, neg)
    v1 = jnp.max(el, axis=-1, keepdims=True)
    i1 = jnp.min(jnp.where(el == v1, lanef, big), axis=-1, keepdims=True)
    el2 = jnp.where(lanef == i1, neg, el)
    v2 = jnp.max(el2, axis=-1, keepdims=True)
    i2 = jnp.min(jnp.where(el2 == v2, lanef, big), axis=-1, keepdims=True)
    e2 = jnp.exp(v2 - v1)
    w1 = gw / (1.0 + e2)
    w2 = gw * e2 / (1.0 + e2)
    return jnp.where(lanef == i1, w1, 0.0) + jnp.where(lanef == i2, w2, 0.0)


def _out_kernel(x_ref, h_ref, yr_ref, ya_ref, mod_ref, wgate_ref, wor_ref, wom_ref, wout_ref,
                n2_ref, rw_ref, rb_ref, x1_ref, h2_ref, cmb_ref):
    gl = _dot(h_ref[0], wgate_ref[...])
    merged = (jax.nn.sigmoid(gl[:, :D]) * _dot(yr_ref[0], wor_ref[...])
              + jax.nn.sigmoid(gl[:, D:]) * _dot(ya_ref[0], wom_ref[...]))
    mix = _dot(merged.astype(BF), wout_ref[...])
    x1 = x_ref[0] + mod_ref[0, 2:3, :] * mix
    x1_ref[0] = x1
    h2 = _rms(x1, n2_ref[...], D) * (1.0 + mod_ref[0, 4:5, :]) + mod_ref[0, 3:4, :]
    h2_ref[0] = h2.astype(BF)
    logits = jnp.dot(h2, rw_ref[...], preferred_element_type=F32,
                     precision=lax.Precision.HIGHEST) + rb_ref[...]
    cmb_ref[0] = _route(logits)


def _merge_out(x, h, yr, ya, mod, mod_row, wts):
    b, t, _ = x.shape
    full = lambda shape: pl.BlockSpec(shape, lambda i, j: (0,) * len(shape))
    tile = lambda w: pl.BlockSpec((1, TM, w), lambda i, j: (i, j, 0))
    return pl.pallas_call(
        _out_kernel,
        grid=(b, t // TM),
        in_specs=[tile(D), tile(D), tile(DR), tile(DH),
                  pl.BlockSpec((1, 6, D), lambda i, j: (mod_row(i), 0, 0)),
                  full((D, 2 * D)), full((DR, D)), full((DH, D)), full((D, D)),
                  full((1, D)), full((D, LANES)), full((1, LANES))],
        out_specs=[tile(D), tile(D), tile(LANES)],
        out_shape=[jax.ShapeDtypeStruct((b, t, D), F32), jax.ShapeDtypeStruct((b, t, D), BF),
                   jax.ShapeDtypeStruct((b, t, LANES), F32)],
        compiler_params=_cparams(("arbitrary", "arbitrary")),
        name="merge_out",
    )(x, h, yr, ya, mod, wts["w_gate"], wts["w_o_rnn"], wts["w_o_mla"], wts["w_out"],
      wts["n2"], wts["router_w"], wts["router_b"])


def _moe_kernel(h2_ref, cmb_ref, x1_ref, mod_ref, w1_ref, w3_ref, w2_ref, o_ref, acc_ref):
    e = pl.program_id(1)

    @pl.when(e == 0)
    def _():
        acc_ref[...] = jnp.zeros_like(acc_ref)

    xt = h2_ref[...]
    a = _dot(xt, w1_ref[0])
    he = (a * jax.nn.sigmoid(a)) * _dot(xt, w3_ref[0])
    y = _dot(he.astype(BF), w2_ref[0])
    cmb = cmb_ref[...]
    lane = lax.broadcasted_iota(jnp.int32, cmb.shape, 1)
    ce = jnp.sum(jnp.where(lane == e, cmb, 0.0), axis=-1, keepdims=True)
    acc_ref[...] += ce * y

    @pl.when(e == NE - 1)
    def _():
        o_ref[...] = x1_ref[...] + mod_ref[0, 5:6, :] * acc_ref[...]


def _experts(h2, cmb, x1, mod, mod_row, wts):
    n = h2.shape[0]
    tok = lambda w: pl.BlockSpec((TMOE, w), lambda i, e: (i, 0))
    return pl.pallas_call(
        _moe_kernel,
        grid=(n // TMOE, NE),
        in_specs=[tok(D), tok(LANES), tok(D),
                  pl.BlockSpec((1, 6, D), lambda i, e: (mod_row(i), 0, 0)),
                  pl.BlockSpec((1, D, DE), lambda i, e: (e, 0, 0)),
                  pl.BlockSpec((1, D, DE), lambda i, e: (e, 0, 0)),
                  pl.BlockSpec((1, DE, D), lambda i, e: (e, 0, 0))],
        out_specs=tok(D),
        out_shape=jax.ShapeDtypeStruct((n, D), F32),
        scratch_shapes=[pltpu.VMEM((TMOE, D), F32)],
        compiler_params=_cparams(("arbitrary", "arbitrary")),
        name="experts",
    )(h2, cmb, x1, mod, wts["exp_w1"], wts["exp_w3"], wts["exp_w2"])


def _pad_heads(w, widths, perm):
    lead = w.shape[:-1]
    per = w.shape[-1] // NH
    w = w.reshape(lead + (NH, per))
    if per == QK:
        w = jnp.concatenate([w[..., :NOPE], w[..., NOPE:][..., perm]], axis=-1)
    w = jnp.pad(w, [(0, 0)] * len(lead) + [(0, 0), (0, HP - per)])
    return w.reshape(lead + (NH * HP,))


def _pad_gain(g, perm):
    g = jnp.concatenate([g[:NOPE], g[NOPE:][perm], jnp.zeros((HP - QK,), F32)])
    return g.reshape(1, HP)


def _block_diag(w):
    per = CH // LRU_BLOCK
    w = w.reshape(DR // CH, per, LRU_BLOCK, LRU_BLOCK)
    bd = jnp.einsum("jpab,pq->jpaqb", w, jnp.eye(per, dtype=w.dtype))
    return bd.reshape(DR // CH, CH, CH)


def _prepare_shared(l, p):
    w_in = p["w_in"][l]
    o1, o2, o3, o4, o5 = DR, 2 * DR, 2 * DR + QL, 2 * DR + QL + KVL, 2 * DR + QL + KVL + ROPE
    bd = jnp.stack([_block_diag(p["lru_wa"][l, 0]), _block_diag(p["lru_wx"][l, 0]),
                    _block_diag(p["lru_wa"][l, 1]), _block_diag(p["lru_wx"][l, 1])]).astype(BF)
    wom = p["w_o_mla"][l].reshape(NH, VD, D)
    wom = jnp.pad(wom, ((0, 0), (0, HP - VD), (0, 0))).reshape(DH, D)
    router_w = jnp.concatenate([p["router_we"][l], p["router_wg"][l],
                                jnp.zeros((D, LANES - NE - NG), F32)], axis=1)
    router_b = jnp.concatenate([p["router_be"][l], p["router_bg"][l],
                                jnp.zeros((LANES - NE - NG,), F32)]).reshape(1, LANES)
    return {
        "n1": p["norm1_g"][l].reshape(1, D), "n2": p["norm2_g"][l].reshape(1, D),
        "w_x": w_in[:, :o1].astype(BF), "w_g": w_in[:, o1:o2].astype(BF),
        "w_q": w_in[:, o2:o3].astype(BF), "w_kv": w_in[:, o3:o4], "w_kr": w_in[:, o4:o5],
        "w_gate": w_in[:, o5:].astype(BF),
        "qan": p["q_a_norm"][l].reshape(1, QL), "kvan": p["kv_a_norm"][l].reshape(1, KVL),
        "w_uk": _pad_heads(p["w_uk"][l], None, None).astype(BF),
        "w_uv": _pad_heads(p["w_uv"][l], None, None).astype(BF),
        "conv_w": p["conv_w"][l], "conv_b": p["conv_b"][l].reshape(1, DR), "bd": bd,
        "lru_ba": p["lru_ba"][l], "lru_bx": p["lru_bx"][l], "lru_lam": p["lru_lam"][l],
        "w_o_rnn": p["w_o_rnn"][l].astype(BF), "w_o_mla": wom.astype(BF), "w_out": p["w_out"][l].astype(BF),
        "router_w": router_w, "router_b": router_b,
        "exp_w1": p["exp_w1"][l].astype(BF), "exp_w3": p["exp_w3"][l].astype(BF),
        "exp_w2": p["exp_w2"][l].astype(BF),
    }


def _with_rope_order(l, p, shared, perm):
    w = dict(shared)
    zeros = lambda n: jnp.zeros((D, n), F32)
    w["w_kvr"] = jnp.concatenate([shared["w_kv"], zeros(NOPE), shared["w_kr"][:, perm],
                                  zeros(HP - QK)], axis=1).astype(BF)
    w["w_uq"] = _pad_heads(p["w_uq"][l], None, perm).astype(BF)
    w["gq"] = _pad_gain(p["q_norm"][l], perm)
    w["gk"] = _pad_gain(p["k_norm"][l], perm)
    return w


def _rope_tables(n_tokens, perm):
    rows = n_tokens // GRID_W
    row = jnp.repeat(jnp.arange(rows), GRID_W).astype(F32)
    col = jnp.tile(jnp.arange(GRID_W), rows).astype(F32)
    axis_dim = ROPE // 2
    inv = ROPE_BASE ** (-jnp.arange(0, axis_dim, 2, dtype=F32) / axis_dim)
    ang = jnp.concatenate([row[:, None] * inv, col[:, None] * inv], axis=-1)
    cos, sin = jnp.cos(ang), jnp.sin(ang)
    ones = lambda n: jnp.ones((n_tokens, n), F32)
    zeros = lambda n: jnp.zeros((n_tokens, n), F32)
    cos_t = jnp.concatenate([ones(NOPE), cos, cos, ones(HP - QK)], axis=1)
    sin_t = jnp.concatenate([zeros(NOPE), -sin, sin, zeros(HP - QK)], axis=1)
    return cos_t, sin_t


def kernel(x_prompt, x_sample, cache_mla_ckv, cache_mla_krope, state_rglru, c, c_ctx, norm1_g, norm2_g, w_mod, b_mod, w_in, conv_w, conv_b, lru_wa, lru_ba, lru_wx, lru_bx, lru_lam, q_a_norm, kv_a_norm, w_uq, w_uk, w_uv, q_norm, k_norm, w_o_rnn, w_o_mla, w_out, router_wg, router_bg, router_we, router_be, exp_w1, exp_w3, exp_w2):
    p = dict(norm1_g=norm1_g, norm2_g=norm2_g, w_in=w_in, conv_w=conv_w, conv_b=conv_b,
             lru_wa=lru_wa, lru_ba=lru_ba, lru_wx=lru_wx, lru_bx=lru_bx, lru_lam=lru_lam,
             q_a_norm=q_a_norm, kv_a_norm=kv_a_norm, w_uq=w_uq, w_uk=w_uk, w_uv=w_uv,
             q_norm=q_norm, k_norm=k_norm, w_o_rnn=w_o_rnn, w_o_mla=w_o_mla, w_out=w_out,
             router_wg=router_wg, router_bg=router_bg, router_we=router_we, router_be=router_be,
             exp_w1=exp_w1, exp_w3=exp_w3, exp_w2=exp_w2)
    depth = w_in.shape[0]
    nb, seq, _ = x_prompt.shape
    db, dseq, _ = x_sample.shape
    ident = np.arange(ROPE)
    halves = np.concatenate([np.arange(0, ROPE, 2), np.arange(1, ROPE, 2)])
    rope_tabs = _rope_tables(dseq, halves)
    cond8 = jnp.concatenate([c_ctx[None, :], c, jnp.zeros((SUBLANES - 1 - db, D), F32)], axis=0)
    ctx_row = lambda i: 0
    lat_row = lambda i: i + 1
    ctx_moe_row = lambda i: 0
    lat_moe_row = lambda i: (i * TMOE) // dseq + 1

    y_prompt, y_sample = x_prompt, x_sample
    ckv_list, krope_list, rnn_list = [], [], []
    for l in range(depth):
        shared = _prepare_shared(l, p)
        w_ctx = _with_rope_order(l, p, shared, ident)
        w_lat = _with_rope_order(l, p, shared, halves)
        mod = _modulation(cond8, w_mod[l], b_mod[l]).reshape(SUBLANES, 6, D)

        h, xr, gg, q, k, v, ckv, kro = _projections(y_prompt, mod, ctx_row, w_ctx, None, True)
        yr, h_fin = _rglru(xr, gg, shared, None, True)
        ya = _attention(q, k, v, None, None, NH)
        x1, h2, cmb = _merge_out(y_prompt, h, yr, ya, mod, ctx_row, shared)
        y_prompt = _experts(h2.reshape(nb * seq, D), cmb.reshape(nb * seq, LANES),
                            x1.reshape(nb * seq, D), mod, ctx_moe_row, shared).reshape(nb, seq, D)
        ckv_list.append(ckv)
        krope_list.append(kro)
        rnn_list.append(h_fin)

        krp_cache = jnp.pad(cache_mla_krope[:, l][..., halves], ((0, 0), (0, 0), (NOPE, HP - QK)))
        kc, vc = _cache_keys_values(cache_mla_ckv[:, l], krp_cache, w_lat)
        h, xr, gg, q, k, v = _projections(y_sample, mod, lat_row, w_lat, rope_tabs, False)
        yr, _ = _rglru(xr, gg, shared, state_rglru[:, l], False)
        ya = _attention(q, k, v, kc, vc, 1)
        x1, h2, cmb = _merge_out(y_sample, h, yr, ya, mod, lat_row, shared)
        y_sample = _experts(h2.reshape(db * dseq, D), cmb.reshape(db * dseq, LANES),
                            x1.reshape(db * dseq, D), mod, lat_moe_row, shared).reshape(db, dseq, D)

    return (y_prompt, y_sample, jnp.stack(ckv_list, axis=1), jnp.stack(krope_list, axis=1),
            jnp.stack(rnn_list, axis=1))
```

```python
import functools
import math

import numpy as np
import jax
import jax.numpy as jnp
from jax import lax
from jax.experimental import pallas as pl
from jax.experimental.pallas import tpu as pltpu

D = 1024
DR = 1024
QL = 384
KVL = 256
NH = 8
NOPE = 64
ROPE = 32
QK = NOPE + ROPE
VD = 64
HP = 128
DH = NH * HP
GRID_W = 64
ROPE_BASE = 10000.0
EPS = 1e-6
LRU_C = 8.0
LRU_BLOCK = 64
CH = 256
NLT = CH // 128
NG = 4
EPG = 4
NE = NG * EPG
DE = 512
LANES = 128
SUBLANES = 8
TM = 256
TMOE = 512
TD = 256
XW = D + 128
VMEM_LIMIT = 52 * 1024 * 1024
BF = jnp.bfloat16
F32 = jnp.float32


def _cparams(sem):
    return pltpu.CompilerParams(dimension_semantics=sem, vmem_limit_bytes=VMEM_LIMIT)


def _dot(a, b):
    return jnp.dot(a, b, preferred_element_type=F32)


def _dot_nt(a, b):
    return lax.dot_general(a, b, (((1,), (1,)), ((), ())), preferred_element_type=F32)


def _rms(x, g, width):
    ms = jnp.sum(x * x, axis=-1, keepdims=True) * (1.0 / width)
    return x * lax.rsqrt(ms + EPS) * g


def _mod_kernel(c_ref, w_ref, b_ref, o_ref):
    c = c_ref[...]
    s = c * jax.nn.sigmoid(c)
    o_ref[...] = _dot(s, w_ref[...]) + b_ref[...]


def _modulation(cond8, w_mod, b_mod):
    n = w_mod.shape[1]
    return pl.pallas_call(
        _mod_kernel,
        grid=(n // D,),
        in_specs=[
            pl.BlockSpec((SUBLANES, D), lambda j: (0, 0)),
            pl.BlockSpec((D, D), lambda j: (0, j)),
            pl.BlockSpec((1, D), lambda j: (0, j)),
        ],
        out_specs=pl.BlockSpec((SUBLANES, D), lambda j: (0, j)),
        out_shape=jax.ShapeDtypeStruct((SUBLANES, n), F32),
        compiler_params=_cparams(("arbitrary",)),
        name="modulation",
    )(cond8, w_mod, b_mod.reshape(1, n))


def _head_norm(xh, gain, cos, sins):
    ms = jnp.sum(xh * xh, axis=-1, keepdims=True) * (1.0 / QK)
    y = xh * lax.rsqrt(ms + EPS) * gain
    if cos is None:
        return y
    lane = lax.broadcasted_iota(jnp.int32, y.shape, 1)
    first = (lane >= NOPE) & (lane < NOPE + ROPE // 2)
    partner = jnp.where(first, pltpu.roll(y, HP - ROPE // 2, 1), pltpu.roll(y, ROPE // 2, 1))
    return y * cos + partner * sins


def _keys_values(ckv, krp, wuk_ref, wuv_ref, gk, cos, sins, k_ref, v_ref):
    cb = ckv.astype(BF)
    kn = _dot(cb, wuk_ref[...])
    v_ref[0] = _dot(cb, wuv_ref[...]).astype(BF)
    for h in range(NH):
        kh = kn[:, h * HP:(h + 1) * HP] + krp
        k_ref[0, :, h * HP:(h + 1) * HP] = _head_norm(kh, gk, cos, sins).astype(BF)


def _proj_kernel(*refs, rope, emit_cache):
    it = iter(refs)
    x_ref, mod_ref, n1_ref = next(it), next(it), next(it)
    wx_ref, wg_ref, wq_ref, wkvr_ref = next(it), next(it), next(it), next(it)
    qan_ref, kvan_ref, wuq_ref, gq_ref = next(it), next(it), next(it), next(it)
    wuk_ref, gk_ref, wuv_ref = next(it), next(it), next(it)
    cos = sins = None
    if rope:
        cos, sins = next(it)[...], next(it)[...]
    h_ref, xr_ref, gg_ref, q_ref, k_ref, v_ref = (next(it) for _ in range(6))
    if emit_cache:
        ckv_ref, kro_ref = next(it), next(it)

    x = x_ref[0]
    sh1 = mod_ref[0, 0:1, :]
    sc1 = mod_ref[0, 1:2, :]
    h = _rms(x, n1_ref[...], D) * (1.0 + sc1) + sh1
    hb = h.astype(BF)
    h_ref[0] = hb
    xr_ref[0] = _dot(hb, wx_ref[...])
    gg_ref[0] = jax.nn.gelu(_dot(hb, wg_ref[...])).astype(BF)

    qn = _rms(_dot(hb, wq_ref[...]), qan_ref[...], QL)
    q = _dot(qn.astype(BF), wuq_ref[...])
    gq = gq_ref[...]
    for hd in range(NH):
        qh = q[:, hd * HP:(hd + 1) * HP]
        q_ref[0, :, hd * HP:(hd + 1) * HP] = _head_norm(qh, gq, cos, sins).astype(BF)

    kvr = _dot(hb, wkvr_ref[...])
    ckv = _rms(kvr[:, :KVL], kvan_ref[...], KVL)
    krp = kvr[:, KVL:KVL + HP]
    if emit_cache:
        ckv_ref[0] = ckv
        kro_ref[0] = krp[:, NOPE:NOPE + ROPE]
    _keys_values(ckv, krp, wuk_ref, wuv_ref, gk_ref[...], cos, sins, k_ref, v_ref)


def _projections(x, mod, mod_row, wts, rope_tabs, emit_cache):
    b, t, _ = x.shape
    nt = t // TM
    rope = rope_tabs is not None
    full = lambda shape: pl.BlockSpec(shape, lambda i, j: (0,) * len(shape))
    tile = lambda w: pl.BlockSpec((1, TM, w), lambda i, j: (i, j, 0))
    in_specs = [
        tile(D),
        pl.BlockSpec((1, 6, D), lambda i, j: (mod_row(i), 0, 0)),
        full((1, D)),
        full((D, DR)), full((D, DR)), full((D, QL)), full((D, KVL + HP)),
        full((1, QL)), full((1, KVL)), full((QL, DH)), full((1, HP)),
        full((KVL, DH)), full((1, HP)), full((KVL, DH)),
    ]
    args = [x, mod, wts["n1"], wts["w_x"], wts["w_g"], wts["w_q"], wts["w_kvr"],
            wts["qan"], wts["kvan"], wts["w_uq"], wts["gq"], wts["w_uk"], wts["gk"], wts["w_uv"]]
    if rope:
        in_specs += [pl.BlockSpec((TM, HP), lambda i, j: (j, 0))] * 2
        args += list(rope_tabs)
    out_specs = [tile(D), tile(DR), tile(DR), tile(DH), tile(DH), tile(DH)]
    out_shape = [jax.ShapeDtypeStruct((b, t, D), BF), jax.ShapeDtypeStruct((b, t, DR), F32),
                 jax.ShapeDtypeStruct((b, t, DR), BF), jax.ShapeDtypeStruct((b, t, DH), BF),
                 jax.ShapeDtypeStruct((b, t, DH), BF), jax.ShapeDtypeStruct((b, t, DH), BF)]
    if emit_cache:
        out_specs += [tile(KVL), tile(ROPE)]
        out_shape += [jax.ShapeDtypeStruct((b, t, KVL), F32), jax.ShapeDtypeStruct((b, t, ROPE), F32)]
    return pl.pallas_call(
        functools.partial(_proj_kernel, rope=rope, emit_cache=emit_cache),
        grid=(b, nt),
        in_specs=in_specs,
        out_specs=out_specs,
        out_shape=out_shape,
        compiler_params=_cparams(("arbitrary", "arbitrary")),
        name="projections",
    )(*args)


def _cache_kv_kernel(ckv_ref, krp_ref, wuk_ref, gk_ref, wuv_ref, k_ref, v_ref):
    _keys_values(ckv_ref[0], krp_ref[0], wuk_ref, wuv_ref, gk_ref[...], None, None, k_ref, v_ref)


def _cache_keys_values(ckv, krp, wts):
    b, s, _ = ckv.shape
    full = lambda shape: pl.BlockSpec(shape, lambda i: (0,) * len(shape))
    return pl.pallas_call(
        _cache_kv_kernel,
        grid=(b,),
        in_specs=[pl.BlockSpec((1, s, KVL), lambda i: (i, 0, 0)),
                  pl.BlockSpec((1, s, HP), lambda i: (i, 0, 0)),
                  full((KVL, DH)), full((1, HP)), full((KVL, DH))],
        out_specs=[pl.BlockSpec((1, s, DH), lambda i: (i, 0, 0))] * 2,
        out_shape=[jax.ShapeDtypeStruct((b, s, DH), BF)] * 2,
        compiler_params=_cparams(("arbitrary",)),
        name="cache_keys_values",
    )(ckv, krp, wts["w_uk"], wts["gk"], wts["w_uv"])


def _rglru_kernel(*refs, t, has_h0, emit_state):
    it = iter(refs)
    xr_ref, gg_ref, cw_ref, cb_ref, bd_ref, ba_ref, bx_ref, lam_ref = (next(it) for _ in range(8))
    h0_ref = next(it) if has_h0 else None
    y_ref = next(it)
    hf_ref = next(it) if emit_state else None
    a_scr, b_scr, hl_scr, pc_scr = (next(it) for _ in range(4))

    seg = t // SUBLANES
    x = xr_ref[0]
    row = lax.broadcasted_iota(jnp.int32, x.shape, 0)
    xm1 = jnp.where(row >= 1, pltpu.roll(x, 1, 0), 0.0)
    xp1 = jnp.where(row < t - 1, pltpu.roll(x, t - 1, 0), 0.0)
    xp2 = jnp.where(row < t - 2, pltpu.roll(x, t - 2, 0), 0.0)
    xc = (cb_ref[...] + xm1 * cw_ref[0:1, :] + x * cw_ref[1:2, :]
          + xp1 * cw_ref[2:3, :] + xp2 * cw_ref[3:4, :])
    xcb = xc.astype(BF)
    for d in range(2):
        r = jax.nn.sigmoid(_dot(xcb, bd_ref[2 * d, 0]) + ba_ref[d:d + 1, :])
        gi = jax.nn.sigmoid(_dot(xcb, bd_ref[2 * d + 1, 0]) + bx_ref[d:d + 1, :])
        nl = -lam_ref[d:d + 1, :]
        softplus = jnp.maximum(nl, 0.0) + jnp.log(1.0 + jnp.exp(-jnp.abs(nl)))
        log_a = (-LRU_C) * r * softplus
        a = jnp.exp(log_a)
        b = jnp.sqrt(1.0 - jnp.exp(2.0 * log_a)) * (gi * xc)
        for c in range(NLT):
            a_scr[d * NLT + c] = a[:, c * LANES:(c + 1) * LANES]
            b_scr[d * NLT + c] = b[:, c * LANES:(c + 1) * LANES]

    def step(i, carry):
        out = []
        for k in range(2 * NLT):
            hk, pk = carry[k]
            pos = i if k < NLT else seg - 1 - i
            rows = pl.ds(pos, SUBLANES, stride=seg)
            ak = a_scr[k, rows, :]
            hk = ak * hk + b_scr[k, rows, :]
            pk = ak * pk
            hl_scr[k, rows, :] = hk
            pc_scr[k, rows, :] = pk
            out.append((hk, pk))
        return tuple(out)

    init = tuple((jnp.zeros((SUBLANES, LANES), F32), jnp.ones((SUBLANES, LANES), F32))
                 for _ in range(2 * NLT))
    final = lax.fori_loop(0, seg, step, init)

    for c in range(NLT):
        lanes = slice(c * LANES, (c + 1) * LANES)
        hf, pf = final[c]
        hb, pb = final[NLT + c]
        if has_h0:
            cf = h0_ref[0, 0:1, lanes]
            cbk = h0_ref[0, 1:2, lanes]
        else:
            cf = jnp.zeros((1, LANES), F32)
            cbk = jnp.zeros((1, LANES), F32)
        carry_f = [None] * SUBLANES
        carry_b = [None] * SUBLANES
        for s in range(SUBLANES):
            carry_f[s] = cf
            cf = hf[s:s + 1, :] + pf[s:s + 1, :] * cf
        for s in reversed(range(SUBLANES)):
            carry_b[s] = cbk
            cbk = hb[s:s + 1, :] + pb[s:s + 1, :] * cbk
        if emit_state:
            hf_ref[0, 0:1, lanes] = cf
            hf_ref[0, 1:2, lanes] = cbk
        for s in range(SUBLANES):
            rows = pl.ds(s * seg, seg)
            hsum = (hl_scr[c, rows, :] + pc_scr[c, rows, :] * carry_f[s]
                    + hl_scr[NLT + c, rows, :] + pc_scr[NLT + c, rows, :] * carry_b[s])
            y_ref[0, rows, lanes] = (hsum * gg_ref[0, rows, lanes].astype(F32)).astype(BF)


def _rglru(xr, gg, wts, h0, emit_state):
    b, t, _ = xr.shape
    nc = DR // CH
    has_h0 = h0 is not None
    chunk = lambda r: pl.BlockSpec((r, CH), lambda i, j: (0, j))
    seq = pl.BlockSpec((1, t, CH), lambda i, j: (i, 0, j))
    state = pl.BlockSpec((1, 2, CH), lambda i, j: (i, 0, j))
    in_specs = [seq, seq, chunk(4), chunk(1),
                pl.BlockSpec((4, 1, CH, CH), lambda i, j: (0, j, 0, 0)),
                chunk(2), chunk(2), chunk(2)]
    args = [xr, gg, wts["conv_w"], wts["conv_b"], wts["bd"], wts["lru_ba"], wts["lru_bx"], wts["lru_lam"]]
    if has_h0:
        in_specs.append(state)
        args.append(h0)
    out_specs = [seq]
    out_shape = [jax.ShapeDtypeStruct((b, t, DR), BF)]
    if emit_state:
        out_specs.append(state)
        out_shape.append(jax.ShapeDtypeStruct((b, 2, DR), F32))
    res = pl.pallas_call(
        functools.partial(_rglru_kernel, t=t, has_h0=has_h0, emit_state=emit_state),
        grid=(b, nc),
        in_specs=in_specs,
        out_specs=out_specs,
        out_shape=out_shape,
        scratch_shapes=[pltpu.VMEM((2 * NLT, t, LANES), F32)] * 4,
        compiler_params=_cparams(("arbitrary", "arbitrary")),
        name="rglru",
    )(*args)
    return res if emit_state else (res[0], None)


def _attn_kernel(*refs, t, n_heads, has_ctx, q_block):
    it = iter(refs)
    q_ref, k_ref, v_ref = next(it), next(it), next(it)
    kc_ref = vc_ref = None
    if has_ctx:
        kc_ref, vc_ref = next(it), next(it)
    o_ref = next(it)
    scale = QK ** -0.5
    for hd in range(n_heads):
        cols = slice(hd * HP, (hd + 1) * HP)
        k = k_ref[0, :, cols]
        v = v_ref[0, :, cols]
        if has_ctx:
            kc = kc_ref[0, :, cols]
            vc = vc_ref[0, :, cols]
        for qb in range(t // q_block):
            rows = slice(qb * q_block, (qb + 1) * q_block)
            q = q_ref[0, rows, cols]
            s = _dot_nt(q, k) * scale
            m = jnp.max(s, axis=-1, keepdims=True)
            if has_ctx:
                sc = _dot_nt(q, kc) * scale
                m = jnp.maximum(m, jnp.max(sc, axis=-1, keepdims=True))
            p = jnp.exp(s - m)
            den = jnp.sum(p, axis=-1, keepdims=True)
            o = _dot(p.astype(BF), v)
            if has_ctx:
                pc = jnp.exp(sc - m)
                den = den + jnp.sum(pc, axis=-1, keepdims=True)
                o = o + _dot(pc.astype(BF), vc)
            o_ref[0, rows, cols] = (o / den).astype(BF)


def _attention(q, k, v, kc, vc, heads_per_step):
    b, t, _ = q.shape
    has_ctx = kc is not None
    w = heads_per_step * HP
    blk = lambda n: pl.BlockSpec((1, n, w), lambda i, j: (i, 0, j))
    in_specs = [blk(t), blk(t), blk(t)]
    args = [q, k, v]
    if has_ctx:
        in_specs += [blk(kc.shape[1])] * 2
        args += [kc, vc]
    return pl.pallas_call(
        functools.partial(_attn_kernel, t=t, n_heads=heads_per_step, has_ctx=has_ctx, q_block=min(t, 256)),
        grid=(b, NH // heads_per_step),
        in_specs=in_specs,
        out_specs=blk(t),
        out_shape=jax.ShapeDtypeStruct((b, t, DH), BF),
        compiler_params=_cparams(("arbitrary", "arbitrary")),
        name="attention",
    )(*args)


def _route(logits):
    lane = lax.broadcasted_iota(jnp.int32, logits.shape, 1)
    lanef = lane.astype(F32)
    neg = -jnp.inf
    big = float(LANES)
    gl = jnp.where((lane >= NE) & (lane < NE + NG), logits, neg)
    gmax = jnp.max(gl, axis=-1, keepdims=True)
    gidx = jnp.min(jnp.where(gl == gmax, lanef, big), axis=-1, keepdims=True) - float(NE)
    gw = 1.0 / jnp.sum(jnp.exp(gl - gmax), axis=-1, keepdims=True)
    lo = gidx * float(EPG)
    el = jnp.where((lanef >= lo) & (lanef < lo + float(EPG)), logits, neg)
    v1 = jnp.max(el, axis=-1, keepdims=True)
    i1 = jnp.min(jnp.where(el == v1, lanef, big), axis=-1, keepdims=True)
    el2 = jnp.where(lanef == i1, neg, el)
    v2 = jnp.max(el2, axis=-1, keepdims=True)
    i2 = jnp.min(jnp.where(el2 == v2, lanef, big), axis=-1, keepdims=True)
    e2 = jnp.exp(v2 - v1)
    w1 = gw / (1.0 + e2)
    w2 = gw * e2 / (1.0 + e2)
    cmb = jnp.where(lanef == i1, w1, 0.0) + jnp.where(lanef == i2, w2, 0.0)
    return cmb, gidx


def _out_kernel(x_ref, h_ref, yr_ref, ya_ref, mod_ref, wgate_ref, wor_ref, wom_ref, wout_ref,
                n2_ref, rw_ref, rb_ref, cnt0_ref, x1_ref, hx_ref, rinfo_ref, cnt_ref, run_ref):
    @pl.when((pl.program_id(0) == 0) & (pl.program_id(1) == 0))
    def _():
        run_ref[...] = cnt0_ref[...]

    gl = _dot(h_ref[0], wgate_ref[...])
    merged = (jax.nn.sigmoid(gl[:, :D]) * _dot(yr_ref[0], wor_ref[...])
              + jax.nn.sigmoid(gl[:, D:]) * _dot(ya_ref[0], wom_ref[...]))
    mix = _dot(merged.astype(BF), wout_ref[...])
    x1 = x_ref[0] + mod_ref[0, 2:3, :] * mix
    x1_ref[0] = x1
    h2 = _rms(x1, n2_ref[...], D) * (1.0 + mod_ref[0, 4:5, :]) + mod_ref[0, 3:4, :]
    logits = jnp.dot(h2, rw_ref[...], preferred_element_type=F32,
                     precision=lax.Precision.HIGHEST) + rb_ref[...]
    cmb, gidx = _route(logits)
    hx_ref[0, :, :D] = h2
    hx_ref[0, :, D:] = cmb

    lanef = lax.broadcasted_iota(jnp.int32, cmb.shape, 1).astype(F32)
    ghot = jnp.where(lanef == gidx, 1.0, 0.0)
    r_i = lax.broadcasted_iota(jnp.int32, (TM, TM), 0)
    c_i = lax.broadcasted_iota(jnp.int32, (TM, TM), 1)
    tri = jnp.where(r_i > c_i, 1.0, 0.0).astype(BF)
    before = _dot(tri, ghot.astype(BF)) + run_ref[0:1, :]
    rank = jnp.sum(before * ghot, axis=-1, keepdims=True)
    rinfo_ref[0] = jnp.where(lanef == 0.0, gidx, jnp.where(lanef == 1.0, rank, 0.0))
    run = run_ref[...] + jnp.sum(ghot, axis=0, keepdims=True)
    run_ref[...] = run
    cnt_ref[...] = run


def _merge_out(x, h, yr, ya, mod, mod_row, wts, cnt0):
    b, t, _ = x.shape
    full = lambda shape: pl.BlockSpec(shape, lambda i, j: (0,) * len(shape))
    tile = lambda w: pl.BlockSpec((1, TM, w), lambda i, j: (i, j, 0))
    return pl.pallas_call(
        _out_kernel,
        grid=(b, t // TM),
        in_specs=[tile(D), tile(D), tile(DR), tile(DH),
                  pl.BlockSpec((1, 6, D), lambda i, j: (mod_row(i), 0, 0)),
                  full((D, 2 * D)), full((DR, D)), full((DH, D)), full((D, D)),
                  full((1, D)), full((D, LANES)), full((1, LANES)), full((SUBLANES, LANES))],
        out_specs=[tile(D), tile(XW), tile(LANES), full((SUBLANES, LANES))],
        out_shape=[jax.ShapeDtypeStruct((b, t, D), F32), jax.ShapeDtypeStruct((b, t, XW), F32),
                   jax.ShapeDtypeStruct((b, t, LANES), F32),
                   jax.ShapeDtypeStruct((SUBLANES, LANES), F32)],
        scratch_shapes=[pltpu.VMEM((SUBLANES, LANES), F32)],
        compiler_params=_cparams(("arbitrary", "arbitrary")),
        name="merge_out",
    )(x, h, yr, ya, mod, wts["w_gate"], wts["w_o_rnn"], wts["w_o_mla"], wts["w_out"],
      wts["n2"], wts["router_w"], wts["router_b"], cnt0)


def _row_copy(src, src_row, dst, dst_row, sem):
    return pltpu.make_async_copy(src.at[pl.ds(src_row, 1)], dst.at[pl.ds(dst_row, 1)], sem)


def _dispatch_kernel(pos_ref, hx_ref, xs_in_ref, xs_ref, sem):
    del xs_in_ref
    base = pl.program_id(0) * TD

    def body(r, carry):
        _row_copy(hx_ref, r, xs_ref, pos_ref[base + r], sem).start()
        return carry

    lax.fori_loop(0, TD, body, 0, unroll=8)
    pltpu.make_async_copy(hx_ref, xs_ref.at[pl.ds(0, TD)], sem).wait()


def _dispatch(pos, hx, xs):
    n = hx.shape[0]
    return pl.pallas_call(
        _dispatch_kernel,
        grid_spec=pltpu.PrefetchScalarGridSpec(
            num_scalar_prefetch=1,
            grid=(n // TD,),
            in_specs=[pl.BlockSpec((TD, XW), lambda i, pos: (i, 0)),
                      pl.BlockSpec(memory_space=pl.ANY)],
            out_specs=pl.BlockSpec(memory_space=pl.ANY),
            scratch_shapes=[pltpu.SemaphoreType.DMA(())],
        ),
        out_shape=jax.ShapeDtypeStruct(xs.shape, xs.dtype),
        input_output_aliases={2: 0},
        compiler_params=_cparams(("arbitrary",)),
        name="dispatch",
    )(pos, hx, xs)


def _moe_kernel(tb_ref, tg_ref, nt_ref, xs_ref, w1_ref, w3_ref, w2_ref, o_ref, acc_ref):
    j = pl.program_id(0)
    e = pl.program_id(1)

    @pl.when(j < nt_ref[0])
    def _():
        @pl.when(e == 0)
        def _():
            acc_ref[...] = jnp.zeros_like(acc_ref)

        xt = xs_ref[:, :D].astype(BF)
        a = _dot(xt, w1_ref[0])
        he = (a * jax.nn.sigmoid(a)) * _dot(xt, w3_ref[0])
        y = _dot(he.astype(BF), w2_ref[0])
        cmb = xs_ref[:, D:]
        lane = lax.broadcasted_iota(jnp.int32, cmb.shape, 1)
        ce = jnp.sum(jnp.where(lane == tg_ref[j] * EPG + e, cmb, 0.0), axis=-1, keepdims=True)
        acc_ref[...] += ce * y

        @pl.when(e == EPG - 1)
        def _():
            o_ref[...] = acc_ref[...]

    @pl.when((j >= nt_ref[0]) & (e == EPG - 1))
    def _():
        o_ref[...] = jnp.zeros_like(o_ref)


def _experts(tile_block, tile_group, n_tiles, xs, wts):
    m = xs.shape[0]

    def w_idx(j, e, tb, tg, nt):
        return (tg[j] * EPG + jnp.where(j < nt[0], e, EPG - 1), 0, 0)

    return pl.pallas_call(
        _moe_kernel,
        grid_spec=pltpu.PrefetchScalarGridSpec(
            num_scalar_prefetch=3,
            grid=(m // TMOE, EPG),
            in_specs=[pl.BlockSpec((TMOE, XW), lambda j, e, tb, tg, nt: (tb[j], 0)),
                      pl.BlockSpec((1, D, DE), w_idx),
                      pl.BlockSpec((1, D, DE), w_idx),
                      pl.BlockSpec((1, DE, D), w_idx)],
            out_specs=pl.BlockSpec((TMOE, D), lambda j, e, tb, tg, nt: (j, 0)),
            scratch_shapes=[pltpu.VMEM((TMOE, D), F32)],
        ),
        out_shape=jax.ShapeDtypeStruct((m, D), F32),
        compiler_params=_cparams(("arbitrary", "arbitrary")),
        name="experts",
    )(tile_block, tile_group, n_tiles, xs, wts["exp_w1"], wts["exp_w3"], wts["exp_w2"])


def _combine_kernel(pos_ref, x1_ref, mod_ref, ys_ref, o_ref, buf_ref, sem, *, n_steps):
    i = pl.program_id(0)

    def issue(step, slot):
        def body(r, carry):
            _row_copy(ys_ref, pos_ref[step * TD + r], buf_ref.at[slot], r, sem.at[slot]).start()
            return carry

        lax.fori_loop(0, TD, body, 0, unroll=8)

    @pl.when(i == 0)
    def _():
        issue(0, 0)

    @pl.when(i + 1 < n_steps)
    def _():
        issue(i + 1, (i + 1) % 2)

    slot = i % 2
    pltpu.make_async_copy(ys_ref.at[pl.ds(0, TD)], buf_ref.at[slot], sem.at[slot]).wait()
    o_ref[...] = x1_ref[...] + mod_ref[0, 5:6, :] * buf_ref[slot]


def _combine(pos, x1, mod, mod_row, ys):
    n = x1.shape[0]
    n_steps = n // TD
    return pl.pallas_call(
        functools.partial(_combine_kernel, n_steps=n_steps),
        grid_spec=pltpu.PrefetchScalarGridSpec(
            num_scalar_prefetch=1,
            grid=(n_steps,),
            in_specs=[pl.BlockSpec((TD, D), lambda i, pos: (i, 0)),
                      pl.BlockSpec((1, 6, D), lambda i, pos: (mod_row(i), 0, 0)),
                      pl.BlockSpec(memory_space=pl.ANY)],
            out_specs=pl.BlockSpec((TD, D), lambda i, pos: (i, 0)),
            scratch_shapes=[pltpu.VMEM((2, TD, D), F32), pltpu.SemaphoreType.DMA((2,))],
        ),
        out_shape=jax.ShapeDtypeStruct((n, D), F32),
        compiler_params=_cparams(("arbitrary",)),
        name="combine",
    )(pos, x1, mod, ys)


def _group_layout(counts, max_tiles):
    padded = ((counts + TMOE - 1) // TMOE) * TMOE
    ends = jnp.cumsum(padded)
    offsets = ends - padded
    n_tiles = (ends[-1] // TMOE).astype(jnp.int32)
    tile = jnp.minimum(jnp.arange(max_tiles, dtype=jnp.int32), n_tiles - 1)
    tile_group = jnp.sum((tile[:, None] * TMOE >= ends[None, :]).astype(jnp.int32), axis=1)
    return offsets, tile, tile_group, n_tiles.reshape(1)


def _pad_heads(w, widths, perm):
    lead = w.shape[:-1]
    per = w.shape[-1] // NH
    w = w.reshape(lead + (NH, per))
    if per == QK:
        w = jnp.concatenate([w[..., :NOPE], w[..., NOPE:][..., perm]], axis=-1)
    w = jnp.pad(w, [(0, 0)] * len(lead) + [(0, 0), (0, HP - per)])
    return w.reshape(lead + (NH * HP,))


def _pad_gain(g, perm):
    g = jnp.concatenate([g[:NOPE], g[NOPE:][perm], jnp.zeros((HP - QK,), F32)])
    return g.reshape(1, HP)


def _block_diag(w):
    per = CH // LRU_BLOCK
    w = w.reshape(DR // CH, per, LRU_BLOCK, LRU_BLOCK)
    bd = jnp.einsum("jpab,pq->jpaqb", w, jnp.eye(per, dtype=w.dtype))
    return bd.reshape(DR // CH, CH, CH)


def _prepare_shared(l, p):
    w_in = p["w_in"][l]
    o1, o2, o3, o4, o5 = DR, 2 * DR, 2 * DR + QL, 2 * DR + QL + KVL, 2 * DR + QL + KVL + ROPE
    bd = jnp.stack([_block_diag(p["lru_wa"][l, 0]), _block_diag(p["lru_wx"][l, 0]),
                    _block_diag(p["lru_wa"][l, 1]), _block_diag(p["lru_wx"][l, 1])]).astype(BF)
    wom = p["w_o_mla"][l].reshape(NH, VD, D)
    wom = jnp.pad(wom, ((0, 0), (0, HP - VD), (0, 0))).reshape(DH, D)
    router_w = jnp.concatenate([p["router_we"][l], p["router_wg"][l],
                                jnp.zeros((D, LANES - NE - NG), F32)], axis=1)
    router_b = jnp.concatenate([p["router_be"][l], p["router_bg"][l],
                                jnp.zeros((LANES - NE - NG,), F32)]).reshape(1, LANES)
    return {
        "n1": p["norm1_g"][l].reshape(1, D), "n2": p["norm2_g"][l].reshape(1, D),
        "w_x": w_in[:, :o1].astype(BF), "w_g": w_in[:, o1:o2].astype(BF),
        "w_q": w_in[:, o2:o3].astype(BF), "w_kv": w_in[:, o3:o4], "w_kr": w_in[:, o4:o5],
        "w_gate": w_in[:, o5:].astype(BF),
        "qan": p["q_a_norm"][l].reshape(1, QL), "kvan": p["kv_a_norm"][l].reshape(1, KVL),
        "w_uk": _pad_heads(p["w_uk"][l], None, None).astype(BF),
        "w_uv": _pad_heads(p["w_uv"][l], None, None).astype(BF),
        "conv_w": p["conv_w"][l], "conv_b": p["conv_b"][l].reshape(1, DR), "bd": bd,
        "lru_ba": p["lru_ba"][l], "lru_bx": p["lru_bx"][l], "lru_lam": p["lru_lam"][l],
        "w_o_rnn": p["w_o_rnn"][l].astype(BF), "w_o_mla": wom.astype(BF), "w_out": p["w_out"][l].astype(BF),
        "router_w": router_w, "router_b": router_b,
        "exp_w1": p["exp_w1"][l].astype(BF), "exp_w3": p["exp_w3"][l].astype(BF),
        "exp_w2": p["exp_w2"][l].astype(BF),
    }


def _with_rope_order(l, p, shared, perm):
    w = dict(shared)
    zeros = lambda n: jnp.zeros((D, n), F32)
    w["w_kvr"] = jnp.concatenate([shared["w_kv"], zeros(NOPE), shared["w_kr"][:, perm],
                                  zeros(HP - QK)], axis=1).astype(BF)
    w["w_uq"] = _pad_heads(p["w_uq"][l], None, perm).astype(BF)
    w["gq"] = _pad_gain(p["q_norm"][l], perm)
    w["gk"] = _pad_gain(p["k_norm"][l], perm)
    return w


def _rope_tables(n_tokens, perm):
    rows = n_tokens // GRID_W
    row = jnp.repeat(jnp.arange(rows), GRID_W).astype(F32)
    col = jnp.tile(jnp.arange(GRID_W), rows).astype(F32)
    axis_dim = ROPE // 2
    inv = ROPE_BASE ** (-jnp.arange(0, axis_dim, 2, dtype=F32) / axis_dim)
    ang = jnp.concatenate([row[:, None] * inv, col[:, None] * inv], axis=-1)
    cos, sin = jnp.cos(ang), jnp.sin(ang)
    ones = lambda n: jnp.ones((n_tokens, n), F32)
    zeros = lambda n: jnp.zeros((n_tokens, n), F32)
    cos_t = jnp.concatenate([ones(NOPE), cos, cos, ones(HP - QK)], axis=1)
    sin_t = jnp.concatenate([zeros(NOPE), -sin, sin, zeros(HP - QK)], axis=1)
    return cos_t, sin_t


def kernel(x_prompt, x_sample, cache_mla_ckv, cache_mla_krope, state_rglru, c, c_ctx, norm1_g, norm2_g, w_mod, b_mod, w_in, conv_w, conv_b, lru_wa, lru_ba, lru_wx, lru_bx, lru_lam, q_a_norm, kv_a_norm, w_uq, w_uk, w_uv, q_norm, k_norm, w_o_rnn, w_o_mla, w_out, router_wg, router_bg, router_we, router_be, exp_w1, exp_w3, exp_w2):
    p = dict(norm1_g=norm1_g, norm2_g=norm2_g, w_in=w_in, conv_w=conv_w, conv_b=conv_b,
             lru_wa=lru_wa, lru_ba=lru_ba, lru_wx=lru_wx, lru_bx=lru_bx, lru_lam=lru_lam,
             q_a_norm=q_a_norm, kv_a_norm=kv_a_norm, w_uq=w_uq, w_uk=w_uk, w_uv=w_uv,
             q_norm=q_norm, k_norm=k_norm, w_o_rnn=w_o_rnn, w_o_mla=w_o_mla, w_out=w_out,
             router_wg=router_wg, router_bg=router_bg, router_we=router_we, router_be=router_be,
             exp_w1=exp_w1, exp_w3=exp_w3, exp_w2=exp_w2)
    depth = w_in.shape[0]
    nb, seq, _ = x_prompt.shape
    db, dseq, _ = x_sample.shape
    ident = np.arange(ROPE)
    halves = np.concatenate([np.arange(0, ROPE, 2), np.arange(1, ROPE, 2)])
    rope_tabs = _rope_tables(dseq, halves)
    cond8 = jnp.concatenate([c_ctx[None, :], c, jnp.zeros((SUBLANES - 1 - db, D), F32)], axis=0)
    ctx_row = lambda i: 0
    lat_row = lambda i: i + 1
    ctx_tok_row = lambda i: 0
    lat_tok_row = lambda i: (i * TD) // dseq + 1
    n_ctx, n_lat = nb * seq, db * dseq
    max_tiles = (n_ctx + n_lat) // TMOE + NG

    y_prompt, y_sample = x_prompt, x_sample
    ckv_list, krope_list, rnn_list = [], [], []
    for l in range(depth):
        shared = _prepare_shared(l, p)
        w_ctx = _with_rope_order(l, p, shared, ident)
        w_lat = _with_rope_order(l, p, shared, halves)
        mod = _modulation(cond8, w_mod[l], b_mod[l]).reshape(SUBLANES, 6, D)

        h, xr, gg, q, k, v, ckv, kro = _projections(y_prompt, mod, ctx_row, w_ctx, None, True)
        yr, h_fin = _rglru(xr, gg, shared, None, True)
        ya = _attention(q, k, v, None, None, NH)
        x1_c, hx_c, ri_c, cnt_c = _merge_out(y_prompt, h, yr, ya, mod, ctx_row, shared,
                                             jnp.zeros((SUBLANES, LANES), F32))
        ckv_list.append(ckv)
        krope_list.append(kro)
        rnn_list.append(h_fin)

        krp_cache = jnp.pad(cache_mla_krope[:, l][..., halves], ((0, 0), (0, 0), (NOPE, HP - QK)))
        kc, vc = _cache_keys_values(cache_mla_ckv[:, l], krp_cache, w_lat)
        h, xr, gg, q, k, v = _projections(y_sample, mod, lat_row, w_lat, rope_tabs, False)
        yr, _ = _rglru(xr, gg, shared, state_rglru[:, l], False)
        ya = _attention(q, k, v, kc, vc, 1)
        x1_l, hx_l, ri_l, cnt_l = _merge_out(y_sample, h, yr, ya, mod, lat_row, shared, cnt_c)

        offsets, tile_block, tile_group, n_tiles = _group_layout(cnt_l[0, :NG].astype(jnp.int32), max_tiles)
        slot = lambda ri: (offsets[ri[..., 0].astype(jnp.int32)] + ri[..., 1].astype(jnp.int32)).reshape(-1)
        pos_c, pos_l = slot(ri_c), slot(ri_l)
        xs = jnp.zeros((max_tiles * TMOE, XW), F32)
        xs = _dispatch(pos_c, hx_c.reshape(n_ctx, XW), xs)
        xs = _dispatch(pos_l, hx_l.reshape(n_lat, XW), xs)
        ys = _experts(tile_block, tile_group, n_tiles, xs, shared)
        y_prompt = _combine(pos_c, x1_c.reshape(n_ctx, D), mod, ctx_tok_row, ys).reshape(nb, seq, D)
        y_sample = _combine(pos_l, x1_l.reshape(n_lat, D), mod, lat_tok_row, ys).reshape(db, dseq, D)

    return (y_prompt, y_sample, jnp.stack(ckv_list, axis=1), jnp.stack(krope_list, axis=1),
            jnp.stack(rnn_list, axis=1))
```

```python
import functools
import math

import numpy as np
import jax
import jax.numpy as jnp
from jax import lax
from jax.experimental import pallas as pl
from jax.experimental.pallas import tpu as pltpu

D = 1024
DR = 1024
QL = 384
KVL = 256
NH = 8
NOPE = 64
ROPE = 32
QK = NOPE + ROPE
VD = 64
HP = 128
DH = NH * HP
GRID_W = 64
ROPE_BASE = 10000.0
EPS = 1e-6
LRU_C = 8.0
LRU_BLOCK = 64
CH = 256
NLT = CH // 128
NG = 4
EPG = 4
NE = NG * EPG
DE = 512
LANES = 128
SUBLANES = 8
TM = 256
TMOE = 512
TD = 256
XW = D + 128
VMEM_LIMIT = 52 * 1024 * 1024
BF = jnp.bfloat16
F32 = jnp.float32


def _cparams(sem):
    return pltpu.CompilerParams(dimension_semantics=sem, vmem_limit_bytes=VMEM_LIMIT)


def _dot(a, b):
    return jnp.dot(a, b, preferred_element_type=F32)


def _dot_nt(a, b):
    return lax.dot_general(a, b, (((1,), (1,)), ((), ())), preferred_element_type=F32)


def _rms(x, g, width):
    ms = jnp.sum(x * x, axis=-1, keepdims=True) * (1.0 / width)
    return x * lax.rsqrt(ms + EPS) * g


def _mod_kernel(c_ref, w_ref, b_ref, o_ref):
    c = c_ref[...]
    s = c * jax.nn.sigmoid(c)
    o_ref[...] = _dot(s, w_ref[...]) + b_ref[...]


def _modulation(cond8, w_mod, b_mod):
    n = w_mod.shape[1]
    return pl.pallas_call(
        _mod_kernel,
        grid=(n // D,),
        in_specs=[
            pl.BlockSpec((SUBLANES, D), lambda j: (0, 0)),
            pl.BlockSpec((D, D), lambda j: (0, j)),
            pl.BlockSpec((1, D), lambda j: (0, j)),
        ],
        out_specs=pl.BlockSpec((SUBLANES, D), lambda j: (0, j)),
        out_shape=jax.ShapeDtypeStruct((SUBLANES, n), F32),
        compiler_params=_cparams(("arbitrary",)),
        name="modulation",
    )(cond8, w_mod, b_mod.reshape(1, n))


def _head_norm(xh, gain, cos, sins):
    ms = jnp.sum(xh * xh, axis=-1, keepdims=True) * (1.0 / QK)
    y = xh * lax.rsqrt(ms + EPS) * gain
    if cos is None:
        return y
    lane = lax.broadcasted_iota(jnp.int32, y.shape, 1)
    first = (lane >= NOPE) & (lane < NOPE + ROPE // 2)
    partner = jnp.where(first, pltpu.roll(y, HP - ROPE // 2, 1), pltpu.roll(y, ROPE // 2, 1))
    return y * cos + partner * sins


def _keys_values(ckv, krp, wuk_ref, wuv_ref, gk, cos, sins, k_ref, v_ref):
    cb = ckv.astype(BF)
    kn = _dot(cb, wuk_ref[...])
    v_ref[0] = _dot(cb, wuv_ref[...]).astype(BF)
    for h in range(NH):
        kh = kn[:, h * HP:(h + 1) * HP] + krp
        k_ref[0, :, h * HP:(h + 1) * HP] = _head_norm(kh, gk, cos, sins).astype(BF)


def _proj_kernel(*refs, rope, emit_cache):
    it = iter(refs)
    x_ref, mod_ref, n1_ref = next(it), next(it), next(it)
    wx_ref, wg_ref, wq_ref, wkvr_ref = next(it), next(it), next(it), next(it)
    qan_ref, kvan_ref, wuq_ref, gq_ref = next(it), next(it), next(it), next(it)
    wuk_ref, gk_ref, wuv_ref = next(it), next(it), next(it)
    cos = sins = None
    if rope:
        cos, sins = next(it)[...], next(it)[...]
    h_ref, xr_ref, gg_ref, q_ref, k_ref, v_ref = (next(it) for _ in range(6))
    if emit_cache:
        ckv_ref, kro_ref = next(it), next(it)

    x = x_ref[0]
    sh1 = mod_ref[0, 0:1, :]
    sc1 = mod_ref[0, 1:2, :]
    h = _rms(x, n1_ref[...], D) * (1.0 + sc1) + sh1
    hb = h.astype(BF)
    h_ref[0] = hb
    xr_ref[0] = _dot(hb, wx_ref[...])
    gg_ref[0] = jax.nn.gelu(_dot(hb, wg_ref[...])).astype(BF)

    qn = _rms(_dot(hb, wq_ref[...]), qan_ref[...], QL)
    q = _dot(qn.astype(BF), wuq_ref[...])
    gq = gq_ref[...]
    for hd in range(NH):
        qh = q[:, hd * HP:(hd + 1) * HP]
        q_ref[0, :, hd * HP:(hd + 1) * HP] = _head_norm(qh, gq, cos, sins).astype(BF)

    kvr = _dot(hb, wkvr_ref[...])
    ckv = _rms(kvr[:, :KVL], kvan_ref[...], KVL)
    krp = kvr[:, KVL:KVL + HP]
    if emit_cache:
        ckv_ref[0] = ckv
        kro_ref[0] = krp[:, NOPE:NOPE + ROPE]
    _keys_values(ckv, krp, wuk_ref, wuv_ref, gk_ref[...], cos, sins, k_ref, v_ref)


def _projections(x, mod, mod_row, wts, rope_tabs, emit_cache):
    b, t, _ = x.shape
    nt = t // TM
    rope = rope_tabs is not None
    full = lambda shape: pl.BlockSpec(shape, lambda i, j: (0,) * len(shape))
    tile = lambda w: pl.BlockSpec((1, TM, w), lambda i, j: (i, j, 0))
    in_specs = [
        tile(D),
        pl.BlockSpec((1, 6, D), lambda i, j: (mod_row(i), 0, 0)),
        full((1, D)),
        full((D, DR)), full((D, DR)), full((D, QL)), full((D, KVL + HP)),
        full((1, QL)), full((1, KVL)), full((QL, DH)), full((1, HP)),
        full((KVL, DH)), full((1, HP)), full((KVL, DH)),
    ]
    args = [x, mod, wts["n1"], wts["w_x"], wts["w_g"], wts["w_q"], wts["w_kvr"],
            wts["qan"], wts["kvan"], wts["w_uq"], wts["gq"], wts["w_uk"], wts["gk"], wts["w_uv"]]
    if rope:
        in_specs += [pl.BlockSpec((TM, HP), lambda i, j: (j, 0))] * 2
        args += list(rope_tabs)
    out_specs = [tile(D), tile(DR), tile(DR), tile(DH), tile(DH), tile(DH)]
    out_shape = [jax.ShapeDtypeStruct((b, t, D), BF), jax.ShapeDtypeStruct((b, t, DR), F32),
                 jax.ShapeDtypeStruct((b, t, DR), BF), jax.ShapeDtypeStruct((b, t, DH), BF),
                 jax.ShapeDtypeStruct((b, t, DH), BF), jax.ShapeDtypeStruct((b, t, DH), BF)]
    if emit_cache:
        out_specs += [tile(KVL), tile(ROPE)]
        out_shape += [jax.ShapeDtypeStruct((b, t, KVL), F32), jax.ShapeDtypeStruct((b, t, ROPE), F32)]
    return pl.pallas_call(
        functools.partial(_proj_kernel, rope=rope, emit_cache=emit_cache),
        grid=(b, nt),
        in_specs=in_specs,
        out_specs=out_specs,
        out_shape=out_shape,
        compiler_params=_cparams(("arbitrary", "arbitrary")),
        name="projections",
    )(*args)


def _cache_kv_kernel(ckv_ref, krp_ref, wuk_ref, gk_ref, wuv_ref, k_ref, v_ref):
    _keys_values(ckv_ref[0], krp_ref[0], wuk_ref, wuv_ref, gk_ref[...], None, None, k_ref, v_ref)


def _cache_keys_values(ckv, krp, wts):
    b, s, _ = ckv.shape
    full = lambda shape: pl.BlockSpec(shape, lambda i: (0,) * len(shape))
    return pl.pallas_call(
        _cache_kv_kernel,
        grid=(b,),
        in_specs=[pl.BlockSpec((1, s, KVL), lambda i: (i, 0, 0)),
                  pl.BlockSpec((1, s, HP), lambda i: (i, 0, 0)),
                  full((KVL, DH)), full((1, HP)), full((KVL, DH))],
        out_specs=[pl.BlockSpec((1, s, DH), lambda i: (i, 0, 0))] * 2,
        out_shape=[jax.ShapeDtypeStruct((b, s, DH), BF)] * 2,
        compiler_params=_cparams(("arbitrary",)),
        name="cache_keys_values",
    )(ckv, krp, wts["w_uk"], wts["gk"], wts["w_uv"])


def _sigmoid(x):
    return 0.5 * jnp.tanh(0.5 * x) + 0.5


def _tile_scan(a, b, forward):
    row = lax.broadcasted_iota(jnp.int32, a.shape, 0)
    for k in (1, 2, 4):
        if forward:
            shift, valid = k, row >= k
        else:
            shift, valid = SUBLANES - k, row < SUBLANES - k
        a_prev = jnp.where(valid, pltpu.roll(a, shift, 0), 1.0)
        b_prev = jnp.where(valid, pltpu.roll(b, shift, 0), 0.0)
        b = a * b_prev + b
        a = a * a_prev
    return a, b


def _rglru_kernel(*refs, t, has_h0, emit_state):
    it = iter(refs)
    xr_ref, gg_ref, cw_ref, cb_ref, bd_ref, ba_ref, bx_ref, lam_ref = (next(it) for _ in range(8))
    h0_ref = next(it) if has_h0 else None
    y_ref = next(it)
    hf_ref = next(it) if emit_state else None
    a_scr, b_scr, h_scr = (next(it) for _ in range(3))

    n_tiles = t // SUBLANES
    x = xr_ref[0]
    row = lax.broadcasted_iota(jnp.int32, x.shape, 0)
    xm1 = jnp.where(row >= 1, pltpu.roll(x, 1, 0), 0.0)
    xp1 = jnp.where(row < t - 1, pltpu.roll(x, t - 1, 0), 0.0)
    xp2 = jnp.where(row < t - 2, pltpu.roll(x, t - 2, 0), 0.0)
    xc = (cb_ref[...] + xm1 * cw_ref[0:1, :] + x * cw_ref[1:2, :]
          + xp1 * cw_ref[2:3, :] + xp2 * cw_ref[3:4, :])
    xcb = xc.astype(BF)
    for d in range(2):
        r = _sigmoid(_dot(xcb, bd_ref[2 * d, 0]) + ba_ref[d:d + 1, :])
        gi = _sigmoid(_dot(xcb, bd_ref[2 * d + 1, 0]) + bx_ref[d:d + 1, :])
        nl = -lam_ref[d:d + 1, :]
        softplus = jnp.maximum(nl, 0.0) + jnp.log(1.0 + jnp.exp(-jnp.abs(nl)))
        a = jnp.exp(r * ((-LRU_C) * softplus))
        a_scr[d] = a
        b_scr[d] = jnp.sqrt(1.0 - a * a) * (gi * xc)

    def step(i, carry):
        out = []
        for d in range(2):
            tile = i if d == 0 else n_tiles - 1 - i
            rows = pl.ds(pl.multiple_of(tile * SUBLANES, SUBLANES), SUBLANES)
            for c in range(NLT):
                lanes = slice(c * LANES, (c + 1) * LANES)
                decay, local = _tile_scan(a_scr[d, rows, lanes], b_scr[d, rows, lanes], d == 0)
                h = local + decay * carry[d * NLT + c]
                h_scr[d, rows, lanes] = h
                last = h[SUBLANES - 1:SUBLANES, :] if d == 0 else h[0:1, :]
                out.append(jnp.broadcast_to(last, (SUBLANES, LANES)))
        return tuple(out)

    init = []
    for d in range(2):
        for c in range(NLT):
            if has_h0:
                h0 = h0_ref[0, d:d + 1, c * LANES:(c + 1) * LANES]
                init.append(jnp.broadcast_to(h0, (SUBLANES, LANES)))
            else:
                init.append(jnp.zeros((SUBLANES, LANES), F32))
    final = lax.fori_loop(0, n_tiles, step, tuple(init), unroll=2)

    if emit_state:
        for d in range(2):
            for c in range(NLT):
                hf_ref[0, d:d + 1, c * LANES:(c + 1) * LANES] = final[d * NLT + c][0:1, :]
    y_ref[0] = ((h_scr[0] + h_scr[1]) * gg_ref[0].astype(F32)).astype(BF)


def _rglru(xr, gg, wts, h0, emit_state):
    b, t, _ = xr.shape
    nc = DR // CH
    has_h0 = h0 is not None
    chunk = lambda r: pl.BlockSpec((r, CH), lambda i, j: (0, j))
    seq = pl.BlockSpec((1, t, CH), lambda i, j: (i, 0, j))
    state = pl.BlockSpec((1, 2, CH), lambda i, j: (i, 0, j))
    in_specs = [seq, seq, chunk(4), chunk(1),
                pl.BlockSpec((4, 1, CH, CH), lambda i, j: (0, j, 0, 0)),
                chunk(2), chunk(2), chunk(2)]
    args = [xr, gg, wts["conv_w"], wts["conv_b"], wts["bd"], wts["lru_ba"], wts["lru_bx"], wts["lru_lam"]]
    if has_h0:
        in_specs.append(state)
        args.append(h0)
    out_specs = [seq]
    out_shape = [jax.ShapeDtypeStruct((b, t, DR), BF)]
    if emit_state:
        out_specs.append(state)
        out_shape.append(jax.ShapeDtypeStruct((b, 2, DR), F32))
    res = pl.pallas_call(
        functools.partial(_rglru_kernel, t=t, has_h0=has_h0, emit_state=emit_state),
        grid=(b, nc),
        in_specs=in_specs,
        out_specs=out_specs,
        out_shape=out_shape,
        scratch_shapes=[pltpu.VMEM((2, t, CH), F32)] * 3,
        compiler_params=_cparams(("arbitrary", "arbitrary")),
        name="rglru",
    )(*args)
    return res if emit_state else (res[0], None)


def _attn_kernel(*refs, t, n_heads, has_ctx, q_block):
    it = iter(refs)
    q_ref, k_ref, v_ref = next(it), next(it), next(it)
    kc_ref = vc_ref = None
    if has_ctx:
        kc_ref, vc_ref = next(it), next(it)
    o_ref = next(it)
    scale = QK ** -0.5
    for hd in range(n_heads):
        cols = slice(hd * HP, (hd + 1) * HP)
        k = k_ref[0, :, cols]
        v = v_ref[0, :, cols]
        if has_ctx:
            kc = kc_ref[0, :, cols]
            vc = vc_ref[0, :, cols]
        for qb in range(t // q_block):
            rows = slice(qb * q_block, (qb + 1) * q_block)
            q = q_ref[0, rows, cols]
            s = _dot_nt(q, k) * scale
            m = jnp.max(s, axis=-1, keepdims=True)
            if has_ctx:
                sc = _dot_nt(q, kc) * scale
                m = jnp.maximum(m, jnp.max(sc, axis=-1, keepdims=True))
            p = jnp.exp(s - m)
            den = jnp.sum(p, axis=-1, keepdims=True)
            o = _dot(p.astype(BF), v)
            if has_ctx:
                pc = jnp.exp(sc - m)
                den = den + jnp.sum(pc, axis=-1, keepdims=True)
                o = o + _dot(pc.astype(BF), vc)
            o_ref[0, rows, cols] = (o / den).astype(BF)


def _attention(q, k, v, kc, vc, heads_per_step):
    b, t, _ = q.shape
    has_ctx = kc is not None
    w = heads_per_step * HP
    blk = lambda n: pl.BlockSpec((1, n, w), lambda i, j: (i, 0, j))
    in_specs = [blk(t), blk(t), blk(t)]
    args = [q, k, v]
    if has_ctx:
        in_specs += [blk(kc.shape[1])] * 2
        args += [kc, vc]
    return pl.pallas_call(
        functools.partial(_attn_kernel, t=t, n_heads=heads_per_step, has_ctx=has_ctx, q_block=min(t, 256)),
        grid=(b, NH // heads_per_step),
        in_specs=in_specs,
        out_specs=blk(t),
        out_shape=jax.ShapeDtypeStruct((b, t, DH), BF),
        compiler_params=_cparams(("arbitrary", "arbitrary")),
        name="attention",
    )(*args)


def _route(logits):
    lane = lax.broadcasted_iota(jnp.int32, logits.shape, 1)
    lanef = lane.astype(F32)
    neg = -jnp.inf
    big = float(LANES)
    gl = jnp.where((lane >= NE) & (lane < NE + NG), logits, neg)
    gmax = jnp.max(gl, axis=-1, keepdims=True)
    gidx = jnp.min(jnp.where(gl == gmax, lanef, big), axis=-1, keepdims=True) - float(NE)
    gw = 1.0 / jnp.sum(jnp.exp(gl - gmax), axis=-1, keepdims=True)
    lo = gidx * float(EPG)
    el = jnp.where((lanef >= lo) & (lanef < lo + float(EPG)), logits, neg)
    v1 = jnp.max(el, axis=-1, keepdims=True)
    i1 = jnp.min(jnp.where(el == v1, lanef, big), axis=-1, keepdims=True)
    el2 = jnp.where(lanef == i1, neg, el)
    v2 = jnp.max(el2, axis=-1, keepdims=True)
    i2 = jnp.min(jnp.where(el2 == v2, lanef, big), axis=-1, keepdims=True)
    e2 = jnp.exp(v2 - v1)
    w1 = gw / (1.0 + e2)
    w2 = gw * e2 / (1.0 + e2)
    cmb = jnp.where(lanef == i1, w1, 0.0) + jnp.where(lanef == i2, w2, 0.0)
    return cmb, gidx


def _out_kernel(x_ref, h_ref, yr_ref, ya_ref, mod_ref, wgate_ref, wor_ref, wom_ref, wout_ref,
                n2_ref, rw_ref, rb_ref, cnt0_ref, x1_ref, hx_ref, rinfo_ref, cnt_ref, run_ref):
    @pl.when((pl.program_id(0) == 0) & (pl.program_id(1) == 0))
    def _():
        run_ref[...] = cnt0_ref[...]

    gl = _dot(h_ref[0], wgate_ref[...])
    merged = (jax.nn.sigmoid(gl[:, :D]) * _dot(yr_ref[0], wor_ref[...])
              + jax.nn.sigmoid(gl[:, D:]) * _dot(ya_ref[0], wom_ref[...]))
    mix = _dot(merged.astype(BF), wout_ref[...])
    x1 = x_ref[0] + mod_ref[0, 2:3, :] * mix
    x1_ref[0] = x1
    h2 = _rms(x1, n2_ref[...], D) * (1.0 + mod_ref[0, 4:5, :]) + mod_ref[0, 3:4, :]
    logits = jnp.dot(h2, rw_ref[...], preferred_element_type=F32,
                     precision=lax.Precision.HIGHEST) + rb_ref[...]
    cmb, gidx = _route(logits)
    hx_ref[0, :, :D] = h2
    hx_ref[0, :, D:] = cmb

    lanef = lax.broadcasted_iota(jnp.int32, cmb.shape, 1).astype(F32)
    ghot = jnp.where(lanef == gidx, 1.0, 0.0)
    r_i = lax.broadcasted_iota(jnp.int32, (TM, TM), 0)
    c_i = lax.broadcasted_iota(jnp.int32, (TM, TM), 1)
    tri = jnp.where(r_i > c_i, 1.0, 0.0).astype(BF)
    before = _dot(tri, ghot.astype(BF)) + run_ref[0:1, :]
    rank = jnp.sum(before * ghot, axis=-1, keepdims=True)
    rinfo_ref[0] = jnp.where(lanef == 0.0, gidx, jnp.where(lanef == 1.0, rank, 0.0))
    run = run_ref[...] + jnp.sum(ghot, axis=0, keepdims=True)
    run_ref[...] = run
    cnt_ref[...] = run


def _merge_out(x, h, yr, ya, mod, mod_row, wts, cnt0):
    b, t, _ = x.shape
    full = lambda shape: pl.BlockSpec(shape, lambda i, j: (0,) * len(shape))
    tile = lambda w: pl.BlockSpec((1, TM, w), lambda i, j: (i, j, 0))
    return pl.pallas_call(
        _out_kernel,
        grid=(b, t // TM),
        in_specs=[tile(D), tile(D), tile(DR), tile(DH),
                  pl.BlockSpec((1, 6, D), lambda i, j: (mod_row(i), 0, 0)),
                  full((D, 2 * D)), full((DR, D)), full((DH, D)), full((D, D)),
                  full((1, D)), full((D, LANES)), full((1, LANES)), full((SUBLANES, LANES))],
        out_specs=[tile(D), tile(XW), tile(LANES), full((SUBLANES, LANES))],
        out_shape=[jax.ShapeDtypeStruct((b, t, D), F32), jax.ShapeDtypeStruct((b, t, XW), F32),
                   jax.ShapeDtypeStruct((b, t, LANES), F32),
                   jax.ShapeDtypeStruct((SUBLANES, LANES), F32)],
        scratch_shapes=[pltpu.VMEM((SUBLANES, LANES), F32)],
        compiler_params=_cparams(("arbitrary", "arbitrary")),
        name="merge_out",
    )(x, h, yr, ya, mod, wts["w_gate"], wts["w_o_rnn"], wts["w_o_mla"], wts["w_out"],
      wts["n2"], wts["router_w"], wts["router_b"], cnt0)


def _row_copy(src, src_row, dst, dst_row, sem):
    return pltpu.make_async_copy(src.at[pl.ds(src_row, 1)], dst.at[pl.ds(dst_row, 1)], sem)


def _dispatch_kernel(pos_ref, hx_ref, xs_in_ref, xs_ref, sem):
    del xs_in_ref
    base = pl.program_id(0) * TD

    def body(r, carry):
        _row_copy(hx_ref, r, xs_ref, pos_ref[base + r], sem).start()
        return carry

    lax.fori_loop(0, TD, body, 0, unroll=8)
    pltpu.make_async_copy(hx_ref, xs_ref.at[pl.ds(0, TD)], sem).wait()


def _dispatch(pos, hx, xs):
    n = hx.shape[0]
    return pl.pallas_call(
        _dispatch_kernel,
        grid_spec=pltpu.PrefetchScalarGridSpec(
            num_scalar_prefetch=1,
            grid=(n // TD,),
            in_specs=[pl.BlockSpec((TD, XW), lambda i, pos: (i, 0)),
                      pl.BlockSpec(memory_space=pl.ANY)],
            out_specs=pl.BlockSpec(memory_space=pl.ANY),
            scratch_shapes=[pltpu.SemaphoreType.DMA(())],
        ),
        out_shape=jax.ShapeDtypeStruct(xs.shape, xs.dtype),
        input_output_aliases={2: 0},
        compiler_params=_cparams(("arbitrary",)),
        name="dispatch",
    )(pos, hx, xs)


def _moe_kernel(tb_ref, tg_ref, nt_ref, xs_ref, w1_ref, w3_ref, w2_ref, o_ref, acc_ref):
    j = pl.program_id(0)
    e = pl.program_id(1)

    @pl.when(j < nt_ref[0])
    def _():
        @pl.when(e == 0)
        def _():
            acc_ref[...] = jnp.zeros_like(acc_ref)

        xt = xs_ref[:, :D].astype(BF)
        a = _dot(xt, w1_ref[0])
        he = (a * jax.nn.sigmoid(a)) * _dot(xt, w3_ref[0])
        y = _dot(he.astype(BF), w2_ref[0])
        cmb = xs_ref[:, D:]
        lane = lax.broadcasted_iota(jnp.int32, cmb.shape, 1)
        ce = jnp.sum(jnp.where(lane == tg_ref[j] * EPG + e, cmb, 0.0), axis=-1, keepdims=True)
        acc_ref[...] += ce * y

        @pl.when(e == EPG - 1)
        def _():
            o_ref[...] = acc_ref[...]

    @pl.when((j >= nt_ref[0]) & (e == EPG - 1))
    def _():
        o_ref[...] = jnp.zeros_like(o_ref)


def _experts(tile_block, tile_group, n_tiles, xs, wts):
    m = xs.shape[0]

    def w_idx(j, e, tb, tg, nt):
        return (tg[j] * EPG + jnp.where(j < nt[0], e, EPG - 1), 0, 0)

    return pl.pallas_call(
        _moe_kernel,
        grid_spec=pltpu.PrefetchScalarGridSpec(
            num_scalar_prefetch=3,
            grid=(m // TMOE, EPG),
            in_specs=[pl.BlockSpec((TMOE, XW), lambda j, e, tb, tg, nt: (tb[j], 0)),
                      pl.BlockSpec((1, D, DE), w_idx),
                      pl.BlockSpec((1, D, DE), w_idx),
                      pl.BlockSpec((1, DE, D), w_idx)],
            out_specs=pl.BlockSpec((TMOE, D), lambda j, e, tb, tg, nt: (j, 0)),
            scratch_shapes=[pltpu.VMEM((TMOE, D), F32)],
        ),
        out_shape=jax.ShapeDtypeStruct((m, D), F32),
        compiler_params=_cparams(("arbitrary", "arbitrary")),
        name="experts",
    )(tile_block, tile_group, n_tiles, xs, wts["exp_w1"], wts["exp_w3"], wts["exp_w2"])


def _combine_kernel(pos_ref, x1_ref, mod_ref, ys_ref, o_ref, buf_ref, sem, *, n_steps):
    i = pl.program_id(0)

    def issue(step, slot):
        def body(r, carry):
            _row_copy(ys_ref, pos_ref[step * TD + r], buf_ref.at[slot], r, sem.at[slot]).start()
            return carry

        lax.fori_loop(0, TD, body, 0, unroll=8)

    @pl.when(i == 0)
    def _():
        issue(0, 0)

    @pl.when(i + 1 < n_steps)
    def _():
        issue(i + 1, (i + 1) % 2)

    slot = i % 2
    pltpu.make_async_copy(ys_ref.at[pl.ds(0, TD)], buf_ref.at[slot], sem.at[slot]).wait()
    o_ref[...] = x1_ref[...] + mod_ref[0, 5:6, :] * buf_ref[slot]


def _combine(pos, x1, mod, mod_row, ys):
    n = x1.shape[0]
    n_steps = n // TD
    return pl.pallas_call(
        functools.partial(_combine_kernel, n_steps=n_steps),
        grid_spec=pltpu.PrefetchScalarGridSpec(
            num_scalar_prefetch=1,
            grid=(n_steps,),
            in_specs=[pl.BlockSpec((TD, D), lambda i, pos: (i, 0)),
                      pl.BlockSpec((1, 6, D), lambda i, pos: (mod_row(i), 0, 0)),
                      pl.BlockSpec(memory_space=pl.ANY)],
            out_specs=pl.BlockSpec((TD, D), lambda i, pos: (i, 0)),
            scratch_shapes=[pltpu.VMEM((2, TD, D), F32), pltpu.SemaphoreType.DMA((2,))],
        ),
        out_shape=jax.ShapeDtypeStruct((n, D), F32),
        compiler_params=_cparams(("arbitrary",)),
        name="combine",
    )(pos, x1, mod, ys)


def _slots(rinfo, offsets):
    group = rinfo[..., 0:1]
    off = jnp.sum(jnp.where(group == jnp.arange(NG, dtype=F32), offsets.astype(F32), 0.0), axis=-1)
    return (off + rinfo[..., 1]).astype(jnp.int32).reshape(-1)


def _group_layout(counts, max_tiles):
    padded = ((counts + TMOE - 1) // TMOE) * TMOE
    ends = jnp.cumsum(padded)
    offsets = ends - padded
    n_tiles = (ends[-1] // TMOE).astype(jnp.int32)
    tile = jnp.minimum(jnp.arange(max_tiles, dtype=jnp.int32), n_tiles - 1)
    tile_group = jnp.sum((tile[:, None] * TMOE >= ends[None, :]).astype(jnp.int32), axis=1)
    return offsets, tile, tile_group, n_tiles.reshape(1)


def _pad_heads(w, widths, perm):
    lead = w.shape[:-1]
    per = w.shape[-1] // NH
    w = w.reshape(lead + (NH, per))
    if per == QK:
        w = jnp.concatenate([w[..., :NOPE], w[..., NOPE:][..., perm]], axis=-1)
    w = jnp.pad(w, [(0, 0)] * len(lead) + [(0, 0), (0, HP - per)])
    return w.reshape(lead + (NH * HP,))


def _pad_gain(g, perm):
    g = jnp.concatenate([g[:NOPE], g[NOPE:][perm], jnp.zeros((HP - QK,), F32)])
    return g.reshape(1, HP)


def _block_diag(w):
    per = CH // LRU_BLOCK
    w = w.reshape(DR // CH, per, LRU_BLOCK, LRU_BLOCK)
    bd = jnp.einsum("jpab,pq->jpaqb", w, jnp.eye(per, dtype=w.dtype))
    return bd.reshape(DR // CH, CH, CH)


def _prepare_shared(l, p):
    w_in = p["w_in"][l]
    o1, o2, o3, o4, o5 = DR, 2 * DR, 2 * DR + QL, 2 * DR + QL + KVL, 2 * DR + QL + KVL + ROPE
    bd = jnp.stack([_block_diag(p["lru_wa"][l, 0]), _block_diag(p["lru_wx"][l, 0]),
                    _block_diag(p["lru_wa"][l, 1]), _block_diag(p["lru_wx"][l, 1])]).astype(BF)
    wom = p["w_o_mla"][l].reshape(NH, VD, D)
    wom = jnp.pad(wom, ((0, 0), (0, HP - VD), (0, 0))).reshape(DH, D)
    router_w = jnp.concatenate([p["router_we"][l], p["router_wg"][l],
                                jnp.zeros((D, LANES - NE - NG), F32)], axis=1)
    router_b = jnp.concatenate([p["router_be"][l], p["router_bg"][l],
                                jnp.zeros((LANES - NE - NG,), F32)]).reshape(1, LANES)
    return {
        "n1": p["norm1_g"][l].reshape(1, D), "n2": p["norm2_g"][l].reshape(1, D),
        "w_x": w_in[:, :o1].astype(BF), "w_g": w_in[:, o1:o2].astype(BF),
        "w_q": w_in[:, o2:o3].astype(BF), "w_kv": w_in[:, o3:o4], "w_kr": w_in[:, o4:o5],
        "w_gate": w_in[:, o5:].astype(BF),
        "qan": p["q_a_norm"][l].reshape(1, QL), "kvan": p["kv_a_norm"][l].reshape(1, KVL),
        "w_uk": _pad_heads(p["w_uk"][l], None, None).astype(BF),
        "w_uv": _pad_heads(p["w_uv"][l], None, None).astype(BF),
        "conv_w": p["conv_w"][l], "conv_b": p["conv_b"][l].reshape(1, DR), "bd": bd,
        "lru_ba": p["lru_ba"][l], "lru_bx": p["lru_bx"][l], "lru_lam": p["lru_lam"][l],
        "w_o_rnn": p["w_o_rnn"][l].astype(BF), "w_o_mla": wom.astype(BF), "w_out": p["w_out"][l].astype(BF),
        "router_w": router_w, "router_b": router_b,
        "exp_w1": p["exp_w1"][l].astype(BF), "exp_w3": p["exp_w3"][l].astype(BF),
        "exp_w2": p["exp_w2"][l].astype(BF),
    }


def _with_rope_order(l, p, shared, perm):
    w = dict(shared)
    zeros = lambda n: jnp.zeros((D, n), F32)
    w["w_kvr"] = jnp.concatenate([shared["w_kv"], zeros(NOPE), shared["w_kr"][:, perm],
                                  zeros(HP - QK)], axis=1).astype(BF)
    w["w_uq"] = _pad_heads(p["w_uq"][l], None, perm).astype(BF)
    w["gq"] = _pad_gain(p["q_norm"][l], perm)
    w["gk"] = _pad_gain(p["k_norm"][l], perm)
    return w


def _rope_tables(n_tokens, perm):
    rows = n_tokens // GRID_W
    row = jnp.repeat(jnp.arange(rows), GRID_W).astype(F32)
    col = jnp.tile(jnp.arange(GRID_W), rows).astype(F32)
    axis_dim = ROPE // 2
    inv = ROPE_BASE ** (-jnp.arange(0, axis_dim, 2, dtype=F32) / axis_dim)
    ang = jnp.concatenate([row[:, None] * inv, col[:, None] * inv], axis=-1)
    cos, sin = jnp.cos(ang), jnp.sin(ang)
    ones = lambda n: jnp.ones((n_tokens, n), F32)
    zeros = lambda n: jnp.zeros((n_tokens, n), F32)
    cos_t = jnp.concatenate([ones(NOPE), cos, cos, ones(HP - QK)], axis=1)
    sin_t = jnp.concatenate([zeros(NOPE), -sin, sin, zeros(HP - QK)], axis=1)
    return cos_t, sin_t


def kernel(x_prompt, x_sample, cache_mla_ckv, cache_mla_krope, state_rglru, c, c_ctx, norm1_g, norm2_g, w_mod, b_mod, w_in, conv_w, conv_b, lru_wa, lru_ba, lru_wx, lru_bx, lru_lam, q_a_norm, kv_a_norm, w_uq, w_uk, w_uv, q_norm, k_norm, w_o_rnn, w_o_mla, w_out, router_wg, router_bg, router_we, router_be, exp_w1, exp_w3, exp_w2):
    p = dict(norm1_g=norm1_g, norm2_g=norm2_g, w_in=w_in, conv_w=conv_w, conv_b=conv_b,
             lru_wa=lru_wa, lru_ba=lru_ba, lru_wx=lru_wx, lru_bx=lru_bx, lru_lam=lru_lam,
             q_a_norm=q_a_norm, kv_a_norm=kv_a_norm, w_uq=w_uq, w_uk=w_uk, w_uv=w_uv,
             q_norm=q_norm, k_norm=k_norm, w_o_rnn=w_o_rnn, w_o_mla=w_o_mla, w_out=w_out,
             router_wg=router_wg, router_bg=router_bg, router_we=router_we, router_be=router_be,
             exp_w1=exp_w1, exp_w3=exp_w3, exp_w2=exp_w2)
    depth = w_in.shape[0]
    nb, seq, _ = x_prompt.shape
    db, dseq, _ = x_sample.shape
    ident = np.arange(ROPE)
    halves = np.concatenate([np.arange(0, ROPE, 2), np.arange(1, ROPE, 2)])
    rope_tabs = _rope_tables(dseq, halves)
    cond8 = jnp.concatenate([c_ctx[None, :], c, jnp.zeros((SUBLANES - 1 - db, D), F32)], axis=0)
    ctx_row = lambda i: 0
    lat_row = lambda i: i + 1
    ctx_tok_row = lambda i: 0
    lat_tok_row = lambda i: (i * TD) // dseq + 1
    n_ctx, n_lat = nb * seq, db * dseq
    max_tiles = (n_ctx + n_lat) // TMOE + NG

    y_prompt, y_sample = x_prompt, x_sample
    ckv_list, krope_list, rnn_list = [], [], []
    for l in range(depth):
        shared = _prepare_shared(l, p)
        w_ctx = _with_rope_order(l, p, shared, ident)
        w_lat = _with_rope_order(l, p, shared, halves)
        mod = _modulation(cond8, w_mod[l], b_mod[l]).reshape(SUBLANES, 6, D)

        h, xr, gg, q, k, v, ckv, kro = _projections(y_prompt, mod, ctx_row, w_ctx, None, True)
        yr, h_fin = _rglru(xr, gg, shared, None, True)
        ya = _attention(q, k, v, None, None, NH)
        x1_c, hx_c, ri_c, cnt_c = _merge_out(y_prompt, h, yr, ya, mod, ctx_row, shared,
                                             jnp.zeros((SUBLANES, LANES), F32))
        ckv_list.append(ckv)
        krope_list.append(kro)
        rnn_list.append(h_fin)

        krp_cache = jnp.pad(cache_mla_krope[:, l][..., halves], ((0, 0), (0, 0), (NOPE, HP - QK)))
        kc, vc = _cache_keys_values(cache_mla_ckv[:, l], krp_cache, w_lat)
        h, xr, gg, q, k, v = _projections(y_sample, mod, lat_row, w_lat, rope_tabs, False)
        yr, _ = _rglru(xr, gg, shared, state_rglru[:, l], False)
        ya = _attention(q, k, v, kc, vc, 1)
        x1_l, hx_l, ri_l, cnt_l = _merge_out(y_sample, h, yr, ya, mod, lat_row, shared, cnt_c)

        offsets, tile_block, tile_group, n_tiles = _group_layout(cnt_l[0, :NG].astype(jnp.int32), max_tiles)
        pos_c, pos_l = _slots(ri_c, offsets), _slots(ri_l, offsets)
        xs = jnp.zeros((max_tiles * TMOE, XW), F32)
        xs = _dispatch(pos_c, hx_c.reshape(n_ctx, XW), xs)
        xs = _dispatch(pos_l, hx_l.reshape(n_lat, XW), xs)
        ys = _experts(tile_block, tile_group, n_tiles, xs, shared)
        y_prompt = _combine(pos_c, x1_c.reshape(n_ctx, D), mod, ctx_tok_row, ys).reshape(nb, seq, D)
        y_sample = _combine(pos_l, x1_l.reshape(n_lat, D), mod, lat_tok_row, ys).reshape(db, dseq, D)

    return (y_prompt, y_sample, jnp.stack(ckv_list, axis=1), jnp.stack(krope_list, axis=1),
            jnp.stack(rnn_list, axis=1))
```

```python
import functools
import math

import numpy as np
import jax
import jax.numpy as jnp
from jax import lax
from jax.experimental import pallas as pl
from jax.experimental.pallas import tpu as pltpu

D = 1024
DR = 1024
QL = 384
KVL = 256
NH = 8
NOPE = 64
ROPE = 32
QK = NOPE + ROPE
VD = 64
HP = 128
DH = NH * HP
GRID_W = 64
ROPE_BASE = 10000.0
EPS = 1e-6
LRU_C = 8.0
LRU_BLOCK = 64
CH = 256
NLT = CH // 128
NG = 4
EPG = 4
NE = NG * EPG
DE = 512
LANES = 128
SUBLANES = 8
TM = 512
TMOE = 512
TD = 256
XW = D + 128
VMEM_LIMIT = 52 * 1024 * 1024
BF = jnp.bfloat16
F32 = jnp.float32


def _cparams(sem):
    return pltpu.CompilerParams(dimension_semantics=sem, vmem_limit_bytes=VMEM_LIMIT)


def _dot(a, b):
    return jnp.dot(a, b, preferred_element_type=F32)


def _dot_nt(a, b):
    return lax.dot_general(a, b, (((1,), (1,)), ((), ())), preferred_element_type=F32)


def _rms(x, g, width):
    ms = jnp.sum(x * x, axis=-1, keepdims=True) * (1.0 / width)
    return x * lax.rsqrt(ms + EPS) * g


def _mod_kernel(c_ref, w_ref, b_ref, o_ref):
    c = c_ref[...]
    s = c * jax.nn.sigmoid(c)
    o_ref[...] = _dot(s, w_ref[...]) + b_ref[...]


def _modulation(cond8, w_mod, b_mod):
    n = w_mod.shape[1]
    return pl.pallas_call(
        _mod_kernel,
        grid=(n // D,),
        in_specs=[
            pl.BlockSpec((SUBLANES, D), lambda j: (0, 0)),
            pl.BlockSpec((D, D), lambda j: (0, j)),
            pl.BlockSpec((1, D), lambda j: (0, j)),
        ],
        out_specs=pl.BlockSpec((SUBLANES, D), lambda j: (0, j)),
        out_shape=jax.ShapeDtypeStruct((SUBLANES, n), F32),
        compiler_params=_cparams(("arbitrary",)),
        name="modulation",
    )(cond8, w_mod, b_mod.reshape(1, n))


def _head_norm(xh, gain, cos, sins):
    ms = jnp.sum(xh * xh, axis=-1, keepdims=True) * (1.0 / QK)
    y = xh * lax.rsqrt(ms + EPS) * gain
    if cos is None:
        return y
    lane = lax.broadcasted_iota(jnp.int32, y.shape, 1)
    first = (lane >= NOPE) & (lane < NOPE + ROPE // 2)
    partner = jnp.where(first, pltpu.roll(y, HP - ROPE // 2, 1), pltpu.roll(y, ROPE // 2, 1))
    return y * cos + partner * sins


def _keys_values(ckv, krp, wuk_ref, wuv_ref, gk, cos, sins, k_ref, v_ref):
    cb = ckv.astype(BF)
    kn = _dot(cb, wuk_ref[...])
    v_ref[...] = _dot(cb, wuv_ref[...]).astype(BF)
    for h in range(NH):
        kh = kn[:, h * HP:(h + 1) * HP] + krp
        k_ref[:, h * HP:(h + 1) * HP] = _head_norm(kh, gk, cos, sins).astype(BF)


def _proj_kernel(*refs, rope, emit_cache):
    it = iter(refs)
    x_ref, mod_ref, n1_ref = next(it), next(it), next(it)
    wx_ref, wg_ref, wq_ref, wkvr_ref = next(it), next(it), next(it), next(it)
    qan_ref, kvan_ref, wuq_ref, gq_ref = next(it), next(it), next(it), next(it)
    wuk_ref, gk_ref, wuv_ref = next(it), next(it), next(it)
    cos = sins = None
    if rope:
        cos, sins = next(it)[...], next(it)[...]
    h_ref, xr_ref, gg_ref, q_ref, k_ref, v_ref = (next(it) for _ in range(6))
    if emit_cache:
        ckv_ref, kro_ref = next(it), next(it)

    x = x_ref[...]
    sh1 = mod_ref[0, 0:1, :]
    sc1 = mod_ref[0, 1:2, :]
    h = _rms(x, n1_ref[...], D) * (1.0 + sc1) + sh1
    hb = h.astype(BF)
    h_ref[...] = hb
    xr_ref[...] = _dot(hb, wx_ref[...])
    gg_ref[...] = jax.nn.gelu(_dot(hb, wg_ref[...])).astype(BF)

    qn = _rms(_dot(hb, wq_ref[...]), qan_ref[...], QL)
    q = _dot(qn.astype(BF), wuq_ref[...])
    gq = gq_ref[...]
    for hd in range(NH):
        qh = q[:, hd * HP:(hd + 1) * HP]
        q_ref[:, hd * HP:(hd + 1) * HP] = _head_norm(qh, gq, cos, sins).astype(BF)

    kvr = _dot(hb, wkvr_ref[...])
    ckv = _rms(kvr[:, :KVL], kvan_ref[...], KVL)
    krp = kvr[:, KVL:KVL + HP]
    if emit_cache:
        ckv_ref[...] = ckv
        kro_ref[...] = krp[:, NOPE:NOPE + ROPE]
    _keys_values(ckv, krp, wuk_ref, wuv_ref, gk_ref[...], cos, sins, k_ref, v_ref)


def _resident(shape):
    return pl.BlockSpec(shape, lambda i: (0,) * len(shape), pipeline_mode=pl.Buffered(1))


def _projections(x, mod, mod_row, wts, rope_tabs, emit_cache):
    n = x.shape[0]
    rope = rope_tabs is not None
    tile = lambda w: pl.BlockSpec((TM, w), lambda i: (i, 0))
    in_specs = [
        tile(D),
        pl.BlockSpec((1, 6, D), lambda i: (mod_row(i), 0, 0)),
        _resident((1, D)),
        _resident((D, DR)), _resident((D, DR)), _resident((D, QL)), _resident((D, KVL + HP)),
        _resident((1, QL)), _resident((1, KVL)), _resident((QL, DH)), _resident((1, HP)),
        _resident((KVL, DH)), _resident((1, HP)), _resident((KVL, DH)),
    ]
    args = [x, mod, wts["n1"], wts["w_x"], wts["w_g"], wts["w_q"], wts["w_kvr"],
            wts["qan"], wts["kvan"], wts["w_uq"], wts["gq"], wts["w_uk"], wts["gk"], wts["w_uv"]]
    if rope:
        tiles_per_seq = rope_tabs[0].shape[0] // TM
        in_specs += [pl.BlockSpec((TM, HP), lambda i: (i % tiles_per_seq, 0))] * 2
        args += list(rope_tabs)
    out_specs = [tile(D), tile(DR), tile(DR), tile(DH), tile(DH), tile(DH)]
    out_shape = [jax.ShapeDtypeStruct((n, D), BF), jax.ShapeDtypeStruct((n, DR), F32),
                 jax.ShapeDtypeStruct((n, DR), BF), jax.ShapeDtypeStruct((n, DH), BF),
                 jax.ShapeDtypeStruct((n, DH), BF), jax.ShapeDtypeStruct((n, DH), BF)]
    if emit_cache:
        out_specs += [tile(KVL), tile(ROPE)]
        out_shape += [jax.ShapeDtypeStruct((n, KVL), F32), jax.ShapeDtypeStruct((n, ROPE), F32)]
    return pl.pallas_call(
        functools.partial(_proj_kernel, rope=rope, emit_cache=emit_cache),
        grid=(n // TM,),
        in_specs=in_specs,
        out_specs=out_specs,
        out_shape=out_shape,
        compiler_params=_cparams(("arbitrary",)),
        name="projections",
    )(*args)


def _cache_kv_kernel(ckv_ref, krp_ref, wuk_ref, gk_ref, wuv_ref, k_ref, v_ref):
    _keys_values(ckv_ref[0], krp_ref[0], wuk_ref, wuv_ref, gk_ref[...], None, None,
                 k_ref.at[0], v_ref.at[0])


def _cache_keys_values(ckv, krp, wts):
    b, s, _ = ckv.shape
    full = lambda shape: pl.BlockSpec(shape, lambda i: (0,) * len(shape))
    return pl.pallas_call(
        _cache_kv_kernel,
        grid=(b,),
        in_specs=[pl.BlockSpec((1, s, KVL), lambda i: (i, 0, 0)),
                  pl.BlockSpec((1, s, HP), lambda i: (i, 0, 0)),
                  full((KVL, DH)), full((1, HP)), full((KVL, DH))],
        out_specs=[pl.BlockSpec((1, s, DH), lambda i: (i, 0, 0))] * 2,
        out_shape=[jax.ShapeDtypeStruct((b, s, DH), BF)] * 2,
        compiler_params=_cparams(("arbitrary",)),
        name="cache_keys_values",
    )(ckv, krp, wts["w_uk"], wts["gk"], wts["w_uv"])


def _sigmoid(x):
    return 0.5 * jnp.tanh(0.5 * x) + 0.5


def _tile_scan(a, b, forward):
    row = lax.broadcasted_iota(jnp.int32, a.shape, 0)
    for k in (1, 2, 4):
        if forward:
            shift, valid = k, row >= k
        else:
            shift, valid = SUBLANES - k, row < SUBLANES - k
        a_prev = jnp.where(valid, pltpu.roll(a, shift, 0), 1.0)
        b_prev = jnp.where(valid, pltpu.roll(b, shift, 0), 0.0)
        b = a * b_prev + b
        a = a * a_prev
    return a, b


def _rglru_kernel(*refs, t, has_h0, emit_state):
    it = iter(refs)
    xr_ref, gg_ref, cw_ref, cb_ref, bd_ref, ba_ref, bx_ref, lam_ref = (next(it) for _ in range(8))
    h0_ref = next(it) if has_h0 else None
    y_ref = next(it)
    hf_ref = next(it) if emit_state else None
    a_scr, b_scr, h_scr = (next(it) for _ in range(3))

    n_tiles = t // SUBLANES
    x = xr_ref[0]
    row = lax.broadcasted_iota(jnp.int32, x.shape, 0)
    xm1 = jnp.where(row >= 1, pltpu.roll(x, 1, 0), 0.0)
    xp1 = jnp.where(row < t - 1, pltpu.roll(x, t - 1, 0), 0.0)
    xp2 = jnp.where(row < t - 2, pltpu.roll(x, t - 2, 0), 0.0)
    xc = (cb_ref[...] + xm1 * cw_ref[0:1, :] + x * cw_ref[1:2, :]
          + xp1 * cw_ref[2:3, :] + xp2 * cw_ref[3:4, :])
    xcb = xc.astype(BF)
    for d in range(2):
        r = _sigmoid(_dot(xcb, bd_ref[2 * d, 0]) + ba_ref[d:d + 1, :])
        gi = _sigmoid(_dot(xcb, bd_ref[2 * d + 1, 0]) + bx_ref[d:d + 1, :])
        nl = -lam_ref[d:d + 1, :]
        softplus = jnp.maximum(nl, 0.0) + jnp.log(1.0 + jnp.exp(-jnp.abs(nl)))
        a = jnp.exp(r * ((-LRU_C) * softplus))
        a_scr[d] = a
        b_scr[d] = jnp.sqrt(1.0 - a * a) * (gi * xc)

    def step(i, carry):
        out = []
        for d in range(2):
            tile = i if d == 0 else n_tiles - 1 - i
            rows = pl.ds(pl.multiple_of(tile * SUBLANES, SUBLANES), SUBLANES)
            for c in range(NLT):
                lanes = slice(c * LANES, (c + 1) * LANES)
                decay, local = _tile_scan(a_scr[d, rows, lanes], b_scr[d, rows, lanes], d == 0)
                h = local + decay * carry[d * NLT + c]
                h_scr[d, rows, lanes] = h
                last = h[SUBLANES - 1:SUBLANES, :] if d == 0 else h[0:1, :]
                out.append(jnp.broadcast_to(last, (SUBLANES, LANES)))
        return tuple(out)

    init = []
    for d in range(2):
        for c in range(NLT):
            if has_h0:
                h0 = h0_ref[0, d:d + 1, c * LANES:(c + 1) * LANES]
                init.append(jnp.broadcast_to(h0, (SUBLANES, LANES)))
            else:
                init.append(jnp.zeros((SUBLANES, LANES), F32))
    final = lax.fori_loop(0, n_tiles, step, tuple(init), unroll=2)

    if emit_state:
        for d in range(2):
            for c in range(NLT):
                hf_ref[0, d:d + 1, c * LANES:(c + 1) * LANES] = final[d * NLT + c][0:1, :]
    y_ref[0] = ((h_scr[0] + h_scr[1]) * gg_ref[0].astype(F32)).astype(BF)


def _rglru(xr, gg, wts, h0, emit_state):
    b, t, _ = xr.shape
    nc = DR // CH
    has_h0 = h0 is not None
    chunk = lambda r: pl.BlockSpec((r, CH), lambda i, j: (0, j))
    seq = pl.BlockSpec((1, t, CH), lambda i, j: (i, 0, j))
    state = pl.BlockSpec((1, 2, CH), lambda i, j: (i, 0, j))
    in_specs = [seq, seq, chunk(4), chunk(1),
                pl.BlockSpec((4, 1, CH, CH), lambda i, j: (0, j, 0, 0)),
                chunk(2), chunk(2), chunk(2)]
    args = [xr, gg, wts["conv_w"], wts["conv_b"], wts["bd"], wts["lru_ba"], wts["lru_bx"], wts["lru_lam"]]
    if has_h0:
        in_specs.append(state)
        args.append(h0)
    out_specs = [seq]
    out_shape = [jax.ShapeDtypeStruct((b, t, DR), BF)]
    if emit_state:
        out_specs.append(state)
        out_shape.append(jax.ShapeDtypeStruct((b, 2, DR), F32))
    res = pl.pallas_call(
        functools.partial(_rglru_kernel, t=t, has_h0=has_h0, emit_state=emit_state),
        grid=(b, nc),
        in_specs=in_specs,
        out_specs=out_specs,
        out_shape=out_shape,
        scratch_shapes=[pltpu.VMEM((2, t, CH), F32)] * 3,
        compiler_params=_cparams(("arbitrary", "arbitrary")),
        name="rglru",
    )(*args)
    return res if emit_state else (res[0], None)


def _attn_kernel(*refs, t, n_heads, has_ctx, q_block):
    it = iter(refs)
    q_ref, k_ref, v_ref = next(it), next(it), next(it)
    kc_ref = vc_ref = None
    if has_ctx:
        kc_ref, vc_ref = next(it), next(it)
    o_ref = next(it)
    scale = QK ** -0.5
    for hd in range(n_heads):
        cols = slice(hd * HP, (hd + 1) * HP)
        k = k_ref[0, :, cols]
        v = v_ref[0, :, cols]
        if has_ctx:
            kc = kc_ref[0, :, cols]
            vc = vc_ref[0, :, cols]
        for qb in range(t // q_block):
            rows = slice(qb * q_block, (qb + 1) * q_block)
            q = q_ref[0, rows, cols]
            s = _dot_nt(q, k) * scale
            m = jnp.max(s, axis=-1, keepdims=True)
            if has_ctx:
                sc = _dot_nt(q, kc) * scale
                m = jnp.maximum(m, jnp.max(sc, axis=-1, keepdims=True))
            p = jnp.exp(s - m)
            den = jnp.sum(p, axis=-1, keepdims=True)
            o = _dot(p.astype(BF), v)
            if has_ctx:
                pc = jnp.exp(sc - m)
                den = den + jnp.sum(pc, axis=-1, keepdims=True)
                o = o + _dot(pc.astype(BF), vc)
            o_ref[0, rows, cols] = (o / den).astype(BF)


def _attention(q, k, v, kc, vc, heads_per_step):
    b, t, _ = q.shape
    has_ctx = kc is not None
    w = heads_per_step * HP
    blk = lambda n: pl.BlockSpec((1, n, w), lambda i, j: (i, 0, j))
    in_specs = [blk(t), blk(t), blk(t)]
    args = [q, k, v]
    if has_ctx:
        in_specs += [blk(kc.shape[1])] * 2
        args += [kc, vc]
    return pl.pallas_call(
        functools.partial(_attn_kernel, t=t, n_heads=heads_per_step, has_ctx=has_ctx, q_block=min(t, 256)),
        grid=(b, NH // heads_per_step),
        in_specs=in_specs,
        out_specs=blk(t),
        out_shape=jax.ShapeDtypeStruct((b, t, DH), BF),
        compiler_params=_cparams(("arbitrary", "arbitrary")),
        name="attention",
    )(*args)


def _route(logits):
    lane = lax.broadcasted_iota(jnp.int32, logits.shape, 1)
    lanef = lane.astype(F32)
    neg = -jnp.inf
    big = float(LANES)
    gl = jnp.where((lane >= NE) & (lane < NE + NG), logits, neg)
    gmax = jnp.max(gl, axis=-1, keepdims=True)
    gidx = jnp.min(jnp.where(gl == gmax, lanef, big), axis=-1, keepdims=True) - float(NE)
    gw = 1.0 / jnp.sum(jnp.exp(gl - gmax), axis=-1, keepdims=True)
    lo = gidx * float(EPG)
    el = jnp.where((lanef >= lo) & (lanef < lo + float(EPG)), logits, neg)
    v1 = jnp.max(el, axis=-1, keepdims=True)
    i1 = jnp.min(jnp.where(el == v1, lanef, big), axis=-1, keepdims=True)
    el2 = jnp.where(lanef == i1, neg, el)
    v2 = jnp.max(el2, axis=-1, keepdims=True)
    i2 = jnp.min(jnp.where(el2 == v2, lanef, big), axis=-1, keepdims=True)
    e2 = jnp.exp(v2 - v1)
    w1 = gw / (1.0 + e2)
    w2 = gw * e2 / (1.0 + e2)
    cmb = jnp.where(lanef == i1, w1, 0.0) + jnp.where(lanef == i2, w2, 0.0)
    return cmb, gidx


def _out_kernel(x_ref, h_ref, yr_ref, ya_ref, mod_ref, wgate_ref, wor_ref, wom_ref, wout_ref,
                n2_ref, rw_ref, rb_ref, cnt0_ref, x1_ref, hx_ref, rinfo_ref, cnt_ref, run_ref):
    @pl.when(pl.program_id(0) == 0)
    def _():
        run_ref[...] = cnt0_ref[...]

    gl = _dot(h_ref[...], wgate_ref[...])
    merged = (_sigmoid(gl[:, :D]) * _dot(yr_ref[...], wor_ref[...])
              + _sigmoid(gl[:, D:]) * _dot(ya_ref[...], wom_ref[...]))
    mix = _dot(merged.astype(BF), wout_ref[...])
    x1 = x_ref[...] + mod_ref[0, 2:3, :] * mix
    x1_ref[...] = x1
    h2 = _rms(x1, n2_ref[...], D) * (1.0 + mod_ref[0, 4:5, :]) + mod_ref[0, 3:4, :]
    h2_hi = h2.astype(BF)
    h2_lo = (h2 - h2_hi.astype(F32)).astype(BF)
    part = _dot(h2_hi, rw_ref[...])
    logits = part[:, :LANES] + part[:, LANES:] + _dot(h2_lo, rw_ref[:, :LANES]) + rb_ref[...]
    cmb, gidx = _route(logits)
    hx_ref[:, :D] = h2
    hx_ref[:, D:] = cmb

    lanef = lax.broadcasted_iota(jnp.int32, cmb.shape, 1).astype(F32)
    ghot = jnp.where(lanef == gidx, 1.0, 0.0)
    r_i = lax.broadcasted_iota(jnp.int32, (TM, TM), 0)
    c_i = lax.broadcasted_iota(jnp.int32, (TM, TM), 1)
    tri = jnp.where(r_i > c_i, 1.0, 0.0).astype(BF)
    before = _dot(tri, ghot.astype(BF)) + run_ref[0:1, :]
    rank = jnp.sum(before * ghot, axis=-1, keepdims=True)
    rinfo_ref[...] = jnp.where(lanef == 0.0, gidx, jnp.where(lanef == 1.0, rank, 0.0))
    run = run_ref[...] + jnp.sum(ghot, axis=0, keepdims=True)
    run_ref[...] = run
    cnt_ref[...] = run


def _merge_out(x, h, yr, ya, mod, mod_row, wts, cnt0):
    n = x.shape[0]
    tile = lambda w: pl.BlockSpec((TM, w), lambda i: (i, 0))
    return pl.pallas_call(
        _out_kernel,
        grid=(n // TM,),
        in_specs=[tile(D), tile(D), tile(DR), tile(DH),
                  pl.BlockSpec((1, 6, D), lambda i: (mod_row(i), 0, 0)),
                  _resident((D, 2 * D)), _resident((DR, D)), _resident((DH, D)), _resident((D, D)),
                  _resident((1, D)), _resident((D, 2 * LANES)), _resident((1, LANES)),
                  _resident((SUBLANES, LANES))],
        out_specs=[tile(D), tile(XW), tile(LANES), pl.BlockSpec((SUBLANES, LANES), lambda i: (0, 0))],
        out_shape=[jax.ShapeDtypeStruct((n, D), F32), jax.ShapeDtypeStruct((n, XW), F32),
                   jax.ShapeDtypeStruct((n, LANES), F32),
                   jax.ShapeDtypeStruct((SUBLANES, LANES), F32)],
        scratch_shapes=[pltpu.VMEM((SUBLANES, LANES), F32)],
        compiler_params=_cparams(("arbitrary",)),
        name="merge_out",
    )(x, h, yr, ya, mod, wts["w_gate"], wts["w_o_rnn"], wts["w_o_mla"], wts["w_out"],
      wts["n2"], wts["router_w"], wts["router_b"], cnt0)


def _row_copy(src, src_row, dst, dst_row, sem):
    return pltpu.make_async_copy(src.at[pl.ds(src_row, 1)], dst.at[pl.ds(dst_row, 1)], sem)


def _dispatch_kernel(pos_ref, hx_ref, xs_in_ref, xs_ref, sem):
    del xs_in_ref
    base = pl.program_id(0) * TD

    def body(r, carry):
        _row_copy(hx_ref, r, xs_ref, pos_ref[base + r], sem).start()
        return carry

    lax.fori_loop(0, TD, body, 0, unroll=8)
    pltpu.make_async_copy(hx_ref, xs_ref.at[pl.ds(0, TD)], sem).wait()


def _dispatch(pos, hx, xs):
    n = hx.shape[0]
    return pl.pallas_call(
        _dispatch_kernel,
        grid_spec=pltpu.PrefetchScalarGridSpec(
            num_scalar_prefetch=1,
            grid=(n // TD,),
            in_specs=[pl.BlockSpec((TD, XW), lambda i, pos: (i, 0)),
                      pl.BlockSpec(memory_space=pl.ANY)],
            out_specs=pl.BlockSpec(memory_space=pl.ANY),
            scratch_shapes=[pltpu.SemaphoreType.DMA(())],
        ),
        out_shape=jax.ShapeDtypeStruct(xs.shape, xs.dtype),
        input_output_aliases={2: 0},
        compiler_params=_cparams(("arbitrary",)),
        name="dispatch",
    )(pos, hx, xs)


def _moe_kernel(tb_ref, tg_ref, nt_ref, xs_ref, w1_ref, w3_ref, w2_ref, o_ref, acc_ref):
    j = pl.program_id(0)
    e = pl.program_id(1)

    @pl.when(j < nt_ref[0])
    def _():
        @pl.when(e == 0)
        def _():
            acc_ref[...] = jnp.zeros_like(acc_ref)

        xt = xs_ref[:, :D].astype(BF)
        a = _dot(xt, w1_ref[0])
        he = (a * jax.nn.sigmoid(a)) * _dot(xt, w3_ref[0])
        y = _dot(he.astype(BF), w2_ref[0])
        cmb = xs_ref[:, D:]
        lane = lax.broadcasted_iota(jnp.int32, cmb.shape, 1)
        ce = jnp.sum(jnp.where(lane == tg_ref[j] * EPG + e, cmb, 0.0), axis=-1, keepdims=True)
        acc_ref[...] += ce * y

        @pl.when(e == EPG - 1)
        def _():
            o_ref[...] = acc_ref[...]

    @pl.when((j >= nt_ref[0]) & (e == EPG - 1))
    def _():
        o_ref[...] = jnp.zeros_like(o_ref)


def _experts(tile_block, tile_group, n_tiles, xs, wts):
    m = xs.shape[0]

    def w_idx(j, e, tb, tg, nt):
        return (tg[j] * EPG + jnp.where(j < nt[0], e, EPG - 1), 0, 0)

    return pl.pallas_call(
        _moe_kernel,
        grid_spec=pltpu.PrefetchScalarGridSpec(
            num_scalar_prefetch=3,
            grid=(m // TMOE, EPG),
            in_specs=[pl.BlockSpec((TMOE, XW), lambda j, e, tb, tg, nt: (tb[j], 0)),
                      pl.BlockSpec((1, D, DE), w_idx),
                      pl.BlockSpec((1, D, DE), w_idx),
                      pl.BlockSpec((1, DE, D), w_idx)],
            out_specs=pl.BlockSpec((TMOE, D), lambda j, e, tb, tg, nt: (j, 0)),
            scratch_shapes=[pltpu.VMEM((TMOE, D), F32)],
        ),
        out_shape=jax.ShapeDtypeStruct((m, D), F32),
        compiler_params=_cparams(("arbitrary", "arbitrary")),
        name="experts",
    )(tile_block, tile_group, n_tiles, xs, wts["exp_w1"], wts["exp_w3"], wts["exp_w2"])


def _combine_kernel(pos_ref, x1_ref, mod_ref, ys_ref, o_ref, buf_ref, sem, *, n_steps):
    i = pl.program_id(0)

    def issue(step, slot):
        def body(r, carry):
            _row_copy(ys_ref, pos_ref[step * TD + r], buf_ref.at[slot], r, sem.at[slot]).start()
            return carry

        lax.fori_loop(0, TD, body, 0, unroll=8)

    @pl.when(i == 0)
    def _():
        issue(0, 0)

    @pl.when(i + 1 < n_steps)
    def _():
        issue(i + 1, (i + 1) % 2)

    slot = i % 2
    pltpu.make_async_copy(ys_ref.at[pl.ds(0, TD)], buf_ref.at[slot], sem.at[slot]).wait()
    o_ref[...] = x1_ref[...] + mod_ref[0, 5:6, :] * buf_ref[slot]


def _combine(pos, x1, mod, mod_row, ys):
    n = x1.shape[0]
    n_steps = n // TD
    return pl.pallas_call(
        functools.partial(_combine_kernel, n_steps=n_steps),
        grid_spec=pltpu.PrefetchScalarGridSpec(
            num_scalar_prefetch=1,
            grid=(n_steps,),
            in_specs=[pl.BlockSpec((TD, D), lambda i, pos: (i, 0)),
                      pl.BlockSpec((1, 6, D), lambda i, pos: (mod_row(i), 0, 0)),
                      pl.BlockSpec(memory_space=pl.ANY)],
            out_specs=pl.BlockSpec((TD, D), lambda i, pos: (i, 0)),
            scratch_shapes=[pltpu.VMEM((2, TD, D), F32), pltpu.SemaphoreType.DMA((2,))],
        ),
        out_shape=jax.ShapeDtypeStruct((n, D), F32),
        compiler_params=_cparams(("arbitrary",)),
        name="combine",
    )(pos, x1, mod, ys)


def _slots(rinfo, offsets):
    group = rinfo[..., 0:1]
    off = jnp.sum(jnp.where(group == jnp.arange(NG, dtype=F32), offsets.astype(F32), 0.0), axis=-1)
    return (off + rinfo[..., 1]).astype(jnp.int32).reshape(-1)


def _group_layout(counts, max_tiles):
    padded = ((counts + TMOE - 1) // TMOE) * TMOE
    ends = jnp.cumsum(padded)
    offsets = ends - padded
    n_tiles = (ends[-1] // TMOE).astype(jnp.int32)
    tile = jnp.minimum(jnp.arange(max_tiles, dtype=jnp.int32), jnp.maximum(n_tiles - 1, 0))
    tile_group = jnp.sum((tile[:, None] * TMOE >= ends[None, :]).astype(jnp.int32), axis=1)
    return offsets, tile, tile_group, n_tiles.reshape(1)


def _pad_heads(w, widths, perm):
    lead = w.shape[:-1]
    per = w.shape[-1] // NH
    w = w.reshape(lead + (NH, per))
    if per == QK:
        w = jnp.concatenate([w[..., :NOPE], w[..., NOPE:][..., perm]], axis=-1)
    w = jnp.pad(w, [(0, 0)] * len(lead) + [(0, 0), (0, HP - per)])
    return w.reshape(lead + (NH * HP,))


def _pad_gain(g, perm):
    g = jnp.concatenate([g[:NOPE], g[NOPE:][perm], jnp.zeros((HP - QK,), F32)])
    return g.reshape(1, HP)


def _block_diag(w):
    per = CH // LRU_BLOCK
    w = w.reshape(DR // CH, per, LRU_BLOCK, LRU_BLOCK)
    bd = jnp.einsum("jpab,pq->jpaqb", w, jnp.eye(per, dtype=w.dtype))
    return bd.reshape(DR // CH, CH, CH)


def _prepare_shared(l, p):
    w_in = p["w_in"][l]
    o1, o2, o3, o4, o5 = DR, 2 * DR, 2 * DR + QL, 2 * DR + QL + KVL, 2 * DR + QL + KVL + ROPE
    bd = jnp.stack([_block_diag(p["lru_wa"][l, 0]), _block_diag(p["lru_wx"][l, 0]),
                    _block_diag(p["lru_wa"][l, 1]), _block_diag(p["lru_wx"][l, 1])]).astype(BF)
    wom = p["w_o_mla"][l].reshape(NH, VD, D)
    wom = jnp.pad(wom, ((0, 0), (0, HP - VD), (0, 0))).reshape(DH, D)
    router_w = jnp.concatenate([p["router_we"][l], p["router_wg"][l],
                                jnp.zeros((D, LANES - NE - NG), F32)], axis=1)
    router_b = jnp.concatenate([p["router_be"][l], p["router_bg"][l],
                                jnp.zeros((LANES - NE - NG,), F32)]).reshape(1, LANES)
    router_hi = router_w.astype(BF)
    router_lo = (router_w - router_hi.astype(F32)).astype(BF)
    router_w = jnp.concatenate([router_hi, router_lo], axis=1)
    return {
        "n1": p["norm1_g"][l].reshape(1, D), "n2": p["norm2_g"][l].reshape(1, D),
        "w_x": w_in[:, :o1].astype(BF), "w_g": w_in[:, o1:o2].astype(BF),
        "w_q": w_in[:, o2:o3].astype(BF), "w_kv": w_in[:, o3:o4], "w_kr": w_in[:, o4:o5],
        "w_gate": w_in[:, o5:].astype(BF),
        "qan": p["q_a_norm"][l].reshape(1, QL), "kvan": p["kv_a_norm"][l].reshape(1, KVL),
        "w_uk": _pad_heads(p["w_uk"][l], None, None).astype(BF),
        "w_uv": _pad_heads(p["w_uv"][l], None, None).astype(BF),
        "conv_w": p["conv_w"][l], "conv_b": p["conv_b"][l].reshape(1, DR), "bd": bd,
        "lru_ba": p["lru_ba"][l], "lru_bx": p["lru_bx"][l], "lru_lam": p["lru_lam"][l],
        "w_o_rnn": p["w_o_rnn"][l].astype(BF), "w_o_mla": wom.astype(BF), "w_out": p["w_out"][l].astype(BF),
        "router_w": router_w, "router_b": router_b,
        "exp_w1": p["exp_w1"][l].astype(BF), "exp_w3": p["exp_w3"][l].astype(BF),
        "exp_w2": p["exp_w2"][l].astype(BF),
    }


def _with_rope_order(l, p, shared, perm):
    w = dict(shared)
    zeros = lambda n: jnp.zeros((D, n), F32)
    w["w_kvr"] = jnp.concatenate([shared["w_kv"], zeros(NOPE), shared["w_kr"][:, perm],
                                  zeros(HP - QK)], axis=1).astype(BF)
    w["w_uq"] = _pad_heads(p["w_uq"][l], None, perm).astype(BF)
    w["gq"] = _pad_gain(p["q_norm"][l], perm)
    w["gk"] = _pad_gain(p["k_norm"][l], perm)
    return w


def _rope_tables(n_tokens, perm):
    rows = n_tokens // GRID_W
    row = jnp.repeat(jnp.arange(rows), GRID_W).astype(F32)
    col = jnp.tile(jnp.arange(GRID_W), rows).astype(F32)
    axis_dim = ROPE // 2
    inv = ROPE_BASE ** (-jnp.arange(0, axis_dim, 2, dtype=F32) / axis_dim)
    ang = jnp.concatenate([row[:, None] * inv, col[:, None] * inv], axis=-1)
    cos, sin = jnp.cos(ang), jnp.sin(ang)
    ones = lambda n: jnp.ones((n_tokens, n), F32)
    zeros = lambda n: jnp.zeros((n_tokens, n), F32)
    cos_t = jnp.concatenate([ones(NOPE), cos, cos, ones(HP - QK)], axis=1)
    sin_t = jnp.concatenate([zeros(NOPE), -sin, sin, zeros(HP - QK)], axis=1)
    return cos_t, sin_t


def kernel(x_prompt, x_sample, cache_mla_ckv, cache_mla_krope, state_rglru, c, c_ctx, norm1_g, norm2_g, w_mod, b_mod, w_in, conv_w, conv_b, lru_wa, lru_ba, lru_wx, lru_bx, lru_lam, q_a_norm, kv_a_norm, w_uq, w_uk, w_uv, q_norm, k_norm, w_o_rnn, w_o_mla, w_out, router_wg, router_bg, router_we, router_be, exp_w1, exp_w3, exp_w2):
    p = dict(norm1_g=norm1_g, norm2_g=norm2_g, w_in=w_in, conv_w=conv_w, conv_b=conv_b,
             lru_wa=lru_wa, lru_ba=lru_ba, lru_wx=lru_wx, lru_bx=lru_bx, lru_lam=lru_lam,
             q_a_norm=q_a_norm, kv_a_norm=kv_a_norm, w_uq=w_uq, w_uk=w_uk, w_uv=w_uv,
             q_norm=q_norm, k_norm=k_norm, w_o_rnn=w_o_rnn, w_o_mla=w_o_mla, w_out=w_out,
             router_wg=router_wg, router_bg=router_bg, router_we=router_we, router_be=router_be,
             exp_w1=exp_w1, exp_w3=exp_w3, exp_w2=exp_w2)
    depth = w_in.shape[0]
    nb, seq, _ = x_prompt.shape
    db, dseq, _ = x_sample.shape
    ident = np.arange(ROPE)
    halves = np.concatenate([np.arange(0, ROPE, 2), np.arange(1, ROPE, 2)])
    rope_tabs = _rope_tables(dseq, halves)
    cond8 = jnp.concatenate([c_ctx[None, :], c, jnp.zeros((SUBLANES - 1 - db, D), F32)], axis=0)
    ctx_row = lambda tile_rows: (lambda i: 0)
    lat_row = lambda tile_rows: (lambda i: (i * tile_rows) // dseq + 1)
    n_ctx, n_lat = nb * seq, db * dseq
    max_tiles = (n_ctx + n_lat) // TMOE + NG
    per_seq = lambda arrs, b, t: [a.reshape(b, t, a.shape[-1]) for a in arrs]
    flat = lambda a: a.reshape(-1, a.shape[-1])

    y_prompt, y_sample = x_prompt.reshape(n_ctx, D), x_sample.reshape(n_lat, D)
    ckv_list, krope_list, rnn_list = [], [], []
    for l in range(depth):
        shared = _prepare_shared(l, p)
        w_ctx = _with_rope_order(l, p, shared, ident)
        w_lat = _with_rope_order(l, p, shared, halves)
        mod = _modulation(cond8, w_mod[l], b_mod[l]).reshape(SUBLANES, 6, D)

        h, xr, gg, q, k, v, ckv, kro = _projections(y_prompt, mod, ctx_row(TM), w_ctx, None, True)
        xr, gg, q, k, v = per_seq([xr, gg, q, k, v], nb, seq)
        yr, h_fin = _rglru(xr, gg, shared, None, True)
        ya = _attention(q, k, v, None, None, NH)
        x1_c, hx_c, ri_c, cnt_c = _merge_out(y_prompt, h, flat(yr), flat(ya), mod, ctx_row(TM), shared,
                                             jnp.zeros((SUBLANES, LANES), F32))
        ckv_list.append(ckv.reshape(nb, seq, KVL))
        krope_list.append(kro.reshape(nb, seq, ROPE))
        rnn_list.append(h_fin)

        krp_cache = jnp.pad(cache_mla_krope[:, l][..., halves], ((0, 0), (0, 0), (NOPE, HP - QK)))
        kc, vc = _cache_keys_values(cache_mla_ckv[:, l], krp_cache, w_lat)
        h, xr, gg, q, k, v = _projections(y_sample, mod, lat_row(TM), w_lat, rope_tabs, False)
        xr, gg, q, k, v = per_seq([xr, gg, q, k, v], db, dseq)
        yr, _ = _rglru(xr, gg, shared, state_rglru[:, l], False)
        ya = _attention(q, k, v, kc, vc, 1)
        x1_l, hx_l, ri_l, cnt_l = _merge_out(y_sample, h, flat(yr), flat(ya), mod, lat_row(TM), shared, cnt_c)

        offsets, tile_block, tile_group, n_tiles = _group_layout(cnt_l[0, :NG].astype(jnp.int32), max_tiles)
        pos_c, pos_l = _slots(ri_c, offsets), _slots(ri_l, offsets)
        xs = jnp.zeros((max_tiles * TMOE, XW), F32)
        xs = _dispatch(pos_c, hx_c, xs)
        xs = _dispatch(pos_l, hx_l, xs)
        ys = _experts(tile_block, tile_group, n_tiles, xs, shared)
        y_prompt = _combine(pos_c, x1_c, mod, ctx_row(TD), ys)
        y_sample = _combine(pos_l, x1_l, mod, lat_row(TD), ys)

    y_prompt, y_sample = y_prompt.reshape(nb, seq, D), y_sample.reshape(db, dseq, D)

    return (y_prompt, y_sample, jnp.stack(ckv_list, axis=1), jnp.stack(krope_list, axis=1),
            jnp.stack(rnn_list, axis=1))
```

```python
import functools
import math

import numpy as np
import jax
import jax.numpy as jnp
from jax import lax
from jax.experimental import pallas as pl
from jax.experimental.pallas import tpu as pltpu

D = 1024
DR = 1024
QL = 384
KVL = 256
NH = 8
NOPE = 64
ROPE = 32
QK = NOPE + ROPE
VD = 64
HP = 128
DH = NH * HP
GRID_W = 64
ROPE_BASE = 10000.0
EPS = 1e-6
LRU_C = 8.0
LRU_BLOCK = 64
CH = 256
NLT = CH // 128
NG = 4
EPG = 4
NE = NG * EPG
DE = 512
LANES = 128
SUBLANES = 8
TM = 512
TMOE = 512
TD = 512
XW = D + 128
VMEM_LIMIT = 52 * 1024 * 1024
BF = jnp.bfloat16
F32 = jnp.float32


def _cparams(sem):
    return pltpu.CompilerParams(dimension_semantics=sem, vmem_limit_bytes=VMEM_LIMIT)


def _dot(a, b):
    return jnp.dot(a, b, preferred_element_type=F32)


def _dot_nt(a, b):
    return lax.dot_general(a, b, (((1,), (1,)), ((), ())), preferred_element_type=F32)


def _rms(x, g, width):
    ms = jnp.sum(x * x, axis=-1, keepdims=True) * (1.0 / width)
    return x * lax.rsqrt(ms + EPS) * g


def _modulated_norm(x, g, scale, shift):
    return _rms(x, g, D) * (1.0 + scale) + shift


def _mod_kernel(c_ref, w_ref, b_ref, o_ref):
    c = c_ref[...]
    s = c * jax.nn.sigmoid(c)
    o_ref[...] = _dot(s, w_ref[...]) + b_ref[...]


def _modulation(cond8, w_mod, b_mod):
    n = w_mod.shape[1]
    return pl.pallas_call(
        _mod_kernel,
        grid=(n // D,),
        in_specs=[
            pl.BlockSpec((SUBLANES, D), lambda j: (0, 0)),
            pl.BlockSpec((D, D), lambda j: (0, j)),
            pl.BlockSpec((1, D), lambda j: (0, j)),
        ],
        out_specs=pl.BlockSpec((SUBLANES, D), lambda j: (0, j)),
        out_shape=jax.ShapeDtypeStruct((SUBLANES, n), F32),
        compiler_params=_cparams(("arbitrary",)),
        name="modulation",
    )(cond8, w_mod, b_mod.reshape(1, n))


def _head_norm(xh, gain, cos=None, partner_scaled=None):
    ms = jnp.sum(xh * xh, axis=-1, keepdims=True) * (1.0 / QK)
    rs = lax.rsqrt(ms + EPS)
    y = xh * rs * gain
    if cos is None:
        return y
    return y * cos + partner_scaled * rs


def _keys_values(ckv, krp, wuk_ref, wuv_ref, gk, cos, partner_scaled, k_ref, v_ref):
    cb = ckv.astype(BF)
    kn = _dot(cb, wuk_ref[...])
    v_ref[...] = _dot(cb, wuv_ref[...]).astype(BF)
    for h in range(NH):
        kh = kn[:, h * HP:(h + 1) * HP] + krp
        k_ref[:, h * HP:(h + 1) * HP] = _head_norm(kh, gk, cos, partner_scaled).astype(BF)


def _proj_kernel(*refs, rope, emit_cache):
    it = iter(refs)
    x_ref, mod_ref, n1_ref = next(it), next(it), next(it)
    wx_ref, wg_ref, wq_ref, wkvr_ref = next(it), next(it), next(it), next(it)
    qan_ref, kvan_ref, wuq_ref, gq_ref = next(it), next(it), next(it), next(it)
    wuk_ref, gk_ref, wuv_ref = next(it), next(it), next(it)
    if rope:
        wuqs_ref, gqs_ref, gks_ref, cos_ref, sins_ref = (next(it) for _ in range(5))
    xr_ref, gg_ref, q_ref, k_ref, v_ref = (next(it) for _ in range(5))
    if emit_cache:
        ckv_ref, kro_ref = next(it), next(it)

    hb = _modulated_norm(x_ref[...], n1_ref[...], mod_ref[0, 1:2, :], mod_ref[0, 0:1, :]).astype(BF)
    xr_ref[...] = _dot(hb, wx_ref[...]).astype(BF)
    gg_ref[...] = jax.nn.gelu(_dot(hb, wg_ref[...])).astype(BF)

    qnb = _rms(_dot(hb, wq_ref[...]), qan_ref[...], QL).astype(BF)
    q = _dot(qnb, wuq_ref[...])
    gq = gq_ref[...]
    cos = q_partner = q_pair_scale = None
    if rope:
        cos, sins = cos_ref[...], sins_ref[...]
        q_partner = _dot(qnb, wuqs_ref[...])
        q_pair_scale = gqs_ref[...] * sins
    for hd in range(NH):
        cols = slice(hd * HP, (hd + 1) * HP)
        partner = q_partner[:, cols] * q_pair_scale if rope else None
        q_ref[:, cols] = _head_norm(q[:, cols], gq, cos, partner).astype(BF)

    kvr = _dot(hb, wkvr_ref[...])
    ckv = _rms(kvr[:, :KVL], kvan_ref[...], KVL)
    krp = kvr[:, KVL:KVL + HP]
    k_partner = kvr[:, KVL + HP:KVL + 2 * HP] * (gks_ref[...] * sins) if rope else None
    if emit_cache:
        ckv_ref[...] = ckv
        kro_ref[...] = krp[:, NOPE:NOPE + ROPE]
    _keys_values(ckv, krp, wuk_ref, wuv_ref, gk_ref[...], cos, k_partner, k_ref, v_ref)


def _resident(shape):
    return pl.BlockSpec(shape, lambda i: (0,) * len(shape), pipeline_mode=pl.Buffered(1))


def _projections(x, mod, mod_row, wts, rope_tabs, emit_cache):
    n = x.shape[0]
    rope = rope_tabs is not None
    tile = lambda w: pl.BlockSpec((TM, w), lambda i: (i, 0))
    in_specs = [
        tile(D),
        pl.BlockSpec((1, 6, D), lambda i: (mod_row(i), 0, 0)),
        _resident((1, D)),
        _resident((D, DR)), _resident((D, DR)), _resident((D, QL)), _resident(wts["w_kvr"].shape),
        _resident((1, QL)), _resident((1, KVL)), _resident((QL, DH)), _resident((1, HP)),
        _resident((KVL, DH)), _resident((1, HP)), _resident((KVL, DH)),
    ]
    args = [x, mod, wts["n1"], wts["w_x"], wts["w_g"], wts["w_q"], wts["w_kvr"],
            wts["qan"], wts["kvan"], wts["w_uq"], wts["gq"], wts["w_uk"], wts["gk"], wts["w_uv"]]
    if rope:
        tiles_per_seq = rope_tabs[0].shape[0] // TM
        in_specs += [_resident((QL, DH)), _resident((1, HP)), _resident((1, HP))]
        in_specs += [pl.BlockSpec((TM, HP), lambda i: (i % tiles_per_seq, 0))] * 2
        args += [wts["w_uq_pair"], wts["gq_pair"], wts["gk_pair"]] + list(rope_tabs)
    out_specs = [tile(DR), tile(DR), tile(DH), tile(DH), tile(DH)]
    out_shape = [jax.ShapeDtypeStruct((n, DR), BF),
                 jax.ShapeDtypeStruct((n, DR), BF), jax.ShapeDtypeStruct((n, DH), BF),
                 jax.ShapeDtypeStruct((n, DH), BF), jax.ShapeDtypeStruct((n, DH), BF)]
    if emit_cache:
        out_specs += [tile(KVL), tile(ROPE)]
        out_shape += [jax.ShapeDtypeStruct((n, KVL), F32), jax.ShapeDtypeStruct((n, ROPE), F32)]
    return pl.pallas_call(
        functools.partial(_proj_kernel, rope=rope, emit_cache=emit_cache),
        grid=(n // TM,),
        in_specs=in_specs,
        out_specs=out_specs,
        out_shape=out_shape,
        compiler_params=_cparams(("arbitrary",)),
        name="projections",
    )(*args)


def _cache_kv_kernel(ckv_ref, krp_ref, wuk_ref, gk_ref, wuv_ref, k_ref, v_ref):
    _keys_values(ckv_ref[0], krp_ref[0], wuk_ref, wuv_ref, gk_ref[...], None, None,
                 k_ref.at[0], v_ref.at[0])


def _cache_keys_values(ckv, krp, wts):
    b, s, _ = ckv.shape
    full = lambda shape: pl.BlockSpec(shape, lambda i: (0,) * len(shape))
    return pl.pallas_call(
        _cache_kv_kernel,
        grid=(b,),
        in_specs=[pl.BlockSpec((1, s, KVL), lambda i: (i, 0, 0)),
                  pl.BlockSpec((1, s, HP), lambda i: (i, 0, 0)),
                  full((KVL, DH)), full((1, HP)), full((KVL, DH))],
        out_specs=[pl.BlockSpec((1, s, DH), lambda i: (i, 0, 0))] * 2,
        out_shape=[jax.ShapeDtypeStruct((b, s, DH), BF)] * 2,
        compiler_params=_cparams(("arbitrary",)),
        name="cache_keys_values",
    )(ckv, krp, wts["w_uk"], wts["gk"], wts["w_uv"])


def _sigmoid(x):
    return 0.5 * jnp.tanh(0.5 * x) + 0.5


def _tile_scan(a, b, forward):
    row = lax.broadcasted_iota(jnp.int32, a.shape, 0)
    for k in (1, 2, 4):
        if forward:
            shift, valid = k, row >= k
        else:
            shift, valid = SUBLANES - k, row < SUBLANES - k
        a_prev = jnp.where(valid, pltpu.roll(a, shift, 0), 1.0)
        b_prev = jnp.where(valid, pltpu.roll(b, shift, 0), 0.0)
        b = a * b_prev + b
        a = a * a_prev
    return a, b


def _rglru_kernel(*refs, t, has_h0, emit_state):
    it = iter(refs)
    xr_ref, gg_ref, cw_ref, cb_ref, bd_ref, ba_ref, bx_ref, lam_ref = (next(it) for _ in range(8))
    h0_ref = next(it) if has_h0 else None
    y_ref = next(it)
    hf_ref = next(it) if emit_state else None
    a_scr, b_scr, h_scr = (next(it) for _ in range(3))

    n_tiles = t // SUBLANES
    x = xr_ref[0].astype(F32)
    row = lax.broadcasted_iota(jnp.int32, x.shape, 0)
    xm1 = jnp.where(row >= 1, pltpu.roll(x, 1, 0), 0.0)
    xp1 = jnp.where(row < t - 1, pltpu.roll(x, t - 1, 0), 0.0)
    xp2 = jnp.where(row < t - 2, pltpu.roll(x, t - 2, 0), 0.0)
    xc = (cb_ref[...] + xm1 * cw_ref[0:1, :] + x * cw_ref[1:2, :]
          + xp1 * cw_ref[2:3, :] + xp2 * cw_ref[3:4, :])
    xcb = xc.astype(BF)
    for d in range(2):
        r = _sigmoid(_dot(xcb, bd_ref[2 * d, 0]) + ba_ref[d:d + 1, :])
        gi = _sigmoid(_dot(xcb, bd_ref[2 * d + 1, 0]) + bx_ref[d:d + 1, :])
        nl = -lam_ref[d:d + 1, :]
        softplus = jnp.maximum(nl, 0.0) + jnp.log(1.0 + jnp.exp(-jnp.abs(nl)))
        a = jnp.exp(r * ((-LRU_C) * softplus))
        a_scr[d] = a
        b_scr[d] = jnp.sqrt(1.0 - a * a) * (gi * xc)

    def step(i, carry):
        out = []
        for d in range(2):
            tile = i if d == 0 else n_tiles - 1 - i
            rows = pl.ds(pl.multiple_of(tile * SUBLANES, SUBLANES), SUBLANES)
            for c in range(NLT):
                lanes = slice(c * LANES, (c + 1) * LANES)
                decay, local = _tile_scan(a_scr[d, rows, lanes], b_scr[d, rows, lanes], d == 0)
                h = local + decay * carry[d * NLT + c]
                h_scr[d, rows, lanes] = h
                last = h[SUBLANES - 1:SUBLANES, :] if d == 0 else h[0:1, :]
                out.append(jnp.broadcast_to(last, (SUBLANES, LANES)))
        return tuple(out)

    init = []
    for d in range(2):
        for c in range(NLT):
            if has_h0:
                h0 = h0_ref[0, d:d + 1, c * LANES:(c + 1) * LANES]
                init.append(jnp.broadcast_to(h0, (SUBLANES, LANES)))
            else:
                init.append(jnp.zeros((SUBLANES, LANES), F32))
    final = lax.fori_loop(0, n_tiles, step, tuple(init), unroll=2)

    if emit_state:
        for d in range(2):
            for c in range(NLT):
                hf_ref[0, d:d + 1, c * LANES:(c + 1) * LANES] = final[d * NLT + c][0:1, :]
    y_ref[0] = ((h_scr[0] + h_scr[1]) * gg_ref[0].astype(F32)).astype(BF)


def _rglru(xr, gg, wts, h0, emit_state):
    b, t, _ = xr.shape
    nc = DR // CH
    has_h0 = h0 is not None
    chunk = lambda r: pl.BlockSpec((r, CH), lambda i, j: (0, j))
    seq = pl.BlockSpec((1, t, CH), lambda i, j: (i, 0, j))
    state = pl.BlockSpec((1, 2, CH), lambda i, j: (i, 0, j))
    in_specs = [seq, seq, chunk(4), chunk(1),
                pl.BlockSpec((4, 1, CH, CH), lambda i, j: (0, j, 0, 0)),
                chunk(2), chunk(2), chunk(2)]
    args = [xr, gg, wts["conv_w"], wts["conv_b"], wts["bd"], wts["lru_ba"], wts["lru_bx"], wts["lru_lam"]]
    if has_h0:
        in_specs.append(state)
        args.append(h0)
    out_specs = [seq]
    out_shape = [jax.ShapeDtypeStruct((b, t, DR), BF)]
    if emit_state:
        out_specs.append(state)
        out_shape.append(jax.ShapeDtypeStruct((b, 2, DR), F32))
    res = pl.pallas_call(
        functools.partial(_rglru_kernel, t=t, has_h0=has_h0, emit_state=emit_state),
        grid=(b, nc),
        in_specs=in_specs,
        out_specs=out_specs,
        out_shape=out_shape,
        scratch_shapes=[pltpu.VMEM((2, t, CH), F32)] * 3,
        compiler_params=_cparams(("arbitrary", "arbitrary")),
        name="rglru",
    )(*args)
    return res if emit_state else (res[0], None)


def _attn_kernel(*refs, t, n_heads, has_ctx, q_block):
    it = iter(refs)
    q_ref, k_ref, v_ref = next(it), next(it), next(it)
    kc_ref = vc_ref = None
    if has_ctx:
        kc_ref, vc_ref = next(it), next(it)
    o_ref = next(it)
    scale = QK ** -0.5
    for hd in range(n_heads):
        cols = slice(hd * HP, (hd + 1) * HP)
        k = k_ref[0, :, cols]
        v = v_ref[0, :, cols]
        if has_ctx:
            kc = kc_ref[0, :, cols]
            vc = vc_ref[0, :, cols]
        for qb in range(t // q_block):
            rows = slice(qb * q_block, (qb + 1) * q_block)
            q = q_ref[0, rows, cols]
            s = _dot_nt(q, k) * scale
            m = jnp.max(s, axis=-1, keepdims=True)
            if has_ctx:
                sc = _dot_nt(q, kc) * scale
                m = jnp.maximum(m, jnp.max(sc, axis=-1, keepdims=True))
            p = jnp.exp(s - m)
            den = jnp.sum(p, axis=-1, keepdims=True)
            o = _dot(p.astype(BF), v)
            if has_ctx:
                pc = jnp.exp(sc - m)
                den = den + jnp.sum(pc, axis=-1, keepdims=True)
                o = o + _dot(pc.astype(BF), vc)
            o_ref[0, rows, cols] = (o / den).astype(BF)


def _attention(q, k, v, kc, vc, heads_per_step):
    b, t, _ = q.shape
    has_ctx = kc is not None
    w = heads_per_step * HP
    blk = lambda n: pl.BlockSpec((1, n, w), lambda i, j: (i, 0, j))
    in_specs = [blk(t), blk(t), blk(t)]
    args = [q, k, v]
    if has_ctx:
        in_specs += [blk(kc.shape[1])] * 2
        args += [kc, vc]
    return pl.pallas_call(
        functools.partial(_attn_kernel, t=t, n_heads=heads_per_step, has_ctx=has_ctx, q_block=min(t, 256)),
        grid=(b, NH // heads_per_step),
        in_specs=in_specs,
        out_specs=blk(t),
        out_shape=jax.ShapeDtypeStruct((b, t, DH), BF),
        compiler_params=_cparams(("arbitrary", "arbitrary")),
        name="attention",
    )(*args)


def _route(logits):
    lane = lax.broadcasted_iota(jnp.int32, logits.shape, 1)
    lanef = lane.astype(F32)
    neg = -jnp.inf
    big = float(LANES)
    gl = jnp.where((lane >= NE) & (lane < NE + NG), logits, neg)
    gmax = jnp.max(gl, axis=-1, keepdims=True)
    gidx = jnp.min(jnp.where(gl == gmax, lanef, big), axis=-1, keepdims=True) - float(NE)
    gw = 1.0 / jnp.sum(jnp.exp(gl - gmax), axis=-1, keepdims=True)
    lo = gidx * float(EPG)
    el = jnp.where((lanef >= lo) & (lanef < lo + float(EPG)), logits, neg)
    v1 = jnp.max(el, axis=-1, keepdims=True)
    i1 = jnp.min(jnp.where(el == v1, lanef, big), axis=-1, keepdims=True)
    el2 = jnp.where(lanef == i1, neg, el)
    v2 = jnp.max(el2, axis=-1, keepdims=True)
    i2 = jnp.min(jnp.where(el2 == v2, lanef, big), axis=-1, keepdims=True)
    e2 = jnp.exp(v2 - v1)
    w1 = gw / (1.0 + e2)
    w2 = gw * e2 / (1.0 + e2)
    cmb = jnp.where(lanef == i1, w1, 0.0) + jnp.where(lanef == i2, w2, 0.0)
    return cmb, gidx


def _out_kernel(x_ref, yr_ref, ya_ref, mod_ref, n1_ref, wgate_ref, wor_ref, wom_ref, wout_ref,
                n2_ref, rw_ref, rb_ref, cnt0_ref, x1_ref, hx_ref, rinfo_ref, cnt_ref, run_ref):
    @pl.when(pl.program_id(0) == 0)
    def _():
        run_ref[...] = cnt0_ref[...]

    x = x_ref[...]
    hb = _modulated_norm(x, n1_ref[...], mod_ref[0, 1:2, :], mod_ref[0, 0:1, :]).astype(BF)
    gl = _dot(hb, wgate_ref[...])
    merged = (_sigmoid(gl[:, :D]) * _dot(yr_ref[...], wor_ref[...])
              + _sigmoid(gl[:, D:]) * _dot(ya_ref[...], wom_ref[...]))
    mix = _dot(merged.astype(BF), wout_ref[...])
    x1 = x + mod_ref[0, 2:3, :] * mix
    x1_ref[...] = x1
    h2 = _modulated_norm(x1, n2_ref[...], mod_ref[0, 4:5, :], mod_ref[0, 3:4, :])
    h2_hi = h2.astype(BF)
    h2_lo = (h2 - h2_hi.astype(F32)).astype(BF)
    part = _dot(h2_hi, rw_ref[...])
    logits = part[:, :LANES] + part[:, LANES:] + _dot(h2_lo, rw_ref[:, :LANES]) + rb_ref[...]
    cmb, gidx = _route(logits)
    hx_ref[:, :D] = h2
    hx_ref[:, D:] = cmb

    lanef = lax.broadcasted_iota(jnp.int32, cmb.shape, 1).astype(F32)
    ghot = jnp.where(lanef == gidx, 1.0, 0.0)
    r_i = lax.broadcasted_iota(jnp.int32, (TM, TM), 0)
    c_i = lax.broadcasted_iota(jnp.int32, (TM, TM), 1)
    tri = jnp.where(r_i > c_i, 1.0, 0.0).astype(BF)
    before = _dot(tri, ghot.astype(BF)) + run_ref[0:1, :]
    rank = jnp.sum(before * ghot, axis=-1, keepdims=True)
    rinfo_ref[...] = jnp.where(lanef == 0.0, gidx, jnp.where(lanef == 1.0, rank, 0.0))
    run = run_ref[...] + jnp.sum(ghot, axis=0, keepdims=True)
    run_ref[...] = run
    cnt_ref[...] = run


def _merge_out(x, yr, ya, mod, mod_row, wts, cnt0):
    n = x.shape[0]
    tile = lambda w: pl.BlockSpec((TM, w), lambda i: (i, 0))
    return pl.pallas_call(
        _out_kernel,
        grid=(n // TM,),
        in_specs=[tile(D), tile(DR), tile(DH),
                  pl.BlockSpec((1, 6, D), lambda i: (mod_row(i), 0, 0)), _resident((1, D)),
                  _resident((D, 2 * D)), _resident((DR, D)), _resident((DH, D)), _resident((D, D)),
                  _resident((1, D)), _resident((D, 2 * LANES)), _resident((1, LANES)),
                  _resident((SUBLANES, LANES))],
        out_specs=[tile(D), tile(XW), tile(LANES), pl.BlockSpec((SUBLANES, LANES), lambda i: (0, 0))],
        out_shape=[jax.ShapeDtypeStruct((n, D), F32), jax.ShapeDtypeStruct((n, XW), F32),
                   jax.ShapeDtypeStruct((n, LANES), F32),
                   jax.ShapeDtypeStruct((SUBLANES, LANES), F32)],
        scratch_shapes=[pltpu.VMEM((SUBLANES, LANES), F32)],
        compiler_params=_cparams(("arbitrary",)),
        name="merge_out",
    )(x, yr, ya, mod, wts["n1"], wts["w_gate"], wts["w_o_rnn"], wts["w_o_mla"], wts["w_out"],
      wts["n2"], wts["router_w"], wts["router_b"], cnt0)


def _row_copy(src, src_row, dst, dst_row, sem):
    return pltpu.make_async_copy(src.at[pl.ds(src_row, 1)], dst.at[pl.ds(dst_row, 1)], sem)


def _dispatch_kernel(pos_ref, hx_ref, xs_in_ref, xs_ref, sem, *, n_steps):
    del xs_in_ref
    i = pl.program_id(0)
    base = i * TD

    def body(r, carry):
        _row_copy(hx_ref, base + r, xs_ref, pos_ref[base + r], sem).start()
        return carry

    lax.fori_loop(0, TD, body, 0, unroll=8)

    def wait_rows():
        pltpu.make_async_copy(hx_ref.at[pl.ds(0, TD)], xs_ref.at[pl.ds(0, TD)], sem).wait()

    @pl.when(i > 0)
    def _():
        wait_rows()

    @pl.when(i == n_steps - 1)
    def _():
        wait_rows()


def _dispatch(pos, hx, xs):
    n = hx.shape[0]
    n_steps = n // TD
    return pl.pallas_call(
        functools.partial(_dispatch_kernel, n_steps=n_steps),
        grid_spec=pltpu.PrefetchScalarGridSpec(
            num_scalar_prefetch=1,
            grid=(n_steps,),
            in_specs=[pl.BlockSpec(memory_space=pl.ANY),
                      pl.BlockSpec(memory_space=pl.ANY)],
            out_specs=pl.BlockSpec(memory_space=pl.ANY),
            scratch_shapes=[pltpu.SemaphoreType.DMA(())],
        ),
        out_shape=jax.ShapeDtypeStruct(xs.shape, xs.dtype),
        input_output_aliases={2: 0},
        compiler_params=_cparams(("arbitrary",)),
        name="dispatch",
    )(pos, hx, xs)


def _moe_kernel(tb_ref, tg_ref, nt_ref, xs_ref, w1_ref, w3_ref, w2_ref, o_ref, acc_ref):
    j = pl.program_id(0)
    e = pl.program_id(1)

    @pl.when(j < nt_ref[0])
    def _():
        @pl.when(e == 0)
        def _():
            acc_ref[...] = jnp.zeros_like(acc_ref)

        xt = xs_ref[:, :D].astype(BF)
        a = _dot(xt, w1_ref[0])
        he = (a * jax.nn.sigmoid(a)) * _dot(xt, w3_ref[0])
        y = _dot(he.astype(BF), w2_ref[0])
        cmb = xs_ref[:, D:]
        lane = lax.broadcasted_iota(jnp.int32, cmb.shape, 1)
        ce = jnp.sum(jnp.where(lane == tg_ref[j] * EPG + e, cmb, 0.0), axis=-1, keepdims=True)
        acc_ref[...] += ce * y

        @pl.when(e == EPG - 1)
        def _():
            o_ref[...] = acc_ref[...]

    @pl.when((j >= nt_ref[0]) & (e == EPG - 1))
    def _():
        o_ref[...] = jnp.zeros_like(o_ref)


def _experts(tile_block, tile_group, n_tiles, xs, wts):
    m = xs.shape[0]

    def w_idx(j, e, tb, tg, nt):
        return (tg[j] * EPG + jnp.where(j < nt[0], e, EPG - 1), 0, 0)

    return pl.pallas_call(
        _moe_kernel,
        grid_spec=pltpu.PrefetchScalarGridSpec(
            num_scalar_prefetch=3,
            grid=(m // TMOE, EPG),
            in_specs=[pl.BlockSpec((TMOE, XW), lambda j, e, tb, tg, nt: (tb[j], 0)),
                      pl.BlockSpec((1, D, DE), w_idx),
                      pl.BlockSpec((1, D, DE), w_idx),
                      pl.BlockSpec((1, DE, D), w_idx)],
            out_specs=pl.BlockSpec((TMOE, D), lambda j, e, tb, tg, nt: (j, 0)),
            scratch_shapes=[pltpu.VMEM((TMOE, D), F32)],
        ),
        out_shape=jax.ShapeDtypeStruct((m, D), F32),
        compiler_params=_cparams(("arbitrary", "arbitrary")),
        name="experts",
    )(tile_block, tile_group, n_tiles, xs, wts["exp_w1"], wts["exp_w3"], wts["exp_w2"])


def _combine_kernel(pos_ref, x1_ref, mod_ref, ys_ref, o_ref, buf_ref, sem, *, n_steps):
    i = pl.program_id(0)

    def issue(step, slot):
        def body(r, carry):
            _row_copy(ys_ref, pos_ref[step * TD + r], buf_ref.at[slot], r, sem.at[slot]).start()
            return carry

        lax.fori_loop(0, TD, body, 0, unroll=8)

    @pl.when(i == 0)
    def _():
        issue(0, 0)

    @pl.when(i + 1 < n_steps)
    def _():
        issue(i + 1, (i + 1) % 2)

    slot = i % 2
    pltpu.make_async_copy(ys_ref.at[pl.ds(0, TD)], buf_ref.at[slot], sem.at[slot]).wait()
    o_ref[...] = x1_ref[...] + mod_ref[0, 5:6, :] * buf_ref[slot]


def _combine(pos, x1, mod, mod_row, ys):
    n = x1.shape[0]
    n_steps = n // TD
    return pl.pallas_call(
        functools.partial(_combine_kernel, n_steps=n_steps),
        grid_spec=pltpu.PrefetchScalarGridSpec(
            num_scalar_prefetch=1,
            grid=(n_steps,),
            in_specs=[pl.BlockSpec((TD, D), lambda i, pos: (i, 0)),
                      pl.BlockSpec((1, 6, D), lambda i, pos: (mod_row(i), 0, 0)),
                      pl.BlockSpec(memory_space=pl.ANY)],
            out_specs=pl.BlockSpec((TD, D), lambda i, pos: (i, 0)),
            scratch_shapes=[pltpu.VMEM((2, TD, D), F32), pltpu.SemaphoreType.DMA((2,))],
        ),
        out_shape=jax.ShapeDtypeStruct((n, D), F32),
        compiler_params=_cparams(("arbitrary",)),
        name="combine",
    )(pos, x1, mod, ys)


def _slots(rinfo, offsets):
    group = rinfo[..., 0:1]
    off = jnp.sum(jnp.where(group == jnp.arange(NG, dtype=F32), offsets.astype(F32), 0.0), axis=-1)
    return (off + rinfo[..., 1]).astype(jnp.int32).reshape(-1)


def _group_layout(counts, max_tiles):
    padded = ((counts + TMOE - 1) // TMOE) * TMOE
    ends = jnp.cumsum(padded)
    offsets = ends - padded
    n_tiles = (ends[-1] // TMOE).astype(jnp.int32)
    tile = jnp.minimum(jnp.arange(max_tiles, dtype=jnp.int32), jnp.maximum(n_tiles - 1, 0))
    tile_group = jnp.sum((tile[:, None] * TMOE >= ends[None, :]).astype(jnp.int32), axis=1)
    return offsets, tile, tile_group, n_tiles.reshape(1)


def _pad_heads(w, perm=None, rotary_only=False):
    lead = w.shape[:-1]
    per = w.shape[-1] // NH
    w = w.reshape(lead + (NH, per))
    if perm is not None:
        nope = jnp.zeros_like(w[..., :NOPE]) if rotary_only else w[..., :NOPE]
        w = jnp.concatenate([nope, w[..., NOPE:][..., perm]], axis=-1)
    w = jnp.pad(w, [(0, 0)] * len(lead) + [(0, 0), (0, HP - per)])
    return w.reshape(lead + (NH * HP,))


def _pad_gain(g, perm, rotary_only=False):
    nope = jnp.zeros((NOPE,), F32) if rotary_only else g[:NOPE]
    g = jnp.concatenate([nope, g[NOPE:][perm], jnp.zeros((HP - QK,), F32)])
    return g.reshape(1, HP)


def _block_diag(w):
    per = CH // LRU_BLOCK
    w = w.reshape(DR // CH, per, LRU_BLOCK, LRU_BLOCK)
    bd = jnp.einsum("jpab,pq->jpaqb", w, jnp.eye(per, dtype=w.dtype))
    return bd.reshape(DR // CH, CH, CH)


def _prepare_shared(l, p):
    w_in = p["w_in"][l]
    o1, o2, o3, o4, o5 = DR, 2 * DR, 2 * DR + QL, 2 * DR + QL + KVL, 2 * DR + QL + KVL + ROPE
    bd = jnp.stack([_block_diag(p["lru_wa"][l, 0]), _block_diag(p["lru_wx"][l, 0]),
                    _block_diag(p["lru_wa"][l, 1]), _block_diag(p["lru_wx"][l, 1])]).astype(BF)
    wom = p["w_o_mla"][l].reshape(NH, VD, D)
    wom = jnp.pad(wom, ((0, 0), (0, HP - VD), (0, 0))).reshape(DH, D)
    router_w = jnp.concatenate([p["router_we"][l], p["router_wg"][l],
                                jnp.zeros((D, LANES - NE - NG), F32)], axis=1)
    router_b = jnp.concatenate([p["router_be"][l], p["router_bg"][l],
                                jnp.zeros((LANES - NE - NG,), F32)]).reshape(1, LANES)
    router_hi = router_w.astype(BF)
    router_lo = (router_w - router_hi.astype(F32)).astype(BF)
    router_w = jnp.concatenate([router_hi, router_lo], axis=1)
    return {
        "n1": p["norm1_g"][l].reshape(1, D), "n2": p["norm2_g"][l].reshape(1, D),
        "w_x": w_in[:, :o1].astype(BF), "w_g": w_in[:, o1:o2].astype(BF),
        "w_q": w_in[:, o2:o3].astype(BF), "w_kv": w_in[:, o3:o4], "w_kr": w_in[:, o4:o5],
        "w_gate": w_in[:, o5:].astype(BF),
        "qan": p["q_a_norm"][l].reshape(1, QL), "kvan": p["kv_a_norm"][l].reshape(1, KVL),
        "w_uk": _pad_heads(p["w_uk"][l]).astype(BF),
        "w_uv": _pad_heads(p["w_uv"][l]).astype(BF),
        "conv_w": p["conv_w"][l], "conv_b": p["conv_b"][l].reshape(1, DR), "bd": bd,
        "lru_ba": p["lru_ba"][l], "lru_bx": p["lru_bx"][l], "lru_lam": p["lru_lam"][l],
        "w_o_rnn": p["w_o_rnn"][l].astype(BF), "w_o_mla": wom.astype(BF), "w_out": p["w_out"][l].astype(BF),
        "router_w": router_w, "router_b": router_b,
        "exp_w1": p["exp_w1"][l].astype(BF), "exp_w3": p["exp_w3"][l].astype(BF),
        "exp_w2": p["exp_w2"][l].astype(BF),
    }


def _with_rope_order(l, p, shared, perm, rotary):
    w = dict(shared)
    zeros = lambda n: jnp.zeros((D, n), F32)
    rope_block = lambda order: [zeros(NOPE), shared["w_kr"][:, order], zeros(HP - QK)]
    kvr = [shared["w_kv"]] + rope_block(perm)
    w["w_uq"] = _pad_heads(p["w_uq"][l], perm).astype(BF)
    w["gq"] = _pad_gain(p["q_norm"][l], perm)
    w["gk"] = _pad_gain(p["k_norm"][l], perm)
    if rotary:
        pair = np.concatenate([perm[ROPE // 2:], perm[:ROPE // 2]])
        kvr += rope_block(pair)
        w["w_uq_pair"] = _pad_heads(p["w_uq"][l], pair, rotary_only=True).astype(BF)
        w["gq_pair"] = _pad_gain(p["q_norm"][l], pair, rotary_only=True)
        w["gk_pair"] = _pad_gain(p["k_norm"][l], pair, rotary_only=True)
    w["w_kvr"] = jnp.concatenate(kvr, axis=1).astype(BF)
    return w


def _rope_tables(n_tokens, perm):
    rows = n_tokens // GRID_W
    row = jnp.repeat(jnp.arange(rows), GRID_W).astype(F32)
    col = jnp.tile(jnp.arange(GRID_W), rows).astype(F32)
    axis_dim = ROPE // 2
    inv = ROPE_BASE ** (-jnp.arange(0, axis_dim, 2, dtype=F32) / axis_dim)
    ang = jnp.concatenate([row[:, None] * inv, col[:, None] * inv], axis=-1)
    cos, sin = jnp.cos(ang), jnp.sin(ang)
    ones = lambda n: jnp.ones((n_tokens, n), F32)
    zeros = lambda n: jnp.zeros((n_tokens, n), F32)
    cos_t = jnp.concatenate([ones(NOPE), cos, cos, ones(HP - QK)], axis=1)
    sin_t = jnp.concatenate([zeros(NOPE), -sin, sin, zeros(HP - QK)], axis=1)
    return cos_t, sin_t


def kernel(x_prompt, x_sample, cache_mla_ckv, cache_mla_krope, state_rglru, c, c_ctx, norm1_g, norm2_g, w_mod, b_mod, w_in, conv_w, conv_b, lru_wa, lru_ba, lru_wx, lru_bx, lru_lam, q_a_norm, kv_a_norm, w_uq, w_uk, w_uv, q_norm, k_norm, w_o_rnn, w_o_mla, w_out, router_wg, router_bg, router_we, router_be, exp_w1, exp_w3, exp_w2):
    p = dict(norm1_g=norm1_g, norm2_g=norm2_g, w_in=w_in, conv_w=conv_w, conv_b=conv_b,
             lru_wa=lru_wa, lru_ba=lru_ba, lru_wx=lru_wx, lru_bx=lru_bx, lru_lam=lru_lam,
             q_a_norm=q_a_norm, kv_a_norm=kv_a_norm, w_uq=w_uq, w_uk=w_uk, w_uv=w_uv,
             q_norm=q_norm, k_norm=k_norm, w_o_rnn=w_o_rnn, w_o_mla=w_o_mla, w_out=w_out,
             router_wg=router_wg, router_bg=router_bg, router_we=router_we, router_be=router_be,
             exp_w1=exp_w1, exp_w3=exp_w3, exp_w2=exp_w2)
    depth = w_in.shape[0]
    nb, seq, _ = x_prompt.shape
    db, dseq, _ = x_sample.shape
    ident = np.arange(ROPE)
    halves = np.concatenate([np.arange(0, ROPE, 2), np.arange(1, ROPE, 2)])
    rope_tabs = _rope_tables(dseq, halves)
    cond8 = jnp.concatenate([c_ctx[None, :], c, jnp.zeros((SUBLANES - 1 - db, D), F32)], axis=0)
    ctx_row = lambda tile_rows: (lambda i: 0)
    lat_row = lambda tile_rows: (lambda i: (i * tile_rows) // dseq + 1)
    n_ctx, n_lat = nb * seq, db * dseq
    max_tiles = (n_ctx + n_lat) // TMOE + NG
    per_seq = lambda arrs, b, t: [a.reshape(b, t, a.shape[-1]) for a in arrs]
    flat = lambda a: a.reshape(-1, a.shape[-1])

    y_prompt, y_sample = x_prompt.reshape(n_ctx, D), x_sample.reshape(n_lat, D)
    ckv_list, krope_list, rnn_list = [], [], []
    for l in range(depth):
        shared = _prepare_shared(l, p)
        w_ctx = _with_rope_order(l, p, shared, ident, False)
        w_lat = _with_rope_order(l, p, shared, halves, True)
        mod = _modulation(cond8, w_mod[l], b_mod[l]).reshape(SUBLANES, 6, D)

        xr, gg, q, k, v, ckv, kro = _projections(y_prompt, mod, ctx_row(TM), w_ctx, None, True)
        xr, gg, q, k, v = per_seq([xr, gg, q, k, v], nb, seq)
        yr, h_fin = _rglru(xr, gg, shared, None, True)
        ya = _attention(q, k, v, None, None, NH)
        x1_c, hx_c, ri_c, cnt_c = _merge_out(y_prompt, flat(yr), flat(ya), mod, ctx_row(TM), shared,
                                             jnp.zeros((SUBLANES, LANES), F32))
        ckv_list.append(ckv.reshape(nb, seq, KVL))
        krope_list.append(kro.reshape(nb, seq, ROPE))
        rnn_list.append(h_fin)

        krp_cache = jnp.pad(cache_mla_krope[:, l][..., halves], ((0, 0), (0, 0), (NOPE, HP - QK)))
        kc, vc = _cache_keys_values(cache_mla_ckv[:, l], krp_cache, w_lat)
        xr, gg, q, k, v = _projections(y_sample, mod, lat_row(TM), w_lat, rope_tabs, False)
        xr, gg, q, k, v = per_seq([xr, gg, q, k, v], db, dseq)
        yr, _ = _rglru(xr, gg, shared, state_rglru[:, l], False)
        ya = _attention(q, k, v, kc, vc, 1)
        x1_l, hx_l, ri_l, cnt_l = _merge_out(y_sample, flat(yr), flat(ya), mod, lat_row(TM), shared, cnt_c)

        offsets, tile_block, tile_group, n_tiles = _group_layout(cnt_l[0, :NG].astype(jnp.int32), max_tiles)
        pos_c, pos_l = _slots(ri_c, offsets), _slots(ri_l, offsets)
        xs = jnp.zeros((max_tiles * TMOE, XW), F32)
        xs = _dispatch(pos_c, hx_c, xs)
        xs = _dispatch(pos_l, hx_l, xs)
        ys = _experts(tile_block, tile_group, n_tiles, xs, shared)
        y_prompt = _combine(pos_c, x1_c, mod, ctx_row(TD), ys)
        y_sample = _combine(pos_l, x1_l, mod, lat_row(TD), ys)

    y_prompt, y_sample = y_prompt.reshape(nb, seq, D), y_sample.reshape(db, dseq, D)

    return (y_prompt, y_sample, jnp.stack(ckv_list, axis=1), jnp.stack(krope_list, axis=1),
            jnp.stack(rnn_list, axis=1))
```

```python
import functools
import math

import numpy as np
import jax
import jax.numpy as jnp
from jax import lax
from jax.experimental import pallas as pl
from jax.experimental.pallas import tpu as pltpu

D = 1024
DR = 1024
QL = 384
KVL = 256
NH = 8
NOPE = 64
ROPE = 32
QK = NOPE + ROPE
VD = 64
HP = 128
DH = NH * HP
GRID_W = 64
ROPE_BASE = 10000.0
EPS = 1e-6
LRU_C = 8.0
LRU_BLOCK = 64
BDW = 256
CH = 512
NLT = CH // 128
NG = 4
EPG = 4
NE = NG * EPG
DE = 512
LANES = 128
SUBLANES = 8
TM = 512
TMOE = 512
TD = 512
XW = D + 128
VMEM_LIMIT = 52 * 1024 * 1024
BF = jnp.bfloat16
F32 = jnp.float32


def _cparams(sem):
    return pltpu.CompilerParams(dimension_semantics=sem, vmem_limit_bytes=VMEM_LIMIT)


def _dot(a, b):
    return jnp.dot(a, b, preferred_element_type=F32)


def _dot_nt(a, b):
    return lax.dot_general(a, b, (((1,), (1,)), ((), ())), preferred_element_type=F32)


def _rms(x, g, width):
    ms = jnp.sum(x * x, axis=-1, keepdims=True) * (1.0 / width)
    return x * lax.rsqrt(ms + EPS) * g


def _modulated_norm(x, g, scale, shift):
    return _rms(x, g, D) * (1.0 + scale) + shift


def _mod_kernel(c_ref, w_ref, b_ref, o_ref):
    c = c_ref[...]
    s = c * jax.nn.sigmoid(c)
    o_ref[...] = _dot(s, w_ref[...]) + b_ref[...]


def _modulation(cond8, w_mod, b_mod):
    n = w_mod.shape[1]
    return pl.pallas_call(
        _mod_kernel,
        grid=(n // D,),
        in_specs=[
            pl.BlockSpec((SUBLANES, D), lambda j: (0, 0)),
            pl.BlockSpec((D, D), lambda j: (0, j)),
            pl.BlockSpec((1, D), lambda j: (0, j)),
        ],
        out_specs=pl.BlockSpec((SUBLANES, D), lambda j: (0, j)),
        out_shape=jax.ShapeDtypeStruct((SUBLANES, n), F32),
        compiler_params=_cparams(("arbitrary",)),
        name="modulation",
    )(cond8, w_mod, b_mod.reshape(1, n))


def _head_norm(xh, gain, cos=None, partner_scaled=None):
    ms = jnp.sum(xh * xh, axis=-1, keepdims=True) * (1.0 / QK)
    rs = lax.rsqrt(ms + EPS)
    y = xh * rs * gain
    if cos is None:
        return y
    return y * cos + partner_scaled * rs


def _keys_values(ckv, krp, wuk_ref, wuv_ref, gk, cos, partner_scaled, k_ref, v_ref):
    cb = ckv.astype(BF)
    kn = _dot(cb, wuk_ref[...])
    v = _dot(cb, wuv_ref[...])
    lane = lax.broadcasted_iota(jnp.int32, v.shape, 1)
    v_ref[...] = jnp.where((lane & (HP - 1)) == VD, 1.0, v).astype(BF)
    for h in range(NH):
        kh = kn[:, h * HP:(h + 1) * HP] + krp
        k_ref[:, h * HP:(h + 1) * HP] = _head_norm(kh, gk, cos, partner_scaled).astype(BF)


def _proj_kernel(*refs, rope, emit_cache):
    it = iter(refs)
    x_ref, mod_ref, n1_ref = next(it), next(it), next(it)
    wx_ref, wg_ref, wq_ref, wkvr_ref = next(it), next(it), next(it), next(it)
    qan_ref, kvan_ref, wuq_ref, gq_ref = next(it), next(it), next(it), next(it)
    wuk_ref, gk_ref, wuv_ref = next(it), next(it), next(it)
    if rope:
        wuqs_ref, gqs_ref, gks_ref, cos_ref, sins_ref = (next(it) for _ in range(5))
    xr_ref, gg_ref, q_ref, k_ref, v_ref = (next(it) for _ in range(5))
    if emit_cache:
        ckv_ref, kro_ref = next(it), next(it)

    hb = _modulated_norm(x_ref[...], n1_ref[...], mod_ref[0, 1:2, :], mod_ref[0, 0:1, :]).astype(BF)
    xr_ref[...] = _dot(hb, wx_ref[...]).astype(BF)
    gg_ref[...] = jax.nn.gelu(_dot(hb, wg_ref[...])).astype(BF)

    qnb = _rms(_dot(hb, wq_ref[...]), qan_ref[...], QL).astype(BF)
    q = _dot(qnb, wuq_ref[...])
    gq = gq_ref[...]
    cos = q_partner = q_pair_scale = None
    if rope:
        cos, sins = cos_ref[...], sins_ref[...]
        q_partner = _dot(qnb, wuqs_ref[...])
        q_pair_scale = gqs_ref[...] * sins
    for hd in range(NH):
        cols = slice(hd * HP, (hd + 1) * HP)
        partner = q_partner[:, cols] * q_pair_scale if rope else None
        q_ref[:, cols] = _head_norm(q[:, cols], gq, cos, partner).astype(BF)

    kvr = _dot(hb, wkvr_ref[...])
    ckv = _rms(kvr[:, :KVL], kvan_ref[...], KVL)
    krp = kvr[:, KVL:KVL + HP]
    k_partner = kvr[:, KVL + HP:KVL + 2 * HP] * (gks_ref[...] * sins) if rope else None
    if emit_cache:
        ckv_ref[...] = ckv
        kro_ref[...] = krp[:, NOPE:NOPE + ROPE]
    _keys_values(ckv, krp, wuk_ref, wuv_ref, gk_ref[...], cos, k_partner, k_ref, v_ref)


def _resident(shape):
    return pl.BlockSpec(shape, lambda i: (0,) * len(shape), pipeline_mode=pl.Buffered(1))


def _projections(x, mod, mod_row, wts, rope_tabs, emit_cache):
    n = x.shape[0]
    rope = rope_tabs is not None
    tile = lambda w: pl.BlockSpec((TM, w), lambda i: (i, 0))
    in_specs = [
        tile(D),
        pl.BlockSpec((1, 6, D), lambda i: (mod_row(i), 0, 0)),
        _resident((1, D)),
        _resident((D, DR)), _resident((D, DR)), _resident((D, QL)), _resident(wts["w_kvr"].shape),
        _resident((1, QL)), _resident((1, KVL)), _resident((QL, DH)), _resident((1, HP)),
        _resident((KVL, DH)), _resident((1, HP)), _resident((KVL, DH)),
    ]
    args = [x, mod, wts["n1"], wts["w_x"], wts["w_g"], wts["w_q"], wts["w_kvr"],
            wts["qan"], wts["kvan"], wts["w_uq"], wts["gq"], wts["w_uk"], wts["gk"], wts["w_uv"]]
    if rope:
        tiles_per_seq = rope_tabs[0].shape[0] // TM
        in_specs += [_resident((QL, DH)), _resident((1, HP)), _resident((1, HP))]
        in_specs += [pl.BlockSpec((TM, HP), lambda i: (i % tiles_per_seq, 0))] * 2
        args += [wts["w_uq_pair"], wts["gq_pair"], wts["gk_pair"]] + list(rope_tabs)
    out_specs = [tile(DR), tile(DR), tile(DH), tile(DH), tile(DH)]
    out_shape = [jax.ShapeDtypeStruct((n, DR), BF),
                 jax.ShapeDtypeStruct((n, DR), BF), jax.ShapeDtypeStruct((n, DH), BF),
                 jax.ShapeDtypeStruct((n, DH), BF), jax.ShapeDtypeStruct((n, DH), BF)]
    if emit_cache:
        out_specs += [tile(KVL), tile(ROPE)]
        out_shape += [jax.ShapeDtypeStruct((n, KVL), F32), jax.ShapeDtypeStruct((n, ROPE), F32)]
    return pl.pallas_call(
        functools.partial(_proj_kernel, rope=rope, emit_cache=emit_cache),
        grid=(n // TM,),
        in_specs=in_specs,
        out_specs=out_specs,
        out_shape=out_shape,
        compiler_params=_cparams(("arbitrary",)),
        name="projections",
    )(*args)


def _cache_kv_kernel(ckv_ref, krp_ref, wuk_ref, gk_ref, wuv_ref, k_ref, v_ref):
    _keys_values(ckv_ref[0], krp_ref[0], wuk_ref, wuv_ref, gk_ref[...], None, None,
                 k_ref.at[0], v_ref.at[0])


def _cache_keys_values(ckv, krp, wts):
    b, s, _ = ckv.shape
    full = lambda shape: pl.BlockSpec(shape, lambda i: (0,) * len(shape))
    return pl.pallas_call(
        _cache_kv_kernel,
        grid=(b,),
        in_specs=[pl.BlockSpec((1, s, KVL), lambda i: (i, 0, 0)),
                  pl.BlockSpec((1, s, HP), lambda i: (i, 0, 0)),
                  full((KVL, DH)), full((1, HP)), full((KVL, DH))],
        out_specs=[pl.BlockSpec((1, s, DH), lambda i: (i, 0, 0))] * 2,
        out_shape=[jax.ShapeDtypeStruct((b, s, DH), BF)] * 2,
        compiler_params=_cparams(("arbitrary",)),
        name="cache_keys_values",
    )(ckv, krp, wts["w_uk"], wts["gk"], wts["w_uv"])


def _sigmoid(x):
    return 0.5 * jnp.tanh(0.5 * x) + 0.5


def _tile_scan(a, b, forward):
    row = lax.broadcasted_iota(jnp.int32, a.shape, 0)
    for k in (1, 2, 4):
        if forward:
            shift, valid = k, row >= k
        else:
            shift, valid = SUBLANES - k, row < SUBLANES - k
        a_prev = jnp.where(valid, pltpu.roll(a, shift, 0), 1.0)
        b_prev = jnp.where(valid, pltpu.roll(b, shift, 0), 0.0)
        b = a * b_prev + b
        a = a * a_prev
    return a, b


def _rglru_kernel(*refs, t, has_h0, emit_state):
    it = iter(refs)
    xr_ref, gg_ref, cw_ref, cb_ref, bd_ref, ba_ref, bx_ref, lam_ref = (next(it) for _ in range(8))
    h0_ref = next(it) if has_h0 else None
    y_ref = next(it)
    hf_ref = next(it) if emit_state else None
    a_scr, b_scr, h_scr = (next(it) for _ in range(3))

    n_tiles = t // SUBLANES
    x = xr_ref[0].astype(F32)
    row = lax.broadcasted_iota(jnp.int32, x.shape, 0)
    xm1 = jnp.where(row >= 1, pltpu.roll(x, 1, 0), 0.0)
    xp1 = jnp.where(row < t - 1, pltpu.roll(x, t - 1, 0), 0.0)
    xp2 = jnp.where(row < t - 2, pltpu.roll(x, t - 2, 0), 0.0)
    xc = (cb_ref[...] + xm1 * cw_ref[0:1, :] + x * cw_ref[1:2, :]
          + xp1 * cw_ref[2:3, :] + xp2 * cw_ref[3:4, :])
    for s in range(CH // BDW):
        cols = slice(s * BDW, (s + 1) * BDW)
        xs = xc[:, cols]
        xsb = xs.astype(BF)
        for d in range(2):
            r = _sigmoid(_dot(xsb, bd_ref[2 * d, s]) + ba_ref[d:d + 1, cols])
            gi = _sigmoid(_dot(xsb, bd_ref[2 * d + 1, s]) + bx_ref[d:d + 1, cols])
            nl = -lam_ref[d:d + 1, cols]
            softplus = jnp.maximum(nl, 0.0) + jnp.log(1.0 + jnp.exp(-jnp.abs(nl)))
            a = jnp.exp(r * ((-LRU_C) * softplus))
            a_scr[d, :, cols] = a
            b_scr[d, :, cols] = jnp.sqrt(1.0 - a * a) * (gi * xs)

    def step(i, carry):
        out = []
        for d in range(2):
            tile = i if d == 0 else n_tiles - 1 - i
            rows = pl.ds(pl.multiple_of(tile * SUBLANES, SUBLANES), SUBLANES)
            for c in range(NLT):
                lanes = slice(c * LANES, (c + 1) * LANES)
                decay, local = _tile_scan(a_scr[d, rows, lanes], b_scr[d, rows, lanes], d == 0)
                h = local + decay * carry[d * NLT + c]
                h_scr[d, rows, lanes] = h
                last = h[SUBLANES - 1:SUBLANES, :] if d == 0 else h[0:1, :]
                out.append(jnp.broadcast_to(last, (SUBLANES, LANES)))
        return tuple(out)

    init = []
    for d in range(2):
        for c in range(NLT):
            if has_h0:
                h0 = h0_ref[0, d:d + 1, c * LANES:(c + 1) * LANES]
                init.append(jnp.broadcast_to(h0, (SUBLANES, LANES)))
            else:
                init.append(jnp.zeros((SUBLANES, LANES), F32))
    final = lax.fori_loop(0, n_tiles, step, tuple(init), unroll=2)

    if emit_state:
        for d in range(2):
            for c in range(NLT):
                hf_ref[0, d:d + 1, c * LANES:(c + 1) * LANES] = final[d * NLT + c][0:1, :]
    y_ref[0] = ((h_scr[0] + h_scr[1]) * gg_ref[0].astype(F32)).astype(BF)


def _rglru(xr, gg, wts, h0, emit_state):
    b, t, _ = xr.shape
    nc = DR // CH
    has_h0 = h0 is not None
    chunk = lambda r: pl.BlockSpec((r, CH), lambda i, j: (0, j))
    seq = pl.BlockSpec((1, t, CH), lambda i, j: (i, 0, j))
    state = pl.BlockSpec((1, 2, CH), lambda i, j: (i, 0, j))
    in_specs = [seq, seq, chunk(4), chunk(1),
                pl.BlockSpec((4, CH // BDW, BDW, BDW), lambda i, j: (0, j, 0, 0)),
                chunk(2), chunk(2), chunk(2)]
    args = [xr, gg, wts["conv_w"], wts["conv_b"], wts["bd"], wts["lru_ba"], wts["lru_bx"], wts["lru_lam"]]
    if has_h0:
        in_specs.append(state)
        args.append(h0)
    out_specs = [seq]
    out_shape = [jax.ShapeDtypeStruct((b, t, DR), BF)]
    if emit_state:
        out_specs.append(state)
        out_shape.append(jax.ShapeDtypeStruct((b, 2, DR), F32))
    res = pl.pallas_call(
        functools.partial(_rglru_kernel, t=t, has_h0=has_h0, emit_state=emit_state),
        grid=(b, nc),
        in_specs=in_specs,
        out_specs=out_specs,
        out_shape=out_shape,
        scratch_shapes=[pltpu.VMEM((2, t, CH), F32)] * 3,
        compiler_params=_cparams(("arbitrary", "arbitrary")),
        name="rglru",
    )(*args)
    return res if emit_state else (res[0], None)


def _attn_kernel(*refs, t, n_heads, has_ctx, q_block):
    it = iter(refs)
    q_ref, k_ref, v_ref = next(it), next(it), next(it)
    kc_ref = vc_ref = None
    if has_ctx:
        kc_ref, vc_ref = next(it), next(it)
    o_ref = next(it)
    log2_scale = (QK ** -0.5) * math.log2(math.e)
    for hd in range(n_heads):
        cols = slice(hd * HP, (hd + 1) * HP)
        k = k_ref[0, :, cols]
        v = v_ref[0, :, cols]
        if has_ctx:
            kc = kc_ref[0, :, cols]
            vc = vc_ref[0, :, cols]
        for qb in range(t // q_block):
            rows = slice(qb * q_block, (qb + 1) * q_block)
            q = q_ref[0, rows, cols]
            s = _dot_nt(q, k) * log2_scale
            m = jnp.max(s, axis=-1, keepdims=True)
            if has_ctx:
                sc = _dot_nt(q, kc) * log2_scale
                m = jnp.maximum(m, jnp.max(sc, axis=-1, keepdims=True))
            o = _dot(jnp.exp2(s - m).astype(BF), v)
            if has_ctx:
                o = o + _dot(jnp.exp2(sc - m).astype(BF), vc)
            o_ref[0, rows, cols] = (o / o[:, VD:VD + 1]).astype(BF)


def _attention(q, k, v, kc, vc, heads_per_step):
    b, t, _ = q.shape
    has_ctx = kc is not None
    w = heads_per_step * HP
    blk = lambda n: pl.BlockSpec((1, n, w), lambda i, j: (i, 0, j))
    in_specs = [blk(t), blk(t), blk(t)]
    args = [q, k, v]
    if has_ctx:
        in_specs += [blk(kc.shape[1])] * 2
        args += [kc, vc]
    return pl.pallas_call(
        functools.partial(_attn_kernel, t=t, n_heads=heads_per_step, has_ctx=has_ctx, q_block=min(t, 256)),
        grid=(b, NH // heads_per_step),
        in_specs=in_specs,
        out_specs=blk(t),
        out_shape=jax.ShapeDtypeStruct((b, t, DH), BF),
        compiler_params=_cparams(("arbitrary", "arbitrary")),
        name="attention",
    )(*args)


def _route(logits):
    lane = lax.broadcasted_iota(jnp.int32, logits.shape, 1)
    lanef = lane.astype(F32)
    neg = -jnp.inf
    big = float(LANES)
    gl = jnp.where((lane >= NE) & (lane < NE + NG), logits, neg)
    gmax = jnp.max(gl, axis=-1, keepdims=True)
    gidx = jnp.min(jnp.where(gl == gmax, lanef, big), axis=-1, keepdims=True) - float(NE)
    gw = 1.0 / jnp.sum(jnp.exp(gl - gmax), axis=-1, keepdims=True)
    lo = gidx * float(EPG)
    el = jnp.where((lanef >= lo) & (lanef < lo + float(EPG)), logits, neg)
    v1 = jnp.max(el, axis=-1, keepdims=True)
    i1 = jnp.min(jnp.where(el == v1, lanef, big), axis=-1, keepdims=True)
    el2 = jnp.where(lanef == i1, neg, el)
    v2 = jnp.max(el2, axis=-1, keepdims=True)
    i2 = jnp.min(jnp.where(el2 == v2, lanef, big), axis=-1, keepdims=True)
    e2 = jnp.exp(v2 - v1)
    w1 = gw / (1.0 + e2)
    w2 = gw * e2 / (1.0 + e2)
    cmb = jnp.where(lanef == i1, w1, 0.0) + jnp.where(lanef == i2, w2, 0.0)
    return cmb, gidx


def _out_kernel(x_ref, yr_ref, ya_ref, mod_ref, n1_ref, wgate_ref, wor_ref, wom_ref, wout_ref,
                n2_ref, rw_ref, rb_ref, cnt0_ref, x1_ref, hx_ref, rinfo_ref, cnt_ref, run_ref):
    @pl.when(pl.program_id(0) == 0)
    def _():
        run_ref[...] = cnt0_ref[...]

    x = x_ref[...]
    hb = _modulated_norm(x, n1_ref[...], mod_ref[0, 1:2, :], mod_ref[0, 0:1, :]).astype(BF)
    gl = _dot(hb, wgate_ref[...])
    merged = (_sigmoid(gl[:, :D]) * _dot(yr_ref[...], wor_ref[...])
              + _sigmoid(gl[:, D:]) * _dot(ya_ref[...], wom_ref[...]))
    mix = _dot(merged.astype(BF), wout_ref[...])
    x1 = x + mod_ref[0, 2:3, :] * mix
    x1_ref[...] = x1
    h2 = _modulated_norm(x1, n2_ref[...], mod_ref[0, 4:5, :], mod_ref[0, 3:4, :])
    h2_hi = h2.astype(BF)
    h2_lo = (h2 - h2_hi.astype(F32)).astype(BF)
    part = _dot(h2_hi, rw_ref[...])
    logits = part[:, :LANES] + part[:, LANES:] + _dot(h2_lo, rw_ref[:, :LANES]) + rb_ref[...]
    cmb, gidx = _route(logits)
    hx_ref[:, :D] = h2
    hx_ref[:, D:] = cmb

    lanef = lax.broadcasted_iota(jnp.int32, cmb.shape, 1).astype(F32)
    ghot = jnp.where(lanef == gidx, 1.0, 0.0)
    r_i = lax.broadcasted_iota(jnp.int32, (TM, TM), 0)
    c_i = lax.broadcasted_iota(jnp.int32, (TM, TM), 1)
    tri = jnp.where(r_i > c_i, 1.0, 0.0).astype(BF)
    before = _dot(tri, ghot.astype(BF)) + run_ref[0:1, :]
    rank = jnp.sum(before * ghot, axis=-1, keepdims=True)
    rinfo_ref[...] = jnp.where(lanef == 0.0, gidx, jnp.where(lanef == 1.0, rank, 0.0))
    run = run_ref[...] + jnp.sum(ghot, axis=0, keepdims=True)
    run_ref[...] = run
    cnt_ref[...] = run


def _merge_out(x, yr, ya, mod, mod_row, wts, cnt0):
    n = x.shape[0]
    tile = lambda w: pl.BlockSpec((TM, w), lambda i: (i, 0))
    return pl.pallas_call(
        _out_kernel,
        grid=(n // TM,),
        in_specs=[tile(D), tile(DR), tile(DH),
                  pl.BlockSpec((1, 6, D), lambda i: (mod_row(i), 0, 0)), _resident((1, D)),
                  _resident((D, 2 * D)), _resident((DR, D)), _resident((DH, D)), _resident((D, D)),
                  _resident((1, D)), _resident((D, 2 * LANES)), _resident((1, LANES)),
                  _resident((SUBLANES, LANES))],
        out_specs=[tile(D), tile(XW), tile(LANES), pl.BlockSpec((SUBLANES, LANES), lambda i: (0, 0))],
        out_shape=[jax.ShapeDtypeStruct((n, D), F32), jax.ShapeDtypeStruct((n, XW), F32),
                   jax.ShapeDtypeStruct((n, LANES), F32),
                   jax.ShapeDtypeStruct((SUBLANES, LANES), F32)],
        scratch_shapes=[pltpu.VMEM((SUBLANES, LANES), F32)],
        compiler_params=_cparams(("arbitrary",)),
        name="merge_out",
    )(x, yr, ya, mod, wts["n1"], wts["w_gate"], wts["w_o_rnn"], wts["w_o_mla"], wts["w_out"],
      wts["n2"], wts["router_w"], wts["router_b"], cnt0)


def _row_copy(src, src_row, dst, dst_row, sem):
    return pltpu.make_async_copy(src.at[pl.ds(src_row, 1)], dst.at[pl.ds(dst_row, 1)], sem)


def _dispatch_kernel(pos_ref, hx_ref, xs_in_ref, xs_ref, sem):
    del xs_in_ref
    base = pl.program_id(0) * TD

    def body(r, carry):
        _row_copy(hx_ref, r, xs_ref, pos_ref[base + r], sem).start()
        return carry

    lax.fori_loop(0, TD, body, 0, unroll=8)
    pltpu.make_async_copy(hx_ref, xs_ref.at[pl.ds(0, TD)], sem).wait()


def _dispatch(pos, hx, xs):
    n = hx.shape[0]
    return pl.pallas_call(
        _dispatch_kernel,
        grid_spec=pltpu.PrefetchScalarGridSpec(
            num_scalar_prefetch=1,
            grid=(n // TD,),
            in_specs=[pl.BlockSpec((TD, XW), lambda i, pos: (i, 0)),
                      pl.BlockSpec(memory_space=pl.ANY)],
            out_specs=pl.BlockSpec(memory_space=pl.ANY),
            scratch_shapes=[pltpu.SemaphoreType.DMA(())],
        ),
        out_shape=jax.ShapeDtypeStruct(xs.shape, xs.dtype),
        input_output_aliases={2: 0},
        compiler_params=_cparams(("arbitrary",)),
        name="dispatch",
    )(pos, hx, xs)


def _moe_kernel(tb_ref, tg_ref, nt_ref, xs_ref, w1_ref, w3_ref, w2_ref, o_ref, acc_ref):
    j = pl.program_id(0)
    e = pl.program_id(1)

    @pl.when(j < nt_ref[0])
    def _():
        @pl.when(e == 0)
        def _():
            acc_ref[...] = jnp.zeros_like(acc_ref)

        xt = xs_ref[:, :D].astype(BF)
        a = _dot(xt, w1_ref[0].astype(BF))
        he = (a * _sigmoid(a)) * _dot(xt, w3_ref[0].astype(BF))
        y = _dot(he.astype(BF), w2_ref[0].astype(BF))
        cmb = xs_ref[:, D:]
        lane = lax.broadcasted_iota(jnp.int32, cmb.shape, 1)
        ce = jnp.sum(jnp.where(lane == tg_ref[j] * EPG + e, cmb, 0.0), axis=-1, keepdims=True)
        acc_ref[...] += ce * y

        @pl.when(e == EPG - 1)
        def _():
            o_ref[...] = acc_ref[...]

    @pl.when((j >= nt_ref[0]) & (e == EPG - 1))
    def _():
        o_ref[...] = jnp.zeros_like(o_ref)


def _experts(tile_block, tile_group, n_tiles, xs, wts):
    m = xs.shape[0]

    def w_idx(j, e, tb, tg, nt):
        return (tg[j] * EPG + jnp.where(j < nt[0], e, EPG - 1), 0, 0)

    return pl.pallas_call(
        _moe_kernel,
        grid_spec=pltpu.PrefetchScalarGridSpec(
            num_scalar_prefetch=3,
            grid=(m // TMOE, EPG),
            in_specs=[pl.BlockSpec((TMOE, XW), lambda j, e, tb, tg, nt: (tb[j], 0)),
                      pl.BlockSpec((1, D, DE), w_idx),
                      pl.BlockSpec((1, D, DE), w_idx),
                      pl.BlockSpec((1, DE, D), w_idx)],
            out_specs=pl.BlockSpec((TMOE, D), lambda j, e, tb, tg, nt: (j, 0)),
            scratch_shapes=[pltpu.VMEM((TMOE, D), F32)],
        ),
        out_shape=jax.ShapeDtypeStruct((m, D), F32),
        compiler_params=_cparams(("arbitrary", "arbitrary")),
        name="experts",
    )(tile_block, tile_group, n_tiles, xs, wts["exp_w1"], wts["exp_w3"], wts["exp_w2"])


def _combine_kernel(pos_ref, x1_ref, mod_ref, ys_ref, o_ref, buf_ref, sem, *, n_steps):
    i = pl.program_id(0)

    def issue(step, slot):
        def body(r, carry):
            _row_copy(ys_ref, pos_ref[step * TD + r], buf_ref.at[slot], r, sem.at[slot]).start()
            return carry

        lax.fori_loop(0, TD, body, 0, unroll=8)

    @pl.when(i == 0)
    def _():
        issue(0, 0)

    @pl.when(i + 1 < n_steps)
    def _():
        issue(i + 1, (i + 1) % 2)

    slot = i % 2
    pltpu.make_async_copy(ys_ref.at[pl.ds(0, TD)], buf_ref.at[slot], sem.at[slot]).wait()
    o_ref[...] = x1_ref[...] + mod_ref[0, 5:6, :] * buf_ref[slot]


def _combine(pos, x1, mod, mod_row, ys):
    n = x1.shape[0]
    n_steps = n // TD
    return pl.pallas_call(
        functools.partial(_combine_kernel, n_steps=n_steps),
        grid_spec=pltpu.PrefetchScalarGridSpec(
            num_scalar_prefetch=1,
            grid=(n_steps,),
            in_specs=[pl.BlockSpec((TD, D), lambda i, pos: (i, 0)),
                      pl.BlockSpec((1, 6, D), lambda i, pos: (mod_row(i), 0, 0)),
                      pl.BlockSpec(memory_space=pl.ANY)],
            out_specs=pl.BlockSpec((TD, D), lambda i, pos: (i, 0)),
            scratch_shapes=[pltpu.VMEM((2, TD, D), F32), pltpu.SemaphoreType.DMA((2,))],
        ),
        out_shape=jax.ShapeDtypeStruct((n, D), F32),
        compiler_params=_cparams(("arbitrary",)),
        name="combine",
    )(pos, x1, mod, ys)


def _slots(rinfo, offsets):
    group = rinfo[..., 0:1]
    off = jnp.sum(jnp.where(group == jnp.arange(NG, dtype=F32), offsets.astype(F32), 0.0), axis=-1)
    return (off + rinfo[..., 1]).astype(jnp.int32).reshape(-1)


def _group_layout(counts, max_tiles):
    padded = ((counts + TMOE - 1) // TMOE) * TMOE
    ends = jnp.cumsum(padded)
    offsets = ends - padded
    n_tiles = (ends[-1] // TMOE).astype(jnp.int32)
    tile = jnp.minimum(jnp.arange(max_tiles, dtype=jnp.int32), jnp.maximum(n_tiles - 1, 0))
    tile_group = jnp.sum((tile[:, None] * TMOE >= ends[None, :]).astype(jnp.int32), axis=1)
    return offsets, tile, tile_group, n_tiles.reshape(1)


def _pad_heads(w, perm=None, rotary_only=False):
    lead = w.shape[:-1]
    per = w.shape[-1] // NH
    w = w.reshape(lead + (NH, per))
    if perm is not None:
        nope = jnp.zeros_like(w[..., :NOPE]) if rotary_only else w[..., :NOPE]
        w = jnp.concatenate([nope, w[..., NOPE:][..., perm]], axis=-1)
    w = jnp.pad(w, [(0, 0)] * len(lead) + [(0, 0), (0, HP - per)])
    return w.reshape(lead + (NH * HP,))


def _pad_gain(g, perm, rotary_only=False):
    nope = jnp.zeros((NOPE,), F32) if rotary_only else g[:NOPE]
    g = jnp.concatenate([nope, g[NOPE:][perm], jnp.zeros((HP - QK,), F32)])
    return g.reshape(1, HP)


def _block_diag(w):
    per = BDW // LRU_BLOCK
    w = w.reshape(DR // BDW, per, LRU_BLOCK, LRU_BLOCK)
    bd = jnp.einsum("jpab,pq->jpaqb", w, jnp.eye(per, dtype=w.dtype))
    return bd.reshape(DR // BDW, BDW, BDW)


def _prepare_shared(l, p):
    w_in = p["w_in"][l]
    o1, o2, o3, o4, o5 = DR, 2 * DR, 2 * DR + QL, 2 * DR + QL + KVL, 2 * DR + QL + KVL + ROPE
    bd = jnp.stack([_block_diag(p["lru_wa"][l, 0]), _block_diag(p["lru_wx"][l, 0]),
                    _block_diag(p["lru_wa"][l, 1]), _block_diag(p["lru_wx"][l, 1])]).astype(BF)
    wom = p["w_o_mla"][l].reshape(NH, VD, D)
    wom = jnp.pad(wom, ((0, 0), (0, HP - VD), (0, 0))).reshape(DH, D)
    router_w = jnp.concatenate([p["router_we"][l], p["router_wg"][l],
                                jnp.zeros((D, LANES - NE - NG), F32)], axis=1)
    router_b = jnp.concatenate([p["router_be"][l], p["router_bg"][l],
                                jnp.zeros((LANES - NE - NG,), F32)]).reshape(1, LANES)
    router_hi = router_w.astype(BF)
    router_lo = (router_w - router_hi.astype(F32)).astype(BF)
    router_w = jnp.concatenate([router_hi, router_lo], axis=1)
    return {
        "n1": p["norm1_g"][l].reshape(1, D), "n2": p["norm2_g"][l].reshape(1, D),
        "w_x": w_in[:, :o1].astype(BF), "w_g": w_in[:, o1:o2].astype(BF),
        "w_q": w_in[:, o2:o3].astype(BF), "w_kv": w_in[:, o3:o4], "w_kr": w_in[:, o4:o5],
        "w_gate": w_in[:, o5:].astype(BF),
        "qan": p["q_a_norm"][l].reshape(1, QL), "kvan": p["kv_a_norm"][l].reshape(1, KVL),
        "w_uk": _pad_heads(p["w_uk"][l]).astype(BF),
        "w_uv": _pad_heads(p["w_uv"][l]).astype(BF),
        "conv_w": p["conv_w"][l], "conv_b": p["conv_b"][l].reshape(1, DR), "bd": bd,
        "lru_ba": p["lru_ba"][l], "lru_bx": p["lru_bx"][l], "lru_lam": p["lru_lam"][l],
        "w_o_rnn": p["w_o_rnn"][l].astype(BF), "w_o_mla": wom.astype(BF), "w_out": p["w_out"][l].astype(BF),
        "router_w": router_w, "router_b": router_b,
        "exp_w1": p["exp_w1"][l], "exp_w3": p["exp_w3"][l], "exp_w2": p["exp_w2"][l],
    }


def _with_rope_order(l, p, shared, perm, rotary):
    w = dict(shared)
    zeros = lambda n: jnp.zeros((D, n), F32)
    rope_block = lambda order: [zeros(NOPE), shared["w_kr"][:, order], zeros(HP - QK)]
    kvr = [shared["w_kv"]] + rope_block(perm)
    w["w_uq"] = _pad_heads(p["w_uq"][l], perm).astype(BF)
    w["gq"] = _pad_gain(p["q_norm"][l], perm)
    w["gk"] = _pad_gain(p["k_norm"][l], perm)
    if rotary:
        pair = np.concatenate([perm[ROPE // 2:], perm[:ROPE // 2]])
        kvr += rope_block(pair)
        w["w_uq_pair"] = _pad_heads(p["w_uq"][l], pair, rotary_only=True).astype(BF)
        w["gq_pair"] = _pad_gain(p["q_norm"][l], pair, rotary_only=True)
        w["gk_pair"] = _pad_gain(p["k_norm"][l], pair, rotary_only=True)
    w["w_kvr"] = jnp.concatenate(kvr, axis=1).astype(BF)
    return w


def _rope_tables(n_tokens, perm):
    rows = n_tokens // GRID_W
    row = jnp.repeat(jnp.arange(rows), GRID_W).astype(F32)
    col = jnp.tile(jnp.arange(GRID_W), rows).astype(F32)
    axis_dim = ROPE // 2
    inv = ROPE_BASE ** (-jnp.arange(0, axis_dim, 2, dtype=F32) / axis_dim)
    ang = jnp.concatenate([row[:, None] * inv, col[:, None] * inv], axis=-1)
    cos, sin = jnp.cos(ang), jnp.sin(ang)
    ones = lambda n: jnp.ones((n_tokens, n), F32)
    zeros = lambda n: jnp.zeros((n_tokens, n), F32)
    cos_t = jnp.concatenate([ones(NOPE), cos, cos, ones(HP - QK)], axis=1)
    sin_t = jnp.concatenate([zeros(NOPE), -sin, sin, zeros(HP - QK)], axis=1)
    return cos_t, sin_t


def kernel(x_prompt, x_sample, cache_mla_ckv, cache_mla_krope, state_rglru, c, c_ctx, norm1_g, norm2_g, w_mod, b_mod, w_in, conv_w, conv_b, lru_wa, lru_ba, lru_wx, lru_bx, lru_lam, q_a_norm, kv_a_norm, w_uq, w_uk, w_uv, q_norm, k_norm, w_o_rnn, w_o_mla, w_out, router_wg, router_bg, router_we, router_be, exp_w1, exp_w3, exp_w2):
    p = dict(norm1_g=norm1_g, norm2_g=norm2_g, w_in=w_in, conv_w=conv_w, conv_b=conv_b,
             lru_wa=lru_wa, lru_ba=lru_ba, lru_wx=lru_wx, lru_bx=lru_bx, lru_lam=lru_lam,
             q_a_norm=q_a_norm, kv_a_norm=kv_a_norm, w_uq=w_uq, w_uk=w_uk, w_uv=w_uv,
             q_norm=q_norm, k_norm=k_norm, w_o_rnn=w_o_rnn, w_o_mla=w_o_mla, w_out=w_out,
             router_wg=router_wg, router_bg=router_bg, router_we=router_we, router_be=router_be,
             exp_w1=exp_w1, exp_w3=exp_w3, exp_w2=exp_w2)
    depth = w_in.shape[0]
    nb, seq, _ = x_prompt.shape
    db, dseq, _ = x_sample.shape
    ident = np.arange(ROPE)
    halves = np.concatenate([np.arange(0, ROPE, 2), np.arange(1, ROPE, 2)])
    rope_tabs = _rope_tables(dseq, halves)
    cond8 = jnp.concatenate([c_ctx[None, :], c, jnp.zeros((SUBLANES - 1 - db, D), F32)], axis=0)
    ctx_row = lambda tile_rows: (lambda i: 0)
    lat_row = lambda tile_rows: (lambda i: (i * tile_rows) // dseq + 1)
    n_ctx, n_lat = nb * seq, db * dseq
    max_tiles = (n_ctx + n_lat) // TMOE + NG
    per_seq = lambda arrs, b, t: [a.reshape(b, t, a.shape[-1]) for a in arrs]
    flat = lambda a: a.reshape(-1, a.shape[-1])

    y_prompt, y_sample = x_prompt.reshape(n_ctx, D), x_sample.reshape(n_lat, D)
    ckv_list, krope_list, rnn_list = [], [], []
    for l in range(depth):
        shared = _prepare_shared(l, p)
        w_ctx = _with_rope_order(l, p, shared, ident, False)
        w_lat = _with_rope_order(l, p, shared, halves, True)
        mod = _modulation(cond8, w_mod[l], b_mod[l]).reshape(SUBLANES, 6, D)

        xr, gg, q, k, v, ckv, kro = _projections(y_prompt, mod, ctx_row(TM), w_ctx, None, True)
        xr, gg, q, k, v = per_seq([xr, gg, q, k, v], nb, seq)
        yr, h_fin = _rglru(xr, gg, shared, None, True)
        ya = _attention(q, k, v, None, None, NH)
        x1_c, hx_c, ri_c, cnt_c = _merge_out(y_prompt, flat(yr), flat(ya), mod, ctx_row(TM), shared,
                                             jnp.zeros((SUBLANES, LANES), F32))
        ckv_list.append(ckv.reshape(nb, seq, KVL))
        krope_list.append(kro.reshape(nb, seq, ROPE))
        rnn_list.append(h_fin)

        krp_cache = jnp.pad(cache_mla_krope[:, l][..., halves], ((0, 0), (0, 0), (NOPE, HP - QK)))
        kc, vc = _cache_keys_values(cache_mla_ckv[:, l], krp_cache, w_lat)
        xr, gg, q, k, v = _projections(y_sample, mod, lat_row(TM), w_lat, rope_tabs, False)
        xr, gg, q, k, v = per_seq([xr, gg, q, k, v], db, dseq)
        yr, _ = _rglru(xr, gg, shared, state_rglru[:, l], False)
        ya = _attention(q, k, v, kc, vc, 1)
        x1_l, hx_l, ri_l, cnt_l = _merge_out(y_sample, flat(yr), flat(ya), mod, lat_row(TM), shared, cnt_c)

        offsets, tile_block, tile_group, n_tiles = _group_layout(cnt_l[0, :NG].astype(jnp.int32), max_tiles)
        pos_c, pos_l = _slots(ri_c, offsets), _slots(ri_l, offsets)
        xs = jnp.zeros((max_tiles * TMOE, XW), F32)
        xs = _dispatch(pos_c, hx_c, xs)
        xs = _dispatch(pos_l, hx_l, xs)
        ys = _experts(tile_block, tile_group, n_tiles, xs, shared)
        y_prompt = _combine(pos_c, x1_c, mod, ctx_row(TD), ys)
        y_sample = _combine(pos_l, x1_l, mod, lat_row(TD), ys)

    y_prompt, y_sample = y_prompt.reshape(nb, seq, D), y_sample.reshape(db, dseq, D)

    return (y_prompt, y_sample, jnp.stack(ckv_list, axis=1), jnp.stack(krope_list, axis=1),
            jnp.stack(rnn_list, axis=1))
```

```python
import functools
import math

import numpy as np
import jax
import jax.numpy as jnp
from jax import lax
from jax.experimental import pallas as pl
from jax.experimental.pallas import tpu as pltpu

D = 1024
DR = 1024
QL = 384
KVL = 256
NH = 8
NOPE = 64
ROPE = 32
QK = NOPE + ROPE
VD = 64
HP = 128
DH = NH * HP
GRID_W = 64
ROPE_BASE = 10000.0
EPS = 1e-6
LRU_C = 8.0
LRU_BLOCK = 64
BDW = 256
CH = 512
NLT = CH // 128
NG = 4
EPG = 4
NE = NG * EPG
DE = 512
LANES = 128
SUBLANES = 8
TM = 512
TMOE = 512
EXPERTS_PER_STEP = 2
TD = 512
XW = D + 128
VMEM_LIMIT = 52 * 1024 * 1024
BF = jnp.bfloat16
F32 = jnp.float32


def _cparams(sem):
    return pltpu.CompilerParams(dimension_semantics=sem, vmem_limit_bytes=VMEM_LIMIT)


def _dot(a, b):
    return jnp.dot(a, b, preferred_element_type=F32)


def _dot_nt(a, b):
    return lax.dot_general(a, b, (((1,), (1,)), ((), ())), preferred_element_type=F32)


def _rms(x, g, width):
    ms = jnp.sum(x * x, axis=-1, keepdims=True) * (1.0 / width)
    return x * lax.rsqrt(ms + EPS) * g


def _modulated_norm(x, g, scale, shift):
    return _rms(x, g, D) * (1.0 + scale) + shift


def _mod_kernel(c_ref, w_ref, b_ref, o_ref):
    c = c_ref[...]
    s = c * jax.nn.sigmoid(c)
    o_ref[...] = _dot(s, w_ref[...]) + b_ref[...]


def _modulation(cond8, w_mod, b_mod):
    n = w_mod.shape[1]
    return pl.pallas_call(
        _mod_kernel,
        grid=(n // D,),
        in_specs=[
            pl.BlockSpec((SUBLANES, D), lambda j: (0, 0)),
            pl.BlockSpec((D, D), lambda j: (0, j)),
            pl.BlockSpec((1, D), lambda j: (0, j)),
        ],
        out_specs=pl.BlockSpec((SUBLANES, D), lambda j: (0, j)),
        out_shape=jax.ShapeDtypeStruct((SUBLANES, n), F32),
        compiler_params=_cparams(("arbitrary",)),
        name="modulation",
    )(cond8, w_mod, b_mod.reshape(1, n))


def _head_norm(xh, gain, cos=None, partner_scaled=None):
    ms = jnp.sum(xh * xh, axis=-1, keepdims=True) * (1.0 / QK)
    rs = lax.rsqrt(ms + EPS)
    y = xh * rs * gain
    if cos is None:
        return y
    return y * cos + partner_scaled * rs


def _keys_values(ckv, krp, wuk_ref, wuv_ref, gk, cos, partner_scaled, k_ref, v_ref):
    cb = ckv.astype(BF)
    kn = _dot(cb, wuk_ref[...])
    v_ref[...] = _dot(cb, wuv_ref[...]).astype(BF)
    for h in range(NH):
        kh = kn[:, h * HP:(h + 1) * HP] + krp
        k_ref[:, h * HP:(h + 1) * HP] = _head_norm(kh, gk, cos, partner_scaled).astype(BF)


def _proj_kernel(*refs, rope, emit_cache):
    it = iter(refs)
    x_ref, mod_ref, n1_ref = next(it), next(it), next(it)
    wx_ref, wg_ref, wq_ref, wkvr_ref = next(it), next(it), next(it), next(it)
    qan_ref, kvan_ref, wuq_ref, gq_ref = next(it), next(it), next(it), next(it)
    wuk_ref, gk_ref, wuv_ref = next(it), next(it), next(it)
    if rope:
        wuqs_ref, gqs_ref, gks_ref, cos_ref, sins_ref = (next(it) for _ in range(5))
    xr_ref, gg_ref, q_ref, k_ref, v_ref = (next(it) for _ in range(5))
    if emit_cache:
        ckv_ref, kro_ref = next(it), next(it)

    hb = _modulated_norm(x_ref[...], n1_ref[...], mod_ref[0, 1:2, :], mod_ref[0, 0:1, :]).astype(BF)
    xr_ref[...] = _dot(hb, wx_ref[...]).astype(BF)
    gg_ref[...] = jax.nn.gelu(_dot(hb, wg_ref[...])).astype(BF)

    qnb = _rms(_dot(hb, wq_ref[...]), qan_ref[...], QL).astype(BF)
    q = _dot(qnb, wuq_ref[...])
    gq = gq_ref[...]
    cos = q_partner = q_pair_scale = None
    if rope:
        cos, sins = cos_ref[...], sins_ref[...]
        q_partner = _dot(qnb, wuqs_ref[...])
        q_pair_scale = gqs_ref[...] * sins
    for hd in range(NH):
        cols = slice(hd * HP, (hd + 1) * HP)
        partner = q_partner[:, cols] * q_pair_scale if rope else None
        q_ref[:, cols] = _head_norm(q[:, cols], gq, cos, partner).astype(BF)

    kvr = _dot(hb, wkvr_ref[...])
    ckv = _rms(kvr[:, :KVL], kvan_ref[...], KVL)
    krp = kvr[:, KVL:KVL + HP]
    k_partner = kvr[:, KVL + HP:KVL + 2 * HP] * (gks_ref[...] * sins) if rope else None
    if emit_cache:
        ckv_ref[...] = ckv
        kro_ref[...] = krp[:, NOPE:NOPE + ROPE]
    _keys_values(ckv, krp, wuk_ref, wuv_ref, gk_ref[...], cos, k_partner, k_ref, v_ref)


def _resident(shape):
    return pl.BlockSpec(shape, lambda i: (0,) * len(shape), pipeline_mode=pl.Buffered(1))


def _projections(x, mod, mod_row, wts, rope_tabs, emit_cache):
    n = x.shape[0]
    rope = rope_tabs is not None
    tile = lambda w: pl.BlockSpec((TM, w), lambda i: (i, 0))
    in_specs = [
        tile(D),
        pl.BlockSpec((1, 6, D), lambda i: (mod_row(i), 0, 0)),
        _resident((1, D)),
        _resident((D, DR)), _resident((D, DR)), _resident((D, QL)), _resident(wts["w_kvr"].shape),
        _resident((1, QL)), _resident((1, KVL)), _resident((QL, DH)), _resident((1, HP)),
        _resident((KVL, DH)), _resident((1, HP)), _resident((KVL, DH)),
    ]
    args = [x, mod, wts["n1"], wts["w_x"], wts["w_g"], wts["w_q"], wts["w_kvr"],
            wts["qan"], wts["kvan"], wts["w_uq"], wts["gq"], wts["w_uk"], wts["gk"], wts["w_uv"]]
    if rope:
        tiles_per_seq = rope_tabs[0].shape[0] // TM
        in_specs += [_resident((QL, DH)), _resident((1, HP)), _resident((1, HP))]
        in_specs += [pl.BlockSpec((TM, HP), lambda i: (i % tiles_per_seq, 0))] * 2
        args += [wts["w_uq_pair"], wts["gq_pair"], wts["gk_pair"]] + list(rope_tabs)
    out_specs = [tile(DR), tile(DR), tile(DH), tile(DH), tile(DH)]
    out_shape = [jax.ShapeDtypeStruct((n, DR), BF),
                 jax.ShapeDtypeStruct((n, DR), BF), jax.ShapeDtypeStruct((n, DH), BF),
                 jax.ShapeDtypeStruct((n, DH), BF), jax.ShapeDtypeStruct((n, DH), BF)]
    if emit_cache:
        out_specs += [tile(KVL), tile(ROPE)]
        out_shape += [jax.ShapeDtypeStruct((n, KVL), F32), jax.ShapeDtypeStruct((n, ROPE), F32)]
    return pl.pallas_call(
        functools.partial(_proj_kernel, rope=rope, emit_cache=emit_cache),
        grid=(n // TM,),
        in_specs=in_specs,
        out_specs=out_specs,
        out_shape=out_shape,
        compiler_params=_cparams(("arbitrary",)),
        name="projections",
    )(*args)


def _cache_kv_kernel(ckv_ref, krp_ref, wuk_ref, gk_ref, wuv_ref, k_ref, v_ref):
    _keys_values(ckv_ref[0], krp_ref[0], wuk_ref, wuv_ref, gk_ref[...], None, None,
                 k_ref.at[0], v_ref.at[0])


def _cache_keys_values(ckv, krp, wts):
    b, s, _ = ckv.shape
    full = lambda shape: pl.BlockSpec(shape, lambda i: (0,) * len(shape))
    return pl.pallas_call(
        _cache_kv_kernel,
        grid=(b,),
        in_specs=[pl.BlockSpec((1, s, KVL), lambda i: (i, 0, 0)),
                  pl.BlockSpec((1, s, HP), lambda i: (i, 0, 0)),
                  full((KVL, DH)), full((1, HP)), full((KVL, DH))],
        out_specs=[pl.BlockSpec((1, s, DH), lambda i: (i, 0, 0))] * 2,
        out_shape=[jax.ShapeDtypeStruct((b, s, DH), BF)] * 2,
        compiler_params=_cparams(("arbitrary",)),
        name="cache_keys_values",
    )(ckv, krp, wts["w_uk"], wts["gk"], wts["w_uv"])


def _sigmoid(x):
    return 0.5 * jnp.tanh(0.5 * x) + 0.5


def _tile_scan(a, b, forward):
    row = lax.broadcasted_iota(jnp.int32, a.shape, 0)
    for k in (1, 2, 4):
        if forward:
            shift, valid = k, row >= k
        else:
            shift, valid = SUBLANES - k, row < SUBLANES - k
        a_prev = jnp.where(valid, pltpu.roll(a, shift, 0), 1.0)
        b_prev = jnp.where(valid, pltpu.roll(b, shift, 0), 0.0)
        b = a * b_prev + b
        a = a * a_prev
    return a, b


def _rglru_kernel(*refs, t, has_h0, emit_state):
    it = iter(refs)
    xr_ref, gg_ref, cw_ref, cb_ref, bd_ref, ba_ref, bx_ref, lam_ref = (next(it) for _ in range(8))
    h0_ref = next(it) if has_h0 else None
    y_ref = next(it)
    hf_ref = next(it) if emit_state else None
    a_scr, b_scr, h_scr = (next(it) for _ in range(3))

    n_tiles = t // SUBLANES
    x = xr_ref[0].astype(F32)
    row = lax.broadcasted_iota(jnp.int32, x.shape, 0)
    xm1 = jnp.where(row >= 1, pltpu.roll(x, 1, 0), 0.0)
    xp1 = jnp.where(row < t - 1, pltpu.roll(x, t - 1, 0), 0.0)
    xp2 = jnp.where(row < t - 2, pltpu.roll(x, t - 2, 0), 0.0)
    xc = (cb_ref[...] + xm1 * cw_ref[0:1, :] + x * cw_ref[1:2, :]
          + xp1 * cw_ref[2:3, :] + xp2 * cw_ref[3:4, :])
    for s in range(CH // BDW):
        cols = slice(s * BDW, (s + 1) * BDW)
        xs = xc[:, cols]
        xsb = xs.astype(BF)
        for d in range(2):
            r = _sigmoid(_dot(xsb, bd_ref[2 * d, s]) + ba_ref[d:d + 1, cols])
            gi = _sigmoid(_dot(xsb, bd_ref[2 * d + 1, s]) + bx_ref[d:d + 1, cols])
            nl = -lam_ref[d:d + 1, cols]
            softplus = jnp.maximum(nl, 0.0) + jnp.log(1.0 + jnp.exp(-jnp.abs(nl)))
            a = jnp.exp(r * ((-LRU_C) * softplus))
            a_scr[d, :, cols] = a
            b_scr[d, :, cols] = jnp.sqrt(1.0 - a * a) * (gi * xs)

    def step(i, carry):
        out = []
        for d in range(2):
            tile = i if d == 0 else n_tiles - 1 - i
            rows = pl.ds(pl.multiple_of(tile * SUBLANES, SUBLANES), SUBLANES)
            for c in range(NLT):
                lanes = slice(c * LANES, (c + 1) * LANES)
                decay, local = _tile_scan(a_scr[d, rows, lanes], b_scr[d, rows, lanes], d == 0)
                h = local + decay * carry[d * NLT + c]
                h_scr[d, rows, lanes] = h
                last = h[SUBLANES - 1:SUBLANES, :] if d == 0 else h[0:1, :]
                out.append(jnp.broadcast_to(last, (SUBLANES, LANES)))
        return tuple(out)

    init = []
    for d in range(2):
        for c in range(NLT):
            if has_h0:
                h0 = h0_ref[0, d:d + 1, c * LANES:(c + 1) * LANES]
                init.append(jnp.broadcast_to(h0, (SUBLANES, LANES)))
            else:
                init.append(jnp.zeros((SUBLANES, LANES), F32))
    final = lax.fori_loop(0, n_tiles, step, tuple(init), unroll=2)

    if emit_state:
        for d in range(2):
            for c in range(NLT):
                hf_ref[0, d:d + 1, c * LANES:(c + 1) * LANES] = final[d * NLT + c][0:1, :]
    y_ref[0] = ((h_scr[0] + h_scr[1]) * gg_ref[0].astype(F32)).astype(BF)


def _rglru(xr, gg, wts, h0, emit_state):
    b, t, _ = xr.shape
    nc = DR // CH
    has_h0 = h0 is not None
    chunk = lambda r: pl.BlockSpec((r, CH), lambda i, j: (0, j))
    seq = pl.BlockSpec((1, t, CH), lambda i, j: (i, 0, j))
    state = pl.BlockSpec((1, 2, CH), lambda i, j: (i, 0, j))
    in_specs = [seq, seq, chunk(4), chunk(1),
                pl.BlockSpec((4, CH // BDW, BDW, BDW), lambda i, j: (0, j, 0, 0)),
                chunk(2), chunk(2), chunk(2)]
    args = [xr, gg, wts["conv_w"], wts["conv_b"], wts["bd"], wts["lru_ba"], wts["lru_bx"], wts["lru_lam"]]
    if has_h0:
        in_specs.append(state)
        args.append(h0)
    out_specs = [seq]
    out_shape = [jax.ShapeDtypeStruct((b, t, DR), BF)]
    if emit_state:
        out_specs.append(state)
        out_shape.append(jax.ShapeDtypeStruct((b, 2, DR), F32))
    res = pl.pallas_call(
        functools.partial(_rglru_kernel, t=t, has_h0=has_h0, emit_state=emit_state),
        grid=(b, nc),
        in_specs=in_specs,
        out_specs=out_specs,
        out_shape=out_shape,
        scratch_shapes=[pltpu.VMEM((2, t, CH), F32)] * 3,
        compiler_params=_cparams(("arbitrary", "arbitrary")),
        name="rglru",
    )(*args)
    return res if emit_state else (res[0], None)


def _attn_kernel(*refs, t, n_heads, has_ctx, q_block):
    it = iter(refs)
    q_ref, k_ref, v_ref = next(it), next(it), next(it)
    kc_ref = vc_ref = None
    if has_ctx:
        kc_ref, vc_ref = next(it), next(it)
    o_ref = next(it)
    log2_scale = (QK ** -0.5) * math.log2(math.e)
    for hd in range(n_heads):
        cols = slice(hd * HP, (hd + 1) * HP)
        k = k_ref[0, :, cols]
        v = v_ref[0, :, cols]
        if has_ctx:
            kc = kc_ref[0, :, cols]
            vc = vc_ref[0, :, cols]
        for qb in range(t // q_block):
            rows = slice(qb * q_block, (qb + 1) * q_block)
            q = q_ref[0, rows, cols]
            s = _dot_nt(q, k) * log2_scale
            m = jnp.max(s, axis=-1, keepdims=True)
            if has_ctx:
                sc = _dot_nt(q, kc) * log2_scale
                m = jnp.maximum(m, jnp.max(sc, axis=-1, keepdims=True))
            p = jnp.exp2(s - m)
            den = jnp.sum(p, axis=-1, keepdims=True)
            o = _dot(p.astype(BF), v)
            if has_ctx:
                pc = jnp.exp2(sc - m)
                den = den + jnp.sum(pc, axis=-1, keepdims=True)
                o = o + _dot(pc.astype(BF), vc)
            o_ref[0, rows, cols] = (o / den).astype(BF)


def _attention(q, k, v, kc, vc, heads_per_step):
    b, t, _ = q.shape
    has_ctx = kc is not None
    w = heads_per_step * HP
    blk = lambda n: pl.BlockSpec((1, n, w), lambda i, j: (i, 0, j))
    in_specs = [blk(t), blk(t), blk(t)]
    args = [q, k, v]
    if has_ctx:
        in_specs += [blk(kc.shape[1])] * 2
        args += [kc, vc]
    return pl.pallas_call(
        functools.partial(_attn_kernel, t=t, n_heads=heads_per_step, has_ctx=has_ctx, q_block=min(t, 256)),
        grid=(b, NH // heads_per_step),
        in_specs=in_specs,
        out_specs=blk(t),
        out_shape=jax.ShapeDtypeStruct((b, t, DH), BF),
        compiler_params=_cparams(("arbitrary", "arbitrary")),
        name="attention",
    )(*args)


def _route(logits):
    lane = lax.broadcasted_iota(jnp.int32, logits.shape, 1)
    lanef = lane.astype(F32)
    neg = -jnp.inf
    big = float(LANES)
    gl = jnp.where((lane >= NE) & (lane < NE + NG), logits, neg)
    gmax = jnp.max(gl, axis=-1, keepdims=True)
    gidx = jnp.min(jnp.where(gl == gmax, lanef, big), axis=-1, keepdims=True) - float(NE)
    gw = 1.0 / jnp.sum(jnp.exp(gl - gmax), axis=-1, keepdims=True)
    lo = gidx * float(EPG)
    el = jnp.where((lanef >= lo) & (lanef < lo + float(EPG)), logits, neg)
    v1 = jnp.max(el, axis=-1, keepdims=True)
    i1 = jnp.min(jnp.where(el == v1, lanef, big), axis=-1, keepdims=True)
    el2 = jnp.where(lanef == i1, neg, el)
    v2 = jnp.max(el2, axis=-1, keepdims=True)
    i2 = jnp.min(jnp.where(el2 == v2, lanef, big), axis=-1, keepdims=True)
    e2 = jnp.exp(v2 - v1)
    w1 = gw / (1.0 + e2)
    w2 = gw * e2 / (1.0 + e2)
    cmb = jnp.where(lanef == i1, w1, 0.0) + jnp.where(lanef == i2, w2, 0.0)
    return cmb, gidx


def _out_kernel(x_ref, yr_ref, ya_ref, mod_ref, n1_ref, wgate_ref, wor_ref, wom_ref, wout_ref,
                n2_ref, rw_ref, rb_ref, cnt0_ref, x1_ref, hx_ref, rinfo_ref, cnt_ref, run_ref):
    @pl.when(pl.program_id(0) == 0)
    def _():
        run_ref[...] = cnt0_ref[...]

    x = x_ref[...]
    hb = _modulated_norm(x, n1_ref[...], mod_ref[0, 1:2, :], mod_ref[0, 0:1, :]).astype(BF)
    gl = _dot(hb, wgate_ref[...])
    merged = (_sigmoid(gl[:, :D]) * _dot(yr_ref[...], wor_ref[...])
              + _sigmoid(gl[:, D:]) * _dot(ya_ref[...], wom_ref[...]))
    mix = _dot(merged.astype(BF), wout_ref[...])
    x1 = x + mod_ref[0, 2:3, :] * mix
    x1_ref[...] = x1
    h2 = _modulated_norm(x1, n2_ref[...], mod_ref[0, 4:5, :], mod_ref[0, 3:4, :])
    h2_hi = h2.astype(BF)
    h2_lo = (h2 - h2_hi.astype(F32)).astype(BF)
    part = _dot(h2_hi, rw_ref[...])
    logits = part[:, :LANES] + part[:, LANES:] + _dot(h2_lo, rw_ref[:, :LANES]) + rb_ref[...]
    cmb, gidx = _route(logits)
    hx_ref[:, :D] = h2
    hx_ref[:, D:] = cmb

    lanef = lax.broadcasted_iota(jnp.int32, cmb.shape, 1).astype(F32)
    ghot = jnp.where(lanef == gidx, 1.0, 0.0)
    r_i = lax.broadcasted_iota(jnp.int32, (TM, TM), 0)
    c_i = lax.broadcasted_iota(jnp.int32, (TM, TM), 1)
    tri = jnp.where(r_i > c_i, 1.0, 0.0).astype(BF)
    before = _dot(tri, ghot.astype(BF)) + run_ref[0:1, :]
    rank = jnp.sum(before * ghot, axis=-1, keepdims=True)
    rinfo_ref[...] = jnp.where(lanef == 0.0, gidx, jnp.where(lanef == 1.0, rank, 0.0))
    run = run_ref[...] + jnp.sum(ghot, axis=0, keepdims=True)
    run_ref[...] = run
    cnt_ref[...] = run


def _merge_out(x, yr, ya, mod, mod_row, wts, cnt0):
    n = x.shape[0]
    tile = lambda w: pl.BlockSpec((TM, w), lambda i: (i, 0))
    return pl.pallas_call(
        _out_kernel,
        grid=(n // TM,),
        in_specs=[tile(D), tile(DR), tile(DH),
                  pl.BlockSpec((1, 6, D), lambda i: (mod_row(i), 0, 0)), _resident((1, D)),
                  _resident((D, 2 * D)), _resident((DR, D)), _resident((DH, D)), _resident((D, D)),
                  _resident((1, D)), _resident((D, 2 * LANES)), _resident((1, LANES)),
                  _resident((SUBLANES, LANES))],
        out_specs=[tile(D), tile(XW), tile(LANES), pl.BlockSpec((SUBLANES, LANES), lambda i: (0, 0))],
        out_shape=[jax.ShapeDtypeStruct((n, D), F32), jax.ShapeDtypeStruct((n, XW), F32),
                   jax.ShapeDtypeStruct((n, LANES), F32),
                   jax.ShapeDtypeStruct((SUBLANES, LANES), F32)],
        scratch_shapes=[pltpu.VMEM((SUBLANES, LANES), F32)],
        compiler_params=_cparams(("arbitrary",)),
        name="merge_out",
    )(x, yr, ya, mod, wts["n1"], wts["w_gate"], wts["w_o_rnn"], wts["w_o_mla"], wts["w_out"],
      wts["n2"], wts["router_w"], wts["router_b"], cnt0)


def _row_copy(src, src_row, dst, dst_row, sem):
    return pltpu.make_async_copy(src.at[pl.ds(src_row, 1)], dst.at[pl.ds(dst_row, 1)], sem)


def _dispatch_kernel(pos_ref, hx_ref, xs_in_ref, xs_ref, sem):
    del xs_in_ref
    base = pl.program_id(0) * TD

    def body(r, carry):
        _row_copy(hx_ref, r, xs_ref, pos_ref[base + r], sem).start()
        return carry

    lax.fori_loop(0, TD, body, 0, unroll=8)
    pltpu.make_async_copy(hx_ref, xs_ref.at[pl.ds(0, TD)], sem).wait()


def _dispatch(pos, hx, xs):
    n = hx.shape[0]
    return pl.pallas_call(
        _dispatch_kernel,
        grid_spec=pltpu.PrefetchScalarGridSpec(
            num_scalar_prefetch=1,
            grid=(n // TD,),
            in_specs=[pl.BlockSpec((TD, XW), lambda i, pos: (i, 0)),
                      pl.BlockSpec(memory_space=pl.ANY)],
            out_specs=pl.BlockSpec(memory_space=pl.ANY),
            scratch_shapes=[pltpu.SemaphoreType.DMA(())],
        ),
        out_shape=jax.ShapeDtypeStruct(xs.shape, xs.dtype),
        input_output_aliases={2: 0},
        compiler_params=_cparams(("arbitrary",)),
        name="dispatch",
    )(pos, hx, xs)


def _moe_kernel(tb_ref, tg_ref, nt_ref, xs_ref, w1_ref, w3_ref, w2_ref, o_ref, acc_ref):
    j = pl.program_id(0)
    pair = pl.program_id(1)

    @pl.when(j < nt_ref[0])
    def _():
        xt = xs_ref[:, :D].astype(BF)
        cmb = xs_ref[:, D:]
        lane = lax.broadcasted_iota(jnp.int32, cmb.shape, 1)
        first = tg_ref[j] * EPG + pair * EXPERTS_PER_STEP
        hidden = []
        for u in range(EXPERTS_PER_STEP):
            a = _dot(xt, w1_ref[u].astype(BF))
            he = (a * _sigmoid(a)) * _dot(xt, w3_ref[u].astype(BF))
            ce = jnp.sum(jnp.where(lane == first + u, cmb, 0.0), axis=-1, keepdims=True)
            hidden.append((he * ce).astype(BF))
        w2 = w2_ref[...].reshape(EXPERTS_PER_STEP * DE, D).astype(BF)
        y = _dot(jnp.concatenate(hidden, axis=1), w2)

        @pl.when(pair == 0)
        def _():
            acc_ref[...] = y

        @pl.when(pair == EPG // EXPERTS_PER_STEP - 1)
        def _():
            o_ref[...] = acc_ref[...] + y

    @pl.when((j >= nt_ref[0]) & (pair == EPG // EXPERTS_PER_STEP - 1))
    def _():
        o_ref[...] = jnp.zeros_like(o_ref)


def _experts(tile_block, tile_group, n_tiles, xs, wts):
    m = xs.shape[0]
    steps = EPG // EXPERTS_PER_STEP
    assert steps == 2, "the kernel keeps one partial sum: first step stores it, second adds and writes"

    def w_idx(j, e, tb, tg, nt):
        return (tg[j] * steps + jnp.where(j < nt[0], e, steps - 1), 0, 0)

    return pl.pallas_call(
        _moe_kernel,
        grid_spec=pltpu.PrefetchScalarGridSpec(
            num_scalar_prefetch=3,
            grid=(m // TMOE, steps),
            in_specs=[pl.BlockSpec((TMOE, XW), lambda j, e, tb, tg, nt: (tb[j], 0)),
                      pl.BlockSpec((EXPERTS_PER_STEP, D, DE), w_idx),
                      pl.BlockSpec((EXPERTS_PER_STEP, D, DE), w_idx),
                      pl.BlockSpec((EXPERTS_PER_STEP, DE, D), w_idx)],
            out_specs=pl.BlockSpec((TMOE, D), lambda j, e, tb, tg, nt: (j, 0)),
            scratch_shapes=[pltpu.VMEM((TMOE, D), F32)],
        ),
        out_shape=jax.ShapeDtypeStruct((m, D), F32),
        compiler_params=_cparams(("arbitrary", "arbitrary")),
        name="experts",
    )(tile_block, tile_group, n_tiles, xs, wts["exp_w1"], wts["exp_w3"], wts["exp_w2"])


def _combine_kernel(pos_ref, x1_ref, mod_ref, ys_ref, o_ref, buf_ref, sem, *, n_steps):
    i = pl.program_id(0)

    def issue(step, slot):
        def body(r, carry):
            _row_copy(ys_ref, pos_ref[step * TD + r], buf_ref.at[slot], r, sem.at[slot]).start()
            return carry

        lax.fori_loop(0, TD, body, 0, unroll=8)

    @pl.when(i == 0)
    def _():
        issue(0, 0)

    @pl.when(i + 1 < n_steps)
    def _():
        issue(i + 1, (i + 1) % 2)

    slot = i % 2
    pltpu.make_async_copy(ys_ref.at[pl.ds(0, TD)], buf_ref.at[slot], sem.at[slot]).wait()
    o_ref[...] = x1_ref[...] + mod_ref[0, 5:6, :] * buf_ref[slot]


def _combine(pos, x1, mod, mod_row, ys):
    n = x1.shape[0]
    n_steps = n // TD
    return pl.pallas_call(
        functools.partial(_combine_kernel, n_steps=n_steps),
        grid_spec=pltpu.PrefetchScalarGridSpec(
            num_scalar_prefetch=1,
            grid=(n_steps,),
            in_specs=[pl.BlockSpec((TD, D), lambda i, pos: (i, 0)),
                      pl.BlockSpec((1, 6, D), lambda i, pos: (mod_row(i), 0, 0)),
                      pl.BlockSpec(memory_space=pl.ANY)],
            out_specs=pl.BlockSpec((TD, D), lambda i, pos: (i, 0)),
            scratch_shapes=[pltpu.VMEM((2, TD, D), F32), pltpu.SemaphoreType.DMA((2,))],
        ),
        out_shape=jax.ShapeDtypeStruct((n, D), F32),
        compiler_params=_cparams(("arbitrary",)),
        name="combine",
    )(pos, x1, mod, ys)


def _slots(rinfo, offsets):
    group = rinfo[..., 0:1]
    off = jnp.sum(jnp.where(group == jnp.arange(NG, dtype=F32), offsets.astype(F32), 0.0), axis=-1)
    return (off + rinfo[..., 1]).astype(jnp.int32).reshape(-1)


def _group_layout(counts, max_tiles):
    padded = ((counts + TMOE - 1) // TMOE) * TMOE
    ends = jnp.cumsum(padded)
    offsets = ends - padded
    n_tiles = (ends[-1] // TMOE).astype(jnp.int32)
    tile = jnp.minimum(jnp.arange(max_tiles, dtype=jnp.int32), jnp.maximum(n_tiles - 1, 0))
    tile_group = jnp.sum((tile[:, None] * TMOE >= ends[None, :]).astype(jnp.int32), axis=1)
    return offsets, tile, tile_group, n_tiles.reshape(1)


def _pad_heads(w, perm=None, rotary_only=False):
    lead = w.shape[:-1]
    per = w.shape[-1] // NH
    w = w.reshape(lead + (NH, per))
    if perm is not None:
        nope = jnp.zeros_like(w[..., :NOPE]) if rotary_only else w[..., :NOPE]
        w = jnp.concatenate([nope, w[..., NOPE:][..., perm]], axis=-1)
    w = jnp.pad(w, [(0, 0)] * len(lead) + [(0, 0), (0, HP - per)])
    return w.reshape(lead + (NH * HP,))


def _pad_gain(g, perm, rotary_only=False):
    nope = jnp.zeros((NOPE,), F32) if rotary_only else g[:NOPE]
    g = jnp.concatenate([nope, g[NOPE:][perm], jnp.zeros((HP - QK,), F32)])
    return g.reshape(1, HP)


def _block_diag(w):
    per = BDW // LRU_BLOCK
    w = w.reshape(DR // BDW, per, LRU_BLOCK, LRU_BLOCK)
    bd = jnp.einsum("jpab,pq->jpaqb", w, jnp.eye(per, dtype=w.dtype))
    return bd.reshape(DR // BDW, BDW, BDW)


def _prepare_shared(l, p):
    w_in = p["w_in"][l]
    o1, o2, o3, o4, o5 = DR, 2 * DR, 2 * DR + QL, 2 * DR + QL + KVL, 2 * DR + QL + KVL + ROPE
    bd = jnp.stack([_block_diag(p["lru_wa"][l, 0]), _block_diag(p["lru_wx"][l, 0]),
                    _block_diag(p["lru_wa"][l, 1]), _block_diag(p["lru_wx"][l, 1])]).astype(BF)
    wom = p["w_o_mla"][l].reshape(NH, VD, D)
    wom = jnp.pad(wom, ((0, 0), (0, HP - VD), (0, 0))).reshape(DH, D)
    router_w = jnp.concatenate([p["router_we"][l], p["router_wg"][l],
                                jnp.zeros((D, LANES - NE - NG), F32)], axis=1)
    router_b = jnp.concatenate([p["router_be"][l], p["router_bg"][l],
                                jnp.zeros((LANES - NE - NG,), F32)]).reshape(1, LANES)
    router_hi = router_w.astype(BF)
    router_lo = (router_w - router_hi.astype(F32)).astype(BF)
    router_w = jnp.concatenate([router_hi, router_lo], axis=1)
    return {
        "n1": p["norm1_g"][l].reshape(1, D), "n2": p["norm2_g"][l].reshape(1, D),
        "w_x": w_in[:, :o1].astype(BF), "w_g": w_in[:, o1:o2].astype(BF),
        "w_q": w_in[:, o2:o3].astype(BF), "w_kv": w_in[:, o3:o4], "w_kr": w_in[:, o4:o5],
        "w_gate": w_in[:, o5:].astype(BF),
        "qan": p["q_a_norm"][l].reshape(1, QL), "kvan": p["kv_a_norm"][l].reshape(1, KVL),
        "w_uk": _pad_heads(p["w_uk"][l]).astype(BF),
        "w_uv": _pad_heads(p["w_uv"][l]).astype(BF),
        "conv_w": p["conv_w"][l], "conv_b": p["conv_b"][l].reshape(1, DR), "bd": bd,
        "lru_ba": p["lru_ba"][l], "lru_bx": p["lru_bx"][l], "lru_lam": p["lru_lam"][l],
        "w_o_rnn": p["w_o_rnn"][l].astype(BF), "w_o_mla": wom.astype(BF), "w_out": p["w_out"][l].astype(BF),
        "router_w": router_w, "router_b": router_b,
        "exp_w1": p["exp_w1"][l], "exp_w3": p["exp_w3"][l], "exp_w2": p["exp_w2"][l],
    }


def _with_rope_order(l, p, shared, perm, rotary):
    w = dict(shared)
    zeros = lambda n: jnp.zeros((D, n), F32)
    rope_block = lambda order: [zeros(NOPE), shared["w_kr"][:, order], zeros(HP - QK)]
    kvr = [shared["w_kv"]] + rope_block(perm)
    w["w_uq"] = _pad_heads(p["w_uq"][l], perm).astype(BF)
    w["gq"] = _pad_gain(p["q_norm"][l], perm)
    w["gk"] = _pad_gain(p["k_norm"][l], perm)
    if rotary:
        pair = np.concatenate([perm[ROPE // 2:], perm[:ROPE // 2]])
        kvr += rope_block(pair)
        w["w_uq_pair"] = _pad_heads(p["w_uq"][l], pair, rotary_only=True).astype(BF)
        w["gq_pair"] = _pad_gain(p["q_norm"][l], pair, rotary_only=True)
        w["gk_pair"] = _pad_gain(p["k_norm"][l], pair, rotary_only=True)
    w["w_kvr"] = jnp.concatenate(kvr, axis=1).astype(BF)
    return w


def _rope_tables(n_tokens, perm):
    rows = n_tokens // GRID_W
    row = np.repeat(np.arange(rows), GRID_W).astype(np.float32)
    col = np.tile(np.arange(GRID_W), rows).astype(np.float32)
    axis_dim = ROPE // 2
    inv = (np.float32(ROPE_BASE) ** (-np.arange(0, axis_dim, 2, dtype=np.float32) / axis_dim)).astype(np.float32)
    ang = np.concatenate([row[:, None] * inv, col[:, None] * inv], axis=-1).astype(np.float32)
    cos, sin = np.cos(ang), np.sin(ang)
    ones = lambda n: np.ones((n_tokens, n), np.float32)
    zeros = lambda n: np.zeros((n_tokens, n), np.float32)
    cos_t = np.concatenate([ones(NOPE), cos, cos, ones(HP - QK)], axis=1)
    sin_t = np.concatenate([zeros(NOPE), -sin, sin, zeros(HP - QK)], axis=1)
    return jnp.asarray(cos_t, F32), jnp.asarray(sin_t, F32)


def kernel(x_prompt, x_sample, cache_mla_ckv, cache_mla_krope, state_rglru, c, c_ctx, norm1_g, norm2_g, w_mod, b_mod, w_in, conv_w, conv_b, lru_wa, lru_ba, lru_wx, lru_bx, lru_lam, q_a_norm, kv_a_norm, w_uq, w_uk, w_uv, q_norm, k_norm, w_o_rnn, w_o_mla, w_out, router_wg, router_bg, router_we, router_be, exp_w1, exp_w3, exp_w2):
    p = dict(norm1_g=norm1_g, norm2_g=norm2_g, w_in=w_in, conv_w=conv_w, conv_b=conv_b,
             lru_wa=lru_wa, lru_ba=lru_ba, lru_wx=lru_wx, lru_bx=lru_bx, lru_lam=lru_lam,
             q_a_norm=q_a_norm, kv_a_norm=kv_a_norm, w_uq=w_uq, w_uk=w_uk, w_uv=w_uv,
             q_norm=q_norm, k_norm=k_norm, w_o_rnn=w_o_rnn, w_o_mla=w_o_mla, w_out=w_out,
             router_wg=router_wg, router_bg=router_bg, router_we=router_we, router_be=router_be,
             exp_w1=exp_w1, exp_w3=exp_w3, exp_w2=exp_w2)
    depth = w_in.shape[0]
    nb, seq, _ = x_prompt.shape
    db, dseq, _ = x_sample.shape
    ident = np.arange(ROPE)
    halves = np.concatenate([np.arange(0, ROPE, 2), np.arange(1, ROPE, 2)])
    rope_tabs = _rope_tables(dseq, halves)
    cond8 = jnp.concatenate([c_ctx[None, :], c, jnp.zeros((SUBLANES - 1 - db, D), F32)], axis=0)
    ctx_row = lambda tile_rows: (lambda i: 0)
    lat_row = lambda tile_rows: (lambda i: (i * tile_rows) // dseq + 1)
    n_ctx, n_lat = nb * seq, db * dseq
    max_tiles = (n_ctx + n_lat) // TMOE + NG
    per_seq = lambda arrs, b, t: [a.reshape(b, t, a.shape[-1]) for a in arrs]
    flat = lambda a: a.reshape(-1, a.shape[-1])

    y_prompt, y_sample = x_prompt.reshape(n_ctx, D), x_sample.reshape(n_lat, D)
    ckv_list, krope_list, rnn_list = [], [], []
    for l in range(depth):
        shared = _prepare_shared(l, p)
        w_ctx = _with_rope_order(l, p, shared, ident, False)
        w_lat = _with_rope_order(l, p, shared, halves, True)
        mod = _modulation(cond8, w_mod[l], b_mod[l]).reshape(SUBLANES, 6, D)

        xr, gg, q, k, v, ckv, kro = _projections(y_prompt, mod, ctx_row(TM), w_ctx, None, True)
        xr, gg, q, k, v = per_seq([xr, gg, q, k, v], nb, seq)
        yr, h_fin = _rglru(xr, gg, shared, None, True)
        ya = _attention(q, k, v, None, None, NH)
        x1_c, hx_c, ri_c, cnt_c = _merge_out(y_prompt, flat(yr), flat(ya), mod, ctx_row(TM), shared,
                                             jnp.zeros((SUBLANES, LANES), F32))
        ckv_list.append(ckv.reshape(nb, seq, KVL))
        krope_list.append(kro.reshape(nb, seq, ROPE))
        rnn_list.append(h_fin)

        krp_cache = jnp.pad(cache_mla_krope[:, l][..., halves], ((0, 0), (0, 0), (NOPE, HP - QK)))
        kc, vc = _cache_keys_values(cache_mla_ckv[:, l], krp_cache, w_lat)
        xr, gg, q, k, v = _projections(y_sample, mod, lat_row(TM), w_lat, rope_tabs, False)
        xr, gg, q, k, v = per_seq([xr, gg, q, k, v], db, dseq)
        yr, _ = _rglru(xr, gg, shared, state_rglru[:, l], False)
        ya = _attention(q, k, v, kc, vc, 2)
        x1_l, hx_l, ri_l, cnt_l = _merge_out(y_sample, flat(yr), flat(ya), mod, lat_row(TM), shared, cnt_c)

        offsets, tile_block, tile_group, n_tiles = _group_layout(cnt_l[0, :NG].astype(jnp.int32), max_tiles)
        pos_c, pos_l = _slots(ri_c, offsets), _slots(ri_l, offsets)
        xs = jnp.zeros((max_tiles * TMOE, XW), F32)
        xs = _dispatch(pos_c, hx_c, xs)
        xs = _dispatch(pos_l, hx_l, xs)
        ys = _experts(tile_block, tile_group, n_tiles, xs, shared)
        y_prompt = _combine(pos_c, x1_c, mod, ctx_row(TD), ys)
        y_sample = _combine(pos_l, x1_l, mod, lat_row(TD), ys)

    y_prompt, y_sample = y_prompt.reshape(nb, seq, D), y_sample.reshape(db, dseq, D)

    return (y_prompt, y_sample, jnp.stack(ckv_list, axis=1), jnp.stack(krope_list, axis=1),
            jnp.stack(rnn_list, axis=1))
```

```python
import functools
import math

import numpy as np
import jax
import jax.numpy as jnp
from jax import lax
from jax.experimental import pallas as pl
from jax.experimental.pallas import tpu as pltpu

D = 1024
DR = 1024
QL = 384
KVL = 256
NH = 8
NOPE = 64
ROPE = 32
QK = NOPE + ROPE
VD = 64
HP = 128
DH = NH * HP
GRID_W = 64
ROPE_BASE = 10000.0
EPS = 1e-6
TINY = 1e-30
LRU_C = 8.0
LRU_BLOCK = 64
BDW = 256
CH = 512
NLT = CH // 128
NG = 4
EPG = 4
NE = NG * EPG
DE = 512
LANES = 128
SUBLANES = 8
TM = 512
TMOE = 512
EXPERTS_PER_STEP = 2
TD = 512
XW = D + 128
VMEM_LIMIT = 52 * 1024 * 1024
BF = jnp.bfloat16
F32 = jnp.float32


def _cparams(sem):
    return pltpu.CompilerParams(dimension_semantics=sem, vmem_limit_bytes=VMEM_LIMIT)


def _dot(a, b):
    return jnp.dot(a, b, preferred_element_type=F32)


def _dot_nt(a, b):
    return lax.dot_general(a, b, (((1,), (1,)), ((), ())), preferred_element_type=F32)


def _rms(x, g, width):
    ms = jnp.sum(x * x, axis=-1, keepdims=True) * (1.0 / width)
    return x * lax.rsqrt(ms + EPS) * g


def _modulated_norm(x, g, scale, shift):
    return _rms(x, g, D) * (1.0 + scale) + shift


def _mod_kernel(c_ref, w_ref, b_ref, o_ref):
    c = c_ref[...]
    s = c * jax.nn.sigmoid(c)
    o_ref[...] = _dot(s, w_ref[...]) + b_ref[...]


def _modulation(cond8, w_mod, b_mod):
    n = w_mod.shape[1]
    return pl.pallas_call(
        _mod_kernel,
        grid=(n // D,),
        in_specs=[
            pl.BlockSpec((SUBLANES, D), lambda j: (0, 0)),
            pl.BlockSpec((D, D), lambda j: (0, j)),
            pl.BlockSpec((1, D), lambda j: (0, j)),
        ],
        out_specs=pl.BlockSpec((SUBLANES, D), lambda j: (0, j)),
        out_shape=jax.ShapeDtypeStruct((SUBLANES, n), F32),
        compiler_params=_cparams(("arbitrary",)),
        name="modulation",
    )(cond8, w_mod, b_mod.reshape(1, n))


def _head_norm(xh, gain, cos=None, partner_scaled=None):
    ms = jnp.sum(xh * xh, axis=-1, keepdims=True) * (1.0 / QK)
    rs = lax.rsqrt(ms + EPS)
    y = xh * rs * gain
    if cos is None:
        return y
    return y * cos + partner_scaled * rs


def _keys_values(ckv, krp, wuk_ref, wuv_ref, gk, cos, partner_scaled, k_ref, v_ref):
    cb = ckv.astype(BF)
    kn = _dot(cb, wuk_ref[...])
    v_ref[...] = _dot(cb, wuv_ref[...]).astype(BF)
    for h in range(NH):
        kh = kn[:, h * HP:(h + 1) * HP] + krp
        k_ref[:, h * HP:(h + 1) * HP] = _head_norm(kh, gk, cos, partner_scaled).astype(BF)


def _proj_kernel(*refs, rope, emit_cache):
    it = iter(refs)
    x_ref, mod_ref, n1_ref = next(it), next(it), next(it)
    wx_ref, wg_ref, wq_ref, wkvr_ref = next(it), next(it), next(it), next(it)
    qan_ref, kvan_ref, wuq_ref, gq_ref = next(it), next(it), next(it), next(it)
    wuk_ref, gk_ref, wuv_ref = next(it), next(it), next(it)
    if rope:
        wuqs_ref, gqs_ref, gks_ref, cos_ref, sins_ref = (next(it) for _ in range(5))
    xr_ref, gg_ref, q_ref, k_ref, v_ref = (next(it) for _ in range(5))
    if emit_cache:
        ckv_ref, kro_ref = next(it), next(it)

    hb = _modulated_norm(x_ref[...], n1_ref[...], mod_ref[0, 1:2, :], mod_ref[0, 0:1, :]).astype(BF)
    xr_ref[...] = _dot(hb, wx_ref[...]).astype(BF)
    gg_ref[...] = jax.nn.gelu(_dot(hb, wg_ref[...])).astype(BF)

    qnb = _rms(_dot(hb, wq_ref[...]), qan_ref[...], QL).astype(BF)
    q = _dot(qnb, wuq_ref[...])
    gq = gq_ref[...]
    cos = q_partner = q_pair_scale = None
    if rope:
        cos, sins = cos_ref[...], sins_ref[...]
        q_partner = _dot(qnb, wuqs_ref[...])
        q_pair_scale = gqs_ref[...] * sins
    for hd in range(NH):
        cols = slice(hd * HP, (hd + 1) * HP)
        partner = q_partner[:, cols] * q_pair_scale if rope else None
        q_ref[:, cols] = _head_norm(q[:, cols], gq, cos, partner).astype(BF)

    kvr = _dot(hb, wkvr_ref[...])
    ckv = _rms(kvr[:, :KVL], kvan_ref[...], KVL)
    krp = kvr[:, KVL:KVL + HP]
    k_partner = kvr[:, KVL + HP:KVL + 2 * HP] * (gks_ref[...] * sins) if rope else None
    if emit_cache:
        ckv_ref[...] = ckv
        kro_ref[...] = krp[:, NOPE:NOPE + ROPE]
    _keys_values(ckv, krp, wuk_ref, wuv_ref, gk_ref[...], cos, k_partner, k_ref, v_ref)


def _resident(shape):
    return pl.BlockSpec(shape, lambda i: (0,) * len(shape), pipeline_mode=pl.Buffered(1))


def _projections(x, mod, mod_row, wts, rope_tabs, emit_cache):
    n = x.shape[0]
    rope = rope_tabs is not None
    tile = lambda w: pl.BlockSpec((TM, w), lambda i: (i, 0))
    in_specs = [
        tile(D),
        pl.BlockSpec((1, 6, D), lambda i: (mod_row(i), 0, 0)),
        _resident((1, D)),
        _resident((D, DR)), _resident((D, DR)), _resident((D, QL)), _resident(wts["w_kvr"].shape),
        _resident((1, QL)), _resident((1, KVL)), _resident((QL, DH)), _resident((1, HP)),
        _resident((KVL, DH)), _resident((1, HP)), _resident((KVL, DH)),
    ]
    args = [x, mod, wts["n1"], wts["w_x"], wts["w_g"], wts["w_q"], wts["w_kvr"],
            wts["qan"], wts["kvan"], wts["w_uq"], wts["gq"], wts["w_uk"], wts["gk"], wts["w_uv"]]
    if rope:
        tiles_per_seq = rope_tabs[0].shape[0] // TM
        in_specs += [_resident((QL, DH)), _resident((1, HP)), _resident((1, HP))]
        in_specs += [pl.BlockSpec((TM, HP), lambda i: (i % tiles_per_seq, 0))] * 2
        args += [wts["w_uq_pair"], wts["gq_pair"], wts["gk_pair"]] + list(rope_tabs)
    out_specs = [tile(DR), tile(DR), tile(DH), tile(DH), tile(DH)]
    out_shape = [jax.ShapeDtypeStruct((n, DR), BF),
                 jax.ShapeDtypeStruct((n, DR), BF), jax.ShapeDtypeStruct((n, DH), BF),
                 jax.ShapeDtypeStruct((n, DH), BF), jax.ShapeDtypeStruct((n, DH), BF)]
    if emit_cache:
        out_specs += [tile(KVL), tile(ROPE)]
        out_shape += [jax.ShapeDtypeStruct((n, KVL), F32), jax.ShapeDtypeStruct((n, ROPE), F32)]
    return pl.pallas_call(
        functools.partial(_proj_kernel, rope=rope, emit_cache=emit_cache),
        grid=(n // TM,),
        in_specs=in_specs,
        out_specs=out_specs,
        out_shape=out_shape,
        compiler_params=_cparams(("arbitrary",)),
        name="projections",
    )(*args)


def _cache_kv_kernel(ckv_ref, krp_ref, wuk_ref, gk_ref, wuv_ref, k_ref, v_ref):
    _keys_values(ckv_ref[0], krp_ref[0], wuk_ref, wuv_ref, gk_ref[...], None, None,
                 k_ref.at[0], v_ref.at[0])


def _cache_keys_values(ckv, krp, wts):
    b, s, _ = ckv.shape
    full = lambda shape: pl.BlockSpec(shape, lambda i: (0,) * len(shape))
    return pl.pallas_call(
        _cache_kv_kernel,
        grid=(b,),
        in_specs=[pl.BlockSpec((1, s, KVL), lambda i: (i, 0, 0)),
                  pl.BlockSpec((1, s, HP), lambda i: (i, 0, 0)),
                  full((KVL, DH)), full((1, HP)), full((KVL, DH))],
        out_specs=[pl.BlockSpec((1, s, DH), lambda i: (i, 0, 0))] * 2,
        out_shape=[jax.ShapeDtypeStruct((b, s, DH), BF)] * 2,
        compiler_params=_cparams(("arbitrary",)),
        name="cache_keys_values",
    )(ckv, krp, wts["w_uk"], wts["gk"], wts["w_uv"])


def _sigmoid(x):
    return 0.5 * jnp.tanh(0.5 * x) + 0.5


def _tile_scan(a, b, forward):
    row = lax.broadcasted_iota(jnp.int32, a.shape, 0)
    for k in (1, 2, 4):
        if forward:
            shift, valid = k, row >= k
        else:
            shift, valid = SUBLANES - k, row < SUBLANES - k
        a_prev = jnp.where(valid, pltpu.roll(a, shift, 0), 1.0)
        b_prev = jnp.where(valid, pltpu.roll(b, shift, 0), 0.0)
        b = a * b_prev + b
        a = a * a_prev
    return a, b


def _rglru_kernel(*refs, t, has_h0, emit_state):
    it = iter(refs)
    xr_ref, gg_ref, cw_ref, cb_ref, bd_ref, ba_ref, bx_ref, lam_ref = (next(it) for _ in range(8))
    h0_ref = next(it) if has_h0 else None
    y_ref = next(it)
    hf_ref = next(it) if emit_state else None
    a_scr, b_scr, h_scr, xpad_scr = (next(it) for _ in range(4))

    n_tiles = t // SUBLANES
    pad = SUBLANES
    zero_rows = jnp.zeros((pad, CH), F32)
    xpad_scr[0:pad, :] = zero_rows
    xpad_scr[pad + t:pad + t + pad, :] = zero_rows
    xpad_scr[pad:pad + t, :] = xr_ref[0].astype(F32)
    xpad = xpad_scr[...]
    rows_all = t + 2 * pad
    xc = cb_ref[...] + xpad[pad:pad + t, :] * cw_ref[1:2, :]
    for tap, shift in ((0, 1), (2, rows_all - 1), (3, rows_all - 2)):
        xc = xc + pltpu.roll(xpad, shift, 0)[pad:pad + t, :] * cw_ref[tap:tap + 1, :]
    for s in range(CH // BDW):
        cols = slice(s * BDW, (s + 1) * BDW)
        xs = xc[:, cols]
        xsb = xs.astype(BF)
        xh = 0.5 * xs
        for d in range(2):
            tr = jnp.tanh(_dot(xsb, bd_ref[2 * d, s]) + ba_ref[d:d + 1, cols])
            ti = jnp.tanh(_dot(xsb, bd_ref[2 * d + 1, s]) + bx_ref[d:d + 1, cols])
            nl = -lam_ref[d:d + 1, cols]
            softplus = jnp.maximum(nl, 0.0) + jnp.log(1.0 + jnp.exp(-jnp.abs(nl)))
            ch = (-0.5 * LRU_C) * softplus
            a = jnp.exp(tr * ch + ch)
            z = 1.0 - a * a
            root = z * lax.rsqrt(jnp.maximum(z, TINY))
            a_scr[d, :, cols] = a
            b_scr[d, :, cols] = root * (ti * xh + xh)

    def step(i, carry):
        out = []
        for d in range(2):
            tile = i if d == 0 else n_tiles - 1 - i
            rows = pl.ds(pl.multiple_of(tile * SUBLANES, SUBLANES), SUBLANES)
            for c in range(NLT):
                lanes = slice(c * LANES, (c + 1) * LANES)
                decay, local = _tile_scan(a_scr[d, rows, lanes], b_scr[d, rows, lanes], d == 0)
                h = local + decay * carry[d * NLT + c]
                h_scr[d, rows, lanes] = h
                last = h[SUBLANES - 1:SUBLANES, :] if d == 0 else h[0:1, :]
                out.append(jnp.broadcast_to(last, (SUBLANES, LANES)))
        return tuple(out)

    init = []
    for d in range(2):
        for c in range(NLT):
            if has_h0:
                h0 = h0_ref[0, d:d + 1, c * LANES:(c + 1) * LANES]
                init.append(jnp.broadcast_to(h0, (SUBLANES, LANES)))
            else:
                init.append(jnp.zeros((SUBLANES, LANES), F32))
    final = lax.fori_loop(0, n_tiles, step, tuple(init), unroll=2)

    if emit_state:
        for d in range(2):
            for c in range(NLT):
                hf_ref[0, d:d + 1, c * LANES:(c + 1) * LANES] = final[d * NLT + c][0:1, :]
    y_ref[0] = ((h_scr[0] + h_scr[1]) * gg_ref[0].astype(F32)).astype(BF)


def _rglru(xr, gg, wts, h0, emit_state):
    b, t, _ = xr.shape
    nc = DR // CH
    has_h0 = h0 is not None
    chunk = lambda r: pl.BlockSpec((r, CH), lambda i, j: (0, j))
    seq = pl.BlockSpec((1, t, CH), lambda i, j: (i, 0, j))
    state = pl.BlockSpec((1, 2, CH), lambda i, j: (i, 0, j))
    in_specs = [seq, seq, chunk(4), chunk(1),
                pl.BlockSpec((4, CH // BDW, BDW, BDW), lambda i, j: (0, j, 0, 0)),
                chunk(2), chunk(2), chunk(2)]
    args = [xr, gg, wts["conv_w"], wts["conv_b"], wts["bd"], wts["lru_ba"], wts["lru_bx"], wts["lru_lam"]]
    if has_h0:
        in_specs.append(state)
        args.append(h0)
    out_specs = [seq]
    out_shape = [jax.ShapeDtypeStruct((b, t, DR), BF)]
    if emit_state:
        out_specs.append(state)
        out_shape.append(jax.ShapeDtypeStruct((b, 2, DR), F32))
    res = pl.pallas_call(
        functools.partial(_rglru_kernel, t=t, has_h0=has_h0, emit_state=emit_state),
        grid=(b, nc),
        in_specs=in_specs,
        out_specs=out_specs,
        out_shape=out_shape,
        scratch_shapes=[pltpu.VMEM((2, t, CH), F32)] * 3 + [pltpu.VMEM((t + 2 * SUBLANES, CH), F32)],
        compiler_params=_cparams(("arbitrary", "arbitrary")),
        name="rglru",
    )(*args)
    return res if emit_state else (res[0], None)


def _attn_kernel(*refs, t, n_heads, has_ctx, q_block):
    it = iter(refs)
    q_ref, k_ref, v_ref = next(it), next(it), next(it)
    kc_ref = vc_ref = None
    if has_ctx:
        kc_ref, vc_ref = next(it), next(it)
    o_ref = next(it)
    log2_scale = (QK ** -0.5) * math.log2(math.e)
    for hd in range(n_heads):
        cols = slice(hd * HP, (hd + 1) * HP)
        k = k_ref[0, :, cols]
        v = v_ref[0, :, cols]
        if has_ctx:
            kc = kc_ref[0, :, cols]
            vc = vc_ref[0, :, cols]
        for qb in range(t // q_block):
            rows = slice(qb * q_block, (qb + 1) * q_block)
            q = q_ref[0, rows, cols]
            s = _dot_nt(q, k) * log2_scale
            m = jnp.max(s, axis=-1, keepdims=True)
            if has_ctx:
                sc = _dot_nt(q, kc) * log2_scale
                m = jnp.maximum(m, jnp.max(sc, axis=-1, keepdims=True))
            p = jnp.exp2(s - m)
            den = jnp.sum(p, axis=-1, keepdims=True)
            o = _dot(p.astype(BF), v)
            if has_ctx:
                pc = jnp.exp2(sc - m)
                den = den + jnp.sum(pc, axis=-1, keepdims=True)
                o = o + _dot(pc.astype(BF), vc)
            o_ref[0, rows, cols] = (o / den).astype(BF)


def _attention(q, k, v, kc, vc, heads_per_step):
    b, t, _ = q.shape
    has_ctx = kc is not None
    w = heads_per_step * HP
    blk = lambda n: pl.BlockSpec((1, n, w), lambda i, j: (i, 0, j))
    in_specs = [blk(t), blk(t), blk(t)]
    args = [q, k, v]
    if has_ctx:
        in_specs += [blk(kc.shape[1])] * 2
        args += [kc, vc]
    return pl.pallas_call(
        functools.partial(_attn_kernel, t=t, n_heads=heads_per_step, has_ctx=has_ctx, q_block=min(t, 256)),
        grid=(b, NH // heads_per_step),
        in_specs=in_specs,
        out_specs=blk(t),
        out_shape=jax.ShapeDtypeStruct((b, t, DH), BF),
        compiler_params=_cparams(("arbitrary", "arbitrary")),
        name="attention",
    )(*args)


def _route(logits):
    lane = lax.broadcasted_iota(jnp.int32, logits.shape, 1)
    lanef = lane.astype(F32)
    neg = -jnp.inf
    big = float(LANES)
    gl = jnp.where((lane >= NE) & (lane < NE + NG), logits, neg)
    gmax = jnp.max(gl, axis=-1, keepdims=True)
    gidx = jnp.min(jnp.where(gl == gmax, lanef, big), axis=-1, keepdims=True) - float(NE)
    gw = 1.0 / jnp.sum(jnp.exp(gl - gmax), axis=-1, keepdims=True)
    lo = gidx * float(EPG)
    el = jnp.where((lanef >= lo) & (lanef < lo + float(EPG)), logits, neg)
    v1 = jnp.max(el, axis=-1, keepdims=True)
    i1 = jnp.min(jnp.where(el == v1, lanef, big), axis=-1, keepdims=True)
    el2 = jnp.where(lanef == i1, neg, el)
    v2 = jnp.max(el2, axis=-1, keepdims=True)
    i2 = jnp.min(jnp.where(el2 == v2, lanef, big), axis=-1, keepdims=True)
    e2 = jnp.exp(v2 - v1)
    w1 = gw / (1.0 + e2)
    w2 = gw * e2 / (1.0 + e2)
    cmb = jnp.where(lanef == i1, w1, 0.0) + jnp.where(lanef == i2, w2, 0.0)
    return cmb, gidx


def _out_kernel(x_ref, yr_ref, ya_ref, mod_ref, n1_ref, wgate_ref, wor_ref, wom_ref, wout_ref,
                n2_ref, rw_ref, rb_ref, cnt0_ref, x1_ref, hx_ref, rinfo_ref, cnt_ref, run_ref):
    @pl.when(pl.program_id(0) == 0)
    def _():
        run_ref[...] = cnt0_ref[...]

    x = x_ref[...]
    hb = _modulated_norm(x, n1_ref[...], mod_ref[0, 1:2, :], mod_ref[0, 0:1, :]).astype(BF)
    gl = _dot(hb, wgate_ref[...])
    merged2 = ((jnp.tanh(gl[:, :D]) + 1.0) * _dot(yr_ref[...], wor_ref[...])
               + (jnp.tanh(gl[:, D:]) + 1.0) * _dot(ya_ref[...], wom_ref[...]))
    mix = _dot(merged2.astype(BF), wout_ref[...])
    x1 = x + mod_ref[0, 2:3, :] * mix
    x1_ref[...] = x1
    h2 = _modulated_norm(x1, n2_ref[...], mod_ref[0, 4:5, :], mod_ref[0, 3:4, :])
    h2_hi = h2.astype(BF)
    h2_lo = (h2 - h2_hi.astype(F32)).astype(BF)
    part = _dot(h2_hi, rw_ref[...])
    logits = part[:, :LANES] + part[:, LANES:] + _dot(h2_lo, rw_ref[:, :LANES]) + rb_ref[...]
    cmb, gidx = _route(logits)
    hx_ref[:, :D] = h2
    hx_ref[:, D:] = cmb

    lanef = lax.broadcasted_iota(jnp.int32, cmb.shape, 1).astype(F32)
    ghot = jnp.where(lanef == gidx, 1.0, 0.0)
    r_i = lax.broadcasted_iota(jnp.int32, (TM, TM), 0)
    c_i = lax.broadcasted_iota(jnp.int32, (TM, TM), 1)
    tri = jnp.where(r_i > c_i, 1.0, 0.0).astype(BF)
    before = _dot(tri, ghot.astype(BF)) + run_ref[0:1, :]
    rank = jnp.sum(before * ghot, axis=-1, keepdims=True)
    rinfo_ref[...] = jnp.where(lanef == 0.0, gidx, jnp.where(lanef == 1.0, rank, 0.0))
    run = run_ref[...] + jnp.sum(ghot, axis=0, keepdims=True)
    run_ref[...] = run
    cnt_ref[...] = run


def _merge_out(x, yr, ya, mod, mod_row, wts, cnt0):
    n = x.shape[0]
    tile = lambda w: pl.BlockSpec((TM, w), lambda i: (i, 0))
    return pl.pallas_call(
        _out_kernel,
        grid=(n // TM,),
        in_specs=[tile(D), tile(DR), tile(DH),
                  pl.BlockSpec((1, 6, D), lambda i: (mod_row(i), 0, 0)), _resident((1, D)),
                  _resident((D, 2 * D)), _resident((DR, D)), _resident((DH, D)), _resident((D, D)),
                  _resident((1, D)), _resident((D, 2 * LANES)), _resident((1, LANES)),
                  _resident((SUBLANES, LANES))],
        out_specs=[tile(D), tile(XW), tile(LANES), pl.BlockSpec((SUBLANES, LANES), lambda i: (0, 0))],
        out_shape=[jax.ShapeDtypeStruct((n, D), F32), jax.ShapeDtypeStruct((n, XW), F32),
                   jax.ShapeDtypeStruct((n, LANES), F32),
                   jax.ShapeDtypeStruct((SUBLANES, LANES), F32)],
        scratch_shapes=[pltpu.VMEM((SUBLANES, LANES), F32)],
        compiler_params=_cparams(("arbitrary",)),
        name="merge_out",
    )(x, yr, ya, mod, wts["n1"], wts["w_gate"], wts["w_o_rnn"], wts["w_o_mla"], wts["w_out"],
      wts["n2"], wts["router_w"], wts["router_b"], cnt0)


def _row_copy(src, src_row, dst, dst_row, sem):
    return pltpu.make_async_copy(src.at[pl.ds(src_row, 1)], dst.at[pl.ds(dst_row, 1)], sem)


def _dispatch_kernel(pos_ref, hx_ref, xs_in_ref, xs_ref, sem):
    del xs_in_ref
    base = pl.program_id(0) * TD

    def body(r, carry):
        _row_copy(hx_ref, r, xs_ref, pos_ref[base + r], sem).start()
        return carry

    lax.fori_loop(0, TD, body, 0, unroll=8)
    pltpu.make_async_copy(hx_ref, xs_ref.at[pl.ds(0, TD)], sem).wait()


def _dispatch(pos, hx, xs):
    n = hx.shape[0]
    return pl.pallas_call(
        _dispatch_kernel,
        grid_spec=pltpu.PrefetchScalarGridSpec(
            num_scalar_prefetch=1,
            grid=(n // TD,),
            in_specs=[pl.BlockSpec((TD, XW), lambda i, pos: (i, 0)),
                      pl.BlockSpec(memory_space=pl.ANY)],
            out_specs=pl.BlockSpec(memory_space=pl.ANY),
            scratch_shapes=[pltpu.SemaphoreType.DMA(())],
        ),
        out_shape=jax.ShapeDtypeStruct(xs.shape, xs.dtype),
        input_output_aliases={2: 0},
        compiler_params=_cparams(("arbitrary",)),
        name="dispatch",
    )(pos, hx, xs)


def _moe_kernel(tb_ref, tg_ref, nt_ref, xs_ref, w1_ref, w3_ref, w2_ref, o_ref, acc_ref):
    j = pl.program_id(0)
    pair = pl.program_id(1)

    @pl.when(j < nt_ref[0])
    def _():
        xt = xs_ref[:, :D].astype(BF)
        cmb = xs_ref[:, D:]
        lane = lax.broadcasted_iota(jnp.int32, cmb.shape, 1)
        first = tg_ref[j] * EPG + pair * EXPERTS_PER_STEP
        hidden = []
        for u in range(EXPERTS_PER_STEP):
            a = _dot(xt, w1_ref[u].astype(BF))
            he = (a * _sigmoid(a)) * _dot(xt, w3_ref[u].astype(BF))
            ce = jnp.sum(jnp.where(lane == first + u, cmb, 0.0), axis=-1, keepdims=True)
            hidden.append((he * ce).astype(BF))
        w2 = w2_ref[...].reshape(EXPERTS_PER_STEP * DE, D).astype(BF)
        y = _dot(jnp.concatenate(hidden, axis=1), w2)

        @pl.when(pair == 0)
        def _():
            acc_ref[...] = y

        @pl.when(pair == EPG // EXPERTS_PER_STEP - 1)
        def _():
            o_ref[...] = acc_ref[...] + y

    @pl.when((j >= nt_ref[0]) & (pair == EPG // EXPERTS_PER_STEP - 1))
    def _():
        o_ref[...] = jnp.zeros_like(o_ref)


def _experts(tile_block, tile_group, n_tiles, xs, wts):
    m = xs.shape[0]
    steps = EPG // EXPERTS_PER_STEP
    assert steps == 2, "the kernel keeps one partial sum: first step stores it, second adds and writes"

    def w_idx(j, e, tb, tg, nt):
        return (tg[j] * steps + jnp.where(j < nt[0], e, steps - 1), 0, 0)

    return pl.pallas_call(
        _moe_kernel,
        grid_spec=pltpu.PrefetchScalarGridSpec(
            num_scalar_prefetch=3,
            grid=(m // TMOE, steps),
            in_specs=[pl.BlockSpec((TMOE, XW), lambda j, e, tb, tg, nt: (tb[j], 0)),
                      pl.BlockSpec((EXPERTS_PER_STEP, D, DE), w_idx),
                      pl.BlockSpec((EXPERTS_PER_STEP, D, DE), w_idx),
                      pl.BlockSpec((EXPERTS_PER_STEP, DE, D), w_idx)],
            out_specs=pl.BlockSpec((TMOE, D), lambda j, e, tb, tg, nt: (j, 0)),
            scratch_shapes=[pltpu.VMEM((TMOE, D), F32)],
        ),
        out_shape=jax.ShapeDtypeStruct((m, D), F32),
        compiler_params=_cparams(("arbitrary", "arbitrary")),
        name="experts",
    )(tile_block, tile_group, n_tiles, xs, wts["exp_w1"], wts["exp_w3"], wts["exp_w2"])


def _combine_kernel(pos_ref, x1_ref, mod_ref, ys_ref, o_ref, buf_ref, sem, *, n_steps):
    i = pl.program_id(0)

    def issue(step, slot):
        def body(r, carry):
            _row_copy(ys_ref, pos_ref[step * TD + r], buf_ref.at[slot], r, sem.at[slot]).start()
            return carry

        lax.fori_loop(0, TD, body, 0, unroll=8)

    @pl.when(i == 0)
    def _():
        issue(0, 0)

    @pl.when(i + 1 < n_steps)
    def _():
        issue(i + 1, (i + 1) % 2)

    slot = i % 2
    pltpu.make_async_copy(ys_ref.at[pl.ds(0, TD)], buf_ref.at[slot], sem.at[slot]).wait()
    o_ref[...] = x1_ref[...] + mod_ref[0, 5:6, :] * buf_ref[slot]


def _combine(pos, x1, mod, mod_row, ys):
    n = x1.shape[0]
    n_steps = n // TD
    return pl.pallas_call(
        functools.partial(_combine_kernel, n_steps=n_steps),
        grid_spec=pltpu.PrefetchScalarGridSpec(
            num_scalar_prefetch=1,
            grid=(n_steps,),
            in_specs=[pl.BlockSpec((TD, D), lambda i, pos: (i, 0)),
                      pl.BlockSpec((1, 6, D), lambda i, pos: (mod_row(i), 0, 0)),
                      pl.BlockSpec(memory_space=pl.ANY)],
            out_specs=pl.BlockSpec((TD, D), lambda i, pos: (i, 0)),
            scratch_shapes=[pltpu.VMEM((2, TD, D), F32), pltpu.SemaphoreType.DMA((2,))],
        ),
        out_shape=jax.ShapeDtypeStruct((n, D), F32),
        compiler_params=_cparams(("arbitrary",)),
        name="combine",
    )(pos, x1, mod, ys)


def _slots(rinfo, offsets):
    group = rinfo[..., 0:1]
    off = jnp.sum(jnp.where(group == jnp.arange(NG, dtype=F32), offsets.astype(F32), 0.0), axis=-1)
    return (off + rinfo[..., 1]).astype(jnp.int32).reshape(-1)


def _group_layout(counts, max_tiles):
    padded = ((counts + TMOE - 1) // TMOE) * TMOE
    ends = jnp.cumsum(padded)
    offsets = ends - padded
    n_tiles = (ends[-1] // TMOE).astype(jnp.int32)
    tile = jnp.minimum(jnp.arange(max_tiles, dtype=jnp.int32), jnp.maximum(n_tiles - 1, 0))
    tile_group = jnp.sum((tile[:, None] * TMOE >= ends[None, :]).astype(jnp.int32), axis=1)
    return offsets, tile, tile_group, n_tiles.reshape(1)


def _pad_heads(w, perm=None, rotary_only=False):
    lead = w.shape[:-1]
    per = w.shape[-1] // NH
    w = w.reshape(lead + (NH, per))
    if perm is not None:
        nope = jnp.zeros_like(w[..., :NOPE]) if rotary_only else w[..., :NOPE]
        w = jnp.concatenate([nope, w[..., NOPE:][..., perm]], axis=-1)
    w = jnp.pad(w, [(0, 0)] * len(lead) + [(0, 0), (0, HP - per)])
    return w.reshape(lead + (NH * HP,))


def _pad_gain(g, perm, rotary_only=False):
    nope = jnp.zeros((NOPE,), F32) if rotary_only else g[:NOPE]
    g = jnp.concatenate([nope, g[NOPE:][perm], jnp.zeros((HP - QK,), F32)])
    return g.reshape(1, HP)


def _block_diag(w):
    per = BDW // LRU_BLOCK
    w = w.reshape(DR // BDW, per, LRU_BLOCK, LRU_BLOCK)
    bd = jnp.einsum("jpab,pq->jpaqb", w, jnp.eye(per, dtype=w.dtype))
    return bd.reshape(DR // BDW, BDW, BDW)


def _prepare_shared(l, p):
    w_in = p["w_in"][l]
    o1, o2, o3, o4, o5 = DR, 2 * DR, 2 * DR + QL, 2 * DR + QL + KVL, 2 * DR + QL + KVL + ROPE
    bd = jnp.stack([_block_diag(p["lru_wa"][l, 0]), _block_diag(p["lru_wx"][l, 0]),
                    _block_diag(p["lru_wa"][l, 1]), _block_diag(p["lru_wx"][l, 1])])
    bd = (0.5 * bd).astype(BF)
    wom = p["w_o_mla"][l].reshape(NH, VD, D)
    wom = jnp.pad(wom, ((0, 0), (0, HP - VD), (0, 0))).reshape(DH, D)
    router_w = jnp.concatenate([p["router_we"][l], p["router_wg"][l],
                                jnp.zeros((D, LANES - NE - NG), F32)], axis=1)
    router_b = jnp.concatenate([p["router_be"][l], p["router_bg"][l],
                                jnp.zeros((LANES - NE - NG,), F32)]).reshape(1, LANES)
    router_hi = router_w.astype(BF)
    router_lo = (router_w - router_hi.astype(F32)).astype(BF)
    router_w = jnp.concatenate([router_hi, router_lo], axis=1)
    return {
        "n1": p["norm1_g"][l].reshape(1, D), "n2": p["norm2_g"][l].reshape(1, D),
        "w_x": w_in[:, :o1].astype(BF), "w_g": w_in[:, o1:o2].astype(BF),
        "w_q": w_in[:, o2:o3].astype(BF), "w_kv": w_in[:, o3:o4], "w_kr": w_in[:, o4:o5],
        "w_gate": (0.5 * w_in[:, o5:]).astype(BF),
        "qan": p["q_a_norm"][l].reshape(1, QL), "kvan": p["kv_a_norm"][l].reshape(1, KVL),
        "w_uk": _pad_heads(p["w_uk"][l]).astype(BF),
        "w_uv": _pad_heads(p["w_uv"][l]).astype(BF),
        "conv_w": p["conv_w"][l], "conv_b": p["conv_b"][l].reshape(1, DR), "bd": bd,
        "lru_ba": 0.5 * p["lru_ba"][l], "lru_bx": 0.5 * p["lru_bx"][l], "lru_lam": p["lru_lam"][l],
        "w_o_rnn": p["w_o_rnn"][l].astype(BF), "w_o_mla": wom.astype(BF), "w_out": (0.5 * p["w_out"][l]).astype(BF),
        "router_w": router_w, "router_b": router_b,
        "exp_w1": p["exp_w1"][l], "exp_w3": p["exp_w3"][l], "exp_w2": p["exp_w2"][l],
    }


def _with_rope_order(l, p, shared, perm, rotary):
    w = dict(shared)
    zeros = lambda n: jnp.zeros((D, n), F32)
    rope_block = lambda order: [zeros(NOPE), shared["w_kr"][:, order], zeros(HP - QK)]
    kvr = [shared["w_kv"]] + rope_block(perm)
    w["w_uq"] = _pad_heads(p["w_uq"][l], perm).astype(BF)
    w["gq"] = _pad_gain(p["q_norm"][l], perm)
    w["gk"] = _pad_gain(p["k_norm"][l], perm)
    if rotary:
        pair = np.concatenate([perm[ROPE // 2:], perm[:ROPE // 2]])
        kvr += rope_block(pair)
        w["w_uq_pair"] = _pad_heads(p["w_uq"][l], pair, rotary_only=True).astype(BF)
        w["gq_pair"] = _pad_gain(p["q_norm"][l], pair, rotary_only=True)
        w["gk_pair"] = _pad_gain(p["k_norm"][l], pair, rotary_only=True)
    w["w_kvr"] = jnp.concatenate(kvr, axis=1).astype(BF)
    return w


def _rope_tables(n_tokens, perm):
    rows = n_tokens // GRID_W
    row = np.repeat(np.arange(rows), GRID_W).astype(np.float32)
    col = np.tile(np.arange(GRID_W), rows).astype(np.float32)
    axis_dim = ROPE // 2
    inv = (np.float32(ROPE_BASE) ** (-np.arange(0, axis_dim, 2, dtype=np.float32) / axis_dim)).astype(np.float32)
    ang = np.concatenate([row[:, None] * inv, col[:, None] * inv], axis=-1).astype(np.float32)
    cos, sin = np.cos(ang), np.sin(ang)
    ones = lambda n: np.ones((n_tokens, n), np.float32)
    zeros = lambda n: np.zeros((n_tokens, n), np.float32)
    cos_t = np.concatenate([ones(NOPE), cos, cos, ones(HP - QK)], axis=1)
    sin_t = np.concatenate([zeros(NOPE), -sin, sin, zeros(HP - QK)], axis=1)
    return jnp.asarray(cos_t, F32), jnp.asarray(sin_t, F32)


def kernel(x_prompt, x_sample, cache_mla_ckv, cache_mla_krope, state_rglru, c, c_ctx, norm1_g, norm2_g, w_mod, b_mod, w_in, conv_w, conv_b, lru_wa, lru_ba, lru_wx, lru_bx, lru_lam, q_a_norm, kv_a_norm, w_uq, w_uk, w_uv, q_norm, k_norm, w_o_rnn, w_o_mla, w_out, router_wg, router_bg, router_we, router_be, exp_w1, exp_w3, exp_w2):
    p = dict(norm1_g=norm1_g, norm2_g=norm2_g, w_in=w_in, conv_w=conv_w, conv_b=conv_b,
             lru_wa=lru_wa, lru_ba=lru_ba, lru_wx=lru_wx, lru_bx=lru_bx, lru_lam=lru_lam,
             q_a_norm=q_a_norm, kv_a_norm=kv_a_norm, w_uq=w_uq, w_uk=w_uk, w_uv=w_uv,
             q_norm=q_norm, k_norm=k_norm, w_o_rnn=w_o_rnn, w_o_mla=w_o_mla, w_out=w_out,
             router_wg=router_wg, router_bg=router_bg, router_we=router_we, router_be=router_be,
             exp_w1=exp_w1, exp_w3=exp_w3, exp_w2=exp_w2)
    depth = w_in.shape[0]
    nb, seq, _ = x_prompt.shape
    db, dseq, _ = x_sample.shape
    ident = np.arange(ROPE)
    halves = np.concatenate([np.arange(0, ROPE, 2), np.arange(1, ROPE, 2)])
    rope_tabs = _rope_tables(dseq, halves)
    cond8 = jnp.concatenate([c_ctx[None, :], c, jnp.zeros((SUBLANES - 1 - db, D), F32)], axis=0)
    ctx_row = lambda tile_rows: (lambda i: 0)
    lat_row = lambda tile_rows: (lambda i: (i * tile_rows) // dseq + 1)
    n_ctx, n_lat = nb * seq, db * dseq
    max_tiles = (n_ctx + n_lat) // TMOE + NG
    per_seq = lambda arrs, b, t: [a.reshape(b, t, a.shape[-1]) for a in arrs]
    flat = lambda a: a.reshape(-1, a.shape[-1])

    y_prompt, y_sample = x_prompt.reshape(n_ctx, D), x_sample.reshape(n_lat, D)
    ckv_list, krope_list, rnn_list = [], [], []
    for l in range(depth):
        shared = _prepare_shared(l, p)
        w_ctx = _with_rope_order(l, p, shared, ident, False)
        w_lat = _with_rope_order(l, p, shared, halves, True)
        mod = _modulation(cond8, w_mod[l], b_mod[l]).reshape(SUBLANES, 6, D)

        xr, gg, q, k, v, ckv, kro = _projections(y_prompt, mod, ctx_row(TM), w_ctx, None, True)
        xr, gg, q, k, v = per_seq([xr, gg, q, k, v], nb, seq)
        yr, h_fin = _rglru(xr, gg, shared, None, True)
        ya = _attention(q, k, v, None, None, NH)
        x1_c, hx_c, ri_c, cnt_c = _merge_out(y_prompt, flat(yr), flat(ya), mod, ctx_row(TM), shared,
                                             jnp.zeros((SUBLANES, LANES), F32))
        ckv_list.append(ckv.reshape(nb, seq, KVL))
        krope_list.append(kro.reshape(nb, seq, ROPE))
        rnn_list.append(h_fin)

        krp_cache = jnp.pad(cache_mla_krope[:, l][..., halves], ((0, 0), (0, 0), (NOPE, HP - QK)))
        kc, vc = _cache_keys_values(cache_mla_ckv[:, l], krp_cache, w_lat)
        xr, gg, q, k, v = _projections(y_sample, mod, lat_row(TM), w_lat, rope_tabs, False)
        xr, gg, q, k, v = per_seq([xr, gg, q, k, v], db, dseq)
        yr, _ = _rglru(xr, gg, shared, state_rglru[:, l], False)
        ya = _attention(q, k, v, kc, vc, 2)
        x1_l, hx_l, ri_l, cnt_l = _merge_out(y_sample, flat(yr), flat(ya), mod, lat_row(TM), shared, cnt_c)

        offsets, tile_block, tile_group, n_tiles = _group_layout(cnt_l[0, :NG].astype(jnp.int32), max_tiles)
        pos_c, pos_l = _slots(ri_c, offsets), _slots(ri_l, offsets)
        xs = jnp.zeros((max_tiles * TMOE, XW), F32)
        xs = _dispatch(pos_c, hx_c, xs)
        xs = _dispatch(pos_l, hx_l, xs)
        ys = _experts(tile_block, tile_group, n_tiles, xs, shared)
        y_prompt = _combine(pos_c, x1_c, mod, ctx_row(TD), ys)
        y_sample = _combine(pos_l, x1_l, mod, lat_row(TD), ys)

    y_prompt, y_sample = y_prompt.reshape(nb, seq, D), y_sample.reshape(db, dseq, D)

    return (y_prompt, y_sample, jnp.stack(ckv_list, axis=1), jnp.stack(krope_list, axis=1),
            jnp.stack(rnn_list, axis=1))
```

```python
import functools
import math

import numpy as np
import jax
import jax.numpy as jnp
from jax import lax
from jax.experimental import pallas as pl
from jax.experimental.pallas import tpu as pltpu

D = 1024
DR = 1024
QL = 384
KVL = 256
NH = 8
NOPE = 64
ROPE = 32
QK = NOPE + ROPE
VD = 64
HP = 128
DH = NH * HP
GRID_W = 64
ROPE_BASE = 10000.0
EPS = 1e-6
TINY = 1e-30
LRU_C = 8.0
LRU_BLOCK = 64
BDW = 256
CH = 512
NLT = CH // 128
NG = 4
EPG = 4
NE = NG * EPG
DE = 512
LANES = 128
SUBLANES = 8
TM = 512
TS = 640
TMOE = 512
EXPERTS_PER_STEP = 2
XW = D + 128
VMEM_LIMIT = 52 * 1024 * 1024
BF = jnp.bfloat16
F32 = jnp.float32


def _cparams(sem):
    return pltpu.CompilerParams(dimension_semantics=sem, vmem_limit_bytes=VMEM_LIMIT)


def _dot(a, b):
    return jnp.dot(a, b, preferred_element_type=F32)


def _dot_nt(a, b):
    return lax.dot_general(a, b, (((1,), (1,)), ((), ())), preferred_element_type=F32)


def _rms(x, g, width):
    ms = jnp.sum(x * x, axis=-1, keepdims=True) * (1.0 / width)
    return x * lax.rsqrt(ms + EPS) * g


def _modulated_norm(x, g, scale, shift):
    return _rms(x, g, D) * (1.0 + scale) + shift


def _mod_kernel(c_ref, w_ref, b_ref, o_ref):
    c = c_ref[...]
    s = c * jax.nn.sigmoid(c)
    o_ref[...] = _dot(s, w_ref[...]) + b_ref[...]


def _modulation(cond8, w_mod, b_mod):
    n = w_mod.shape[1]
    return pl.pallas_call(
        _mod_kernel,
        grid=(n // D,),
        in_specs=[
            pl.BlockSpec((SUBLANES, D), lambda j: (0, 0)),
            pl.BlockSpec((D, D), lambda j: (0, j)),
            pl.BlockSpec((1, D), lambda j: (0, j)),
        ],
        out_specs=pl.BlockSpec((SUBLANES, D), lambda j: (0, j)),
        out_shape=jax.ShapeDtypeStruct((SUBLANES, n), F32),
        compiler_params=_cparams(("arbitrary",)),
        name="modulation",
    )(cond8, w_mod, b_mod.reshape(1, n))


def _head_norm(xh, gain, cos=None, partner_scaled=None):
    ms = jnp.sum(xh * xh, axis=-1, keepdims=True) * (1.0 / QK)
    rs = lax.rsqrt(ms + EPS)
    y = xh * rs * gain
    if cos is None:
        return y
    return y * cos + partner_scaled * rs


def _keys_values(ckv, krp, wuk_ref, wuv_ref, gk, cos, partner_scaled, k_ref, v_ref):
    cb = ckv.astype(BF)
    kn = _dot(cb, wuk_ref[...])
    v_ref[...] = _dot(cb, wuv_ref[...]).astype(BF)
    for h in range(NH):
        kh = kn[:, h * HP:(h + 1) * HP] + krp
        k_ref[:, h * HP:(h + 1) * HP] = _head_norm(kh, gk, cos, partner_scaled).astype(BF)


def _proj_kernel(*refs, rope, emit_cache):
    it = iter(refs)
    x_ref, mod_ref, n1_ref = next(it), next(it), next(it)
    wx_ref, wg_ref, wq_ref, wkvr_ref = next(it), next(it), next(it), next(it)
    qan_ref, kvan_ref, wuq_ref, gq_ref = next(it), next(it), next(it), next(it)
    wuk_ref, gk_ref, wuv_ref = next(it), next(it), next(it)
    if rope:
        wuqs_ref, gqs_ref, gks_ref, cos_ref, sins_ref = (next(it) for _ in range(5))
    xr_ref, gg_ref, q_ref, k_ref, v_ref = (next(it) for _ in range(5))
    if emit_cache:
        ckv_ref, kro_ref = next(it), next(it)

    hb = _modulated_norm(x_ref[...], n1_ref[...], mod_ref[0, 1:2, :], mod_ref[0, 0:1, :]).astype(BF)
    xr_ref[...] = _dot(hb, wx_ref[...]).astype(BF)
    gg_ref[...] = jax.nn.gelu(_dot(hb, wg_ref[...])).astype(BF)

    qnb = _rms(_dot(hb, wq_ref[...]), qan_ref[...], QL).astype(BF)
    q = _dot(qnb, wuq_ref[...])
    gq = gq_ref[...]
    cos = q_partner = q_pair_scale = None
    if rope:
        cos, sins = cos_ref[...], sins_ref[...]
        q_partner = _dot(qnb, wuqs_ref[...])
        q_pair_scale = gqs_ref[...] * sins
    for hd in range(NH):
        cols = slice(hd * HP, (hd + 1) * HP)
        partner = q_partner[:, cols] * q_pair_scale if rope else None
        q_ref[:, cols] = _head_norm(q[:, cols], gq, cos, partner).astype(BF)

    kvr = _dot(hb, wkvr_ref[...])
    ckv = _rms(kvr[:, :KVL], kvan_ref[...], KVL)
    krp = kvr[:, KVL:KVL + HP]
    k_partner = kvr[:, KVL + HP:KVL + 2 * HP] * (gks_ref[...] * sins) if rope else None
    if emit_cache:
        ckv_ref[...] = ckv
        kro_ref[...] = krp[:, NOPE:NOPE + ROPE]
    _keys_values(ckv, krp, wuk_ref, wuv_ref, gk_ref[...], cos, k_partner, k_ref, v_ref)


def _resident(shape):
    return pl.BlockSpec(shape, lambda i: (0,) * len(shape), pipeline_mode=pl.Buffered(1))


def _projections(x, mod, mod_row, wts, rope_tabs, emit_cache):
    n = x.shape[0]
    rope = rope_tabs is not None
    tile = lambda w: pl.BlockSpec((TM, w), lambda i: (i, 0))
    in_specs = [
        tile(D),
        pl.BlockSpec((1, 6, D), lambda i: (mod_row(i), 0, 0)),
        _resident((1, D)),
        _resident((D, DR)), _resident((D, DR)), _resident((D, QL)), _resident(wts["w_kvr"].shape),
        _resident((1, QL)), _resident((1, KVL)), _resident((QL, DH)), _resident((1, HP)),
        _resident((KVL, DH)), _resident((1, HP)), _resident((KVL, DH)),
    ]
    args = [x, mod, wts["n1"], wts["w_x"], wts["w_g"], wts["w_q"], wts["w_kvr"],
            wts["qan"], wts["kvan"], wts["w_uq"], wts["gq"], wts["w_uk"], wts["gk"], wts["w_uv"]]
    if rope:
        tiles_per_seq = rope_tabs[0].shape[0] // TM
        in_specs += [_resident((QL, DH)), _resident((1, HP)), _resident((1, HP))]
        in_specs += [pl.BlockSpec((TM, HP), lambda i: (i % tiles_per_seq, 0))] * 2
        args += [wts["w_uq_pair"], wts["gq_pair"], wts["gk_pair"]] + list(rope_tabs)
    out_specs = [tile(DR), tile(DR), tile(DH), tile(DH), tile(DH)]
    out_shape = [jax.ShapeDtypeStruct((n, DR), BF),
                 jax.ShapeDtypeStruct((n, DR), BF), jax.ShapeDtypeStruct((n, DH), BF),
                 jax.ShapeDtypeStruct((n, DH), BF), jax.ShapeDtypeStruct((n, DH), BF)]
    if emit_cache:
        out_specs += [tile(KVL), tile(ROPE)]
        out_shape += [jax.ShapeDtypeStruct((n, KVL), F32), jax.ShapeDtypeStruct((n, ROPE), F32)]
    return pl.pallas_call(
        functools.partial(_proj_kernel, rope=rope, emit_cache=emit_cache),
        grid=(n // TM,),
        in_specs=in_specs,
        out_specs=out_specs,
        out_shape=out_shape,
        compiler_params=_cparams(("arbitrary",)),
        name="projections",
    )(*args)


def _cache_kv_kernel(ckv_ref, krp_ref, wuk_ref, gk_ref, wuv_ref, k_ref, v_ref):
    _keys_values(ckv_ref[0], krp_ref[0], wuk_ref, wuv_ref, gk_ref[...], None, None,
                 k_ref.at[0], v_ref.at[0])


def _cache_keys_values(ckv, krp, wts):
    b, s, _ = ckv.shape
    full = lambda shape: pl.BlockSpec(shape, lambda i: (0,) * len(shape))
    return pl.pallas_call(
        _cache_kv_kernel,
        grid=(b,),
        in_specs=[pl.BlockSpec((1, s, KVL), lambda i: (i, 0, 0)),
                  pl.BlockSpec((1, s, HP), lambda i: (i, 0, 0)),
                  full((KVL, DH)), full((1, HP)), full((KVL, DH))],
        out_specs=[pl.BlockSpec((1, s, DH), lambda i: (i, 0, 0))] * 2,
        out_shape=[jax.ShapeDtypeStruct((b, s, DH), BF)] * 2,
        compiler_params=_cparams(("arbitrary",)),
        name="cache_keys_values",
    )(ckv, krp, wts["w_uk"], wts["gk"], wts["w_uv"])


def _sigmoid(x):
    return 0.5 * jnp.tanh(0.5 * x) + 0.5


def _tile_scan(a, b, forward):
    row = lax.broadcasted_iota(jnp.int32, a.shape, 0)
    for k in (1, 2, 4):
        if forward:
            shift, valid = k, row >= k
        else:
            shift, valid = SUBLANES - k, row < SUBLANES - k
        a_prev = jnp.where(valid, pltpu.roll(a, shift, 0), 1.0)
        b_prev = jnp.where(valid, pltpu.roll(b, shift, 0), 0.0)
        b = a * b_prev + b
        a = a * a_prev
    return a, b


def _rglru_kernel(*refs, t, has_h0, emit_state):
    it = iter(refs)
    xr_ref, gg_ref, cw_ref, cb_ref, bd_ref, ba_ref, bx_ref, lam_ref = (next(it) for _ in range(8))
    h0_ref = next(it) if has_h0 else None
    y_ref = next(it)
    hf_ref = next(it) if emit_state else None
    a_scr, b_scr, h_scr, xpad_scr = (next(it) for _ in range(4))

    n_tiles = t // SUBLANES
    pad = SUBLANES
    zero_rows = jnp.zeros((pad, CH), F32)
    xpad_scr[0:pad, :] = zero_rows
    xpad_scr[pad + t:pad + t + pad, :] = zero_rows
    xpad_scr[pad:pad + t, :] = xr_ref[0].astype(F32)
    xpad = xpad_scr[...]
    rows_all = t + 2 * pad
    xc = cb_ref[...] + xpad[pad:pad + t, :] * cw_ref[1:2, :]
    for tap, shift in ((0, 1), (2, rows_all - 1), (3, rows_all - 2)):
        xc = xc + pltpu.roll(xpad, shift, 0)[pad:pad + t, :] * cw_ref[tap:tap + 1, :]
    for s in range(CH // BDW):
        cols = slice(s * BDW, (s + 1) * BDW)
        xs = xc[:, cols]
        xsb = xs.astype(BF)
        xh = 0.5 * xs
        for d in range(2):
            tr = jnp.tanh(_dot(xsb, bd_ref[2 * d, s]) + ba_ref[d:d + 1, cols])
            ti = jnp.tanh(_dot(xsb, bd_ref[2 * d + 1, s]) + bx_ref[d:d + 1, cols])
            nl = -lam_ref[d:d + 1, cols]
            softplus = jnp.maximum(nl, 0.0) + jnp.log(1.0 + jnp.exp(-jnp.abs(nl)))
            ch = (-0.5 * LRU_C) * softplus
            a = jnp.exp(tr * ch + ch)
            z = 1.0 - a * a
            root = z * lax.rsqrt(jnp.maximum(z, TINY))
            a_scr[d, :, cols] = a
            b_scr[d, :, cols] = root * (ti * xh + xh)

    def step(i, carry):
        out = []
        for d in range(2):
            tile = i if d == 0 else n_tiles - 1 - i
            rows = pl.ds(pl.multiple_of(tile * SUBLANES, SUBLANES), SUBLANES)
            for c in range(NLT):
                lanes = slice(c * LANES, (c + 1) * LANES)
                decay, local = _tile_scan(a_scr[d, rows, lanes], b_scr[d, rows, lanes], d == 0)
                h = local + decay * carry[d * NLT + c]
                h_scr[d, rows, lanes] = h
                last = h[SUBLANES - 1:SUBLANES, :] if d == 0 else h[0:1, :]
                out.append(jnp.broadcast_to(last, (SUBLANES, LANES)))
        return tuple(out)

    init = []
    for d in range(2):
        for c in range(NLT):
            if has_h0:
                h0 = h0_ref[0, d:d + 1, c * LANES:(c + 1) * LANES]
                init.append(jnp.broadcast_to(h0, (SUBLANES, LANES)))
            else:
                init.append(jnp.zeros((SUBLANES, LANES), F32))
    final = lax.fori_loop(0, n_tiles, step, tuple(init), unroll=2)

    if emit_state:
        for d in range(2):
            for c in range(NLT):
                hf_ref[0, d:d + 1, c * LANES:(c + 1) * LANES] = final[d * NLT + c][0:1, :]
    y_ref[0] = ((h_scr[0] + h_scr[1]) * gg_ref[0].astype(F32)).astype(BF)


def _rglru(xr, gg, wts, h0, emit_state):
    b, t, _ = xr.shape
    nc = DR // CH
    has_h0 = h0 is not None
    chunk = lambda r: pl.BlockSpec((r, CH), lambda i, j: (0, j))
    seq = pl.BlockSpec((1, t, CH), lambda i, j: (i, 0, j))
    state = pl.BlockSpec((1, 2, CH), lambda i, j: (i, 0, j))
    in_specs = [seq, seq, chunk(4), chunk(1),
                pl.BlockSpec((4, CH // BDW, BDW, BDW), lambda i, j: (0, j, 0, 0)),
                chunk(2), chunk(2), chunk(2)]
    args = [xr, gg, wts["conv_w"], wts["conv_b"], wts["bd"], wts["lru_ba"], wts["lru_bx"], wts["lru_lam"]]
    if has_h0:
        in_specs.append(state)
        args.append(h0)
    out_specs = [seq]
    out_shape = [jax.ShapeDtypeStruct((b, t, DR), BF)]
    if emit_state:
        out_specs.append(state)
        out_shape.append(jax.ShapeDtypeStruct((b, 2, DR), F32))
    res = pl.pallas_call(
        functools.partial(_rglru_kernel, t=t, has_h0=has_h0, emit_state=emit_state),
        grid=(b, nc),
        in_specs=in_specs,
        out_specs=out_specs,
        out_shape=out_shape,
        scratch_shapes=[pltpu.VMEM((2, t, CH), F32)] * 3 + [pltpu.VMEM((t + 2 * SUBLANES, CH), F32)],
        compiler_params=_cparams(("arbitrary", "arbitrary")),
        name="rglru",
    )(*args)
    return res if emit_state else (res[0], None)


def _attn_kernel(*refs, t, n_heads, has_ctx, q_block):
    it = iter(refs)
    q_ref, k_ref, v_ref = next(it), next(it), next(it)
    kc_ref = vc_ref = None
    if has_ctx:
        kc_ref, vc_ref = next(it), next(it)
    o_ref = next(it)
    log2_scale = (QK ** -0.5) * math.log2(math.e)
    for hd in range(n_heads):
        cols = slice(hd * HP, (hd + 1) * HP)
        k = k_ref[0, :, cols]
        v = v_ref[0, :, cols]
        if has_ctx:
            kc = kc_ref[0, :, cols]
            vc = vc_ref[0, :, cols]
        for qb in range(t // q_block):
            rows = slice(qb * q_block, (qb + 1) * q_block)
            q = q_ref[0, rows, cols]
            s = _dot_nt(q, k) * log2_scale
            m = jnp.max(s, axis=-1, keepdims=True)
            if has_ctx:
                sc = _dot_nt(q, kc) * log2_scale
                m = jnp.maximum(m, jnp.max(sc, axis=-1, keepdims=True))
            p = jnp.exp2(s - m)
            den = jnp.sum(p, axis=-1, keepdims=True)
            o = _dot(p.astype(BF), v)
            if has_ctx:
                pc = jnp.exp2(sc - m)
                den = den + jnp.sum(pc, axis=-1, keepdims=True)
                o = o + _dot(pc.astype(BF), vc)
            o_ref[0, rows, cols] = (o / den).astype(BF)


def _attention(q, k, v, kc, vc, heads_per_step):
    b, t, _ = q.shape
    has_ctx = kc is not None
    w = heads_per_step * HP
    blk = lambda n: pl.BlockSpec((1, n, w), lambda i, j: (i, 0, j))
    in_specs = [blk(t), blk(t), blk(t)]
    args = [q, k, v]
    if has_ctx:
        in_specs += [blk(kc.shape[1])] * 2
        args += [kc, vc]
    return pl.pallas_call(
        functools.partial(_attn_kernel, t=t, n_heads=heads_per_step, has_ctx=has_ctx, q_block=min(t, 256)),
        grid=(b, NH // heads_per_step),
        in_specs=in_specs,
        out_specs=blk(t),
        out_shape=jax.ShapeDtypeStruct((b, t, DH), BF),
        compiler_params=_cparams(("arbitrary", "arbitrary")),
        name="attention",
    )(*args)


def _route(logits):
    lane = lax.broadcasted_iota(jnp.int32, logits.shape, 1)
    lanef = lane.astype(F32)
    neg = -jnp.inf
    big = float(LANES)
    gl = jnp.where((lane >= NE) & (lane < NE + NG), logits, neg)
    gmax = jnp.max(gl, axis=-1, keepdims=True)
    gidx = jnp.min(jnp.where(gl == gmax, lanef, big), axis=-1, keepdims=True) - float(NE)
    gw = 1.0 / jnp.sum(jnp.exp(gl - gmax), axis=-1, keepdims=True)
    lo = gidx * float(EPG)
    el = jnp.where((lanef >= lo) & (lanef < lo + float(EPG)), logits, neg)
    v1 = jnp.max(el, axis=-1, keepdims=True)
    i1 = jnp.min(jnp.where(el == v1, lanef, big), axis=-1, keepdims=True)
    el2 = jnp.where(lanef == i1, neg, el)
    v2 = jnp.max(el2, axis=-1, keepdims=True)
    i2 = jnp.min(jnp.where(el2 == v2, lanef, big), axis=-1, keepdims=True)
    e2 = jnp.exp(v2 - v1)
    w1 = gw / (1.0 + e2)
    w2 = gw * e2 / (1.0 + e2)
    cmb = jnp.where(lanef == i1, w1, 0.0) + jnp.where(lanef == i2, w2, 0.0)
    return cmb, gidx


def _out_kernel(x_ref, yr_ref, ya_ref, mod_ref, n1_ref, wgate_ref, wor_ref, wom_ref, wout_ref,
                n2_ref, rw_ref, rb_ref, x1_ref, hx_ref, rinfo_ref, tcnt_ref):
    x = x_ref[...]
    hb = _modulated_norm(x, n1_ref[...], mod_ref[0, 1:2, :], mod_ref[0, 0:1, :]).astype(BF)
    gl = _dot(hb, wgate_ref[...])
    merged2 = ((jnp.tanh(gl[:, :D]) + 1.0) * _dot(yr_ref[...], wor_ref[...])
               + (jnp.tanh(gl[:, D:]) + 1.0) * _dot(ya_ref[...], wom_ref[...]))
    mix = _dot(merged2.astype(BF), wout_ref[...])
    x1 = x + mod_ref[0, 2:3, :] * mix
    x1_ref[...] = x1
    h2 = _modulated_norm(x1, n2_ref[...], mod_ref[0, 4:5, :], mod_ref[0, 3:4, :])
    h2_hi = h2.astype(BF)
    h2_lo = (h2 - h2_hi.astype(F32)).astype(BF)
    part = _dot(h2_hi, rw_ref[...])
    logits = part[:, :LANES] + part[:, LANES:] + _dot(h2_lo, rw_ref[:, :LANES]) + rb_ref[...]
    cmb, gidx = _route(logits)

    lanef = lax.broadcasted_iota(jnp.int32, cmb.shape, 1).astype(F32)
    ghot = jnp.where(lanef == gidx, 1.0, 0.0)
    r_i = lax.broadcasted_iota(jnp.int32, (TM, TM), 0)
    c_i = lax.broadcasted_iota(jnp.int32, (TM, TM), 1)
    tri = jnp.where(r_i > c_i, 1.0, 0.0).astype(BF)
    earlier_same = jnp.sum(_dot(tri, ghot.astype(BF)) * ghot, axis=-1, keepdims=True)
    counts = jnp.sum(ghot, axis=0, keepdims=True)
    padded = jnp.floor((counts + (SUBLANES - 1.0)) * (1.0 / SUBLANES)) * SUBLANES
    lower_groups = jnp.sum(jnp.where(lanef < gidx, padded, 0.0), axis=-1, keepdims=True)
    lpos = lower_groups + earlier_same
    s_i = lax.broadcasted_iota(jnp.int32, (TM, TS), 1)
    to_sorted = jnp.where(s_i.astype(F32) == lpos, 1.0, 0.0).astype(BF)
    c1 = cmb.astype(BF)
    c2 = (cmb - c1.astype(F32)).astype(BF)
    c3 = (cmb - c1.astype(F32) - c2.astype(F32)).astype(BF)
    payload = jnp.concatenate([h2.astype(BF), c1, c2, c3], axis=1)
    srt = lax.dot_general(to_sorted, payload, (((0,), (0,)), ((), ())), preferred_element_type=F32)
    hx_ref[:, :D] = srt[:, :D]
    hx_ref[:, D:] = srt[:, D:D + LANES] + srt[:, D + LANES:D + 2 * LANES] + srt[:, D + 2 * LANES:]

    rinfo_ref[...] = jnp.where(lanef == 0.0, gidx, jnp.where(lanef == 1.0, lpos, 0.0))
    tcnt_ref[0] = jnp.broadcast_to(counts, (SUBLANES, LANES))


def _merge_out(x, yr, ya, mod, mod_row, wts):
    n = x.shape[0]
    n_tiles = n // TM
    tile = lambda w: pl.BlockSpec((TM, w), lambda i: (i, 0))
    return pl.pallas_call(
        _out_kernel,
        grid=(n_tiles,),
        in_specs=[tile(D), tile(DR), tile(DH),
                  pl.BlockSpec((1, 6, D), lambda i: (mod_row(i), 0, 0)), _resident((1, D)),
                  _resident((D, 2 * D)), _resident((DR, D)), _resident((DH, D)), _resident((D, D)),
                  _resident((1, D)), _resident((D, 2 * LANES)), _resident((1, LANES))],
        out_specs=[tile(D), pl.BlockSpec((TS, XW), lambda i: (i, 0)), tile(LANES),
                   pl.BlockSpec((1, SUBLANES, LANES), lambda i: (i, 0, 0))],
        out_shape=[jax.ShapeDtypeStruct((n, D), F32), jax.ShapeDtypeStruct((n_tiles * TS, XW), F32),
                   jax.ShapeDtypeStruct((n, LANES), F32),
                   jax.ShapeDtypeStruct((n_tiles, SUBLANES, LANES), F32)],
        compiler_params=_cparams(("arbitrary",)),
        name="merge_out",
    )(x, yr, ya, mod, wts["n1"], wts["w_gate"], wts["w_o_rnn"], wts["w_o_mla"], wts["w_out"],
      wts["n2"], wts["router_w"], wts["router_b"])


def _run_copies(local_ref, far_ref, len_ref, tile, make_copy, action):
    for g in range(NG):
        idx = tile * NG + g
        n = len_ref[idx]
        local0 = local_ref[idx]
        far0 = far_ref[idx]
        for k in reversed(range(SUBLANES.bit_length() - 1, TS.bit_length())):
            size = 1 << k
            done = (n >> (k + 1)) << (k + 1)

            @pl.when(((n >> k) & 1) == 1)
            def _():
                action(make_copy(pl.multiple_of(local0 + done, SUBLANES),
                                 pl.multiple_of(far0 + done, SUBLANES), size))


def _dispatch_kernel(local_ref, far_ref, len_ref, hx_ref, xs_in_ref, xs_ref, sem):
    del xs_in_ref

    def copy(src, dst, size):
        return pltpu.make_async_copy(hx_ref.at[pl.ds(src, size)], xs_ref.at[pl.ds(dst, size)], sem)

    tile = pl.program_id(0)
    _run_copies(local_ref, far_ref, len_ref, tile, copy, lambda c: c.start())
    _run_copies(local_ref, far_ref, len_ref, tile, copy, lambda c: c.wait())


def _dispatch(tables, hx, xs):
    n_tiles = hx.shape[0] // TS
    return pl.pallas_call(
        _dispatch_kernel,
        grid_spec=pltpu.PrefetchScalarGridSpec(
            num_scalar_prefetch=3,
            grid=(n_tiles,),
            in_specs=[pl.BlockSpec((TS, XW), lambda i, *_: (i, 0)),
                      pl.BlockSpec(memory_space=pl.ANY)],
            out_specs=pl.BlockSpec(memory_space=pl.ANY),
            scratch_shapes=[pltpu.SemaphoreType.DMA(())],
        ),
        out_shape=jax.ShapeDtypeStruct(xs.shape, xs.dtype),
        input_output_aliases={4: 0},
        compiler_params=_cparams(("arbitrary",)),
        name="dispatch",
    )(*tables, hx, xs)


def _moe_kernel(tb_ref, tg_ref, nt_ref, xs_ref, w1_ref, w3_ref, w2_ref, o_ref, acc_ref):
    j = pl.program_id(0)
    pair = pl.program_id(1)

    @pl.when(j < nt_ref[0])
    def _():
        xt = xs_ref[:, :D].astype(BF)
        cmb = xs_ref[:, D:]
        lane = lax.broadcasted_iota(jnp.int32, cmb.shape, 1)
        first = tg_ref[j] * EPG + pair * EXPERTS_PER_STEP
        hidden = []
        for u in range(EXPERTS_PER_STEP):
            a = _dot(xt, w1_ref[u].astype(BF))
            he = (a * _sigmoid(a)) * _dot(xt, w3_ref[u].astype(BF))
            ce = jnp.sum(jnp.where(lane == first + u, cmb, 0.0), axis=-1, keepdims=True)
            hidden.append((he * ce).astype(BF))
        w2 = w2_ref[...].reshape(EXPERTS_PER_STEP * DE, D).astype(BF)
        y = _dot(jnp.concatenate(hidden, axis=1), w2)

        @pl.when(pair == 0)
        def _():
            acc_ref[...] = y

        @pl.when(pair == EPG // EXPERTS_PER_STEP - 1)
        def _():
            o_ref[...] = acc_ref[...] + y

    @pl.when((j >= nt_ref[0]) & (pair == EPG // EXPERTS_PER_STEP - 1))
    def _():
        o_ref[...] = jnp.zeros_like(o_ref)


def _experts(tile_block, tile_group, n_tiles, xs, wts):
    m = xs.shape[0]
    steps = EPG // EXPERTS_PER_STEP
    assert steps == 2, "the kernel keeps one partial sum: first step stores it, second adds and writes"

    def w_idx(j, e, tb, tg, nt):
        return (tg[j] * steps + jnp.where(j < nt[0], e, steps - 1), 0, 0)

    return pl.pallas_call(
        _moe_kernel,
        grid_spec=pltpu.PrefetchScalarGridSpec(
            num_scalar_prefetch=3,
            grid=(m // TMOE, steps),
            in_specs=[pl.BlockSpec((TMOE, XW), lambda j, e, tb, tg, nt: (tb[j], 0)),
                      pl.BlockSpec((EXPERTS_PER_STEP, D, DE), w_idx),
                      pl.BlockSpec((EXPERTS_PER_STEP, D, DE), w_idx),
                      pl.BlockSpec((EXPERTS_PER_STEP, DE, D), w_idx)],
            out_specs=pl.BlockSpec((TMOE, D), lambda j, e, tb, tg, nt: (j, 0)),
            scratch_shapes=[pltpu.VMEM((TMOE, D), F32)],
        ),
        out_shape=jax.ShapeDtypeStruct((m, D), F32),
        compiler_params=_cparams(("arbitrary", "arbitrary")),
        name="experts",
    )(tile_block, tile_group, n_tiles, xs, wts["exp_w1"], wts["exp_w3"], wts["exp_w2"])


def _combine_kernel(local_ref, far_ref, len_ref, x1_ref, rinfo_ref, mod_ref, ys_ref, o_ref,
                    buf_ref, sem, *, n_steps):
    i = pl.program_id(0)

    def runs(step, slot, action):
        def copy(dst, src, size):
            return pltpu.make_async_copy(ys_ref.at[pl.ds(src, size)],
                                         buf_ref.at[slot, pl.ds(dst, size)], sem.at[slot])

        _run_copies(local_ref, far_ref, len_ref, step, copy, action)

    @pl.when(i == 0)
    def _():
        buf_ref[...] = jnp.zeros_like(buf_ref)
        runs(0, 0, lambda c: c.start())

    @pl.when(i + 1 < n_steps)
    def _():
        runs(i + 1, (i + 1) % 2, lambda c: c.start())

    slot = i % 2
    runs(i, slot, lambda c: c.wait())
    lpos = rinfo_ref[:, 1:2]
    s_i = lax.broadcasted_iota(jnp.int32, (TM, TS), 1)
    from_sorted = jnp.where(s_i.astype(F32) == lpos, 1.0, 0.0).astype(BF)
    moe = _dot(from_sorted, buf_ref[slot].astype(BF))
    o_ref[...] = x1_ref[...] + mod_ref[0, 5:6, :] * moe


def _combine(tables, x1, rinfo, mod, mod_row, ys):
    n = x1.shape[0]
    n_steps = n // TM
    return pl.pallas_call(
        functools.partial(_combine_kernel, n_steps=n_steps),
        grid_spec=pltpu.PrefetchScalarGridSpec(
            num_scalar_prefetch=3,
            grid=(n_steps,),
            in_specs=[pl.BlockSpec((TM, D), lambda i, *_: (i, 0)),
                      pl.BlockSpec((TM, LANES), lambda i, *_: (i, 0)),
                      pl.BlockSpec((1, 6, D), lambda i, *_: (mod_row(i), 0, 0)),
                      pl.BlockSpec(memory_space=pl.ANY)],
            out_specs=pl.BlockSpec((TM, D), lambda i, *_: (i, 0)),
            scratch_shapes=[pltpu.VMEM((2, TS, D), F32), pltpu.SemaphoreType.DMA((2,))],
        ),
        out_shape=jax.ShapeDtypeStruct((n, D), F32),
        compiler_params=_cparams(("arbitrary",)),
        name="combine",
    )(*tables, x1, rinfo, mod, ys)


def _run_lengths(tile_counts):
    counts = tile_counts[:, 0, :NG].astype(jnp.int32)
    return ((counts + SUBLANES - 1) // SUBLANES) * SUBLANES


def _run_tables(lengths, first_far):
    local = jnp.cumsum(lengths, axis=1) - lengths
    far = first_far[None, :] + jnp.cumsum(lengths, axis=0) - lengths
    flat = lambda a: a.astype(jnp.int32).reshape(-1)
    return flat(local), flat(far), flat(lengths)


def _group_layout(counts, max_tiles):
    padded = ((counts + TMOE - 1) // TMOE) * TMOE
    ends = jnp.cumsum(padded)
    offsets = ends - padded
    n_tiles = (ends[-1] // TMOE).astype(jnp.int32)
    tile = jnp.minimum(jnp.arange(max_tiles, dtype=jnp.int32), jnp.maximum(n_tiles - 1, 0))
    tile_group = jnp.sum((tile[:, None] * TMOE >= ends[None, :]).astype(jnp.int32), axis=1)
    return offsets, tile, tile_group, n_tiles.reshape(1)


def _pad_heads(w, perm=None, rotary_only=False):
    lead = w.shape[:-1]
    per = w.shape[-1] // NH
    w = w.reshape(lead + (NH, per))
    if perm is not None:
        nope = jnp.zeros_like(w[..., :NOPE]) if rotary_only else w[..., :NOPE]
        w = jnp.concatenate([nope, w[..., NOPE:][..., perm]], axis=-1)
    w = jnp.pad(w, [(0, 0)] * len(lead) + [(0, 0), (0, HP - per)])
    return w.reshape(lead + (NH * HP,))


def _pad_gain(g, perm, rotary_only=False):
    nope = jnp.zeros((NOPE,), F32) if rotary_only else g[:NOPE]
    g = jnp.concatenate([nope, g[NOPE:][perm], jnp.zeros((HP - QK,), F32)])
    return g.reshape(1, HP)


def _block_diag(w):
    per = BDW // LRU_BLOCK
    w = w.reshape(DR // BDW, per, LRU_BLOCK, LRU_BLOCK)
    bd = jnp.einsum("jpab,pq->jpaqb", w, jnp.eye(per, dtype=w.dtype))
    return bd.reshape(DR // BDW, BDW, BDW)


def _prepare_shared(l, p):
    w_in = p["w_in"][l]
    o1, o2, o3, o4, o5 = DR, 2 * DR, 2 * DR + QL, 2 * DR + QL + KVL, 2 * DR + QL + KVL + ROPE
    bd = jnp.stack([_block_diag(p["lru_wa"][l, 0]), _block_diag(p["lru_wx"][l, 0]),
                    _block_diag(p["lru_wa"][l, 1]), _block_diag(p["lru_wx"][l, 1])])
    bd = (0.5 * bd).astype(BF)
    wom = p["w_o_mla"][l].reshape(NH, VD, D)
    wom = jnp.pad(wom, ((0, 0), (0, HP - VD), (0, 0))).reshape(DH, D)
    router_w = jnp.concatenate([p["router_we"][l], p["router_wg"][l],
                                jnp.zeros((D, LANES - NE - NG), F32)], axis=1)
    router_b = jnp.concatenate([p["router_be"][l], p["router_bg"][l],
                                jnp.zeros((LANES - NE - NG,), F32)]).reshape(1, LANES)
    router_hi = router_w.astype(BF)
    router_lo = (router_w - router_hi.astype(F32)).astype(BF)
    router_w = jnp.concatenate([router_hi, router_lo], axis=1)
    return {
        "n1": p["norm1_g"][l].reshape(1, D), "n2": p["norm2_g"][l].reshape(1, D),
        "w_x": w_in[:, :o1].astype(BF), "w_g": w_in[:, o1:o2].astype(BF),
        "w_q": w_in[:, o2:o3].astype(BF), "w_kv": w_in[:, o3:o4], "w_kr": w_in[:, o4:o5],
        "w_gate": (0.5 * w_in[:, o5:]).astype(BF),
        "qan": p["q_a_norm"][l].reshape(1, QL), "kvan": p["kv_a_norm"][l].reshape(1, KVL),
        "w_uk": _pad_heads(p["w_uk"][l]).astype(BF),
        "w_uv": _pad_heads(p["w_uv"][l]).astype(BF),
        "conv_w": p["conv_w"][l], "conv_b": p["conv_b"][l].reshape(1, DR), "bd": bd,
        "lru_ba": 0.5 * p["lru_ba"][l], "lru_bx": 0.5 * p["lru_bx"][l], "lru_lam": p["lru_lam"][l],
        "w_o_rnn": p["w_o_rnn"][l].astype(BF), "w_o_mla": wom.astype(BF), "w_out": (0.5 * p["w_out"][l]).astype(BF),
        "router_w": router_w, "router_b": router_b,
        "exp_w1": p["exp_w1"][l], "exp_w3": p["exp_w3"][l], "exp_w2": p["exp_w2"][l],
    }


def _with_rope_order(l, p, shared, perm, rotary):
    w = dict(shared)
    zeros = lambda n: jnp.zeros((D, n), F32)
    rope_block = lambda order: [zeros(NOPE), shared["w_kr"][:, order], zeros(HP - QK)]
    kvr = [shared["w_kv"]] + rope_block(perm)
    w["w_uq"] = _pad_heads(p["w_uq"][l], perm).astype(BF)
    w["gq"] = _pad_gain(p["q_norm"][l], perm)
    w["gk"] = _pad_gain(p["k_norm"][l], perm)
    if rotary:
        pair = np.concatenate([perm[ROPE // 2:], perm[:ROPE // 2]])
        kvr += rope_block(pair)
        w["w_uq_pair"] = _pad_heads(p["w_uq"][l], pair, rotary_only=True).astype(BF)
        w["gq_pair"] = _pad_gain(p["q_norm"][l], pair, rotary_only=True)
        w["gk_pair"] = _pad_gain(p["k_norm"][l], pair, rotary_only=True)
    w["w_kvr"] = jnp.concatenate(kvr, axis=1).astype(BF)
    return w


def _rope_tables(n_tokens, perm):
    rows = n_tokens // GRID_W
    row = np.repeat(np.arange(rows), GRID_W).astype(np.float32)
    col = np.tile(np.arange(GRID_W), rows).astype(np.float32)
    axis_dim = ROPE // 2
    inv = (np.float32(ROPE_BASE) ** (-np.arange(0, axis_dim, 2, dtype=np.float32) / axis_dim)).astype(np.float32)
    ang = np.concatenate([row[:, None] * inv, col[:, None] * inv], axis=-1).astype(np.float32)
    cos, sin = np.cos(ang), np.sin(ang)
    ones = lambda n: np.ones((n_tokens, n), np.float32)
    zeros = lambda n: np.zeros((n_tokens, n), np.float32)
    cos_t = np.concatenate([ones(NOPE), cos, cos, ones(HP - QK)], axis=1)
    sin_t = np.concatenate([zeros(NOPE), -sin, sin, zeros(HP - QK)], axis=1)
    return jnp.asarray(cos_t, F32), jnp.asarray(sin_t, F32)


def kernel(x_prompt, x_sample, cache_mla_ckv, cache_mla_krope, state_rglru, c, c_ctx, norm1_g, norm2_g, w_mod, b_mod, w_in, conv_w, conv_b, lru_wa, lru_ba, lru_wx, lru_bx, lru_lam, q_a_norm, kv_a_norm, w_uq, w_uk, w_uv, q_norm, k_norm, w_o_rnn, w_o_mla, w_out, router_wg, router_bg, router_we, router_be, exp_w1, exp_w3, exp_w2):
    p = dict(norm1_g=norm1_g, norm2_g=norm2_g, w_in=w_in, conv_w=conv_w, conv_b=conv_b,
             lru_wa=lru_wa, lru_ba=lru_ba, lru_wx=lru_wx, lru_bx=lru_bx, lru_lam=lru_lam,
             q_a_norm=q_a_norm, kv_a_norm=kv_a_norm, w_uq=w_uq, w_uk=w_uk, w_uv=w_uv,
             q_norm=q_norm, k_norm=k_norm, w_o_rnn=w_o_rnn, w_o_mla=w_o_mla, w_out=w_out,
             router_wg=router_wg, router_bg=router_bg, router_we=router_we, router_be=router_be,
             exp_w1=exp_w1, exp_w3=exp_w3, exp_w2=exp_w2)
    depth = w_in.shape[0]
    nb, seq, _ = x_prompt.shape
    db, dseq, _ = x_sample.shape
    ident = np.arange(ROPE)
    halves = np.concatenate([np.arange(0, ROPE, 2), np.arange(1, ROPE, 2)])
    rope_tabs = _rope_tables(dseq, halves)
    cond8 = jnp.concatenate([c_ctx[None, :], c, jnp.zeros((SUBLANES - 1 - db, D), F32)], axis=0)
    ctx_row = lambda tile_rows: (lambda i: 0)
    lat_row = lambda tile_rows: (lambda i: (i * tile_rows) // dseq + 1)
    n_ctx, n_lat = nb * seq, db * dseq
    run_padding = ((n_ctx + n_lat) // TM) * NG * (SUBLANES - 1)
    max_tiles = -(-(n_ctx + n_lat + run_padding) // TMOE) + NG
    per_seq = lambda arrs, b, t: [a.reshape(b, t, a.shape[-1]) for a in arrs]
    flat = lambda a: a.reshape(-1, a.shape[-1])

    y_prompt, y_sample = x_prompt.reshape(n_ctx, D), x_sample.reshape(n_lat, D)
    ckv_list, krope_list, rnn_list = [], [], []
    for l in range(depth):
        shared = _prepare_shared(l, p)
        w_ctx = _with_rope_order(l, p, shared, ident, False)
        w_lat = _with_rope_order(l, p, shared, halves, True)
        mod = _modulation(cond8, w_mod[l], b_mod[l]).reshape(SUBLANES, 6, D)

        xr, gg, q, k, v, ckv, kro = _projections(y_prompt, mod, ctx_row(TM), w_ctx, None, True)
        xr, gg, q, k, v = per_seq([xr, gg, q, k, v], nb, seq)
        yr, h_fin = _rglru(xr, gg, shared, None, True)
        ya = _attention(q, k, v, None, None, NH)
        x1_c, hx_c, ri_c, tc_c = _merge_out(y_prompt, flat(yr), flat(ya), mod, ctx_row(TM), shared)
        ckv_list.append(ckv.reshape(nb, seq, KVL))
        krope_list.append(kro.reshape(nb, seq, ROPE))
        rnn_list.append(h_fin)

        krp_cache = jnp.pad(cache_mla_krope[:, l][..., halves], ((0, 0), (0, 0), (NOPE, HP - QK)))
        kc, vc = _cache_keys_values(cache_mla_ckv[:, l], krp_cache, w_lat)
        xr, gg, q, k, v = _projections(y_sample, mod, lat_row(TM), w_lat, rope_tabs, False)
        xr, gg, q, k, v = per_seq([xr, gg, q, k, v], db, dseq)
        yr, _ = _rglru(xr, gg, shared, state_rglru[:, l], False)
        ya = _attention(q, k, v, kc, vc, 2)
        x1_l, hx_l, ri_l, tc_l = _merge_out(y_sample, flat(yr), flat(ya), mod, lat_row(TM), shared)

        len_c, len_l = _run_lengths(tc_c), _run_lengths(tc_l)
        rows_c = jnp.sum(len_c, axis=0)
        offsets, tile_block, tile_group, n_tiles = _group_layout(rows_c + jnp.sum(len_l, axis=0), max_tiles)
        runs_c = _run_tables(len_c, offsets)
        runs_l = _run_tables(len_l, offsets + rows_c)
        xs = jnp.zeros((max_tiles * TMOE, XW), F32)
        xs = _dispatch(runs_c, hx_c, xs)
        xs = _dispatch(runs_l, hx_l, xs)
        ys = _experts(tile_block, tile_group, n_tiles, xs, shared)
        y_prompt = _combine(runs_c, x1_c, ri_c, mod, ctx_row(TM), ys)
        y_sample = _combine(runs_l, x1_l, ri_l, mod, lat_row(TM), ys)

    y_prompt, y_sample = y_prompt.reshape(nb, seq, D), y_sample.reshape(db, dseq, D)

    return (y_prompt, y_sample, jnp.stack(ckv_list, axis=1), jnp.stack(krope_list, axis=1),
            jnp.stack(rnn_list, axis=1))
```

```python
import functools
import math

import numpy as np
import jax
import jax.numpy as jnp
from jax import lax
from jax.experimental import pallas as pl
from jax.experimental.pallas import tpu as pltpu

D = 1024
DR = 1024
QL = 384
KVL = 256
NH = 8
NOPE = 64
ROPE = 32
QK = NOPE + ROPE
VD = 64
HP = 128
DH = NH * HP
GRID_W = 64
ROPE_BASE = 10000.0
EPS = 1e-6
TINY = 1e-30
LRU_C = 8.0
LRU_BLOCK = 64
BDW = 256
CH = 512
NLT = CH // 128
NG = 4
EPG = 4
NE = NG * EPG
DE = 512
LANES = 128
SUBLANES = 8
TM = 512
TS = 640
TMOE = 512
EXPERTS_PER_STEP = 2
XW = D + 3 * 128
RUN_ALIGN = 16
VMEM_LIMIT = 52 * 1024 * 1024
BF = jnp.bfloat16
F32 = jnp.float32


def _cparams(sem):
    return pltpu.CompilerParams(dimension_semantics=sem, vmem_limit_bytes=VMEM_LIMIT)


def _dot(a, b):
    return jnp.dot(a, b, preferred_element_type=F32)


def _dot_nt(a, b):
    return lax.dot_general(a, b, (((1,), (1,)), ((), ())), preferred_element_type=F32)


def _rms(x, g, width):
    ms = jnp.sum(x * x, axis=-1, keepdims=True) * (1.0 / width)
    return x * lax.rsqrt(ms + EPS) * g


def _modulated_norm(x, g, scale, shift):
    return _rms(x, g, D) * (1.0 + scale) + shift


def _mod_kernel(c_ref, w_ref, b_ref, o_ref):
    c = c_ref[...]
    s = c * jax.nn.sigmoid(c)
    o_ref[...] = _dot(s, w_ref[...]) + b_ref[...]


def _modulation(cond8, w_mod, b_mod):
    n = w_mod.shape[1]
    return pl.pallas_call(
        _mod_kernel,
        grid=(n // D,),
        in_specs=[
            pl.BlockSpec((SUBLANES, D), lambda j: (0, 0)),
            pl.BlockSpec((D, D), lambda j: (0, j)),
            pl.BlockSpec((1, D), lambda j: (0, j)),
        ],
        out_specs=pl.BlockSpec((SUBLANES, D), lambda j: (0, j)),
        out_shape=jax.ShapeDtypeStruct((SUBLANES, n), F32),
        compiler_params=_cparams(("arbitrary",)),
        name="modulation",
    )(cond8, w_mod, b_mod.reshape(1, n))


def _head_norm(xh, gain, cos=None, partner_scaled=None):
    ms = jnp.sum(xh * xh, axis=-1, keepdims=True) * (1.0 / QK)
    rs = lax.rsqrt(ms + EPS)
    y = xh * rs * gain
    if cos is None:
        return y
    return y * cos + partner_scaled * rs


def _keys_values(ckv, krp, wuk_ref, wuv_ref, gk, cos, partner_scaled, k_ref, v_ref):
    cb = ckv.astype(BF)
    kn = _dot(cb, wuk_ref[...])
    v_ref[...] = _dot(cb, wuv_ref[...]).astype(BF)
    for h in range(NH):
        kh = kn[:, h * HP:(h + 1) * HP] + krp
        k_ref[:, h * HP:(h + 1) * HP] = _head_norm(kh, gk, cos, partner_scaled).astype(BF)


def _proj_kernel(*refs, rope, emit_cache):
    it = iter(refs)
    x_ref, mod_ref, n1_ref = next(it), next(it), next(it)
    wx_ref, wg_ref, wq_ref, wkvr_ref = next(it), next(it), next(it), next(it)
    qan_ref, kvan_ref, wuq_ref, gq_ref = next(it), next(it), next(it), next(it)
    wuk_ref, gk_ref, wuv_ref = next(it), next(it), next(it)
    if rope:
        wuqs_ref, gqs_ref, gks_ref, cos_ref, sins_ref = (next(it) for _ in range(5))
    xr_ref, gg_ref, q_ref, k_ref, v_ref = (next(it) for _ in range(5))
    if emit_cache:
        ckv_ref, kro_ref = next(it), next(it)

    hb = _modulated_norm(x_ref[...], n1_ref[...], mod_ref[0, 1:2, :], mod_ref[0, 0:1, :]).astype(BF)
    xr_ref[...] = _dot(hb, wx_ref[...]).astype(BF)
    gg_ref[...] = jax.nn.gelu(_dot(hb, wg_ref[...])).astype(BF)

    qnb = _rms(_dot(hb, wq_ref[...]), qan_ref[...], QL).astype(BF)
    q = _dot(qnb, wuq_ref[...])
    gq = gq_ref[...]
    cos = q_partner = q_pair_scale = None
    if rope:
        cos, sins = cos_ref[...], sins_ref[...]
        q_partner = _dot(qnb, wuqs_ref[...])
        q_pair_scale = gqs_ref[...] * sins
    for hd in range(NH):
        cols = slice(hd * HP, (hd + 1) * HP)
        partner = q_partner[:, cols] * q_pair_scale if rope else None
        q_ref[:, cols] = _head_norm(q[:, cols], gq, cos, partner).astype(BF)

    kvr = _dot(hb, wkvr_ref[...])
    ckv = _rms(kvr[:, :KVL], kvan_ref[...], KVL)
    krp = kvr[:, KVL:KVL + HP]
    k_partner = kvr[:, KVL + HP:KVL + 2 * HP] * (gks_ref[...] * sins) if rope else None
    if emit_cache:
        ckv_ref[...] = ckv
        kro_ref[...] = krp[:, NOPE:NOPE + ROPE]
    _keys_values(ckv, krp, wuk_ref, wuv_ref, gk_ref[...], cos, k_partner, k_ref, v_ref)


def _resident(shape):
    return pl.BlockSpec(shape, lambda i: (0,) * len(shape), pipeline_mode=pl.Buffered(1))


def _projections(x, mod, mod_row, wts, rope_tabs, emit_cache):
    n = x.shape[0]
    rope = rope_tabs is not None
    tile = lambda w: pl.BlockSpec((TM, w), lambda i: (i, 0))
    in_specs = [
        tile(D),
        pl.BlockSpec((1, 6, D), lambda i: (mod_row(i), 0, 0)),
        _resident((1, D)),
        _resident((D, DR)), _resident((D, DR)), _resident((D, QL)), _resident(wts["w_kvr"].shape),
        _resident((1, QL)), _resident((1, KVL)), _resident((QL, DH)), _resident((1, HP)),
        _resident((KVL, DH)), _resident((1, HP)), _resident((KVL, DH)),
    ]
    args = [x, mod, wts["n1"], wts["w_x"], wts["w_g"], wts["w_q"], wts["w_kvr"],
            wts["qan"], wts["kvan"], wts["w_uq"], wts["gq"], wts["w_uk"], wts["gk"], wts["w_uv"]]
    if rope:
        tiles_per_seq = rope_tabs[0].shape[0] // TM
        in_specs += [_resident((QL, DH)), _resident((1, HP)), _resident((1, HP))]
        in_specs += [pl.BlockSpec((TM, HP), lambda i: (i % tiles_per_seq, 0))] * 2
        args += [wts["w_uq_pair"], wts["gq_pair"], wts["gk_pair"]] + list(rope_tabs)
    out_specs = [tile(DR), tile(DR), tile(DH), tile(DH), tile(DH)]
    out_shape = [jax.ShapeDtypeStruct((n, DR), BF),
                 jax.ShapeDtypeStruct((n, DR), BF), jax.ShapeDtypeStruct((n, DH), BF),
                 jax.ShapeDtypeStruct((n, DH), BF), jax.ShapeDtypeStruct((n, DH), BF)]
    if emit_cache:
        out_specs += [tile(KVL), tile(ROPE)]
        out_shape += [jax.ShapeDtypeStruct((n, KVL), F32), jax.ShapeDtypeStruct((n, ROPE), F32)]
    return pl.pallas_call(
        functools.partial(_proj_kernel, rope=rope, emit_cache=emit_cache),
        grid=(n // TM,),
        in_specs=in_specs,
        out_specs=out_specs,
        out_shape=out_shape,
        compiler_params=_cparams(("arbitrary",)),
        name="projections",
    )(*args)


def _cache_kv_kernel(ckv_ref, krp_ref, wuk_ref, gk_ref, wuv_ref, k_ref, v_ref):
    _keys_values(ckv_ref[0], krp_ref[0], wuk_ref, wuv_ref, gk_ref[...], None, None,
                 k_ref.at[0], v_ref.at[0])


def _cache_keys_values(ckv, krp, wts):
    b, s, _ = ckv.shape
    full = lambda shape: pl.BlockSpec(shape, lambda i: (0,) * len(shape))
    return pl.pallas_call(
        _cache_kv_kernel,
        grid=(b,),
        in_specs=[pl.BlockSpec((1, s, KVL), lambda i: (i, 0, 0)),
                  pl.BlockSpec((1, s, HP), lambda i: (i, 0, 0)),
                  full((KVL, DH)), full((1, HP)), full((KVL, DH))],
        out_specs=[pl.BlockSpec((1, s, DH), lambda i: (i, 0, 0))] * 2,
        out_shape=[jax.ShapeDtypeStruct((b, s, DH), BF)] * 2,
        compiler_params=_cparams(("arbitrary",)),
        name="cache_keys_values",
    )(ckv, krp, wts["w_uk"], wts["gk"], wts["w_uv"])


def _sigmoid(x):
    return 0.5 * jnp.tanh(0.5 * x) + 0.5


def _tile_scan(a, b, forward):
    row = lax.broadcasted_iota(jnp.int32, a.shape, 0)
    for k in (1, 2, 4):
        if forward:
            shift, valid = k, row >= k
        else:
            shift, valid = SUBLANES - k, row < SUBLANES - k
        a_prev = jnp.where(valid, pltpu.roll(a, shift, 0), 1.0)
        b_prev = jnp.where(valid, pltpu.roll(b, shift, 0), 0.0)
        b = a * b_prev + b
        a = a * a_prev
    return a, b


def _rglru_kernel(*refs, t, has_h0, emit_state):
    it = iter(refs)
    xr_ref, gg_ref, cw_ref, cb_ref, bd_ref, ba_ref, bx_ref, lam_ref = (next(it) for _ in range(8))
    h0_ref = next(it) if has_h0 else None
    y_ref = next(it)
    hf_ref = next(it) if emit_state else None
    a_scr, b_scr, h_scr, xpad_scr = (next(it) for _ in range(4))

    n_tiles = t // SUBLANES
    pad = SUBLANES
    zero_rows = jnp.zeros((pad, CH), F32)
    xpad_scr[0:pad, :] = zero_rows
    xpad_scr[pad + t:pad + t + pad, :] = zero_rows
    xpad_scr[pad:pad + t, :] = xr_ref[0].astype(F32)
    xpad = xpad_scr[...]
    rows_all = t + 2 * pad
    xc = cb_ref[...] + xpad[pad:pad + t, :] * cw_ref[1:2, :]
    for tap, shift in ((0, 1), (2, rows_all - 1), (3, rows_all - 2)):
        xc = xc + pltpu.roll(xpad, shift, 0)[pad:pad + t, :] * cw_ref[tap:tap + 1, :]
    for s in range(CH // BDW):
        cols = slice(s * BDW, (s + 1) * BDW)
        xs = xc[:, cols]
        xsb = xs.astype(BF)
        xh = 0.5 * xs
        for d in range(2):
            tr = jnp.tanh(_dot(xsb, bd_ref[2 * d, s]) + ba_ref[d:d + 1, cols])
            ti = jnp.tanh(_dot(xsb, bd_ref[2 * d + 1, s]) + bx_ref[d:d + 1, cols])
            nl = -lam_ref[d:d + 1, cols]
            softplus = jnp.maximum(nl, 0.0) + jnp.log(1.0 + jnp.exp(-jnp.abs(nl)))
            ch = (-0.5 * LRU_C) * softplus
            a = jnp.exp(tr * ch + ch)
            z = 1.0 - a * a
            root = z * lax.rsqrt(jnp.maximum(z, TINY))
            a_scr[d, :, cols] = a
            b_scr[d, :, cols] = root * (ti * xh + xh)

    def step(i, carry):
        out = []
        for d in range(2):
            tile = i if d == 0 else n_tiles - 1 - i
            rows = pl.ds(pl.multiple_of(tile * SUBLANES, SUBLANES), SUBLANES)
            for c in range(NLT):
                lanes = slice(c * LANES, (c + 1) * LANES)
                decay, local = _tile_scan(a_scr[d, rows, lanes], b_scr[d, rows, lanes], d == 0)
                h = local + decay * carry[d * NLT + c]
                h_scr[d, rows, lanes] = h
                last = h[SUBLANES - 1:SUBLANES, :] if d == 0 else h[0:1, :]
                out.append(jnp.broadcast_to(last, (SUBLANES, LANES)))
        return tuple(out)

    init = []
    for d in range(2):
        for c in range(NLT):
            if has_h0:
                h0 = h0_ref[0, d:d + 1, c * LANES:(c + 1) * LANES]
                init.append(jnp.broadcast_to(h0, (SUBLANES, LANES)))
            else:
                init.append(jnp.zeros((SUBLANES, LANES), F32))
    final = lax.fori_loop(0, n_tiles, step, tuple(init), unroll=2)

    if emit_state:
        for d in range(2):
            for c in range(NLT):
                hf_ref[0, d:d + 1, c * LANES:(c + 1) * LANES] = final[d * NLT + c][0:1, :]
    y_ref[0] = ((h_scr[0] + h_scr[1]) * gg_ref[0].astype(F32)).astype(BF)


def _rglru(xr, gg, wts, h0, emit_state):
    b, t, _ = xr.shape
    nc = DR // CH
    has_h0 = h0 is not None
    chunk = lambda r: pl.BlockSpec((r, CH), lambda i, j: (0, j))
    seq = pl.BlockSpec((1, t, CH), lambda i, j: (i, 0, j))
    state = pl.BlockSpec((1, 2, CH), lambda i, j: (i, 0, j))
    in_specs = [seq, seq, chunk(4), chunk(1),
                pl.BlockSpec((4, CH // BDW, BDW, BDW), lambda i, j: (0, j, 0, 0)),
                chunk(2), chunk(2), chunk(2)]
    args = [xr, gg, wts["conv_w"], wts["conv_b"], wts["bd"], wts["lru_ba"], wts["lru_bx"], wts["lru_lam"]]
    if has_h0:
        in_specs.append(state)
        args.append(h0)
    out_specs = [seq]
    out_shape = [jax.ShapeDtypeStruct((b, t, DR), BF)]
    if emit_state:
        out_specs.append(state)
        out_shape.append(jax.ShapeDtypeStruct((b, 2, DR), F32))
    res = pl.pallas_call(
        functools.partial(_rglru_kernel, t=t, has_h0=has_h0, emit_state=emit_state),
        grid=(b, nc),
        in_specs=in_specs,
        out_specs=out_specs,
        out_shape=out_shape,
        scratch_shapes=[pltpu.VMEM((2, t, CH), F32)] * 3 + [pltpu.VMEM((t + 2 * SUBLANES, CH), F32)],
        compiler_params=_cparams(("arbitrary", "arbitrary")),
        name="rglru",
    )(*args)
    return res if emit_state else (res[0], None)


def _attn_kernel(*refs, t, n_heads, has_ctx, q_block):
    it = iter(refs)
    q_ref, k_ref, v_ref = next(it), next(it), next(it)
    kc_ref = vc_ref = None
    if has_ctx:
        kc_ref, vc_ref = next(it), next(it)
    o_ref = next(it)
    log2_scale = (QK ** -0.5) * math.log2(math.e)
    for hd in range(n_heads):
        cols = slice(hd * HP, (hd + 1) * HP)
        k = k_ref[0, :, cols]
        v = v_ref[0, :, cols]
        if has_ctx:
            kc = kc_ref[0, :, cols]
            vc = vc_ref[0, :, cols]
        for qb in range(t // q_block):
            rows = slice(qb * q_block, (qb + 1) * q_block)
            q = q_ref[0, rows, cols]
            s = _dot_nt(q, k) * log2_scale
            m = jnp.max(s, axis=-1, keepdims=True)
            if has_ctx:
                sc = _dot_nt(q, kc) * log2_scale
                m = jnp.maximum(m, jnp.max(sc, axis=-1, keepdims=True))
            p = jnp.exp2(s - m)
            den = jnp.sum(p, axis=-1, keepdims=True)
            o = _dot(p.astype(BF), v)
            if has_ctx:
                pc = jnp.exp2(sc - m)
                den = den + jnp.sum(pc, axis=-1, keepdims=True)
                o = o + _dot(pc.astype(BF), vc)
            o_ref[0, rows, cols] = (o / den).astype(BF)


def _attention(q, k, v, kc, vc, heads_per_step):
    b, t, _ = q.shape
    has_ctx = kc is not None
    w = heads_per_step * HP
    blk = lambda n: pl.BlockSpec((1, n, w), lambda i, j: (i, 0, j))
    in_specs = [blk(t), blk(t), blk(t)]
    args = [q, k, v]
    if has_ctx:
        in_specs += [blk(kc.shape[1])] * 2
        args += [kc, vc]
    return pl.pallas_call(
        functools.partial(_attn_kernel, t=t, n_heads=heads_per_step, has_ctx=has_ctx, q_block=min(t, 256)),
        grid=(b, NH // heads_per_step),
        in_specs=in_specs,
        out_specs=blk(t),
        out_shape=jax.ShapeDtypeStruct((b, t, DH), BF),
        compiler_params=_cparams(("arbitrary", "arbitrary")),
        name="attention",
    )(*args)


def _route(logits):
    lane = lax.broadcasted_iota(jnp.int32, logits.shape, 1)
    lanef = lane.astype(F32)
    neg = -jnp.inf
    big = float(LANES)
    gl = jnp.where((lane >= NE) & (lane < NE + NG), logits, neg)
    gmax = jnp.max(gl, axis=-1, keepdims=True)
    gidx = jnp.min(jnp.where(gl == gmax, lanef, big), axis=-1, keepdims=True) - float(NE)
    gw = 1.0 / jnp.sum(jnp.exp(gl - gmax), axis=-1, keepdims=True)
    lo = gidx * float(EPG)
    el = jnp.where((lanef >= lo) & (lanef < lo + float(EPG)), logits, neg)
    v1 = jnp.max(el, axis=-1, keepdims=True)
    i1 = jnp.min(jnp.where(el == v1, lanef, big), axis=-1, keepdims=True)
    el2 = jnp.where(lanef == i1, neg, el)
    v2 = jnp.max(el2, axis=-1, keepdims=True)
    i2 = jnp.min(jnp.where(el2 == v2, lanef, big), axis=-1, keepdims=True)
    e2 = jnp.exp(v2 - v1)
    w1 = gw / (1.0 + e2)
    w2 = gw * e2 / (1.0 + e2)
    cmb = jnp.where(lanef == i1, w1, 0.0) + jnp.where(lanef == i2, w2, 0.0)
    return cmb, gidx


def _out_kernel(x_ref, yr_ref, ya_ref, mod_ref, n1_ref, wgate_ref, wor_ref, wom_ref, wout_ref,
                n2_ref, rw_ref, rb_ref, x1_ref, hx_ref, rinfo_ref, tcnt_ref):
    x = x_ref[...]
    hb = _modulated_norm(x, n1_ref[...], mod_ref[0, 1:2, :], mod_ref[0, 0:1, :]).astype(BF)
    gl = _dot(hb, wgate_ref[...])
    merged2 = ((jnp.tanh(gl[:, :D]) + 1.0) * _dot(yr_ref[...], wor_ref[...])
               + (jnp.tanh(gl[:, D:]) + 1.0) * _dot(ya_ref[...], wom_ref[...]))
    mix = _dot(merged2.astype(BF), wout_ref[...])
    x1 = x + mod_ref[0, 2:3, :] * mix
    x1_ref[...] = x1
    h2 = _modulated_norm(x1, n2_ref[...], mod_ref[0, 4:5, :], mod_ref[0, 3:4, :])
    h2_hi = h2.astype(BF)
    h2_lo = (h2 - h2_hi.astype(F32)).astype(BF)
    part = _dot(h2_hi, rw_ref[...])
    logits = part[:, :LANES] + part[:, LANES:] + _dot(h2_lo, rw_ref[:, :LANES]) + rb_ref[...]
    cmb, gidx = _route(logits)

    lanef = lax.broadcasted_iota(jnp.int32, cmb.shape, 1).astype(F32)
    ghot = jnp.where(lanef == gidx, 1.0, 0.0)
    r_i = lax.broadcasted_iota(jnp.int32, (TM, TM), 0)
    c_i = lax.broadcasted_iota(jnp.int32, (TM, TM), 1)
    tri = jnp.where(r_i > c_i, 1.0, 0.0).astype(BF)
    earlier_same = jnp.sum(_dot(tri, ghot.astype(BF)) * ghot, axis=-1, keepdims=True)
    counts = jnp.sum(ghot, axis=0, keepdims=True)
    padded = jnp.floor((counts + (RUN_ALIGN - 1.0)) * (1.0 / RUN_ALIGN)) * RUN_ALIGN
    lower_groups = jnp.sum(jnp.where(lanef < gidx, padded, 0.0), axis=-1, keepdims=True)
    lpos = lower_groups + earlier_same
    s_i = lax.broadcasted_iota(jnp.int32, (TM, TS), 1)
    to_sorted = jnp.where(s_i.astype(F32) == lpos, 1.0, 0.0).astype(BF)
    c1 = cmb.astype(BF)
    c2 = (cmb - c1.astype(F32)).astype(BF)
    c3 = (cmb - c1.astype(F32) - c2.astype(F32)).astype(BF)
    payload = jnp.concatenate([h2.astype(BF), c1, c2, c3], axis=1)
    srt = lax.dot_general(to_sorted, payload, (((0,), (0,)), ((), ())), preferred_element_type=F32)
    hx_ref[...] = srt.astype(BF)

    rinfo_ref[...] = jnp.where(lanef == 0.0, gidx, jnp.where(lanef == 1.0, lpos, 0.0))
    tcnt_ref[0] = jnp.broadcast_to(counts, (SUBLANES, LANES))


def _merge_out(x, yr, ya, mod, mod_row, wts):
    n = x.shape[0]
    n_tiles = n // TM
    tile = lambda w: pl.BlockSpec((TM, w), lambda i: (i, 0))
    return pl.pallas_call(
        _out_kernel,
        grid=(n_tiles,),
        in_specs=[tile(D), tile(DR), tile(DH),
                  pl.BlockSpec((1, 6, D), lambda i: (mod_row(i), 0, 0)), _resident((1, D)),
                  _resident((D, 2 * D)), _resident((DR, D)), _resident((DH, D)), _resident((D, D)),
                  _resident((1, D)), _resident((D, 2 * LANES)), _resident((1, LANES))],
        out_specs=[tile(D), pl.BlockSpec((TS, XW), lambda i: (i, 0)), tile(LANES),
                   pl.BlockSpec((1, SUBLANES, LANES), lambda i: (i, 0, 0))],
        out_shape=[jax.ShapeDtypeStruct((n, D), F32), jax.ShapeDtypeStruct((n_tiles * TS, XW), BF),
                   jax.ShapeDtypeStruct((n, LANES), F32),
                   jax.ShapeDtypeStruct((n_tiles, SUBLANES, LANES), F32)],
        compiler_params=_cparams(("arbitrary",)),
        name="merge_out",
    )(x, yr, ya, mod, wts["n1"], wts["w_gate"], wts["w_o_rnn"], wts["w_o_mla"], wts["w_out"],
      wts["n2"], wts["router_w"], wts["router_b"])


def _run_copies(local_ref, far_ref, len_ref, tile, make_copy, action):
    for g in range(NG):
        idx = tile * NG + g
        n = len_ref[idx]
        local0 = local_ref[idx]
        far0 = far_ref[idx]
        for k in reversed(range(RUN_ALIGN.bit_length() - 1, TS.bit_length())):
            size = 1 << k
            done = (n >> (k + 1)) << (k + 1)

            @pl.when(((n >> k) & 1) == 1)
            def _():
                action(make_copy(pl.multiple_of(local0 + done, RUN_ALIGN),
                                 pl.multiple_of(far0 + done, RUN_ALIGN), size))


def _dispatch_kernel(local_ref, far_ref, len_ref, hx_ref, xs_in_ref, xs_ref, sem):
    del xs_in_ref

    def copy(src, dst, size):
        return pltpu.make_async_copy(hx_ref.at[pl.ds(src, size)], xs_ref.at[pl.ds(dst, size)], sem)

    tile = pl.program_id(0)
    _run_copies(local_ref, far_ref, len_ref, tile, copy, lambda c: c.start())
    _run_copies(local_ref, far_ref, len_ref, tile, copy, lambda c: c.wait())


def _dispatch(tables, hx, xs):
    n_tiles = hx.shape[0] // TS
    return pl.pallas_call(
        _dispatch_kernel,
        grid_spec=pltpu.PrefetchScalarGridSpec(
            num_scalar_prefetch=3,
            grid=(n_tiles,),
            in_specs=[pl.BlockSpec((TS, XW), lambda i, *_: (i, 0)),
                      pl.BlockSpec(memory_space=pl.ANY)],
            out_specs=pl.BlockSpec(memory_space=pl.ANY),
            scratch_shapes=[pltpu.SemaphoreType.DMA(())],
        ),
        out_shape=jax.ShapeDtypeStruct(xs.shape, xs.dtype),
        input_output_aliases={4: 0},
        compiler_params=_cparams(("arbitrary",)),
        name="dispatch",
    )(*tables, hx, xs)


def _moe_kernel(tb_ref, tg_ref, nt_ref, xs_ref, w1_ref, w3_ref, w2_ref, o_ref, acc_ref):
    j = pl.program_id(0)
    pair = pl.program_id(1)

    @pl.when(j < nt_ref[0])
    def _():
        xt = xs_ref[:, :D]
        cmb = (xs_ref[:, D:D + LANES].astype(F32) + xs_ref[:, D + LANES:D + 2 * LANES].astype(F32)
               + xs_ref[:, D + 2 * LANES:].astype(F32))
        lane = lax.broadcasted_iota(jnp.int32, cmb.shape, 1)
        first = tg_ref[j] * EPG + pair * EXPERTS_PER_STEP
        hidden = []
        for u in range(EXPERTS_PER_STEP):
            a = _dot(xt, w1_ref[u].astype(BF))
            he = (a * _sigmoid(a)) * _dot(xt, w3_ref[u].astype(BF))
            ce = jnp.sum(jnp.where(lane == first + u, cmb, 0.0), axis=-1, keepdims=True)
            hidden.append((he * ce).astype(BF))
        w2 = w2_ref[...].reshape(EXPERTS_PER_STEP * DE, D).astype(BF)
        y = _dot(jnp.concatenate(hidden, axis=1), w2)

        @pl.when(pair == 0)
        def _():
            acc_ref[...] = y

        @pl.when(pair == EPG // EXPERTS_PER_STEP - 1)
        def _():
            o_ref[...] = (acc_ref[...] + y).astype(BF)

    @pl.when((j >= nt_ref[0]) & (pair == EPG // EXPERTS_PER_STEP - 1))
    def _():
        o_ref[...] = jnp.zeros_like(o_ref)


def _experts(tile_block, tile_group, n_tiles, xs, wts):
    m = xs.shape[0]
    steps = EPG // EXPERTS_PER_STEP
    assert steps == 2, "the kernel keeps one partial sum: first step stores it, second adds and writes"

    def w_idx(j, e, tb, tg, nt):
        return (tg[j] * steps + jnp.where(j < nt[0], e, steps - 1), 0, 0)

    return pl.pallas_call(
        _moe_kernel,
        grid_spec=pltpu.PrefetchScalarGridSpec(
            num_scalar_prefetch=3,
            grid=(m // TMOE, steps),
            in_specs=[pl.BlockSpec((TMOE, XW), lambda j, e, tb, tg, nt: (tb[j], 0)),
                      pl.BlockSpec((EXPERTS_PER_STEP, D, DE), w_idx),
                      pl.BlockSpec((EXPERTS_PER_STEP, D, DE), w_idx),
                      pl.BlockSpec((EXPERTS_PER_STEP, DE, D), w_idx)],
            out_specs=pl.BlockSpec((TMOE, D), lambda j, e, tb, tg, nt: (j, 0)),
            scratch_shapes=[pltpu.VMEM((TMOE, D), F32)],
        ),
        out_shape=jax.ShapeDtypeStruct((m, D), BF),
        compiler_params=_cparams(("arbitrary", "arbitrary")),
        name="experts",
    )(tile_block, tile_group, n_tiles, xs, wts["exp_w1"], wts["exp_w3"], wts["exp_w2"])


def _combine_kernel(local_ref, far_ref, len_ref, x1_ref, rinfo_ref, mod_ref, ys_ref, o_ref,
                    buf_ref, sem, *, n_steps):
    i = pl.program_id(0)

    def runs(step, slot, action):
        def copy(dst, src, size):
            return pltpu.make_async_copy(ys_ref.at[pl.ds(src, size)],
                                         buf_ref.at[slot, pl.ds(dst, size)], sem.at[slot])

        _run_copies(local_ref, far_ref, len_ref, step, copy, action)

    @pl.when(i == 0)
    def _():
        buf_ref[...] = jnp.zeros_like(buf_ref)
        runs(0, 0, lambda c: c.start())

    @pl.when(i + 1 < n_steps)
    def _():
        runs(i + 1, (i + 1) % 2, lambda c: c.start())

    slot = i % 2
    runs(i, slot, lambda c: c.wait())
    lpos = rinfo_ref[:, 1:2]
    s_i = lax.broadcasted_iota(jnp.int32, (TM, TS), 1)
    from_sorted = jnp.where(s_i.astype(F32) == lpos, 1.0, 0.0).astype(BF)
    moe = _dot(from_sorted, buf_ref[slot])
    o_ref[...] = x1_ref[...] + mod_ref[0, 5:6, :] * moe


def _combine(tables, x1, rinfo, mod, mod_row, ys):
    n = x1.shape[0]
    n_steps = n // TM
    return pl.pallas_call(
        functools.partial(_combine_kernel, n_steps=n_steps),
        grid_spec=pltpu.PrefetchScalarGridSpec(
            num_scalar_prefetch=3,
            grid=(n_steps,),
            in_specs=[pl.BlockSpec((TM, D), lambda i, *_: (i, 0)),
                      pl.BlockSpec((TM, LANES), lambda i, *_: (i, 0)),
                      pl.BlockSpec((1, 6, D), lambda i, *_: (mod_row(i), 0, 0)),
                      pl.BlockSpec(memory_space=pl.ANY)],
            out_specs=pl.BlockSpec((TM, D), lambda i, *_: (i, 0)),
            scratch_shapes=[pltpu.VMEM((2, TS, D), BF), pltpu.SemaphoreType.DMA((2,))],
        ),
        out_shape=jax.ShapeDtypeStruct((n, D), F32),
        compiler_params=_cparams(("arbitrary",)),
        name="combine",
    )(*tables, x1, rinfo, mod, ys)


def _run_lengths(tile_counts):
    counts = tile_counts[:, 0, :NG].astype(jnp.int32)
    return ((counts + RUN_ALIGN - 1) // RUN_ALIGN) * RUN_ALIGN


def _run_tables(lengths, first_far):
    local = jnp.cumsum(lengths, axis=1) - lengths
    far = first_far[None, :] + jnp.cumsum(lengths, axis=0) - lengths
    flat = lambda a: a.astype(jnp.int32).reshape(-1)
    return flat(local), flat(far), flat(lengths)


def _group_layout(counts, max_tiles):
    padded = ((counts + TMOE - 1) // TMOE) * TMOE
    ends = jnp.cumsum(padded)
    offsets = ends - padded
    n_tiles = (ends[-1] // TMOE).astype(jnp.int32)
    tile = jnp.minimum(jnp.arange(max_tiles, dtype=jnp.int32), jnp.maximum(n_tiles - 1, 0))
    tile_group = jnp.sum((tile[:, None] * TMOE >= ends[None, :]).astype(jnp.int32), axis=1)
    return offsets, tile, tile_group, n_tiles.reshape(1)


def _pad_heads(w, perm=None, rotary_only=False):
    lead = w.shape[:-1]
    per = w.shape[-1] // NH
    w = w.reshape(lead + (NH, per))
    if perm is not None:
        nope = jnp.zeros_like(w[..., :NOPE]) if rotary_only else w[..., :NOPE]
        w = jnp.concatenate([nope, w[..., NOPE:][..., perm]], axis=-1)
    w = jnp.pad(w, [(0, 0)] * len(lead) + [(0, 0), (0, HP - per)])
    return w.reshape(lead + (NH * HP,))


def _pad_gain(g, perm, rotary_only=False):
    nope = jnp.zeros((NOPE,), F32) if rotary_only else g[:NOPE]
    g = jnp.concatenate([nope, g[NOPE:][perm], jnp.zeros((HP - QK,), F32)])
    return g.reshape(1, HP)


def _block_diag(w):
    per = BDW // LRU_BLOCK
    w = w.reshape(DR // BDW, per, LRU_BLOCK, LRU_BLOCK)
    bd = jnp.einsum("jpab,pq->jpaqb", w, jnp.eye(per, dtype=w.dtype))
    return bd.reshape(DR // BDW, BDW, BDW)


def _prepare_shared(l, p):
    w_in = p["w_in"][l]
    o1, o2, o3, o4, o5 = DR, 2 * DR, 2 * DR + QL, 2 * DR + QL + KVL, 2 * DR + QL + KVL + ROPE
    bd = jnp.stack([_block_diag(p["lru_wa"][l, 0]), _block_diag(p["lru_wx"][l, 0]),
                    _block_diag(p["lru_wa"][l, 1]), _block_diag(p["lru_wx"][l, 1])])
    bd = (0.5 * bd).astype(BF)
    wom = p["w_o_mla"][l].reshape(NH, VD, D)
    wom = jnp.pad(wom, ((0, 0), (0, HP - VD), (0, 0))).reshape(DH, D)
    router_w = jnp.concatenate([p["router_we"][l], p["router_wg"][l],
                                jnp.zeros((D, LANES - NE - NG), F32)], axis=1)
    router_b = jnp.concatenate([p["router_be"][l], p["router_bg"][l],
                                jnp.zeros((LANES - NE - NG,), F32)]).reshape(1, LANES)
    router_hi = router_w.astype(BF)
    router_lo = (router_w - router_hi.astype(F32)).astype(BF)
    router_w = jnp.concatenate([router_hi, router_lo], axis=1)
    return {
        "n1": p["norm1_g"][l].reshape(1, D), "n2": p["norm2_g"][l].reshape(1, D),
        "w_x": w_in[:, :o1].astype(BF), "w_g": w_in[:, o1:o2].astype(BF),
        "w_q": w_in[:, o2:o3].astype(BF), "w_kv": w_in[:, o3:o4], "w_kr": w_in[:, o4:o5],
        "w_gate": (0.5 * w_in[:, o5:]).astype(BF),
        "qan": p["q_a_norm"][l].reshape(1, QL), "kvan": p["kv_a_norm"][l].reshape(1, KVL),
        "w_uk": _pad_heads(p["w_uk"][l]).astype(BF),
        "w_uv": _pad_heads(p["w_uv"][l]).astype(BF),
        "conv_w": p["conv_w"][l], "conv_b": p["conv_b"][l].reshape(1, DR), "bd": bd,
        "lru_ba": 0.5 * p["lru_ba"][l], "lru_bx": 0.5 * p["lru_bx"][l], "lru_lam": p["lru_lam"][l],
        "w_o_rnn": p["w_o_rnn"][l].astype(BF), "w_o_mla": wom.astype(BF), "w_out": (0.5 * p["w_out"][l]).astype(BF),
        "router_w": router_w, "router_b": router_b,
        "exp_w1": p["exp_w1"][l], "exp_w3": p["exp_w3"][l], "exp_w2": p["exp_w2"][l],
    }


def _with_rope_order(l, p, shared, perm, rotary):
    w = dict(shared)
    zeros = lambda n: jnp.zeros((D, n), F32)
    rope_block = lambda order: [zeros(NOPE), shared["w_kr"][:, order], zeros(HP - QK)]
    kvr = [shared["w_kv"]] + rope_block(perm)
    w["w_uq"] = _pad_heads(p["w_uq"][l], perm).astype(BF)
    w["gq"] = _pad_gain(p["q_norm"][l], perm)
    w["gk"] = _pad_gain(p["k_norm"][l], perm)
    if rotary:
        pair = np.concatenate([perm[ROPE // 2:], perm[:ROPE // 2]])
        kvr += rope_block(pair)
        w["w_uq_pair"] = _pad_heads(p["w_uq"][l], pair, rotary_only=True).astype(BF)
        w["gq_pair"] = _pad_gain(p["q_norm"][l], pair, rotary_only=True)
        w["gk_pair"] = _pad_gain(p["k_norm"][l], pair, rotary_only=True)
    w["w_kvr"] = jnp.concatenate(kvr, axis=1).astype(BF)
    return w


def _rope_tables(n_tokens, perm):
    rows = n_tokens // GRID_W
    row = np.repeat(np.arange(rows), GRID_W).astype(np.float32)
    col = np.tile(np.arange(GRID_W), rows).astype(np.float32)
    axis_dim = ROPE // 2
    inv = (np.float32(ROPE_BASE) ** (-np.arange(0, axis_dim, 2, dtype=np.float32) / axis_dim)).astype(np.float32)
    ang = np.concatenate([row[:, None] * inv, col[:, None] * inv], axis=-1).astype(np.float32)
    cos, sin = np.cos(ang), np.sin(ang)
    ones = lambda n: np.ones((n_tokens, n), np.float32)
    zeros = lambda n: np.zeros((n_tokens, n), np.float32)
    cos_t = np.concatenate([ones(NOPE), cos, cos, ones(HP - QK)], axis=1)
    sin_t = np.concatenate([zeros(NOPE), -sin, sin, zeros(HP - QK)], axis=1)
    return jnp.asarray(cos_t, F32), jnp.asarray(sin_t, F32)


def kernel(x_prompt, x_sample, cache_mla_ckv, cache_mla_krope, state_rglru, c, c_ctx, norm1_g, norm2_g, w_mod, b_mod, w_in, conv_w, conv_b, lru_wa, lru_ba, lru_wx, lru_bx, lru_lam, q_a_norm, kv_a_norm, w_uq, w_uk, w_uv, q_norm, k_norm, w_o_rnn, w_o_mla, w_out, router_wg, router_bg, router_we, router_be, exp_w1, exp_w3, exp_w2):
    p = dict(norm1_g=norm1_g, norm2_g=norm2_g, w_in=w_in, conv_w=conv_w, conv_b=conv_b,
             lru_wa=lru_wa, lru_ba=lru_ba, lru_wx=lru_wx, lru_bx=lru_bx, lru_lam=lru_lam,
             q_a_norm=q_a_norm, kv_a_norm=kv_a_norm, w_uq=w_uq, w_uk=w_uk, w_uv=w_uv,
             q_norm=q_norm, k_norm=k_norm, w_o_rnn=w_o_rnn, w_o_mla=w_o_mla, w_out=w_out,
             router_wg=router_wg, router_bg=router_bg, router_we=router_we, router_be=router_be,
             exp_w1=exp_w1, exp_w3=exp_w3, exp_w2=exp_w2)
    depth = w_in.shape[0]
    nb, seq, _ = x_prompt.shape
    db, dseq, _ = x_sample.shape
    ident = np.arange(ROPE)
    halves = np.concatenate([np.arange(0, ROPE, 2), np.arange(1, ROPE, 2)])
    rope_tabs = _rope_tables(dseq, halves)
    cond8 = jnp.concatenate([c_ctx[None, :], c, jnp.zeros((SUBLANES - 1 - db, D), F32)], axis=0)
    ctx_row = lambda tile_rows: (lambda i: 0)
    lat_row = lambda tile_rows: (lambda i: (i * tile_rows) // dseq + 1)
    n_ctx, n_lat = nb * seq, db * dseq
    run_padding = ((n_ctx + n_lat) // TM) * NG * (RUN_ALIGN - 1)
    max_tiles = -(-(n_ctx + n_lat + run_padding) // TMOE) + NG
    per_seq = lambda arrs, b, t: [a.reshape(b, t, a.shape[-1]) for a in arrs]
    flat = lambda a: a.reshape(-1, a.shape[-1])

    y_prompt, y_sample = x_prompt.reshape(n_ctx, D), x_sample.reshape(n_lat, D)
    ckv_list, krope_list, rnn_list = [], [], []
    for l in range(depth):
        shared = _prepare_shared(l, p)
        w_ctx = _with_rope_order(l, p, shared, ident, False)
        w_lat = _with_rope_order(l, p, shared, halves, True)
        mod = _modulation(cond8, w_mod[l], b_mod[l]).reshape(SUBLANES, 6, D)

        xr, gg, q, k, v, ckv, kro = _projections(y_prompt, mod, ctx_row(TM), w_ctx, None, True)
        xr, gg, q, k, v = per_seq([xr, gg, q, k, v], nb, seq)
        yr, h_fin = _rglru(xr, gg, shared, None, True)
        ya = _attention(q, k, v, None, None, NH)
        x1_c, hx_c, ri_c, tc_c = _merge_out(y_prompt, flat(yr), flat(ya), mod, ctx_row(TM), shared)
        ckv_list.append(ckv.reshape(nb, seq, KVL))
        krope_list.append(kro.reshape(nb, seq, ROPE))
        rnn_list.append(h_fin)

        krp_cache = jnp.pad(cache_mla_krope[:, l][..., halves], ((0, 0), (0, 0), (NOPE, HP - QK)))
        kc, vc = _cache_keys_values(cache_mla_ckv[:, l], krp_cache, w_lat)
        xr, gg, q, k, v = _projections(y_sample, mod, lat_row(TM), w_lat, rope_tabs, False)
        xr, gg, q, k, v = per_seq([xr, gg, q, k, v], db, dseq)
        yr, _ = _rglru(xr, gg, shared, state_rglru[:, l], False)
        ya = _attention(q, k, v, kc, vc, 2)
        x1_l, hx_l, ri_l, tc_l = _merge_out(y_sample, flat(yr), flat(ya), mod, lat_row(TM), shared)

        len_c, len_l = _run_lengths(tc_c), _run_lengths(tc_l)
        rows_c = jnp.sum(len_c, axis=0)
        offsets, tile_block, tile_group, n_tiles = _group_layout(rows_c + jnp.sum(len_l, axis=0), max_tiles)
        runs_c = _run_tables(len_c, offsets)
        runs_l = _run_tables(len_l, offsets + rows_c)
        xs = jnp.zeros((max_tiles * TMOE, XW), BF)
        xs = _dispatch(runs_c, hx_c, xs)
        xs = _dispatch(runs_l, hx_l, xs)
        ys = _experts(tile_block, tile_group, n_tiles, xs, shared)
        y_prompt = _combine(runs_c, x1_c, ri_c, mod, ctx_row(TM), ys)
        y_sample = _combine(runs_l, x1_l, ri_l, mod, lat_row(TM), ys)

    y_prompt, y_sample = y_prompt.reshape(nb, seq, D), y_sample.reshape(db, dseq, D)

    return (y_prompt, y_sample, jnp.stack(ckv_list, axis=1), jnp.stack(krope_list, axis=1),
            jnp.stack(rnn_list, axis=1))
```

```python
import functools
import math

import numpy as np
import jax
import jax.numpy as jnp
from jax import lax
from jax.experimental import pallas as pl
from jax.experimental.pallas import tpu as pltpu

D = 1024
DR = 1024
QL = 384
KVL = 256
NH = 8
NOPE = 64
ROPE = 32
QK = NOPE + ROPE
VD = 64
HP = 128
DH = NH * HP
GRID_W = 64
ROPE_BASE = 10000.0
EPS = 1e-6
TINY = 1e-30
LRU_C = 8.0
LRU_BLOCK = 64
BDW = 256
CH = 512
NLT = CH // 128
NG = 4
EPG = 4
NE = NG * EPG
DE = 512
LANES = 128
SUBLANES = 8
TM = 512
TS = 640
TMOE = 512
EXPERTS_PER_STEP = 2
XW = D + 3 * 128
RUN_ALIGN = 16
VMEM_LIMIT = 52 * 1024 * 1024
BF = jnp.bfloat16
F32 = jnp.float32


def _cparams(sem):
    return pltpu.CompilerParams(dimension_semantics=sem, vmem_limit_bytes=VMEM_LIMIT)


def _dot(a, b):
    return jnp.dot(a, b, preferred_element_type=F32)


def _dot_nt(a, b):
    return lax.dot_general(a, b, (((1,), (1,)), ((), ())), preferred_element_type=F32)


def _rms(x, g, width):
    ms = jnp.sum(x * x, axis=-1, keepdims=True) * (1.0 / width)
    return x * lax.rsqrt(ms + EPS) * g


def _modulated_norm(x, g, scale, shift):
    return _rms(x, g * (1.0 + scale), D) + shift


def _mod_kernel(c_ref, w_ref, b_ref, o_ref):
    c = c_ref[...]
    s = c * jax.nn.sigmoid(c)
    o_ref[...] = _dot(s, w_ref[...]) + b_ref[...]


def _modulation(cond8, w_mod, b_mod):
    n = w_mod.shape[1]
    return pl.pallas_call(
        _mod_kernel,
        grid=(n // D,),
        in_specs=[
            pl.BlockSpec((SUBLANES, D), lambda j: (0, 0)),
            pl.BlockSpec((D, D), lambda j: (0, j)),
            pl.BlockSpec((1, D), lambda j: (0, j)),
        ],
        out_specs=pl.BlockSpec((SUBLANES, D), lambda j: (0, j)),
        out_shape=jax.ShapeDtypeStruct((SUBLANES, n), F32),
        compiler_params=_cparams(("arbitrary",)),
        name="modulation",
    )(cond8, w_mod, b_mod.reshape(1, n))


def _head_norm(xh, gain, cos=None, partner_scaled=None):
    ms = jnp.sum(xh * xh, axis=-1, keepdims=True) * (1.0 / QK)
    rs = lax.rsqrt(ms + EPS)
    y = xh * rs * gain
    if cos is None:
        return y
    return y * cos + partner_scaled * rs


def _keys_values(ckv, krp, wuk_ref, wuv_ref, gk, cos, partner_scaled, k_ref, v_ref):
    cb = ckv.astype(BF)
    kn = _dot(cb, wuk_ref[...])
    v_ref[...] = _dot(cb, wuv_ref[...]).astype(BF)
    for h in range(NH):
        kh = kn[:, h * HP:(h + 1) * HP] + krp
        k_ref[:, h * HP:(h + 1) * HP] = _head_norm(kh, gk, cos, partner_scaled).astype(BF)


def _proj_kernel(*refs, rope, emit_cache):
    it = iter(refs)
    x_ref, mod_ref, n1_ref = next(it), next(it), next(it)
    wx_ref, wg_ref, wq_ref, wkvr_ref = next(it), next(it), next(it), next(it)
    qan_ref, kvan_ref, wuq_ref, gq_ref = next(it), next(it), next(it), next(it)
    wuk_ref, gk_ref, wuv_ref = next(it), next(it), next(it)
    if rope:
        wuqs_ref, gqs_ref, gks_ref, cos_ref, sins_ref = (next(it) for _ in range(5))
    h_ref, xr_ref, gg_ref, q_ref, k_ref, v_ref = (next(it) for _ in range(6))
    if emit_cache:
        ckv_ref, kro_ref = next(it), next(it)

    hb = _modulated_norm(x_ref[...], n1_ref[...], mod_ref[0, 1:2, :], mod_ref[0, 0:1, :]).astype(BF)
    h_ref[...] = hb
    xr_ref[...] = _dot(hb, wx_ref[...]).astype(BF)
    gg_ref[...] = jax.nn.gelu(_dot(hb, wg_ref[...])).astype(BF)

    qnb = _rms(_dot(hb, wq_ref[...]), qan_ref[...], QL).astype(BF)
    q = _dot(qnb, wuq_ref[...])
    gq = gq_ref[...]
    cos = q_partner = q_pair_scale = None
    if rope:
        cos, sins = cos_ref[...], sins_ref[...]
        q_partner = _dot(qnb, wuqs_ref[...])
        q_pair_scale = gqs_ref[...] * sins
    for hd in range(NH):
        cols = slice(hd * HP, (hd + 1) * HP)
        partner = q_partner[:, cols] * q_pair_scale if rope else None
        q_ref[:, cols] = _head_norm(q[:, cols], gq, cos, partner).astype(BF)

    kvr = _dot(hb, wkvr_ref[...])
    ckv = _rms(kvr[:, :KVL], kvan_ref[...], KVL)
    krp = kvr[:, KVL:KVL + HP]
    k_partner = kvr[:, KVL + HP:KVL + 2 * HP] * (gks_ref[...] * sins) if rope else None
    if emit_cache:
        ckv_ref[...] = ckv
        kro_ref[...] = krp[:, NOPE:NOPE + ROPE]
    _keys_values(ckv, krp, wuk_ref, wuv_ref, gk_ref[...], cos, k_partner, k_ref, v_ref)


def _resident(shape):
    return pl.BlockSpec(shape, lambda i: (0,) * len(shape), pipeline_mode=pl.Buffered(1))


def _projections(x, mod, mod_row, wts, rope_tabs, emit_cache):
    n = x.shape[0]
    rope = rope_tabs is not None
    tile = lambda w: pl.BlockSpec((TM, w), lambda i: (i, 0))
    in_specs = [
        tile(D),
        pl.BlockSpec((1, 6, D), lambda i: (mod_row(i), 0, 0)),
        _resident((1, D)),
        _resident((D, DR)), _resident((D, DR)), _resident((D, QL)), _resident(wts["w_kvr"].shape),
        _resident((1, QL)), _resident((1, KVL)), _resident((QL, DH)), _resident((1, HP)),
        _resident((KVL, DH)), _resident((1, HP)), _resident((KVL, DH)),
    ]
    args = [x, mod, wts["n1"], wts["w_x"], wts["w_g"], wts["w_q"], wts["w_kvr"],
            wts["qan"], wts["kvan"], wts["w_uq"], wts["gq"], wts["w_uk"], wts["gk"], wts["w_uv"]]
    if rope:
        tiles_per_seq = rope_tabs[0].shape[0] // TM
        in_specs += [_resident((QL, DH)), _resident((1, HP)), _resident((1, HP))]
        in_specs += [pl.BlockSpec((TM, HP), lambda i: (i % tiles_per_seq, 0))] * 2
        args += [wts["w_uq_pair"], wts["gq_pair"], wts["gk_pair"]] + list(rope_tabs)
    out_specs = [tile(D), tile(DR), tile(DR), tile(DH), tile(DH), tile(DH)]
    out_shape = [jax.ShapeDtypeStruct((n, D), BF), jax.ShapeDtypeStruct((n, DR), BF),
                 jax.ShapeDtypeStruct((n, DR), BF), jax.ShapeDtypeStruct((n, DH), BF),
                 jax.ShapeDtypeStruct((n, DH), BF), jax.ShapeDtypeStruct((n, DH), BF)]
    if emit_cache:
        out_specs += [tile(KVL), tile(ROPE)]
        out_shape += [jax.ShapeDtypeStruct((n, KVL), F32), jax.ShapeDtypeStruct((n, ROPE), F32)]
    return pl.pallas_call(
        functools.partial(_proj_kernel, rope=rope, emit_cache=emit_cache),
        grid=(n // TM,),
        in_specs=in_specs,
        out_specs=out_specs,
        out_shape=out_shape,
        compiler_params=_cparams(("arbitrary",)),
        name="projections",
    )(*args)


def _cache_kv_kernel(ckv_ref, krp_ref, wuk_ref, gk_ref, wuv_ref, k_ref, v_ref):
    _keys_values(ckv_ref[0], krp_ref[0], wuk_ref, wuv_ref, gk_ref[...], None, None,
                 k_ref.at[0], v_ref.at[0])


def _cache_keys_values(ckv, krp, wts):
    b, s, _ = ckv.shape
    full = lambda shape: pl.BlockSpec(shape, lambda i: (0,) * len(shape))
    return pl.pallas_call(
        _cache_kv_kernel,
        grid=(b,),
        in_specs=[pl.BlockSpec((1, s, KVL), lambda i: (i, 0, 0)),
                  pl.BlockSpec((1, s, HP), lambda i: (i, 0, 0)),
                  full((KVL, DH)), full((1, HP)), full((KVL, DH))],
        out_specs=[pl.BlockSpec((1, s, DH), lambda i: (i, 0, 0))] * 2,
        out_shape=[jax.ShapeDtypeStruct((b, s, DH), BF)] * 2,
        compiler_params=_cparams(("arbitrary",)),
        name="cache_keys_values",
    )(ckv, krp, wts["w_uk"], wts["gk"], wts["w_uv"])


def _sigmoid(x):
    return 0.5 * jnp.tanh(0.5 * x) + 0.5


def _tile_scan(a, b, forward):
    row = lax.broadcasted_iota(jnp.int32, a.shape, 0)
    for k in (1, 2, 4):
        if forward:
            shift, valid = k, row >= k
        else:
            shift, valid = SUBLANES - k, row < SUBLANES - k
        a_prev = jnp.where(valid, pltpu.roll(a, shift, 0), 1.0)
        b_prev = jnp.where(valid, pltpu.roll(b, shift, 0), 0.0)
        b = a * b_prev + b
        a = a * a_prev
    return a, b


def _rglru_kernel(*refs, t, has_h0, emit_state):
    it = iter(refs)
    xr_ref, gg_ref, cw_ref, cb_ref, bd_ref, ba_ref, bx_ref, lam_ref = (next(it) for _ in range(8))
    h0_ref = next(it) if has_h0 else None
    y_ref = next(it)
    hf_ref = next(it) if emit_state else None
    a_scr, b_scr, h_scr, xpad_scr = (next(it) for _ in range(4))

    n_tiles = t // SUBLANES
    pad = SUBLANES
    zero_rows = jnp.zeros((pad, CH), F32)
    xpad_scr[0:pad, :] = zero_rows
    xpad_scr[pad + t:pad + t + pad, :] = zero_rows
    xpad_scr[pad:pad + t, :] = xr_ref[0].astype(F32)
    xpad = xpad_scr[...]
    rows_all = t + 2 * pad
    xc = cb_ref[...] + xpad[pad:pad + t, :] * cw_ref[1:2, :]
    for tap, shift in ((0, 1), (2, rows_all - 1), (3, rows_all - 2)):
        xc = xc + pltpu.roll(xpad, shift, 0)[pad:pad + t, :] * cw_ref[tap:tap + 1, :]
    for s in range(CH // BDW):
        cols = slice(s * BDW, (s + 1) * BDW)
        xs = xc[:, cols]
        xsb = xs.astype(BF)
        xh = 0.5 * xs
        for d in range(2):
            tr = jnp.tanh(_dot(xsb, bd_ref[2 * d, s]) + ba_ref[d:d + 1, cols])
            ti = jnp.tanh(_dot(xsb, bd_ref[2 * d + 1, s]) + bx_ref[d:d + 1, cols])
            nl = -lam_ref[d:d + 1, cols]
            softplus = jnp.maximum(nl, 0.0) + jnp.log(1.0 + jnp.exp(-jnp.abs(nl)))
            ch = (-0.5 * LRU_C) * softplus
            a = jnp.exp(tr * ch + ch)
            z = 1.0 - a * a
            root = z * lax.rsqrt(jnp.maximum(z, TINY))
            a_scr[d, :, cols] = a
            b_scr[d, :, cols] = root * (ti * xh + xh)

    def step(i, carry):
        out = []
        for d in range(2):
            tile = i if d == 0 else n_tiles - 1 - i
            rows = pl.ds(pl.multiple_of(tile * SUBLANES, SUBLANES), SUBLANES)
            for c in range(NLT):
                lanes = slice(c * LANES, (c + 1) * LANES)
                decay, local = _tile_scan(a_scr[d, rows, lanes], b_scr[d, rows, lanes], d == 0)
                h = local + decay * carry[d * NLT + c]
                h_scr[d, rows, lanes] = h
                last = h[SUBLANES - 1:SUBLANES, :] if d == 0 else h[0:1, :]
                out.append(jnp.broadcast_to(last, (SUBLANES, LANES)))
        return tuple(out)

    init = []
    for d in range(2):
        for c in range(NLT):
            if has_h0:
                h0 = h0_ref[0, d:d + 1, c * LANES:(c + 1) * LANES]
                init.append(jnp.broadcast_to(h0, (SUBLANES, LANES)))
            else:
                init.append(jnp.zeros((SUBLANES, LANES), F32))
    final = lax.fori_loop(0, n_tiles, step, tuple(init), unroll=2)

    if emit_state:
        for d in range(2):
            for c in range(NLT):
                hf_ref[0, d:d + 1, c * LANES:(c + 1) * LANES] = final[d * NLT + c][0:1, :]
    y_ref[0] = ((h_scr[0] + h_scr[1]) * gg_ref[0].astype(F32)).astype(BF)


def _rglru(xr, gg, wts, h0, emit_state):
    b, t, _ = xr.shape
    nc = DR // CH
    has_h0 = h0 is not None
    chunk = lambda r: pl.BlockSpec((r, CH), lambda i, j: (0, j))
    seq = pl.BlockSpec((1, t, CH), lambda i, j: (i, 0, j))
    state = pl.BlockSpec((1, 2, CH), lambda i, j: (i, 0, j))
    in_specs = [seq, seq, chunk(4), chunk(1),
                pl.BlockSpec((4, CH // BDW, BDW, BDW), lambda i, j: (0, j, 0, 0)),
                chunk(2), chunk(2), chunk(2)]
    args = [xr, gg, wts["conv_w"], wts["conv_b"], wts["bd"], wts["lru_ba"], wts["lru_bx"], wts["lru_lam"]]
    if has_h0:
        in_specs.append(state)
        args.append(h0)
    out_specs = [seq]
    out_shape = [jax.ShapeDtypeStruct((b, t, DR), BF)]
    if emit_state:
        out_specs.append(state)
        out_shape.append(jax.ShapeDtypeStruct((b, 2, DR), F32))
    res = pl.pallas_call(
        functools.partial(_rglru_kernel, t=t, has_h0=has_h0, emit_state=emit_state),
        grid=(b, nc),
        in_specs=in_specs,
        out_specs=out_specs,
        out_shape=out_shape,
        scratch_shapes=[pltpu.VMEM((2, t, CH), F32)] * 3 + [pltpu.VMEM((t + 2 * SUBLANES, CH), F32)],
        compiler_params=_cparams(("arbitrary", "arbitrary")),
        name="rglru",
    )(*args)
    return res if emit_state else (res[0], None)


def _attn_kernel(*refs, t, n_heads, has_ctx, q_block):
    it = iter(refs)
    q_ref, k_ref, v_ref = next(it), next(it), next(it)
    kc_ref = vc_ref = None
    if has_ctx:
        kc_ref, vc_ref = next(it), next(it)
    o_ref = next(it)
    log2_scale = (QK ** -0.5) * math.log2(math.e)
    for hd in range(n_heads):
        cols = slice(hd * HP, (hd + 1) * HP)
        k = k_ref[0, :, cols]
        v = v_ref[0, :, cols]
        if has_ctx:
            kc = kc_ref[0, :, cols]
            vc = vc_ref[0, :, cols]
        for qb in range(t // q_block):
            rows = slice(qb * q_block, (qb + 1) * q_block)
            q = q_ref[0, rows, cols]
            s = _dot_nt(q, k) * log2_scale
            m = jnp.max(s, axis=-1, keepdims=True)
            if has_ctx:
                sc = _dot_nt(q, kc) * log2_scale
                m = jnp.maximum(m, jnp.max(sc, axis=-1, keepdims=True))
            p = jnp.exp2(s - m)
            den = jnp.sum(p, axis=-1, keepdims=True)
            o = _dot(p.astype(BF), v)
            if has_ctx:
                pc = jnp.exp2(sc - m)
                den = den + jnp.sum(pc, axis=-1, keepdims=True)
                o = o + _dot(pc.astype(BF), vc)
            o_ref[0, rows, cols] = (o / den).astype(BF)


def _attention(q, k, v, kc, vc, heads_per_step):
    b, t, _ = q.shape
    has_ctx = kc is not None
    w = heads_per_step * HP
    blk = lambda n: pl.BlockSpec((1, n, w), lambda i, j: (i, 0, j))
    in_specs = [blk(t), blk(t), blk(t)]
    args = [q, k, v]
    if has_ctx:
        in_specs += [blk(kc.shape[1])] * 2
        args += [kc, vc]
    return pl.pallas_call(
        functools.partial(_attn_kernel, t=t, n_heads=heads_per_step, has_ctx=has_ctx, q_block=min(t, 256)),
        grid=(b, NH // heads_per_step),
        in_specs=in_specs,
        out_specs=blk(t),
        out_shape=jax.ShapeDtypeStruct((b, t, DH), BF),
        compiler_params=_cparams(("arbitrary", "arbitrary")),
        name="attention",
    )(*args)


def _route(logits):
    lane = lax.broadcasted_iota(jnp.int32, logits.shape, 1)
    lanef = lane.astype(F32)
    neg = -jnp.inf
    big = float(LANES)
    gl = jnp.where((lane >= NE) & (lane < NE + NG), logits, neg)
    gmax = jnp.max(gl, axis=-1, keepdims=True)
    gidx = jnp.min(jnp.where(gl == gmax, lanef, big), axis=-1, keepdims=True) - float(NE)
    gw = 1.0 / jnp.sum(jnp.exp(gl - gmax), axis=-1, keepdims=True)
    lo = gidx * float(EPG)
    el = jnp.where((lanef >= lo) & (lanef < lo + float(EPG)), logits, neg)
    v1 = jnp.max(el, axis=-1, keepdims=True)
    i1 = jnp.min(jnp.where(el == v1, lanef, big), axis=-1, keepdims=True)
    el2 = jnp.where(lanef == i1, neg, el)
    v2 = jnp.max(el2, axis=-1, keepdims=True)
    i2 = jnp.min(jnp.where(el2 == v2, lanef, big), axis=-1, keepdims=True)
    e2 = jnp.exp(v2 - v1)
    w1 = gw / (1.0 + e2)
    w2 = gw * e2 / (1.0 + e2)
    cmb = jnp.where(lanef == i1, w1, 0.0) + jnp.where(lanef == i2, w2, 0.0)
    return cmb, gidx


def _out_kernel(x_ref, h_ref, yr_ref, ya_ref, mod_ref, wgate_ref, wor_ref, wom_ref, wout_ref,
                n2_ref, rw_ref, rb_ref, x1_ref, hx_ref, rinfo_ref, tcnt_ref):
    x = x_ref[...]
    gl = _dot(h_ref[...], wgate_ref[...])
    merged2 = ((jnp.tanh(gl[:, :D]) + 1.0) * _dot(yr_ref[...], wor_ref[...])
               + (jnp.tanh(gl[:, D:]) + 1.0) * _dot(ya_ref[...], wom_ref[...]))
    mix = _dot(merged2.astype(BF), wout_ref[...])
    x1 = x + mod_ref[0, 2:3, :] * mix
    x1_ref[...] = x1
    h2 = _modulated_norm(x1, n2_ref[...], mod_ref[0, 4:5, :], mod_ref[0, 3:4, :])
    h2_hi = h2.astype(BF)
    h2_lo = (h2 - h2_hi.astype(F32)).astype(BF)
    part = _dot(h2_hi, rw_ref[...])
    logits = part[:, :LANES] + part[:, LANES:] + _dot(h2_lo, rw_ref[:, :LANES]) + rb_ref[...]
    cmb, gidx = _route(logits)

    lanef = lax.broadcasted_iota(jnp.int32, cmb.shape, 1).astype(F32)
    ghot = jnp.where(lanef == gidx, 1.0, 0.0)
    r_i = lax.broadcasted_iota(jnp.int32, (TM, TM), 0)
    c_i = lax.broadcasted_iota(jnp.int32, (TM, TM), 1)
    tri = jnp.where(r_i > c_i, 1.0, 0.0).astype(BF)
    earlier_same = jnp.sum(_dot(tri, ghot.astype(BF)) * ghot, axis=-1, keepdims=True)
    counts = jnp.sum(ghot, axis=0, keepdims=True)
    padded = jnp.floor((counts + (RUN_ALIGN - 1.0)) * (1.0 / RUN_ALIGN)) * RUN_ALIGN
    lower_groups = jnp.sum(jnp.where(lanef < gidx, padded, 0.0), axis=-1, keepdims=True)
    lpos = lower_groups + earlier_same
    s_i = lax.broadcasted_iota(jnp.int32, (TM, TS), 1)
    to_sorted = jnp.where(s_i.astype(F32) == lpos, 1.0, 0.0).astype(BF)
    c1 = cmb.astype(BF)
    c2 = (cmb - c1.astype(F32)).astype(BF)
    c3 = (cmb - c1.astype(F32) - c2.astype(F32)).astype(BF)
    payload = jnp.concatenate([h2.astype(BF), c1, c2, c3], axis=1)
    srt = lax.dot_general(to_sorted, payload, (((0,), (0,)), ((), ())), preferred_element_type=F32)
    hx_ref[...] = srt.astype(BF)

    rinfo_ref[...] = jnp.where(lanef == 0.0, gidx, jnp.where(lanef == 1.0, lpos, 0.0))
    tcnt_ref[0] = jnp.broadcast_to(counts, (SUBLANES, LANES))


def _merge_out(x, h, yr, ya, mod, mod_row, wts):
    n = x.shape[0]
    n_tiles = n // TM
    tile = lambda w: pl.BlockSpec((TM, w), lambda i: (i, 0))
    return pl.pallas_call(
        _out_kernel,
        grid=(n_tiles,),
        in_specs=[tile(D), tile(D), tile(DR), tile(DH),
                  pl.BlockSpec((1, 6, D), lambda i: (mod_row(i), 0, 0)),
                  _resident((D, 2 * D)), _resident((DR, D)), _resident((DH, D)), _resident((D, D)),
                  _resident((1, D)), _resident((D, 2 * LANES)), _resident((1, LANES))],
        out_specs=[tile(D), pl.BlockSpec((TS, XW), lambda i: (i, 0)), tile(LANES),
                   pl.BlockSpec((1, SUBLANES, LANES), lambda i: (i, 0, 0))],
        out_shape=[jax.ShapeDtypeStruct((n, D), F32), jax.ShapeDtypeStruct((n_tiles * TS, XW), BF),
                   jax.ShapeDtypeStruct((n, LANES), F32),
                   jax.ShapeDtypeStruct((n_tiles, SUBLANES, LANES), F32)],
        compiler_params=_cparams(("arbitrary",)),
        name="merge_out",
    )(x, h, yr, ya, mod, wts["w_gate"], wts["w_o_rnn"], wts["w_o_mla"], wts["w_out"],
      wts["n2"], wts["router_w"], wts["router_b"])


def _run_copies(local_ref, far_ref, len_ref, tile, make_copy, action):
    for g in range(NG):
        idx = tile * NG + g
        n = len_ref[idx]
        local0 = local_ref[idx]
        far0 = far_ref[idx]
        for k in reversed(range(RUN_ALIGN.bit_length() - 1, TS.bit_length())):
            size = 1 << k
            done = (n >> (k + 1)) << (k + 1)

            @pl.when(((n >> k) & 1) == 1)
            def _():
                action(make_copy(pl.multiple_of(local0 + done, RUN_ALIGN),
                                 pl.multiple_of(far0 + done, RUN_ALIGN), size))


def _dispatch_kernel(local_ref, far_ref, len_ref, hx_ref, xs_in_ref, xs_ref, sem):
    del xs_in_ref

    def copy(src, dst, size):
        return pltpu.make_async_copy(hx_ref.at[pl.ds(src, size)], xs_ref.at[pl.ds(dst, size)], sem)

    tile = pl.program_id(0)
    _run_copies(local_ref, far_ref, len_ref, tile, copy, lambda c: c.start())
    _run_copies(local_ref, far_ref, len_ref, tile, copy, lambda c: c.wait())


def _dispatch(tables, hx, xs):
    n_tiles = hx.shape[0] // TS
    return pl.pallas_call(
        _dispatch_kernel,
        grid_spec=pltpu.PrefetchScalarGridSpec(
            num_scalar_prefetch=3,
            grid=(n_tiles,),
            in_specs=[pl.BlockSpec((TS, XW), lambda i, *_: (i, 0)),
                      pl.BlockSpec(memory_space=pl.ANY)],
            out_specs=pl.BlockSpec(memory_space=pl.ANY),
            scratch_shapes=[pltpu.SemaphoreType.DMA(())],
        ),
        out_shape=jax.ShapeDtypeStruct(xs.shape, xs.dtype),
        input_output_aliases={4: 0},
        compiler_params=_cparams(("arbitrary",)),
        name="dispatch",
    )(*tables, hx, xs)


def _moe_kernel(tb_ref, tg_ref, nt_ref, xs_ref, w1_ref, w3_ref, w2_ref, o_ref, acc_ref):
    j = pl.program_id(0)
    pair = pl.program_id(1)

    @pl.when(j < nt_ref[0])
    def _():
        xt = xs_ref[:, :D]
        cmb = (xs_ref[:, D:D + LANES].astype(F32) + xs_ref[:, D + LANES:D + 2 * LANES].astype(F32)
               + xs_ref[:, D + 2 * LANES:].astype(F32))
        lane = lax.broadcasted_iota(jnp.int32, cmb.shape, 1)
        first = tg_ref[j] * EPG + pair * EXPERTS_PER_STEP
        hidden = []
        for u in range(EXPERTS_PER_STEP):
            a = _dot(xt, w1_ref[u].astype(BF))
            he = (a * _sigmoid(a)) * _dot(xt, w3_ref[u].astype(BF))
            ce = jnp.sum(jnp.where(lane == first + u, cmb, 0.0), axis=-1, keepdims=True)
            hidden.append((he * ce).astype(BF))
        w2 = w2_ref[...].reshape(EXPERTS_PER_STEP * DE, D).astype(BF)
        y = _dot(jnp.concatenate(hidden, axis=1), w2)

        @pl.when(pair == 0)
        def _():
            acc_ref[...] = y

        @pl.when(pair == EPG // EXPERTS_PER_STEP - 1)
        def _():
            o_ref[...] = (acc_ref[...] + y).astype(BF)

    @pl.when((j >= nt_ref[0]) & (pair == EPG // EXPERTS_PER_STEP - 1))
    def _():
        o_ref[...] = jnp.zeros_like(o_ref)


def _experts(tile_block, tile_group, n_tiles, xs, wts):
    m = xs.shape[0]
    steps = EPG // EXPERTS_PER_STEP
    assert steps == 2, "the kernel keeps one partial sum: first step stores it, second adds and writes"

    def w_idx(j, e, tb, tg, nt):
        return (tg[j] * steps + jnp.where(j < nt[0], e, steps - 1), 0, 0)

    return pl.pallas_call(
        _moe_kernel,
        grid_spec=pltpu.PrefetchScalarGridSpec(
            num_scalar_prefetch=3,
            grid=(m // TMOE, steps),
            in_specs=[pl.BlockSpec((TMOE, XW), lambda j, e, tb, tg, nt: (tb[j], 0)),
                      pl.BlockSpec((EXPERTS_PER_STEP, D, DE), w_idx),
                      pl.BlockSpec((EXPERTS_PER_STEP, D, DE), w_idx),
                      pl.BlockSpec((EXPERTS_PER_STEP, DE, D), w_idx)],
            out_specs=pl.BlockSpec((TMOE, D), lambda j, e, tb, tg, nt: (j, 0)),
            scratch_shapes=[pltpu.VMEM((TMOE, D), F32)],
        ),
        out_shape=jax.ShapeDtypeStruct((m, D), BF),
        compiler_params=_cparams(("arbitrary", "arbitrary")),
        name="experts",
    )(tile_block, tile_group, n_tiles, xs, wts["exp_w1"], wts["exp_w3"], wts["exp_w2"])


def _combine_kernel(local_ref, far_ref, len_ref, x1_ref, rinfo_ref, mod_ref, ys_ref, o_ref,
                    buf_ref, sem, *, n_steps):
    i = pl.program_id(0)

    def runs(step, slot, action):
        def copy(dst, src, size):
            return pltpu.make_async_copy(ys_ref.at[pl.ds(src, size)],
                                         buf_ref.at[slot, pl.ds(dst, size)], sem.at[slot])

        _run_copies(local_ref, far_ref, len_ref, step, copy, action)

    @pl.when(i == 0)
    def _():
        buf_ref[...] = jnp.zeros_like(buf_ref)
        runs(0, 0, lambda c: c.start())

    @pl.when(i + 1 < n_steps)
    def _():
        runs(i + 1, (i + 1) % 2, lambda c: c.start())

    slot = i % 2
    runs(i, slot, lambda c: c.wait())
    lpos = rinfo_ref[:, 1:2]
    s_i = lax.broadcasted_iota(jnp.int32, (TM, TS), 1)
    from_sorted = jnp.where(s_i.astype(F32) == lpos, 1.0, 0.0).astype(BF)
    moe = _dot(from_sorted, buf_ref[slot])
    o_ref[...] = x1_ref[...] + mod_ref[0, 5:6, :] * moe


def _combine(tables, x1, rinfo, mod, mod_row, ys):
    n = x1.shape[0]
    n_steps = n // TM
    return pl.pallas_call(
        functools.partial(_combine_kernel, n_steps=n_steps),
        grid_spec=pltpu.PrefetchScalarGridSpec(
            num_scalar_prefetch=3,
            grid=(n_steps,),
            in_specs=[pl.BlockSpec((TM, D), lambda i, *_: (i, 0)),
                      pl.BlockSpec((TM, LANES), lambda i, *_: (i, 0)),
                      pl.BlockSpec((1, 6, D), lambda i, *_: (mod_row(i), 0, 0)),
                      pl.BlockSpec(memory_space=pl.ANY)],
            out_specs=pl.BlockSpec((TM, D), lambda i, *_: (i, 0)),
            scratch_shapes=[pltpu.VMEM((2, TS, D), BF), pltpu.SemaphoreType.DMA((2,))],
        ),
        out_shape=jax.ShapeDtypeStruct((n, D), F32),
        compiler_params=_cparams(("arbitrary",)),
        name="combine",
    )(*tables, x1, rinfo, mod, ys)


def _run_lengths(tile_counts):
    counts = tile_counts[:, 0, :NG].astype(jnp.int32)
    return ((counts + RUN_ALIGN - 1) // RUN_ALIGN) * RUN_ALIGN


def _run_tables(lengths, first_far):
    local = jnp.cumsum(lengths, axis=1) - lengths
    far = first_far[None, :] + jnp.cumsum(lengths, axis=0) - lengths
    flat = lambda a: a.astype(jnp.int32).reshape(-1)
    return flat(local), flat(far), flat(lengths)


def _group_layout(counts, max_tiles):
    padded = ((counts + TMOE - 1) // TMOE) * TMOE
    ends = jnp.cumsum(padded)
    offsets = ends - padded
    n_tiles = (ends[-1] // TMOE).astype(jnp.int32)
    tile = jnp.minimum(jnp.arange(max_tiles, dtype=jnp.int32), jnp.maximum(n_tiles - 1, 0))
    tile_group = jnp.sum((tile[:, None] * TMOE >= ends[None, :]).astype(jnp.int32), axis=1)
    return offsets, tile, tile_group, n_tiles.reshape(1)


def _pad_heads(w, perm=None, rotary_only=False):
    lead = w.shape[:-1]
    per = w.shape[-1] // NH
    w = w.reshape(lead + (NH, per))
    if perm is not None:
        nope = jnp.zeros_like(w[..., :NOPE]) if rotary_only else w[..., :NOPE]
        w = jnp.concatenate([nope, w[..., NOPE:][..., perm]], axis=-1)
    w = jnp.pad(w, [(0, 0)] * len(lead) + [(0, 0), (0, HP - per)])
    return w.reshape(lead + (NH * HP,))


def _pad_gain(g, perm, rotary_only=False):
    nope = jnp.zeros((NOPE,), F32) if rotary_only else g[:NOPE]
    g = jnp.concatenate([nope, g[NOPE:][perm], jnp.zeros((HP - QK,), F32)])
    return g.reshape(1, HP)


def _block_diag(w):
    per = BDW // LRU_BLOCK
    w = w.reshape(DR // BDW, per, LRU_BLOCK, LRU_BLOCK)
    bd = jnp.einsum("jpab,pq->jpaqb", w, jnp.eye(per, dtype=w.dtype))
    return bd.reshape(DR // BDW, BDW, BDW)


def _prepare_shared(l, p):
    w_in = p["w_in"][l]
    o1, o2, o3, o4, o5 = DR, 2 * DR, 2 * DR + QL, 2 * DR + QL + KVL, 2 * DR + QL + KVL + ROPE
    bd = jnp.stack([_block_diag(p["lru_wa"][l, 0]), _block_diag(p["lru_wx"][l, 0]),
                    _block_diag(p["lru_wa"][l, 1]), _block_diag(p["lru_wx"][l, 1])])
    bd = (0.5 * bd).astype(BF)
    wom = p["w_o_mla"][l].reshape(NH, VD, D)
    wom = jnp.pad(wom, ((0, 0), (0, HP - VD), (0, 0))).reshape(DH, D)
    router_w = jnp.concatenate([p["router_we"][l], p["router_wg"][l],
                                jnp.zeros((D, LANES - NE - NG), F32)], axis=1)
    router_b = jnp.concatenate([p["router_be"][l], p["router_bg"][l],
                                jnp.zeros((LANES - NE - NG,), F32)]).reshape(1, LANES)
    router_hi = router_w.astype(BF)
    router_lo = (router_w - router_hi.astype(F32)).astype(BF)
    router_w = jnp.concatenate([router_hi, router_lo], axis=1)
    return {
        "n1": p["norm1_g"][l].reshape(1, D), "n2": p["norm2_g"][l].reshape(1, D),
        "w_x": w_in[:, :o1].astype(BF), "w_g": w_in[:, o1:o2].astype(BF),
        "w_q": w_in[:, o2:o3].astype(BF), "w_kv": w_in[:, o3:o4], "w_kr": w_in[:, o4:o5],
        "w_gate": (0.5 * w_in[:, o5:]).astype(BF),
        "qan": p["q_a_norm"][l].reshape(1, QL), "kvan": p["kv_a_norm"][l].reshape(1, KVL),
        "w_uk": _pad_heads(p["w_uk"][l]).astype(BF),
        "w_uv": _pad_heads(p["w_uv"][l]).astype(BF),
        "conv_w": p["conv_w"][l], "conv_b": p["conv_b"][l].reshape(1, DR), "bd": bd,
        "lru_ba": 0.5 * p["lru_ba"][l], "lru_bx": 0.5 * p["lru_bx"][l], "lru_lam": p["lru_lam"][l],
        "w_o_rnn": p["w_o_rnn"][l].astype(BF), "w_o_mla": wom.astype(BF), "w_out": (0.5 * p["w_out"][l]).astype(BF),
        "router_w": router_w, "router_b": router_b,
        "exp_w1": p["exp_w1"][l], "exp_w3": p["exp_w3"][l], "exp_w2": p["exp_w2"][l],
    }


def _with_rope_order(l, p, shared, perm, rotary):
    w = dict(shared)
    zeros = lambda n: jnp.zeros((D, n), F32)
    rope_block = lambda order: [zeros(NOPE), shared["w_kr"][:, order], zeros(HP - QK)]
    kvr = [shared["w_kv"]] + rope_block(perm)
    w["w_uq"] = _pad_heads(p["w_uq"][l], perm).astype(BF)
    w["gq"] = _pad_gain(p["q_norm"][l], perm)
    w["gk"] = _pad_gain(p["k_norm"][l], perm)
    if rotary:
        pair = np.concatenate([perm[ROPE // 2:], perm[:ROPE // 2]])
        kvr += rope_block(pair)
        w["w_uq_pair"] = _pad_heads(p["w_uq"][l], pair, rotary_only=True).astype(BF)
        w["gq_pair"] = _pad_gain(p["q_norm"][l], pair, rotary_only=True)
        w["gk_pair"] = _pad_gain(p["k_norm"][l], pair, rotary_only=True)
    w["w_kvr"] = jnp.concatenate(kvr, axis=1).astype(BF)
    return w


def _rope_tables(n_tokens, perm):
    rows = n_tokens // GRID_W
    row = np.repeat(np.arange(rows), GRID_W).astype(np.float32)
    col = np.tile(np.arange(GRID_W), rows).astype(np.float32)
    axis_dim = ROPE // 2
    inv = (np.float32(ROPE_BASE) ** (-np.arange(0, axis_dim, 2, dtype=np.float32) / axis_dim)).astype(np.float32)
    ang = np.concatenate([row[:, None] * inv, col[:, None] * inv], axis=-1).astype(np.float32)
    cos, sin = np.cos(ang), np.sin(ang)
    ones = lambda n: np.ones((n_tokens, n), np.float32)
    zeros = lambda n: np.zeros((n_tokens, n), np.float32)
    cos_t = np.concatenate([ones(NOPE), cos, cos, ones(HP - QK)], axis=1)
    sin_t = np.concatenate([zeros(NOPE), -sin, sin, zeros(HP - QK)], axis=1)
    return jnp.asarray(cos_t, F32), jnp.asarray(sin_t, F32)


def kernel(x_prompt, x_sample, cache_mla_ckv, cache_mla_krope, state_rglru, c, c_ctx, norm1_g, norm2_g, w_mod, b_mod, w_in, conv_w, conv_b, lru_wa, lru_ba, lru_wx, lru_bx, lru_lam, q_a_norm, kv_a_norm, w_uq, w_uk, w_uv, q_norm, k_norm, w_o_rnn, w_o_mla, w_out, router_wg, router_bg, router_we, router_be, exp_w1, exp_w3, exp_w2):
    p = dict(norm1_g=norm1_g, norm2_g=norm2_g, w_in=w_in, conv_w=conv_w, conv_b=conv_b,
             lru_wa=lru_wa, lru_ba=lru_ba, lru_wx=lru_wx, lru_bx=lru_bx, lru_lam=lru_lam,
             q_a_norm=q_a_norm, kv_a_norm=kv_a_norm, w_uq=w_uq, w_uk=w_uk, w_uv=w_uv,
             q_norm=q_norm, k_norm=k_norm, w_o_rnn=w_o_rnn, w_o_mla=w_o_mla, w_out=w_out,
             router_wg=router_wg, router_bg=router_bg, router_we=router_we, router_be=router_be,
             exp_w1=exp_w1, exp_w3=exp_w3, exp_w2=exp_w2)
    depth = w_in.shape[0]
    nb, seq, _ = x_prompt.shape
    db, dseq, _ = x_sample.shape
    ident = np.arange(ROPE)
    halves = np.concatenate([np.arange(0, ROPE, 2), np.arange(1, ROPE, 2)])
    rope_tabs = _rope_tables(dseq, halves)
    cond8 = jnp.concatenate([c_ctx[None, :], c, jnp.zeros((SUBLANES - 1 - db, D), F32)], axis=0)
    ctx_row = lambda tile_rows: (lambda i: 0)
    lat_row = lambda tile_rows: (lambda i: (i * tile_rows) // dseq + 1)
    n_ctx, n_lat = nb * seq, db * dseq
    run_padding = ((n_ctx + n_lat) // TM) * NG * (RUN_ALIGN - 1)
    max_tiles = -(-(n_ctx + n_lat + run_padding) // TMOE) + NG
    per_seq = lambda arrs, b, t: [a.reshape(b, t, a.shape[-1]) for a in arrs]
    flat = lambda a: a.reshape(-1, a.shape[-1])

    y_prompt, y_sample = x_prompt.reshape(n_ctx, D), x_sample.reshape(n_lat, D)
    ckv_list, krope_list, rnn_list = [], [], []
    for l in range(depth):
        shared = _prepare_shared(l, p)
        w_ctx = _with_rope_order(l, p, shared, ident, False)
        w_lat = _with_rope_order(l, p, shared, halves, True)
        mod = _modulation(cond8, w_mod[l], b_mod[l]).reshape(SUBLANES, 6, D)

        h, xr, gg, q, k, v, ckv, kro = _projections(y_prompt, mod, ctx_row(TM), w_ctx, None, True)
        xr, gg, q, k, v = per_seq([xr, gg, q, k, v], nb, seq)
        yr, h_fin = _rglru(xr, gg, shared, None, True)
        ya = _attention(q, k, v, None, None, NH)
        x1_c, hx_c, ri_c, tc_c = _merge_out(y_prompt, h, flat(yr), flat(ya), mod, ctx_row(TM), shared)
        ckv_list.append(ckv.reshape(nb, seq, KVL))
        krope_list.append(kro.reshape(nb, seq, ROPE))
        rnn_list.append(h_fin)

        krp_cache = jnp.pad(cache_mla_krope[:, l][..., halves], ((0, 0), (0, 0), (NOPE, HP - QK)))
        kc, vc = _cache_keys_values(cache_mla_ckv[:, l], krp_cache, w_lat)
        h, xr, gg, q, k, v = _projections(y_sample, mod, lat_row(TM), w_lat, rope_tabs, False)
        xr, gg, q, k, v = per_seq([xr, gg, q, k, v], db, dseq)
        yr, _ = _rglru(xr, gg, shared, state_rglru[:, l], False)
        ya = _attention(q, k, v, kc, vc, 2)
        x1_l, hx_l, ri_l, tc_l = _merge_out(y_sample, h, flat(yr), flat(ya), mod, lat_row(TM), shared)

        len_c, len_l = _run_lengths(tc_c), _run_lengths(tc_l)
        rows_c = jnp.sum(len_c, axis=0)
        offsets, tile_block, tile_group, n_tiles = _group_layout(rows_c + jnp.sum(len_l, axis=0), max_tiles)
        runs_c = _run_tables(len_c, offsets)
        runs_l = _run_tables(len_l, offsets + rows_c)
        xs = jnp.zeros((max_tiles * TMOE, XW), BF)
        xs = _dispatch(runs_c, hx_c, xs)
        xs = _dispatch(runs_l, hx_l, xs)
        ys = _experts(tile_block, tile_group, n_tiles, xs, shared)
        y_prompt = _combine(runs_c, x1_c, ri_c, mod, ctx_row(TM), ys)
        y_sample = _combine(runs_l, x1_l, ri_l, mod, lat_row(TM), ys)

    y_prompt, y_sample = y_prompt.reshape(nb, seq, D), y_sample.reshape(db, dseq, D)

    return (y_prompt, y_sample, jnp.stack(ckv_list, axis=1), jnp.stack(krope_list, axis=1),
            jnp.stack(rnn_list, axis=1))
```

```python
import functools
import math

import numpy as np
import jax
import jax.numpy as jnp
from jax import lax
from jax.experimental import pallas as pl
from jax.experimental.pallas import tpu as pltpu

D = 1024
DR = 1024
QL = 384
KVL = 256
NH = 8
NOPE = 64
ROPE = 32
QK = NOPE + ROPE
VD = 64
HP = 128
DH = NH * HP
GRID_W = 64
ROPE_BASE = 10000.0
EPS = 1e-6
TINY = 1e-30
LRU_C = 8.0
LRU_BLOCK = 64
BDW = 256
CH = 512
NLT = CH // 128
NG = 4
EPG = 4
NE = NG * EPG
DE = 512
LANES = 128
SUBLANES = 8
TM = 512
TS = 640
TMOE = 512
EXPERTS_PER_STEP = 2
XW = D + 3 * 128
RUN_ALIGN = 16
VMEM_LIMIT = 52 * 1024 * 1024
BF = jnp.bfloat16
F32 = jnp.float32


def _cparams(sem):
    return pltpu.CompilerParams(dimension_semantics=sem, vmem_limit_bytes=VMEM_LIMIT)


def _dot(a, b):
    return jnp.dot(a, b, preferred_element_type=F32)


def _dot_nt(a, b):
    return lax.dot_general(a, b, (((1,), (1,)), ((), ())), preferred_element_type=F32)


def _rms(x, g, width):
    ms = jnp.sum(x * x, axis=-1, keepdims=True) * (1.0 / width)
    return x * lax.rsqrt(ms + EPS) * g


def _modulated_norm(x, g, scale, shift):
    return _rms(x, g * (1.0 + scale), D) + shift


def _mod_kernel(c_ref, w_ref, b_ref, o_ref):
    c = c_ref[...]
    s = c * jax.nn.sigmoid(c)
    o_ref[...] = _dot(s, w_ref[...]) + b_ref[...]


def _modulation(cond8, w_mod, b_mod):
    n = w_mod.shape[1]
    return pl.pallas_call(
        _mod_kernel,
        grid=(n // D,),
        in_specs=[
            pl.BlockSpec((SUBLANES, D), lambda j: (0, 0)),
            pl.BlockSpec((D, D), lambda j: (0, j)),
            pl.BlockSpec((1, D), lambda j: (0, j)),
        ],
        out_specs=pl.BlockSpec((SUBLANES, D), lambda j: (0, j)),
        out_shape=jax.ShapeDtypeStruct((SUBLANES, n), F32),
        compiler_params=_cparams(("arbitrary",)),
        name="modulation",
    )(cond8, w_mod, b_mod.reshape(1, n))


def _head_norm(xh, gain, cos=None, partner_scaled=None):
    ms = jnp.sum(xh * xh, axis=-1, keepdims=True) * (1.0 / QK)
    rs = lax.rsqrt(ms + EPS)
    y = xh * rs * gain
    if cos is None:
        return y
    return y * cos + partner_scaled * rs


def _keys_values(ckv, krp, wuk_ref, wuv_ref, gk, cos, partner_scaled, k_ref, v_ref):
    cb = ckv.astype(BF)
    kn = _dot(cb, wuk_ref[...])
    v_ref[...] = _dot(cb, wuv_ref[...]).astype(BF)
    for h in range(NH):
        kh = kn[:, h * HP:(h + 1) * HP] + krp
        k_ref[:, h * HP:(h + 1) * HP] = _head_norm(kh, gk, cos, partner_scaled).astype(BF)


def _proj_kernel(*refs, rope, emit_cache):
    it = iter(refs)
    x_ref, mod_ref, n1_ref = next(it), next(it), next(it)
    wx_ref, wg_ref, wq_ref, wkvr_ref = next(it), next(it), next(it), next(it)
    qan_ref, kvan_ref, wuq_ref, gq_ref = next(it), next(it), next(it), next(it)
    wuk_ref, gk_ref, wuv_ref = next(it), next(it), next(it)
    if rope:
        wuqs_ref, gqs_ref, gks_ref, cos_ref, sins_ref = (next(it) for _ in range(5))
    h_ref, xr_ref, gg_ref, q_ref, k_ref, v_ref = (next(it) for _ in range(6))
    if emit_cache:
        ckv_ref, kro_ref = next(it), next(it)

    hb = _modulated_norm(x_ref[...], n1_ref[...], mod_ref[0, 1:2, :], mod_ref[0, 0:1, :]).astype(BF)
    h_ref[...] = hb
    xr_ref[...] = _dot_nt(hb, wx_ref[...]).astype(BF)
    gg_ref[...] = jax.nn.gelu(_dot_nt(hb, wg_ref[...])).astype(BF)

    qnb = _rms(_dot_nt(hb, wq_ref[...]), qan_ref[...], QL).astype(BF)
    q = _dot(qnb, wuq_ref[...])
    gq = gq_ref[...]
    cos = q_partner = q_pair_scale = None
    if rope:
        cos, sins = cos_ref[...], sins_ref[...]
        q_partner = _dot(qnb, wuqs_ref[...])
        q_pair_scale = gqs_ref[...] * sins
    for hd in range(NH):
        cols = slice(hd * HP, (hd + 1) * HP)
        partner = q_partner[:, cols] * q_pair_scale if rope else None
        q_ref[:, cols] = _head_norm(q[:, cols], gq, cos, partner).astype(BF)

    kvr = _dot_nt(hb, wkvr_ref[...])
    ckv = _rms(kvr[:, :KVL], kvan_ref[...], KVL)
    krp = kvr[:, KVL:KVL + HP]
    k_partner = kvr[:, KVL + HP:KVL + 2 * HP] * (gks_ref[...] * sins) if rope else None
    if emit_cache:
        ckv_ref[...] = ckv
        kro_ref[...] = krp[:, NOPE:NOPE + ROPE]
    _keys_values(ckv, krp, wuk_ref, wuv_ref, gk_ref[...], cos, k_partner, k_ref, v_ref)


def _resident(shape):
    return pl.BlockSpec(shape, lambda i: (0,) * len(shape), pipeline_mode=pl.Buffered(1))


def _projections(x, mod, mod_row, wts, rope_tabs, emit_cache):
    n = x.shape[0]
    rope = rope_tabs is not None
    tile = lambda w: pl.BlockSpec((TM, w), lambda i: (i, 0))
    in_specs = [
        tile(D),
        pl.BlockSpec((1, 6, D), lambda i: (mod_row(i), 0, 0)),
        _resident((1, D)),
        _resident((DR, D)), _resident((DR, D)), _resident((QL, D)), _resident(wts["w_kvr"].shape),
        _resident((1, QL)), _resident((1, KVL)), _resident((QL, DH)), _resident((1, HP)),
        _resident((KVL, DH)), _resident((1, HP)), _resident((KVL, DH)),
    ]
    args = [x, mod, wts["n1"], wts["w_x"], wts["w_g"], wts["w_q"], wts["w_kvr"],
            wts["qan"], wts["kvan"], wts["w_uq"], wts["gq"], wts["w_uk"], wts["gk"], wts["w_uv"]]
    if rope:
        tiles_per_seq = rope_tabs[0].shape[0] // TM
        in_specs += [_resident((QL, DH)), _resident((1, HP)), _resident((1, HP))]
        in_specs += [pl.BlockSpec((TM, HP), lambda i: (i % tiles_per_seq, 0))] * 2
        args += [wts["w_uq_pair"], wts["gq_pair"], wts["gk_pair"]] + list(rope_tabs)
    out_specs = [tile(D), tile(DR), tile(DR), tile(DH), tile(DH), tile(DH)]
    out_shape = [jax.ShapeDtypeStruct((n, D), BF), jax.ShapeDtypeStruct((n, DR), BF),
                 jax.ShapeDtypeStruct((n, DR), BF), jax.ShapeDtypeStruct((n, DH), BF),
                 jax.ShapeDtypeStruct((n, DH), BF), jax.ShapeDtypeStruct((n, DH), BF)]
    if emit_cache:
        out_specs += [tile(KVL), tile(ROPE)]
        out_shape += [jax.ShapeDtypeStruct((n, KVL), F32), jax.ShapeDtypeStruct((n, ROPE), F32)]
    return pl.pallas_call(
        functools.partial(_proj_kernel, rope=rope, emit_cache=emit_cache),
        grid=(n // TM,),
        in_specs=in_specs,
        out_specs=out_specs,
        out_shape=out_shape,
        compiler_params=_cparams(("arbitrary",)),
        name="projections",
    )(*args)


def _cache_kv_kernel(ckv_ref, krp_ref, wuk_ref, gk_ref, wuv_ref, k_ref, v_ref):
    _keys_values(ckv_ref[0], krp_ref[0], wuk_ref, wuv_ref, gk_ref[...], None, None,
                 k_ref.at[0], v_ref.at[0])


def _cache_keys_values(ckv, krp, wts):
    b, s, _ = ckv.shape
    full = lambda shape: pl.BlockSpec(shape, lambda i: (0,) * len(shape))
    return pl.pallas_call(
        _cache_kv_kernel,
        grid=(b,),
        in_specs=[pl.BlockSpec((1, s, KVL), lambda i: (i, 0, 0)),
                  pl.BlockSpec((1, s, HP), lambda i: (i, 0, 0)),
                  full((KVL, DH)), full((1, HP)), full((KVL, DH))],
        out_specs=[pl.BlockSpec((1, s, DH), lambda i: (i, 0, 0))] * 2,
        out_shape=[jax.ShapeDtypeStruct((b, s, DH), BF)] * 2,
        compiler_params=_cparams(("arbitrary",)),
        name="cache_keys_values",
    )(ckv, krp, wts["w_uk"], wts["gk"], wts["w_uv"])


def _sigmoid(x):
    return 0.5 * jnp.tanh(0.5 * x) + 0.5


def _tile_scan(a, b, forward):
    row = lax.broadcasted_iota(jnp.int32, a.shape, 0)
    for k in (1, 2, 4):
        if forward:
            shift, valid = k, row >= k
        else:
            shift, valid = SUBLANES - k, row < SUBLANES - k
        a_prev = jnp.where(valid, pltpu.roll(a, shift, 0), 1.0)
        b_prev = jnp.where(valid, pltpu.roll(b, shift, 0), 0.0)
        b = a * b_prev + b
        a = a * a_prev
    return a, b


def _rglru_kernel(*refs, t, has_h0, emit_state):
    it = iter(refs)
    xr_ref, gg_ref, cw_ref, cb_ref, bd_ref, ba_ref, bx_ref, lam_ref = (next(it) for _ in range(8))
    h0_ref = next(it) if has_h0 else None
    y_ref = next(it)
    hf_ref = next(it) if emit_state else None
    a_scr, b_scr, h_scr, xpad_scr = (next(it) for _ in range(4))

    n_tiles = t // SUBLANES
    pad = SUBLANES
    zero_rows = jnp.zeros((pad, CH), F32)
    xpad_scr[0:pad, :] = zero_rows
    xpad_scr[pad + t:pad + t + pad, :] = zero_rows
    xpad_scr[pad:pad + t, :] = xr_ref[0].astype(F32)
    xpad = xpad_scr[...]
    rows_all = t + 2 * pad
    xc = cb_ref[...] + xpad[pad:pad + t, :] * cw_ref[1:2, :]
    for tap, shift in ((0, 1), (2, rows_all - 1), (3, rows_all - 2)):
        xc = xc + pltpu.roll(xpad, shift, 0)[pad:pad + t, :] * cw_ref[tap:tap + 1, :]
    for s in range(CH // BDW):
        cols = slice(s * BDW, (s + 1) * BDW)
        xs = xc[:, cols]
        xsb = xs.astype(BF)
        xh = 0.5 * xs
        for d in range(2):
            tr = jnp.tanh(_dot(xsb, bd_ref[2 * d, s]) + ba_ref[d:d + 1, cols])
            ti = jnp.tanh(_dot(xsb, bd_ref[2 * d + 1, s]) + bx_ref[d:d + 1, cols])
            nl = -lam_ref[d:d + 1, cols]
            softplus = jnp.maximum(nl, 0.0) + jnp.log(1.0 + jnp.exp(-jnp.abs(nl)))
            ch = (-0.5 * LRU_C) * softplus
            a = jnp.exp(tr * ch + ch)
            z = 1.0 - a * a
            root = z * lax.rsqrt(jnp.maximum(z, TINY))
            a_scr[d, :, cols] = a
            b_scr[d, :, cols] = root * (ti * xh + xh)

    def step(i, carry):
        out = []
        for d in range(2):
            tile = i if d == 0 else n_tiles - 1 - i
            rows = pl.ds(pl.multiple_of(tile * SUBLANES, SUBLANES), SUBLANES)
            for c in range(NLT):
                lanes = slice(c * LANES, (c + 1) * LANES)
                decay, local = _tile_scan(a_scr[d, rows, lanes], b_scr[d, rows, lanes], d == 0)
                h = local + decay * carry[d * NLT + c]
                h_scr[d, rows, lanes] = h
                last = h[SUBLANES - 1:SUBLANES, :] if d == 0 else h[0:1, :]
                out.append(jnp.broadcast_to(last, (SUBLANES, LANES)))
        return tuple(out)

    init = []
    for d in range(2):
        for c in range(NLT):
            if has_h0:
                h0 = h0_ref[0, d:d + 1, c * LANES:(c + 1) * LANES]
                init.append(jnp.broadcast_to(h0, (SUBLANES, LANES)))
            else:
                init.append(jnp.zeros((SUBLANES, LANES), F32))
    final = lax.fori_loop(0, n_tiles, step, tuple(init), unroll=2)

    if emit_state:
        for d in range(2):
            for c in range(NLT):
                hf_ref[0, d:d + 1, c * LANES:(c + 1) * LANES] = final[d * NLT + c][0:1, :]
    y_ref[0] = ((h_scr[0] + h_scr[1]) * gg_ref[0].astype(F32)).astype(BF)


def _rglru(xr, gg, wts, h0, emit_state):
    b, t, _ = xr.shape
    nc = DR // CH
    has_h0 = h0 is not None
    chunk = lambda r: pl.BlockSpec((r, CH), lambda i, j: (0, j))
    seq = pl.BlockSpec((1, t, CH), lambda i, j: (i, 0, j))
    state = pl.BlockSpec((1, 2, CH), lambda i, j: (i, 0, j))
    in_specs = [seq, seq, chunk(4), chunk(1),
                pl.BlockSpec((4, CH // BDW, BDW, BDW), lambda i, j: (0, j, 0, 0)),
                chunk(2), chunk(2), chunk(2)]
    args = [xr, gg, wts["conv_w"], wts["conv_b"], wts["bd"], wts["lru_ba"], wts["lru_bx"], wts["lru_lam"]]
    if has_h0:
        in_specs.append(state)
        args.append(h0)
    out_specs = [seq]
    out_shape = [jax.ShapeDtypeStruct((b, t, DR), BF)]
    if emit_state:
        out_specs.append(state)
        out_shape.append(jax.ShapeDtypeStruct((b, 2, DR), F32))
    res = pl.pallas_call(
        functools.partial(_rglru_kernel, t=t, has_h0=has_h0, emit_state=emit_state),
        grid=(b, nc),
        in_specs=in_specs,
        out_specs=out_specs,
        out_shape=out_shape,
        scratch_shapes=[pltpu.VMEM((2, t, CH), F32)] * 3 + [pltpu.VMEM((t + 2 * SUBLANES, CH), F32)],
        compiler_params=_cparams(("arbitrary", "arbitrary")),
        name="rglru",
    )(*args)
    return res if emit_state else (res[0], None)


def _attn_kernel(*refs, t, n_heads, has_ctx, q_block):
    it = iter(refs)
    q_ref, k_ref, v_ref = next(it), next(it), next(it)
    kc_ref = vc_ref = None
    if has_ctx:
        kc_ref, vc_ref = next(it), next(it)
    o_ref = next(it)
    log2_scale = (QK ** -0.5) * math.log2(math.e)
    for hd in range(n_heads):
        cols = slice(hd * HP, (hd + 1) * HP)
        k = k_ref[0, :, cols]
        v = v_ref[0, :, cols]
        if has_ctx:
            kc = kc_ref[0, :, cols]
            vc = vc_ref[0, :, cols]
        for qb in range(t // q_block):
            rows = slice(qb * q_block, (qb + 1) * q_block)
            q = q_ref[0, rows, cols]
            s = _dot_nt(q, k) * log2_scale
            m = jnp.max(s, axis=-1, keepdims=True)
            if has_ctx:
                sc = _dot_nt(q, kc) * log2_scale
                m = jnp.maximum(m, jnp.max(sc, axis=-1, keepdims=True))
            p = jnp.exp2(s - m)
            den = jnp.sum(p, axis=-1, keepdims=True)
            o = _dot(p.astype(BF), v)
            if has_ctx:
                pc = jnp.exp2(sc - m)
                den = den + jnp.sum(pc, axis=-1, keepdims=True)
                o = o + _dot(pc.astype(BF), vc)
            o_ref[0, rows, cols] = (o / den).astype(BF)


def _attention(q, k, v, kc, vc, heads_per_step):
    b, t, _ = q.shape
    has_ctx = kc is not None
    w = heads_per_step * HP
    blk = lambda n: pl.BlockSpec((1, n, w), lambda i, j: (i, 0, j))
    in_specs = [blk(t), blk(t), blk(t)]
    args = [q, k, v]
    if has_ctx:
        in_specs += [blk(kc.shape[1])] * 2
        args += [kc, vc]
    return pl.pallas_call(
        functools.partial(_attn_kernel, t=t, n_heads=heads_per_step, has_ctx=has_ctx, q_block=min(t, 256)),
        grid=(b, NH // heads_per_step),
        in_specs=in_specs,
        out_specs=blk(t),
        out_shape=jax.ShapeDtypeStruct((b, t, DH), BF),
        compiler_params=_cparams(("arbitrary", "arbitrary")),
        name="attention",
    )(*args)


def _route(logits):
    lane = lax.broadcasted_iota(jnp.int32, logits.shape, 1)
    lanef = lane.astype(F32)
    neg = -jnp.inf
    big = float(LANES)
    gl = jnp.where((lane >= NE) & (lane < NE + NG), logits, neg)
    gmax = jnp.max(gl, axis=-1, keepdims=True)
    gidx = jnp.min(jnp.where(gl == gmax, lanef, big), axis=-1, keepdims=True) - float(NE)
    gw = 1.0 / jnp.sum(jnp.exp(gl - gmax), axis=-1, keepdims=True)
    lo = gidx * float(EPG)
    el = jnp.where((lanef >= lo) & (lanef < lo + float(EPG)), logits, neg)
    v1 = jnp.max(el, axis=-1, keepdims=True)
    i1 = jnp.min(jnp.where(el == v1, lanef, big), axis=-1, keepdims=True)
    el2 = jnp.where(lanef == i1, neg, el)
    v2 = jnp.max(el2, axis=-1, keepdims=True)
    i2 = jnp.min(jnp.where(el2 == v2, lanef, big), axis=-1, keepdims=True)
    e2 = jnp.exp(v2 - v1)
    w1 = gw / (1.0 + e2)
    w2 = gw * e2 / (1.0 + e2)
    cmb = jnp.where(lanef == i1, w1, 0.0) + jnp.where(lanef == i2, w2, 0.0)
    return cmb, gidx


def _out_kernel(x_ref, h_ref, yr_ref, ya_ref, mod_ref, wgate_ref, wor_ref, wom_ref, wout_ref,
                n2_ref, rw_ref, rb_ref, x1_ref, hx_ref, rinfo_ref, tcnt_ref):
    x = x_ref[...]
    gl = _dot_nt(h_ref[...], wgate_ref[...])
    merged2 = ((jnp.tanh(gl[:, :D]) + 1.0) * _dot(yr_ref[...], wor_ref[...])
               + (jnp.tanh(gl[:, D:]) + 1.0) * _dot(ya_ref[...], wom_ref[...]))
    mix = _dot(merged2.astype(BF), wout_ref[...])
    x1 = x + mod_ref[0, 2:3, :] * mix
    x1_ref[...] = x1
    h2 = _modulated_norm(x1, n2_ref[...], mod_ref[0, 4:5, :], mod_ref[0, 3:4, :])
    h2_hi = h2.astype(BF)
    h2_lo = (h2 - h2_hi.astype(F32)).astype(BF)
    part = _dot(h2_hi, rw_ref[...])
    logits = part[:, :LANES] + part[:, LANES:] + _dot(h2_lo, rw_ref[:, :LANES]) + rb_ref[...]
    cmb, gidx = _route(logits)

    lanef = lax.broadcasted_iota(jnp.int32, cmb.shape, 1).astype(F32)
    ghot = jnp.where(lanef == gidx, 1.0, 0.0)
    r_i = lax.broadcasted_iota(jnp.int32, (TM, TM), 0)
    c_i = lax.broadcasted_iota(jnp.int32, (TM, TM), 1)
    tri = jnp.where(r_i > c_i, 1.0, 0.0).astype(BF)
    earlier_same = jnp.sum(_dot(tri, ghot.astype(BF)) * ghot, axis=-1, keepdims=True)
    counts = jnp.sum(ghot, axis=0, keepdims=True)
    padded = jnp.floor((counts + (RUN_ALIGN - 1.0)) * (1.0 / RUN_ALIGN)) * RUN_ALIGN
    lower_groups = jnp.sum(jnp.where(lanef < gidx, padded, 0.0), axis=-1, keepdims=True)
    lpos = lower_groups + earlier_same
    s_i = lax.broadcasted_iota(jnp.int32, (TM, TS), 1)
    to_sorted = jnp.where(s_i.astype(F32) == lpos, 1.0, 0.0).astype(BF)
    c1 = cmb.astype(BF)
    c2 = (cmb - c1.astype(F32)).astype(BF)
    c3 = (cmb - c1.astype(F32) - c2.astype(F32)).astype(BF)
    payload = jnp.concatenate([h2.astype(BF), c1, c2, c3], axis=1)
    srt = lax.dot_general(to_sorted, payload, (((0,), (0,)), ((), ())), preferred_element_type=F32)
    hx_ref[...] = srt.astype(BF)

    rinfo_ref[...] = jnp.where(lanef == 0.0, gidx, jnp.where(lanef == 1.0, lpos, 0.0))
    tcnt_ref[0] = jnp.broadcast_to(counts, (SUBLANES, LANES))


def _merge_out(x, h, yr, ya, mod, mod_row, wts):
    n = x.shape[0]
    n_tiles = n // TM
    tile = lambda w: pl.BlockSpec((TM, w), lambda i: (i, 0))
    return pl.pallas_call(
        _out_kernel,
        grid=(n_tiles,),
        in_specs=[tile(D), tile(D), tile(DR), tile(DH),
                  pl.BlockSpec((1, 6, D), lambda i: (mod_row(i), 0, 0)),
                  _resident((2 * D, D)), _resident((DR, D)), _resident((DH, D)), _resident((D, D)),
                  _resident((1, D)), _resident((D, 2 * LANES)), _resident((1, LANES))],
        out_specs=[tile(D), pl.BlockSpec((TS, XW), lambda i: (i, 0)), tile(LANES),
                   pl.BlockSpec((1, SUBLANES, LANES), lambda i: (i, 0, 0))],
        out_shape=[jax.ShapeDtypeStruct((n, D), F32), jax.ShapeDtypeStruct((n_tiles * TS, XW), BF),
                   jax.ShapeDtypeStruct((n, LANES), F32),
                   jax.ShapeDtypeStruct((n_tiles, SUBLANES, LANES), F32)],
        compiler_params=_cparams(("arbitrary",)),
        name="merge_out",
    )(x, h, yr, ya, mod, wts["w_gate"], wts["w_o_rnn"], wts["w_o_mla"], wts["w_out"],
      wts["n2"], wts["router_w"], wts["router_b"])


def _run_copies(local_ref, far_ref, len_ref, tile, make_copy, action):
    for g in range(NG):
        idx = tile * NG + g
        n = len_ref[idx]
        local0 = local_ref[idx]
        far0 = far_ref[idx]
        for k in reversed(range(RUN_ALIGN.bit_length() - 1, TS.bit_length())):
            size = 1 << k
            done = (n >> (k + 1)) << (k + 1)

            @pl.when(((n >> k) & 1) == 1)
            def _():
                action(make_copy(pl.multiple_of(local0 + done, RUN_ALIGN),
                                 pl.multiple_of(far0 + done, RUN_ALIGN), size))


def _dispatch_kernel(local_ref, far_ref, len_ref, hx_ref, xs_in_ref, xs_ref, sem):
    del xs_in_ref

    def copy(src, dst, size):
        return pltpu.make_async_copy(hx_ref.at[pl.ds(src, size)], xs_ref.at[pl.ds(dst, size)], sem)

    tile = pl.program_id(0)
    _run_copies(local_ref, far_ref, len_ref, tile, copy, lambda c: c.start())
    _run_copies(local_ref, far_ref, len_ref, tile, copy, lambda c: c.wait())


def _dispatch(tables, hx, xs):
    n_tiles = hx.shape[0] // TS
    return pl.pallas_call(
        _dispatch_kernel,
        grid_spec=pltpu.PrefetchScalarGridSpec(
            num_scalar_prefetch=3,
            grid=(n_tiles,),
            in_specs=[pl.BlockSpec((TS, XW), lambda i, *_: (i, 0)),
                      pl.BlockSpec(memory_space=pl.ANY)],
            out_specs=pl.BlockSpec(memory_space=pl.ANY),
            scratch_shapes=[pltpu.SemaphoreType.DMA(())],
        ),
        out_shape=jax.ShapeDtypeStruct(xs.shape, xs.dtype),
        input_output_aliases={4: 0},
        compiler_params=_cparams(("arbitrary",)),
        name="dispatch",
    )(*tables, hx, xs)


def _moe_kernel(tb_ref, tg_ref, nt_ref, xs_ref, w1_ref, w3_ref, w2_ref, o_ref, acc_ref):
    j = pl.program_id(0)
    pair = pl.program_id(1)

    @pl.when(j < nt_ref[0])
    def _():
        xt = xs_ref[:, :D]
        cmb = (xs_ref[:, D:D + LANES].astype(F32) + xs_ref[:, D + LANES:D + 2 * LANES].astype(F32)
               + xs_ref[:, D + 2 * LANES:].astype(F32))
        lane = lax.broadcasted_iota(jnp.int32, cmb.shape, 1)
        first = tg_ref[j] * EPG + pair * EXPERTS_PER_STEP
        hidden = []
        for u in range(EXPERTS_PER_STEP):
            a = _dot(xt, w1_ref[u].astype(BF))
            he = (a * _sigmoid(a)) * _dot(xt, w3_ref[u].astype(BF))
            ce = jnp.sum(jnp.where(lane == first + u, cmb, 0.0), axis=-1, keepdims=True)
            hidden.append((he * ce).astype(BF))
        w2 = w2_ref[...].reshape(EXPERTS_PER_STEP * DE, D).astype(BF)
        y = _dot(jnp.concatenate(hidden, axis=1), w2)

        @pl.when(pair == 0)
        def _():
            acc_ref[...] = y

        @pl.when(pair == EPG // EXPERTS_PER_STEP - 1)
        def _():
            o_ref[...] = (acc_ref[...] + y).astype(BF)

    @pl.when((j >= nt_ref[0]) & (pair == EPG // EXPERTS_PER_STEP - 1))
    def _():
        o_ref[...] = jnp.zeros_like(o_ref)


def _experts(tile_block, tile_group, n_tiles, xs, wts):
    m = xs.shape[0]
    steps = EPG // EXPERTS_PER_STEP
    assert steps == 2, "the kernel keeps one partial sum: first step stores it, second adds and writes"

    def w_idx(j, e, tb, tg, nt):
        return (tg[j] * steps + jnp.where(j < nt[0], e, steps - 1), 0, 0)

    return pl.pallas_call(
        _moe_kernel,
        grid_spec=pltpu.PrefetchScalarGridSpec(
            num_scalar_prefetch=3,
            grid=(m // TMOE, steps),
            in_specs=[pl.BlockSpec((TMOE, XW), lambda j, e, tb, tg, nt: (tb[j], 0)),
                      pl.BlockSpec((EXPERTS_PER_STEP, D, DE), w_idx),
                      pl.BlockSpec((EXPERTS_PER_STEP, D, DE), w_idx),
                      pl.BlockSpec((EXPERTS_PER_STEP, DE, D), w_idx)],
            out_specs=pl.BlockSpec((TMOE, D), lambda j, e, tb, tg, nt: (j, 0)),
            scratch_shapes=[pltpu.VMEM((TMOE, D), F32)],
        ),
        out_shape=jax.ShapeDtypeStruct((m, D), BF),
        compiler_params=_cparams(("arbitrary", "arbitrary")),
        name="experts",
    )(tile_block, tile_group, n_tiles, xs, wts["exp_w1"], wts["exp_w3"], wts["exp_w2"])


def _combine_kernel(local_ref, far_ref, len_ref, x1_ref, rinfo_ref, mod_ref, ys_ref, o_ref,
                    buf_ref, sem, *, n_steps):
    i = pl.program_id(0)

    def runs(step, slot, action):
        def copy(dst, src, size):
            return pltpu.make_async_copy(ys_ref.at[pl.ds(src, size)],
                                         buf_ref.at[slot, pl.ds(dst, size)], sem.at[slot])

        _run_copies(local_ref, far_ref, len_ref, step, copy, action)

    @pl.when(i == 0)
    def _():
        buf_ref[...] = jnp.zeros_like(buf_ref)
        runs(0, 0, lambda c: c.start())

    @pl.when(i + 1 < n_steps)
    def _():
        runs(i + 1, (i + 1) % 2, lambda c: c.start())

    slot = i % 2
    runs(i, slot, lambda c: c.wait())
    lpos = rinfo_ref[:, 1:2]
    s_i = lax.broadcasted_iota(jnp.int32, (TM, TS), 1)
    from_sorted = jnp.where(s_i.astype(F32) == lpos, 1.0, 0.0).astype(BF)
    moe = _dot(from_sorted, buf_ref[slot])
    o_ref[...] = x1_ref[...] + mod_ref[0, 5:6, :] * moe


def _combine(tables, x1, rinfo, mod, mod_row, ys):
    n = x1.shape[0]
    n_steps = n // TM
    return pl.pallas_call(
        functools.partial(_combine_kernel, n_steps=n_steps),
        grid_spec=pltpu.PrefetchScalarGridSpec(
            num_scalar_prefetch=3,
            grid=(n_steps,),
            in_specs=[pl.BlockSpec((TM, D), lambda i, *_: (i, 0)),
                      pl.BlockSpec((TM, LANES), lambda i, *_: (i, 0)),
                      pl.BlockSpec((1, 6, D), lambda i, *_: (mod_row(i), 0, 0)),
                      pl.BlockSpec(memory_space=pl.ANY)],
            out_specs=pl.BlockSpec((TM, D), lambda i, *_: (i, 0)),
            scratch_shapes=[pltpu.VMEM((2, TS, D), BF), pltpu.SemaphoreType.DMA((2,))],
        ),
        out_shape=jax.ShapeDtypeStruct((n, D), F32),
        compiler_params=_cparams(("arbitrary",)),
        name="combine",
    )(*tables, x1, rinfo, mod, ys)


def _run_lengths(tile_counts):
    counts = tile_counts[:, 0, :NG].astype(jnp.int32)
    return ((counts + RUN_ALIGN - 1) // RUN_ALIGN) * RUN_ALIGN


def _run_tables(lengths, first_far):
    local = jnp.cumsum(lengths, axis=1) - lengths
    far = first_far[None, :] + jnp.cumsum(lengths, axis=0) - lengths
    flat = lambda a: a.astype(jnp.int32).reshape(-1)
    return flat(local), flat(far), flat(lengths)


def _group_layout(counts, max_tiles):
    padded = ((counts + TMOE - 1) // TMOE) * TMOE
    ends = jnp.cumsum(padded)
    offsets = ends - padded
    n_tiles = (ends[-1] // TMOE).astype(jnp.int32)
    tile = jnp.minimum(jnp.arange(max_tiles, dtype=jnp.int32), jnp.maximum(n_tiles - 1, 0))
    tile_group = jnp.sum((tile[:, None] * TMOE >= ends[None, :]).astype(jnp.int32), axis=1)
    return offsets, tile, tile_group, n_tiles.reshape(1)


def _pad_heads(w, perm=None, rotary_only=False):
    lead = w.shape[:-1]
    per = w.shape[-1] // NH
    w = w.reshape(lead + (NH, per))
    if perm is not None:
        nope = jnp.zeros_like(w[..., :NOPE]) if rotary_only else w[..., :NOPE]
        w = jnp.concatenate([nope, w[..., NOPE:][..., perm]], axis=-1)
    w = jnp.pad(w, [(0, 0)] * len(lead) + [(0, 0), (0, HP - per)])
    return w.reshape(lead + (NH * HP,))


def _pad_gain(g, perm, rotary_only=False):
    nope = jnp.zeros((NOPE,), F32) if rotary_only else g[:NOPE]
    g = jnp.concatenate([nope, g[NOPE:][perm], jnp.zeros((HP - QK,), F32)])
    return g.reshape(1, HP)


def _block_diag(w):
    per = BDW // LRU_BLOCK
    w = w.reshape(DR // BDW, per, LRU_BLOCK, LRU_BLOCK)
    bd = jnp.einsum("jpab,pq->jpaqb", w, jnp.eye(per, dtype=w.dtype))
    return bd.reshape(DR // BDW, BDW, BDW)


def _prepare_shared(l, p):
    w_in = lambda a, b: jnp.transpose(lax.slice(p["w_in"], (l, 0, a), (l + 1, D, b)).reshape(D, b - a))
    o1, o2, o3, o4, o5 = DR, 2 * DR, 2 * DR + QL, 2 * DR + QL + KVL, 2 * DR + QL + KVL + ROPE
    bd = jnp.stack([_block_diag(p["lru_wa"][l, 0]), _block_diag(p["lru_wx"][l, 0]),
                    _block_diag(p["lru_wa"][l, 1]), _block_diag(p["lru_wx"][l, 1])])
    bd = (0.5 * bd).astype(BF)
    wom = p["w_o_mla"][l].reshape(NH, VD, D)
    wom = jnp.pad(wom, ((0, 0), (0, HP - VD), (0, 0))).reshape(DH, D)
    router_w = jnp.concatenate([p["router_we"][l], p["router_wg"][l],
                                jnp.zeros((D, LANES - NE - NG), F32)], axis=1)
    router_b = jnp.concatenate([p["router_be"][l], p["router_bg"][l],
                                jnp.zeros((LANES - NE - NG,), F32)]).reshape(1, LANES)
    router_hi = router_w.astype(BF)
    router_lo = (router_w - router_hi.astype(F32)).astype(BF)
    router_w = jnp.concatenate([router_hi, router_lo], axis=1)
    return {
        "n1": p["norm1_g"][l].reshape(1, D), "n2": p["norm2_g"][l].reshape(1, D),
        "w_x": w_in(0, o1).astype(BF), "w_g": w_in(o1, o2).astype(BF),
        "w_q": w_in(o2, o3).astype(BF), "w_kv": w_in(o3, o4), "w_kr": w_in(o4, o5),
        "w_gate": (0.5 * w_in(o5, o5 + 2 * D)).astype(BF),
        "qan": p["q_a_norm"][l].reshape(1, QL), "kvan": p["kv_a_norm"][l].reshape(1, KVL),
        "w_uk": _pad_heads(p["w_uk"][l]).astype(BF),
        "w_uv": _pad_heads(p["w_uv"][l]).astype(BF),
        "conv_w": p["conv_w"][l], "conv_b": p["conv_b"][l].reshape(1, DR), "bd": bd,
        "lru_ba": 0.5 * p["lru_ba"][l], "lru_bx": 0.5 * p["lru_bx"][l], "lru_lam": p["lru_lam"][l],
        "w_o_rnn": p["w_o_rnn"][l].astype(BF), "w_o_mla": wom.astype(BF), "w_out": (0.5 * p["w_out"][l]).astype(BF),
        "router_w": router_w, "router_b": router_b,
        "exp_w1": p["exp_w1"][l], "exp_w3": p["exp_w3"][l], "exp_w2": p["exp_w2"][l],
    }


def _with_rope_order(l, p, shared, perm, rotary):
    w = dict(shared)
    zeros = lambda n: jnp.zeros((n, D), F32)
    rope_block = lambda order: [zeros(NOPE), shared["w_kr"][order, :], zeros(HP - QK)]
    kvr = [shared["w_kv"]] + rope_block(perm)
    w["w_uq"] = _pad_heads(p["w_uq"][l], perm).astype(BF)
    w["gq"] = _pad_gain(p["q_norm"][l], perm)
    w["gk"] = _pad_gain(p["k_norm"][l], perm)
    if rotary:
        pair = np.concatenate([perm[ROPE // 2:], perm[:ROPE // 2]])
        kvr += rope_block(pair)
        w["w_uq_pair"] = _pad_heads(p["w_uq"][l], pair, rotary_only=True).astype(BF)
        w["gq_pair"] = _pad_gain(p["q_norm"][l], pair, rotary_only=True)
        w["gk_pair"] = _pad_gain(p["k_norm"][l], pair, rotary_only=True)
    w["w_kvr"] = jnp.concatenate(kvr, axis=0).astype(BF)
    return w


def _rope_tables(n_tokens, perm):
    rows = n_tokens // GRID_W
    row = np.repeat(np.arange(rows), GRID_W).astype(np.float32)
    col = np.tile(np.arange(GRID_W), rows).astype(np.float32)
    axis_dim = ROPE // 2
    inv = (np.float32(ROPE_BASE) ** (-np.arange(0, axis_dim, 2, dtype=np.float32) / axis_dim)).astype(np.float32)
    ang = np.concatenate([row[:, None] * inv, col[:, None] * inv], axis=-1).astype(np.float32)
    cos, sin = np.cos(ang), np.sin(ang)
    ones = lambda n: np.ones((n_tokens, n), np.float32)
    zeros = lambda n: np.zeros((n_tokens, n), np.float32)
    cos_t = np.concatenate([ones(NOPE), cos, cos, ones(HP - QK)], axis=1)
    sin_t = np.concatenate([zeros(NOPE), -sin, sin, zeros(HP - QK)], axis=1)
    return jnp.asarray(cos_t, F32), jnp.asarray(sin_t, F32)


def kernel(x_prompt, x_sample, cache_mla_ckv, cache_mla_krope, state_rglru, c, c_ctx, norm1_g, norm2_g, w_mod, b_mod, w_in, conv_w, conv_b, lru_wa, lru_ba, lru_wx, lru_bx, lru_lam, q_a_norm, kv_a_norm, w_uq, w_uk, w_uv, q_norm, k_norm, w_o_rnn, w_o_mla, w_out, router_wg, router_bg, router_we, router_be, exp_w1, exp_w3, exp_w2):
    p = dict(norm1_g=norm1_g, norm2_g=norm2_g, w_in=w_in, conv_w=conv_w, conv_b=conv_b,
             lru_wa=lru_wa, lru_ba=lru_ba, lru_wx=lru_wx, lru_bx=lru_bx, lru_lam=lru_lam,
             q_a_norm=q_a_norm, kv_a_norm=kv_a_norm, w_uq=w_uq, w_uk=w_uk, w_uv=w_uv,
             q_norm=q_norm, k_norm=k_norm, w_o_rnn=w_o_rnn, w_o_mla=w_o_mla, w_out=w_out,
             router_wg=router_wg, router_bg=router_bg, router_we=router_we, router_be=router_be,
             exp_w1=exp_w1, exp_w3=exp_w3, exp_w2=exp_w2)
    depth = w_in.shape[0]
    nb, seq, _ = x_prompt.shape
    db, dseq, _ = x_sample.shape
    ident = np.arange(ROPE)
    halves = np.concatenate([np.arange(0, ROPE, 2), np.arange(1, ROPE, 2)])
    rope_tabs = _rope_tables(dseq, halves)
    cond8 = jnp.concatenate([c_ctx[None, :], c, jnp.zeros((SUBLANES - 1 - db, D), F32)], axis=0)
    ctx_row = lambda tile_rows: (lambda i: 0)
    lat_row = lambda tile_rows: (lambda i: (i * tile_rows) // dseq + 1)
    n_ctx, n_lat = nb * seq, db * dseq
    run_padding = ((n_ctx + n_lat) // TM) * NG * (RUN_ALIGN - 1)
    max_tiles = -(-(n_ctx + n_lat + run_padding) // TMOE) + NG
    per_seq = lambda arrs, b, t: [a.reshape(b, t, a.shape[-1]) for a in arrs]
    flat = lambda a: a.reshape(-1, a.shape[-1])

    y_prompt, y_sample = x_prompt.reshape(n_ctx, D), x_sample.reshape(n_lat, D)
    ckv_list, krope_list, rnn_list = [], [], []
    for l in range(depth):
        shared = _prepare_shared(l, p)
        w_ctx = _with_rope_order(l, p, shared, ident, False)
        w_lat = _with_rope_order(l, p, shared, halves, True)
        mod = _modulation(cond8, w_mod[l], b_mod[l]).reshape(SUBLANES, 6, D)

        h, xr, gg, q, k, v, ckv, kro = _projections(y_prompt, mod, ctx_row(TM), w_ctx, None, True)
        xr, gg, q, k, v = per_seq([xr, gg, q, k, v], nb, seq)
        yr, h_fin = _rglru(xr, gg, shared, None, True)
        ya = _attention(q, k, v, None, None, NH)
        x1_c, hx_c, ri_c, tc_c = _merge_out(y_prompt, h, flat(yr), flat(ya), mod, ctx_row(TM), shared)
        ckv_list.append(ckv.reshape(nb, seq, KVL))
        krope_list.append(kro.reshape(nb, seq, ROPE))
        rnn_list.append(h_fin)

        krp_cache = jnp.pad(cache_mla_krope[:, l][..., halves], ((0, 0), (0, 0), (NOPE, HP - QK)))
        kc, vc = _cache_keys_values(cache_mla_ckv[:, l], krp_cache, w_lat)
        h, xr, gg, q, k, v = _projections(y_sample, mod, lat_row(TM), w_lat, rope_tabs, False)
        xr, gg, q, k, v = per_seq([xr, gg, q, k, v], db, dseq)
        yr, _ = _rglru(xr, gg, shared, state_rglru[:, l], False)
        ya = _attention(q, k, v, kc, vc, 2)
        x1_l, hx_l, ri_l, tc_l = _merge_out(y_sample, h, flat(yr), flat(ya), mod, lat_row(TM), shared)

        len_c, len_l = _run_lengths(tc_c), _run_lengths(tc_l)
        rows_c = jnp.sum(len_c, axis=0)
        offsets, tile_block, tile_group, n_tiles = _group_layout(rows_c + jnp.sum(len_l, axis=0), max_tiles)
        runs_c = _run_tables(len_c, offsets)
        runs_l = _run_tables(len_l, offsets + rows_c)
        xs = jnp.zeros((max_tiles * TMOE, XW), BF)
        xs = _dispatch(runs_c, hx_c, xs)
        xs = _dispatch(runs_l, hx_l, xs)
        ys = _experts(tile_block, tile_group, n_tiles, xs, shared)
        y_prompt = _combine(runs_c, x1_c, ri_c, mod, ctx_row(TM), ys)
        y_sample = _combine(runs_l, x1_l, ri_l, mod, lat_row(TM), ys)

    y_prompt, y_sample = y_prompt.reshape(nb, seq, D), y_sample.reshape(db, dseq, D)

    return (y_prompt, y_sample, jnp.stack(ckv_list, axis=1), jnp.stack(krope_list, axis=1),
            jnp.stack(rnn_list, axis=1))
```

```python
import functools
import math

import numpy as np
import jax
import jax.numpy as jnp
from jax import lax
from jax.experimental import pallas as pl
from jax.experimental.pallas import tpu as pltpu

D = 1024
DR = 1024
QL = 384
KVL = 256
NH = 8
NOPE = 64
ROPE = 32
QK = NOPE + ROPE
VD = 64
HP = 128
DH = NH * HP
GRID_W = 64
ROPE_BASE = 10000.0
EPS = 1e-6
TINY = 1e-30
LRU_C = 8.0
LRU_BLOCK = 64
BDW = 256
CH = 512
NLT = CH // 128
NG = 4
EPG = 4
NE = NG * EPG
DE = 512
LANES = 128
SUBLANES = 8
TM = 512
TS = 640
TMOE = 512
EXPERTS_PER_STEP = 2
TILE_GANG = 2
XW = D + 3 * 128
RUN_ALIGN = 16
VMEM_LIMIT = 52 * 1024 * 1024
BF = jnp.bfloat16
F32 = jnp.float32


def _cparams(sem):
    return pltpu.CompilerParams(dimension_semantics=sem, vmem_limit_bytes=VMEM_LIMIT)


def _dot(a, b):
    return jnp.dot(a, b, preferred_element_type=F32)


def _dot_nt(a, b):
    return lax.dot_general(a, b, (((1,), (1,)), ((), ())), preferred_element_type=F32)


def _rms(x, g, width):
    ms = jnp.sum(x * x, axis=-1, keepdims=True) * (1.0 / width)
    return x * lax.rsqrt(ms + EPS) * g


def _modulated_norm(x, g, scale, shift):
    return _rms(x, g * (1.0 + scale), D) + shift


def _mod_kernel(c_ref, w_ref, b_ref, o_ref):
    c = c_ref[...]
    s = c * jax.nn.sigmoid(c)
    o_ref[...] = _dot(s, w_ref[...]) + b_ref[...]


def _modulation(cond8, w_mod, b_mod):
    n = w_mod.shape[1]
    return pl.pallas_call(
        _mod_kernel,
        grid=(n // D,),
        in_specs=[
            pl.BlockSpec((SUBLANES, D), lambda j: (0, 0)),
            pl.BlockSpec((D, D), lambda j: (0, j)),
            pl.BlockSpec((1, D), lambda j: (0, j)),
        ],
        out_specs=pl.BlockSpec((SUBLANES, D), lambda j: (0, j)),
        out_shape=jax.ShapeDtypeStruct((SUBLANES, n), F32),
        compiler_params=_cparams(("arbitrary",)),
        name="modulation",
    )(cond8, w_mod, b_mod.reshape(1, n))


def _head_norm(xh, gain, cos=None, partner_scaled=None):
    ms = jnp.sum(xh * xh, axis=-1, keepdims=True) * (1.0 / QK)
    rs = lax.rsqrt(ms + EPS)
    y = xh * rs * gain
    if cos is None:
        return y
    return y * cos + partner_scaled * rs


def _keys_values(ckv, krp, wuk_ref, wuv_ref, gk, cos, partner_scaled, k_ref, v_ref):
    cb = ckv.astype(BF)
    kn = _dot(cb, wuk_ref[...])
    v_ref[...] = _dot(cb, wuv_ref[...]).astype(BF)
    for h in range(NH):
        kh = kn[:, h * HP:(h + 1) * HP] + krp
        k_ref[:, h * HP:(h + 1) * HP] = _head_norm(kh, gk, cos, partner_scaled).astype(BF)


def _proj_kernel(*refs, rope, emit_cache):
    it = iter(refs)
    x_ref, mod_ref, n1_ref = next(it), next(it), next(it)
    wx_ref, wg_ref, wq_ref, wkvr_ref = next(it), next(it), next(it), next(it)
    qan_ref, kvan_ref, wuq_ref, gq_ref = next(it), next(it), next(it), next(it)
    wuk_ref, gk_ref, wuv_ref = next(it), next(it), next(it)
    if rope:
        wuqs_ref, gqs_ref, gks_ref, cos_ref, sins_ref = (next(it) for _ in range(5))
    h_ref, xr_ref, gg_ref, q_ref, k_ref, v_ref = (next(it) for _ in range(6))
    if emit_cache:
        ckv_ref, kro_ref = next(it), next(it)

    hb = _modulated_norm(x_ref[...], n1_ref[...], mod_ref[0, 1:2, :], mod_ref[0, 0:1, :]).astype(BF)
    h_ref[...] = hb
    xr_ref[...] = _dot_nt(hb, wx_ref[...]).astype(BF)
    gg_ref[...] = jax.nn.gelu(_dot_nt(hb, wg_ref[...])).astype(BF)

    qnb = _rms(_dot_nt(hb, wq_ref[...]), qan_ref[...], QL).astype(BF)
    q = _dot(qnb, wuq_ref[...])
    gq = gq_ref[...]
    cos = q_partner = q_pair_scale = None
    if rope:
        cos, sins = cos_ref[...], sins_ref[...]
        q_partner = _dot(qnb, wuqs_ref[...])
        q_pair_scale = gqs_ref[...] * sins
    for hd in range(NH):
        cols = slice(hd * HP, (hd + 1) * HP)
        partner = q_partner[:, cols] * q_pair_scale if rope else None
        q_ref[:, cols] = _head_norm(q[:, cols], gq, cos, partner).astype(BF)

    kvr = _dot_nt(hb, wkvr_ref[...])
    ckv = _rms(kvr[:, :KVL], kvan_ref[...], KVL)
    krp = kvr[:, KVL:KVL + HP]
    k_partner = kvr[:, KVL + HP:KVL + 2 * HP] * (gks_ref[...] * sins) if rope else None
    if emit_cache:
        ckv_ref[...] = ckv
        kro_ref[...] = krp[:, NOPE:NOPE + ROPE]
    _keys_values(ckv, krp, wuk_ref, wuv_ref, gk_ref[...], cos, k_partner, k_ref, v_ref)


def _resident(shape):
    return pl.BlockSpec(shape, lambda i: (0,) * len(shape), pipeline_mode=pl.Buffered(1))


def _projections(x, mod, mod_row, wts, rope_tabs, emit_cache):
    n = x.shape[0]
    rope = rope_tabs is not None
    tile = lambda w: pl.BlockSpec((TM, w), lambda i: (i, 0))
    in_specs = [
        tile(D),
        pl.BlockSpec((1, 6, D), lambda i: (mod_row(i), 0, 0)),
        _resident((1, D)),
        _resident((DR, D)), _resident((DR, D)), _resident((QL, D)), _resident(wts["w_kvr"].shape),
        _resident((1, QL)), _resident((1, KVL)), _resident((QL, DH)), _resident((1, HP)),
        _resident((KVL, DH)), _resident((1, HP)), _resident((KVL, DH)),
    ]
    args = [x, mod, wts["n1"], wts["w_x"], wts["w_g"], wts["w_q"], wts["w_kvr"],
            wts["qan"], wts["kvan"], wts["w_uq"], wts["gq"], wts["w_uk"], wts["gk"], wts["w_uv"]]
    if rope:
        tiles_per_seq = rope_tabs[0].shape[0] // TM
        in_specs += [_resident((QL, DH)), _resident((1, HP)), _resident((1, HP))]
        in_specs += [pl.BlockSpec((TM, HP), lambda i: (i % tiles_per_seq, 0))] * 2
        args += [wts["w_uq_pair"], wts["gq_pair"], wts["gk_pair"]] + list(rope_tabs)
    out_specs = [tile(D), tile(DR), tile(DR), tile(DH), tile(DH), tile(DH)]
    out_shape = [jax.ShapeDtypeStruct((n, D), BF), jax.ShapeDtypeStruct((n, DR), BF),
                 jax.ShapeDtypeStruct((n, DR), BF), jax.ShapeDtypeStruct((n, DH), BF),
                 jax.ShapeDtypeStruct((n, DH), BF), jax.ShapeDtypeStruct((n, DH), BF)]
    if emit_cache:
        out_specs += [tile(KVL), tile(ROPE)]
        out_shape += [jax.ShapeDtypeStruct((n, KVL), F32), jax.ShapeDtypeStruct((n, ROPE), F32)]
    return pl.pallas_call(
        functools.partial(_proj_kernel, rope=rope, emit_cache=emit_cache),
        grid=(n // TM,),
        in_specs=in_specs,
        out_specs=out_specs,
        out_shape=out_shape,
        compiler_params=_cparams(("arbitrary",)),
        name="projections",
    )(*args)


def _cache_kv_kernel(ckv_ref, krp_ref, wuk_ref, gk_ref, wuv_ref, k_ref, v_ref):
    _keys_values(ckv_ref[0], krp_ref[0], wuk_ref, wuv_ref, gk_ref[...], None, None,
                 k_ref.at[0], v_ref.at[0])


def _cache_keys_values(ckv, krp, wts):
    b, s, _ = ckv.shape
    full = lambda shape: pl.BlockSpec(shape, lambda i: (0,) * len(shape))
    return pl.pallas_call(
        _cache_kv_kernel,
        grid=(b,),
        in_specs=[pl.BlockSpec((1, s, KVL), lambda i: (i, 0, 0)),
                  pl.BlockSpec((1, s, HP), lambda i: (i, 0, 0)),
                  full((KVL, DH)), full((1, HP)), full((KVL, DH))],
        out_specs=[pl.BlockSpec((1, s, DH), lambda i: (i, 0, 0))] * 2,
        out_shape=[jax.ShapeDtypeStruct((b, s, DH), BF)] * 2,
        compiler_params=_cparams(("arbitrary",)),
        name="cache_keys_values",
    )(ckv, krp, wts["w_uk"], wts["gk"], wts["w_uv"])


def _sigmoid(x):
    return 0.5 * jnp.tanh(0.5 * x) + 0.5


def _tile_scan(a, b, forward):
    row = lax.broadcasted_iota(jnp.int32, a.shape, 0)
    for k in (1, 2, 4):
        if forward:
            shift, valid = k, row >= k
        else:
            shift, valid = SUBLANES - k, row < SUBLANES - k
        a_prev = jnp.where(valid, pltpu.roll(a, shift, 0), 1.0)
        b_prev = jnp.where(valid, pltpu.roll(b, shift, 0), 0.0)
        b = a * b_prev + b
        a = a * a_prev
    return a, b


def _rglru_kernel(*refs, t, has_h0, emit_state):
    it = iter(refs)
    xr_ref, gg_ref, cw_ref, cb_ref, bd_ref, ba_ref, bx_ref, lam_ref = (next(it) for _ in range(8))
    h0_ref = next(it) if has_h0 else None
    y_ref = next(it)
    hf_ref = next(it) if emit_state else None
    a_scr, b_scr, h_scr, xpad_scr = (next(it) for _ in range(4))

    n_tiles = t // SUBLANES
    pad = SUBLANES
    zero_rows = jnp.zeros((pad, CH), F32)
    xpad_scr[0:pad, :] = zero_rows
    xpad_scr[pad + t:pad + t + pad, :] = zero_rows
    xpad_scr[pad:pad + t, :] = xr_ref[0].astype(F32)
    xpad = xpad_scr[...]
    rows_all = t + 2 * pad
    xc = cb_ref[...] + xpad[pad:pad + t, :] * cw_ref[1:2, :]
    for tap, shift in ((0, 1), (2, rows_all - 1), (3, rows_all - 2)):
        xc = xc + pltpu.roll(xpad, shift, 0)[pad:pad + t, :] * cw_ref[tap:tap + 1, :]
    for s in range(CH // BDW):
        cols = slice(s * BDW, (s + 1) * BDW)
        xs = xc[:, cols]
        xsb = xs.astype(BF)
        xh = 0.5 * xs
        for d in range(2):
            tr = jnp.tanh(_dot(xsb, bd_ref[2 * d, s]) + ba_ref[d:d + 1, cols])
            ti = jnp.tanh(_dot(xsb, bd_ref[2 * d + 1, s]) + bx_ref[d:d + 1, cols])
            nl = -lam_ref[d:d + 1, cols]
            softplus = jnp.maximum(nl, 0.0) + jnp.log(1.0 + jnp.exp(-jnp.abs(nl)))
            ch = (-0.5 * LRU_C) * softplus
            a = jnp.exp(tr * ch + ch)
            z = 1.0 - a * a
            root = z * lax.rsqrt(jnp.maximum(z, TINY))
            a_scr[d, :, cols] = a
            b_scr[d, :, cols] = root * (ti * xh + xh)

    def step(i, carry):
        out = []
        for d in range(2):
            tile = i if d == 0 else n_tiles - 1 - i
            rows = pl.ds(pl.multiple_of(tile * SUBLANES, SUBLANES), SUBLANES)
            for c in range(NLT):
                lanes = slice(c * LANES, (c + 1) * LANES)
                decay, local = _tile_scan(a_scr[d, rows, lanes], b_scr[d, rows, lanes], d == 0)
                h = local + decay * carry[d * NLT + c]
                h_scr[d, rows, lanes] = h
                last = h[SUBLANES - 1:SUBLANES, :] if d == 0 else h[0:1, :]
                out.append(jnp.broadcast_to(last, (SUBLANES, LANES)))
        return tuple(out)

    init = []
    for d in range(2):
        for c in range(NLT):
            if has_h0:
                h0 = h0_ref[0, d:d + 1, c * LANES:(c + 1) * LANES]
                init.append(jnp.broadcast_to(h0, (SUBLANES, LANES)))
            else:
                init.append(jnp.zeros((SUBLANES, LANES), F32))
    final = lax.fori_loop(0, n_tiles, step, tuple(init), unroll=2)

    if emit_state:
        for d in range(2):
            for c in range(NLT):
                hf_ref[0, d:d + 1, c * LANES:(c + 1) * LANES] = final[d * NLT + c][0:1, :]
    y_ref[0] = ((h_scr[0] + h_scr[1]) * gg_ref[0].astype(F32)).astype(BF)


def _rglru(xr, gg, wts, h0, emit_state):
    b, t, _ = xr.shape
    nc = DR // CH
    has_h0 = h0 is not None
    chunk = lambda r: pl.BlockSpec((r, CH), lambda i, j: (0, j))
    seq = pl.BlockSpec((1, t, CH), lambda i, j: (i, 0, j))
    state = pl.BlockSpec((1, 2, CH), lambda i, j: (i, 0, j))
    in_specs = [seq, seq, chunk(4), chunk(1),
                pl.BlockSpec((4, CH // BDW, BDW, BDW), lambda i, j: (0, j, 0, 0)),
                chunk(2), chunk(2), chunk(2)]
    args = [xr, gg, wts["conv_w"], wts["conv_b"], wts["bd"], wts["lru_ba"], wts["lru_bx"], wts["lru_lam"]]
    if has_h0:
        in_specs.append(state)
        args.append(h0)
    out_specs = [seq]
    out_shape = [jax.ShapeDtypeStruct((b, t, DR), BF)]
    if emit_state:
        out_specs.append(state)
        out_shape.append(jax.ShapeDtypeStruct((b, 2, DR), F32))
    res = pl.pallas_call(
        functools.partial(_rglru_kernel, t=t, has_h0=has_h0, emit_state=emit_state),
        grid=(b, nc),
        in_specs=in_specs,
        out_specs=out_specs,
        out_shape=out_shape,
        scratch_shapes=[pltpu.VMEM((2, t, CH), F32)] * 3 + [pltpu.VMEM((t + 2 * SUBLANES, CH), F32)],
        compiler_params=_cparams(("arbitrary", "arbitrary")),
        name="rglru",
    )(*args)
    return res if emit_state else (res[0], None)


def _attn_kernel(*refs, t, n_heads, has_ctx, q_block):
    it = iter(refs)
    q_ref, k_ref, v_ref = next(it), next(it), next(it)
    kc_ref = vc_ref = None
    if has_ctx:
        kc_ref, vc_ref = next(it), next(it)
    o_ref = next(it)
    log2_scale = (QK ** -0.5) * math.log2(math.e)
    for hd in range(n_heads):
        cols = slice(hd * HP, (hd + 1) * HP)
        k = k_ref[0, :, cols]
        v = v_ref[0, :, cols]
        if has_ctx:
            kc = kc_ref[0, :, cols]
            vc = vc_ref[0, :, cols]
        for qb in range(t // q_block):
            rows = slice(qb * q_block, (qb + 1) * q_block)
            q = q_ref[0, rows, cols]
            s = _dot_nt(q, k) * log2_scale
            m = jnp.max(s, axis=-1, keepdims=True)
            if has_ctx:
                sc = _dot_nt(q, kc) * log2_scale
                m = jnp.maximum(m, jnp.max(sc, axis=-1, keepdims=True))
            p = jnp.exp2(s - m)
            den = jnp.sum(p, axis=-1, keepdims=True)
            o = _dot(p.astype(BF), v)
            if has_ctx:
                pc = jnp.exp2(sc - m)
                den = den + jnp.sum(pc, axis=-1, keepdims=True)
                o = o + _dot(pc.astype(BF), vc)
            o_ref[0, rows, cols] = (o / den).astype(BF)


def _attention(q, k, v, kc, vc, heads_per_step):
    b, t, _ = q.shape
    has_ctx = kc is not None
    w = heads_per_step * HP
    blk = lambda n: pl.BlockSpec((1, n, w), lambda i, j: (i, 0, j))
    in_specs = [blk(t), blk(t), blk(t)]
    args = [q, k, v]
    if has_ctx:
        in_specs += [blk(kc.shape[1])] * 2
        args += [kc, vc]
    return pl.pallas_call(
        functools.partial(_attn_kernel, t=t, n_heads=heads_per_step, has_ctx=has_ctx, q_block=min(t, 256)),
        grid=(b, NH // heads_per_step),
        in_specs=in_specs,
        out_specs=blk(t),
        out_shape=jax.ShapeDtypeStruct((b, t, DH), BF),
        compiler_params=_cparams(("arbitrary", "arbitrary")),
        name="attention",
    )(*args)


def _route(logits):
    lane = lax.broadcasted_iota(jnp.int32, logits.shape, 1)
    lanef = lane.astype(F32)
    neg = -jnp.inf
    big = float(LANES)
    gl = jnp.where((lane >= NE) & (lane < NE + NG), logits, neg)
    gmax = jnp.max(gl, axis=-1, keepdims=True)
    gidx = jnp.min(jnp.where(gl == gmax, lanef, big), axis=-1, keepdims=True) - float(NE)
    gw = 1.0 / jnp.sum(jnp.exp(gl - gmax), axis=-1, keepdims=True)
    lo = gidx * float(EPG)
    el = jnp.where((lanef >= lo) & (lanef < lo + float(EPG)), logits, neg)
    v1 = jnp.max(el, axis=-1, keepdims=True)
    i1 = jnp.min(jnp.where(el == v1, lanef, big), axis=-1, keepdims=True)
    el2 = jnp.where(lanef == i1, neg, el)
    v2 = jnp.max(el2, axis=-1, keepdims=True)
    i2 = jnp.min(jnp.where(el2 == v2, lanef, big), axis=-1, keepdims=True)
    e2 = jnp.exp(v2 - v1)
    w1 = gw / (1.0 + e2)
    w2 = gw * e2 / (1.0 + e2)
    cmb = jnp.where(lanef == i1, w1, 0.0) + jnp.where(lanef == i2, w2, 0.0)
    return cmb, gidx


def _out_kernel(x_ref, h_ref, yr_ref, ya_ref, mod_ref, wgate_ref, wor_ref, wom_ref, wout_ref,
                n2_ref, rw_ref, rb_ref, x1_ref, hx_ref, rinfo_ref, tcnt_ref):
    x = x_ref[...]
    gl = _dot_nt(h_ref[...], wgate_ref[...])
    merged2 = ((jnp.tanh(gl[:, :D]) + 1.0) * _dot(yr_ref[...], wor_ref[...])
               + (jnp.tanh(gl[:, D:]) + 1.0) * _dot(ya_ref[...], wom_ref[...]))
    mix = _dot(merged2.astype(BF), wout_ref[...])
    x1 = x + mod_ref[0, 2:3, :] * mix
    x1_ref[...] = x1
    h2 = _modulated_norm(x1, n2_ref[...], mod_ref[0, 4:5, :], mod_ref[0, 3:4, :])
    h2_hi = h2.astype(BF)
    h2_lo = (h2 - h2_hi.astype(F32)).astype(BF)
    part = _dot(h2_hi, rw_ref[...])
    logits = part[:, :LANES] + part[:, LANES:] + _dot(h2_lo, rw_ref[:, :LANES]) + rb_ref[...]
    cmb, gidx = _route(logits)

    lanef = lax.broadcasted_iota(jnp.int32, cmb.shape, 1).astype(F32)
    ghot = jnp.where(lanef == gidx, 1.0, 0.0)
    r_i = lax.broadcasted_iota(jnp.int32, (TM, TM), 0)
    c_i = lax.broadcasted_iota(jnp.int32, (TM, TM), 1)
    tri = jnp.where(r_i > c_i, 1.0, 0.0).astype(BF)
    earlier_same = jnp.sum(_dot(tri, ghot.astype(BF)) * ghot, axis=-1, keepdims=True)
    counts = jnp.sum(ghot, axis=0, keepdims=True)
    padded = jnp.floor((counts + (RUN_ALIGN - 1.0)) * (1.0 / RUN_ALIGN)) * RUN_ALIGN
    lower_groups = jnp.sum(jnp.where(lanef < gidx, padded, 0.0), axis=-1, keepdims=True)
    lpos = lower_groups + earlier_same
    s_i = lax.broadcasted_iota(jnp.int32, (TM, TS), 1)
    to_sorted = jnp.where(s_i.astype(F32) == lpos, 1.0, 0.0).astype(BF)
    c1 = cmb.astype(BF)
    c2 = (cmb - c1.astype(F32)).astype(BF)
    c3 = (cmb - c1.astype(F32) - c2.astype(F32)).astype(BF)
    payload = jnp.concatenate([h2.astype(BF), c1, c2, c3], axis=1)
    srt = lax.dot_general(to_sorted, payload, (((0,), (0,)), ((), ())), preferred_element_type=F32)
    hx_ref[...] = srt.astype(BF)

    rinfo_ref[...] = jnp.where(lanef == 0.0, gidx, jnp.where(lanef == 1.0, lpos, 0.0))
    tcnt_ref[0] = jnp.broadcast_to(counts, (SUBLANES, LANES))


def _merge_out(x, h, yr, ya, mod, mod_row, wts):
    n = x.shape[0]
    n_tiles = n // TM
    tile = lambda w: pl.BlockSpec((TM, w), lambda i: (i, 0))
    return pl.pallas_call(
        _out_kernel,
        grid=(n_tiles,),
        in_specs=[tile(D), tile(D), tile(DR), tile(DH),
                  pl.BlockSpec((1, 6, D), lambda i: (mod_row(i), 0, 0)),
                  _resident((2 * D, D)), _resident((DR, D)), _resident((DH, D)), _resident((D, D)),
                  _resident((1, D)), _resident((D, 2 * LANES)), _resident((1, LANES))],
        out_specs=[tile(D), pl.BlockSpec((TS, XW), lambda i: (i, 0)), tile(LANES),
                   pl.BlockSpec((1, SUBLANES, LANES), lambda i: (i, 0, 0))],
        out_shape=[jax.ShapeDtypeStruct((n, D), F32), jax.ShapeDtypeStruct((n_tiles * TS, XW), BF),
                   jax.ShapeDtypeStruct((n, LANES), F32),
                   jax.ShapeDtypeStruct((n_tiles, SUBLANES, LANES), F32)],
        compiler_params=_cparams(("arbitrary",)),
        name="merge_out",
    )(x, h, yr, ya, mod, wts["w_gate"], wts["w_o_rnn"], wts["w_o_mla"], wts["w_out"],
      wts["n2"], wts["router_w"], wts["router_b"])


def _run_copies(local_ref, far_ref, len_ref, tile, make_copy, action):
    for g in range(NG):
        idx = tile * NG + g
        n = len_ref[idx]
        local0 = local_ref[idx]
        far0 = far_ref[idx]
        for k in reversed(range(RUN_ALIGN.bit_length() - 1, TS.bit_length())):
            size = 1 << k
            done = (n >> (k + 1)) << (k + 1)

            @pl.when(((n >> k) & 1) == 1)
            def _():
                action(make_copy(pl.multiple_of(local0 + done, RUN_ALIGN),
                                 pl.multiple_of(far0 + done, RUN_ALIGN), size))


def _dispatch_kernel(local_ref, far_ref, len_ref, hx_ref, xs_in_ref, xs_ref, sem):
    del xs_in_ref

    def copy(src, dst, size):
        return pltpu.make_async_copy(hx_ref.at[pl.ds(src, size)], xs_ref.at[pl.ds(dst, size)], sem)

    tile = pl.program_id(0)
    _run_copies(local_ref, far_ref, len_ref, tile, copy, lambda c: c.start())
    _run_copies(local_ref, far_ref, len_ref, tile, copy, lambda c: c.wait())


def _dispatch(tables, hx, xs):
    n_tiles = hx.shape[0] // TS
    return pl.pallas_call(
        _dispatch_kernel,
        grid_spec=pltpu.PrefetchScalarGridSpec(
            num_scalar_prefetch=3,
            grid=(n_tiles,),
            in_specs=[pl.BlockSpec((TS, XW), lambda i, *_: (i, 0)),
                      pl.BlockSpec(memory_space=pl.ANY)],
            out_specs=pl.BlockSpec(memory_space=pl.ANY),
            scratch_shapes=[pltpu.SemaphoreType.DMA(())],
        ),
        out_shape=jax.ShapeDtypeStruct(xs.shape, xs.dtype),
        input_output_aliases={4: 0},
        compiler_params=_cparams(("arbitrary",)),
        name="dispatch",
    )(*tables, hx, xs)


def _moe_kernel(tb_ref, tg_ref, nt_ref, xs_ref, w1_ref, w3_ref, w2_ref, o_ref, acc_ref):
    slot = pl.program_id(2)
    j = pl.program_id(0) * TILE_GANG + slot
    pair = pl.program_id(1)

    @pl.when(j < nt_ref[0])
    def _():
        xt = xs_ref[:, :D]
        cmb = (xs_ref[:, D:D + LANES].astype(F32) + xs_ref[:, D + LANES:D + 2 * LANES].astype(F32)
               + xs_ref[:, D + 2 * LANES:].astype(F32))
        lane = lax.broadcasted_iota(jnp.int32, cmb.shape, 1)
        first = tg_ref[j] * EPG + pair * EXPERTS_PER_STEP
        hidden = []
        for u in range(EXPERTS_PER_STEP):
            a = _dot(xt, w1_ref[u].astype(BF))
            he = (a * _sigmoid(a)) * _dot(xt, w3_ref[u].astype(BF))
            ce = jnp.sum(jnp.where(lane == first + u, cmb, 0.0), axis=-1, keepdims=True)
            hidden.append((he * ce).astype(BF))
        w2 = w2_ref[...].reshape(EXPERTS_PER_STEP * DE, D).astype(BF)
        y = _dot(jnp.concatenate(hidden, axis=1), w2)

        @pl.when(pair == 0)
        def _():
            acc_ref[slot] = y

        @pl.when(pair == EPG // EXPERTS_PER_STEP - 1)
        def _():
            o_ref[...] = (acc_ref[slot] + y).astype(BF)

    @pl.when((j >= nt_ref[0]) & (pair == EPG // EXPERTS_PER_STEP - 1))
    def _():
        o_ref[...] = jnp.zeros_like(o_ref)


def _experts(tile_block, tile_group, n_tiles, xs, wts):
    m = xs.shape[0]
    steps = EPG // EXPERTS_PER_STEP
    assert steps == 2, "the kernel keeps one partial sum: first step stores it, second adds and writes"

    def w_idx(gang, e, slot, tb, tg, nt):
        j = gang * TILE_GANG + slot
        return (tg[j] * steps + jnp.where(j < nt[0], e, steps - 1), 0, 0)

    def o_idx(gang, e, slot, tb, tg, nt):
        return (jnp.where(e == steps - 1, gang * TILE_GANG + slot, jnp.maximum(gang * TILE_GANG - 1, 0)), 0)

    return pl.pallas_call(
        _moe_kernel,
        grid_spec=pltpu.PrefetchScalarGridSpec(
            num_scalar_prefetch=3,
            grid=(m // (TMOE * TILE_GANG), steps, TILE_GANG),
            in_specs=[pl.BlockSpec((TMOE, XW), lambda gang, e, slot, tb, tg, nt: (tb[gang * TILE_GANG + slot], 0)),
                      pl.BlockSpec((EXPERTS_PER_STEP, D, DE), w_idx),
                      pl.BlockSpec((EXPERTS_PER_STEP, D, DE), w_idx),
                      pl.BlockSpec((EXPERTS_PER_STEP, DE, D), w_idx)],
            out_specs=pl.BlockSpec((TMOE, D), o_idx),
            scratch_shapes=[pltpu.VMEM((TILE_GANG, TMOE, D), F32)],
        ),
        out_shape=jax.ShapeDtypeStruct((m, D), BF),
        compiler_params=_cparams(("arbitrary", "arbitrary", "arbitrary")),
        name="experts",
    )(tile_block, tile_group, n_tiles, xs, wts["exp_w1"], wts["exp_w3"], wts["exp_w2"])


def _combine_kernel(local_ref, far_ref, len_ref, x1_ref, rinfo_ref, mod_ref, ys_ref, o_ref,
                    buf_ref, sem, *, n_steps):
    i = pl.program_id(0)

    def runs(step, slot, action):
        def copy(dst, src, size):
            return pltpu.make_async_copy(ys_ref.at[pl.ds(src, size)],
                                         buf_ref.at[slot, pl.ds(dst, size)], sem.at[slot])

        _run_copies(local_ref, far_ref, len_ref, step, copy, action)

    @pl.when(i == 0)
    def _():
        buf_ref[...] = jnp.zeros_like(buf_ref)
        runs(0, 0, lambda c: c.start())

    @pl.when(i + 1 < n_steps)
    def _():
        runs(i + 1, (i + 1) % 2, lambda c: c.start())

    slot = i % 2
    runs(i, slot, lambda c: c.wait())
    lpos = rinfo_ref[:, 1:2]
    s_i = lax.broadcasted_iota(jnp.int32, (TM, TS), 1)
    from_sorted = jnp.where(s_i.astype(F32) == lpos, 1.0, 0.0).astype(BF)
    moe = _dot(from_sorted, buf_ref[slot])
    o_ref[...] = x1_ref[...] + mod_ref[0, 5:6, :] * moe


def _combine(tables, x1, rinfo, mod, mod_row, ys):
    n = x1.shape[0]
    n_steps = n // TM
    return pl.pallas_call(
        functools.partial(_combine_kernel, n_steps=n_steps),
        grid_spec=pltpu.PrefetchScalarGridSpec(
            num_scalar_prefetch=3,
            grid=(n_steps,),
            in_specs=[pl.BlockSpec((TM, D), lambda i, *_: (i, 0)),
                      pl.BlockSpec((TM, LANES), lambda i, *_: (i, 0)),
                      pl.BlockSpec((1, 6, D), lambda i, *_: (mod_row(i), 0, 0)),
                      pl.BlockSpec(memory_space=pl.ANY)],
            out_specs=pl.BlockSpec((TM, D), lambda i, *_: (i, 0)),
            scratch_shapes=[pltpu.VMEM((2, TS, D), BF), pltpu.SemaphoreType.DMA((2,))],
        ),
        out_shape=jax.ShapeDtypeStruct((n, D), F32),
        compiler_params=_cparams(("arbitrary",)),
        name="combine",
    )(*tables, x1, rinfo, mod, ys)


def _run_lengths(tile_counts):
    counts = tile_counts[:, 0, :NG].astype(jnp.int32)
    return ((counts + RUN_ALIGN - 1) // RUN_ALIGN) * RUN_ALIGN


def _run_tables(lengths, first_far):
    local = jnp.cumsum(lengths, axis=1) - lengths
    far = first_far[None, :] + jnp.cumsum(lengths, axis=0) - lengths
    flat = lambda a: a.astype(jnp.int32).reshape(-1)
    return flat(local), flat(far), flat(lengths)


def _group_layout(counts, max_tiles):
    padded = ((counts + TMOE - 1) // TMOE) * TMOE
    ends = jnp.cumsum(padded)
    offsets = ends - padded
    n_tiles = (ends[-1] // TMOE).astype(jnp.int32)
    tile = jnp.minimum(jnp.arange(max_tiles, dtype=jnp.int32), jnp.maximum(n_tiles - 1, 0))
    tile_group = jnp.sum((tile[:, None] * TMOE >= ends[None, :]).astype(jnp.int32), axis=1)
    return offsets, tile, tile_group, n_tiles.reshape(1)


def _pad_heads(w, perm=None, rotary_only=False):
    lead = w.shape[:-1]
    per = w.shape[-1] // NH
    w = w.reshape(lead + (NH, per))
    if perm is not None:
        nope = jnp.zeros_like(w[..., :NOPE]) if rotary_only else w[..., :NOPE]
        w = jnp.concatenate([nope, w[..., NOPE:][..., perm]], axis=-1)
    w = jnp.pad(w, [(0, 0)] * len(lead) + [(0, 0), (0, HP - per)])
    return w.reshape(lead + (NH * HP,))


def _pad_gain(g, perm, rotary_only=False):
    nope = jnp.zeros((NOPE,), F32) if rotary_only else g[:NOPE]
    g = jnp.concatenate([nope, g[NOPE:][perm], jnp.zeros((HP - QK,), F32)])
    return g.reshape(1, HP)


def _block_diag(w):
    per = BDW // LRU_BLOCK
    w = w.reshape(DR // BDW, per, LRU_BLOCK, LRU_BLOCK)
    bd = jnp.einsum("jpab,pq->jpaqb", w, jnp.eye(per, dtype=w.dtype))
    return bd.reshape(DR // BDW, BDW, BDW)


def _prepare_shared(l, p):
    w_in = lambda a, b: jnp.transpose(lax.slice(p["w_in"], (l, 0, a), (l + 1, D, b)).reshape(D, b - a))
    o1, o2, o3, o4, o5 = DR, 2 * DR, 2 * DR + QL, 2 * DR + QL + KVL, 2 * DR + QL + KVL + ROPE
    bd = jnp.stack([_block_diag(p["lru_wa"][l, 0]), _block_diag(p["lru_wx"][l, 0]),
                    _block_diag(p["lru_wa"][l, 1]), _block_diag(p["lru_wx"][l, 1])])
    bd = (0.5 * bd).astype(BF)
    wom = p["w_o_mla"][l].reshape(NH, VD, D)
    wom = jnp.pad(wom, ((0, 0), (0, HP - VD), (0, 0))).reshape(DH, D)
    router_w = jnp.concatenate([p["router_we"][l], p["router_wg"][l],
                                jnp.zeros((D, LANES - NE - NG), F32)], axis=1)
    router_b = jnp.concatenate([p["router_be"][l], p["router_bg"][l],
                                jnp.zeros((LANES - NE - NG,), F32)]).reshape(1, LANES)
    router_hi = router_w.astype(BF)
    router_lo = (router_w - router_hi.astype(F32)).astype(BF)
    router_w = jnp.concatenate([router_hi, router_lo], axis=1)
    return {
        "n1": p["norm1_g"][l].reshape(1, D), "n2": p["norm2_g"][l].reshape(1, D),
        "w_x": w_in(0, o1).astype(BF), "w_g": w_in(o1, o2).astype(BF),
        "w_q": w_in(o2, o3).astype(BF), "w_kv": w_in(o3, o4), "w_kr": w_in(o4, o5),
        "w_gate": (0.5 * w_in(o5, o5 + 2 * D)).astype(BF),
        "qan": p["q_a_norm"][l].reshape(1, QL), "kvan": p["kv_a_norm"][l].reshape(1, KVL),
        "w_uk": _pad_heads(p["w_uk"][l]).astype(BF),
        "w_uv": _pad_heads(p["w_uv"][l]).astype(BF),
        "conv_w": p["conv_w"][l], "conv_b": p["conv_b"][l].reshape(1, DR), "bd": bd,
        "lru_ba": 0.5 * p["lru_ba"][l], "lru_bx": 0.5 * p["lru_bx"][l], "lru_lam": p["lru_lam"][l],
        "w_o_rnn": p["w_o_rnn"][l].astype(BF), "w_o_mla": wom.astype(BF), "w_out": (0.5 * p["w_out"][l]).astype(BF),
        "router_w": router_w, "router_b": router_b,
        "exp_w1": p["exp_w1"][l], "exp_w3": p["exp_w3"][l], "exp_w2": p["exp_w2"][l],
    }


def _with_rope_order(l, p, shared, perm, rotary):
    w = dict(shared)
    zeros = lambda n: jnp.zeros((n, D), F32)
    rope_block = lambda order: [zeros(NOPE), shared["w_kr"][order, :], zeros(HP - QK)]
    kvr = [shared["w_kv"]] + rope_block(perm)
    w["w_uq"] = _pad_heads(p["w_uq"][l], perm).astype(BF)
    w["gq"] = _pad_gain(p["q_norm"][l], perm)
    w["gk"] = _pad_gain(p["k_norm"][l], perm)
    if rotary:
        pair = np.concatenate([perm[ROPE // 2:], perm[:ROPE // 2]])
        kvr += rope_block(pair)
        w["w_uq_pair"] = _pad_heads(p["w_uq"][l], pair, rotary_only=True).astype(BF)
        w["gq_pair"] = _pad_gain(p["q_norm"][l], pair, rotary_only=True)
        w["gk_pair"] = _pad_gain(p["k_norm"][l], pair, rotary_only=True)
    w["w_kvr"] = jnp.concatenate(kvr, axis=0).astype(BF)
    return w


def _rope_tables(n_tokens, perm):
    rows = n_tokens // GRID_W
    row = np.repeat(np.arange(rows), GRID_W).astype(np.float32)
    col = np.tile(np.arange(GRID_W), rows).astype(np.float32)
    axis_dim = ROPE // 2
    inv = (np.float32(ROPE_BASE) ** (-np.arange(0, axis_dim, 2, dtype=np.float32) / axis_dim)).astype(np.float32)
    ang = np.concatenate([row[:, None] * inv, col[:, None] * inv], axis=-1).astype(np.float32)
    cos, sin = np.cos(ang), np.sin(ang)
    ones = lambda n: np.ones((n_tokens, n), np.float32)
    zeros = lambda n: np.zeros((n_tokens, n), np.float32)
    cos_t = np.concatenate([ones(NOPE), cos, cos, ones(HP - QK)], axis=1)
    sin_t = np.concatenate([zeros(NOPE), -sin, sin, zeros(HP - QK)], axis=1)
    return jnp.asarray(cos_t, F32), jnp.asarray(sin_t, F32)


def kernel(x_prompt, x_sample, cache_mla_ckv, cache_mla_krope, state_rglru, c, c_ctx, norm1_g, norm2_g, w_mod, b_mod, w_in, conv_w, conv_b, lru_wa, lru_ba, lru_wx, lru_bx, lru_lam, q_a_norm, kv_a_norm, w_uq, w_uk, w_uv, q_norm, k_norm, w_o_rnn, w_o_mla, w_out, router_wg, router_bg, router_we, router_be, exp_w1, exp_w3, exp_w2):
    p = dict(norm1_g=norm1_g, norm2_g=norm2_g, w_in=w_in, conv_w=conv_w, conv_b=conv_b,
             lru_wa=lru_wa, lru_ba=lru_ba, lru_wx=lru_wx, lru_bx=lru_bx, lru_lam=lru_lam,
             q_a_norm=q_a_norm, kv_a_norm=kv_a_norm, w_uq=w_uq, w_uk=w_uk, w_uv=w_uv,
             q_norm=q_norm, k_norm=k_norm, w_o_rnn=w_o_rnn, w_o_mla=w_o_mla, w_out=w_out,
             router_wg=router_wg, router_bg=router_bg, router_we=router_we, router_be=router_be,
             exp_w1=exp_w1, exp_w3=exp_w3, exp_w2=exp_w2)
    depth = w_in.shape[0]
    nb, seq, _ = x_prompt.shape
    db, dseq, _ = x_sample.shape
    ident = np.arange(ROPE)
    halves = np.concatenate([np.arange(0, ROPE, 2), np.arange(1, ROPE, 2)])
    rope_tabs = _rope_tables(dseq, halves)
    cond8 = jnp.concatenate([c_ctx[None, :], c, jnp.zeros((SUBLANES - 1 - db, D), F32)], axis=0)
    ctx_row = lambda tile_rows: (lambda i: 0)
    lat_row = lambda tile_rows: (lambda i: (i * tile_rows) // dseq + 1)
    n_ctx, n_lat = nb * seq, db * dseq
    run_padding = ((n_ctx + n_lat) // TM) * NG * (RUN_ALIGN - 1)
    max_tiles = -(-(n_ctx + n_lat + run_padding) // TMOE) + NG
    max_tiles = -(-max_tiles // TILE_GANG) * TILE_GANG
    per_seq = lambda arrs, b, t: [a.reshape(b, t, a.shape[-1]) for a in arrs]
    flat = lambda a: a.reshape(-1, a.shape[-1])

    y_prompt, y_sample = x_prompt.reshape(n_ctx, D), x_sample.reshape(n_lat, D)
    ckv_list, krope_list, rnn_list = [], [], []
    for l in range(depth):
        shared = _prepare_shared(l, p)
        w_ctx = _with_rope_order(l, p, shared, ident, False)
        w_lat = _with_rope_order(l, p, shared, halves, True)
        mod = _modulation(cond8, w_mod[l], b_mod[l]).reshape(SUBLANES, 6, D)

        h, xr, gg, q, k, v, ckv, kro = _projections(y_prompt, mod, ctx_row(TM), w_ctx, None, True)
        xr, gg, q, k, v = per_seq([xr, gg, q, k, v], nb, seq)
        yr, h_fin = _rglru(xr, gg, shared, None, True)
        ya = _attention(q, k, v, None, None, NH)
        x1_c, hx_c, ri_c, tc_c = _merge_out(y_prompt, h, flat(yr), flat(ya), mod, ctx_row(TM), shared)
        ckv_list.append(ckv.reshape(nb, seq, KVL))
        krope_list.append(kro.reshape(nb, seq, ROPE))
        rnn_list.append(h_fin)

        krp_cache = jnp.pad(cache_mla_krope[:, l][..., halves], ((0, 0), (0, 0), (NOPE, HP - QK)))
        kc, vc = _cache_keys_values(cache_mla_ckv[:, l], krp_cache, w_lat)
        h, xr, gg, q, k, v = _projections(y_sample, mod, lat_row(TM), w_lat, rope_tabs, False)
        xr, gg, q, k, v = per_seq([xr, gg, q, k, v], db, dseq)
        yr, _ = _rglru(xr, gg, shared, state_rglru[:, l], False)
        ya = _attention(q, k, v, kc, vc, 2)
        x1_l, hx_l, ri_l, tc_l = _merge_out(y_sample, h, flat(yr), flat(ya), mod, lat_row(TM), shared)

        len_c, len_l = _run_lengths(tc_c), _run_lengths(tc_l)
        rows_c = jnp.sum(len_c, axis=0)
        offsets, tile_block, tile_group, n_tiles = _group_layout(rows_c + jnp.sum(len_l, axis=0), max_tiles)
        runs_c = _run_tables(len_c, offsets)
        runs_l = _run_tables(len_l, offsets + rows_c)
        xs = jnp.zeros((max_tiles * TMOE, XW), BF)
        xs = _dispatch(runs_c, hx_c, xs)
        xs = _dispatch(runs_l, hx_l, xs)
        ys = _experts(tile_block, tile_group, n_tiles, xs, shared)
        y_prompt = _combine(runs_c, x1_c, ri_c, mod, ctx_row(TM), ys)
        y_sample = _combine(runs_l, x1_l, ri_l, mod, lat_row(TM), ys)

    y_prompt, y_sample = y_prompt.reshape(nb, seq, D), y_sample.reshape(db, dseq, D)

    return (y_prompt, y_sample, jnp.stack(ckv_list, axis=1), jnp.stack(krope_list, axis=1),
            jnp.stack(rnn_list, axis=1))
```

```python
import functools
import math

import numpy as np
import jax
import jax.numpy as jnp
from jax import lax
from jax.experimental import pallas as pl
from jax.experimental.pallas import tpu as pltpu

D = 1024
DR = 1024
QL = 384
KVL = 256
NH = 8
NOPE = 64
ROPE = 32
QK = NOPE + ROPE
VD = 64
HP = 128
DH = NH * HP
GRID_W = 64
ROPE_BASE = 10000.0
EPS = 1e-6
TINY = 1e-30
LRU_C = 8.0
LRU_BLOCK = 64
BDW = 256
CH = 512
NLT = CH // 128
NG = 4
EPG = 4
NE = NG * EPG
DE = 512
LANES = 128
SUBLANES = 8
TM = 512
TS = 640
TMOE = 512
EXPERTS_PER_STEP = 2
XW = D + 3 * 128
RUN_ALIGN = 16
VMEM_LIMIT = 52 * 1024 * 1024
BF = jnp.bfloat16
F32 = jnp.float32


def _cparams(sem):
    return pltpu.CompilerParams(dimension_semantics=sem, vmem_limit_bytes=VMEM_LIMIT)


def _dot(a, b):
    return jnp.dot(a, b, preferred_element_type=F32)


def _dot_nt(a, b):
    return lax.dot_general(a, b, (((1,), (1,)), ((), ())), preferred_element_type=F32)


def _rms(x, g, width):
    ms = jnp.sum(x * x, axis=-1, keepdims=True) * (1.0 / width)
    return x * lax.rsqrt(ms + EPS) * g


def _modulated_norm(x, g, scale, shift):
    return _rms(x, g * (1.0 + scale), D) + shift


def _mod_kernel(c_ref, w_ref, b_ref, o_ref):
    c = c_ref[...]
    s = c * jax.nn.sigmoid(c)
    o_ref[...] = _dot(s, w_ref[...]) + b_ref[...]


def _modulation(cond8, w_mod, b_mod):
    n = w_mod.shape[1]
    return pl.pallas_call(
        _mod_kernel,
        grid=(n // D,),
        in_specs=[
            pl.BlockSpec((SUBLANES, D), lambda j: (0, 0)),
            pl.BlockSpec((D, D), lambda j: (0, j)),
            pl.BlockSpec((1, D), lambda j: (0, j)),
        ],
        out_specs=pl.BlockSpec((SUBLANES, D), lambda j: (0, j)),
        out_shape=jax.ShapeDtypeStruct((SUBLANES, n), F32),
        compiler_params=_cparams(("arbitrary",)),
        name="modulation",
    )(cond8, w_mod, b_mod.reshape(1, n))


def _head_norm(xh, gain, cos=None, partner_scaled=None):
    ms = jnp.sum(xh * xh, axis=-1, keepdims=True) * (1.0 / QK)
    rs = lax.rsqrt(ms + EPS)
    y = xh * rs * gain
    if cos is None:
        return y
    return y * cos + partner_scaled * rs


def _keys_values(ckv, krp, wuk_ref, wuv_ref, gk, cos, partner_scaled, k_ref, v_ref):
    cb = ckv.astype(BF)
    kn = _dot(cb, wuk_ref[...])
    v_ref[...] = _dot(cb, wuv_ref[...]).astype(BF)
    for h in range(NH):
        kh = kn[:, h * HP:(h + 1) * HP] + krp
        k_ref[:, h * HP:(h + 1) * HP] = _head_norm(kh, gk, cos, partner_scaled).astype(BF)


def _proj_kernel(*refs, rope, emit_cache):
    it = iter(refs)
    x_ref, mod_ref, n1_ref = next(it), next(it), next(it)
    wx_ref, wg_ref, wq_ref, wkvr_ref = next(it), next(it), next(it), next(it)
    qan_ref, kvan_ref, wuq_ref, gq_ref = next(it), next(it), next(it), next(it)
    wuk_ref, gk_ref, wuv_ref = next(it), next(it), next(it)
    if rope:
        wuqs_ref, gqs_ref, gks_ref, cos_ref, sins_ref = (next(it) for _ in range(5))
    h_ref, xr_ref, gg_ref, q_ref, k_ref, v_ref = (next(it) for _ in range(6))
    if emit_cache:
        ckv_ref, kro_ref = next(it), next(it)

    hb = _modulated_norm(x_ref[...], n1_ref[...], mod_ref[0, 1:2, :], mod_ref[0, 0:1, :]).astype(BF)
    h_ref[...] = hb
    xr_ref[...] = _dot_nt(hb, wx_ref[...]).astype(BF)
    gg_ref[...] = jax.nn.gelu(_dot_nt(hb, wg_ref[...])).astype(BF)

    qnb = _rms(_dot_nt(hb, wq_ref[...]), qan_ref[...], QL).astype(BF)
    q = _dot(qnb, wuq_ref[...])
    gq = gq_ref[...]
    cos = q_partner = q_pair_scale = None
    if rope:
        cos, sins = cos_ref[...], sins_ref[...]
        q_partner = _dot(qnb, wuqs_ref[...])
        q_pair_scale = gqs_ref[...] * sins
    for hd in range(NH):
        cols = slice(hd * HP, (hd + 1) * HP)
        partner = q_partner[:, cols] * q_pair_scale if rope else None
        q_ref[:, cols] = _head_norm(q[:, cols], gq, cos, partner).astype(BF)

    kvr = _dot_nt(hb, wkvr_ref[...])
    ckv = _rms(kvr[:, :KVL], kvan_ref[...], KVL)
    krp = kvr[:, KVL:KVL + HP]
    k_partner = kvr[:, KVL + HP:KVL + 2 * HP] * (gks_ref[...] * sins) if rope else None
    if emit_cache:
        ckv_ref[...] = ckv
        kro_ref[...] = krp[:, NOPE:NOPE + ROPE]
    _keys_values(ckv, krp, wuk_ref, wuv_ref, gk_ref[...], cos, k_partner, k_ref, v_ref)


def _resident(shape):
    return pl.BlockSpec(shape, lambda i: (0,) * len(shape), pipeline_mode=pl.Buffered(1))


def _projections(x, mod, mod_row, wts, rope_tabs, emit_cache):
    n = x.shape[0]
    rope = rope_tabs is not None
    tile = lambda w: pl.BlockSpec((TM, w), lambda i: (i, 0))
    in_specs = [
        tile(D),
        pl.BlockSpec((1, 6, D), lambda i: (mod_row(i), 0, 0)),
        _resident((1, D)),
        _resident((DR, D)), _resident((DR, D)), _resident((QL, D)), _resident(wts["w_kvr"].shape),
        _resident((1, QL)), _resident((1, KVL)), _resident((QL, DH)), _resident((1, HP)),
        _resident((KVL, DH)), _resident((1, HP)), _resident((KVL, DH)),
    ]
    args = [x, mod, wts["n1"], wts["w_x"], wts["w_g"], wts["w_q"], wts["w_kvr"],
            wts["qan"], wts["kvan"], wts["w_uq"], wts["gq"], wts["w_uk"], wts["gk"], wts["w_uv"]]
    if rope:
        tiles_per_seq = rope_tabs[0].shape[0] // TM
        in_specs += [_resident((QL, DH)), _resident((1, HP)), _resident((1, HP))]
        in_specs += [pl.BlockSpec((TM, HP), lambda i: (i % tiles_per_seq, 0))] * 2
        args += [wts["w_uq_pair"], wts["gq_pair"], wts["gk_pair"]] + list(rope_tabs)
    out_specs = [tile(D), tile(DR), tile(DR), tile(DH), tile(DH), tile(DH)]
    out_shape = [jax.ShapeDtypeStruct((n, D), BF), jax.ShapeDtypeStruct((n, DR), BF),
                 jax.ShapeDtypeStruct((n, DR), BF), jax.ShapeDtypeStruct((n, DH), BF),
                 jax.ShapeDtypeStruct((n, DH), BF), jax.ShapeDtypeStruct((n, DH), BF)]
    if emit_cache:
        out_specs += [tile(KVL), tile(ROPE)]
        out_shape += [jax.ShapeDtypeStruct((n, KVL), F32), jax.ShapeDtypeStruct((n, ROPE), F32)]
    return pl.pallas_call(
        functools.partial(_proj_kernel, rope=rope, emit_cache=emit_cache),
        grid=(n // TM,),
        in_specs=in_specs,
        out_specs=out_specs,
        out_shape=out_shape,
        compiler_params=_cparams(("arbitrary",)),
        name="projections",
    )(*args)


def _cache_kv_kernel(ckv_ref, krp_ref, wuk_ref, gk_ref, wuv_ref, k_ref, v_ref):
    _keys_values(ckv_ref[0], krp_ref[0], wuk_ref, wuv_ref, gk_ref[...], None, None,
                 k_ref.at[0], v_ref.at[0])


def _cache_keys_values(ckv, krp, wts):
    b, s, _ = ckv.shape
    full = lambda shape: pl.BlockSpec(shape, lambda i: (0,) * len(shape))
    return pl.pallas_call(
        _cache_kv_kernel,
        grid=(b,),
        in_specs=[pl.BlockSpec((1, s, KVL), lambda i: (i, 0, 0)),
                  pl.BlockSpec((1, s, HP), lambda i: (i, 0, 0)),
                  full((KVL, DH)), full((1, HP)), full((KVL, DH))],
        out_specs=[pl.BlockSpec((1, s, DH), lambda i: (i, 0, 0))] * 2,
        out_shape=[jax.ShapeDtypeStruct((b, s, DH), BF)] * 2,
        compiler_params=_cparams(("arbitrary",)),
        name="cache_keys_values",
    )(ckv, krp, wts["w_uk"], wts["gk"], wts["w_uv"])


WIN = 256
SEG = WIN // 8


def _window_permutation():
    dst = np.arange(WIN)
    p = np.zeros((WIN, WIN), np.float32)
    p[dst, (dst % SUBLANES) * SEG + dst // SUBLANES] = 1.0
    return p


def _sigmoid(x):
    return 0.5 * jnp.tanh(0.5 * x) + 0.5


def _segment_pass(a_scr, b_scr, bases, inits, out_scr=None):
    def body(k, carry):
        new = []
        for d in range(2):
            h, p = carry[d]
            i = k if d == 0 else SEG - 1 - k
            rows = pl.ds(pl.multiple_of(bases[d] + i * SUBLANES, SUBLANES), SUBLANES)
            a = a_scr[d, rows, :]
            h = a * h + b_scr[d, rows, :]
            if out_scr is None:
                p = a * p
            else:
                out_scr[d, rows, :] = h
            new.append((h, p))
        return tuple(new)

    init = tuple((inits[d], jnp.ones_like(inits[d])) for d in range(2))
    return lax.fori_loop(0, SEG, body, init, unroll=4)


def _segment_entries(end, decay, carry_in, forward):
    order = range(SUBLANES) if forward else reversed(range(SUBLANES))
    rows = [None] * SUBLANES
    c = carry_in
    for s in order:
        rows[s] = c
        c = end[s:s + 1, :] + decay[s:s + 1, :] * c
    return jnp.concatenate(rows, axis=0), c


def _rglru_kernel(*refs, t, has_h0, emit_state):
    it = iter(refs)
    xr_ref, gg_ref, perm_ref, unperm_ref = (next(it) for _ in range(4))
    cw_ref, cb_ref, bd_ref, ba_ref, bx_ref, lam_ref = (next(it) for _ in range(6))
    h0_ref = next(it) if has_h0 else None
    y_ref = next(it)
    hf_ref = next(it) if emit_state else None
    a_scr, b_scr, h_scr = (next(it) for _ in range(3))

    n_win = t // WIN
    sub = lax.broadcasted_iota(jnp.int32, (SUBLANES, CH), 0)
    zero_row = jnp.zeros((1, CH), F32)
    edge = 2 * SUBLANES
    for w in range(n_win):
        lo, hi = w * WIN, (w + 1) * WIN
        xp = _dot(perm_ref[...], xr_ref[0, lo:hi, :])
        before = xr_ref[0, lo - edge:lo, :].astype(F32)[edge - 1:edge, :] if w > 0 else zero_row
        after = xr_ref[0, hi:hi + edge, :].astype(F32) if w < n_win - 1 else None
        after0 = after[0:1, :] if after is not None else zero_row
        after1 = after[1:2, :] if after is not None else zero_row
        tile_m1 = jnp.where(sub == 0, before, pltpu.roll(xp[WIN - SUBLANES:WIN, :], 1, 0))
        tile_p0 = jnp.where(sub == SUBLANES - 1, after0, pltpu.roll(xp[0:SUBLANES, :], SUBLANES - 1, 0))
        tile_p1 = jnp.where(sub == SUBLANES - 1, after1,
                            pltpu.roll(xp[SUBLANES:2 * SUBLANES, :], SUBLANES - 1, 0))
        xe = jnp.concatenate([tile_m1, xp, tile_p0, tile_p1], axis=0)
        xc = cb_ref[...]
        for tap in range(4):
            xc = xc + xe[tap * SUBLANES:tap * SUBLANES + WIN, :] * cw_ref[tap:tap + 1, :]
        for s in range(CH // BDW):
            cols = slice(s * BDW, (s + 1) * BDW)
            xs = xc[:, cols]
            xsb = xs.astype(BF)
            xh = 0.5 * xs
            for d in range(2):
                tr = jnp.tanh(_dot(xsb, bd_ref[2 * d, s]) + ba_ref[d:d + 1, cols])
                ti = jnp.tanh(_dot(xsb, bd_ref[2 * d + 1, s]) + bx_ref[d:d + 1, cols])
                nl = -lam_ref[d:d + 1, cols]
                softplus = jnp.maximum(nl, 0.0) + jnp.log(1.0 + jnp.exp(-jnp.abs(nl)))
                ch = (-0.5 * LRU_C) * softplus
                a = jnp.exp(tr * ch + ch)
                z = 1.0 - a * a
                root = z * lax.rsqrt(jnp.maximum(z, TINY))
                a_scr[d, lo:hi, cols] = a
                b_scr[d, lo:hi, cols] = root * (ti * xh + xh)

    if has_h0:
        carry = [h0_ref[0, 0:1, :], h0_ref[0, 1:2, :]]
    else:
        carry = [zero_row, zero_row]
    zeros = jnp.zeros((SUBLANES, CH), F32)
    for k in range(n_win):
        bases = (k * WIN, (n_win - 1 - k) * WIN)
        totals = _segment_pass(a_scr, b_scr, bases, (zeros, zeros))
        entries = []
        for d in range(2):
            entry, carry[d] = _segment_entries(totals[d][0], totals[d][1], carry[d], d == 0)
            entries.append(entry)
        _segment_pass(a_scr, b_scr, bases, entries, out_scr=h_scr)

    if emit_state:
        hf_ref[0, 0:1, :] = carry[0]
        hf_ref[0, 1:2, :] = carry[1]
    for w in range(n_win):
        lo, hi = w * WIN, (w + 1) * WIN
        gate = _dot(perm_ref[...], gg_ref[0, lo:hi, :])
        yp = ((h_scr[0, lo:hi, :] + h_scr[1, lo:hi, :]) * gate).astype(BF)
        y_ref[0, lo:hi, :] = _dot(unperm_ref[...], yp).astype(BF)


def _rglru(xr, gg, wts, h0, emit_state):
    b, t, _ = xr.shape
    nc = DR // CH
    has_h0 = h0 is not None
    chunk = lambda r: pl.BlockSpec((r, CH), lambda i, j: (0, j))
    seq = pl.BlockSpec((1, t, CH), lambda i, j: (i, 0, j))
    state = pl.BlockSpec((1, 2, CH), lambda i, j: (i, 0, j))
    window = pl.BlockSpec((WIN, WIN), lambda i, j: (0, 0))
    perm = _window_permutation()
    in_specs = [seq, seq, window, window, chunk(4), chunk(1),
                pl.BlockSpec((4, CH // BDW, BDW, BDW), lambda i, j: (0, j, 0, 0)),
                chunk(2), chunk(2), chunk(2)]
    args = [xr, gg, jnp.asarray(perm, BF), jnp.asarray(perm.T, BF),
            wts["conv_w"], wts["conv_b"], wts["bd"], wts["lru_ba"], wts["lru_bx"], wts["lru_lam"]]
    if has_h0:
        in_specs.append(state)
        args.append(h0)
    out_specs = [seq]
    out_shape = [jax.ShapeDtypeStruct((b, t, DR), BF)]
    if emit_state:
        out_specs.append(state)
        out_shape.append(jax.ShapeDtypeStruct((b, 2, DR), F32))
    res = pl.pallas_call(
        functools.partial(_rglru_kernel, t=t, has_h0=has_h0, emit_state=emit_state),
        grid=(b, nc),
        in_specs=in_specs,
        out_specs=out_specs,
        out_shape=out_shape,
        scratch_shapes=[pltpu.VMEM((2, t, CH), F32)] * 3,
        compiler_params=_cparams(("arbitrary", "arbitrary")),
        name="rglru",
    )(*args)
    return res if emit_state else (res[0], None)


def _attn_kernel(*refs, t, n_heads, has_ctx, q_block):
    it = iter(refs)
    q_ref, k_ref, v_ref = next(it), next(it), next(it)
    kc_ref = vc_ref = None
    if has_ctx:
        kc_ref, vc_ref = next(it), next(it)
    o_ref = next(it)
    log2_scale = (QK ** -0.5) * math.log2(math.e)
    for hd in range(n_heads):
        cols = slice(hd * HP, (hd + 1) * HP)
        k = k_ref[0, :, cols]
        v = v_ref[0, :, cols]
        if has_ctx:
            kc = kc_ref[0, :, cols]
            vc = vc_ref[0, :, cols]
        for qb in range(t // q_block):
            rows = slice(qb * q_block, (qb + 1) * q_block)
            q = q_ref[0, rows, cols]
            s = _dot_nt(q, k) * log2_scale
            m = jnp.max(s, axis=-1, keepdims=True)
            if has_ctx:
                sc = _dot_nt(q, kc) * log2_scale
                m = jnp.maximum(m, jnp.max(sc, axis=-1, keepdims=True))
            p = jnp.exp2(s - m)
            den = jnp.sum(p, axis=-1, keepdims=True)
            o = _dot(p.astype(BF), v)
            if has_ctx:
                pc = jnp.exp2(sc - m)
                den = den + jnp.sum(pc, axis=-1, keepdims=True)
                o = o + _dot(pc.astype(BF), vc)
            o_ref[0, rows, cols] = (o / den).astype(BF)


def _attention(q, k, v, kc, vc, heads_per_step):
    b, t, _ = q.shape
    has_ctx = kc is not None
    w = heads_per_step * HP
    blk = lambda n: pl.BlockSpec((1, n, w), lambda i, j: (i, 0, j))
    in_specs = [blk(t), blk(t), blk(t)]
    args = [q, k, v]
    if has_ctx:
        in_specs += [blk(kc.shape[1])] * 2
        args += [kc, vc]
    return pl.pallas_call(
        functools.partial(_attn_kernel, t=t, n_heads=heads_per_step, has_ctx=has_ctx, q_block=min(t, 256)),
        grid=(b, NH // heads_per_step),
        in_specs=in_specs,
        out_specs=blk(t),
        out_shape=jax.ShapeDtypeStruct((b, t, DH), BF),
        compiler_params=_cparams(("arbitrary", "arbitrary")),
        name="attention",
    )(*args)


def _route(logits):
    lane = lax.broadcasted_iota(jnp.int32, logits.shape, 1)
    lanef = lane.astype(F32)
    neg = -jnp.inf
    big = float(LANES)
    gl = jnp.where((lane >= NE) & (lane < NE + NG), logits, neg)
    gmax = jnp.max(gl, axis=-1, keepdims=True)
    gidx = jnp.min(jnp.where(gl == gmax, lanef, big), axis=-1, keepdims=True) - float(NE)
    gw = 1.0 / jnp.sum(jnp.exp(gl - gmax), axis=-1, keepdims=True)
    lo = gidx * float(EPG)
    el = jnp.where((lanef >= lo) & (lanef < lo + float(EPG)), logits, neg)
    v1 = jnp.max(el, axis=-1, keepdims=True)
    i1 = jnp.min(jnp.where(el == v1, lanef, big), axis=-1, keepdims=True)
    el2 = jnp.where(lanef == i1, neg, el)
    v2 = jnp.max(el2, axis=-1, keepdims=True)
    i2 = jnp.min(jnp.where(el2 == v2, lanef, big), axis=-1, keepdims=True)
    e2 = jnp.exp(v2 - v1)
    w1 = gw / (1.0 + e2)
    w2 = gw * e2 / (1.0 + e2)
    cmb = jnp.where(lanef == i1, w1, 0.0) + jnp.where(lanef == i2, w2, 0.0)
    return cmb, gidx


def _out_kernel(x_ref, h_ref, yr_ref, ya_ref, mod_ref, wgate_ref, wor_ref, wom_ref, wout_ref,
                n2_ref, rw_ref, rb_ref, x1_ref, hx_ref, rinfo_ref, tcnt_ref):
    x = x_ref[...]
    gl = _dot_nt(h_ref[...], wgate_ref[...])
    merged2 = ((jnp.tanh(gl[:, :D]) + 1.0) * _dot(yr_ref[...], wor_ref[...])
               + (jnp.tanh(gl[:, D:]) + 1.0) * _dot(ya_ref[...], wom_ref[...]))
    mix = _dot(merged2.astype(BF), wout_ref[...])
    x1 = x + mod_ref[0, 2:3, :] * mix
    x1_ref[...] = x1
    h2 = _modulated_norm(x1, n2_ref[...], mod_ref[0, 4:5, :], mod_ref[0, 3:4, :])
    h2_hi = h2.astype(BF)
    h2_lo = (h2 - h2_hi.astype(F32)).astype(BF)
    part = _dot(h2_hi, rw_ref[...])
    logits = part[:, :LANES] + part[:, LANES:] + _dot(h2_lo, rw_ref[:, :LANES]) + rb_ref[...]
    cmb, gidx = _route(logits)

    lanef = lax.broadcasted_iota(jnp.int32, cmb.shape, 1).astype(F32)
    ghot = jnp.where(lanef == gidx, 1.0, 0.0)
    r_i = lax.broadcasted_iota(jnp.int32, (TM, TM), 0)
    c_i = lax.broadcasted_iota(jnp.int32, (TM, TM), 1)
    tri = jnp.where(r_i > c_i, 1.0, 0.0).astype(BF)
    earlier_same = jnp.sum(_dot(tri, ghot.astype(BF)) * ghot, axis=-1, keepdims=True)
    counts = jnp.sum(ghot, axis=0, keepdims=True)
    padded = jnp.floor((counts + (RUN_ALIGN - 1.0)) * (1.0 / RUN_ALIGN)) * RUN_ALIGN
    lower_groups = jnp.sum(jnp.where(lanef < gidx, padded, 0.0), axis=-1, keepdims=True)
    lpos = lower_groups + earlier_same
    s_i = lax.broadcasted_iota(jnp.int32, (TM, TS), 1)
    to_sorted = jnp.where(s_i.astype(F32) == lpos, 1.0, 0.0).astype(BF)
    c1 = cmb.astype(BF)
    c2 = (cmb - c1.astype(F32)).astype(BF)
    c3 = (cmb - c1.astype(F32) - c2.astype(F32)).astype(BF)
    payload = jnp.concatenate([h2.astype(BF), c1, c2, c3], axis=1)
    srt = lax.dot_general(to_sorted, payload, (((0,), (0,)), ((), ())), preferred_element_type=F32)
    hx_ref[...] = srt.astype(BF)

    rinfo_ref[...] = jnp.where(lanef == 0.0, gidx, jnp.where(lanef == 1.0, lpos, 0.0))
    tcnt_ref[0] = jnp.broadcast_to(counts, (SUBLANES, LANES))


def _merge_out(x, h, yr, ya, mod, mod_row, wts):
    n = x.shape[0]
    n_tiles = n // TM
    tile = lambda w: pl.BlockSpec((TM, w), lambda i: (i, 0))
    return pl.pallas_call(
        _out_kernel,
        grid=(n_tiles,),
        in_specs=[tile(D), tile(D), tile(DR), tile(DH),
                  pl.BlockSpec((1, 6, D), lambda i: (mod_row(i), 0, 0)),
                  _resident((2 * D, D)), _resident((DR, D)), _resident((DH, D)), _resident((D, D)),
                  _resident((1, D)), _resident((D, 2 * LANES)), _resident((1, LANES))],
        out_specs=[tile(D), pl.BlockSpec((TS, XW), lambda i: (i, 0)), tile(LANES),
                   pl.BlockSpec((1, SUBLANES, LANES), lambda i: (i, 0, 0))],
        out_shape=[jax.ShapeDtypeStruct((n, D), F32), jax.ShapeDtypeStruct((n_tiles * TS, XW), BF),
                   jax.ShapeDtypeStruct((n, LANES), F32),
                   jax.ShapeDtypeStruct((n_tiles, SUBLANES, LANES), F32)],
        compiler_params=_cparams(("arbitrary",)),
        name="merge_out",
    )(x, h, yr, ya, mod, wts["w_gate"], wts["w_o_rnn"], wts["w_o_mla"], wts["w_out"],
      wts["n2"], wts["router_w"], wts["router_b"])


def _run_copies(local_ref, far_ref, len_ref, tile, make_copy, action):
    for g in range(NG):
        idx = tile * NG + g
        n = len_ref[idx]
        local0 = local_ref[idx]
        far0 = far_ref[idx]
        for k in reversed(range(RUN_ALIGN.bit_length() - 1, TS.bit_length())):
            size = 1 << k
            done = (n >> (k + 1)) << (k + 1)

            @pl.when(((n >> k) & 1) == 1)
            def _():
                action(make_copy(pl.multiple_of(local0 + done, RUN_ALIGN),
                                 pl.multiple_of(far0 + done, RUN_ALIGN), size))


def _dispatch_kernel(local_ref, far_ref, len_ref, hx_ref, xs_in_ref, xs_ref, sem):
    del xs_in_ref

    def copy(src, dst, size):
        return pltpu.make_async_copy(hx_ref.at[pl.ds(src, size)], xs_ref.at[pl.ds(dst, size)], sem)

    tile = pl.program_id(0)
    _run_copies(local_ref, far_ref, len_ref, tile, copy, lambda c: c.start())
    _run_copies(local_ref, far_ref, len_ref, tile, copy, lambda c: c.wait())


def _dispatch(tables, hx, xs):
    n_tiles = hx.shape[0] // TS
    return pl.pallas_call(
        _dispatch_kernel,
        grid_spec=pltpu.PrefetchScalarGridSpec(
            num_scalar_prefetch=3,
            grid=(n_tiles,),
            in_specs=[pl.BlockSpec((TS, XW), lambda i, *_: (i, 0)),
                      pl.BlockSpec(memory_space=pl.ANY)],
            out_specs=pl.BlockSpec(memory_space=pl.ANY),
            scratch_shapes=[pltpu.SemaphoreType.DMA(())],
        ),
        out_shape=jax.ShapeDtypeStruct(xs.shape, xs.dtype),
        input_output_aliases={4: 0},
        compiler_params=_cparams(("arbitrary",)),
        name="dispatch",
    )(*tables, hx, xs)


def _moe_kernel(tb_ref, tg_ref, nt_ref, xs_ref, w1_ref, w3_ref, w2_ref, o_ref, acc_ref):
    j = pl.program_id(0)
    pair = pl.program_id(1)

    @pl.when(j < nt_ref[0])
    def _():
        xt = xs_ref[:, :D]
        cmb = (xs_ref[:, D:D + LANES].astype(F32) + xs_ref[:, D + LANES:D + 2 * LANES].astype(F32)
               + xs_ref[:, D + 2 * LANES:].astype(F32))
        lane = lax.broadcasted_iota(jnp.int32, cmb.shape, 1)
        first = tg_ref[j] * EPG + pair * EXPERTS_PER_STEP
        hidden = []
        for u in range(EXPERTS_PER_STEP):
            a = _dot(xt, w1_ref[u].astype(BF))
            he = (a * _sigmoid(a)) * _dot(xt, w3_ref[u].astype(BF))
            ce = jnp.sum(jnp.where(lane == first + u, cmb, 0.0), axis=-1, keepdims=True)
            hidden.append((he * ce).astype(BF))
        w2 = w2_ref[...].reshape(EXPERTS_PER_STEP * DE, D).astype(BF)
        y = _dot(jnp.concatenate(hidden, axis=1), w2)

        @pl.when(pair == 0)
        def _():
            acc_ref[...] = y

        @pl.when(pair == EPG // EXPERTS_PER_STEP - 1)
        def _():
            o_ref[...] = (acc_ref[...] + y).astype(BF)

    @pl.when((j >= nt_ref[0]) & (pair == EPG // EXPERTS_PER_STEP - 1))
    def _():
        o_ref[...] = jnp.zeros_like(o_ref)


def _experts(tile_block, tile_group, n_tiles, xs, wts):
    m = xs.shape[0]
    steps = EPG // EXPERTS_PER_STEP
    assert steps == 2, "the kernel keeps one partial sum: first step stores it, second adds and writes"

    def w_idx(j, e, tb, tg, nt):
        return (tg[j] * steps + jnp.where(j < nt[0], e, steps - 1), 0, 0)

    return pl.pallas_call(
        _moe_kernel,
        grid_spec=pltpu.PrefetchScalarGridSpec(
            num_scalar_prefetch=3,
            grid=(m // TMOE, steps),
            in_specs=[pl.BlockSpec((TMOE, XW), lambda j, e, tb, tg, nt: (tb[j], 0)),
                      pl.BlockSpec((EXPERTS_PER_STEP, D, DE), w_idx),
                      pl.BlockSpec((EXPERTS_PER_STEP, D, DE), w_idx),
                      pl.BlockSpec((EXPERTS_PER_STEP, DE, D), w_idx)],
            out_specs=pl.BlockSpec((TMOE, D), lambda j, e, tb, tg, nt: (j, 0)),
            scratch_shapes=[pltpu.VMEM((TMOE, D), F32)],
        ),
        out_shape=jax.ShapeDtypeStruct((m, D), BF),
        compiler_params=_cparams(("arbitrary", "arbitrary")),
        name="experts",
    )(tile_block, tile_group, n_tiles, xs, wts["exp_w1"], wts["exp_w3"], wts["exp_w2"])


def _combine_kernel(local_ref, far_ref, len_ref, x1_ref, rinfo_ref, mod_ref, ys_ref, o_ref,
                    buf_ref, sem, *, n_steps):
    i = pl.program_id(0)

    def runs(step, slot, action):
        def copy(dst, src, size):
            return pltpu.make_async_copy(ys_ref.at[pl.ds(src, size)],
                                         buf_ref.at[slot, pl.ds(dst, size)], sem.at[slot])

        _run_copies(local_ref, far_ref, len_ref, step, copy, action)

    @pl.when(i == 0)
    def _():
        buf_ref[...] = jnp.zeros_like(buf_ref)
        runs(0, 0, lambda c: c.start())

    @pl.when(i + 1 < n_steps)
    def _():
        runs(i + 1, (i + 1) % 2, lambda c: c.start())

    slot = i % 2
    runs(i, slot, lambda c: c.wait())
    lpos = rinfo_ref[:, 1:2]
    s_i = lax.broadcasted_iota(jnp.int32, (TM, TS), 1)
    from_sorted = jnp.where(s_i.astype(F32) == lpos, 1.0, 0.0).astype(BF)
    moe = _dot(from_sorted, buf_ref[slot])
    o_ref[...] = x1_ref[...] + mod_ref[0, 5:6, :] * moe


def _combine(tables, x1, rinfo, mod, mod_row, ys):
    n = x1.shape[0]
    n_steps = n // TM
    return pl.pallas_call(
        functools.partial(_combine_kernel, n_steps=n_steps),
        grid_spec=pltpu.PrefetchScalarGridSpec(
            num_scalar_prefetch=3,
            grid=(n_steps,),
            in_specs=[pl.BlockSpec((TM, D), lambda i, *_: (i, 0)),
                      pl.BlockSpec((TM, LANES), lambda i, *_: (i, 0)),
                      pl.BlockSpec((1, 6, D), lambda i, *_: (mod_row(i), 0, 0)),
                      pl.BlockSpec(memory_space=pl.ANY)],
            out_specs=pl.BlockSpec((TM, D), lambda i, *_: (i, 0)),
            scratch_shapes=[pltpu.VMEM((2, TS, D), BF), pltpu.SemaphoreType.DMA((2,))],
        ),
        out_shape=jax.ShapeDtypeStruct((n, D), F32),
        compiler_params=_cparams(("arbitrary",)),
        name="combine",
    )(*tables, x1, rinfo, mod, ys)


def _run_lengths(tile_counts):
    counts = tile_counts[:, 0, :NG].astype(jnp.int32)
    return ((counts + RUN_ALIGN - 1) // RUN_ALIGN) * RUN_ALIGN


def _run_tables(lengths, first_far):
    local = jnp.cumsum(lengths, axis=1) - lengths
    far = first_far[None, :] + jnp.cumsum(lengths, axis=0) - lengths
    flat = lambda a: a.astype(jnp.int32).reshape(-1)
    return flat(local), flat(far), flat(lengths)


def _group_layout(counts, max_tiles):
    padded = ((counts + TMOE - 1) // TMOE) * TMOE
    ends = jnp.cumsum(padded)
    offsets = ends - padded
    n_tiles = (ends[-1] // TMOE).astype(jnp.int32)
    tile = jnp.minimum(jnp.arange(max_tiles, dtype=jnp.int32), jnp.maximum(n_tiles - 1, 0))
    tile_group = jnp.sum((tile[:, None] * TMOE >= ends[None, :]).astype(jnp.int32), axis=1)
    return offsets, tile, tile_group, n_tiles.reshape(1)


def _pad_heads(w, perm=None, rotary_only=False):
    lead = w.shape[:-1]
    per = w.shape[-1] // NH
    w = w.reshape(lead + (NH, per))
    if perm is not None:
        nope = jnp.zeros_like(w[..., :NOPE]) if rotary_only else w[..., :NOPE]
        w = jnp.concatenate([nope, w[..., NOPE:][..., perm]], axis=-1)
    w = jnp.pad(w, [(0, 0)] * len(lead) + [(0, 0), (0, HP - per)])
    return w.reshape(lead + (NH * HP,))


def _pad_gain(g, perm, rotary_only=False):
    nope = jnp.zeros((NOPE,), F32) if rotary_only else g[:NOPE]
    g = jnp.concatenate([nope, g[NOPE:][perm], jnp.zeros((HP - QK,), F32)])
    return g.reshape(1, HP)


def _block_diag(w):
    per = BDW // LRU_BLOCK
    w = w.reshape(DR // BDW, per, LRU_BLOCK, LRU_BLOCK)
    bd = jnp.einsum("jpab,pq->jpaqb", w, jnp.eye(per, dtype=w.dtype))
    return bd.reshape(DR // BDW, BDW, BDW)


def _prepare_shared(l, p):
    w_in = lambda a, b: jnp.transpose(lax.slice(p["w_in"], (l, 0, a), (l + 1, D, b)).reshape(D, b - a))
    o1, o2, o3, o4, o5 = DR, 2 * DR, 2 * DR + QL, 2 * DR + QL + KVL, 2 * DR + QL + KVL + ROPE
    bd = jnp.stack([_block_diag(p["lru_wa"][l, 0]), _block_diag(p["lru_wx"][l, 0]),
                    _block_diag(p["lru_wa"][l, 1]), _block_diag(p["lru_wx"][l, 1])])
    bd = (0.5 * bd).astype(BF)
    wom = p["w_o_mla"][l].reshape(NH, VD, D)
    wom = jnp.pad(wom, ((0, 0), (0, HP - VD), (0, 0))).reshape(DH, D)
    router_w = jnp.concatenate([p["router_we"][l], p["router_wg"][l],
                                jnp.zeros((D, LANES - NE - NG), F32)], axis=1)
    router_b = jnp.concatenate([p["router_be"][l], p["router_bg"][l],
                                jnp.zeros((LANES - NE - NG,), F32)]).reshape(1, LANES)
    router_hi = router_w.astype(BF)
    router_lo = (router_w - router_hi.astype(F32)).astype(BF)
    router_w = jnp.concatenate([router_hi, router_lo], axis=1)
    return {
        "n1": p["norm1_g"][l].reshape(1, D), "n2": p["norm2_g"][l].reshape(1, D),
        "w_x": w_in(0, o1).astype(BF), "w_g": w_in(o1, o2).astype(BF),
        "w_q": w_in(o2, o3).astype(BF), "w_kv": w_in(o3, o4), "w_kr": w_in(o4, o5),
        "w_gate": (0.5 * w_in(o5, o5 + 2 * D)).astype(BF),
        "qan": p["q_a_norm"][l].reshape(1, QL), "kvan": p["kv_a_norm"][l].reshape(1, KVL),
        "w_uk": _pad_heads(p["w_uk"][l]).astype(BF),
        "w_uv": _pad_heads(p["w_uv"][l]).astype(BF),
        "conv_w": p["conv_w"][l], "conv_b": p["conv_b"][l].reshape(1, DR), "bd": bd,
        "lru_ba": 0.5 * p["lru_ba"][l], "lru_bx": 0.5 * p["lru_bx"][l], "lru_lam": p["lru_lam"][l],
        "w_o_rnn": p["w_o_rnn"][l].astype(BF), "w_o_mla": wom.astype(BF), "w_out": (0.5 * p["w_out"][l]).astype(BF),
        "router_w": router_w, "router_b": router_b,
        "exp_w1": p["exp_w1"][l], "exp_w3": p["exp_w3"][l], "exp_w2": p["exp_w2"][l],
    }


def _with_rope_order(l, p, shared, perm, rotary):
    w = dict(shared)
    zeros = lambda n: jnp.zeros((n, D), F32)
    rope_block = lambda order: [zeros(NOPE), shared["w_kr"][order, :], zeros(HP - QK)]
    kvr = [shared["w_kv"]] + rope_block(perm)
    w["w_uq"] = _pad_heads(p["w_uq"][l], perm).astype(BF)
    w["gq"] = _pad_gain(p["q_norm"][l], perm)
    w["gk"] = _pad_gain(p["k_norm"][l], perm)
    if rotary:
        pair = np.concatenate([perm[ROPE // 2:], perm[:ROPE // 2]])
        kvr += rope_block(pair)
        w["w_uq_pair"] = _pad_heads(p["w_uq"][l], pair, rotary_only=True).astype(BF)
        w["gq_pair"] = _pad_gain(p["q_norm"][l], pair, rotary_only=True)
        w["gk_pair"] = _pad_gain(p["k_norm"][l], pair, rotary_only=True)
    w["w_kvr"] = jnp.concatenate(kvr, axis=0).astype(BF)
    return w


def _rope_tables(n_tokens, perm):
    rows = n_tokens // GRID_W
    row = np.repeat(np.arange(rows), GRID_W).astype(np.float32)
    col = np.tile(np.arange(GRID_W), rows).astype(np.float32)
    axis_dim = ROPE // 2
    inv = (np.float32(ROPE_BASE) ** (-np.arange(0, axis_dim, 2, dtype=np.float32) / axis_dim)).astype(np.float32)
    ang = np.concatenate([row[:, None] * inv, col[:, None] * inv], axis=-1).astype(np.float32)
    cos, sin = np.cos(ang), np.sin(ang)
    ones = lambda n: np.ones((n_tokens, n), np.float32)
    zeros = lambda n: np.zeros((n_tokens, n), np.float32)
    cos_t = np.concatenate([ones(NOPE), cos, cos, ones(HP - QK)], axis=1)
    sin_t = np.concatenate([zeros(NOPE), -sin, sin, zeros(HP - QK)], axis=1)
    return jnp.asarray(cos_t, F32), jnp.asarray(sin_t, F32)


def kernel(x_prompt, x_sample, cache_mla_ckv, cache_mla_krope, state_rglru, c, c_ctx, norm1_g, norm2_g, w_mod, b_mod, w_in, conv_w, conv_b, lru_wa, lru_ba, lru_wx, lru_bx, lru_lam, q_a_norm, kv_a_norm, w_uq, w_uk, w_uv, q_norm, k_norm, w_o_rnn, w_o_mla, w_out, router_wg, router_bg, router_we, router_be, exp_w1, exp_w3, exp_w2):
    p = dict(norm1_g=norm1_g, norm2_g=norm2_g, w_in=w_in, conv_w=conv_w, conv_b=conv_b,
             lru_wa=lru_wa, lru_ba=lru_ba, lru_wx=lru_wx, lru_bx=lru_bx, lru_lam=lru_lam,
             q_a_norm=q_a_norm, kv_a_norm=kv_a_norm, w_uq=w_uq, w_uk=w_uk, w_uv=w_uv,
             q_norm=q_norm, k_norm=k_norm, w_o_rnn=w_o_rnn, w_o_mla=w_o_mla, w_out=w_out,
             router_wg=router_wg, router_bg=router_bg, router_we=router_we, router_be=router_be,
             exp_w1=exp_w1, exp_w3=exp_w3, exp_w2=exp_w2)
    depth = w_in.shape[0]
    nb, seq, _ = x_prompt.shape
    db, dseq, _ = x_sample.shape
    ident = np.arange(ROPE)
    halves = np.concatenate([np.arange(0, ROPE, 2), np.arange(1, ROPE, 2)])
    rope_tabs = _rope_tables(dseq, halves)
    cond8 = jnp.concatenate([c_ctx[None, :], c, jnp.zeros((SUBLANES - 1 - db, D), F32)], axis=0)
    ctx_row = lambda tile_rows: (lambda i: 0)
    lat_row = lambda tile_rows: (lambda i: (i * tile_rows) // dseq + 1)
    n_ctx, n_lat = nb * seq, db * dseq
    run_padding = ((n_ctx + n_lat) // TM) * NG * (RUN_ALIGN - 1)
    max_tiles = -(-(n_ctx + n_lat + run_padding) // TMOE) + NG
    per_seq = lambda arrs, b, t: [a.reshape(b, t, a.shape[-1]) for a in arrs]
    flat = lambda a: a.reshape(-1, a.shape[-1])

    y_prompt, y_sample = x_prompt.reshape(n_ctx, D), x_sample.reshape(n_lat, D)
    ckv_list, krope_list, rnn_list = [], [], []
    for l in range(depth):
        shared = _prepare_shared(l, p)
        w_ctx = _with_rope_order(l, p, shared, ident, False)
        w_lat = _with_rope_order(l, p, shared, halves, True)
        mod = _modulation(cond8, w_mod[l], b_mod[l]).reshape(SUBLANES, 6, D)

        h, xr, gg, q, k, v, ckv, kro = _projections(y_prompt, mod, ctx_row(TM), w_ctx, None, True)
        xr, gg, q, k, v = per_seq([xr, gg, q, k, v], nb, seq)
        yr, h_fin = _rglru(xr, gg, shared, None, True)
        ya = _attention(q, k, v, None, None, NH)
        x1_c, hx_c, ri_c, tc_c = _merge_out(y_prompt, h, flat(yr), flat(ya), mod, ctx_row(TM), shared)
        ckv_list.append(ckv.reshape(nb, seq, KVL))
        krope_list.append(kro.reshape(nb, seq, ROPE))
        rnn_list.append(h_fin)

        krp_cache = jnp.pad(cache_mla_krope[:, l][..., halves], ((0, 0), (0, 0), (NOPE, HP - QK)))
        kc, vc = _cache_keys_values(cache_mla_ckv[:, l], krp_cache, w_lat)
        h, xr, gg, q, k, v = _projections(y_sample, mod, lat_row(TM), w_lat, rope_tabs, False)
        xr, gg, q, k, v = per_seq([xr, gg, q, k, v], db, dseq)
        yr, _ = _rglru(xr, gg, shared, state_rglru[:, l], False)
        ya = _attention(q, k, v, kc, vc, 2)
        x1_l, hx_l, ri_l, tc_l = _merge_out(y_sample, h, flat(yr), flat(ya), mod, lat_row(TM), shared)

        len_c, len_l = _run_lengths(tc_c), _run_lengths(tc_l)
        rows_c = jnp.sum(len_c, axis=0)
        offsets, tile_block, tile_group, n_tiles = _group_layout(rows_c + jnp.sum(len_l, axis=0), max_tiles)
        runs_c = _run_tables(len_c, offsets)
        runs_l = _run_tables(len_l, offsets + rows_c)
        xs = jnp.zeros((max_tiles * TMOE, XW), BF)
        xs = _dispatch(runs_c, hx_c, xs)
        xs = _dispatch(runs_l, hx_l, xs)
        ys = _experts(tile_block, tile_group, n_tiles, xs, shared)
        y_prompt = _combine(runs_c, x1_c, ri_c, mod, ctx_row(TM), ys)
        y_sample = _combine(runs_l, x1_l, ri_l, mod, lat_row(TM), ys)

    y_prompt, y_sample = y_prompt.reshape(nb, seq, D), y_sample.reshape(db, dseq, D)

    return (y_prompt, y_sample, jnp.stack(ckv_list, axis=1), jnp.stack(krope_list, axis=1),
            jnp.stack(rnn_list, axis=1))
```

```python
import functools
import math

import numpy as np
import jax
import jax.numpy as jnp
from jax import lax
from jax.experimental import pallas as pl
from jax.experimental.pallas import tpu as pltpu

D = 1024
DR = 1024
QL = 384
KVL = 256
NH = 8
NOPE = 64
ROPE = 32
QK = NOPE + ROPE
VD = 64
HP = 128
DH = NH * HP
GRID_W = 64
ROPE_BASE = 10000.0
EPS = 1e-6
TINY = 1e-30
LRU_C = 8.0
LRU_BLOCK = 64
BDW = 256
CH = 1024
NLT = CH // 128
NG = 4
EPG = 4
NE = NG * EPG
DE = 512
LANES = 128
SUBLANES = 8
TM = 512
TS = 640
TMOE = 512
EXPERTS_PER_STEP = 2
XW = D + 3 * 128
RUN_ALIGN = 16
VMEM_LIMIT = 52 * 1024 * 1024
BF = jnp.bfloat16
F32 = jnp.float32


def _cparams(sem):
    return pltpu.CompilerParams(dimension_semantics=sem, vmem_limit_bytes=VMEM_LIMIT)


def _dot(a, b):
    return jnp.dot(a, b, preferred_element_type=F32)


def _dot_nt(a, b):
    return lax.dot_general(a, b, (((1,), (1,)), ((), ())), preferred_element_type=F32)


def _rms(x, g, width):
    ms = jnp.sum(x * x, axis=-1, keepdims=True) * (1.0 / width)
    return x * lax.rsqrt(ms + EPS) * g


def _modulated_norm(x, g, scale, shift):
    return _rms(x, g * (1.0 + scale), D) + shift


def _mod_kernel(c_ref, w_ref, b_ref, o_ref):
    c = c_ref[...]
    s = c * jax.nn.sigmoid(c)
    o_ref[...] = _dot(s, w_ref[...]) + b_ref[...]


def _modulation(cond8, w_mod, b_mod):
    n = w_mod.shape[1]
    return pl.pallas_call(
        _mod_kernel,
        grid=(n // D,),
        in_specs=[
            pl.BlockSpec((SUBLANES, D), lambda j: (0, 0)),
            pl.BlockSpec((D, D), lambda j: (0, j)),
            pl.BlockSpec((1, D), lambda j: (0, j)),
        ],
        out_specs=pl.BlockSpec((SUBLANES, D), lambda j: (0, j)),
        out_shape=jax.ShapeDtypeStruct((SUBLANES, n), F32),
        compiler_params=_cparams(("arbitrary",)),
        name="modulation",
    )(cond8, w_mod, b_mod.reshape(1, n))


def _head_norm(xh, gain, cos=None, partner_scaled=None):
    ms = jnp.sum(xh * xh, axis=-1, keepdims=True) * (1.0 / QK)
    rs = lax.rsqrt(ms + EPS)
    y = xh * rs * gain
    if cos is None:
        return y
    return y * cos + partner_scaled * rs


def _keys_values(ckv, krp, wuk_ref, wuv_ref, gk, cos, partner_scaled, k_ref, v_ref):
    cb = ckv.astype(BF)
    kn = _dot(cb, wuk_ref[...])
    v_ref[...] = _dot(cb, wuv_ref[...]).astype(BF)
    for h in range(NH):
        kh = kn[:, h * HP:(h + 1) * HP] + krp
        k_ref[:, h * HP:(h + 1) * HP] = _head_norm(kh, gk, cos, partner_scaled).astype(BF)


def _proj_kernel(*refs, rope, emit_cache):
    it = iter(refs)
    x_ref, mod_ref, n1_ref = next(it), next(it), next(it)
    wx_ref, wg_ref, wq_ref, wkvr_ref = next(it), next(it), next(it), next(it)
    qan_ref, kvan_ref, wuq_ref, gq_ref = next(it), next(it), next(it), next(it)
    wuk_ref, gk_ref, wuv_ref = next(it), next(it), next(it)
    if rope:
        wuqs_ref, gqs_ref, gks_ref, cos_ref, sins_ref = (next(it) for _ in range(5))
    h_ref, xr_ref, gg_ref, q_ref, k_ref, v_ref = (next(it) for _ in range(6))
    if emit_cache:
        ckv_ref, kro_ref = next(it), next(it)

    hb = _modulated_norm(x_ref[...], n1_ref[...], mod_ref[0, 1:2, :], mod_ref[0, 0:1, :]).astype(BF)
    h_ref[...] = hb
    xr_ref[...] = _dot_nt(hb, wx_ref[...]).astype(BF)
    gg_ref[...] = jax.nn.gelu(_dot_nt(hb, wg_ref[...])).astype(BF)

    qnb = _rms(_dot_nt(hb, wq_ref[...]), qan_ref[...], QL).astype(BF)
    q = _dot(qnb, wuq_ref[...])
    gq = gq_ref[...]
    cos = q_partner = q_pair_scale = None
    if rope:
        cos, sins = cos_ref[...], sins_ref[...]
        q_partner = _dot(qnb, wuqs_ref[...])
        q_pair_scale = gqs_ref[...] * sins
    for hd in range(NH):
        cols = slice(hd * HP, (hd + 1) * HP)
        partner = q_partner[:, cols] * q_pair_scale if rope else None
        q_ref[:, cols] = _head_norm(q[:, cols], gq, cos, partner).astype(BF)

    kvr = _dot_nt(hb, wkvr_ref[...])
    ckv = _rms(kvr[:, :KVL], kvan_ref[...], KVL)
    krp = kvr[:, KVL:KVL + HP]
    k_partner = kvr[:, KVL + HP:KVL + 2 * HP] * (gks_ref[...] * sins) if rope else None
    if emit_cache:
        ckv_ref[...] = ckv
        kro_ref[...] = krp[:, NOPE:NOPE + ROPE]
    _keys_values(ckv, krp, wuk_ref, wuv_ref, gk_ref[...], cos, k_partner, k_ref, v_ref)


def _resident(shape):
    return pl.BlockSpec(shape, lambda i: (0,) * len(shape), pipeline_mode=pl.Buffered(1))


def _projections(x, mod, mod_row, wts, rope_tabs, emit_cache):
    n = x.shape[0]
    rope = rope_tabs is not None
    tile = lambda w: pl.BlockSpec((TM, w), lambda i: (i, 0))
    in_specs = [
        tile(D),
        pl.BlockSpec((1, 6, D), lambda i: (mod_row(i), 0, 0)),
        _resident((1, D)),
        _resident((DR, D)), _resident((DR, D)), _resident((QL, D)), _resident(wts["w_kvr"].shape),
        _resident((1, QL)), _resident((1, KVL)), _resident((QL, DH)), _resident((1, HP)),
        _resident((KVL, DH)), _resident((1, HP)), _resident((KVL, DH)),
    ]
    args = [x, mod, wts["n1"], wts["w_x"], wts["w_g"], wts["w_q"], wts["w_kvr"],
            wts["qan"], wts["kvan"], wts["w_uq"], wts["gq"], wts["w_uk"], wts["gk"], wts["w_uv"]]
    if rope:
        tiles_per_seq = rope_tabs[0].shape[0] // TM
        in_specs += [_resident((QL, DH)), _resident((1, HP)), _resident((1, HP))]
        in_specs += [pl.BlockSpec((TM, HP), lambda i: (i % tiles_per_seq, 0))] * 2
        args += [wts["w_uq_pair"], wts["gq_pair"], wts["gk_pair"]] + list(rope_tabs)
    out_specs = [tile(D), tile(DR), tile(DR), tile(DH), tile(DH), tile(DH)]
    out_shape = [jax.ShapeDtypeStruct((n, D), BF), jax.ShapeDtypeStruct((n, DR), BF),
                 jax.ShapeDtypeStruct((n, DR), BF), jax.ShapeDtypeStruct((n, DH), BF),
                 jax.ShapeDtypeStruct((n, DH), BF), jax.ShapeDtypeStruct((n, DH), BF)]
    if emit_cache:
        out_specs += [tile(KVL), tile(ROPE)]
        out_shape += [jax.ShapeDtypeStruct((n, KVL), F32), jax.ShapeDtypeStruct((n, ROPE), F32)]
    return pl.pallas_call(
        functools.partial(_proj_kernel, rope=rope, emit_cache=emit_cache),
        grid=(n // TM,),
        in_specs=in_specs,
        out_specs=out_specs,
        out_shape=out_shape,
        compiler_params=_cparams(("arbitrary",)),
        name="projections",
    )(*args)


def _cache_kv_kernel(ckv_ref, krp_ref, wuk_ref, gk_ref, wuv_ref, k_ref, v_ref):
    _keys_values(ckv_ref[0], krp_ref[0], wuk_ref, wuv_ref, gk_ref[...], None, None,
                 k_ref.at[0], v_ref.at[0])


def _cache_keys_values(ckv, krp, wts):
    b, s, _ = ckv.shape
    full = lambda shape: pl.BlockSpec(shape, lambda i: (0,) * len(shape))
    return pl.pallas_call(
        _cache_kv_kernel,
        grid=(b,),
        in_specs=[pl.BlockSpec((1, s, KVL), lambda i: (i, 0, 0)),
                  pl.BlockSpec((1, s, HP), lambda i: (i, 0, 0)),
                  full((KVL, DH)), full((1, HP)), full((KVL, DH))],
        out_specs=[pl.BlockSpec((1, s, DH), lambda i: (i, 0, 0))] * 2,
        out_shape=[jax.ShapeDtypeStruct((b, s, DH), BF)] * 2,
        compiler_params=_cparams(("arbitrary",)),
        name="cache_keys_values",
    )(ckv, krp, wts["w_uk"], wts["gk"], wts["w_uv"])


WIN = 256
SEG = WIN // 8


def _window_permutation():
    dst = np.arange(WIN)
    p = np.zeros((WIN, WIN), np.float32)
    p[dst, (dst % SUBLANES) * SEG + dst // SUBLANES] = 1.0
    return p


def _sigmoid(x):
    return 0.5 * jnp.tanh(0.5 * x) + 0.5


def _segment_pass(a_scr, b_scr, bases, inits, out_scr=None):
    def body(k, carry):
        new = []
        for d in range(2):
            h, p = carry[d]
            i = k if d == 0 else SEG - 1 - k
            rows = pl.ds(pl.multiple_of(bases[d] + i * SUBLANES, SUBLANES), SUBLANES)
            a = a_scr[d, rows, :]
            h = a * h + b_scr[d, rows, :]
            if out_scr is None:
                p = a * p
            else:
                out_scr[d, rows, :] = h
            new.append((h, p))
        return tuple(new)

    init = tuple((inits[d], jnp.ones_like(inits[d])) for d in range(2))
    return lax.fori_loop(0, SEG, body, init, unroll=4)


def _segment_entries(end, decay, carry_in, forward):
    order = range(SUBLANES) if forward else reversed(range(SUBLANES))
    rows = [None] * SUBLANES
    c = carry_in
    for s in order:
        rows[s] = c
        c = end[s:s + 1, :] + decay[s:s + 1, :] * c
    return jnp.concatenate(rows, axis=0), c


def _rglru_kernel(*refs, t, has_h0, emit_state):
    it = iter(refs)
    xr_ref, gg_ref, perm_ref, unperm_ref = (next(it) for _ in range(4))
    cw_ref, cb_ref, bd_ref, ba_ref, bx_ref, lam_ref = (next(it) for _ in range(6))
    h0_ref = next(it) if has_h0 else None
    y_ref = next(it)
    hf_ref = next(it) if emit_state else None
    a_scr, b_scr, h_scr = (next(it) for _ in range(3))

    n_win = t // WIN
    sub = lax.broadcasted_iota(jnp.int32, (SUBLANES, CH), 0)
    zero_row = jnp.zeros((1, CH), F32)
    edge = 2 * SUBLANES
    for w in range(n_win):
        lo, hi = w * WIN, (w + 1) * WIN
        xp = _dot(perm_ref[...], xr_ref[0, lo:hi, :])
        before = xr_ref[0, lo - edge:lo, :].astype(F32)[edge - 1:edge, :] if w > 0 else zero_row
        after = xr_ref[0, hi:hi + edge, :].astype(F32) if w < n_win - 1 else None
        after0 = after[0:1, :] if after is not None else zero_row
        after1 = after[1:2, :] if after is not None else zero_row
        tile_m1 = jnp.where(sub == 0, before, pltpu.roll(xp[WIN - SUBLANES:WIN, :], 1, 0))
        tile_p0 = jnp.where(sub == SUBLANES - 1, after0, pltpu.roll(xp[0:SUBLANES, :], SUBLANES - 1, 0))
        tile_p1 = jnp.where(sub == SUBLANES - 1, after1,
                            pltpu.roll(xp[SUBLANES:2 * SUBLANES, :], SUBLANES - 1, 0))
        xe = jnp.concatenate([tile_m1, xp, tile_p0, tile_p1], axis=0)
        xc = cb_ref[...]
        for tap in range(4):
            xc = xc + xe[tap * SUBLANES:tap * SUBLANES + WIN, :] * cw_ref[tap:tap + 1, :]
        for s in range(CH // BDW):
            cols = slice(s * BDW, (s + 1) * BDW)
            xs = xc[:, cols]
            xsb = xs.astype(BF)
            xh = 0.5 * xs
            for d in range(2):
                tr = jnp.tanh(_dot(xsb, bd_ref[2 * d, s]) + ba_ref[d:d + 1, cols])
                ti = jnp.tanh(_dot(xsb, bd_ref[2 * d + 1, s]) + bx_ref[d:d + 1, cols])
                nl = -lam_ref[d:d + 1, cols]
                softplus = jnp.maximum(nl, 0.0) + jnp.log(1.0 + jnp.exp(-jnp.abs(nl)))
                ch = (-0.5 * LRU_C) * softplus
                a = jnp.exp(tr * ch + ch)
                z = 1.0 - a * a
                root = z * lax.rsqrt(jnp.maximum(z, TINY))
                a_scr[d, lo:hi, cols] = a
                b_scr[d, lo:hi, cols] = root * (ti * xh + xh)

    if has_h0:
        carry = [h0_ref[0, 0:1, :], h0_ref[0, 1:2, :]]
    else:
        carry = [zero_row, zero_row]
    zeros = jnp.zeros((SUBLANES, CH), F32)
    for k in range(n_win):
        bases = (k * WIN, (n_win - 1 - k) * WIN)
        totals = _segment_pass(a_scr, b_scr, bases, (zeros, zeros))
        entries = []
        for d in range(2):
            entry, carry[d] = _segment_entries(totals[d][0], totals[d][1], carry[d], d == 0)
            entries.append(entry)
        _segment_pass(a_scr, b_scr, bases, entries, out_scr=h_scr)

    if emit_state:
        hf_ref[0, 0:1, :] = carry[0]
        hf_ref[0, 1:2, :] = carry[1]
    for w in range(n_win):
        lo, hi = w * WIN, (w + 1) * WIN
        gate = _dot(perm_ref[...], gg_ref[0, lo:hi, :])
        yp = ((h_scr[0, lo:hi, :] + h_scr[1, lo:hi, :]) * gate).astype(BF)
        y_ref[0, lo:hi, :] = _dot(unperm_ref[...], yp).astype(BF)


def _rglru(xr, gg, wts, h0, emit_state):
    b, t, _ = xr.shape
    nc = DR // CH
    has_h0 = h0 is not None
    chunk = lambda r: pl.BlockSpec((r, CH), lambda i, j: (0, j))
    seq = pl.BlockSpec((1, t, CH), lambda i, j: (i, 0, j))
    state = pl.BlockSpec((1, 2, CH), lambda i, j: (i, 0, j))
    window = pl.BlockSpec((WIN, WIN), lambda i, j: (0, 0))
    perm = _window_permutation()
    in_specs = [seq, seq, window, window, chunk(4), chunk(1),
                pl.BlockSpec((4, CH // BDW, BDW, BDW), lambda i, j: (0, j, 0, 0)),
                chunk(2), chunk(2), chunk(2)]
    args = [xr, gg, jnp.asarray(perm, BF), jnp.asarray(perm.T, BF),
            wts["conv_w"], wts["conv_b"], wts["bd"], wts["lru_ba"], wts["lru_bx"], wts["lru_lam"]]
    if has_h0:
        in_specs.append(state)
        args.append(h0)
    out_specs = [seq]
    out_shape = [jax.ShapeDtypeStruct((b, t, DR), BF)]
    if emit_state:
        out_specs.append(state)
        out_shape.append(jax.ShapeDtypeStruct((b, 2, DR), F32))
    res = pl.pallas_call(
        functools.partial(_rglru_kernel, t=t, has_h0=has_h0, emit_state=emit_state),
        grid=(b, nc),
        in_specs=in_specs,
        out_specs=out_specs,
        out_shape=out_shape,
        scratch_shapes=[pltpu.VMEM((2, t, CH), F32)] * 3,
        compiler_params=_cparams(("arbitrary", "arbitrary")),
        name="rglru",
    )(*args)
    return res if emit_state else (res[0], None)


def _attn_kernel(*refs, t, n_heads, has_ctx, q_block):
    it = iter(refs)
    q_ref, k_ref, v_ref = next(it), next(it), next(it)
    kc_ref = vc_ref = None
    if has_ctx:
        kc_ref, vc_ref = next(it), next(it)
    o_ref = next(it)
    log2_scale = (QK ** -0.5) * math.log2(math.e)
    for hd in range(n_heads):
        cols = slice(hd * HP, (hd + 1) * HP)
        k = k_ref[0, :, cols]
        v = v_ref[0, :, cols]
        if has_ctx:
            kc = kc_ref[0, :, cols]
            vc = vc_ref[0, :, cols]
        for qb in range(t // q_block):
            rows = slice(qb * q_block, (qb + 1) * q_block)
            q = q_ref[0, rows, cols]
            s = _dot_nt(q, k) * log2_scale
            m = jnp.max(s, axis=-1, keepdims=True)
            if has_ctx:
                sc = _dot_nt(q, kc) * log2_scale
                m = jnp.maximum(m, jnp.max(sc, axis=-1, keepdims=True))
            p = jnp.exp2(s - m)
            den = jnp.sum(p, axis=-1, keepdims=True)
            o = _dot(p.astype(BF), v)
            if has_ctx:
                pc = jnp.exp2(sc - m)
                den = den + jnp.sum(pc, axis=-1, keepdims=True)
                o = o + _dot(pc.astype(BF), vc)
            o_ref[0, rows, cols] = (o / den).astype(BF)


def _attention(q, k, v, kc, vc, heads_per_step):
    b, t, _ = q.shape
    has_ctx = kc is not None
    w = heads_per_step * HP
    blk = lambda n: pl.BlockSpec((1, n, w), lambda i, j: (i, 0, j))
    in_specs = [blk(t), blk(t), blk(t)]
    args = [q, k, v]
    if has_ctx:
        in_specs += [blk(kc.shape[1])] * 2
        args += [kc, vc]
    return pl.pallas_call(
        functools.partial(_attn_kernel, t=t, n_heads=heads_per_step, has_ctx=has_ctx, q_block=min(t, 256)),
        grid=(b, NH // heads_per_step),
        in_specs=in_specs,
        out_specs=blk(t),
        out_shape=jax.ShapeDtypeStruct((b, t, DH), BF),
        compiler_params=_cparams(("arbitrary", "arbitrary")),
        name="attention",
    )(*args)


def _route(logits):
    lane = lax.broadcasted_iota(jnp.int32, logits.shape, 1)
    lanef = lane.astype(F32)
    neg = -jnp.inf
    big = float(LANES)
    gl = jnp.where((lane >= NE) & (lane < NE + NG), logits, neg)
    gmax = jnp.max(gl, axis=-1, keepdims=True)
    gidx = jnp.min(jnp.where(gl == gmax, lanef, big), axis=-1, keepdims=True) - float(NE)
    gw = 1.0 / jnp.sum(jnp.exp(gl - gmax), axis=-1, keepdims=True)
    lo = gidx * float(EPG)
    el = jnp.where((lanef >= lo) & (lanef < lo + float(EPG)), logits, neg)
    v1 = jnp.max(el, axis=-1, keepdims=True)
    i1 = jnp.min(jnp.where(el == v1, lanef, big), axis=-1, keepdims=True)
    el2 = jnp.where(lanef == i1, neg, el)
    v2 = jnp.max(el2, axis=-1, keepdims=True)
    i2 = jnp.min(jnp.where(el2 == v2, lanef, big), axis=-1, keepdims=True)
    e2 = jnp.exp(v2 - v1)
    w1 = gw / (1.0 + e2)
    w2 = gw * e2 / (1.0 + e2)
    cmb = jnp.where(lanef == i1, w1, 0.0) + jnp.where(lanef == i2, w2, 0.0)
    return cmb, gidx


def _out_kernel(x_ref, h_ref, yr_ref, ya_ref, mod_ref, wgate_ref, wor_ref, wom_ref, wout_ref,
                n2_ref, rw_ref, rb_ref, x1_ref, hx_ref, rinfo_ref, tcnt_ref):
    x = x_ref[...]
    gl = _dot_nt(h_ref[...], wgate_ref[...])
    merged2 = ((jnp.tanh(gl[:, :D]) + 1.0) * _dot(yr_ref[...], wor_ref[...])
               + (jnp.tanh(gl[:, D:]) + 1.0) * _dot(ya_ref[...], wom_ref[...]))
    mix = _dot(merged2.astype(BF), wout_ref[...])
    x1 = x + mod_ref[0, 2:3, :] * mix
    x1_ref[...] = x1
    h2 = _modulated_norm(x1, n2_ref[...], mod_ref[0, 4:5, :], mod_ref[0, 3:4, :])
    h2_hi = h2.astype(BF)
    h2_lo = (h2 - h2_hi.astype(F32)).astype(BF)
    part = _dot(h2_hi, rw_ref[...])
    logits = part[:, :LANES] + part[:, LANES:] + _dot(h2_lo, rw_ref[:, :LANES]) + rb_ref[...]
    cmb, gidx = _route(logits)

    lanef = lax.broadcasted_iota(jnp.int32, cmb.shape, 1).astype(F32)
    ghot = jnp.where(lanef == gidx, 1.0, 0.0)
    r_i = lax.broadcasted_iota(jnp.int32, (TM, TM), 0)
    c_i = lax.broadcasted_iota(jnp.int32, (TM, TM), 1)
    tri = jnp.where(r_i > c_i, 1.0, 0.0).astype(BF)
    earlier_same = jnp.sum(_dot(tri, ghot.astype(BF)) * ghot, axis=-1, keepdims=True)
    counts = jnp.sum(ghot, axis=0, keepdims=True)
    padded = jnp.floor((counts + (RUN_ALIGN - 1.0)) * (1.0 / RUN_ALIGN)) * RUN_ALIGN
    lower_groups = jnp.sum(jnp.where(lanef < gidx, padded, 0.0), axis=-1, keepdims=True)
    lpos = lower_groups + earlier_same
    s_i = lax.broadcasted_iota(jnp.int32, (TM, TS), 1)
    to_sorted = jnp.where(s_i.astype(F32) == lpos, 1.0, 0.0).astype(BF)
    c1 = cmb.astype(BF)
    c2 = (cmb - c1.astype(F32)).astype(BF)
    c3 = (cmb - c1.astype(F32) - c2.astype(F32)).astype(BF)
    payload = jnp.concatenate([h2.astype(BF), c1, c2, c3], axis=1)
    srt = lax.dot_general(to_sorted, payload, (((0,), (0,)), ((), ())), preferred_element_type=F32)
    hx_ref[...] = srt.astype(BF)

    rinfo_ref[...] = jnp.where(lanef == 0.0, gidx, jnp.where(lanef == 1.0, lpos, 0.0))
    tcnt_ref[0] = jnp.broadcast_to(counts, (SUBLANES, LANES))


def _merge_out(x, h, yr, ya, mod, mod_row, wts):
    n = x.shape[0]
    n_tiles = n // TM
    tile = lambda w: pl.BlockSpec((TM, w), lambda i: (i, 0))
    return pl.pallas_call(
        _out_kernel,
        grid=(n_tiles,),
        in_specs=[tile(D), tile(D), tile(DR), tile(DH),
                  pl.BlockSpec((1, 6, D), lambda i: (mod_row(i), 0, 0)),
                  _resident((2 * D, D)), _resident((DR, D)), _resident((DH, D)), _resident((D, D)),
                  _resident((1, D)), _resident((D, 2 * LANES)), _resident((1, LANES))],
        out_specs=[tile(D), pl.BlockSpec((TS, XW), lambda i: (i, 0)), tile(LANES),
                   pl.BlockSpec((1, SUBLANES, LANES), lambda i: (i, 0, 0))],
        out_shape=[jax.ShapeDtypeStruct((n, D), F32), jax.ShapeDtypeStruct((n_tiles * TS, XW), BF),
                   jax.ShapeDtypeStruct((n, LANES), F32),
                   jax.ShapeDtypeStruct((n_tiles, SUBLANES, LANES), F32)],
        compiler_params=_cparams(("arbitrary",)),
        name="merge_out",
    )(x, h, yr, ya, mod, wts["w_gate"], wts["w_o_rnn"], wts["w_o_mla"], wts["w_out"],
      wts["n2"], wts["router_w"], wts["router_b"])


def _run_copies(local_ref, far_ref, len_ref, tile, make_copy, action):
    for g in range(NG):
        idx = tile * NG + g
        n = len_ref[idx]
        local0 = local_ref[idx]
        far0 = far_ref[idx]
        for k in reversed(range(RUN_ALIGN.bit_length() - 1, TS.bit_length())):
            size = 1 << k
            done = (n >> (k + 1)) << (k + 1)

            @pl.when(((n >> k) & 1) == 1)
            def _():
                action(make_copy(pl.multiple_of(local0 + done, RUN_ALIGN),
                                 pl.multiple_of(far0 + done, RUN_ALIGN), size))


def _dispatch_kernel(local_ref, far_ref, len_ref, hx_ref, xs_in_ref, xs_ref, sem):
    del xs_in_ref

    def copy(src, dst, size):
        return pltpu.make_async_copy(hx_ref.at[pl.ds(src, size)], xs_ref.at[pl.ds(dst, size)], sem)

    tile = pl.program_id(0)
    _run_copies(local_ref, far_ref, len_ref, tile, copy, lambda c: c.start())
    _run_copies(local_ref, far_ref, len_ref, tile, copy, lambda c: c.wait())


def _dispatch(tables, hx, xs):
    n_tiles = hx.shape[0] // TS
    return pl.pallas_call(
        _dispatch_kernel,
        grid_spec=pltpu.PrefetchScalarGridSpec(
            num_scalar_prefetch=3,
            grid=(n_tiles,),
            in_specs=[pl.BlockSpec((TS, XW), lambda i, *_: (i, 0)),
                      pl.BlockSpec(memory_space=pl.ANY)],
            out_specs=pl.BlockSpec(memory_space=pl.ANY),
            scratch_shapes=[pltpu.SemaphoreType.DMA(())],
        ),
        out_shape=jax.ShapeDtypeStruct(xs.shape, xs.dtype),
        input_output_aliases={4: 0},
        compiler_params=_cparams(("arbitrary",)),
        name="dispatch",
    )(*tables, hx, xs)


def _moe_kernel(tb_ref, tg_ref, nt_ref, xs_ref, w1_ref, w3_ref, w2_ref, o_ref, acc_ref):
    j = pl.program_id(0)
    pair = pl.program_id(1)

    @pl.when(j < nt_ref[0])
    def _():
        xt = xs_ref[:, :D]
        cmb = (xs_ref[:, D:D + LANES].astype(F32) + xs_ref[:, D + LANES:D + 2 * LANES].astype(F32)
               + xs_ref[:, D + 2 * LANES:].astype(F32))
        lane = lax.broadcasted_iota(jnp.int32, cmb.shape, 1)
        first = tg_ref[j] * EPG + pair * EXPERTS_PER_STEP
        hidden = []
        for u in range(EXPERTS_PER_STEP):
            a = _dot(xt, w1_ref[u].astype(BF))
            he = (a * _sigmoid(a)) * _dot(xt, w3_ref[u].astype(BF))
            ce = jnp.sum(jnp.where(lane == first + u, cmb, 0.0), axis=-1, keepdims=True)
            hidden.append((he * ce).astype(BF))
        w2 = w2_ref[...].reshape(EXPERTS_PER_STEP * DE, D).astype(BF)
        y = _dot(jnp.concatenate(hidden, axis=1), w2)

        @pl.when(pair == 0)
        def _():
            acc_ref[...] = y

        @pl.when(pair == EPG // EXPERTS_PER_STEP - 1)
        def _():
            o_ref[...] = (acc_ref[...] + y).astype(BF)

    @pl.when((j >= nt_ref[0]) & (pair == EPG // EXPERTS_PER_STEP - 1))
    def _():
        o_ref[...] = jnp.zeros_like(o_ref)


def _experts(tile_block, tile_group, n_tiles, xs, wts):
    m = xs.shape[0]
    steps = EPG // EXPERTS_PER_STEP
    assert steps == 2, "the kernel keeps one partial sum: first step stores it, second adds and writes"

    def w_idx(j, e, tb, tg, nt):
        return (tg[j] * steps + jnp.where(j < nt[0], e, steps - 1), 0, 0)

    return pl.pallas_call(
        _moe_kernel,
        grid_spec=pltpu.PrefetchScalarGridSpec(
            num_scalar_prefetch=3,
            grid=(m // TMOE, steps),
            in_specs=[pl.BlockSpec((TMOE, XW), lambda j, e, tb, tg, nt: (tb[j], 0)),
                      pl.BlockSpec((EXPERTS_PER_STEP, D, DE), w_idx),
                      pl.BlockSpec((EXPERTS_PER_STEP, D, DE), w_idx),
                      pl.BlockSpec((EXPERTS_PER_STEP, DE, D), w_idx)],
            out_specs=pl.BlockSpec((TMOE, D), lambda j, e, tb, tg, nt: (j, 0)),
            scratch_shapes=[pltpu.VMEM((TMOE, D), F32)],
        ),
        out_shape=jax.ShapeDtypeStruct((m, D), BF),
        compiler_params=_cparams(("arbitrary", "arbitrary")),
        name="experts",
    )(tile_block, tile_group, n_tiles, xs, wts["exp_w1"], wts["exp_w3"], wts["exp_w2"])


def _combine_kernel(local_ref, far_ref, len_ref, x1_ref, rinfo_ref, mod_ref, ys_ref, o_ref,
                    buf_ref, sem, *, n_steps):
    i = pl.program_id(0)

    def runs(step, slot, action):
        def copy(dst, src, size):
            return pltpu.make_async_copy(ys_ref.at[pl.ds(src, size)],
                                         buf_ref.at[slot, pl.ds(dst, size)], sem.at[slot])

        _run_copies(local_ref, far_ref, len_ref, step, copy, action)

    @pl.when(i == 0)
    def _():
        buf_ref[...] = jnp.zeros_like(buf_ref)
        runs(0, 0, lambda c: c.start())

    @pl.when(i + 1 < n_steps)
    def _():
        runs(i + 1, (i + 1) % 2, lambda c: c.start())

    slot = i % 2
    runs(i, slot, lambda c: c.wait())
    lpos = rinfo_ref[:, 1:2]
    s_i = lax.broadcasted_iota(jnp.int32, (TM, TS), 1)
    from_sorted = jnp.where(s_i.astype(F32) == lpos, 1.0, 0.0).astype(BF)
    moe = _dot(from_sorted, buf_ref[slot])
    o_ref[...] = x1_ref[...] + mod_ref[0, 5:6, :] * moe


def _combine(tables, x1, rinfo, mod, mod_row, ys):
    n = x1.shape[0]
    n_steps = n // TM
    return pl.pallas_call(
        functools.partial(_combine_kernel, n_steps=n_steps),
        grid_spec=pltpu.PrefetchScalarGridSpec(
            num_scalar_prefetch=3,
            grid=(n_steps,),
            in_specs=[pl.BlockSpec((TM, D), lambda i, *_: (i, 0)),
                      pl.BlockSpec((TM, LANES), lambda i, *_: (i, 0)),
                      pl.BlockSpec((1, 6, D), lambda i, *_: (mod_row(i), 0, 0)),
                      pl.BlockSpec(memory_space=pl.ANY)],
            out_specs=pl.BlockSpec((TM, D), lambda i, *_: (i, 0)),
            scratch_shapes=[pltpu.VMEM((2, TS, D), BF), pltpu.SemaphoreType.DMA((2,))],
        ),
        out_shape=jax.ShapeDtypeStruct((n, D), F32),
        compiler_params=_cparams(("arbitrary",)),
        name="combine",
    )(*tables, x1, rinfo, mod, ys)


def _run_lengths(tile_counts):
    counts = tile_counts[:, 0, :NG].astype(jnp.int32)
    return ((counts + RUN_ALIGN - 1) // RUN_ALIGN) * RUN_ALIGN


def _run_tables(lengths, first_far):
    local = jnp.cumsum(lengths, axis=1) - lengths
    far = first_far[None, :] + jnp.cumsum(lengths, axis=0) - lengths
    flat = lambda a: a.astype(jnp.int32).reshape(-1)
    return flat(local), flat(far), flat(lengths)


def _group_layout(counts, max_tiles):
    padded = ((counts + TMOE - 1) // TMOE) * TMOE
    ends = jnp.cumsum(padded)
    offsets = ends - padded
    n_tiles = (ends[-1] // TMOE).astype(jnp.int32)
    tile = jnp.minimum(jnp.arange(max_tiles, dtype=jnp.int32), jnp.maximum(n_tiles - 1, 0))
    tile_group = jnp.sum((tile[:, None] * TMOE >= ends[None, :]).astype(jnp.int32), axis=1)
    return offsets, tile, tile_group, n_tiles.reshape(1)


def _pad_heads(w, perm=None, rotary_only=False):
    lead = w.shape[:-1]
    per = w.shape[-1] // NH
    w = w.reshape(lead + (NH, per))
    if perm is not None:
        nope = jnp.zeros_like(w[..., :NOPE]) if rotary_only else w[..., :NOPE]
        w = jnp.concatenate([nope, w[..., NOPE:][..., perm]], axis=-1)
    w = jnp.pad(w, [(0, 0)] * len(lead) + [(0, 0), (0, HP - per)])
    return w.reshape(lead + (NH * HP,))


def _pad_gain(g, perm, rotary_only=False):
    nope = jnp.zeros((NOPE,), F32) if rotary_only else g[:NOPE]
    g = jnp.concatenate([nope, g[NOPE:][perm], jnp.zeros((HP - QK,), F32)])
    return g.reshape(1, HP)


def _block_diag(w):
    per = BDW // LRU_BLOCK
    w = w.reshape(DR // BDW, per, LRU_BLOCK, LRU_BLOCK)
    bd = jnp.einsum("jpab,pq->jpaqb", w, jnp.eye(per, dtype=w.dtype))
    return bd.reshape(DR // BDW, BDW, BDW)


def _prepare_shared(l, p):
    w_in = lambda a, b: jnp.transpose(lax.slice(p["w_in"], (l, 0, a), (l + 1, D, b)).reshape(D, b - a))
    o1, o2, o3, o4, o5 = DR, 2 * DR, 2 * DR + QL, 2 * DR + QL + KVL, 2 * DR + QL + KVL + ROPE
    bd = jnp.stack([_block_diag(p["lru_wa"][l, 0]), _block_diag(p["lru_wx"][l, 0]),
                    _block_diag(p["lru_wa"][l, 1]), _block_diag(p["lru_wx"][l, 1])])
    bd = (0.5 * bd).astype(BF)
    wom = p["w_o_mla"][l].reshape(NH, VD, D)
    wom = jnp.pad(wom, ((0, 0), (0, HP - VD), (0, 0))).reshape(DH, D)
    router_w = jnp.concatenate([p["router_we"][l], p["router_wg"][l],
                                jnp.zeros((D, LANES - NE - NG), F32)], axis=1)
    router_b = jnp.concatenate([p["router_be"][l], p["router_bg"][l],
                                jnp.zeros((LANES - NE - NG,), F32)]).reshape(1, LANES)
    router_hi = router_w.astype(BF)
    router_lo = (router_w - router_hi.astype(F32)).astype(BF)
    router_w = jnp.concatenate([router_hi, router_lo], axis=1)
    return {
        "n1": p["norm1_g"][l].reshape(1, D), "n2": p["norm2_g"][l].reshape(1, D),
        "w_x": w_in(0, o1).astype(BF), "w_g": w_in(o1, o2).astype(BF),
        "w_q": w_in(o2, o3).astype(BF), "w_kv": w_in(o3, o4), "w_kr": w_in(o4, o5),
        "w_gate": (0.5 * w_in(o5, o5 + 2 * D)).astype(BF),
        "qan": p["q_a_norm"][l].reshape(1, QL), "kvan": p["kv_a_norm"][l].reshape(1, KVL),
        "w_uk": _pad_heads(p["w_uk"][l]).astype(BF),
        "w_uv": _pad_heads(p["w_uv"][l]).astype(BF),
        "conv_w": p["conv_w"][l], "conv_b": p["conv_b"][l].reshape(1, DR), "bd": bd,
        "lru_ba": 0.5 * p["lru_ba"][l], "lru_bx": 0.5 * p["lru_bx"][l], "lru_lam": p["lru_lam"][l],
        "w_o_rnn": p["w_o_rnn"][l].astype(BF), "w_o_mla": wom.astype(BF), "w_out": (0.5 * p["w_out"][l]).astype(BF),
        "router_w": router_w, "router_b": router_b,
        "exp_w1": p["exp_w1"][l], "exp_w3": p["exp_w3"][l], "exp_w2": p["exp_w2"][l],
    }


def _with_rope_order(l, p, shared, perm, rotary):
    w = dict(shared)
    zeros = lambda n: jnp.zeros((n, D), F32)
    rope_block = lambda order: [zeros(NOPE), shared["w_kr"][order, :], zeros(HP - QK)]
    kvr = [shared["w_kv"]] + rope_block(perm)
    w["w_uq"] = _pad_heads(p["w_uq"][l], perm).astype(BF)
    w["gq"] = _pad_gain(p["q_norm"][l], perm)
    w["gk"] = _pad_gain(p["k_norm"][l], perm)
    if rotary:
        pair = np.concatenate([perm[ROPE // 2:], perm[:ROPE // 2]])
        kvr += rope_block(pair)
        w["w_uq_pair"] = _pad_heads(p["w_uq"][l], pair, rotary_only=True).astype(BF)
        w["gq_pair"] = _pad_gain(p["q_norm"][l], pair, rotary_only=True)
        w["gk_pair"] = _pad_gain(p["k_norm"][l], pair, rotary_only=True)
    w["w_kvr"] = jnp.concatenate(kvr, axis=0).astype(BF)
    return w


def _rope_tables(n_tokens, perm):
    rows = n_tokens // GRID_W
    row = np.repeat(np.arange(rows), GRID_W).astype(np.float32)
    col = np.tile(np.arange(GRID_W), rows).astype(np.float32)
    axis_dim = ROPE // 2
    inv = (np.float32(ROPE_BASE) ** (-np.arange(0, axis_dim, 2, dtype=np.float32) / axis_dim)).astype(np.float32)
    ang = np.concatenate([row[:, None] * inv, col[:, None] * inv], axis=-1).astype(np.float32)
    cos, sin = np.cos(ang), np.sin(ang)
    ones = lambda n: np.ones((n_tokens, n), np.float32)
    zeros = lambda n: np.zeros((n_tokens, n), np.float32)
    cos_t = np.concatenate([ones(NOPE), cos, cos, ones(HP - QK)], axis=1)
    sin_t = np.concatenate([zeros(NOPE), -sin, sin, zeros(HP - QK)], axis=1)
    return jnp.asarray(cos_t, F32), jnp.asarray(sin_t, F32)


def kernel(x_prompt, x_sample, cache_mla_ckv, cache_mla_krope, state_rglru, c, c_ctx, norm1_g, norm2_g, w_mod, b_mod, w_in, conv_w, conv_b, lru_wa, lru_ba, lru_wx, lru_bx, lru_lam, q_a_norm, kv_a_norm, w_uq, w_uk, w_uv, q_norm, k_norm, w_o_rnn, w_o_mla, w_out, router_wg, router_bg, router_we, router_be, exp_w1, exp_w3, exp_w2):
    p = dict(norm1_g=norm1_g, norm2_g=norm2_g, w_in=w_in, conv_w=conv_w, conv_b=conv_b,
             lru_wa=lru_wa, lru_ba=lru_ba, lru_wx=lru_wx, lru_bx=lru_bx, lru_lam=lru_lam,
             q_a_norm=q_a_norm, kv_a_norm=kv_a_norm, w_uq=w_uq, w_uk=w_uk, w_uv=w_uv,
             q_norm=q_norm, k_norm=k_norm, w_o_rnn=w_o_rnn, w_o_mla=w_o_mla, w_out=w_out,
             router_wg=router_wg, router_bg=router_bg, router_we=router_we, router_be=router_be,
             exp_w1=exp_w1, exp_w3=exp_w3, exp_w2=exp_w2)
    depth = w_in.shape[0]
    nb, seq, _ = x_prompt.shape
    db, dseq, _ = x_sample.shape
    ident = np.arange(ROPE)
    halves = np.concatenate([np.arange(0, ROPE, 2), np.arange(1, ROPE, 2)])
    rope_tabs = _rope_tables(dseq, halves)
    cond8 = jnp.concatenate([c_ctx[None, :], c, jnp.zeros((SUBLANES - 1 - db, D), F32)], axis=0)
    ctx_row = lambda tile_rows: (lambda i: 0)
    lat_row = lambda tile_rows: (lambda i: (i * tile_rows) // dseq + 1)
    n_ctx, n_lat = nb * seq, db * dseq
    run_padding = ((n_ctx + n_lat) // TM) * NG * (RUN_ALIGN - 1)
    max_tiles = -(-(n_ctx + n_lat + run_padding) // TMOE) + NG
    per_seq = lambda arrs, b, t: [a.reshape(b, t, a.shape[-1]) for a in arrs]
    flat = lambda a: a.reshape(-1, a.shape[-1])

    y_prompt, y_sample = x_prompt.reshape(n_ctx, D), x_sample.reshape(n_lat, D)
    ckv_list, krope_list, rnn_list = [], [], []
    for l in range(depth):
        shared = _prepare_shared(l, p)
        w_ctx = _with_rope_order(l, p, shared, ident, False)
        w_lat = _with_rope_order(l, p, shared, halves, True)
        mod = _modulation(cond8, w_mod[l], b_mod[l]).reshape(SUBLANES, 6, D)

        h, xr, gg, q, k, v, ckv, kro = _projections(y_prompt, mod, ctx_row(TM), w_ctx, None, True)
        xr, gg, q, k, v = per_seq([xr, gg, q, k, v], nb, seq)
        yr, h_fin = _rglru(xr, gg, shared, None, True)
        ya = _attention(q, k, v, None, None, NH)
        x1_c, hx_c, ri_c, tc_c = _merge_out(y_prompt, h, flat(yr), flat(ya), mod, ctx_row(TM), shared)
        ckv_list.append(ckv.reshape(nb, seq, KVL))
        krope_list.append(kro.reshape(nb, seq, ROPE))
        rnn_list.append(h_fin)

        krp_cache = jnp.pad(cache_mla_krope[:, l][..., halves], ((0, 0), (0, 0), (NOPE, HP - QK)))
        kc, vc = _cache_keys_values(cache_mla_ckv[:, l], krp_cache, w_lat)
        h, xr, gg, q, k, v = _projections(y_sample, mod, lat_row(TM), w_lat, rope_tabs, False)
        xr, gg, q, k, v = per_seq([xr, gg, q, k, v], db, dseq)
        yr, _ = _rglru(xr, gg, shared, state_rglru[:, l], False)
        ya = _attention(q, k, v, kc, vc, 2)
        x1_l, hx_l, ri_l, tc_l = _merge_out(y_sample, h, flat(yr), flat(ya), mod, lat_row(TM), shared)

        len_c, len_l = _run_lengths(tc_c), _run_lengths(tc_l)
        rows_c = jnp.sum(len_c, axis=0)
        offsets, tile_block, tile_group, n_tiles = _group_layout(rows_c + jnp.sum(len_l, axis=0), max_tiles)
        runs_c = _run_tables(len_c, offsets)
        runs_l = _run_tables(len_l, offsets + rows_c)
        xs = jnp.zeros((max_tiles * TMOE, XW), BF)
        xs = _dispatch(runs_c, hx_c, xs)
        xs = _dispatch(runs_l, hx_l, xs)
        ys = _experts(tile_block, tile_group, n_tiles, xs, shared)
        y_prompt = _combine(runs_c, x1_c, ri_c, mod, ctx_row(TM), ys)
        y_sample = _combine(runs_l, x1_l, ri_l, mod, lat_row(TM), ys)

    y_prompt, y_sample = y_prompt.reshape(nb, seq, D), y_sample.reshape(db, dseq, D)

    return (y_prompt, y_sample, jnp.stack(ckv_list, axis=1), jnp.stack(krope_list, axis=1),
            jnp.stack(rnn_list, axis=1))
```

```python
import functools
import math

import numpy as np
import jax
import jax.numpy as jnp
from jax import lax
from jax.experimental import pallas as pl
from jax.experimental.pallas import tpu as pltpu

D = 1024
DR = 1024
QL = 384
KVL = 256
NH = 8
NOPE = 64
ROPE = 32
QK = NOPE + ROPE
VD = 64
HP = 128
DH = NH * HP
GRID_W = 64
ROPE_BASE = 10000.0
EPS = 1e-6
TINY = 1e-30
LRU_C = 8.0
LRU_BLOCK = 64
BDW = 256
CH = 1024
NG = 4
EPG = 4
NE = NG * EPG
DE = 512
LANES = 128
SUBLANES = 8
TM = 512
TS = 640
TMOE = 512
EXPERTS_PER_STEP = 2
XW = D + 3 * LANES
RUN_ALIGN = 16
VMEM_LIMIT = 52 * 1024 * 1024
BF = jnp.bfloat16
F32 = jnp.float32


def _cparams(sem):
    return pltpu.CompilerParams(dimension_semantics=sem, vmem_limit_bytes=VMEM_LIMIT)


def _dot(a, b):
    return jnp.dot(a, b, preferred_element_type=F32)


def _dot_nt(a, b):
    return lax.dot_general(a, b, (((1,), (1,)), ((), ())), preferred_element_type=F32)


def _rms(x, g, width):
    ms = jnp.sum(x * x, axis=-1, keepdims=True) * (1.0 / width)
    return x * lax.rsqrt(ms + EPS) * g


def _modulated_norm(x, g, scale, shift):
    return _rms(x, g * (1.0 + scale), D) + shift


def _mod_kernel(c_ref, w_ref, b_ref, o_ref):
    c = c_ref[...]
    s = c * jax.nn.sigmoid(c)
    o_ref[...] = _dot(s, w_ref[...]) + b_ref[...]


def _modulation(cond8, w_mod, b_mod):
    n = w_mod.shape[1]
    return pl.pallas_call(
        _mod_kernel,
        grid=(n // D,),
        in_specs=[
            pl.BlockSpec((SUBLANES, D), lambda j: (0, 0)),
            pl.BlockSpec((D, D), lambda j: (0, j)),
            pl.BlockSpec((1, D), lambda j: (0, j)),
        ],
        out_specs=pl.BlockSpec((SUBLANES, D), lambda j: (0, j)),
        out_shape=jax.ShapeDtypeStruct((SUBLANES, n), F32),
        compiler_params=_cparams(("arbitrary",)),
        name="modulation",
    )(cond8, w_mod, b_mod.reshape(1, n))


def _head_norm(xh, gain, cos=None, partner_scaled=None):
    ms = jnp.sum(xh * xh, axis=-1, keepdims=True) * (1.0 / QK)
    rs = lax.rsqrt(ms + EPS)
    y = xh * rs * gain
    if cos is None:
        return y
    return y * cos + partner_scaled * rs


def _keys_values(ckv, krp, wuk_ref, wuv_ref, gk, cos, partner_scaled, k_ref, v_ref):
    cb = ckv.astype(BF)
    kn = _dot(cb, wuk_ref[...])
    v_ref[...] = _dot(cb, wuv_ref[...]).astype(BF)
    for h in range(NH):
        kh = kn[:, h * HP:(h + 1) * HP] + krp
        k_ref[:, h * HP:(h + 1) * HP] = _head_norm(kh, gk, cos, partner_scaled).astype(BF)


def _proj_kernel(*refs, rope, emit_cache):
    it = iter(refs)
    x_ref, mod_ref, n1_ref = next(it), next(it), next(it)
    wx_ref, wg_ref, wq_ref, wkvr_ref = next(it), next(it), next(it), next(it)
    qan_ref, kvan_ref, wuq_ref, gq_ref = next(it), next(it), next(it), next(it)
    wuk_ref, gk_ref, wuv_ref = next(it), next(it), next(it)
    if rope:
        wuqs_ref, gqs_ref, gks_ref, cos_ref, sins_ref = (next(it) for _ in range(5))
    h_ref, xr_ref, gg_ref, q_ref, k_ref, v_ref = (next(it) for _ in range(6))
    if emit_cache:
        ckv_ref, kro_ref = next(it), next(it)

    hb = _modulated_norm(x_ref[...], n1_ref[...], mod_ref[0, 1:2, :], mod_ref[0, 0:1, :]).astype(BF)
    h_ref[...] = hb
    xr_ref[...] = _dot_nt(hb, wx_ref[...]).astype(BF)
    gg_ref[...] = jax.nn.gelu(_dot_nt(hb, wg_ref[...])).astype(BF)

    qnb = _rms(_dot_nt(hb, wq_ref[...]), qan_ref[...], QL).astype(BF)
    q = _dot(qnb, wuq_ref[...])
    gq = gq_ref[...]
    cos = q_partner = q_pair_scale = None
    if rope:
        cos, sins = cos_ref[...], sins_ref[...]
        q_partner = _dot(qnb, wuqs_ref[...])
        q_pair_scale = gqs_ref[...] * sins
    for hd in range(NH):
        cols = slice(hd * HP, (hd + 1) * HP)
        partner = q_partner[:, cols] * q_pair_scale if rope else None
        q_ref[:, cols] = _head_norm(q[:, cols], gq, cos, partner).astype(BF)

    kvr = _dot_nt(hb, wkvr_ref[...])
    ckv = _rms(kvr[:, :KVL], kvan_ref[...], KVL)
    krp = kvr[:, KVL:KVL + HP]
    k_partner = kvr[:, KVL + HP:KVL + 2 * HP] * (gks_ref[...] * sins) if rope else None
    if emit_cache:
        ckv_ref[...] = ckv
        kro_ref[...] = krp[:, NOPE:NOPE + ROPE]
    _keys_values(ckv, krp, wuk_ref, wuv_ref, gk_ref[...], cos, k_partner, k_ref, v_ref)


def _resident(shape):
    return pl.BlockSpec(shape, lambda i: (0,) * len(shape), pipeline_mode=pl.Buffered(1))


def _projections(x, mod, mod_row, wts, rope_tabs, emit_cache):
    n = x.shape[0]
    rope = rope_tabs is not None
    tile = lambda w: pl.BlockSpec((TM, w), lambda i: (i, 0))
    in_specs = [
        tile(D),
        pl.BlockSpec((1, 6, D), lambda i: (mod_row(i), 0, 0)),
        _resident((1, D)),
        _resident((DR, D)), _resident((DR, D)), _resident((QL, D)), _resident(wts["w_kvr"].shape),
        _resident((1, QL)), _resident((1, KVL)), _resident((QL, DH)), _resident((1, HP)),
        _resident((KVL, DH)), _resident((1, HP)), _resident((KVL, DH)),
    ]
    args = [x, mod, wts["n1"], wts["w_x"], wts["w_g"], wts["w_q"], wts["w_kvr"],
            wts["qan"], wts["kvan"], wts["w_uq"], wts["gq"], wts["w_uk"], wts["gk"], wts["w_uv"]]
    if rope:
        tiles_per_seq = rope_tabs[0].shape[0] // TM
        in_specs += [_resident((QL, DH)), _resident((1, HP)), _resident((1, HP))]
        in_specs += [pl.BlockSpec((TM, HP), lambda i: (i % tiles_per_seq, 0))] * 2
        args += [wts["w_uq_pair"], wts["gq_pair"], wts["gk_pair"]] + list(rope_tabs)
    out_specs = [tile(D), tile(DR), tile(DR), tile(DH), tile(DH), tile(DH)]
    out_shape = [jax.ShapeDtypeStruct((n, D), BF), jax.ShapeDtypeStruct((n, DR), BF),
                 jax.ShapeDtypeStruct((n, DR), BF), jax.ShapeDtypeStruct((n, DH), BF),
                 jax.ShapeDtypeStruct((n, DH), BF), jax.ShapeDtypeStruct((n, DH), BF)]
    if emit_cache:
        out_specs += [tile(KVL), tile(ROPE)]
        out_shape += [jax.ShapeDtypeStruct((n, KVL), F32), jax.ShapeDtypeStruct((n, ROPE), F32)]
    return pl.pallas_call(
        functools.partial(_proj_kernel, rope=rope, emit_cache=emit_cache),
        grid=(n // TM,),
        in_specs=in_specs,
        out_specs=out_specs,
        out_shape=out_shape,
        compiler_params=_cparams(("arbitrary",)),
        name="projections",
    )(*args)


def _cache_kv_kernel(ckv_ref, krp_ref, wuk_ref, gk_ref, wuv_ref, k_ref, v_ref):
    _keys_values(ckv_ref[0], krp_ref[0], wuk_ref, wuv_ref, gk_ref[...], None, None,
                 k_ref.at[0], v_ref.at[0])


def _cache_keys_values(ckv, krp, wts):
    b, s, _ = ckv.shape
    full = lambda shape: pl.BlockSpec(shape, lambda i: (0,) * len(shape))
    return pl.pallas_call(
        _cache_kv_kernel,
        grid=(b,),
        in_specs=[pl.BlockSpec((1, s, KVL), lambda i: (i, 0, 0)),
                  pl.BlockSpec((1, s, HP), lambda i: (i, 0, 0)),
                  full((KVL, DH)), full((1, HP)), full((KVL, DH))],
        out_specs=[pl.BlockSpec((1, s, DH), lambda i: (i, 0, 0))] * 2,
        out_shape=[jax.ShapeDtypeStruct((b, s, DH), BF)] * 2,
        compiler_params=_cparams(("arbitrary",)),
        name="cache_keys_values",
    )(ckv, krp, wts["w_uk"], wts["gk"], wts["w_uv"])


WIN = 256
SEG = WIN // 8


def _window_permutation():
    dst = np.arange(WIN)
    p = np.zeros((WIN, WIN), np.float32)
    p[dst, (dst % SUBLANES) * SEG + dst // SUBLANES] = 1.0
    return p


def _sigmoid(x):
    return 0.5 * jnp.tanh(0.5 * x) + 0.5


def _segment_pass(a_scr, b_scr, bases, inits, out_scr=None):
    def body(k, carry):
        new = []
        for d in range(2):
            h, p = carry[d]
            i = k if d == 0 else SEG - 1 - k
            rows = pl.ds(pl.multiple_of(bases[d] + i * SUBLANES, SUBLANES), SUBLANES)
            a = a_scr[d, rows, :]
            h = a * h + b_scr[d, rows, :]
            if out_scr is None:
                p = a * p
            else:
                out_scr[d, rows, :] = h
            new.append((h, p))
        return tuple(new)

    init = tuple((inits[d], jnp.ones_like(inits[d])) for d in range(2))
    return lax.fori_loop(0, SEG, body, init, unroll=4)


def _segment_entries(end, decay, carry_in, forward):
    order = range(SUBLANES) if forward else reversed(range(SUBLANES))
    rows = [None] * SUBLANES
    c = carry_in
    for s in order:
        rows[s] = c
        c = end[s:s + 1, :] + decay[s:s + 1, :] * c
    return jnp.concatenate(rows, axis=0), c


def _rglru_kernel(*refs, t, has_h0, emit_state):
    it = iter(refs)
    xr_ref, gg_ref, perm_ref, unperm_ref = (next(it) for _ in range(4))
    cw_ref, cb_ref, bd_ref, ba_ref, bx_ref, lam_ref = (next(it) for _ in range(6))
    h0_ref = next(it) if has_h0 else None
    y_ref = next(it)
    hf_ref = next(it) if emit_state else None
    a_scr, b_scr, h_scr = (next(it) for _ in range(3))

    n_win = t // WIN
    sub = lax.broadcasted_iota(jnp.int32, (SUBLANES, CH), 0)
    zero_row = jnp.zeros((1, CH), F32)
    edge = 2 * SUBLANES
    for w in range(n_win):
        lo, hi = w * WIN, (w + 1) * WIN
        xp = _dot(perm_ref[...], xr_ref[0, lo:hi, :])
        before = xr_ref[0, lo - edge:lo, :].astype(F32)[edge - 1:edge, :] if w > 0 else zero_row
        after = xr_ref[0, hi:hi + edge, :].astype(F32) if w < n_win - 1 else None
        after0 = after[0:1, :] if after is not None else zero_row
        after1 = after[1:2, :] if after is not None else zero_row
        tile_m1 = jnp.where(sub == 0, before, pltpu.roll(xp[WIN - SUBLANES:WIN, :], 1, 0))
        tile_p0 = jnp.where(sub == SUBLANES - 1, after0, pltpu.roll(xp[0:SUBLANES, :], SUBLANES - 1, 0))
        tile_p1 = jnp.where(sub == SUBLANES - 1, after1,
                            pltpu.roll(xp[SUBLANES:2 * SUBLANES, :], SUBLANES - 1, 0))
        xe = jnp.concatenate([tile_m1, xp, tile_p0, tile_p1], axis=0)
        xc = cb_ref[...]
        for tap in range(4):
            xc = xc + xe[tap * SUBLANES:tap * SUBLANES + WIN, :] * cw_ref[tap:tap + 1, :]
        for s in range(CH // BDW):
            cols = slice(s * BDW, (s + 1) * BDW)
            xs = xc[:, cols]
            xsb = xs.astype(BF)
            xh = 0.5 * xs
            for d in range(2):
                tr = jnp.tanh(_dot(xsb, bd_ref[2 * d, s]) + ba_ref[d:d + 1, cols])
                ti = jnp.tanh(_dot(xsb, bd_ref[2 * d + 1, s]) + bx_ref[d:d + 1, cols])
                nl = -lam_ref[d:d + 1, cols]
                softplus = jnp.maximum(nl, 0.0) + jnp.log(1.0 + jnp.exp(-jnp.abs(nl)))
                ch = (-0.5 * LRU_C) * softplus
                a = jnp.exp(tr * ch + ch)
                z = 1.0 - a * a
                root = z * lax.rsqrt(jnp.maximum(z, TINY))
                a_scr[d, lo:hi, cols] = a
                b_scr[d, lo:hi, cols] = root * (ti * xh + xh)

    if has_h0:
        carry = [h0_ref[0, 0:1, :], h0_ref[0, 1:2, :]]
    else:
        carry = [zero_row, zero_row]
    zeros = jnp.zeros((SUBLANES, CH), F32)
    for k in range(n_win):
        bases = (k * WIN, (n_win - 1 - k) * WIN)
        totals = _segment_pass(a_scr, b_scr, bases, (zeros, zeros))
        entries = []
        for d in range(2):
            entry, carry[d] = _segment_entries(totals[d][0], totals[d][1], carry[d], d == 0)
            entries.append(entry)
        _segment_pass(a_scr, b_scr, bases, entries, out_scr=h_scr)

    if emit_state:
        hf_ref[0, 0:1, :] = carry[0]
        hf_ref[0, 1:2, :] = carry[1]
    for w in range(n_win):
        lo, hi = w * WIN, (w + 1) * WIN
        gate = _dot(perm_ref[...], gg_ref[0, lo:hi, :])
        yp = ((h_scr[0, lo:hi, :] + h_scr[1, lo:hi, :]) * gate).astype(BF)
        y_ref[0, lo:hi, :] = _dot(unperm_ref[...], yp).astype(BF)


def _rglru(xr, gg, wts, h0, emit_state):
    b, t, _ = xr.shape
    nc = DR // CH
    has_h0 = h0 is not None
    chunk = lambda r: pl.BlockSpec((r, CH), lambda i, j: (0, j))
    seq = pl.BlockSpec((1, t, CH), lambda i, j: (i, 0, j))
    state = pl.BlockSpec((1, 2, CH), lambda i, j: (i, 0, j))
    window = pl.BlockSpec((WIN, WIN), lambda i, j: (0, 0))
    perm = _window_permutation()
    in_specs = [seq, seq, window, window, chunk(4), chunk(1),
                pl.BlockSpec((4, CH // BDW, BDW, BDW), lambda i, j: (0, j, 0, 0)),
                chunk(2), chunk(2), chunk(2)]
    args = [xr, gg, jnp.asarray(perm, BF), jnp.asarray(perm.T, BF),
            wts["conv_w"], wts["conv_b"], wts["bd"], wts["lru_ba"], wts["lru_bx"], wts["lru_lam"]]
    if has_h0:
        in_specs.append(state)
        args.append(h0)
    out_specs = [seq]
    out_shape = [jax.ShapeDtypeStruct((b, t, DR), BF)]
    if emit_state:
        out_specs.append(state)
        out_shape.append(jax.ShapeDtypeStruct((b, 2, DR), F32))
    res = pl.pallas_call(
        functools.partial(_rglru_kernel, t=t, has_h0=has_h0, emit_state=emit_state),
        grid=(b, nc),
        in_specs=in_specs,
        out_specs=out_specs,
        out_shape=out_shape,
        scratch_shapes=[pltpu.VMEM((2, t, CH), F32)] * 3,
        compiler_params=_cparams(("arbitrary", "arbitrary")),
        name="rglru",
    )(*args)
    return res if emit_state else (res[0], None)


def _attn_kernel(*refs, t, n_seqs, n_heads, has_ctx, q_block):
    it = iter(refs)
    q_ref, k_ref, v_ref = next(it), next(it), next(it)
    kc_ref = vc_ref = None
    if has_ctx:
        kc_ref, vc_ref = next(it), next(it)
    o_ref = next(it)
    log2_scale = (QK ** -0.5) * math.log2(math.e)
    for sq, hd in [(sq, hd) for sq in range(n_seqs) for hd in range(n_heads)]:
        cols = slice(hd * HP, (hd + 1) * HP)
        k = k_ref[sq, :, cols]
        v = v_ref[sq, :, cols]
        if has_ctx:
            kc = kc_ref[sq, :, cols]
            vc = vc_ref[sq, :, cols]
        for qb in range(t // q_block):
            rows = slice(qb * q_block, (qb + 1) * q_block)
            q = q_ref[sq, rows, cols]
            s = _dot_nt(q, k) * log2_scale
            m = jnp.max(s, axis=-1, keepdims=True)
            if has_ctx:
                sc = _dot_nt(q, kc) * log2_scale
                m = jnp.maximum(m, jnp.max(sc, axis=-1, keepdims=True))
            p = jnp.exp2(s - m)
            den = jnp.sum(p, axis=-1, keepdims=True)
            o = _dot(p.astype(BF), v)
            if has_ctx:
                pc = jnp.exp2(sc - m)
                den = den + jnp.sum(pc, axis=-1, keepdims=True)
                o = o + _dot(pc.astype(BF), vc)
            o_ref[sq, rows, cols] = (o / den).astype(BF)


def _attention(q, k, v, kc, vc, seqs_per_step, heads_per_step):
    b, t, _ = q.shape
    has_ctx = kc is not None
    w = heads_per_step * HP
    blk = lambda n: pl.BlockSpec((seqs_per_step, n, w), lambda i, j: (i, 0, j))
    in_specs = [blk(t), blk(t), blk(t)]
    args = [q, k, v]
    if has_ctx:
        in_specs += [blk(kc.shape[1])] * 2
        args += [kc, vc]
    return pl.pallas_call(
        functools.partial(_attn_kernel, t=t, n_seqs=seqs_per_step, n_heads=heads_per_step,
                          has_ctx=has_ctx, q_block=min(t, 256)),
        grid=(b // seqs_per_step, NH // heads_per_step),
        in_specs=in_specs,
        out_specs=blk(t),
        out_shape=jax.ShapeDtypeStruct((b, t, DH), BF),
        compiler_params=_cparams(("arbitrary", "arbitrary")),
        name="attention",
    )(*args)


def _route(logits):
    lane = lax.broadcasted_iota(jnp.int32, logits.shape, 1)
    lanef = lane.astype(F32)
    neg = -jnp.inf
    big = float(LANES)
    gl = jnp.where((lane >= NE) & (lane < NE + NG), logits, neg)
    gmax = jnp.max(gl, axis=-1, keepdims=True)
    gidx = jnp.min(jnp.where(gl == gmax, lanef, big), axis=-1, keepdims=True) - float(NE)
    gw = 1.0 / jnp.sum(jnp.exp(gl - gmax), axis=-1, keepdims=True)
    lo = gidx * float(EPG)
    el = jnp.where((lanef >= lo) & (lanef < lo + float(EPG)), logits, neg)
    v1 = jnp.max(el, axis=-1, keepdims=True)
    i1 = jnp.min(jnp.where(el == v1, lanef, big), axis=-1, keepdims=True)
    el2 = jnp.where(lanef == i1, neg, el)
    v2 = jnp.max(el2, axis=-1, keepdims=True)
    i2 = jnp.min(jnp.where(el2 == v2, lanef, big), axis=-1, keepdims=True)
    e2 = jnp.exp(v2 - v1)
    w1 = gw / (1.0 + e2)
    w2 = gw * e2 / (1.0 + e2)
    cmb = jnp.where(lanef == i1, w1, 0.0) + jnp.where(lanef == i2, w2, 0.0)
    return cmb, gidx


def _out_kernel(x_ref, h_ref, yr_ref, ya_ref, mod_ref, wgate_ref, wor_ref, wom_ref, wout_ref,
                n2_ref, rw_ref, rb_ref, x1_ref, hx_ref, rinfo_ref, tcnt_ref):
    x = x_ref[...]
    gl = _dot_nt(h_ref[...], wgate_ref[...])
    merged2 = ((jnp.tanh(gl[:, :D]) + 1.0) * _dot(yr_ref[...], wor_ref[...])
               + (jnp.tanh(gl[:, D:]) + 1.0) * _dot(ya_ref[...], wom_ref[...]))
    mix = _dot(merged2.astype(BF), wout_ref[...])
    x1 = x + mod_ref[0, 2:3, :] * mix
    x1_ref[...] = x1
    h2 = _modulated_norm(x1, n2_ref[...], mod_ref[0, 4:5, :], mod_ref[0, 3:4, :])
    h2_hi = h2.astype(BF)
    h2_lo = (h2 - h2_hi.astype(F32)).astype(BF)
    part = _dot(h2_hi, rw_ref[...])
    logits = part[:, :LANES] + part[:, LANES:] + _dot(h2_lo, rw_ref[:, :LANES]) + rb_ref[...]
    cmb, gidx = _route(logits)

    lanef = lax.broadcasted_iota(jnp.int32, cmb.shape, 1).astype(F32)
    ghot = jnp.where(lanef == gidx, 1.0, 0.0)
    r_i = lax.broadcasted_iota(jnp.int32, (TM, TM), 0)
    c_i = lax.broadcasted_iota(jnp.int32, (TM, TM), 1)
    tri = jnp.where(r_i > c_i, 1.0, 0.0).astype(BF)
    earlier_same = jnp.sum(_dot(tri, ghot.astype(BF)) * ghot, axis=-1, keepdims=True)
    counts = jnp.sum(ghot, axis=0, keepdims=True)
    padded = jnp.floor((counts + (RUN_ALIGN - 1.0)) * (1.0 / RUN_ALIGN)) * RUN_ALIGN
    lower_groups = jnp.sum(jnp.where(lanef < gidx, padded, 0.0), axis=-1, keepdims=True)
    lpos = lower_groups + earlier_same
    s_i = lax.broadcasted_iota(jnp.int32, (TM, TS), 1)
    to_sorted = jnp.where(s_i.astype(F32) == lpos, 1.0, 0.0).astype(BF)
    c1 = cmb.astype(BF)
    c2 = (cmb - c1.astype(F32)).astype(BF)
    c3 = (cmb - c1.astype(F32) - c2.astype(F32)).astype(BF)
    payload = jnp.concatenate([h2.astype(BF), c1, c2, c3], axis=1)
    srt = lax.dot_general(to_sorted, payload, (((0,), (0,)), ((), ())), preferred_element_type=F32)
    hx_ref[...] = srt.astype(BF)

    rinfo_ref[...] = jnp.where(lanef == 0.0, gidx, jnp.where(lanef == 1.0, lpos, 0.0))
    tcnt_ref[0] = jnp.broadcast_to(counts, (SUBLANES, LANES))


def _merge_out(x, h, yr, ya, mod, mod_row, wts):
    n = x.shape[0]
    n_tiles = n // TM
    tile = lambda w: pl.BlockSpec((TM, w), lambda i: (i, 0))
    return pl.pallas_call(
        _out_kernel,
        grid=(n_tiles,),
        in_specs=[tile(D), tile(D), tile(DR), tile(DH),
                  pl.BlockSpec((1, 6, D), lambda i: (mod_row(i), 0, 0)),
                  _resident((2 * D, D)), _resident((DR, D)), _resident((DH, D)), _resident((D, D)),
                  _resident((1, D)), _resident((D, 2 * LANES)), _resident((1, LANES))],
        out_specs=[tile(D), pl.BlockSpec((TS, XW), lambda i: (i, 0)), tile(LANES),
                   pl.BlockSpec((1, SUBLANES, LANES), lambda i: (i, 0, 0))],
        out_shape=[jax.ShapeDtypeStruct((n, D), F32), jax.ShapeDtypeStruct((n_tiles * TS, XW), BF),
                   jax.ShapeDtypeStruct((n, LANES), F32),
                   jax.ShapeDtypeStruct((n_tiles, SUBLANES, LANES), F32)],
        compiler_params=_cparams(("arbitrary",)),
        name="merge_out",
    )(x, h, yr, ya, mod, wts["w_gate"], wts["w_o_rnn"], wts["w_o_mla"], wts["w_out"],
      wts["n2"], wts["router_w"], wts["router_b"])


def _run_copies(local_ref, far_ref, len_ref, tile, make_copy, action):
    for g in range(NG):
        idx = tile * NG + g
        n = len_ref[idx]
        local0 = local_ref[idx]
        far0 = far_ref[idx]
        for k in reversed(range(RUN_ALIGN.bit_length() - 1, TS.bit_length())):
            size = 1 << k
            done = (n >> (k + 1)) << (k + 1)

            @pl.when(((n >> k) & 1) == 1)
            def _():
                action(make_copy(pl.multiple_of(local0 + done, RUN_ALIGN),
                                 pl.multiple_of(far0 + done, RUN_ALIGN), size))


def _dispatch_kernel(local_ref, far_ref, len_ref, hx_ref, xs_in_ref, xs_ref, sem):
    del xs_in_ref

    def copy(src, dst, size):
        return pltpu.make_async_copy(hx_ref.at[pl.ds(src, size)], xs_ref.at[pl.ds(dst, size)], sem)

    tile = pl.program_id(0)
    _run_copies(local_ref, far_ref, len_ref, tile, copy, lambda c: c.start())
    _run_copies(local_ref, far_ref, len_ref, tile, copy, lambda c: c.wait())


def _dispatch(tables, hx, xs):
    n_tiles = hx.shape[0] // TS
    return pl.pallas_call(
        _dispatch_kernel,
        grid_spec=pltpu.PrefetchScalarGridSpec(
            num_scalar_prefetch=3,
            grid=(n_tiles,),
            in_specs=[pl.BlockSpec((TS, XW), lambda i, *_: (i, 0)),
                      pl.BlockSpec(memory_space=pl.ANY)],
            out_specs=pl.BlockSpec(memory_space=pl.ANY),
            scratch_shapes=[pltpu.SemaphoreType.DMA(())],
        ),
        out_shape=jax.ShapeDtypeStruct(xs.shape, xs.dtype),
        input_output_aliases={4: 0},
        compiler_params=_cparams(("arbitrary",)),
        name="dispatch",
    )(*tables, hx, xs)


def _moe_kernel(tb_ref, tg_ref, nt_ref, xs_ref, w1_ref, w3_ref, w2_ref, o_ref, acc_ref):
    j = pl.program_id(0)
    pair = pl.program_id(1)

    @pl.when(j < nt_ref[0])
    def _():
        xt = xs_ref[:, :D]
        cmb = (xs_ref[:, D:D + LANES].astype(F32) + xs_ref[:, D + LANES:D + 2 * LANES].astype(F32)
               + xs_ref[:, D + 2 * LANES:].astype(F32))
        lane = lax.broadcasted_iota(jnp.int32, cmb.shape, 1)
        first = tg_ref[j] * EPG + pair * EXPERTS_PER_STEP
        hidden = []
        for u in range(EXPERTS_PER_STEP):
            a = _dot(xt, w1_ref[u].astype(BF))
            he = (a * _sigmoid(a)) * _dot(xt, w3_ref[u].astype(BF))
            ce = jnp.sum(jnp.where(lane == first + u, cmb, 0.0), axis=-1, keepdims=True)
            hidden.append((he * ce).astype(BF))
        w2 = w2_ref[...].reshape(EXPERTS_PER_STEP * DE, D).astype(BF)
        y = _dot(jnp.concatenate(hidden, axis=1), w2)

        @pl.when(pair == 0)
        def _():
            acc_ref[...] = y

        @pl.when(pair == EPG // EXPERTS_PER_STEP - 1)
        def _():
            o_ref[...] = (acc_ref[...] + y).astype(BF)

    @pl.when((j >= nt_ref[0]) & (pair == EPG // EXPERTS_PER_STEP - 1))
    def _():
        o_ref[...] = jnp.zeros_like(o_ref)


def _experts(tile_block, tile_group, n_tiles, xs, wts):
    m = xs.shape[0]
    steps = EPG // EXPERTS_PER_STEP
    assert steps == 2, "the kernel keeps one partial sum: first step stores it, second adds and writes"

    def w_idx(j, e, tb, tg, nt):
        return (tg[j] * steps + jnp.where(j < nt[0], e, steps - 1), 0, 0)

    return pl.pallas_call(
        _moe_kernel,
        grid_spec=pltpu.PrefetchScalarGridSpec(
            num_scalar_prefetch=3,
            grid=(m // TMOE, steps),
            in_specs=[pl.BlockSpec((TMOE, XW), lambda j, e, tb, tg, nt: (tb[j], 0)),
                      pl.BlockSpec((EXPERTS_PER_STEP, D, DE), w_idx),
                      pl.BlockSpec((EXPERTS_PER_STEP, D, DE), w_idx),
                      pl.BlockSpec((EXPERTS_PER_STEP, DE, D), w_idx)],
            out_specs=pl.BlockSpec((TMOE, D), lambda j, e, tb, tg, nt: (j, 0)),
            scratch_shapes=[pltpu.VMEM((TMOE, D), F32)],
        ),
        out_shape=jax.ShapeDtypeStruct((m, D), BF),
        compiler_params=_cparams(("arbitrary", "arbitrary")),
        name="experts",
    )(tile_block, tile_group, n_tiles, xs, wts["exp_w1"], wts["exp_w3"], wts["exp_w2"])


def _combine_kernel(local_ref, far_ref, len_ref, x1_ref, rinfo_ref, mod_ref, ys_ref, o_ref,
                    buf_ref, sem, *, n_steps):
    i = pl.program_id(0)

    def runs(step, slot, action):
        def copy(dst, src, size):
            return pltpu.make_async_copy(ys_ref.at[pl.ds(src, size)],
                                         buf_ref.at[slot, pl.ds(dst, size)], sem.at[slot])

        _run_copies(local_ref, far_ref, len_ref, step, copy, action)

    @pl.when(i == 0)
    def _():
        buf_ref[...] = jnp.zeros_like(buf_ref)
        runs(0, 0, lambda c: c.start())

    @pl.when(i + 1 < n_steps)
    def _():
        runs(i + 1, (i + 1) % 2, lambda c: c.start())

    slot = i % 2
    runs(i, slot, lambda c: c.wait())
    lpos = rinfo_ref[:, 1:2]
    s_i = lax.broadcasted_iota(jnp.int32, (TM, TS), 1)
    from_sorted = jnp.where(s_i.astype(F32) == lpos, 1.0, 0.0).astype(BF)
    moe = _dot(from_sorted, buf_ref[slot])
    o_ref[...] = x1_ref[...] + mod_ref[0, 5:6, :] * moe


def _combine(tables, x1, rinfo, mod, mod_row, ys):
    n = x1.shape[0]
    n_steps = n // TM
    return pl.pallas_call(
        functools.partial(_combine_kernel, n_steps=n_steps),
        grid_spec=pltpu.PrefetchScalarGridSpec(
            num_scalar_prefetch=3,
            grid=(n_steps,),
            in_specs=[pl.BlockSpec((TM, D), lambda i, *_: (i, 0)),
                      pl.BlockSpec((TM, LANES), lambda i, *_: (i, 0)),
                      pl.BlockSpec((1, 6, D), lambda i, *_: (mod_row(i), 0, 0)),
                      pl.BlockSpec(memory_space=pl.ANY)],
            out_specs=pl.BlockSpec((TM, D), lambda i, *_: (i, 0)),
            scratch_shapes=[pltpu.VMEM((2, TS, D), BF), pltpu.SemaphoreType.DMA((2,))],
        ),
        out_shape=jax.ShapeDtypeStruct((n, D), F32),
        compiler_params=_cparams(("arbitrary",)),
        name="combine",
    )(*tables, x1, rinfo, mod, ys)


def _run_lengths(tile_counts):
    counts = tile_counts[:, 0, :NG].astype(jnp.int32)
    return ((counts + RUN_ALIGN - 1) // RUN_ALIGN) * RUN_ALIGN


def _run_tables(lengths, first_far):
    local = jnp.cumsum(lengths, axis=1) - lengths
    far = first_far[None, :] + jnp.cumsum(lengths, axis=0) - lengths
    flat = lambda a: a.astype(jnp.int32).reshape(-1)
    return flat(local), flat(far), flat(lengths)


def _group_layout(counts, max_tiles):
    padded = ((counts + TMOE - 1) // TMOE) * TMOE
    ends = jnp.cumsum(padded)
    offsets = ends - padded
    n_tiles = (ends[-1] // TMOE).astype(jnp.int32)
    tile = jnp.minimum(jnp.arange(max_tiles, dtype=jnp.int32), jnp.maximum(n_tiles - 1, 0))
    tile_group = jnp.sum((tile[:, None] * TMOE >= ends[None, :]).astype(jnp.int32), axis=1)
    return offsets, tile, tile_group, n_tiles.reshape(1)


def _pad_heads(w, perm=None, rotary_only=False):
    lead = w.shape[:-1]
    per = w.shape[-1] // NH
    w = w.reshape(lead + (NH, per))
    if perm is not None:
        nope = jnp.zeros_like(w[..., :NOPE]) if rotary_only else w[..., :NOPE]
        w = jnp.concatenate([nope, w[..., NOPE:][..., perm]], axis=-1)
    w = jnp.pad(w, [(0, 0)] * len(lead) + [(0, 0), (0, HP - per)])
    return w.reshape(lead + (NH * HP,))


def _pad_gain(g, perm, rotary_only=False):
    nope = jnp.zeros((NOPE,), F32) if rotary_only else g[:NOPE]
    g = jnp.concatenate([nope, g[NOPE:][perm], jnp.zeros((HP - QK,), F32)])
    return g.reshape(1, HP)


def _block_diag(w):
    per = BDW // LRU_BLOCK
    w = w.reshape(DR // BDW, per, LRU_BLOCK, LRU_BLOCK)
    bd = jnp.einsum("jpab,pq->jpaqb", w, jnp.eye(per, dtype=w.dtype))
    return bd.reshape(DR // BDW, BDW, BDW)


def _prepare_shared(l, p):
    w_in = lambda a, b: jnp.transpose(lax.slice(p["w_in"], (l, 0, a), (l + 1, D, b)).reshape(D, b - a))
    o1, o2, o3, o4, o5 = DR, 2 * DR, 2 * DR + QL, 2 * DR + QL + KVL, 2 * DR + QL + KVL + ROPE
    bd = jnp.stack([_block_diag(p["lru_wa"][l, 0]), _block_diag(p["lru_wx"][l, 0]),
                    _block_diag(p["lru_wa"][l, 1]), _block_diag(p["lru_wx"][l, 1])])
    bd = (0.5 * bd).astype(BF)
    wom = p["w_o_mla"][l].reshape(NH, VD, D)
    wom = jnp.pad(wom, ((0, 0), (0, HP - VD), (0, 0))).reshape(DH, D)
    router_w = jnp.concatenate([p["router_we"][l], p["router_wg"][l],
                                jnp.zeros((D, LANES - NE - NG), F32)], axis=1)
    router_b = jnp.concatenate([p["router_be"][l], p["router_bg"][l],
                                jnp.zeros((LANES - NE - NG,), F32)]).reshape(1, LANES)
    router_hi = router_w.astype(BF)
    router_lo = (router_w - router_hi.astype(F32)).astype(BF)
    router_w = jnp.concatenate([router_hi, router_lo], axis=1)
    return {
        "n1": p["norm1_g"][l].reshape(1, D), "n2": p["norm2_g"][l].reshape(1, D),
        "w_x": w_in(0, o1).astype(BF), "w_g": w_in(o1, o2).astype(BF),
        "w_q": w_in(o2, o3).astype(BF), "w_kv": w_in(o3, o4), "w_kr": w_in(o4, o5),
        "w_gate": (0.5 * w_in(o5, o5 + 2 * D)).astype(BF),
        "qan": p["q_a_norm"][l].reshape(1, QL), "kvan": p["kv_a_norm"][l].reshape(1, KVL),
        "w_uk": _pad_heads(p["w_uk"][l]).astype(BF),
        "w_uv": _pad_heads(p["w_uv"][l]).astype(BF),
        "conv_w": p["conv_w"][l], "conv_b": p["conv_b"][l].reshape(1, DR), "bd": bd,
        "lru_ba": 0.5 * p["lru_ba"][l], "lru_bx": 0.5 * p["lru_bx"][l], "lru_lam": p["lru_lam"][l],
        "w_o_rnn": p["w_o_rnn"][l].astype(BF), "w_o_mla": wom.astype(BF), "w_out": (0.5 * p["w_out"][l]).astype(BF),
        "router_w": router_w, "router_b": router_b,
        "exp_w1": p["exp_w1"][l], "exp_w3": p["exp_w3"][l], "exp_w2": p["exp_w2"][l],
    }


def _with_rope_order(l, p, shared, perm, rotary):
    w = dict(shared)
    zeros = lambda n: jnp.zeros((n, D), F32)
    rope_block = lambda order: [zeros(NOPE), shared["w_kr"][order, :], zeros(HP - QK)]
    kvr = [shared["w_kv"]] + rope_block(perm)
    w["w_uq"] = _pad_heads(p["w_uq"][l], perm).astype(BF)
    w["gq"] = _pad_gain(p["q_norm"][l], perm)
    w["gk"] = _pad_gain(p["k_norm"][l], perm)
    if rotary:
        pair = np.concatenate([perm[ROPE // 2:], perm[:ROPE // 2]])
        kvr += rope_block(pair)
        w["w_uq_pair"] = _pad_heads(p["w_uq"][l], pair, rotary_only=True).astype(BF)
        w["gq_pair"] = _pad_gain(p["q_norm"][l], pair, rotary_only=True)
        w["gk_pair"] = _pad_gain(p["k_norm"][l], pair, rotary_only=True)
    w["w_kvr"] = jnp.concatenate(kvr, axis=0).astype(BF)
    return w


def _rope_tables(n_tokens):
    rows = n_tokens // GRID_W
    row = np.repeat(np.arange(rows), GRID_W).astype(np.float32)
    col = np.tile(np.arange(GRID_W), rows).astype(np.float32)
    axis_dim = ROPE // 2
    inv = (np.float32(ROPE_BASE) ** (-np.arange(0, axis_dim, 2, dtype=np.float32) / axis_dim)).astype(np.float32)
    ang = np.concatenate([row[:, None] * inv, col[:, None] * inv], axis=-1).astype(np.float32)
    cos, sin = np.cos(ang), np.sin(ang)
    ones = lambda n: np.ones((n_tokens, n), np.float32)
    zeros = lambda n: np.zeros((n_tokens, n), np.float32)
    cos_t = np.concatenate([ones(NOPE), cos, cos, ones(HP - QK)], axis=1)
    sin_t = np.concatenate([zeros(NOPE), -sin, sin, zeros(HP - QK)], axis=1)
    return jnp.asarray(cos_t, F32), jnp.asarray(sin_t, F32)


def kernel(x_prompt, x_sample, cache_mla_ckv, cache_mla_krope, state_rglru, c, c_ctx, norm1_g, norm2_g, w_mod, b_mod, w_in, conv_w, conv_b, lru_wa, lru_ba, lru_wx, lru_bx, lru_lam, q_a_norm, kv_a_norm, w_uq, w_uk, w_uv, q_norm, k_norm, w_o_rnn, w_o_mla, w_out, router_wg, router_bg, router_we, router_be, exp_w1, exp_w3, exp_w2):
    p = dict(norm1_g=norm1_g, norm2_g=norm2_g, w_in=w_in, conv_w=conv_w, conv_b=conv_b,
             lru_wa=lru_wa, lru_ba=lru_ba, lru_wx=lru_wx, lru_bx=lru_bx, lru_lam=lru_lam,
             q_a_norm=q_a_norm, kv_a_norm=kv_a_norm, w_uq=w_uq, w_uk=w_uk, w_uv=w_uv,
             q_norm=q_norm, k_norm=k_norm, w_o_rnn=w_o_rnn, w_o_mla=w_o_mla, w_out=w_out,
             router_wg=router_wg, router_bg=router_bg, router_we=router_we, router_be=router_be,
             exp_w1=exp_w1, exp_w3=exp_w3, exp_w2=exp_w2)
    depth = w_in.shape[0]
    nb, seq, _ = x_prompt.shape
    db, dseq, _ = x_sample.shape
    ident = np.arange(ROPE)
    halves = np.concatenate([np.arange(0, ROPE, 2), np.arange(1, ROPE, 2)])
    rope_tabs = _rope_tables(dseq)
    cond8 = jnp.concatenate([c_ctx[None, :], c, jnp.zeros((SUBLANES - 1 - db, D), F32)], axis=0)
    ctx_row = lambda tile_rows: (lambda i: 0)
    lat_row = lambda tile_rows: (lambda i: (i * tile_rows) // dseq + 1)
    n_ctx, n_lat = nb * seq, db * dseq
    run_padding = ((n_ctx + n_lat) // TM) * NG * (RUN_ALIGN - 1)
    max_tiles = -(-(n_ctx + n_lat + run_padding) // TMOE) + NG
    per_seq = lambda arrs, b, t: [a.reshape(b, t, a.shape[-1]) for a in arrs]
    flat = lambda a: a.reshape(-1, a.shape[-1])

    y_prompt, y_sample = x_prompt.reshape(n_ctx, D), x_sample.reshape(n_lat, D)
    ckv_list, krope_list, rnn_list = [], [], []
    for l in range(depth):
        shared = _prepare_shared(l, p)
        w_ctx = _with_rope_order(l, p, shared, ident, False)
        w_lat = _with_rope_order(l, p, shared, halves, True)
        mod = _modulation(cond8, w_mod[l], b_mod[l]).reshape(SUBLANES, 6, D)

        h, xr, gg, q, k, v, ckv, kro = _projections(y_prompt, mod, ctx_row(TM), w_ctx, None, True)
        xr, gg, q, k, v = per_seq([xr, gg, q, k, v], nb, seq)
        yr, h_fin = _rglru(xr, gg, shared, None, True)
        ya = _attention(q, k, v, None, None, 4, NH)
        x1_c, hx_c, ri_c, tc_c = _merge_out(y_prompt, h, flat(yr), flat(ya), mod, ctx_row(TM), shared)
        ckv_list.append(ckv.reshape(nb, seq, KVL))
        krope_list.append(kro.reshape(nb, seq, ROPE))
        rnn_list.append(h_fin)

        krp_cache = jnp.pad(cache_mla_krope[:, l][..., halves], ((0, 0), (0, 0), (NOPE, HP - QK)))
        kc, vc = _cache_keys_values(cache_mla_ckv[:, l], krp_cache, w_lat)
        h, xr, gg, q, k, v = _projections(y_sample, mod, lat_row(TM), w_lat, rope_tabs, False)
        xr, gg, q, k, v = per_seq([xr, gg, q, k, v], db, dseq)
        yr, _ = _rglru(xr, gg, shared, state_rglru[:, l], False)
        ya = _attention(q, k, v, kc, vc, 1, 4)
        x1_l, hx_l, ri_l, tc_l = _merge_out(y_sample, h, flat(yr), flat(ya), mod, lat_row(TM), shared)

        len_c, len_l = _run_lengths(tc_c), _run_lengths(tc_l)
        rows_c = jnp.sum(len_c, axis=0)
        offsets, tile_block, tile_group, n_tiles = _group_layout(rows_c + jnp.sum(len_l, axis=0), max_tiles)
        runs_c = _run_tables(len_c, offsets)
        runs_l = _run_tables(len_l, offsets + rows_c)
        xs = jnp.zeros((max_tiles * TMOE, XW), BF)
        xs = _dispatch(runs_c, hx_c, xs)
        xs = _dispatch(runs_l, hx_l, xs)
        ys = _experts(tile_block, tile_group, n_tiles, xs, shared)
        y_prompt = _combine(runs_c, x1_c, ri_c, mod, ctx_row(TM), ys)
        y_sample = _combine(runs_l, x1_l, ri_l, mod, lat_row(TM), ys)

    y_prompt, y_sample = y_prompt.reshape(nb, seq, D), y_sample.reshape(db, dseq, D)

    return (y_prompt, y_sample, jnp.stack(ckv_list, axis=1), jnp.stack(krope_list, axis=1),
            jnp.stack(rnn_list, axis=1))
```

```python
import functools
import math

import numpy as np
import jax
import jax.numpy as jnp
from jax import lax
from jax.experimental import pallas as pl
from jax.experimental.pallas import tpu as pltpu

D = 1024
DR = 1024
QL = 384
KVL = 256
NH = 8
NOPE = 64
ROPE = 32
QK = NOPE + ROPE
VD = 64
HP = 128
DH = NH * HP
GRID_W = 64
ROPE_BASE = 10000.0
EPS = 1e-6
TINY = 1e-30
LRU_C = 8.0
LRU_BLOCK = 64
BDW = 256
CH = 1024
NG = 4
EPG = 4
NE = NG * EPG
DE = 512
LANES = 128
SUBLANES = 8
TM = 512
TS = 640
TMOE = 512
EXPERTS_PER_STEP = 2
XW = D + 3 * LANES
RUN_ALIGN = 16
VMEM_LIMIT = 52 * 1024 * 1024
BF = jnp.bfloat16
F32 = jnp.float32


def _cparams(sem):
    return pltpu.CompilerParams(dimension_semantics=sem, vmem_limit_bytes=VMEM_LIMIT)


def _dot(a, b):
    return jnp.dot(a, b, preferred_element_type=F32)


def _dot_nt(a, b):
    return lax.dot_general(a, b, (((1,), (1,)), ((), ())), preferred_element_type=F32)


def _rms(x, g, width):
    ms = jnp.sum(x * x, axis=-1, keepdims=True) * (1.0 / width)
    return x * lax.rsqrt(ms + EPS) * g


def _modulated_norm(x, g, scale, shift):
    return _rms(x, g * (1.0 + scale), D) + shift


def _mod_kernel(c_ref, w_ref, b_ref, o_ref):
    c = c_ref[...]
    s = c * jax.nn.sigmoid(c)
    o_ref[...] = _dot(s, w_ref[...]) + b_ref[...]


def _modulation(cond8, w_mod, b_mod):
    n = w_mod.shape[1]
    return pl.pallas_call(
        _mod_kernel,
        grid=(n // D,),
        in_specs=[
            pl.BlockSpec((SUBLANES, D), lambda j: (0, 0)),
            pl.BlockSpec((D, D), lambda j: (0, j)),
            pl.BlockSpec((1, D), lambda j: (0, j)),
        ],
        out_specs=pl.BlockSpec((SUBLANES, D), lambda j: (0, j)),
        out_shape=jax.ShapeDtypeStruct((SUBLANES, n), F32),
        compiler_params=_cparams(("arbitrary",)),
        name="modulation",
    )(cond8, w_mod, b_mod.reshape(1, n))


def _head_norm(xh, gain, cos=None, partner_scaled=None):
    ms = jnp.sum(xh * xh, axis=-1, keepdims=True) * (1.0 / QK)
    rs = lax.rsqrt(ms + EPS)
    y = xh * rs * gain
    if cos is None:
        return y
    return y * cos + partner_scaled * rs


def _keys_values(ckv, krp, wuk_ref, wuv_ref, gk, cos, partner_scaled, k_ref, v_ref):
    cb = ckv.astype(BF)
    kn = _dot(cb, wuk_ref[...])
    v_ref[...] = _dot(cb, wuv_ref[...]).astype(BF)
    for h in range(NH):
        kh = kn[:, h * HP:(h + 1) * HP] + krp
        k_ref[:, h * HP:(h + 1) * HP] = _head_norm(kh, gk, cos, partner_scaled).astype(BF)


def _proj_kernel(*refs, rope, emit_cache):
    it = iter(refs)
    x_ref, mod_ref, n1_ref = next(it), next(it), next(it)
    wx_ref, wg_ref, wq_ref, wkvr_ref = next(it), next(it), next(it), next(it)
    qan_ref, kvan_ref, wuq_ref, gq_ref = next(it), next(it), next(it), next(it)
    wuk_ref, gk_ref, wuv_ref = next(it), next(it), next(it)
    if rope:
        wuqs_ref, gqs_ref, gks_ref, cos_ref, sins_ref = (next(it) for _ in range(5))
    h_ref, xr_ref, gg_ref, q_ref, k_ref, v_ref = (next(it) for _ in range(6))
    if emit_cache:
        ckv_ref, kro_ref = next(it), next(it)

    hb = _modulated_norm(x_ref[...], n1_ref[...], mod_ref[0, 1:2, :], mod_ref[0, 0:1, :]).astype(BF)
    h_ref[...] = hb
    xr_ref[...] = _dot_nt(hb, wx_ref[...]).astype(BF)
    gg_ref[...] = jax.nn.gelu(_dot_nt(hb, wg_ref[...])).astype(BF)

    qnb = _rms(_dot_nt(hb, wq_ref[...]), qan_ref[...], QL).astype(BF)
    q = _dot(qnb, wuq_ref[...])
    gq = gq_ref[...]
    cos = q_partner = q_pair_scale = None
    if rope:
        cos, sins = cos_ref[...], sins_ref[...]
        q_partner = _dot(qnb, wuqs_ref[...])
        q_pair_scale = gqs_ref[...] * sins
    for hd in range(NH):
        cols = slice(hd * HP, (hd + 1) * HP)
        partner = q_partner[:, cols] * q_pair_scale if rope else None
        q_ref[:, cols] = _head_norm(q[:, cols], gq, cos, partner).astype(BF)

    kvr = _dot_nt(hb, wkvr_ref[...])
    ckv = _rms(kvr[:, :KVL], kvan_ref[...], KVL)
    krp = kvr[:, KVL:KVL + HP]
    k_partner = kvr[:, KVL + HP:KVL + 2 * HP] * (gks_ref[...] * sins) if rope else None
    if emit_cache:
        ckv_ref[...] = ckv
        kro_ref[...] = krp[:, NOPE:NOPE + ROPE]
    _keys_values(ckv, krp, wuk_ref, wuv_ref, gk_ref[...], cos, k_partner, k_ref, v_ref)


def _resident(shape):
    return pl.BlockSpec(shape, lambda i: (0,) * len(shape), pipeline_mode=pl.Buffered(1))


def _projections(x, mod, mod_row, wts, rope_tabs, emit_cache):
    n = x.shape[0]
    rope = rope_tabs is not None
    tile = lambda w: pl.BlockSpec((TM, w), lambda i: (i, 0))
    in_specs = [
        tile(D),
        pl.BlockSpec((1, 6, D), lambda i: (mod_row(i), 0, 0)),
        _resident((1, D)),
        _resident((DR, D)), _resident((DR, D)), _resident((QL, D)), _resident(wts["w_kvr"].shape),
        _resident((1, QL)), _resident((1, KVL)), _resident((QL, DH)), _resident((1, HP)),
        _resident((KVL, DH)), _resident((1, HP)), _resident((KVL, DH)),
    ]
    args = [x, mod, wts["n1"], wts["w_x"], wts["w_g"], wts["w_q"], wts["w_kvr"],
            wts["qan"], wts["kvan"], wts["w_uq"], wts["gq"], wts["w_uk"], wts["gk"], wts["w_uv"]]
    if rope:
        tiles_per_seq = rope_tabs[0].shape[0] // TM
        in_specs += [_resident((QL, DH)), _resident((1, HP)), _resident((1, HP))]
        in_specs += [pl.BlockSpec((TM, HP), lambda i: (i % tiles_per_seq, 0))] * 2
        args += [wts["w_uq_pair"], wts["gq_pair"], wts["gk_pair"]] + list(rope_tabs)
    out_specs = [tile(D), tile(DR), tile(DR), tile(DH), tile(DH), tile(DH)]
    out_shape = [jax.ShapeDtypeStruct((n, D), BF), jax.ShapeDtypeStruct((n, DR), BF),
                 jax.ShapeDtypeStruct((n, DR), BF), jax.ShapeDtypeStruct((n, DH), BF),
                 jax.ShapeDtypeStruct((n, DH), BF), jax.ShapeDtypeStruct((n, DH), BF)]
    if emit_cache:
        out_specs += [tile(KVL), tile(ROPE)]
        out_shape += [jax.ShapeDtypeStruct((n, KVL), F32), jax.ShapeDtypeStruct((n, ROPE), F32)]
    return pl.pallas_call(
        functools.partial(_proj_kernel, rope=rope, emit_cache=emit_cache),
        grid=(n // TM,),
        in_specs=in_specs,
        out_specs=out_specs,
        out_shape=out_shape,
        compiler_params=_cparams(("arbitrary",)),
        name="projections",
    )(*args)


def _cache_kv_kernel(ckv_ref, krp_ref, wuk_ref, gk_ref, wuv_ref, k_ref, v_ref):
    _keys_values(ckv_ref[0], krp_ref[0], wuk_ref, wuv_ref, gk_ref[...], None, None,
                 k_ref.at[0], v_ref.at[0])


def _cache_keys_values(ckv, krp, wts):
    b, s, _ = ckv.shape
    full = lambda shape: pl.BlockSpec(shape, lambda i: (0,) * len(shape))
    return pl.pallas_call(
        _cache_kv_kernel,
        grid=(b,),
        in_specs=[pl.BlockSpec((1, s, KVL), lambda i: (i, 0, 0)),
                  pl.BlockSpec((1, s, HP), lambda i: (i, 0, 0)),
                  full((KVL, DH)), full((1, HP)), full((KVL, DH))],
        out_specs=[pl.BlockSpec((1, s, DH), lambda i: (i, 0, 0))] * 2,
        out_shape=[jax.ShapeDtypeStruct((b, s, DH), BF)] * 2,
        compiler_params=_cparams(("arbitrary",)),
        name="cache_keys_values",
    )(ckv, krp, wts["w_uk"], wts["gk"], wts["w_uv"])


WIN = 256
SEG = WIN // 8


def _window_permutation():
    dst = np.arange(WIN)
    p = np.zeros((WIN, WIN), np.float32)
    p[dst, (dst % SUBLANES) * SEG + dst // SUBLANES] = 1.0
    return p


def _sigmoid(x):
    return 0.5 * jnp.tanh(0.5 * x) + 0.5


def _segment_pass(a_scr, b_scr, bases, inits, out_scr=None):
    def body(k, carry):
        new = []
        for d in range(2):
            h, p = carry[d]
            i = k if d == 0 else SEG - 1 - k
            rows = pl.ds(pl.multiple_of(bases[d] + i * SUBLANES, SUBLANES), SUBLANES)
            a = a_scr[d, rows, :]
            h = a * h + b_scr[d, rows, :]
            if out_scr is None:
                p = a * p
            else:
                out_scr[d, rows, :] = h
            new.append((h, p))
        return tuple(new)

    init = tuple((inits[d], jnp.ones_like(inits[d])) for d in range(2))
    return lax.fori_loop(0, SEG, body, init, unroll=4)


def _segment_entries(end, decay, carry_in, forward):
    order = range(SUBLANES) if forward else reversed(range(SUBLANES))
    rows = [None] * SUBLANES
    c = carry_in
    for s in order:
        rows[s] = c
        c = end[s:s + 1, :] + decay[s:s + 1, :] * c
    return jnp.concatenate(rows, axis=0), c


def _rglru_kernel(*refs, t, has_h0, emit_state):
    it = iter(refs)
    xr_ref, gg_ref, perm_ref, unperm_ref = (next(it) for _ in range(4))
    cw_ref, cb_ref, bd_ref, ba_ref, bx_ref, lam_ref = (next(it) for _ in range(6))
    h0_ref = next(it) if has_h0 else None
    y_ref = next(it)
    hf_ref = next(it) if emit_state else None
    a_scr, b_scr, h_scr = (next(it) for _ in range(3))

    n_win = t // WIN
    sub = lax.broadcasted_iota(jnp.int32, (SUBLANES, CH), 0)
    zero_row = jnp.zeros((1, CH), F32)
    edge = 2 * SUBLANES
    for w in range(n_win):
        lo, hi = w * WIN, (w + 1) * WIN
        xp = _dot(perm_ref[...], xr_ref[0, lo:hi, :])
        before = xr_ref[0, lo - edge:lo, :].astype(F32)[edge - 1:edge, :] if w > 0 else zero_row
        after = xr_ref[0, hi:hi + edge, :].astype(F32) if w < n_win - 1 else None
        after0 = after[0:1, :] if after is not None else zero_row
        after1 = after[1:2, :] if after is not None else zero_row
        tile_m1 = jnp.where(sub == 0, before, pltpu.roll(xp[WIN - SUBLANES:WIN, :], 1, 0))
        tile_p0 = jnp.where(sub == SUBLANES - 1, after0, pltpu.roll(xp[0:SUBLANES, :], SUBLANES - 1, 0))
        tile_p1 = jnp.where(sub == SUBLANES - 1, after1,
                            pltpu.roll(xp[SUBLANES:2 * SUBLANES, :], SUBLANES - 1, 0))
        xe = jnp.concatenate([tile_m1, xp, tile_p0, tile_p1], axis=0)
        xc = cb_ref[...]
        for tap in range(4):
            xc = xc + xe[tap * SUBLANES:tap * SUBLANES + WIN, :] * cw_ref[tap:tap + 1, :]
        for s in range(CH // BDW):
            cols = slice(s * BDW, (s + 1) * BDW)
            xs = xc[:, cols]
            xsb = xs.astype(BF)
            xh = 0.5 * xs
            for d in range(2):
                tr = jnp.tanh(_dot(xsb, bd_ref[2 * d, s]) + ba_ref[d:d + 1, cols])
                ti = jnp.tanh(_dot(xsb, bd_ref[2 * d + 1, s]) + bx_ref[d:d + 1, cols])
                nl = -lam_ref[d:d + 1, cols]
                softplus = jnp.maximum(nl, 0.0) + jnp.log(1.0 + jnp.exp(-jnp.abs(nl)))
                ch = (-0.5 * LRU_C) * softplus
                a = jnp.exp(tr * ch + ch)
                z = 1.0 - a * a
                root = z * lax.rsqrt(jnp.maximum(z, TINY))
                a_scr[d, lo:hi, cols] = a
                b_scr[d, lo:hi, cols] = root * (ti * xh + xh)

    if has_h0:
        carry = [h0_ref[0, 0:1, :], h0_ref[0, 1:2, :]]
    else:
        carry = [zero_row, zero_row]
    zeros = jnp.zeros((SUBLANES, CH), F32)
    for k in range(n_win):
        bases = (k * WIN, (n_win - 1 - k) * WIN)
        totals = _segment_pass(a_scr, b_scr, bases, (zeros, zeros))
        entries = []
        for d in range(2):
            entry, carry[d] = _segment_entries(totals[d][0], totals[d][1], carry[d], d == 0)
            entries.append(entry)
        _segment_pass(a_scr, b_scr, bases, entries, out_scr=h_scr)

    if emit_state:
        hf_ref[0, 0:1, :] = carry[0]
        hf_ref[0, 1:2, :] = carry[1]
    for w in range(n_win):
        lo, hi = w * WIN, (w + 1) * WIN
        gate = _dot(perm_ref[...], gg_ref[0, lo:hi, :])
        yp = ((h_scr[0, lo:hi, :] + h_scr[1, lo:hi, :]) * gate).astype(BF)
        y_ref[0, lo:hi, :] = _dot(unperm_ref[...], yp).astype(BF)


def _rglru(xr, gg, wts, h0, emit_state):
    b, t, _ = xr.shape
    nc = DR // CH
    has_h0 = h0 is not None
    chunk = lambda r: pl.BlockSpec((r, CH), lambda i, j: (0, j))
    seq = pl.BlockSpec((1, t, CH), lambda i, j: (i, 0, j))
    state = pl.BlockSpec((1, 2, CH), lambda i, j: (i, 0, j))
    window = pl.BlockSpec((WIN, WIN), lambda i, j: (0, 0))
    perm = _window_permutation()
    in_specs = [seq, seq, window, window, chunk(4), chunk(1),
                pl.BlockSpec((4, CH // BDW, BDW, BDW), lambda i, j: (0, j, 0, 0)),
                chunk(2), chunk(2), chunk(2)]
    args = [xr, gg, jnp.asarray(perm, BF), jnp.asarray(perm.T, BF),
            wts["conv_w"], wts["conv_b"], wts["bd"], wts["lru_ba"], wts["lru_bx"], wts["lru_lam"]]
    if has_h0:
        in_specs.append(state)
        args.append(h0)
    out_specs = [seq]
    out_shape = [jax.ShapeDtypeStruct((b, t, DR), BF)]
    if emit_state:
        out_specs.append(state)
        out_shape.append(jax.ShapeDtypeStruct((b, 2, DR), F32))
    res = pl.pallas_call(
        functools.partial(_rglru_kernel, t=t, has_h0=has_h0, emit_state=emit_state),
        grid=(b, nc),
        in_specs=in_specs,
        out_specs=out_specs,
        out_shape=out_shape,
        scratch_shapes=[pltpu.VMEM((2, t, CH), F32)] * 3,
        compiler_params=_cparams(("arbitrary", "arbitrary")),
        name="rglru",
    )(*args)
    return res if emit_state else (res[0], None)


def _attn_kernel(*refs, t, n_seqs, n_heads, has_ctx, q_block):
    it = iter(refs)
    q_ref, k_ref, v_ref = next(it), next(it), next(it)
    kc_ref = vc_ref = None
    if has_ctx:
        kc_ref, vc_ref = next(it), next(it)
    o_ref = next(it)
    log2_scale = (QK ** -0.5) * math.log2(math.e)
    for sq, hd in [(sq, hd) for sq in range(n_seqs) for hd in range(n_heads)]:
        cols = slice(hd * HP, (hd + 1) * HP)
        k = k_ref[sq, :, cols]
        v = v_ref[sq, :, cols]
        if has_ctx:
            kc = kc_ref[sq, :, cols]
            vc = vc_ref[sq, :, cols]
        for qb in range(t // q_block):
            rows = slice(qb * q_block, (qb + 1) * q_block)
            q = q_ref[sq, rows, cols]
            s = _dot_nt(q, k) * log2_scale
            m = jnp.max(s, axis=-1, keepdims=True)
            if has_ctx:
                sc = _dot_nt(q, kc) * log2_scale
                m = jnp.maximum(m, jnp.max(sc, axis=-1, keepdims=True))
            p = jnp.exp2(s - m)
            den = jnp.sum(p, axis=-1, keepdims=True)
            o = _dot(p.astype(BF), v)
            if has_ctx:
                pc = jnp.exp2(sc - m)
                den = den + jnp.sum(pc, axis=-1, keepdims=True)
                o = o + _dot(pc.astype(BF), vc)
            o_ref[sq, rows, cols] = (o / den).astype(BF)


def _attention(q, k, v, kc, vc, seqs_per_step, heads_per_step):
    b, t, _ = q.shape
    has_ctx = kc is not None
    w = heads_per_step * HP
    blk = lambda n: pl.BlockSpec((seqs_per_step, n, w), lambda i, j: (i, 0, j))
    in_specs = [blk(t), blk(t), blk(t)]
    args = [q, k, v]
    if has_ctx:
        in_specs += [blk(kc.shape[1])] * 2
        args += [kc, vc]
    return pl.pallas_call(
        functools.partial(_attn_kernel, t=t, n_seqs=seqs_per_step, n_heads=heads_per_step,
                          has_ctx=has_ctx, q_block=min(t, 256)),
        grid=(b // seqs_per_step, NH // heads_per_step),
        in_specs=in_specs,
        out_specs=blk(t),
        out_shape=jax.ShapeDtypeStruct((b, t, DH), BF),
        compiler_params=_cparams(("arbitrary", "arbitrary")),
        name="attention",
    )(*args)


def _route(logits):
    lane = lax.broadcasted_iota(jnp.int32, logits.shape, 1)
    lanef = lane.astype(F32)
    neg = -jnp.inf
    big = float(LANES)
    gl = jnp.where((lane >= NE) & (lane < NE + NG), logits, neg)
    gmax = jnp.max(gl, axis=-1, keepdims=True)
    gidx = jnp.min(jnp.where(gl == gmax, lanef, big), axis=-1, keepdims=True) - float(NE)
    gw = 1.0 / jnp.sum(jnp.exp(gl - gmax), axis=-1, keepdims=True)
    lo = gidx * float(EPG)
    el = jnp.where((lanef >= lo) & (lanef < lo + float(EPG)), logits, neg)
    v1 = jnp.max(el, axis=-1, keepdims=True)
    i1 = jnp.min(jnp.where(el == v1, lanef, big), axis=-1, keepdims=True)
    el2 = jnp.where(lanef == i1, neg, el)
    v2 = jnp.max(el2, axis=-1, keepdims=True)
    i2 = jnp.min(jnp.where(el2 == v2, lanef, big), axis=-1, keepdims=True)
    e2 = jnp.exp(v2 - v1)
    w1 = gw / (1.0 + e2)
    w2 = gw * e2 / (1.0 + e2)
    cmb = jnp.where(lanef == i1, w1, 0.0) + jnp.where(lanef == i2, w2, 0.0)
    return cmb, gidx


def _out_kernel(x_ref, h_ref, yr_ref, ya_ref, mod_ref, wgate_ref, wor_ref, wom_ref, wout_ref,
                n2_ref, rw_ref, rb_ref, x1_ref, hx_ref, rinfo_ref, tcnt_ref):
    x = x_ref[...]
    gl = _dot_nt(h_ref[...], wgate_ref[...])
    merged2 = ((jnp.tanh(gl[:, :D]) + 1.0) * _dot(yr_ref[...], wor_ref[...])
               + (jnp.tanh(gl[:, D:]) + 1.0) * _dot(ya_ref[...], wom_ref[...]))
    mix = _dot(merged2.astype(BF), wout_ref[...])
    x1 = x + mod_ref[0, 2:3, :] * mix
    x1_ref[...] = x1
    h2 = _modulated_norm(x1, n2_ref[...], mod_ref[0, 4:5, :], mod_ref[0, 3:4, :])
    h2_hi = h2.astype(BF)
    h2_lo = (h2 - h2_hi.astype(F32)).astype(BF)
    part = _dot(h2_hi, rw_ref[...])
    logits = part[:, :LANES] + part[:, LANES:] + _dot(h2_lo, rw_ref[:, :LANES]) + rb_ref[...]
    cmb, gidx = _route(logits)

    lanef = lax.broadcasted_iota(jnp.int32, cmb.shape, 1).astype(F32)
    ghot = jnp.where(lanef == gidx, 1.0, 0.0)
    r_i = lax.broadcasted_iota(jnp.int32, (TM, TM), 0)
    c_i = lax.broadcasted_iota(jnp.int32, (TM, TM), 1)
    tri = jnp.where(r_i > c_i, 1.0, 0.0).astype(BF)
    earlier_same = jnp.sum(_dot(tri, ghot.astype(BF)) * ghot, axis=-1, keepdims=True)
    counts = jnp.sum(ghot, axis=0, keepdims=True)
    padded = jnp.floor((counts + (RUN_ALIGN - 1.0)) * (1.0 / RUN_ALIGN)) * RUN_ALIGN
    lower_groups = jnp.sum(jnp.where(lanef < gidx, padded, 0.0), axis=-1, keepdims=True)
    lpos = lower_groups + earlier_same
    s_i = lax.broadcasted_iota(jnp.int32, (TM, TS), 1)
    to_sorted = jnp.where(s_i.astype(F32) == lpos, 1.0, 0.0).astype(BF)
    c1 = cmb.astype(BF)
    c2 = (cmb - c1.astype(F32)).astype(BF)
    c3 = (cmb - c1.astype(F32) - c2.astype(F32)).astype(BF)
    payload = jnp.concatenate([h2.astype(BF), c1, c2, c3], axis=1)
    srt = lax.dot_general(to_sorted, payload, (((0,), (0,)), ((), ())), preferred_element_type=F32)
    hx_ref[...] = srt.astype(BF)

    rinfo_ref[...] = jnp.where(lanef == 0.0, gidx, jnp.where(lanef == 1.0, lpos, 0.0))
    tcnt_ref[0] = jnp.broadcast_to(counts, (SUBLANES, LANES))


def _merge_out(x, h, yr, ya, mod, mod_row, wts):
    n = x.shape[0]
    n_tiles = n // TM
    tile = lambda w: pl.BlockSpec((TM, w), lambda i: (i, 0))
    return pl.pallas_call(
        _out_kernel,
        grid=(n_tiles,),
        in_specs=[tile(D), tile(D), tile(DR), tile(DH),
                  pl.BlockSpec((1, 6, D), lambda i: (mod_row(i), 0, 0)),
                  _resident((2 * D, D)), _resident((DR, D)), _resident((DH, D)), _resident((D, D)),
                  _resident((1, D)), _resident((D, 2 * LANES)), _resident((1, LANES))],
        out_specs=[tile(D), pl.BlockSpec((TS, XW), lambda i: (i, 0)), tile(LANES),
                   pl.BlockSpec((1, SUBLANES, LANES), lambda i: (i, 0, 0))],
        out_shape=[jax.ShapeDtypeStruct((n, D), F32), jax.ShapeDtypeStruct((n_tiles * TS, XW), BF),
                   jax.ShapeDtypeStruct((n, LANES), F32),
                   jax.ShapeDtypeStruct((n_tiles, SUBLANES, LANES), F32)],
        compiler_params=_cparams(("arbitrary",)),
        name="merge_out",
    )(x, h, yr, ya, mod, wts["w_gate"], wts["w_o_rnn"], wts["w_o_mla"], wts["w_out"],
      wts["n2"], wts["router_w"], wts["router_b"])


def _run_copies(local_ref, far_ref, len_ref, tile, make_copy, action):
    for g in range(NG):
        idx = tile * NG + g
        n = len_ref[idx]
        local0 = local_ref[idx]
        far0 = far_ref[idx]
        for k in reversed(range(RUN_ALIGN.bit_length() - 1, TS.bit_length())):
            size = 1 << k
            done = (n >> (k + 1)) << (k + 1)

            @pl.when(((n >> k) & 1) == 1)
            def _():
                action(make_copy(pl.multiple_of(local0 + done, RUN_ALIGN),
                                 pl.multiple_of(far0 + done, RUN_ALIGN), size))


def _dispatch_kernel(local_ref, far_ref, len_ref, hx_ref, xs_in_ref, xs_ref, sem):
    del xs_in_ref

    def copy(src, dst, size):
        return pltpu.make_async_copy(hx_ref.at[pl.ds(src, size)], xs_ref.at[pl.ds(dst, size)], sem)

    tile = pl.program_id(0)
    _run_copies(local_ref, far_ref, len_ref, tile, copy, lambda c: c.start())
    _run_copies(local_ref, far_ref, len_ref, tile, copy, lambda c: c.wait())


def _dispatch(tables, hx, xs):
    n_tiles = hx.shape[0] // TS
    return pl.pallas_call(
        _dispatch_kernel,
        grid_spec=pltpu.PrefetchScalarGridSpec(
            num_scalar_prefetch=3,
            grid=(n_tiles,),
            in_specs=[pl.BlockSpec((TS, XW), lambda i, *_: (i, 0)),
                      pl.BlockSpec(memory_space=pl.ANY)],
            out_specs=pl.BlockSpec(memory_space=pl.ANY),
            scratch_shapes=[pltpu.SemaphoreType.DMA(())],
        ),
        out_shape=jax.ShapeDtypeStruct(xs.shape, xs.dtype),
        input_output_aliases={4: 0},
        compiler_params=_cparams(("arbitrary",)),
        name="dispatch",
    )(*tables, hx, xs)


def _opens_group(j, tg):
    return (j == 0) | (tg[j] != tg[jnp.maximum(j - 1, 0)])


def _moe_kernel(tb_ref, tg_ref, nt_ref, xs_ref, w1_ref, w3_ref, w2_ref, o_ref,
                acc_ref, w1c_ref, w3c_ref, w2c_ref):
    j = pl.program_id(0)
    pair = pl.program_id(1)

    @pl.when((j < nt_ref[0]) & _opens_group(j, tg_ref))
    def _():
        w1c_ref[pair] = w1_ref[...].astype(BF)
        w3c_ref[pair] = w3_ref[...].astype(BF)
        w2c_ref[pair] = w2_ref[...].reshape(EXPERTS_PER_STEP * DE, D).astype(BF)

    @pl.when(j < nt_ref[0])
    def _():
        xt = xs_ref[:, :D]
        cmb = (xs_ref[:, D:D + LANES].astype(F32) + xs_ref[:, D + LANES:D + 2 * LANES].astype(F32)
               + xs_ref[:, D + 2 * LANES:].astype(F32))
        lane = lax.broadcasted_iota(jnp.int32, cmb.shape, 1)
        first = tg_ref[j] * EPG + pair * EXPERTS_PER_STEP
        hidden = []
        for u in range(EXPERTS_PER_STEP):
            a = _dot(xt, w1c_ref[pair, u])
            he = (a * _sigmoid(a)) * _dot(xt, w3c_ref[pair, u])
            ce = jnp.sum(jnp.where(lane == first + u, cmb, 0.0), axis=-1, keepdims=True)
            hidden.append((he * ce).astype(BF))
        y = _dot(jnp.concatenate(hidden, axis=1), w2c_ref[pair])

        @pl.when(pair == 0)
        def _():
            acc_ref[...] = y

        @pl.when(pair == EPG // EXPERTS_PER_STEP - 1)
        def _():
            o_ref[...] = (acc_ref[...] + y).astype(BF)

    @pl.when((j >= nt_ref[0]) & (pair == EPG // EXPERTS_PER_STEP - 1))
    def _():
        o_ref[...] = jnp.zeros_like(o_ref)


def _experts(tile_block, tile_group, n_tiles, xs, wts):
    m = xs.shape[0]
    steps = EPG // EXPERTS_PER_STEP
    assert steps == 2, "the kernel keeps one partial sum: first step stores it, second adds and writes"

    def w_idx(j, e, tb, tg, nt):
        needed = (j < nt[0]) & _opens_group(j, tg)
        return (tg[j] * steps + jnp.where(needed, e, steps - 1), 0, 0)

    return pl.pallas_call(
        _moe_kernel,
        grid_spec=pltpu.PrefetchScalarGridSpec(
            num_scalar_prefetch=3,
            grid=(m // TMOE, steps),
            in_specs=[pl.BlockSpec((TMOE, XW), lambda j, e, tb, tg, nt: (tb[j], 0)),
                      pl.BlockSpec((EXPERTS_PER_STEP, D, DE), w_idx),
                      pl.BlockSpec((EXPERTS_PER_STEP, D, DE), w_idx),
                      pl.BlockSpec((EXPERTS_PER_STEP, DE, D), w_idx)],
            out_specs=pl.BlockSpec((TMOE, D), lambda j, e, tb, tg, nt: (j, 0)),
            scratch_shapes=[pltpu.VMEM((TMOE, D), F32),
                            pltpu.VMEM((steps, EXPERTS_PER_STEP, D, DE), BF),
                            pltpu.VMEM((steps, EXPERTS_PER_STEP, D, DE), BF),
                            pltpu.VMEM((steps, EXPERTS_PER_STEP * DE, D), BF)],
        ),
        out_shape=jax.ShapeDtypeStruct((m, D), BF),
        compiler_params=_cparams(("arbitrary", "arbitrary")),
        name="experts",
    )(tile_block, tile_group, n_tiles, xs, wts["exp_w1"], wts["exp_w3"], wts["exp_w2"])


def _combine_kernel(local_ref, far_ref, len_ref, x1_ref, rinfo_ref, mod_ref, ys_ref, o_ref,
                    buf_ref, sem, *, n_steps):
    i = pl.program_id(0)

    def runs(step, slot, action):
        def copy(dst, src, size):
            return pltpu.make_async_copy(ys_ref.at[pl.ds(src, size)],
                                         buf_ref.at[slot, pl.ds(dst, size)], sem.at[slot])

        _run_copies(local_ref, far_ref, len_ref, step, copy, action)

    @pl.when(i == 0)
    def _():
        buf_ref[...] = jnp.zeros_like(buf_ref)
        runs(0, 0, lambda c: c.start())

    @pl.when(i + 1 < n_steps)
    def _():
        runs(i + 1, (i + 1) % 2, lambda c: c.start())

    slot = i % 2
    runs(i, slot, lambda c: c.wait())
    lpos = rinfo_ref[:, 1:2]
    s_i = lax.broadcasted_iota(jnp.int32, (TM, TS), 1)
    from_sorted = jnp.where(s_i.astype(F32) == lpos, 1.0, 0.0).astype(BF)
    moe = _dot(from_sorted, buf_ref[slot])
    o_ref[...] = x1_ref[...] + mod_ref[0, 5:6, :] * moe


def _combine(tables, x1, rinfo, mod, mod_row, ys):
    n = x1.shape[0]
    n_steps = n // TM
    return pl.pallas_call(
        functools.partial(_combine_kernel, n_steps=n_steps),
        grid_spec=pltpu.PrefetchScalarGridSpec(
            num_scalar_prefetch=3,
            grid=(n_steps,),
            in_specs=[pl.BlockSpec((TM, D), lambda i, *_: (i, 0)),
                      pl.BlockSpec((TM, LANES), lambda i, *_: (i, 0)),
                      pl.BlockSpec((1, 6, D), lambda i, *_: (mod_row(i), 0, 0)),
                      pl.BlockSpec(memory_space=pl.ANY)],
            out_specs=pl.BlockSpec((TM, D), lambda i, *_: (i, 0)),
            scratch_shapes=[pltpu.VMEM((2, TS, D), BF), pltpu.SemaphoreType.DMA((2,))],
        ),
        out_shape=jax.ShapeDtypeStruct((n, D), F32),
        compiler_params=_cparams(("arbitrary",)),
        name="combine",
    )(*tables, x1, rinfo, mod, ys)


def _run_lengths(tile_counts):
    counts = tile_counts[:, 0, :NG].astype(jnp.int32)
    return ((counts + RUN_ALIGN - 1) // RUN_ALIGN) * RUN_ALIGN


def _run_tables(lengths, first_far):
    local = jnp.cumsum(lengths, axis=1) - lengths
    far = first_far[None, :] + jnp.cumsum(lengths, axis=0) - lengths
    flat = lambda a: a.astype(jnp.int32).reshape(-1)
    return flat(local), flat(far), flat(lengths)


def _group_layout(counts, max_tiles):
    padded = ((counts + TMOE - 1) // TMOE) * TMOE
    ends = jnp.cumsum(padded)
    offsets = ends - padded
    n_tiles = (ends[-1] // TMOE).astype(jnp.int32)
    tile = jnp.minimum(jnp.arange(max_tiles, dtype=jnp.int32), jnp.maximum(n_tiles - 1, 0))
    tile_group = jnp.sum((tile[:, None] * TMOE >= ends[None, :]).astype(jnp.int32), axis=1)
    return offsets, tile, tile_group, n_tiles.reshape(1)


def _pad_heads(w, perm=None, rotary_only=False):
    lead = w.shape[:-1]
    per = w.shape[-1] // NH
    w = w.reshape(lead + (NH, per))
    if perm is not None:
        nope = jnp.zeros_like(w[..., :NOPE]) if rotary_only else w[..., :NOPE]
        w = jnp.concatenate([nope, w[..., NOPE:][..., perm]], axis=-1)
    w = jnp.pad(w, [(0, 0)] * len(lead) + [(0, 0), (0, HP - per)])
    return w.reshape(lead + (NH * HP,))


def _pad_gain(g, perm, rotary_only=False):
    nope = jnp.zeros((NOPE,), F32) if rotary_only else g[:NOPE]
    g = jnp.concatenate([nope, g[NOPE:][perm], jnp.zeros((HP - QK,), F32)])
    return g.reshape(1, HP)


def _block_diag(w):
    per = BDW // LRU_BLOCK
    w = w.reshape(DR // BDW, per, LRU_BLOCK, LRU_BLOCK)
    bd = jnp.einsum("jpab,pq->jpaqb", w, jnp.eye(per, dtype=w.dtype))
    return bd.reshape(DR // BDW, BDW, BDW)


def _prepare_shared(l, p):
    w_in = lambda a, b: jnp.transpose(lax.slice(p["w_in"], (l, 0, a), (l + 1, D, b)).reshape(D, b - a))
    o1, o2, o3, o4, o5 = DR, 2 * DR, 2 * DR + QL, 2 * DR + QL + KVL, 2 * DR + QL + KVL + ROPE
    bd = jnp.stack([_block_diag(p["lru_wa"][l, 0]), _block_diag(p["lru_wx"][l, 0]),
                    _block_diag(p["lru_wa"][l, 1]), _block_diag(p["lru_wx"][l, 1])])
    bd = (0.5 * bd).astype(BF)
    wom = p["w_o_mla"][l].reshape(NH, VD, D)
    wom = jnp.pad(wom, ((0, 0), (0, HP - VD), (0, 0))).reshape(DH, D)
    router_w = jnp.concatenate([p["router_we"][l], p["router_wg"][l],
                                jnp.zeros((D, LANES - NE - NG), F32)], axis=1)
    router_b = jnp.concatenate([p["router_be"][l], p["router_bg"][l],
                                jnp.zeros((LANES - NE - NG,), F32)]).reshape(1, LANES)
    router_hi = router_w.astype(BF)
    router_lo = (router_w - router_hi.astype(F32)).astype(BF)
    router_w = jnp.concatenate([router_hi, router_lo], axis=1)
    return {
        "n1": p["norm1_g"][l].reshape(1, D), "n2": p["norm2_g"][l].reshape(1, D),
        "w_x": w_in(0, o1).astype(BF), "w_g": w_in(o1, o2).astype(BF),
        "w_q": w_in(o2, o3).astype(BF), "w_kv": w_in(o3, o4), "w_kr": w_in(o4, o5),
        "w_gate": (0.5 * w_in(o5, o5 + 2 * D)).astype(BF),
        "qan": p["q_a_norm"][l].reshape(1, QL), "kvan": p["kv_a_norm"][l].reshape(1, KVL),
        "w_uk": _pad_heads(p["w_uk"][l]).astype(BF),
        "w_uv": _pad_heads(p["w_uv"][l]).astype(BF),
        "conv_w": p["conv_w"][l], "conv_b": p["conv_b"][l].reshape(1, DR), "bd": bd,
        "lru_ba": 0.5 * p["lru_ba"][l], "lru_bx": 0.5 * p["lru_bx"][l], "lru_lam": p["lru_lam"][l],
        "w_o_rnn": p["w_o_rnn"][l].astype(BF), "w_o_mla": wom.astype(BF), "w_out": (0.5 * p["w_out"][l]).astype(BF),
        "router_w": router_w, "router_b": router_b,
        "exp_w1": p["exp_w1"][l], "exp_w3": p["exp_w3"][l], "exp_w2": p["exp_w2"][l],
    }


def _with_rope_order(l, p, shared, perm, rotary):
    w = dict(shared)
    zeros = lambda n: jnp.zeros((n, D), F32)
    rope_block = lambda order: [zeros(NOPE), shared["w_kr"][order, :], zeros(HP - QK)]
    kvr = [shared["w_kv"]] + rope_block(perm)
    w["w_uq"] = _pad_heads(p["w_uq"][l], perm).astype(BF)
    w["gq"] = _pad_gain(p["q_norm"][l], perm)
    w["gk"] = _pad_gain(p["k_norm"][l], perm)
    if rotary:
        pair = np.concatenate([perm[ROPE // 2:], perm[:ROPE // 2]])
        kvr += rope_block(pair)
        w["w_uq_pair"] = _pad_heads(p["w_uq"][l], pair, rotary_only=True).astype(BF)
        w["gq_pair"] = _pad_gain(p["q_norm"][l], pair, rotary_only=True)
        w["gk_pair"] = _pad_gain(p["k_norm"][l], pair, rotary_only=True)
    w["w_kvr"] = jnp.concatenate(kvr, axis=0).astype(BF)
    return w


def _rope_tables(n_tokens):
    rows = n_tokens // GRID_W
    row = np.repeat(np.arange(rows), GRID_W).astype(np.float32)
    col = np.tile(np.arange(GRID_W), rows).astype(np.float32)
    axis_dim = ROPE // 2
    inv = (np.float32(ROPE_BASE) ** (-np.arange(0, axis_dim, 2, dtype=np.float32) / axis_dim)).astype(np.float32)
    ang = np.concatenate([row[:, None] * inv, col[:, None] * inv], axis=-1).astype(np.float32)
    cos, sin = np.cos(ang), np.sin(ang)
    ones = lambda n: np.ones((n_tokens, n), np.float32)
    zeros = lambda n: np.zeros((n_tokens, n), np.float32)
    cos_t = np.concatenate([ones(NOPE), cos, cos, ones(HP - QK)], axis=1)
    sin_t = np.concatenate([zeros(NOPE), -sin, sin, zeros(HP - QK)], axis=1)
    return jnp.asarray(cos_t, F32), jnp.asarray(sin_t, F32)


def kernel(x_prompt, x_sample, cache_mla_ckv, cache_mla_krope, state_rglru, c, c_ctx, norm1_g, norm2_g, w_mod, b_mod, w_in, conv_w, conv_b, lru_wa, lru_ba, lru_wx, lru_bx, lru_lam, q_a_norm, kv_a_norm, w_uq, w_uk, w_uv, q_norm, k_norm, w_o_rnn, w_o_mla, w_out, router_wg, router_bg, router_we, router_be, exp_w1, exp_w3, exp_w2):
    p = dict(norm1_g=norm1_g, norm2_g=norm2_g, w_in=w_in, conv_w=conv_w, conv_b=conv_b,
             lru_wa=lru_wa, lru_ba=lru_ba, lru_wx=lru_wx, lru_bx=lru_bx, lru_lam=lru_lam,
             q_a_norm=q_a_norm, kv_a_norm=kv_a_norm, w_uq=w_uq, w_uk=w_uk, w_uv=w_uv,
             q_norm=q_norm, k_norm=k_norm, w_o_rnn=w_o_rnn, w_o_mla=w_o_mla, w_out=w_out,
             router_wg=router_wg, router_bg=router_bg, router_we=router_we, router_be=router_be,
             exp_w1=exp_w1, exp_w3=exp_w3, exp_w2=exp_w2)
    depth = w_in.shape[0]
    nb, seq, _ = x_prompt.shape
    db, dseq, _ = x_sample.shape
    ident = np.arange(ROPE)
    halves = np.concatenate([np.arange(0, ROPE, 2), np.arange(1, ROPE, 2)])
    rope_tabs = _rope_tables(dseq)
    cond8 = jnp.concatenate([c_ctx[None, :], c, jnp.zeros((SUBLANES - 1 - db, D), F32)], axis=0)
    ctx_row = lambda tile_rows: (lambda i: 0)
    lat_row = lambda tile_rows: (lambda i: (i * tile_rows) // dseq + 1)
    n_ctx, n_lat = nb * seq, db * dseq
    run_padding = ((n_ctx + n_lat) // TM) * NG * (RUN_ALIGN - 1)
    max_tiles = -(-(n_ctx + n_lat + run_padding) // TMOE) + NG
    per_seq = lambda arrs, b, t: [a.reshape(b, t, a.shape[-1]) for a in arrs]
    flat = lambda a: a.reshape(-1, a.shape[-1])

    y_prompt, y_sample = x_prompt.reshape(n_ctx, D), x_sample.reshape(n_lat, D)
    ckv_list, krope_list, rnn_list = [], [], []
    for l in range(depth):
        shared = _prepare_shared(l, p)
        w_ctx = _with_rope_order(l, p, shared, ident, False)
        w_lat = _with_rope_order(l, p, shared, halves, True)
        mod = _modulation(cond8, w_mod[l], b_mod[l]).reshape(SUBLANES, 6, D)

        h, xr, gg, q, k, v, ckv, kro = _projections(y_prompt, mod, ctx_row(TM), w_ctx, None, True)
        xr, gg, q, k, v = per_seq([xr, gg, q, k, v], nb, seq)
        yr, h_fin = _rglru(xr, gg, shared, None, True)
        ya = _attention(q, k, v, None, None, 4, NH)
        x1_c, hx_c, ri_c, tc_c = _merge_out(y_prompt, h, flat(yr), flat(ya), mod, ctx_row(TM), shared)
        ckv_list.append(ckv.reshape(nb, seq, KVL))
        krope_list.append(kro.reshape(nb, seq, ROPE))
        rnn_list.append(h_fin)

        krp_cache = jnp.pad(cache_mla_krope[:, l][..., halves], ((0, 0), (0, 0), (NOPE, HP - QK)))
        kc, vc = _cache_keys_values(cache_mla_ckv[:, l], krp_cache, w_lat)
        h, xr, gg, q, k, v = _projections(y_sample, mod, lat_row(TM), w_lat, rope_tabs, False)
        xr, gg, q, k, v = per_seq([xr, gg, q, k, v], db, dseq)
        yr, _ = _rglru(xr, gg, shared, state_rglru[:, l], False)
        ya = _attention(q, k, v, kc, vc, 1, 4)
        x1_l, hx_l, ri_l, tc_l = _merge_out(y_sample, h, flat(yr), flat(ya), mod, lat_row(TM), shared)

        len_c, len_l = _run_lengths(tc_c), _run_lengths(tc_l)
        rows_c = jnp.sum(len_c, axis=0)
        offsets, tile_block, tile_group, n_tiles = _group_layout(rows_c + jnp.sum(len_l, axis=0), max_tiles)
        runs_c = _run_tables(len_c, offsets)
        runs_l = _run_tables(len_l, offsets + rows_c)
        xs = jnp.zeros((max_tiles * TMOE, XW), BF)
        xs = _dispatch(runs_c, hx_c, xs)
        xs = _dispatch(runs_l, hx_l, xs)
        ys = _experts(tile_block, tile_group, n_tiles, xs, shared)
        y_prompt = _combine(runs_c, x1_c, ri_c, mod, ctx_row(TM), ys)
        y_sample = _combine(runs_l, x1_l, ri_l, mod, lat_row(TM), ys)

    y_prompt, y_sample = y_prompt.reshape(nb, seq, D), y_sample.reshape(db, dseq, D)

    return (y_prompt, y_sample, jnp.stack(ckv_list, axis=1), jnp.stack(krope_list, axis=1),
            jnp.stack(rnn_list, axis=1))
```

```python
import functools
import math

import numpy as np
import jax
import jax.numpy as jnp
from jax import lax
from jax.experimental import pallas as pl
from jax.experimental.pallas import tpu as pltpu

D = 1024
DR = 1024
QL = 384
KVL = 256
NH = 8
NOPE = 64
ROPE = 32
QK = NOPE + ROPE
VD = 64
HP = 128
DH = NH * HP
GRID_W = 64
ROPE_BASE = 10000.0
EPS = 1e-6
TINY = 1e-30
LRU_C = 8.0
LRU_BLOCK = 64
BDW = 256
CH = 1024
NG = 4
EPG = 4
NE = NG * EPG
DE = 512
LANES = 128
SUBLANES = 8
TM = 512
TS = 640
TMOE = 512
EXPERTS_PER_STEP = 2
XW = D + 3 * LANES
RUN_ALIGN = 16
VMEM_LIMIT = 52 * 1024 * 1024
BF = jnp.bfloat16
F32 = jnp.float32


def _cparams(sem):
    return pltpu.CompilerParams(dimension_semantics=sem, vmem_limit_bytes=VMEM_LIMIT)


def _dot(a, b):
    return jnp.dot(a, b, preferred_element_type=F32)


def _dot_nt(a, b):
    return lax.dot_general(a, b, (((1,), (1,)), ((), ())), preferred_element_type=F32)


def _rms(x, g, width):
    ms = jnp.sum(x * x, axis=-1, keepdims=True) * (1.0 / width)
    return x * lax.rsqrt(ms + EPS) * g


def _modulated_norm(x, g, scale, shift):
    return _rms(x, g * (1.0 + scale), D) + shift


def _mod_kernel(c_ref, w_ref, b_ref, o_ref):
    c = c_ref[...]
    s = c * jax.nn.sigmoid(c)
    o_ref[...] = _dot(s, w_ref[...]) + b_ref[...]


def _modulation(cond8, w_mod, b_mod):
    n = w_mod.shape[1]
    return pl.pallas_call(
        _mod_kernel,
        grid=(n // D,),
        in_specs=[
            pl.BlockSpec((SUBLANES, D), lambda j: (0, 0)),
            pl.BlockSpec((D, D), lambda j: (0, j)),
            pl.BlockSpec((1, D), lambda j: (0, j)),
        ],
        out_specs=pl.BlockSpec((SUBLANES, D), lambda j: (0, j)),
        out_shape=jax.ShapeDtypeStruct((SUBLANES, n), F32),
        compiler_params=_cparams(("arbitrary",)),
        name="modulation",
    )(cond8, w_mod, b_mod.reshape(1, n))


def _head_norm(xh, gain, cos=None, partner_scaled=None):
    ms = jnp.sum(xh * xh, axis=-1, keepdims=True) * (1.0 / QK)
    rs = lax.rsqrt(ms + EPS)
    y = xh * rs * gain
    if cos is None:
        return y
    return y * cos + partner_scaled * rs


def _keys_values(ckv, krp, wuk_ref, wuv_ref, gk, cos, partner_scaled, k_ref, v_ref):
    cb = ckv.astype(BF)
    kn = _dot(cb, wuk_ref[...])
    v_ref[...] = _dot(cb, wuv_ref[...]).astype(BF)
    for h in range(NH):
        kh = kn[:, h * HP:(h + 1) * HP] + krp
        k_ref[:, h * HP:(h + 1) * HP] = _head_norm(kh, gk, cos, partner_scaled).astype(BF)


def _proj_kernel(*refs, rope, emit_cache):
    it = iter(refs)
    x_ref, mod_ref, n1_ref = next(it), next(it), next(it)
    wx_ref, wg_ref, wq_ref, wkvr_ref = next(it), next(it), next(it), next(it)
    qan_ref, kvan_ref, wuq_ref, gq_ref = next(it), next(it), next(it), next(it)
    wuk_ref, gk_ref, wuv_ref = next(it), next(it), next(it)
    if rope:
        wuqs_ref, gqs_ref, gks_ref, cos_ref, sins_ref = (next(it) for _ in range(5))
    h_ref, xr_ref, gg_ref, q_ref, k_ref, v_ref = (next(it) for _ in range(6))
    if emit_cache:
        ckv_ref, kro_ref = next(it), next(it)

    hb = _modulated_norm(x_ref[...], n1_ref[...], mod_ref[0, 1:2, :], mod_ref[0, 0:1, :]).astype(BF)
    h_ref[...] = hb
    xr_ref[...] = _dot_nt(hb, wx_ref[...]).astype(BF)
    gg_ref[...] = jax.nn.gelu(_dot_nt(hb, wg_ref[...])).astype(BF)

    qnb = _rms(_dot_nt(hb, wq_ref[...]), qan_ref[...], QL).astype(BF)
    q = _dot(qnb, wuq_ref[...])
    gq = gq_ref[...]
    cos = q_partner = q_pair_scale = None
    if rope:
        cos, sins = cos_ref[...], sins_ref[...]
        q_partner = _dot(qnb, wuqs_ref[...])
        q_pair_scale = gqs_ref[...] * sins
    for hd in range(NH):
        cols = slice(hd * HP, (hd + 1) * HP)
        partner = q_partner[:, cols] * q_pair_scale if rope else None
        q_ref[:, cols] = _head_norm(q[:, cols], gq, cos, partner).astype(BF)

    kvr = _dot_nt(hb, wkvr_ref[...])
    ckv = _rms(kvr[:, :KVL], kvan_ref[...], KVL)
    krp = kvr[:, KVL:KVL + HP]
    k_partner = kvr[:, KVL + HP:KVL + 2 * HP] * (gks_ref[...] * sins) if rope else None
    if emit_cache:
        ckv_ref[...] = ckv
        kro_ref[...] = krp[:, NOPE:NOPE + ROPE]
    _keys_values(ckv, krp, wuk_ref, wuv_ref, gk_ref[...], cos, k_partner, k_ref, v_ref)


def _resident(shape):
    return pl.BlockSpec(shape, lambda i: (0,) * len(shape), pipeline_mode=pl.Buffered(1))


def _projections(x, mod, mod_row, wts, rope_tabs, emit_cache):
    n = x.shape[0]
    rope = rope_tabs is not None
    tile = lambda w: pl.BlockSpec((TM, w), lambda i: (i, 0))
    in_specs = [
        tile(D),
        pl.BlockSpec((1, 6, D), lambda i: (mod_row(i), 0, 0)),
        _resident((1, D)),
        _resident((DR, D)), _resident((DR, D)), _resident((QL, D)), _resident(wts["w_kvr"].shape),
        _resident((1, QL)), _resident((1, KVL)), _resident((QL, DH)), _resident((1, HP)),
        _resident((KVL, DH)), _resident((1, HP)), _resident((KVL, DH)),
    ]
    args = [x, mod, wts["n1"], wts["w_x"], wts["w_g"], wts["w_q"], wts["w_kvr"],
            wts["qan"], wts["kvan"], wts["w_uq"], wts["gq"], wts["w_uk"], wts["gk"], wts["w_uv"]]
    if rope:
        tiles_per_seq = rope_tabs[0].shape[0] // TM
        in_specs += [_resident((QL, DH)), _resident((1, HP)), _resident((1, HP))]
        in_specs += [pl.BlockSpec((TM, HP), lambda i: (i % tiles_per_seq, 0))] * 2
        args += [wts["w_uq_pair"], wts["gq_pair"], wts["gk_pair"]] + list(rope_tabs)
    out_specs = [tile(D), tile(DR), tile(DR), tile(DH), tile(DH), tile(DH)]
    out_shape = [jax.ShapeDtypeStruct((n, D), BF), jax.ShapeDtypeStruct((n, DR), BF),
                 jax.ShapeDtypeStruct((n, DR), BF), jax.ShapeDtypeStruct((n, DH), BF),
                 jax.ShapeDtypeStruct((n, DH), BF), jax.ShapeDtypeStruct((n, DH), BF)]
    if emit_cache:
        out_specs += [tile(KVL), tile(ROPE)]
        out_shape += [jax.ShapeDtypeStruct((n, KVL), F32), jax.ShapeDtypeStruct((n, ROPE), F32)]
    return pl.pallas_call(
        functools.partial(_proj_kernel, rope=rope, emit_cache=emit_cache),
        grid=(n // TM,),
        in_specs=in_specs,
        out_specs=out_specs,
        out_shape=out_shape,
        compiler_params=_cparams(("arbitrary",)),
        name="projections",
    )(*args)


def _cache_kv_kernel(ckv_ref, krp_ref, wuk_ref, gk_ref, wuv_ref, k_ref, v_ref):
    _keys_values(ckv_ref[0], krp_ref[0], wuk_ref, wuv_ref, gk_ref[...], None, None,
                 k_ref.at[0], v_ref.at[0])


def _cache_keys_values(ckv, krp, wts):
    b, s, _ = ckv.shape
    full = lambda shape: pl.BlockSpec(shape, lambda i: (0,) * len(shape))
    return pl.pallas_call(
        _cache_kv_kernel,
        grid=(b,),
        in_specs=[pl.BlockSpec((1, s, KVL), lambda i: (i, 0, 0)),
                  pl.BlockSpec((1, s, HP), lambda i: (i, 0, 0)),
                  full((KVL, DH)), full((1, HP)), full((KVL, DH))],
        out_specs=[pl.BlockSpec((1, s, DH), lambda i: (i, 0, 0))] * 2,
        out_shape=[jax.ShapeDtypeStruct((b, s, DH), BF)] * 2,
        compiler_params=_cparams(("arbitrary",)),
        name="cache_keys_values",
    )(ckv, krp, wts["w_uk"], wts["gk"], wts["w_uv"])


WIN = 256
SEG = WIN // 8


def _window_permutation():
    dst = np.arange(WIN)
    p = np.zeros((WIN, WIN), np.float32)
    p[dst, (dst % SUBLANES) * SEG + dst // SUBLANES] = 1.0
    return p


def _sigmoid(x):
    return 0.5 * jnp.tanh(0.5 * x) + 0.5


def _segment_pass(a_scr, b_scr, bases, inits, out_scr=None):
    def body(k, carry):
        new = []
        for d in range(2):
            h, p = carry[d]
            i = k if d == 0 else SEG - 1 - k
            rows = pl.ds(pl.multiple_of(bases[d] + i * SUBLANES, SUBLANES), SUBLANES)
            a = a_scr[d, rows, :]
            h = a * h + b_scr[d, rows, :]
            if out_scr is None:
                p = a * p
            else:
                out_scr[d, rows, :] = h
            new.append((h, p))
        return tuple(new)

    init = tuple((inits[d], jnp.ones_like(inits[d])) for d in range(2))
    return lax.fori_loop(0, SEG, body, init, unroll=4)


def _segment_entries(end, decay, carry_in, forward):
    order = range(SUBLANES) if forward else reversed(range(SUBLANES))
    rows = [None] * SUBLANES
    c = carry_in
    for s in order:
        rows[s] = c
        c = end[s:s + 1, :] + decay[s:s + 1, :] * c
    return jnp.concatenate(rows, axis=0), c


def _rglru_kernel(*refs, t, has_h0, emit_state):
    it = iter(refs)
    xr_ref, gg_ref, perm_ref, unperm_ref = (next(it) for _ in range(4))
    cw_ref, cb_ref, bd_ref, ba_ref, bx_ref, lam_ref = (next(it) for _ in range(6))
    h0_ref = next(it) if has_h0 else None
    y_ref = next(it)
    hf_ref = next(it) if emit_state else None
    a_scr, b_scr, h_scr = (next(it) for _ in range(3))

    n_win = t // WIN
    sub = lax.broadcasted_iota(jnp.int32, (SUBLANES, CH), 0)
    zero_row = jnp.zeros((1, CH), F32)
    edge = 2 * SUBLANES
    for w in range(n_win):
        lo, hi = w * WIN, (w + 1) * WIN
        xp = _dot(perm_ref[...], xr_ref[0, lo:hi, :])
        before = xr_ref[0, lo - edge:lo, :].astype(F32)[edge - 1:edge, :] if w > 0 else zero_row
        after = xr_ref[0, hi:hi + edge, :].astype(F32) if w < n_win - 1 else None
        after0 = after[0:1, :] if after is not None else zero_row
        after1 = after[1:2, :] if after is not None else zero_row
        tile_m1 = jnp.where(sub == 0, before, pltpu.roll(xp[WIN - SUBLANES:WIN, :], 1, 0))
        tile_p0 = jnp.where(sub == SUBLANES - 1, after0, pltpu.roll(xp[0:SUBLANES, :], SUBLANES - 1, 0))
        tile_p1 = jnp.where(sub == SUBLANES - 1, after1,
                            pltpu.roll(xp[SUBLANES:2 * SUBLANES, :], SUBLANES - 1, 0))
        xe = jnp.concatenate([tile_m1, xp, tile_p0, tile_p1], axis=0)
        xc = cb_ref[...]
        for tap in range(4):
            xc = xc + xe[tap * SUBLANES:tap * SUBLANES + WIN, :] * cw_ref[tap:tap + 1, :]
        for s in range(CH // BDW):
            cols = slice(s * BDW, (s + 1) * BDW)
            xs = xc[:, cols]
            xsb = xs.astype(BF)
            xh = 0.5 * xs
            for d in range(2):
                tr = jnp.tanh(_dot(xsb, bd_ref[2 * d, s]) + ba_ref[d:d + 1, cols])
                ti = jnp.tanh(_dot(xsb, bd_ref[2 * d + 1, s]) + bx_ref[d:d + 1, cols])
                nl = -lam_ref[d:d + 1, cols]
                softplus = jnp.maximum(nl, 0.0) + jnp.log(1.0 + jnp.exp(-jnp.abs(nl)))
                ch = (-0.5 * LRU_C) * softplus
                a = jnp.exp(tr * ch + ch)
                z = 1.0 - a * a
                root = z * lax.rsqrt(jnp.maximum(z, TINY))
                a_scr[d, lo:hi, cols] = a
                b_scr[d, lo:hi, cols] = root * (ti * xh + xh)

    if has_h0:
        carry = [h0_ref[0, 0:1, :], h0_ref[0, 1:2, :]]
    else:
        carry = [zero_row, zero_row]
    zeros = jnp.zeros((SUBLANES, CH), F32)
    for k in range(n_win):
        bases = (k * WIN, (n_win - 1 - k) * WIN)
        totals = _segment_pass(a_scr, b_scr, bases, (zeros, zeros))
        entries = []
        for d in range(2):
            entry, carry[d] = _segment_entries(totals[d][0], totals[d][1], carry[d], d == 0)
            entries.append(entry)
        _segment_pass(a_scr, b_scr, bases, entries, out_scr=h_scr)

    if emit_state:
        hf_ref[0, 0:1, :] = carry[0]
        hf_ref[0, 1:2, :] = carry[1]
    for w in range(n_win):
        lo, hi = w * WIN, (w + 1) * WIN
        gate = _dot(perm_ref[...], gg_ref[0, lo:hi, :])
        yp = ((h_scr[0, lo:hi, :] + h_scr[1, lo:hi, :]) * gate).astype(BF)
        y_ref[0, lo:hi, :] = _dot(unperm_ref[...], yp).astype(BF)


def _rglru(xr, gg, wts, h0, emit_state):
    b, t, _ = xr.shape
    nc = DR // CH
    has_h0 = h0 is not None
    chunk = lambda r: pl.BlockSpec((r, CH), lambda i, j: (0, j))
    seq = pl.BlockSpec((1, t, CH), lambda i, j: (i, 0, j))
    state = pl.BlockSpec((1, 2, CH), lambda i, j: (i, 0, j))
    window = pl.BlockSpec((WIN, WIN), lambda i, j: (0, 0))
    perm = _window_permutation()
    in_specs = [seq, seq, window, window, chunk(4), chunk(1),
                pl.BlockSpec((4, CH // BDW, BDW, BDW), lambda i, j: (0, j, 0, 0)),
                chunk(2), chunk(2), chunk(2)]
    args = [xr, gg, jnp.asarray(perm, BF), jnp.asarray(perm.T, BF),
            wts["conv_w"], wts["conv_b"], wts["bd"], wts["lru_ba"], wts["lru_bx"], wts["lru_lam"]]
    if has_h0:
        in_specs.append(state)
        args.append(h0)
    out_specs = [seq]
    out_shape = [jax.ShapeDtypeStruct((b, t, DR), BF)]
    if emit_state:
        out_specs.append(state)
        out_shape.append(jax.ShapeDtypeStruct((b, 2, DR), F32))
    res = pl.pallas_call(
        functools.partial(_rglru_kernel, t=t, has_h0=has_h0, emit_state=emit_state),
        grid=(b, nc),
        in_specs=in_specs,
        out_specs=out_specs,
        out_shape=out_shape,
        scratch_shapes=[pltpu.VMEM((2, t, CH), F32)] * 3,
        compiler_params=_cparams(("arbitrary", "arbitrary")),
        name="rglru",
    )(*args)
    return res if emit_state else (res[0], None)


def _attn_kernel(*refs, t, n_seqs, n_heads, has_ctx, q_block):
    it = iter(refs)
    q_ref, k_ref, v_ref = next(it), next(it), next(it)
    kc_ref = vc_ref = None
    if has_ctx:
        kc_ref, vc_ref = next(it), next(it)
    o_ref = next(it)
    log2_scale = (QK ** -0.5) * math.log2(math.e)
    for sq, hd in [(sq, hd) for sq in range(n_seqs) for hd in range(n_heads)]:
        cols = slice(hd * HP, (hd + 1) * HP)
        k = k_ref[sq, :, cols]
        v = v_ref[sq, :, cols]
        if has_ctx:
            kc = kc_ref[sq, :, cols]
            vc = vc_ref[sq, :, cols]
        for qb in range(t // q_block):
            rows = slice(qb * q_block, (qb + 1) * q_block)
            q = q_ref[sq, rows, cols]
            s = _dot_nt(q, k) * log2_scale
            m = jnp.max(s, axis=-1, keepdims=True)
            if has_ctx:
                sc = _dot_nt(q, kc) * log2_scale
                m = jnp.maximum(m, jnp.max(sc, axis=-1, keepdims=True))
            p = jnp.exp2(s - m)
            den = jnp.sum(p, axis=-1, keepdims=True)
            o = _dot(p.astype(BF), v)
            if has_ctx:
                pc = jnp.exp2(sc - m)
                den = den + jnp.sum(pc, axis=-1, keepdims=True)
                o = o + _dot(pc.astype(BF), vc)
            o_ref[sq, rows, cols] = (o / den).astype(BF)


def _attention(q, k, v, kc, vc, seqs_per_step, heads_per_step):
    b, t, _ = q.shape
    has_ctx = kc is not None
    w = heads_per_step * HP
    blk = lambda n: pl.BlockSpec((seqs_per_step, n, w), lambda i, j: (i, 0, j))
    in_specs = [blk(t), blk(t), blk(t)]
    args = [q, k, v]
    if has_ctx:
        in_specs += [blk(kc.shape[1])] * 2
        args += [kc, vc]
    return pl.pallas_call(
        functools.partial(_attn_kernel, t=t, n_seqs=seqs_per_step, n_heads=heads_per_step,
                          has_ctx=has_ctx, q_block=min(t, 256)),
        grid=(b // seqs_per_step, NH // heads_per_step),
        in_specs=in_specs,
        out_specs=blk(t),
        out_shape=jax.ShapeDtypeStruct((b, t, DH), BF),
        compiler_params=_cparams(("arbitrary", "arbitrary")),
        name="attention",
    )(*args)


def _route(logits):
    lane = lax.broadcasted_iota(jnp.int32, logits.shape, 1)
    lanef = lane.astype(F32)
    neg = -jnp.inf
    big = float(LANES)
    gl = jnp.where((lane >= NE) & (lane < NE + NG), logits, neg)
    gmax = jnp.max(gl, axis=-1, keepdims=True)
    gidx = jnp.min(jnp.where(gl == gmax, lanef, big), axis=-1, keepdims=True) - float(NE)
    gw = 1.0 / jnp.sum(jnp.exp(gl - gmax), axis=-1, keepdims=True)
    lo = gidx * float(EPG)
    el = jnp.where((lanef >= lo) & (lanef < lo + float(EPG)), logits, neg)
    v1 = jnp.max(el, axis=-1, keepdims=True)
    i1 = jnp.min(jnp.where(el == v1, lanef, big), axis=-1, keepdims=True)
    el2 = jnp.where(lanef == i1, neg, el)
    v2 = jnp.max(el2, axis=-1, keepdims=True)
    i2 = jnp.min(jnp.where(el2 == v2, lanef, big), axis=-1, keepdims=True)
    e2 = jnp.exp(v2 - v1)
    w1 = gw / (1.0 + e2)
    w2 = gw * e2 / (1.0 + e2)
    cmb = jnp.where(lanef == i1, w1, 0.0) + jnp.where(lanef == i2, w2, 0.0)
    return cmb, gidx


def _out_kernel(x_ref, h_ref, yr_ref, ya_ref, mod_ref, wgate_ref, wor_ref, wom_ref, wout_ref,
                n2_ref, rw_ref, rb_ref, x1_ref, hx_ref, rinfo_ref, tcnt_ref):
    x = x_ref[...]
    gl = _dot_nt(h_ref[...], wgate_ref[...])
    merged2 = ((jnp.tanh(gl[:, :D]) + 1.0) * _dot(yr_ref[...], wor_ref[...])
               + (jnp.tanh(gl[:, D:]) + 1.0) * _dot(ya_ref[...], wom_ref[...]))
    mix = _dot(merged2.astype(BF), wout_ref[...])
    x1 = x + mod_ref[0, 2:3, :] * mix
    x1_ref[...] = x1
    h2 = _modulated_norm(x1, n2_ref[...], mod_ref[0, 4:5, :], mod_ref[0, 3:4, :])
    h2_hi = h2.astype(BF)
    h2_lo = (h2 - h2_hi.astype(F32)).astype(BF)
    part = _dot(h2_hi, rw_ref[...])
    logits = part[:, :LANES] + part[:, LANES:] + _dot(h2_lo, rw_ref[:, :LANES]) + rb_ref[...]
    cmb, gidx = _route(logits)

    lanef = lax.broadcasted_iota(jnp.int32, cmb.shape, 1).astype(F32)
    ghot = jnp.where(lanef == gidx, 1.0, 0.0)
    r_i = lax.broadcasted_iota(jnp.int32, (TM, TM), 0)
    c_i = lax.broadcasted_iota(jnp.int32, (TM, TM), 1)
    tri = jnp.where(r_i > c_i, 1.0, 0.0).astype(BF)
    earlier_same = jnp.sum(_dot(tri, ghot.astype(BF)) * ghot, axis=-1, keepdims=True)
    counts = jnp.sum(ghot, axis=0, keepdims=True)
    padded = jnp.floor((counts + (RUN_ALIGN - 1.0)) * (1.0 / RUN_ALIGN)) * RUN_ALIGN
    lower_groups = jnp.sum(jnp.where(lanef < gidx, padded, 0.0), axis=-1, keepdims=True)
    lpos = lower_groups + earlier_same
    s_i = lax.broadcasted_iota(jnp.int32, (TM, TS), 1)
    to_sorted = jnp.where(s_i.astype(F32) == lpos, 1.0, 0.0).astype(BF)
    c1 = cmb.astype(BF)
    c2 = (cmb - c1.astype(F32)).astype(BF)
    c3 = (cmb - c1.astype(F32) - c2.astype(F32)).astype(BF)
    payload = jnp.concatenate([h2.astype(BF), c1, c2, c3], axis=1)
    srt = lax.dot_general(to_sorted, payload, (((0,), (0,)), ((), ())), preferred_element_type=F32)
    hx_ref[...] = srt.astype(BF)

    rinfo_ref[...] = jnp.where(lanef == 0.0, gidx, jnp.where(lanef == 1.0, lpos, 0.0))
    tcnt_ref[0] = jnp.broadcast_to(counts, (SUBLANES, LANES))


def _merge_out(x, h, yr, ya, mod, mod_row, wts):
    n = x.shape[0]
    n_tiles = n // TM
    tile = lambda w: pl.BlockSpec((TM, w), lambda i: (i, 0))
    return pl.pallas_call(
        _out_kernel,
        grid=(n_tiles,),
        in_specs=[tile(D), tile(D), tile(DR), tile(DH),
                  pl.BlockSpec((1, 6, D), lambda i: (mod_row(i), 0, 0)),
                  _resident((2 * D, D)), _resident((DR, D)), _resident((DH, D)), _resident((D, D)),
                  _resident((1, D)), _resident((D, 2 * LANES)), _resident((1, LANES))],
        out_specs=[tile(D), pl.BlockSpec((TS, XW), lambda i: (i, 0)), tile(LANES),
                   pl.BlockSpec((1, SUBLANES, LANES), lambda i: (i, 0, 0))],
        out_shape=[jax.ShapeDtypeStruct((n, D), F32), jax.ShapeDtypeStruct((n_tiles * TS, XW), BF),
                   jax.ShapeDtypeStruct((n, LANES), F32),
                   jax.ShapeDtypeStruct((n_tiles, SUBLANES, LANES), F32)],
        compiler_params=_cparams(("arbitrary",)),
        name="merge_out",
    )(x, h, yr, ya, mod, wts["w_gate"], wts["w_o_rnn"], wts["w_o_mla"], wts["w_out"],
      wts["n2"], wts["router_w"], wts["router_b"])


def _run_copies(local_ref, far_ref, len_ref, tile, make_copy, action):
    for g in range(NG):
        idx = tile * NG + g
        n = len_ref[idx]
        local0 = local_ref[idx]
        far0 = far_ref[idx]
        for k in reversed(range(RUN_ALIGN.bit_length() - 1, TS.bit_length())):
            size = 1 << k
            done = (n >> (k + 1)) << (k + 1)

            @pl.when(((n >> k) & 1) == 1)
            def _():
                action(make_copy(pl.multiple_of(local0 + done, RUN_ALIGN),
                                 pl.multiple_of(far0 + done, RUN_ALIGN), size))


def _dispatch_kernel(local_ref, far_ref, len_ref, hx_ref, xs_in_ref, xs_ref, sem):
    del xs_in_ref

    def copy(src, dst, size):
        return pltpu.make_async_copy(hx_ref.at[pl.ds(src, size)], xs_ref.at[pl.ds(dst, size)], sem)

    tile = pl.program_id(0)
    _run_copies(local_ref, far_ref, len_ref, tile, copy, lambda c: c.start())
    _run_copies(local_ref, far_ref, len_ref, tile, copy, lambda c: c.wait())


def _dispatch(tables, hx, xs):
    n_tiles = hx.shape[0] // TS
    return pl.pallas_call(
        _dispatch_kernel,
        grid_spec=pltpu.PrefetchScalarGridSpec(
            num_scalar_prefetch=3,
            grid=(n_tiles,),
            in_specs=[pl.BlockSpec((TS, XW), lambda i, *_: (i, 0)),
                      pl.BlockSpec(memory_space=pl.ANY)],
            out_specs=pl.BlockSpec(memory_space=pl.ANY),
            scratch_shapes=[pltpu.SemaphoreType.DMA(())],
        ),
        out_shape=jax.ShapeDtypeStruct(xs.shape, xs.dtype),
        input_output_aliases={4: 0},
        compiler_params=_cparams(("arbitrary",)),
        name="dispatch",
    )(*tables, hx, xs)


def _opens_group(j, tg):
    return (j == 0) | (tg[j] != tg[jnp.maximum(j - 1, 0)])


def _moe_kernel(tb_ref, tg_ref, nt_ref, xs_ref, w1_ref, w3_ref, w2_ref, o_ref,
                acc_ref, w1c_ref, w3c_ref, w2c_ref):
    j = pl.program_id(0)
    pair = pl.program_id(1)

    @pl.when((j < nt_ref[0]) & _opens_group(j, tg_ref))
    def _():
        w1c_ref[pair] = w1_ref[...].astype(BF)
        w3c_ref[pair] = w3_ref[...].astype(BF)
        w2c_ref[pair] = w2_ref[...].reshape(EXPERTS_PER_STEP * DE, D).astype(BF)

    @pl.when(j < nt_ref[0])
    def _():
        xt = xs_ref[:, :D]
        cmb = (xs_ref[:, D:D + LANES].astype(F32) + xs_ref[:, D + LANES:D + 2 * LANES].astype(F32)
               + xs_ref[:, D + 2 * LANES:].astype(F32))
        lane = lax.broadcasted_iota(jnp.int32, cmb.shape, 1)
        first = tg_ref[j] * EPG + pair * EXPERTS_PER_STEP
        hidden = []
        for u in range(EXPERTS_PER_STEP):
            a = _dot(xt, w1c_ref[pair, u])
            he = (a * _sigmoid(a)) * _dot(xt, w3c_ref[pair, u])
            ce = jnp.sum(jnp.where(lane == first + u, cmb, 0.0), axis=-1, keepdims=True)
            hidden.append((he * ce).astype(BF))
        y = _dot(jnp.concatenate(hidden, axis=1), w2c_ref[pair])

        @pl.when(pair == 0)
        def _():
            acc_ref[...] = y

        @pl.when(pair == EPG // EXPERTS_PER_STEP - 1)
        def _():
            o_ref[...] = (acc_ref[...] + y).astype(BF)

    @pl.when((j >= nt_ref[0]) & (pair == EPG // EXPERTS_PER_STEP - 1))
    def _():
        o_ref[...] = jnp.zeros_like(o_ref)


def _experts(tile_block, tile_group, n_tiles, xs, wts):
    m = xs.shape[0]
    steps = EPG // EXPERTS_PER_STEP
    assert steps == 2, "the kernel keeps one partial sum: first step stores it, second adds and writes"

    def w_idx(j, e, tb, tg, nt):
        needed = (j < nt[0]) & _opens_group(j, tg)
        return (tg[j] * steps + jnp.where(needed, e, steps - 1), 0, 0)

    return pl.pallas_call(
        _moe_kernel,
        grid_spec=pltpu.PrefetchScalarGridSpec(
            num_scalar_prefetch=3,
            grid=(m // TMOE, steps),
            in_specs=[pl.BlockSpec((TMOE, XW), lambda j, e, tb, tg, nt: (tb[j], 0)),
                      pl.BlockSpec((EXPERTS_PER_STEP, D, DE), w_idx),
                      pl.BlockSpec((EXPERTS_PER_STEP, D, DE), w_idx),
                      pl.BlockSpec((EXPERTS_PER_STEP, DE, D), w_idx)],
            out_specs=pl.BlockSpec((TMOE, D), lambda j, e, tb, tg, nt: (j, 0)),
            scratch_shapes=[pltpu.VMEM((TMOE, D), F32),
                            pltpu.VMEM((steps, EXPERTS_PER_STEP, D, DE), BF),
                            pltpu.VMEM((steps, EXPERTS_PER_STEP, D, DE), BF),
                            pltpu.VMEM((steps, EXPERTS_PER_STEP * DE, D), BF)],
        ),
        out_shape=jax.ShapeDtypeStruct((m, D), BF),
        compiler_params=_cparams(("arbitrary", "arbitrary")),
        name="experts",
    )(tile_block, tile_group, n_tiles, xs, wts["exp_w1"], wts["exp_w3"], wts["exp_w2"])


def _combine_kernel(local_ref, far_ref, len_ref, x1_ref, rinfo_ref, mod_ref, ys_ref, o_ref,
                    buf_ref, sem, *, n_steps):
    i = pl.program_id(0)

    def runs(step, slot, action):
        def copy(dst, src, size):
            return pltpu.make_async_copy(ys_ref.at[pl.ds(src, size)],
                                         buf_ref.at[slot, pl.ds(dst, size)], sem.at[slot])

        _run_copies(local_ref, far_ref, len_ref, step, copy, action)

    @pl.when(i == 0)
    def _():
        buf_ref[...] = jnp.zeros_like(buf_ref)
        runs(0, 0, lambda c: c.start())

    @pl.when(i + 1 < n_steps)
    def _():
        runs(i + 1, (i + 1) % 2, lambda c: c.start())

    slot = i % 2
    runs(i, slot, lambda c: c.wait())
    lpos = rinfo_ref[:, 1:2]
    s_i = lax.broadcasted_iota(jnp.int32, (TM, TS), 1)
    from_sorted = jnp.where(s_i.astype(F32) == lpos, 1.0, 0.0).astype(BF)
    moe = _dot(from_sorted, buf_ref[slot])
    o_ref[...] = x1_ref[...] + mod_ref[0, 5:6, :] * moe


def _combine(tables, x1, rinfo, mod, mod_row, ys):
    n = x1.shape[0]
    n_steps = n // TM
    return pl.pallas_call(
        functools.partial(_combine_kernel, n_steps=n_steps),
        grid_spec=pltpu.PrefetchScalarGridSpec(
            num_scalar_prefetch=3,
            grid=(n_steps,),
            in_specs=[pl.BlockSpec((TM, D), lambda i, *_: (i, 0)),
                      pl.BlockSpec((TM, LANES), lambda i, *_: (i, 0)),
                      pl.BlockSpec((1, 6, D), lambda i, *_: (mod_row(i), 0, 0)),
                      pl.BlockSpec(memory_space=pl.ANY)],
            out_specs=pl.BlockSpec((TM, D), lambda i, *_: (i, 0)),
            scratch_shapes=[pltpu.VMEM((2, TS, D), BF), pltpu.SemaphoreType.DMA((2,))],
        ),
        out_shape=jax.ShapeDtypeStruct((n, D), F32),
        compiler_params=_cparams(("arbitrary",)),
        name="combine",
    )(*tables, x1, rinfo, mod, ys)


def _run_lengths(tile_counts):
    counts = tile_counts[:, 0, :NG].astype(jnp.int32)
    return ((counts + RUN_ALIGN - 1) // RUN_ALIGN) * RUN_ALIGN


def _run_tables(lengths, first_far):
    local = jnp.cumsum(lengths, axis=1) - lengths
    far = first_far[None, :] + jnp.cumsum(lengths, axis=0) - lengths
    flat = lambda a: a.astype(jnp.int32).reshape(-1)
    return flat(local), flat(far), flat(lengths)


def _group_layout(counts, max_tiles):
    padded = ((counts + TMOE - 1) // TMOE) * TMOE
    ends = jnp.cumsum(padded)
    offsets = ends - padded
    n_tiles = (ends[-1] // TMOE).astype(jnp.int32)
    tile = jnp.minimum(jnp.arange(max_tiles, dtype=jnp.int32), jnp.maximum(n_tiles - 1, 0))
    tile_group = jnp.sum((tile[:, None] * TMOE >= ends[None, :]).astype(jnp.int32), axis=1)
    return offsets, tile, tile_group, n_tiles.reshape(1)


def _pad_heads(w, perm=None, rotary_only=False):
    lead = w.shape[:-1]
    per = w.shape[-1] // NH
    w = w.reshape(lead + (NH, per))
    if perm is not None:
        nope = jnp.zeros_like(w[..., :NOPE]) if rotary_only else w[..., :NOPE]
        w = jnp.concatenate([nope, w[..., NOPE:][..., perm]], axis=-1)
    w = jnp.pad(w, [(0, 0)] * len(lead) + [(0, 0), (0, HP - per)])
    return w.reshape(lead + (NH * HP,))


def _pad_gain(g, perm, rotary_only=False):
    nope = jnp.zeros((NOPE,), F32) if rotary_only else g[:NOPE]
    g = jnp.concatenate([nope, g[NOPE:][perm], jnp.zeros((HP - QK,), F32)])
    return g.reshape(1, HP)


def _block_diag(w):
    per = BDW // LRU_BLOCK
    rows = w.reshape(DR // BDW, BDW, LRU_BLOCK)
    idx = np.arange(BDW) // LRU_BLOCK
    mask = jnp.asarray(idx[:, None] == idx[None, :], w.dtype)
    return jnp.concatenate([rows] * per, axis=-1) * mask


def _prepare_shared(l, p):
    w_in = lambda a, b: jnp.transpose(lax.slice(p["w_in"], (l, 0, a), (l + 1, D, b)).reshape(D, b - a))
    o1, o2, o3, o4, o5 = DR, 2 * DR, 2 * DR + QL, 2 * DR + QL + KVL, 2 * DR + QL + KVL + ROPE
    bd = jnp.stack([_block_diag(p["lru_wa"][l, 0]), _block_diag(p["lru_wx"][l, 0]),
                    _block_diag(p["lru_wa"][l, 1]), _block_diag(p["lru_wx"][l, 1])])
    bd = (0.5 * bd).astype(BF)
    wom = p["w_o_mla"][l].reshape(NH, VD, D)
    wom = jnp.pad(wom, ((0, 0), (0, HP - VD), (0, 0))).reshape(DH, D)
    router_w = jnp.concatenate([p["router_we"][l], p["router_wg"][l],
                                jnp.zeros((D, LANES - NE - NG), F32)], axis=1)
    router_b = jnp.concatenate([p["router_be"][l], p["router_bg"][l],
                                jnp.zeros((LANES - NE - NG,), F32)]).reshape(1, LANES)
    router_hi = router_w.astype(BF)
    router_lo = (router_w - router_hi.astype(F32)).astype(BF)
    router_w = jnp.concatenate([router_hi, router_lo], axis=1)
    return {
        "n1": p["norm1_g"][l].reshape(1, D), "n2": p["norm2_g"][l].reshape(1, D),
        "w_x": w_in(0, o1).astype(BF), "w_g": w_in(o1, o2).astype(BF),
        "w_q": w_in(o2, o3).astype(BF), "w_kv": w_in(o3, o4), "w_kr": w_in(o4, o5),
        "w_gate": (0.5 * w_in(o5, o5 + 2 * D)).astype(BF),
        "qan": p["q_a_norm"][l].reshape(1, QL), "kvan": p["kv_a_norm"][l].reshape(1, KVL),
        "w_uk": _pad_heads(p["w_uk"][l]).astype(BF),
        "w_uv": _pad_heads(p["w_uv"][l]).astype(BF),
        "conv_w": p["conv_w"][l], "conv_b": p["conv_b"][l].reshape(1, DR), "bd": bd,
        "lru_ba": 0.5 * p["lru_ba"][l], "lru_bx": 0.5 * p["lru_bx"][l], "lru_lam": p["lru_lam"][l],
        "w_o_rnn": p["w_o_rnn"][l].astype(BF), "w_o_mla": wom.astype(BF), "w_out": (0.5 * p["w_out"][l]).astype(BF),
        "router_w": router_w, "router_b": router_b,
        "exp_w1": p["exp_w1"][l], "exp_w3": p["exp_w3"][l], "exp_w2": p["exp_w2"][l],
    }


def _with_rope_order(l, p, shared, perm, rotary):
    w = dict(shared)
    zeros = lambda n: jnp.zeros((n, D), F32)
    rope_block = lambda order: [zeros(NOPE), shared["w_kr"][order, :], zeros(HP - QK)]
    kvr = [shared["w_kv"]] + rope_block(perm)
    w["w_uq"] = _pad_heads(p["w_uq"][l], perm).astype(BF)
    w["gq"] = _pad_gain(p["q_norm"][l], perm)
    w["gk"] = _pad_gain(p["k_norm"][l], perm)
    if rotary:
        pair = np.concatenate([perm[ROPE // 2:], perm[:ROPE // 2]])
        kvr += rope_block(pair)
        w["w_uq_pair"] = _pad_heads(p["w_uq"][l], pair, rotary_only=True).astype(BF)
        w["gq_pair"] = _pad_gain(p["q_norm"][l], pair, rotary_only=True)
        w["gk_pair"] = _pad_gain(p["k_norm"][l], pair, rotary_only=True)
    w["w_kvr"] = jnp.concatenate(kvr, axis=0).astype(BF)
    return w


def _rope_tables(n_tokens):
    rows = n_tokens // GRID_W
    row = np.repeat(np.arange(rows), GRID_W).astype(np.float32)
    col = np.tile(np.arange(GRID_W), rows).astype(np.float32)
    axis_dim = ROPE // 2
    inv = (np.float32(ROPE_BASE) ** (-np.arange(0, axis_dim, 2, dtype=np.float32) / axis_dim)).astype(np.float32)
    ang = np.concatenate([row[:, None] * inv, col[:, None] * inv], axis=-1).astype(np.float32)
    cos, sin = np.cos(ang), np.sin(ang)
    ones = lambda n: np.ones((n_tokens, n), np.float32)
    zeros = lambda n: np.zeros((n_tokens, n), np.float32)
    cos_t = np.concatenate([ones(NOPE), cos, cos, ones(HP - QK)], axis=1)
    sin_t = np.concatenate([zeros(NOPE), -sin, sin, zeros(HP - QK)], axis=1)
    return jnp.asarray(cos_t, F32), jnp.asarray(sin_t, F32)


def kernel(x_prompt, x_sample, cache_mla_ckv, cache_mla_krope, state_rglru, c, c_ctx, norm1_g, norm2_g, w_mod, b_mod, w_in, conv_w, conv_b, lru_wa, lru_ba, lru_wx, lru_bx, lru_lam, q_a_norm, kv_a_norm, w_uq, w_uk, w_uv, q_norm, k_norm, w_o_rnn, w_o_mla, w_out, router_wg, router_bg, router_we, router_be, exp_w1, exp_w3, exp_w2):
    p = dict(norm1_g=norm1_g, norm2_g=norm2_g, w_in=w_in, conv_w=conv_w, conv_b=conv_b,
             lru_wa=lru_wa, lru_ba=lru_ba, lru_wx=lru_wx, lru_bx=lru_bx, lru_lam=lru_lam,
             q_a_norm=q_a_norm, kv_a_norm=kv_a_norm, w_uq=w_uq, w_uk=w_uk, w_uv=w_uv,
             q_norm=q_norm, k_norm=k_norm, w_o_rnn=w_o_rnn, w_o_mla=w_o_mla, w_out=w_out,
             router_wg=router_wg, router_bg=router_bg, router_we=router_we, router_be=router_be,
             exp_w1=exp_w1, exp_w3=exp_w3, exp_w2=exp_w2)
    depth = w_in.shape[0]
    nb, seq, _ = x_prompt.shape
    db, dseq, _ = x_sample.shape
    ident = np.arange(ROPE)
    halves = np.concatenate([np.arange(0, ROPE, 2), np.arange(1, ROPE, 2)])
    rope_tabs = _rope_tables(dseq)
    cond8 = jnp.concatenate([c_ctx[None, :], c, jnp.zeros((SUBLANES - 1 - db, D), F32)], axis=0)
    ctx_row = lambda tile_rows: (lambda i: 0)
    lat_row = lambda tile_rows: (lambda i: (i * tile_rows) // dseq + 1)
    n_ctx, n_lat = nb * seq, db * dseq
    run_padding = ((n_ctx + n_lat) // TM) * NG * (RUN_ALIGN - 1)
    max_tiles = -(-(n_ctx + n_lat + run_padding) // TMOE) + NG
    per_seq = lambda arrs, b, t: [a.reshape(b, t, a.shape[-1]) for a in arrs]
    flat = lambda a: a.reshape(-1, a.shape[-1])

    y_prompt, y_sample = x_prompt.reshape(n_ctx, D), x_sample.reshape(n_lat, D)
    ckv_list, krope_list, rnn_list = [], [], []
    for l in range(depth):
        shared = _prepare_shared(l, p)
        w_ctx = _with_rope_order(l, p, shared, ident, False)
        w_lat = _with_rope_order(l, p, shared, halves, True)
        mod = _modulation(cond8, w_mod[l], b_mod[l]).reshape(SUBLANES, 6, D)

        h, xr, gg, q, k, v, ckv, kro = _projections(y_prompt, mod, ctx_row(TM), w_ctx, None, True)
        xr, gg, q, k, v = per_seq([xr, gg, q, k, v], nb, seq)
        yr, h_fin = _rglru(xr, gg, shared, None, True)
        ya = _attention(q, k, v, None, None, 4, NH)
        x1_c, hx_c, ri_c, tc_c = _merge_out(y_prompt, h, flat(yr), flat(ya), mod, ctx_row(TM), shared)
        ckv_list.append(ckv.reshape(nb, seq, KVL))
        krope_list.append(kro.reshape(nb, seq, ROPE))
        rnn_list.append(h_fin)

        krp_cache = jnp.pad(cache_mla_krope[:, l][..., halves], ((0, 0), (0, 0), (NOPE, HP - QK)))
        kc, vc = _cache_keys_values(cache_mla_ckv[:, l], krp_cache, w_lat)
        h, xr, gg, q, k, v = _projections(y_sample, mod, lat_row(TM), w_lat, rope_tabs, False)
        xr, gg, q, k, v = per_seq([xr, gg, q, k, v], db, dseq)
        yr, _ = _rglru(xr, gg, shared, state_rglru[:, l], False)
        ya = _attention(q, k, v, kc, vc, 1, 4)
        x1_l, hx_l, ri_l, tc_l = _merge_out(y_sample, h, flat(yr), flat(ya), mod, lat_row(TM), shared)

        len_c, len_l = _run_lengths(tc_c), _run_lengths(tc_l)
        rows_c = jnp.sum(len_c, axis=0)
        offsets, tile_block, tile_group, n_tiles = _group_layout(rows_c + jnp.sum(len_l, axis=0), max_tiles)
        runs_c = _run_tables(len_c, offsets)
        runs_l = _run_tables(len_l, offsets + rows_c)
        xs = jnp.zeros((max_tiles * TMOE, XW), BF)
        xs = _dispatch(runs_c, hx_c, xs)
        xs = _dispatch(runs_l, hx_l, xs)
        ys = _experts(tile_block, tile_group, n_tiles, xs, shared)
        y_prompt = _combine(runs_c, x1_c, ri_c, mod, ctx_row(TM), ys)
        y_sample = _combine(runs_l, x1_l, ri_l, mod, lat_row(TM), ys)

    y_prompt, y_sample = y_prompt.reshape(nb, seq, D), y_sample.reshape(db, dseq, D)

    return (y_prompt, y_sample, jnp.stack(ckv_list, axis=1), jnp.stack(krope_list, axis=1),
            jnp.stack(rnn_list, axis=1))
```

```python
import functools
import math

import numpy as np
import jax
import jax.numpy as jnp
from jax import lax
from jax.experimental import pallas as pl
from jax.experimental.pallas import tpu as pltpu

D = 1024
DR = 1024
QL = 384
KVL = 256
NH = 8
NOPE = 64
ROPE = 32
QK = NOPE + ROPE
VD = 64
O_G, O_Q, O_KV, O_KR, O_GATE = DR, 2 * DR, 2 * DR + QL, 2 * DR + QL + KVL, 2 * DR + QL + KVL + ROPE
HP = 128
DH = NH * HP
GRID_W = 64
ROPE_BASE = 10000.0
EPS = 1e-6
TINY = 1e-30
LRU_C = 8.0
LRU_BLOCK = 64
BDW = 256
CH = 1024
NG = 4
EPG = 4
NE = NG * EPG
DE = 512
LANES = 128
SUBLANES = 8
TM = 512
TS = 640
TMOE = 512
EXPERTS_PER_STEP = 2
XW = D + 3 * LANES
RUN_ALIGN = 16
VMEM_LIMIT = 52 * 1024 * 1024
BF = jnp.bfloat16
F32 = jnp.float32


def _cparams(sem):
    return pltpu.CompilerParams(dimension_semantics=sem, vmem_limit_bytes=VMEM_LIMIT)


def _dot(a, b):
    return jnp.dot(a, b, preferred_element_type=F32)


def _dot_nt(a, b):
    return lax.dot_general(a, b, (((1,), (1,)), ((), ())), preferred_element_type=F32)


def _rms(x, g, width):
    ms = jnp.sum(x * x, axis=-1, keepdims=True) * (1.0 / width)
    return x * lax.rsqrt(ms + EPS) * g


def _modulated_norm(x, g, scale, shift):
    return _rms(x, g * (1.0 + scale), D) + shift


def _mod_kernel(c_ref, w_ref, b_ref, o_ref):
    c = c_ref[...]
    s = c * jax.nn.sigmoid(c)
    o_ref[...] = _dot(s, w_ref[...]) + b_ref[...]


def _modulation(cond8, w_mod, b_mod):
    n = w_mod.shape[1]
    return pl.pallas_call(
        _mod_kernel,
        grid=(n // D,),
        in_specs=[
            pl.BlockSpec((SUBLANES, D), lambda j: (0, 0)),
            pl.BlockSpec((D, D), lambda j: (0, j)),
            pl.BlockSpec((1, D), lambda j: (0, j)),
        ],
        out_specs=pl.BlockSpec((SUBLANES, D), lambda j: (0, j)),
        out_shape=jax.ShapeDtypeStruct((SUBLANES, n), F32),
        compiler_params=_cparams(("arbitrary",)),
        name="modulation",
    )(cond8, w_mod, b_mod.reshape(1, n))


def _head_norm(xh, gain, cos=None, partner_scaled=None):
    ms = jnp.sum(xh * xh, axis=-1, keepdims=True) * (1.0 / QK)
    rs = lax.rsqrt(ms + EPS)
    y = xh * rs * gain
    if cos is None:
        return y
    return y * cos + partner_scaled * rs


def _keys_values(ckv, krp, wuk_ref, wuv_ref, gk, cos, partner_scaled, k_ref, v_ref):
    cb = ckv.astype(BF)
    kn = _dot(cb, wuk_ref[...])
    v_ref[...] = _dot(cb, wuv_ref[...]).astype(BF)
    for h in range(NH):
        kh = kn[:, h * HP:(h + 1) * HP] + krp
        k_ref[:, h * HP:(h + 1) * HP] = _head_norm(kh, gk, cos, partner_scaled).astype(BF)


def _proj_kernel(*refs, rope, emit_cache):
    it = iter(refs)
    x_ref, mod_ref, n1_ref = next(it), next(it), next(it)
    win_ref, wkvr_ref = next(it), next(it)
    wx_ref, wg_ref, wq_ref = win_ref.at[0:O_G], win_ref.at[O_G:O_Q], win_ref.at[O_Q:O_KV]
    qan_ref, kvan_ref, wuq_ref, gq_ref = next(it), next(it), next(it), next(it)
    wuk_ref, gk_ref, wuv_ref = next(it), next(it), next(it)
    if rope:
        wuqs_ref, gqs_ref, gks_ref, cos_ref, sins_ref = (next(it) for _ in range(5))
    h_ref, xr_ref, gg_ref, q_ref, k_ref, v_ref = (next(it) for _ in range(6))
    if emit_cache:
        ckv_ref, kro_ref = next(it), next(it)

    hb = _modulated_norm(x_ref[...], n1_ref[...], mod_ref[0, 1:2, :], mod_ref[0, 0:1, :]).astype(BF)
    h_ref[...] = hb
    xr_ref[...] = _dot_nt(hb, wx_ref[...]).astype(BF)
    gg_ref[...] = jax.nn.gelu(_dot_nt(hb, wg_ref[...])).astype(BF)

    qnb = _rms(_dot_nt(hb, wq_ref[...]), qan_ref[...], QL).astype(BF)
    q = _dot(qnb, wuq_ref[...])
    gq = gq_ref[...]
    cos = q_partner = q_pair_scale = None
    if rope:
        cos, sins = cos_ref[...], sins_ref[...]
        q_partner = _dot(qnb, wuqs_ref[...])
        q_pair_scale = gqs_ref[...] * sins
    for hd in range(NH):
        cols = slice(hd * HP, (hd + 1) * HP)
        partner = q_partner[:, cols] * q_pair_scale if rope else None
        q_ref[:, cols] = _head_norm(q[:, cols], gq, cos, partner).astype(BF)

    kvr = _dot_nt(hb, wkvr_ref[...])
    ckv = _rms(kvr[:, :KVL], kvan_ref[...], KVL)
    krp = kvr[:, KVL:KVL + HP]
    k_partner = kvr[:, KVL + HP:KVL + 2 * HP] * (gks_ref[...] * sins) if rope else None
    if emit_cache:
        ckv_ref[...] = ckv
        kro_ref[...] = krp[:, NOPE:NOPE + ROPE]
    _keys_values(ckv, krp, wuk_ref, wuv_ref, gk_ref[...], cos, k_partner, k_ref, v_ref)


def _resident(shape):
    return pl.BlockSpec(shape, lambda i: (0,) * len(shape), pipeline_mode=pl.Buffered(1))


def _projections(x, mod, mod_row, wts, rope_tabs, emit_cache):
    n = x.shape[0]
    rope = rope_tabs is not None
    tile = lambda w: pl.BlockSpec((TM, w), lambda i: (i, 0))
    in_specs = [
        tile(D),
        pl.BlockSpec((1, 6, D), lambda i: (mod_row(i), 0, 0)),
        _resident((1, D)),
        _resident(wts["w_in_t"].shape), _resident(wts["w_kvr"].shape),
        _resident((1, QL)), _resident((1, KVL)), _resident((QL, DH)), _resident((1, HP)),
        _resident((KVL, DH)), _resident((1, HP)), _resident((KVL, DH)),
    ]
    args = [x, mod, wts["n1"], wts["w_in_t"], wts["w_kvr"],
            wts["qan"], wts["kvan"], wts["w_uq"], wts["gq"], wts["w_uk"], wts["gk"], wts["w_uv"]]
    if rope:
        tiles_per_seq = rope_tabs[0].shape[0] // TM
        in_specs += [_resident((QL, DH)), _resident((1, HP)), _resident((1, HP))]
        in_specs += [pl.BlockSpec((TM, HP), lambda i: (i % tiles_per_seq, 0))] * 2
        args += [wts["w_uq_pair"], wts["gq_pair"], wts["gk_pair"]] + list(rope_tabs)
    out_specs = [tile(D), tile(DR), tile(DR), tile(DH), tile(DH), tile(DH)]
    out_shape = [jax.ShapeDtypeStruct((n, D), BF), jax.ShapeDtypeStruct((n, DR), BF),
                 jax.ShapeDtypeStruct((n, DR), BF), jax.ShapeDtypeStruct((n, DH), BF),
                 jax.ShapeDtypeStruct((n, DH), BF), jax.ShapeDtypeStruct((n, DH), BF)]
    if emit_cache:
        out_specs += [tile(KVL), tile(ROPE)]
        out_shape += [jax.ShapeDtypeStruct((n, KVL), F32), jax.ShapeDtypeStruct((n, ROPE), F32)]
    return pl.pallas_call(
        functools.partial(_proj_kernel, rope=rope, emit_cache=emit_cache),
        grid=(n // TM,),
        in_specs=in_specs,
        out_specs=out_specs,
        out_shape=out_shape,
        compiler_params=_cparams(("arbitrary",)),
        name="projections",
    )(*args)


def _cache_kv_kernel(ckv_ref, krp_ref, wuk_ref, gk_ref, wuv_ref, k_ref, v_ref):
    _keys_values(ckv_ref[0], krp_ref[0], wuk_ref, wuv_ref, gk_ref[...], None, None,
                 k_ref.at[0], v_ref.at[0])


def _cache_keys_values(ckv, krp, wts):
    b, s, _ = ckv.shape
    full = lambda shape: pl.BlockSpec(shape, lambda i: (0,) * len(shape))
    return pl.pallas_call(
        _cache_kv_kernel,
        grid=(b,),
        in_specs=[pl.BlockSpec((1, s, KVL), lambda i: (i, 0, 0)),
                  pl.BlockSpec((1, s, HP), lambda i: (i, 0, 0)),
                  full((KVL, DH)), full((1, HP)), full((KVL, DH))],
        out_specs=[pl.BlockSpec((1, s, DH), lambda i: (i, 0, 0))] * 2,
        out_shape=[jax.ShapeDtypeStruct((b, s, DH), BF)] * 2,
        compiler_params=_cparams(("arbitrary",)),
        name="cache_keys_values",
    )(ckv, krp, wts["w_uk"], wts["gk"], wts["w_uv"])


WIN = 256
SEG = WIN // 8


def _window_permutation():
    dst = np.arange(WIN)
    p = np.zeros((WIN, WIN), np.float32)
    p[dst, (dst % SUBLANES) * SEG + dst // SUBLANES] = 1.0
    return p


def _sigmoid(x):
    return 0.5 * jnp.tanh(0.5 * x) + 0.5


def _segment_pass(a_scr, b_scr, bases, inits, out_scr=None):
    def body(k, carry):
        new = []
        for d in range(2):
            h, p = carry[d]
            i = k if d == 0 else SEG - 1 - k
            rows = pl.ds(pl.multiple_of(bases[d] + i * SUBLANES, SUBLANES), SUBLANES)
            a = a_scr[d, rows, :]
            h = a * h + b_scr[d, rows, :]
            if out_scr is None:
                p = a * p
            else:
                out_scr[d, rows, :] = h
            new.append((h, p))
        return tuple(new)

    init = tuple((inits[d], jnp.ones_like(inits[d])) for d in range(2))
    return lax.fori_loop(0, SEG, body, init, unroll=4)


def _segment_entries(end, decay, carry_in, forward):
    order = range(SUBLANES) if forward else reversed(range(SUBLANES))
    rows = [None] * SUBLANES
    c = carry_in
    for s in order:
        rows[s] = c
        c = end[s:s + 1, :] + decay[s:s + 1, :] * c
    return jnp.concatenate(rows, axis=0), c


def _rglru_kernel(*refs, t, has_h0, emit_state):
    it = iter(refs)
    xr_ref, gg_ref, perm_ref, unperm_ref = (next(it) for _ in range(4))
    cw_ref, cb_ref, bd_ref, ba_ref, bx_ref, lam_ref = (next(it) for _ in range(6))
    h0_ref = next(it) if has_h0 else None
    y_ref = next(it)
    hf_ref = next(it) if emit_state else None
    a_scr, b_scr, h_scr = (next(it) for _ in range(3))

    n_win = t // WIN
    sub = lax.broadcasted_iota(jnp.int32, (SUBLANES, CH), 0)
    zero_row = jnp.zeros((1, CH), F32)
    edge = 2 * SUBLANES
    for w in range(n_win):
        lo, hi = w * WIN, (w + 1) * WIN
        xp = _dot(perm_ref[...], xr_ref[0, lo:hi, :])
        before = xr_ref[0, lo - edge:lo, :].astype(F32)[edge - 1:edge, :] if w > 0 else zero_row
        after = xr_ref[0, hi:hi + edge, :].astype(F32) if w < n_win - 1 else None
        after0 = after[0:1, :] if after is not None else zero_row
        after1 = after[1:2, :] if after is not None else zero_row
        tile_m1 = jnp.where(sub == 0, before, pltpu.roll(xp[WIN - SUBLANES:WIN, :], 1, 0))
        tile_p0 = jnp.where(sub == SUBLANES - 1, after0, pltpu.roll(xp[0:SUBLANES, :], SUBLANES - 1, 0))
        tile_p1 = jnp.where(sub == SUBLANES - 1, after1,
                            pltpu.roll(xp[SUBLANES:2 * SUBLANES, :], SUBLANES - 1, 0))
        xe = jnp.concatenate([tile_m1, xp, tile_p0, tile_p1], axis=0)
        xc = cb_ref[...]
        for tap in range(4):
            xc = xc + xe[tap * SUBLANES:tap * SUBLANES + WIN, :] * cw_ref[tap:tap + 1, :]
        for s in range(CH // BDW):
            cols = slice(s * BDW, (s + 1) * BDW)
            xs = xc[:, cols]
            xsb = xs.astype(BF)
            xh = 0.5 * xs
            for d in range(2):
                tr = jnp.tanh(_dot(xsb, bd_ref[2 * d, s]) + ba_ref[d:d + 1, cols])
                ti = jnp.tanh(_dot(xsb, bd_ref[2 * d + 1, s]) + bx_ref[d:d + 1, cols])
                nl = -lam_ref[d:d + 1, cols]
                softplus = jnp.maximum(nl, 0.0) + jnp.log(1.0 + jnp.exp(-jnp.abs(nl)))
                ch = (-0.5 * LRU_C) * softplus
                a = jnp.exp(tr * ch + ch)
                z = 1.0 - a * a
                root = z * lax.rsqrt(jnp.maximum(z, TINY))
                a_scr[d, lo:hi, cols] = a
                b_scr[d, lo:hi, cols] = root * (ti * xh + xh)

    if has_h0:
        carry = [h0_ref[0, 0:1, :], h0_ref[0, 1:2, :]]
    else:
        carry = [zero_row, zero_row]
    zeros = jnp.zeros((SUBLANES, CH), F32)
    for k in range(n_win):
        bases = (k * WIN, (n_win - 1 - k) * WIN)
        totals = _segment_pass(a_scr, b_scr, bases, (zeros, zeros))
        entries = []
        for d in range(2):
            entry, carry[d] = _segment_entries(totals[d][0], totals[d][1], carry[d], d == 0)
            entries.append(entry)
        _segment_pass(a_scr, b_scr, bases, entries, out_scr=h_scr)

    if emit_state:
        hf_ref[0, 0:1, :] = carry[0]
        hf_ref[0, 1:2, :] = carry[1]
    for w in range(n_win):
        lo, hi = w * WIN, (w + 1) * WIN
        gate = _dot(perm_ref[...], gg_ref[0, lo:hi, :])
        yp = ((h_scr[0, lo:hi, :] + h_scr[1, lo:hi, :]) * gate).astype(BF)
        y_ref[0, lo:hi, :] = _dot(unperm_ref[...], yp).astype(BF)


def _rglru(xr, gg, wts, h0, emit_state):
    b, t, _ = xr.shape
    nc = DR // CH
    has_h0 = h0 is not None
    chunk = lambda r: pl.BlockSpec((r, CH), lambda i, j: (0, j))
    seq = pl.BlockSpec((1, t, CH), lambda i, j: (i, 0, j))
    state = pl.BlockSpec((1, 2, CH), lambda i, j: (i, 0, j))
    window = pl.BlockSpec((WIN, WIN), lambda i, j: (0, 0))
    perm = _window_permutation()
    in_specs = [seq, seq, window, window, chunk(4), chunk(1),
                pl.BlockSpec((4, CH // BDW, BDW, BDW), lambda i, j: (0, j, 0, 0)),
                chunk(2), chunk(2), chunk(2)]
    args = [xr, gg, jnp.asarray(perm, BF), jnp.asarray(perm.T, BF),
            wts["conv_w"], wts["conv_b"], wts["bd"], wts["lru_ba"], wts["lru_bx"], wts["lru_lam"]]
    if has_h0:
        in_specs.append(state)
        args.append(h0)
    out_specs = [seq]
    out_shape = [jax.ShapeDtypeStruct((b, t, DR), BF)]
    if emit_state:
        out_specs.append(state)
        out_shape.append(jax.ShapeDtypeStruct((b, 2, DR), F32))
    res = pl.pallas_call(
        functools.partial(_rglru_kernel, t=t, has_h0=has_h0, emit_state=emit_state),
        grid=(b, nc),
        in_specs=in_specs,
        out_specs=out_specs,
        out_shape=out_shape,
        scratch_shapes=[pltpu.VMEM((2, t, CH), F32)] * 3,
        compiler_params=_cparams(("arbitrary", "arbitrary")),
        name="rglru",
    )(*args)
    return res if emit_state else (res[0], None)


def _attn_kernel(*refs, t, n_seqs, n_heads, has_ctx, q_block):
    it = iter(refs)
    q_ref, k_ref, v_ref = next(it), next(it), next(it)
    kc_ref = vc_ref = None
    if has_ctx:
        kc_ref, vc_ref = next(it), next(it)
    o_ref = next(it)
    log2_scale = (QK ** -0.5) * math.log2(math.e)
    for sq, hd in [(sq, hd) for sq in range(n_seqs) for hd in range(n_heads)]:
        cols = slice(hd * HP, (hd + 1) * HP)
        k = k_ref[sq, :, cols]
        v = v_ref[sq, :, cols]
        if has_ctx:
            kc = kc_ref[sq, :, cols]
            vc = vc_ref[sq, :, cols]
        for qb in range(t // q_block):
            rows = slice(qb * q_block, (qb + 1) * q_block)
            q = q_ref[sq, rows, cols]
            s = _dot_nt(q, k) * log2_scale
            m = jnp.max(s, axis=-1, keepdims=True)
            if has_ctx:
                sc = _dot_nt(q, kc) * log2_scale
                m = jnp.maximum(m, jnp.max(sc, axis=-1, keepdims=True))
            p = jnp.exp2(s - m)
            den = jnp.sum(p, axis=-1, keepdims=True)
            o = _dot(p.astype(BF), v)
            if has_ctx:
                pc = jnp.exp2(sc - m)
                den = den + jnp.sum(pc, axis=-1, keepdims=True)
                o = o + _dot(pc.astype(BF), vc)
            o_ref[sq, rows, cols] = (o / den).astype(BF)


def _attention(q, k, v, kc, vc, seqs_per_step, heads_per_step):
    b, t, _ = q.shape
    has_ctx = kc is not None
    w = heads_per_step * HP
    blk = lambda n: pl.BlockSpec((seqs_per_step, n, w), lambda i, j: (i, 0, j))
    in_specs = [blk(t), blk(t), blk(t)]
    args = [q, k, v]
    if has_ctx:
        in_specs += [blk(kc.shape[1])] * 2
        args += [kc, vc]
    return pl.pallas_call(
        functools.partial(_attn_kernel, t=t, n_seqs=seqs_per_step, n_heads=heads_per_step,
                          has_ctx=has_ctx, q_block=min(t, 256)),
        grid=(b // seqs_per_step, NH // heads_per_step),
        in_specs=in_specs,
        out_specs=blk(t),
        out_shape=jax.ShapeDtypeStruct((b, t, DH), BF),
        compiler_params=_cparams(("arbitrary", "arbitrary")),
        name="attention",
    )(*args)


def _route(logits):
    lane = lax.broadcasted_iota(jnp.int32, logits.shape, 1)
    lanef = lane.astype(F32)
    neg = -jnp.inf
    big = float(LANES)
    gl = jnp.where((lane >= NE) & (lane < NE + NG), logits, neg)
    gmax = jnp.max(gl, axis=-1, keepdims=True)
    gidx = jnp.min(jnp.where(gl == gmax, lanef, big), axis=-1, keepdims=True) - float(NE)
    gw = 1.0 / jnp.sum(jnp.exp(gl - gmax), axis=-1, keepdims=True)
    lo = gidx * float(EPG)
    el = jnp.where((lanef >= lo) & (lanef < lo + float(EPG)), logits, neg)
    v1 = jnp.max(el, axis=-1, keepdims=True)
    i1 = jnp.min(jnp.where(el == v1, lanef, big), axis=-1, keepdims=True)
    el2 = jnp.where(lanef == i1, neg, el)
    v2 = jnp.max(el2, axis=-1, keepdims=True)
    i2 = jnp.min(jnp.where(el2 == v2, lanef, big), axis=-1, keepdims=True)
    e2 = jnp.exp(v2 - v1)
    w1 = gw / (1.0 + e2)
    w2 = gw * e2 / (1.0 + e2)
    cmb = jnp.where(lanef == i1, w1, 0.0) + jnp.where(lanef == i2, w2, 0.0)
    return cmb, gidx


def _out_kernel(x_ref, h_ref, yr_ref, ya_ref, mod_ref, win_ref, wor_ref, wom_ref, wout_ref,
                n2_ref, rw_ref, rb_ref, x1_ref, hx_ref, rinfo_ref, tcnt_ref):
    x = x_ref[...]
    gl = _dot_nt(h_ref[...], win_ref[O_GATE:O_GATE + 2 * D, :])
    merged2 = ((jnp.tanh(0.5 * gl[:, :D]) + 1.0) * _dot(yr_ref[...], wor_ref[...])
               + (jnp.tanh(0.5 * gl[:, D:]) + 1.0) * _dot(ya_ref[...], wom_ref[...]))
    mix = _dot(merged2.astype(BF), wout_ref[...])
    x1 = x + mod_ref[0, 2:3, :] * mix
    x1_ref[...] = x1
    h2 = _modulated_norm(x1, n2_ref[...], mod_ref[0, 4:5, :], mod_ref[0, 3:4, :])
    h2_hi = h2.astype(BF)
    h2_lo = (h2 - h2_hi.astype(F32)).astype(BF)
    part = _dot(h2_hi, rw_ref[...])
    logits = part[:, :LANES] + part[:, LANES:] + _dot(h2_lo, rw_ref[:, :LANES]) + rb_ref[...]
    cmb, gidx = _route(logits)

    lanef = lax.broadcasted_iota(jnp.int32, cmb.shape, 1).astype(F32)
    ghot = jnp.where(lanef == gidx, 1.0, 0.0)
    r_i = lax.broadcasted_iota(jnp.int32, (TM, TM), 0)
    c_i = lax.broadcasted_iota(jnp.int32, (TM, TM), 1)
    tri = jnp.where(r_i > c_i, 1.0, 0.0).astype(BF)
    earlier_same = jnp.sum(_dot(tri, ghot.astype(BF)) * ghot, axis=-1, keepdims=True)
    counts = jnp.sum(ghot, axis=0, keepdims=True)
    padded = jnp.floor((counts + (RUN_ALIGN - 1.0)) * (1.0 / RUN_ALIGN)) * RUN_ALIGN
    lower_groups = jnp.sum(jnp.where(lanef < gidx, padded, 0.0), axis=-1, keepdims=True)
    lpos = lower_groups + earlier_same
    s_i = lax.broadcasted_iota(jnp.int32, (TM, TS), 1)
    to_sorted = jnp.where(s_i.astype(F32) == lpos, 1.0, 0.0).astype(BF)
    c1 = cmb.astype(BF)
    c2 = (cmb - c1.astype(F32)).astype(BF)
    c3 = (cmb - c1.astype(F32) - c2.astype(F32)).astype(BF)
    payload = jnp.concatenate([h2.astype(BF), c1, c2, c3], axis=1)
    srt = lax.dot_general(to_sorted, payload, (((0,), (0,)), ((), ())), preferred_element_type=F32)
    hx_ref[...] = srt.astype(BF)

    rinfo_ref[...] = jnp.where(lanef == 0.0, gidx, jnp.where(lanef == 1.0, lpos, 0.0))
    tcnt_ref[0] = jnp.broadcast_to(counts, (SUBLANES, LANES))


def _merge_out(x, h, yr, ya, mod, mod_row, wts):
    n = x.shape[0]
    n_tiles = n // TM
    tile = lambda w: pl.BlockSpec((TM, w), lambda i: (i, 0))
    return pl.pallas_call(
        _out_kernel,
        grid=(n_tiles,),
        in_specs=[tile(D), tile(D), tile(DR), tile(DH),
                  pl.BlockSpec((1, 6, D), lambda i: (mod_row(i), 0, 0)),
                  _resident(wts["w_in_t"].shape), _resident((DR, D)), _resident((DH, D)), _resident((D, D)),
                  _resident((1, D)), _resident((D, 2 * LANES)), _resident((1, LANES))],
        out_specs=[tile(D), pl.BlockSpec((TS, XW), lambda i: (i, 0)), tile(LANES),
                   pl.BlockSpec((1, SUBLANES, LANES), lambda i: (i, 0, 0))],
        out_shape=[jax.ShapeDtypeStruct((n, D), F32), jax.ShapeDtypeStruct((n_tiles * TS, XW), BF),
                   jax.ShapeDtypeStruct((n, LANES), F32),
                   jax.ShapeDtypeStruct((n_tiles, SUBLANES, LANES), F32)],
        compiler_params=_cparams(("arbitrary",)),
        name="merge_out",
    )(x, h, yr, ya, mod, wts["w_in_t"], wts["w_o_rnn"], wts["w_o_mla"], wts["w_out"],
      wts["n2"], wts["router_w"], wts["router_b"])


def _run_copies(local_ref, far_ref, len_ref, tile, make_copy, action):
    for g in range(NG):
        idx = tile * NG + g
        n = len_ref[idx]
        local0 = local_ref[idx]
        far0 = far_ref[idx]
        for k in reversed(range(RUN_ALIGN.bit_length() - 1, TS.bit_length())):
            size = 1 << k
            done = (n >> (k + 1)) << (k + 1)

            @pl.when(((n >> k) & 1) == 1)
            def _():
                action(make_copy(pl.multiple_of(local0 + done, RUN_ALIGN),
                                 pl.multiple_of(far0 + done, RUN_ALIGN), size))


def _dispatch_kernel(local_ref, far_ref, len_ref, hx_ref, xs_in_ref, xs_ref, sem):
    del xs_in_ref

    def copy(src, dst, size):
        return pltpu.make_async_copy(hx_ref.at[pl.ds(src, size)], xs_ref.at[pl.ds(dst, size)], sem)

    tile = pl.program_id(0)
    _run_copies(local_ref, far_ref, len_ref, tile, copy, lambda c: c.start())
    _run_copies(local_ref, far_ref, len_ref, tile, copy, lambda c: c.wait())


def _dispatch(tables, hx, xs):
    n_tiles = hx.shape[0] // TS
    return pl.pallas_call(
        _dispatch_kernel,
        grid_spec=pltpu.PrefetchScalarGridSpec(
            num_scalar_prefetch=3,
            grid=(n_tiles,),
            in_specs=[pl.BlockSpec((TS, XW), lambda i, *_: (i, 0)),
                      pl.BlockSpec(memory_space=pl.ANY)],
            out_specs=pl.BlockSpec(memory_space=pl.ANY),
            scratch_shapes=[pltpu.SemaphoreType.DMA(())],
        ),
        out_shape=jax.ShapeDtypeStruct(xs.shape, xs.dtype),
        input_output_aliases={4: 0},
        compiler_params=_cparams(("arbitrary",)),
        name="dispatch",
    )(*tables, hx, xs)


def _opens_group(j, tg):
    return (j == 0) | (tg[j] != tg[jnp.maximum(j - 1, 0)])


def _moe_kernel(tb_ref, tg_ref, nt_ref, xs_ref, w1_ref, w3_ref, w2_ref, o_ref,
                acc_ref, w1c_ref, w3c_ref, w2c_ref):
    j = pl.program_id(0)
    pair = pl.program_id(1)

    @pl.when((j < nt_ref[0]) & _opens_group(j, tg_ref))
    def _():
        w1c_ref[pair] = w1_ref[...].astype(BF)
        w3c_ref[pair] = w3_ref[...].astype(BF)
        w2c_ref[pair] = w2_ref[...].reshape(EXPERTS_PER_STEP * DE, D).astype(BF)

    @pl.when(j < nt_ref[0])
    def _():
        xt = xs_ref[:, :D]
        cmb = (xs_ref[:, D:D + LANES].astype(F32) + xs_ref[:, D + LANES:D + 2 * LANES].astype(F32)
               + xs_ref[:, D + 2 * LANES:].astype(F32))
        lane = lax.broadcasted_iota(jnp.int32, cmb.shape, 1)
        first = tg_ref[j] * EPG + pair * EXPERTS_PER_STEP
        hidden = []
        for u in range(EXPERTS_PER_STEP):
            a = _dot(xt, w1c_ref[pair, u])
            he = (a * _sigmoid(a)) * _dot(xt, w3c_ref[pair, u])
            ce = jnp.sum(jnp.where(lane == first + u, cmb, 0.0), axis=-1, keepdims=True)
            hidden.append((he * ce).astype(BF))
        y = _dot(jnp.concatenate(hidden, axis=1), w2c_ref[pair])

        @pl.when(pair == 0)
        def _():
            acc_ref[...] = y

        @pl.when(pair == EPG // EXPERTS_PER_STEP - 1)
        def _():
            o_ref[...] = (acc_ref[...] + y).astype(BF)

    @pl.when((j >= nt_ref[0]) & (pair == EPG // EXPERTS_PER_STEP - 1))
    def _():
        o_ref[...] = jnp.zeros_like(o_ref)


def _experts(tile_block, tile_group, n_tiles, xs, wts):
    m = xs.shape[0]
    steps = EPG // EXPERTS_PER_STEP
    assert steps == 2, "the kernel keeps one partial sum: first step stores it, second adds and writes"

    def w_idx(j, e, tb, tg, nt):
        needed = (j < nt[0]) & _opens_group(j, tg)
        return (tg[j] * steps + jnp.where(needed, e, steps - 1), 0, 0)

    return pl.pallas_call(
        _moe_kernel,
        grid_spec=pltpu.PrefetchScalarGridSpec(
            num_scalar_prefetch=3,
            grid=(m // TMOE, steps),
            in_specs=[pl.BlockSpec((TMOE, XW), lambda j, e, tb, tg, nt: (tb[j], 0)),
                      pl.BlockSpec((EXPERTS_PER_STEP, D, DE), w_idx),
                      pl.BlockSpec((EXPERTS_PER_STEP, D, DE), w_idx),
                      pl.BlockSpec((EXPERTS_PER_STEP, DE, D), w_idx)],
            out_specs=pl.BlockSpec((TMOE, D), lambda j, e, tb, tg, nt: (j, 0)),
            scratch_shapes=[pltpu.VMEM((TMOE, D), F32),
                            pltpu.VMEM((steps, EXPERTS_PER_STEP, D, DE), BF),
                            pltpu.VMEM((steps, EXPERTS_PER_STEP, D, DE), BF),
                            pltpu.VMEM((steps, EXPERTS_PER_STEP * DE, D), BF)],
        ),
        out_shape=jax.ShapeDtypeStruct((m, D), BF),
        compiler_params=_cparams(("arbitrary", "arbitrary")),
        name="experts",
    )(tile_block, tile_group, n_tiles, xs, wts["exp_w1"], wts["exp_w3"], wts["exp_w2"])


def _combine_kernel(local_ref, far_ref, len_ref, x1_ref, rinfo_ref, mod_ref, ys_ref, o_ref,
                    buf_ref, sem, *, n_steps):
    i = pl.program_id(0)

    def runs(step, slot, action):
        def copy(dst, src, size):
            return pltpu.make_async_copy(ys_ref.at[pl.ds(src, size)],
                                         buf_ref.at[slot, pl.ds(dst, size)], sem.at[slot])

        _run_copies(local_ref, far_ref, len_ref, step, copy, action)

    @pl.when(i == 0)
    def _():
        buf_ref[...] = jnp.zeros_like(buf_ref)
        runs(0, 0, lambda c: c.start())

    @pl.when(i + 1 < n_steps)
    def _():
        runs(i + 1, (i + 1) % 2, lambda c: c.start())

    slot = i % 2
    runs(i, slot, lambda c: c.wait())
    lpos = rinfo_ref[:, 1:2]
    s_i = lax.broadcasted_iota(jnp.int32, (TM, TS), 1)
    from_sorted = jnp.where(s_i.astype(F32) == lpos, 1.0, 0.0).astype(BF)
    moe = _dot(from_sorted, buf_ref[slot])
    o_ref[...] = x1_ref[...] + mod_ref[0, 5:6, :] * moe


def _combine(tables, x1, rinfo, mod, mod_row, ys):
    n = x1.shape[0]
    n_steps = n // TM
    return pl.pallas_call(
        functools.partial(_combine_kernel, n_steps=n_steps),
        grid_spec=pltpu.PrefetchScalarGridSpec(
            num_scalar_prefetch=3,
            grid=(n_steps,),
            in_specs=[pl.BlockSpec((TM, D), lambda i, *_: (i, 0)),
                      pl.BlockSpec((TM, LANES), lambda i, *_: (i, 0)),
                      pl.BlockSpec((1, 6, D), lambda i, *_: (mod_row(i), 0, 0)),
                      pl.BlockSpec(memory_space=pl.ANY)],
            out_specs=pl.BlockSpec((TM, D), lambda i, *_: (i, 0)),
            scratch_shapes=[pltpu.VMEM((2, TS, D), BF), pltpu.SemaphoreType.DMA((2,))],
        ),
        out_shape=jax.ShapeDtypeStruct((n, D), F32),
        compiler_params=_cparams(("arbitrary",)),
        name="combine",
    )(*tables, x1, rinfo, mod, ys)


def _run_lengths(tile_counts):
    counts = tile_counts[:, 0, :NG].astype(jnp.int32)
    return ((counts + RUN_ALIGN - 1) // RUN_ALIGN) * RUN_ALIGN


def _run_tables(lengths, first_far):
    local = jnp.cumsum(lengths, axis=1) - lengths
    far = first_far[None, :] + jnp.cumsum(lengths, axis=0) - lengths
    flat = lambda a: a.astype(jnp.int32).reshape(-1)
    return flat(local), flat(far), flat(lengths)


def _group_layout(counts, max_tiles):
    padded = ((counts + TMOE - 1) // TMOE) * TMOE
    ends = jnp.cumsum(padded)
    offsets = ends - padded
    n_tiles = (ends[-1] // TMOE).astype(jnp.int32)
    tile = jnp.minimum(jnp.arange(max_tiles, dtype=jnp.int32), jnp.maximum(n_tiles - 1, 0))
    tile_group = jnp.sum((tile[:, None] * TMOE >= ends[None, :]).astype(jnp.int32), axis=1)
    return offsets, tile, tile_group, n_tiles.reshape(1)


def _pad_heads(w, perm=None, rotary_only=False):
    lead = w.shape[:-1]
    per = w.shape[-1] // NH
    w = w.reshape(lead + (NH, per))
    if perm is not None:
        nope = jnp.zeros_like(w[..., :NOPE]) if rotary_only else w[..., :NOPE]
        w = jnp.concatenate([nope, w[..., NOPE:][..., perm]], axis=-1)
    w = jnp.pad(w, [(0, 0)] * len(lead) + [(0, 0), (0, HP - per)])
    return w.reshape(lead + (NH * HP,))


def _pad_gain(g, perm, rotary_only=False):
    nope = jnp.zeros((NOPE,), F32) if rotary_only else g[:NOPE]
    g = jnp.concatenate([nope, g[NOPE:][perm], jnp.zeros((HP - QK,), F32)])
    return g.reshape(1, HP)


def _block_diag(w):
    per = BDW // LRU_BLOCK
    rows = w.reshape(DR // BDW, BDW, LRU_BLOCK)
    idx = np.arange(BDW) // LRU_BLOCK
    mask = jnp.asarray(idx[:, None] == idx[None, :], w.dtype)
    return jnp.concatenate([rows] * per, axis=-1) * mask


def _prepare_shared(l, p):
    w_in = lambda a, b: jnp.transpose(lax.slice(p["w_in"], (l, 0, a), (l + 1, D, b)).reshape(D, b - a))
    bd = jnp.stack([_block_diag(p["lru_wa"][l, 0]), _block_diag(p["lru_wx"][l, 0]),
                    _block_diag(p["lru_wa"][l, 1]), _block_diag(p["lru_wx"][l, 1])])
    bd = (0.5 * bd).astype(BF)
    wom = p["w_o_mla"][l].reshape(NH, VD, D)
    wom = jnp.pad(wom, ((0, 0), (0, HP - VD), (0, 0))).reshape(DH, D)
    router_w = jnp.concatenate([p["router_we"][l], p["router_wg"][l],
                                jnp.zeros((D, LANES - NE - NG), F32)], axis=1)
    router_b = jnp.concatenate([p["router_be"][l], p["router_bg"][l],
                                jnp.zeros((LANES - NE - NG,), F32)]).reshape(1, LANES)
    router_hi = router_w.astype(BF)
    router_lo = (router_w - router_hi.astype(F32)).astype(BF)
    router_w = jnp.concatenate([router_hi, router_lo], axis=1)
    return {
        "n1": p["norm1_g"][l].reshape(1, D), "n2": p["norm2_g"][l].reshape(1, D),
        "w_in_t": jnp.transpose(p["w_in"][l]).astype(BF),
        "w_kv": w_in(O_KV, O_KR), "w_kr": w_in(O_KR, O_GATE),
        "qan": p["q_a_norm"][l].reshape(1, QL), "kvan": p["kv_a_norm"][l].reshape(1, KVL),
        "w_uk": _pad_heads(p["w_uk"][l]).astype(BF),
        "w_uv": _pad_heads(p["w_uv"][l]).astype(BF),
        "conv_w": p["conv_w"][l], "conv_b": p["conv_b"][l].reshape(1, DR), "bd": bd,
        "lru_ba": 0.5 * p["lru_ba"][l], "lru_bx": 0.5 * p["lru_bx"][l], "lru_lam": p["lru_lam"][l],
        "w_o_rnn": p["w_o_rnn"][l].astype(BF), "w_o_mla": wom.astype(BF), "w_out": (0.5 * p["w_out"][l]).astype(BF),
        "router_w": router_w, "router_b": router_b,
        "exp_w1": p["exp_w1"][l], "exp_w3": p["exp_w3"][l], "exp_w2": p["exp_w2"][l],
    }


def _with_rope_order(l, p, shared, perm, rotary):
    w = dict(shared)
    zeros = lambda n: jnp.zeros((n, D), F32)
    rope_block = lambda order: [zeros(NOPE), shared["w_kr"][order, :], zeros(HP - QK)]
    kvr = [shared["w_kv"]] + rope_block(perm)
    w["w_uq"] = _pad_heads(p["w_uq"][l], perm).astype(BF)
    w["gq"] = _pad_gain(p["q_norm"][l], perm)
    w["gk"] = _pad_gain(p["k_norm"][l], perm)
    if rotary:
        pair = np.concatenate([perm[ROPE // 2:], perm[:ROPE // 2]])
        kvr += rope_block(pair)
        w["w_uq_pair"] = _pad_heads(p["w_uq"][l], pair, rotary_only=True).astype(BF)
        w["gq_pair"] = _pad_gain(p["q_norm"][l], pair, rotary_only=True)
        w["gk_pair"] = _pad_gain(p["k_norm"][l], pair, rotary_only=True)
    w["w_kvr"] = jnp.concatenate(kvr, axis=0).astype(BF)
    return w


def _rope_tables(n_tokens):
    rows = n_tokens // GRID_W
    row = np.repeat(np.arange(rows), GRID_W).astype(np.float32)
    col = np.tile(np.arange(GRID_W), rows).astype(np.float32)
    axis_dim = ROPE // 2
    inv = (np.float32(ROPE_BASE) ** (-np.arange(0, axis_dim, 2, dtype=np.float32) / axis_dim)).astype(np.float32)
    ang = np.concatenate([row[:, None] * inv, col[:, None] * inv], axis=-1).astype(np.float32)
    cos, sin = np.cos(ang), np.sin(ang)
    ones = lambda n: np.ones((n_tokens, n), np.float32)
    zeros = lambda n: np.zeros((n_tokens, n), np.float32)
    cos_t = np.concatenate([ones(NOPE), cos, cos, ones(HP - QK)], axis=1)
    sin_t = np.concatenate([zeros(NOPE), -sin, sin, zeros(HP - QK)], axis=1)
    return jnp.asarray(cos_t, F32), jnp.asarray(sin_t, F32)


def kernel(x_prompt, x_sample, cache_mla_ckv, cache_mla_krope, state_rglru, c, c_ctx, norm1_g, norm2_g, w_mod, b_mod, w_in, conv_w, conv_b, lru_wa, lru_ba, lru_wx, lru_bx, lru_lam, q_a_norm, kv_a_norm, w_uq, w_uk, w_uv, q_norm, k_norm, w_o_rnn, w_o_mla, w_out, router_wg, router_bg, router_we, router_be, exp_w1, exp_w3, exp_w2):
    p = dict(norm1_g=norm1_g, norm2_g=norm2_g, w_in=w_in, conv_w=conv_w, conv_b=conv_b,
             lru_wa=lru_wa, lru_ba=lru_ba, lru_wx=lru_wx, lru_bx=lru_bx, lru_lam=lru_lam,
             q_a_norm=q_a_norm, kv_a_norm=kv_a_norm, w_uq=w_uq, w_uk=w_uk, w_uv=w_uv,
             q_norm=q_norm, k_norm=k_norm, w_o_rnn=w_o_rnn, w_o_mla=w_o_mla, w_out=w_out,
             router_wg=router_wg, router_bg=router_bg, router_we=router_we, router_be=router_be,
             exp_w1=exp_w1, exp_w3=exp_w3, exp_w2=exp_w2)
    depth = w_in.shape[0]
    nb, seq, _ = x_prompt.shape
    db, dseq, _ = x_sample.shape
    ident = np.arange(ROPE)
    halves = np.concatenate([np.arange(0, ROPE, 2), np.arange(1, ROPE, 2)])
    rope_tabs = _rope_tables(dseq)
    cond8 = jnp.concatenate([c_ctx[None, :], c, jnp.zeros((SUBLANES - 1 - db, D), F32)], axis=0)
    ctx_row = lambda tile_rows: (lambda i: 0)
    lat_row = lambda tile_rows: (lambda i: (i * tile_rows) // dseq + 1)
    n_ctx, n_lat = nb * seq, db * dseq
    run_padding = ((n_ctx + n_lat) // TM) * NG * (RUN_ALIGN - 1)
    max_tiles = -(-(n_ctx + n_lat + run_padding) // TMOE) + NG
    per_seq = lambda arrs, b, t: [a.reshape(b, t, a.shape[-1]) for a in arrs]
    flat = lambda a: a.reshape(-1, a.shape[-1])

    y_prompt, y_sample = x_prompt.reshape(n_ctx, D), x_sample.reshape(n_lat, D)
    ckv_list, krope_list, rnn_list = [], [], []
    for l in range(depth):
        shared = _prepare_shared(l, p)
        w_ctx = _with_rope_order(l, p, shared, ident, False)
        w_lat = _with_rope_order(l, p, shared, halves, True)
        mod = _modulation(cond8, w_mod[l], b_mod[l]).reshape(SUBLANES, 6, D)

        h, xr, gg, q, k, v, ckv, kro = _projections(y_prompt, mod, ctx_row(TM), w_ctx, None, True)
        xr, gg, q, k, v = per_seq([xr, gg, q, k, v], nb, seq)
        yr, h_fin = _rglru(xr, gg, shared, None, True)
        ya = _attention(q, k, v, None, None, 4, NH)
        x1_c, hx_c, ri_c, tc_c = _merge_out(y_prompt, h, flat(yr), flat(ya), mod, ctx_row(TM), shared)
        ckv_list.append(ckv.reshape(nb, seq, KVL))
        krope_list.append(kro.reshape(nb, seq, ROPE))
        rnn_list.append(h_fin)

        krp_cache = jnp.pad(cache_mla_krope[:, l][..., halves], ((0, 0), (0, 0), (NOPE, HP - QK)))
        kc, vc = _cache_keys_values(cache_mla_ckv[:, l], krp_cache, w_lat)
        h, xr, gg, q, k, v = _projections(y_sample, mod, lat_row(TM), w_lat, rope_tabs, False)
        xr, gg, q, k, v = per_seq([xr, gg, q, k, v], db, dseq)
        yr, _ = _rglru(xr, gg, shared, state_rglru[:, l], False)
        ya = _attention(q, k, v, kc, vc, 1, 4)
        x1_l, hx_l, ri_l, tc_l = _merge_out(y_sample, h, flat(yr), flat(ya), mod, lat_row(TM), shared)

        len_c, len_l = _run_lengths(tc_c), _run_lengths(tc_l)
        rows_c = jnp.sum(len_c, axis=0)
        offsets, tile_block, tile_group, n_tiles = _group_layout(rows_c + jnp.sum(len_l, axis=0), max_tiles)
        runs_c = _run_tables(len_c, offsets)
        runs_l = _run_tables(len_l, offsets + rows_c)
        xs = jnp.zeros((max_tiles * TMOE, XW), BF)
        xs = _dispatch(runs_c, hx_c, xs)
        xs = _dispatch(runs_l, hx_l, xs)
        ys = _experts(tile_block, tile_group, n_tiles, xs, shared)
        y_prompt = _combine(runs_c, x1_c, ri_c, mod, ctx_row(TM), ys)
        y_sample = _combine(runs_l, x1_l, ri_l, mod, lat_row(TM), ys)

    y_prompt, y_sample = y_prompt.reshape(nb, seq, D), y_sample.reshape(db, dseq, D)

    return (y_prompt, y_sample, jnp.stack(ckv_list, axis=1), jnp.stack(krope_list, axis=1),
            jnp.stack(rnn_list, axis=1))
```

```python
import functools
import math

import numpy as np
import jax
import jax.numpy as jnp
from jax import lax
from jax.experimental import pallas as pl
from jax.experimental.pallas import tpu as pltpu

D = 1024
DR = 1024
QL = 384
KVL = 256
NH = 8
NOPE = 64
ROPE = 32
QK = NOPE + ROPE
VD = 64
O_G, O_Q, O_KV, O_KR, O_GATE = DR, 2 * DR, 2 * DR + QL, 2 * DR + QL + KVL, 2 * DR + QL + KVL + ROPE
HP = 128
DH = NH * HP
GRID_W = 64
ROPE_BASE = 10000.0
EPS = 1e-6
TINY = 1e-30
LRU_C = 8.0
LRU_BLOCK = 64
BDW = 256
CH = 1024
NG = 4
EPG = 4
NE = NG * EPG
DE = 512
LANES = 128
SUBLANES = 8
TM = 512
TS = 640
TMOE = 512
EXPERTS_PER_STEP = 2
XW = D + 3 * LANES
RUN_ALIGN = 16
VMEM_LIMIT = 52 * 1024 * 1024
BF = jnp.bfloat16
F32 = jnp.float32


def _cparams(sem):
    return pltpu.CompilerParams(dimension_semantics=sem, vmem_limit_bytes=VMEM_LIMIT)


def _dot(a, b):
    return jnp.dot(a, b, preferred_element_type=F32)


def _dot_nt(a, b):
    return lax.dot_general(a, b, (((1,), (1,)), ((), ())), preferred_element_type=F32)


def _rms(x, g, width):
    ms = jnp.sum(x * x, axis=-1, keepdims=True) * (1.0 / width)
    return x * lax.rsqrt(ms + EPS) * g


def _modulated_norm(x, g, scale, shift):
    return _rms(x, g * (1.0 + scale), D) + shift


def _mod_kernel(c_ref, w_ref, b_ref, o_ref):
    c = c_ref[...]
    s = c * jax.nn.sigmoid(c)
    o_ref[...] = _dot(s, w_ref[...]) + b_ref[...]


def _modulation(cond8, w_mod, b_mod):
    n = w_mod.shape[1]
    return pl.pallas_call(
        _mod_kernel,
        grid=(n // D,),
        in_specs=[
            pl.BlockSpec((SUBLANES, D), lambda j: (0, 0)),
            pl.BlockSpec((D, D), lambda j: (0, j)),
            pl.BlockSpec((1, D), lambda j: (0, j)),
        ],
        out_specs=pl.BlockSpec((SUBLANES, D), lambda j: (0, j)),
        out_shape=jax.ShapeDtypeStruct((SUBLANES, n), F32),
        compiler_params=_cparams(("arbitrary",)),
        name="modulation",
    )(cond8, w_mod, b_mod.reshape(1, n))


def _head_norm(xh, gain, cos=None, partner_scaled=None):
    ms = jnp.sum(xh * xh, axis=-1, keepdims=True) * (1.0 / QK)
    rs = lax.rsqrt(ms + EPS)
    y = xh * rs * gain
    if cos is None:
        return y
    return y * cos + partner_scaled * rs


def _keys_values(ckv, krp, wuk_ref, wuv_ref, gk, cos, partner_scaled, k_ref, v_ref):
    cb = ckv.astype(BF)
    kn = _dot(cb, wuk_ref[...])
    v_ref[...] = _dot(cb, wuv_ref[...]).astype(BF)
    for h in range(NH):
        kh = kn[:, h * HP:(h + 1) * HP] + krp
        k_ref[:, h * HP:(h + 1) * HP] = _head_norm(kh, gk, cos, partner_scaled).astype(BF)


def _proj_kernel(*refs, rope, emit_cache):
    it = iter(refs)
    x_ref, mod_ref, n1_ref = next(it), next(it), next(it)
    win_ref, wkvr_ref = next(it), next(it)
    wx_ref, wg_ref, wq_ref = win_ref.at[0:O_G], win_ref.at[O_G:O_Q], win_ref.at[O_Q:O_KV]
    qan_ref, kvan_ref, wuq_ref, gq_ref = next(it), next(it), next(it), next(it)
    wuk_ref, gk_ref, wuv_ref = next(it), next(it), next(it)
    if rope:
        wuqs_ref, gqs_ref, gks_ref, cos_ref, sins_ref = (next(it) for _ in range(5))
    h_ref, xr_ref, gg_ref, q_ref, k_ref, v_ref = (next(it) for _ in range(6))
    if emit_cache:
        ckv_ref, kro_ref = next(it), next(it)

    hb = _modulated_norm(x_ref[...], n1_ref[...], mod_ref[0, 1:2, :], mod_ref[0, 0:1, :]).astype(BF)
    h_ref[...] = hb
    xr_ref[...] = _dot_nt(hb, wx_ref[...]).astype(BF)
    gg_ref[...] = jax.nn.gelu(_dot_nt(hb, wg_ref[...])).astype(BF)

    qnb = _rms(_dot_nt(hb, wq_ref[...]), qan_ref[...], QL).astype(BF)
    q = _dot(qnb, wuq_ref[...])
    gq = gq_ref[...]
    cos = q_partner = q_pair_scale = None
    if rope:
        cos, sins = cos_ref[...], sins_ref[...]
        q_partner = _dot(qnb, wuqs_ref[...])
        q_pair_scale = gqs_ref[...] * sins
    for hd in range(NH):
        cols = slice(hd * HP, (hd + 1) * HP)
        partner = q_partner[:, cols] * q_pair_scale if rope else None
        q_ref[:, cols] = _head_norm(q[:, cols], gq, cos, partner).astype(BF)

    kvr = _dot_nt(hb, wkvr_ref[...])
    ckv = _rms(kvr[:, :KVL], kvan_ref[...], KVL)
    krp = kvr[:, KVL:KVL + HP]
    k_partner = kvr[:, KVL + HP:KVL + 2 * HP] * (gks_ref[...] * sins) if rope else None
    if emit_cache:
        ckv_ref[...] = ckv
        kro_ref[...] = krp[:, NOPE:NOPE + ROPE]
    _keys_values(ckv, krp, wuk_ref, wuv_ref, gk_ref[...], cos, k_partner, k_ref, v_ref)


def _resident(shape):
    return pl.BlockSpec(shape, lambda i: (0,) * len(shape), pipeline_mode=pl.Buffered(1))


def _projections(x, mod, mod_row, wts, rope_tabs, emit_cache):
    n = x.shape[0]
    rope = rope_tabs is not None
    tile = lambda w: pl.BlockSpec((TM, w), lambda i: (i, 0))
    in_specs = [
        tile(D),
        pl.BlockSpec((1, 6, D), lambda i: (mod_row(i), 0, 0)),
        _resident((1, D)),
        _resident(wts["w_in_t"].shape), _resident(wts["w_kvr"].shape),
        _resident((1, QL)), _resident((1, KVL)), _resident((QL, DH)), _resident((1, HP)),
        _resident((KVL, DH)), _resident((1, HP)), _resident((KVL, DH)),
    ]
    args = [x, mod, wts["n1"], wts["w_in_t"], wts["w_kvr"],
            wts["qan"], wts["kvan"], wts["w_uq"], wts["gq"], wts["w_uk"], wts["gk"], wts["w_uv"]]
    if rope:
        tiles_per_seq = rope_tabs[0].shape[0] // TM
        in_specs += [_resident((QL, DH)), _resident((1, HP)), _resident((1, HP))]
        in_specs += [pl.BlockSpec((TM, HP), lambda i: (i % tiles_per_seq, 0))] * 2
        args += [wts["w_uq_pair"], wts["gq_pair"], wts["gk_pair"]] + list(rope_tabs)
    out_specs = [tile(D), tile(DR), tile(DR), tile(DH), tile(DH), tile(DH)]
    out_shape = [jax.ShapeDtypeStruct((n, D), BF), jax.ShapeDtypeStruct((n, DR), BF),
                 jax.ShapeDtypeStruct((n, DR), BF), jax.ShapeDtypeStruct((n, DH), BF),
                 jax.ShapeDtypeStruct((n, DH), BF), jax.ShapeDtypeStruct((n, DH), BF)]
    if emit_cache:
        out_specs += [tile(KVL), tile(ROPE)]
        out_shape += [jax.ShapeDtypeStruct((n, KVL), F32), jax.ShapeDtypeStruct((n, ROPE), F32)]
    return pl.pallas_call(
        functools.partial(_proj_kernel, rope=rope, emit_cache=emit_cache),
        grid=(n // TM,),
        in_specs=in_specs,
        out_specs=out_specs,
        out_shape=out_shape,
        compiler_params=_cparams(("arbitrary",)),
        name="projections",
    )(*args)


def _cache_kv_kernel(ckv_ref, krp_ref, wuk_ref, gk_ref, wuv_ref, k_ref, v_ref):
    _keys_values(ckv_ref[0], krp_ref[0], wuk_ref, wuv_ref, gk_ref[...], None, None,
                 k_ref.at[0], v_ref.at[0])


def _cache_keys_values(ckv, krp, wts):
    b, s, _ = ckv.shape
    full = lambda shape: pl.BlockSpec(shape, lambda i: (0,) * len(shape))
    return pl.pallas_call(
        _cache_kv_kernel,
        grid=(b,),
        in_specs=[pl.BlockSpec((1, s, KVL), lambda i: (i, 0, 0)),
                  pl.BlockSpec((1, s, HP), lambda i: (i, 0, 0)),
                  full((KVL, DH)), full((1, HP)), full((KVL, DH))],
        out_specs=[pl.BlockSpec((1, s, DH), lambda i: (i, 0, 0))] * 2,
        out_shape=[jax.ShapeDtypeStruct((b, s, DH), BF)] * 2,
        compiler_params=_cparams(("arbitrary",)),
        name="cache_keys_values",
    )(ckv, krp, wts["w_uk"], wts["gk"], wts["w_uv"])


WIN = 256
SEG = WIN // 8


def _window_permutation():
    dst = np.arange(WIN)
    p = np.zeros((WIN, WIN), np.float32)
    p[dst, (dst % SUBLANES) * SEG + dst // SUBLANES] = 1.0
    return p


def _sigmoid(x):
    return 0.5 * jnp.tanh(0.5 * x) + 0.5


def _segment_pass(a_scr, b_scr, bases, inits, out_scr=None):
    def body(k, carry):
        new = []
        for d in range(2):
            h, p = carry[d]
            i = k if d == 0 else SEG - 1 - k
            rows = pl.ds(pl.multiple_of(bases[d] + i * SUBLANES, SUBLANES), SUBLANES)
            a = a_scr[d, rows, :]
            h = a * h + b_scr[d, rows, :]
            if out_scr is None:
                p = a * p
            else:
                out_scr[d, rows, :] = h
            new.append((h, p))
        return tuple(new)

    init = tuple((inits[d], jnp.ones_like(inits[d])) for d in range(2))
    return lax.fori_loop(0, SEG, body, init, unroll=4)


def _segment_entries(end, decay, carry_in, forward):
    order = range(SUBLANES) if forward else reversed(range(SUBLANES))
    rows = [None] * SUBLANES
    c = carry_in
    for s in order:
        rows[s] = c
        c = end[s:s + 1, :] + decay[s:s + 1, :] * c
    return jnp.concatenate(rows, axis=0), c


def _rglru_kernel(*refs, t, has_h0, emit_state):
    it = iter(refs)
    xr_ref, gg_ref, perm_ref, unperm_ref = (next(it) for _ in range(4))
    cw_ref, cb_ref, bd_ref, ba_ref, bx_ref, lam_ref = (next(it) for _ in range(6))
    h0_ref = next(it) if has_h0 else None
    y_ref = next(it)
    hf_ref = next(it) if emit_state else None
    a_scr, b_scr, h_scr = (next(it) for _ in range(3))

    n_win = t // WIN
    sub = lax.broadcasted_iota(jnp.int32, (SUBLANES, CH), 0)
    zero_row = jnp.zeros((1, CH), F32)
    edge = 2 * SUBLANES
    for w in range(n_win):
        lo, hi = w * WIN, (w + 1) * WIN
        xp = _dot(perm_ref[...], xr_ref[0, lo:hi, :])
        before = xr_ref[0, lo - edge:lo, :].astype(F32)[edge - 1:edge, :] if w > 0 else zero_row
        after = xr_ref[0, hi:hi + edge, :].astype(F32) if w < n_win - 1 else None
        after0 = after[0:1, :] if after is not None else zero_row
        after1 = after[1:2, :] if after is not None else zero_row
        tile_m1 = jnp.where(sub == 0, before, pltpu.roll(xp[WIN - SUBLANES:WIN, :], 1, 0))
        tile_p0 = jnp.where(sub == SUBLANES - 1, after0, pltpu.roll(xp[0:SUBLANES, :], SUBLANES - 1, 0))
        tile_p1 = jnp.where(sub == SUBLANES - 1, after1,
                            pltpu.roll(xp[SUBLANES:2 * SUBLANES, :], SUBLANES - 1, 0))
        xe = jnp.concatenate([tile_m1, xp, tile_p0, tile_p1], axis=0)
        xc = cb_ref[...]
        for tap in range(4):
            xc = xc + xe[tap * SUBLANES:tap * SUBLANES + WIN, :] * cw_ref[tap:tap + 1, :]
        for s in range(CH // BDW):
            cols = slice(s * BDW, (s + 1) * BDW)
            xs = xc[:, cols]
            xsb = xs.astype(BF)
            xh = 0.5 * xs
            for d in range(2):
                tr = jnp.tanh(_dot(xsb, bd_ref[2 * d, s]) + ba_ref[d:d + 1, cols])
                ti = jnp.tanh(_dot(xsb, bd_ref[2 * d + 1, s]) + bx_ref[d:d + 1, cols])
                nl = -lam_ref[d:d + 1, cols]
                softplus = jnp.maximum(nl, 0.0) + jnp.log(1.0 + jnp.exp(-jnp.abs(nl)))
                ch = (-0.5 * LRU_C) * softplus
                a = jnp.exp(tr * ch + ch)
                z = 1.0 - a * a
                root = z * lax.rsqrt(jnp.maximum(z, TINY))
                a_scr[d, lo:hi, cols] = a
                b_scr[d, lo:hi, cols] = root * (ti * xh + xh)

    if has_h0:
        carry = [h0_ref[0, 0:1, :], h0_ref[0, 1:2, :]]
    else:
        carry = [zero_row, zero_row]
    zeros = jnp.zeros((SUBLANES, CH), F32)
    for k in range(n_win):
        bases = (k * WIN, (n_win - 1 - k) * WIN)
        totals = _segment_pass(a_scr, b_scr, bases, (zeros, zeros))
        entries = []
        for d in range(2):
            entry, carry[d] = _segment_entries(totals[d][0], totals[d][1], carry[d], d == 0)
            entries.append(entry)
        _segment_pass(a_scr, b_scr, bases, entries, out_scr=h_scr)

    if emit_state:
        hf_ref[0, 0:1, :] = carry[0]
        hf_ref[0, 1:2, :] = carry[1]
    for w in range(n_win):
        lo, hi = w * WIN, (w + 1) * WIN
        gate = _dot(perm_ref[...], gg_ref[0, lo:hi, :])
        yp = ((h_scr[0, lo:hi, :] + h_scr[1, lo:hi, :]) * gate).astype(BF)
        y_ref[0, lo:hi, :] = _dot(unperm_ref[...], yp).astype(BF)


def _rglru(xr, gg, wts, h0, emit_state):
    b, t, _ = xr.shape
    nc = DR // CH
    has_h0 = h0 is not None
    chunk = lambda r: pl.BlockSpec((r, CH), lambda i, j: (0, j))
    seq = pl.BlockSpec((1, t, CH), lambda i, j: (i, 0, j))
    state = pl.BlockSpec((1, 2, CH), lambda i, j: (i, 0, j))
    window = pl.BlockSpec((WIN, WIN), lambda i, j: (0, 0))
    perm = _window_permutation()
    in_specs = [seq, seq, window, window, chunk(4), chunk(1),
                pl.BlockSpec((4, CH // BDW, BDW, BDW), lambda i, j: (0, j, 0, 0)),
                chunk(2), chunk(2), chunk(2)]
    args = [xr, gg, jnp.asarray(perm, BF), jnp.asarray(perm.T, BF),
            wts["conv_w"], wts["conv_b"], wts["bd"], wts["lru_ba"], wts["lru_bx"], wts["lru_lam"]]
    if has_h0:
        in_specs.append(state)
        args.append(h0)
    out_specs = [seq]
    out_shape = [jax.ShapeDtypeStruct((b, t, DR), BF)]
    if emit_state:
        out_specs.append(state)
        out_shape.append(jax.ShapeDtypeStruct((b, 2, DR), F32))
    res = pl.pallas_call(
        functools.partial(_rglru_kernel, t=t, has_h0=has_h0, emit_state=emit_state),
        grid=(b, nc),
        in_specs=in_specs,
        out_specs=out_specs,
        out_shape=out_shape,
        scratch_shapes=[pltpu.VMEM((2, t, CH), F32)] * 3,
        compiler_params=_cparams(("arbitrary", "arbitrary")),
        name="rglru",
    )(*args)
    return res if emit_state else (res[0], None)


def _attn_kernel(*refs, t, n_seqs, n_heads, has_ctx, q_block):
    it = iter(refs)
    q_ref, k_ref, v_ref = next(it), next(it), next(it)
    kc_ref = vc_ref = None
    if has_ctx:
        kc_ref, vc_ref = next(it), next(it)
    o_ref = next(it)
    log2_scale = (QK ** -0.5) * math.log2(math.e)
    for sq, hd in [(sq, hd) for sq in range(n_seqs) for hd in range(n_heads)]:
        cols = slice(hd * HP, (hd + 1) * HP)
        k = k_ref[sq, :, cols]
        v = v_ref[sq, :, cols]
        if has_ctx:
            kc = kc_ref[sq, :, cols]
            vc = vc_ref[sq, :, cols]
        for qb in range(t // q_block):
            rows = slice(qb * q_block, (qb + 1) * q_block)
            q = q_ref[sq, rows, cols]
            s = _dot_nt(q, k) * log2_scale
            m = jnp.max(s, axis=-1, keepdims=True)
            if has_ctx:
                sc = _dot_nt(q, kc) * log2_scale
                m = jnp.maximum(m, jnp.max(sc, axis=-1, keepdims=True))
            p = jnp.exp2(s - m)
            den = jnp.sum(p, axis=-1, keepdims=True)
            o = _dot(p.astype(BF), v)
            if has_ctx:
                pc = jnp.exp2(sc - m)
                den = den + jnp.sum(pc, axis=-1, keepdims=True)
                o = o + _dot(pc.astype(BF), vc)
            o_ref[sq, rows, cols] = (o / den).astype(BF)


def _attention(q, k, v, kc, vc, seqs_per_step, heads_per_step):
    b, t, _ = q.shape
    has_ctx = kc is not None
    w = heads_per_step * HP
    blk = lambda n: pl.BlockSpec((seqs_per_step, n, w), lambda i, j: (i, 0, j))
    in_specs = [blk(t), blk(t), blk(t)]
    args = [q, k, v]
    if has_ctx:
        in_specs += [blk(kc.shape[1])] * 2
        args += [kc, vc]
    return pl.pallas_call(
        functools.partial(_attn_kernel, t=t, n_seqs=seqs_per_step, n_heads=heads_per_step,
                          has_ctx=has_ctx, q_block=min(t, 256)),
        grid=(b // seqs_per_step, NH // heads_per_step),
        in_specs=in_specs,
        out_specs=blk(t),
        out_shape=jax.ShapeDtypeStruct((b, t, DH), BF),
        compiler_params=_cparams(("arbitrary", "arbitrary")),
        name="attention",
    )(*args)


def _route(logits):
    lane = lax.broadcasted_iota(jnp.int32, logits.shape, 1)
    lanef = lane.astype(F32)
    neg = -jnp.inf
    big = float(LANES)
    gl = jnp.where((lane >= NE) & (lane < NE + NG), logits, neg)
    gmax = jnp.max(gl, axis=-1, keepdims=True)
    gidx = jnp.min(jnp.where(gl == gmax, lanef, big), axis=-1, keepdims=True) - float(NE)
    gw = 1.0 / jnp.sum(jnp.exp(gl - gmax), axis=-1, keepdims=True)
    lo = gidx * float(EPG)
    el = jnp.where((lanef >= lo) & (lanef < lo + float(EPG)), logits, neg)
    v1 = jnp.max(el, axis=-1, keepdims=True)
    i1 = jnp.min(jnp.where(el == v1, lanef, big), axis=-1, keepdims=True)
    el2 = jnp.where(lanef == i1, neg, el)
    v2 = jnp.max(el2, axis=-1, keepdims=True)
    i2 = jnp.min(jnp.where(el2 == v2, lanef, big), axis=-1, keepdims=True)
    e2 = jnp.exp(v2 - v1)
    w1 = gw / (1.0 + e2)
    w2 = gw * e2 / (1.0 + e2)
    cmb = jnp.where(lanef == i1, w1, 0.0) + jnp.where(lanef == i2, w2, 0.0)
    return cmb, gidx


def _out_kernel(x_ref, h_ref, yr_ref, ya_ref, mod_ref, win_ref, wor_ref, wom_ref, wout_ref,
                n2_ref, rw_ref, rb_ref, x1_ref, hx_ref, rinfo_ref, tcnt_ref):
    x = x_ref[...]
    gl = _dot_nt(h_ref[...], win_ref[O_GATE:O_GATE + 2 * D, :])
    merged2 = ((jnp.tanh(0.5 * gl[:, :D]) + 1.0) * _dot(yr_ref[...], wor_ref[...])
               + (jnp.tanh(0.5 * gl[:, D:]) + 1.0) * _dot(ya_ref[...], wom_ref[...]))
    mix = _dot(merged2.astype(BF), wout_ref[...])
    x1 = x + mod_ref[0, 2:3, :] * mix
    x1_ref[...] = x1
    h2 = _modulated_norm(x1, n2_ref[...], mod_ref[0, 4:5, :], mod_ref[0, 3:4, :])
    h2_hi = h2.astype(BF)
    h2_lo = (h2 - h2_hi.astype(F32)).astype(BF)
    part = _dot(h2_hi, rw_ref[...])
    logits = part[:, :LANES] + part[:, LANES:] + _dot(h2_lo, rw_ref[:, :LANES]) + rb_ref[...]
    cmb, gidx = _route(logits)

    lanef = lax.broadcasted_iota(jnp.int32, cmb.shape, 1).astype(F32)
    ghot = jnp.where(lanef == gidx, 1.0, 0.0)
    r_i = lax.broadcasted_iota(jnp.int32, (TM, TM), 0)
    c_i = lax.broadcasted_iota(jnp.int32, (TM, TM), 1)
    tri = jnp.where(r_i > c_i, 1.0, 0.0).astype(BF)
    earlier_same = jnp.sum(_dot(tri, ghot.astype(BF)) * ghot, axis=-1, keepdims=True)
    counts = jnp.sum(ghot, axis=0, keepdims=True)
    padded = jnp.floor((counts + (RUN_ALIGN - 1.0)) * (1.0 / RUN_ALIGN)) * RUN_ALIGN
    lower_groups = jnp.sum(jnp.where(lanef < gidx, padded, 0.0), axis=-1, keepdims=True)
    lpos = lower_groups + earlier_same
    s_i = lax.broadcasted_iota(jnp.int32, (TM, TS), 1)
    to_sorted = jnp.where(s_i.astype(F32) == lpos, 1.0, 0.0).astype(BF)
    c1 = cmb.astype(BF)
    c2 = (cmb - c1.astype(F32)).astype(BF)
    c3 = (cmb - c1.astype(F32) - c2.astype(F32)).astype(BF)
    payload = jnp.concatenate([h2.astype(BF), c1, c2, c3], axis=1)
    srt = lax.dot_general(to_sorted, payload, (((0,), (0,)), ((), ())), preferred_element_type=F32)
    hx_ref[...] = srt.astype(BF)

    rinfo_ref[...] = jnp.where(lanef == 0.0, gidx, jnp.where(lanef == 1.0, lpos, 0.0))
    tcnt_ref[0] = jnp.broadcast_to(counts, (SUBLANES, LANES))


def _merge_out(x, h, yr, ya, mod, mod_row, wts):
    n = x.shape[0]
    n_tiles = n // TM
    tile = lambda w: pl.BlockSpec((TM, w), lambda i: (i, 0))
    return pl.pallas_call(
        _out_kernel,
        grid=(n_tiles,),
        in_specs=[tile(D), tile(D), tile(DR), tile(DH),
                  pl.BlockSpec((1, 6, D), lambda i: (mod_row(i), 0, 0)),
                  _resident(wts["w_in_t"].shape), _resident((DR, D)), _resident((DH, D)), _resident((D, D)),
                  _resident((1, D)), _resident((D, 2 * LANES)), _resident((1, LANES))],
        out_specs=[tile(D), pl.BlockSpec((TS, XW), lambda i: (i, 0)), tile(LANES),
                   pl.BlockSpec((1, SUBLANES, LANES), lambda i: (i, 0, 0))],
        out_shape=[jax.ShapeDtypeStruct((n, D), F32), jax.ShapeDtypeStruct((n_tiles * TS, XW), BF),
                   jax.ShapeDtypeStruct((n, LANES), F32),
                   jax.ShapeDtypeStruct((n_tiles, SUBLANES, LANES), F32)],
        compiler_params=_cparams(("arbitrary",)),
        name="merge_out",
    )(x, h, yr, ya, mod, wts["w_in_t"], wts["w_o_rnn"], wts["w_o_mla"], wts["w_out"],
      wts["n2"], wts["router_w"], wts["router_b"])


def _run_copies(local_ref, far_ref, len_ref, first, count, make_copy, action):
    for g in range(count):
        idx = first + g
        n = len_ref[idx]
        local0 = local_ref[idx] if local_ref is not None else 0
        far0 = far_ref[idx]
        for k in reversed(range(RUN_ALIGN.bit_length() - 1, TS.bit_length())):
            size = 1 << k
            done = (n >> (k + 1)) << (k + 1)

            @pl.when(((n >> k) & 1) == 1)
            def _():
                src = local0 if local_ref is None else pl.multiple_of(local0 + done, RUN_ALIGN)
                action(make_copy(src, pl.multiple_of(far0 + done, RUN_ALIGN), size))


ZERO_RUNS = 12


def _dispatch_kernel(local_ref, far_ref, len_ref, zfar_ref, zlen_ref, hxc_ref, hxl_ref, xs_ref,
                     zero_ref, sem, *, n_ctx_tiles):
    tile = pl.program_id(0)

    def copy_from(src_ref):
        def copy(src, dst, size):
            return pltpu.make_async_copy(src_ref.at[pl.ds(src, size)], xs_ref.at[pl.ds(dst, size)], sem)
        return copy

    def move(copy, *tables):
        _run_copies(*tables, copy, lambda c: c.start())
        _run_copies(*tables, copy, lambda c: c.wait())

    @pl.when(tile == 0)
    def _():
        zero_ref[...] = jnp.zeros_like(zero_ref)
        move(copy_from(zero_ref), None, zfar_ref, zlen_ref, 0, ZERO_RUNS)

    @pl.when(tile < n_ctx_tiles)
    def _():
        move(copy_from(hxc_ref), local_ref, far_ref, len_ref, tile * NG, NG)

    @pl.when(tile >= n_ctx_tiles)
    def _():
        move(copy_from(hxl_ref), local_ref, far_ref, len_ref, tile * NG, NG)


def _dispatch(tables, zero_tables, hx_c, hx_l, n_rows):
    n_c, n_l = hx_c.shape[0] // TS, hx_l.shape[0] // TS
    return pl.pallas_call(
        functools.partial(_dispatch_kernel, n_ctx_tiles=n_c),
        grid_spec=pltpu.PrefetchScalarGridSpec(
            num_scalar_prefetch=5,
            grid=(n_c + n_l,),
            in_specs=[pl.BlockSpec((TS, XW), lambda i, *_: (jnp.minimum(i, n_c - 1), 0)),
                      pl.BlockSpec((TS, XW), lambda i, *_: (jnp.maximum(i - n_c, 0), 0))],
            out_specs=pl.BlockSpec(memory_space=pl.ANY),
            scratch_shapes=[pltpu.VMEM((TMOE, XW), BF), pltpu.SemaphoreType.DMA(())],
        ),
        out_shape=jax.ShapeDtypeStruct((n_rows, XW), BF),
        compiler_params=_cparams(("arbitrary",)),
        name="dispatch",
    )(*tables, *zero_tables, hx_c, hx_l)


def _opens_group(j, tg):
    return (j == 0) | (tg[j] != tg[jnp.maximum(j - 1, 0)])


def _moe_kernel(tb_ref, tg_ref, nt_ref, xs_ref, w1_ref, w3_ref, w2_ref, o_ref,
                acc_ref, w1c_ref, w3c_ref, w2c_ref):
    j = pl.program_id(0)
    pair = pl.program_id(1)

    @pl.when((j < nt_ref[0]) & _opens_group(j, tg_ref))
    def _():
        w1c_ref[pair] = w1_ref[...].astype(BF)
        w3c_ref[pair] = w3_ref[...].astype(BF)
        w2c_ref[pair] = w2_ref[...].reshape(EXPERTS_PER_STEP * DE, D).astype(BF)

    @pl.when(j < nt_ref[0])
    def _():
        xt = xs_ref[:, :D]
        cmb = (xs_ref[:, D:D + LANES].astype(F32) + xs_ref[:, D + LANES:D + 2 * LANES].astype(F32)
               + xs_ref[:, D + 2 * LANES:].astype(F32))
        lane = lax.broadcasted_iota(jnp.int32, cmb.shape, 1)
        first = tg_ref[j] * EPG + pair * EXPERTS_PER_STEP
        hidden = []
        for u in range(EXPERTS_PER_STEP):
            a = _dot(xt, w1c_ref[pair, u])
            he = (a * _sigmoid(a)) * _dot(xt, w3c_ref[pair, u])
            ce = jnp.sum(jnp.where(lane == first + u, cmb, 0.0), axis=-1, keepdims=True)
            hidden.append((he * ce).astype(BF))
        y = _dot(jnp.concatenate(hidden, axis=1), w2c_ref[pair])

        @pl.when(pair == 0)
        def _():
            acc_ref[...] = y

        @pl.when(pair == EPG // EXPERTS_PER_STEP - 1)
        def _():
            o_ref[...] = (acc_ref[...] + y).astype(BF)

    @pl.when((j >= nt_ref[0]) & (pair == EPG // EXPERTS_PER_STEP - 1))
    def _():
        o_ref[...] = jnp.zeros_like(o_ref)


def _experts(tile_block, tile_group, n_tiles, xs, wts):
    m = xs.shape[0]
    steps = EPG // EXPERTS_PER_STEP
    assert steps == 2, "the kernel keeps one partial sum: first step stores it, second adds and writes"

    def w_idx(j, e, tb, tg, nt):
        needed = (j < nt[0]) & _opens_group(j, tg)
        return (tg[j] * steps + jnp.where(needed, e, steps - 1), 0, 0)

    return pl.pallas_call(
        _moe_kernel,
        grid_spec=pltpu.PrefetchScalarGridSpec(
            num_scalar_prefetch=3,
            grid=(m // TMOE, steps),
            in_specs=[pl.BlockSpec((TMOE, XW), lambda j, e, tb, tg, nt: (tb[j], 0)),
                      pl.BlockSpec((EXPERTS_PER_STEP, D, DE), w_idx),
                      pl.BlockSpec((EXPERTS_PER_STEP, D, DE), w_idx),
                      pl.BlockSpec((EXPERTS_PER_STEP, DE, D), w_idx)],
            out_specs=pl.BlockSpec((TMOE, D), lambda j, e, tb, tg, nt: (j, 0)),
            scratch_shapes=[pltpu.VMEM((TMOE, D), F32),
                            pltpu.VMEM((steps, EXPERTS_PER_STEP, D, DE), BF),
                            pltpu.VMEM((steps, EXPERTS_PER_STEP, D, DE), BF),
                            pltpu.VMEM((steps, EXPERTS_PER_STEP * DE, D), BF)],
        ),
        out_shape=jax.ShapeDtypeStruct((m, D), BF),
        compiler_params=_cparams(("arbitrary", "arbitrary")),
        name="experts",
    )(tile_block, tile_group, n_tiles, xs, wts["exp_w1"], wts["exp_w3"], wts["exp_w2"])


def _combine_kernel(local_ref, far_ref, len_ref, x1_ref, rinfo_ref, mod_ref, ys_ref, o_ref,
                    buf_ref, sem, *, n_steps):
    i = pl.program_id(0)

    def runs(step, slot, action):
        def copy(dst, src, size):
            return pltpu.make_async_copy(ys_ref.at[pl.ds(src, size)],
                                         buf_ref.at[slot, pl.ds(dst, size)], sem.at[slot])

        _run_copies(local_ref, far_ref, len_ref, step * NG, NG, copy, action)

    @pl.when(i == 0)
    def _():
        buf_ref[...] = jnp.zeros_like(buf_ref)
        runs(0, 0, lambda c: c.start())

    @pl.when(i + 1 < n_steps)
    def _():
        runs(i + 1, (i + 1) % 2, lambda c: c.start())

    slot = i % 2
    runs(i, slot, lambda c: c.wait())
    lpos = rinfo_ref[:, 1:2]
    s_i = lax.broadcasted_iota(jnp.int32, (TM, TS), 1)
    from_sorted = jnp.where(s_i.astype(F32) == lpos, 1.0, 0.0).astype(BF)
    moe = _dot(from_sorted, buf_ref[slot])
    o_ref[...] = x1_ref[...] + mod_ref[0, 5:6, :] * moe


def _combine(tables, x1, rinfo, mod, mod_row, ys):
    n = x1.shape[0]
    n_steps = n // TM
    return pl.pallas_call(
        functools.partial(_combine_kernel, n_steps=n_steps),
        grid_spec=pltpu.PrefetchScalarGridSpec(
            num_scalar_prefetch=3,
            grid=(n_steps,),
            in_specs=[pl.BlockSpec((TM, D), lambda i, *_: (i, 0)),
                      pl.BlockSpec((TM, LANES), lambda i, *_: (i, 0)),
                      pl.BlockSpec((1, 6, D), lambda i, *_: (mod_row(i), 0, 0)),
                      pl.BlockSpec(memory_space=pl.ANY)],
            out_specs=pl.BlockSpec((TM, D), lambda i, *_: (i, 0)),
            scratch_shapes=[pltpu.VMEM((2, TS, D), BF), pltpu.SemaphoreType.DMA((2,))],
        ),
        out_shape=jax.ShapeDtypeStruct((n, D), F32),
        compiler_params=_cparams(("arbitrary",)),
        name="combine",
    )(*tables, x1, rinfo, mod, ys)


def _run_lengths(tile_counts):
    counts = tile_counts[:, 0, :NG].astype(jnp.int32)
    return ((counts + RUN_ALIGN - 1) // RUN_ALIGN) * RUN_ALIGN


def _run_tables(lengths, first_far):
    local = jnp.cumsum(lengths, axis=1) - lengths
    far = first_far[None, :] + jnp.cumsum(lengths, axis=0) - lengths
    flat = lambda a: a.astype(jnp.int32).reshape(-1)
    return flat(local), flat(far), flat(lengths)


def _group_layout(counts, max_tiles):
    padded = ((counts + TMOE - 1) // TMOE) * TMOE
    ends = jnp.cumsum(padded)
    offsets = ends - padded
    n_tiles = (ends[-1] // TMOE).astype(jnp.int32)
    tile = jnp.minimum(jnp.arange(max_tiles, dtype=jnp.int32), jnp.maximum(n_tiles - 1, 0))
    tile_group = jnp.sum((tile[:, None] * TMOE >= ends[None, :]).astype(jnp.int32), axis=1)
    spare = ends[-1] + TMOE * jnp.arange(ZERO_RUNS - NG, dtype=jnp.int32)
    spare_len = jnp.where(spare < max_tiles * TMOE, TMOE, 0)
    zero_start = jnp.concatenate([offsets + counts, jnp.where(spare_len > 0, spare, 0)]).astype(jnp.int32)
    zero_len = jnp.concatenate([padded - counts, spare_len]).astype(jnp.int32)
    return offsets, tile, tile_group, n_tiles.reshape(1), (zero_start, zero_len)


def _pad_heads(w, perm=None, rotary_only=False):
    lead = w.shape[:-1]
    per = w.shape[-1] // NH
    w = w.reshape(lead + (NH, per))
    if perm is not None:
        nope = jnp.zeros_like(w[..., :NOPE]) if rotary_only else w[..., :NOPE]
        w = jnp.concatenate([nope, w[..., NOPE:][..., perm]], axis=-1)
    w = jnp.pad(w, [(0, 0)] * len(lead) + [(0, 0), (0, HP - per)])
    return w.reshape(lead + (NH * HP,))


def _pad_gain(g, perm, rotary_only=False):
    nope = jnp.zeros((NOPE,), F32) if rotary_only else g[:NOPE]
    g = jnp.concatenate([nope, g[NOPE:][perm], jnp.zeros((HP - QK,), F32)])
    return g.reshape(1, HP)


def _block_diag(w):
    per = BDW // LRU_BLOCK
    rows = w.reshape(DR // BDW, BDW, LRU_BLOCK)
    idx = np.arange(BDW) // LRU_BLOCK
    mask = jnp.asarray(idx[:, None] == idx[None, :], w.dtype)
    return jnp.concatenate([rows] * per, axis=-1) * mask


def _prepare_shared(l, p):
    w_in = lambda a, b: jnp.transpose(lax.slice(p["w_in"], (l, 0, a), (l + 1, D, b)).reshape(D, b - a))
    bd = jnp.stack([_block_diag(p["lru_wa"][l, 0]), _block_diag(p["lru_wx"][l, 0]),
                    _block_diag(p["lru_wa"][l, 1]), _block_diag(p["lru_wx"][l, 1])])
    bd = (0.5 * bd).astype(BF)
    wom = p["w_o_mla"][l].reshape(NH, VD, D)
    wom = jnp.pad(wom, ((0, 0), (0, HP - VD), (0, 0))).reshape(DH, D)
    router_w = jnp.concatenate([p["router_we"][l], p["router_wg"][l],
                                jnp.zeros((D, LANES - NE - NG), F32)], axis=1)
    router_b = jnp.concatenate([p["router_be"][l], p["router_bg"][l],
                                jnp.zeros((LANES - NE - NG,), F32)]).reshape(1, LANES)
    router_hi = router_w.astype(BF)
    router_lo = (router_w - router_hi.astype(F32)).astype(BF)
    router_w = jnp.concatenate([router_hi, router_lo], axis=1)
    return {
        "n1": p["norm1_g"][l].reshape(1, D), "n2": p["norm2_g"][l].reshape(1, D),
        "w_in_t": jnp.transpose(p["w_in"][l]).astype(BF),
        "w_kv": w_in(O_KV, O_KR), "w_kr": w_in(O_KR, O_GATE),
        "qan": p["q_a_norm"][l].reshape(1, QL), "kvan": p["kv_a_norm"][l].reshape(1, KVL),
        "w_uk": _pad_heads(p["w_uk"][l]).astype(BF),
        "w_uv": _pad_heads(p["w_uv"][l]).astype(BF),
        "conv_w": p["conv_w"][l], "conv_b": p["conv_b"][l].reshape(1, DR), "bd": bd,
        "lru_ba": 0.5 * p["lru_ba"][l], "lru_bx": 0.5 * p["lru_bx"][l], "lru_lam": p["lru_lam"][l],
        "w_o_rnn": p["w_o_rnn"][l].astype(BF), "w_o_mla": wom.astype(BF), "w_out": (0.5 * p["w_out"][l]).astype(BF),
        "router_w": router_w, "router_b": router_b,
        "exp_w1": p["exp_w1"][l], "exp_w3": p["exp_w3"][l], "exp_w2": p["exp_w2"][l],
    }


def _with_rope_order(l, p, shared, perm, rotary):
    w = dict(shared)
    zeros = lambda n: jnp.zeros((n, D), F32)
    rope_block = lambda order: [zeros(NOPE), shared["w_kr"][order, :], zeros(HP - QK)]
    kvr = [shared["w_kv"]] + rope_block(perm)
    w["w_uq"] = _pad_heads(p["w_uq"][l], perm).astype(BF)
    w["gq"] = _pad_gain(p["q_norm"][l], perm)
    w["gk"] = _pad_gain(p["k_norm"][l], perm)
    if rotary:
        pair = np.concatenate([perm[ROPE // 2:], perm[:ROPE // 2]])
        kvr += rope_block(pair)
        w["w_uq_pair"] = _pad_heads(p["w_uq"][l], pair, rotary_only=True).astype(BF)
        w["gq_pair"] = _pad_gain(p["q_norm"][l], pair, rotary_only=True)
        w["gk_pair"] = _pad_gain(p["k_norm"][l], pair, rotary_only=True)
    w["w_kvr"] = jnp.concatenate(kvr, axis=0).astype(BF)
    return w


def _rope_tables(n_tokens):
    rows = n_tokens // GRID_W
    row = np.repeat(np.arange(rows), GRID_W).astype(np.float32)
    col = np.tile(np.arange(GRID_W), rows).astype(np.float32)
    axis_dim = ROPE // 2
    inv = (np.float32(ROPE_BASE) ** (-np.arange(0, axis_dim, 2, dtype=np.float32) / axis_dim)).astype(np.float32)
    ang = np.concatenate([row[:, None] * inv, col[:, None] * inv], axis=-1).astype(np.float32)
    cos, sin = np.cos(ang), np.sin(ang)
    ones = lambda n: np.ones((n_tokens, n), np.float32)
    zeros = lambda n: np.zeros((n_tokens, n), np.float32)
    cos_t = np.concatenate([ones(NOPE), cos, cos, ones(HP - QK)], axis=1)
    sin_t = np.concatenate([zeros(NOPE), -sin, sin, zeros(HP - QK)], axis=1)
    return jnp.asarray(cos_t, F32), jnp.asarray(sin_t, F32)


def kernel(x_prompt, x_sample, cache_mla_ckv, cache_mla_krope, state_rglru, c, c_ctx, norm1_g, norm2_g, w_mod, b_mod, w_in, conv_w, conv_b, lru_wa, lru_ba, lru_wx, lru_bx, lru_lam, q_a_norm, kv_a_norm, w_uq, w_uk, w_uv, q_norm, k_norm, w_o_rnn, w_o_mla, w_out, router_wg, router_bg, router_we, router_be, exp_w1, exp_w3, exp_w2):
    p = dict(norm1_g=norm1_g, norm2_g=norm2_g, w_in=w_in, conv_w=conv_w, conv_b=conv_b,
             lru_wa=lru_wa, lru_ba=lru_ba, lru_wx=lru_wx, lru_bx=lru_bx, lru_lam=lru_lam,
             q_a_norm=q_a_norm, kv_a_norm=kv_a_norm, w_uq=w_uq, w_uk=w_uk, w_uv=w_uv,
             q_norm=q_norm, k_norm=k_norm, w_o_rnn=w_o_rnn, w_o_mla=w_o_mla, w_out=w_out,
             router_wg=router_wg, router_bg=router_bg, router_we=router_we, router_be=router_be,
             exp_w1=exp_w1, exp_w3=exp_w3, exp_w2=exp_w2)
    depth = w_in.shape[0]
    nb, seq, _ = x_prompt.shape
    db, dseq, _ = x_sample.shape
    ident = np.arange(ROPE)
    halves = np.concatenate([np.arange(0, ROPE, 2), np.arange(1, ROPE, 2)])
    rope_tabs = _rope_tables(dseq)
    cond8 = jnp.concatenate([c_ctx[None, :], c, jnp.zeros((SUBLANES - 1 - db, D), F32)], axis=0)
    ctx_row = lambda tile_rows: (lambda i: 0)
    lat_row = lambda tile_rows: (lambda i: (i * tile_rows) // dseq + 1)
    n_ctx, n_lat = nb * seq, db * dseq
    run_padding = ((n_ctx + n_lat) // TM) * NG * (RUN_ALIGN - 1)
    max_tiles = -(-(n_ctx + n_lat + run_padding) // TMOE) + NG
    assert max_tiles - (n_ctx + n_lat) // TMOE <= ZERO_RUNS - NG, "more unused expert tiles than zero runs"
    per_seq = lambda arrs, b, t: [a.reshape(b, t, a.shape[-1]) for a in arrs]
    flat = lambda a: a.reshape(-1, a.shape[-1])

    y_prompt, y_sample = x_prompt.reshape(n_ctx, D), x_sample.reshape(n_lat, D)
    ckv_list, krope_list, rnn_list = [], [], []
    for l in range(depth):
        shared = _prepare_shared(l, p)
        w_ctx = _with_rope_order(l, p, shared, ident, False)
        w_lat = _with_rope_order(l, p, shared, halves, True)
        mod = _modulation(cond8, w_mod[l], b_mod[l]).reshape(SUBLANES, 6, D)

        h, xr, gg, q, k, v, ckv, kro = _projections(y_prompt, mod, ctx_row(TM), w_ctx, None, True)
        xr, gg, q, k, v = per_seq([xr, gg, q, k, v], nb, seq)
        yr, h_fin = _rglru(xr, gg, shared, None, True)
        ya = _attention(q, k, v, None, None, 4, NH)
        x1_c, hx_c, ri_c, tc_c = _merge_out(y_prompt, h, flat(yr), flat(ya), mod, ctx_row(TM), shared)
        ckv_list.append(ckv.reshape(nb, seq, KVL))
        krope_list.append(kro.reshape(nb, seq, ROPE))
        rnn_list.append(h_fin)

        krp_cache = jnp.pad(cache_mla_krope[:, l][..., halves], ((0, 0), (0, 0), (NOPE, HP - QK)))
        kc, vc = _cache_keys_values(cache_mla_ckv[:, l], krp_cache, w_lat)
        h, xr, gg, q, k, v = _projections(y_sample, mod, lat_row(TM), w_lat, rope_tabs, False)
        xr, gg, q, k, v = per_seq([xr, gg, q, k, v], db, dseq)
        yr, _ = _rglru(xr, gg, shared, state_rglru[:, l], False)
        ya = _attention(q, k, v, kc, vc, 1, 4)
        x1_l, hx_l, ri_l, tc_l = _merge_out(y_sample, h, flat(yr), flat(ya), mod, lat_row(TM), shared)

        len_c, len_l = _run_lengths(tc_c), _run_lengths(tc_l)
        rows_c = jnp.sum(len_c, axis=0)
        offsets, tile_block, tile_group, n_tiles, zero_runs = _group_layout(
            rows_c + jnp.sum(len_l, axis=0), max_tiles)
        runs_c = _run_tables(len_c, offsets)
        runs_l = _run_tables(len_l, offsets + rows_c)
        runs = tuple(jnp.concatenate([a, b]) for a, b in zip(runs_c, runs_l))
        xs = _dispatch(runs, zero_runs, hx_c, hx_l, max_tiles * TMOE)
        ys = _experts(tile_block, tile_group, n_tiles, xs, shared)
        y_prompt = _combine(runs_c, x1_c, ri_c, mod, ctx_row(TM), ys)
        y_sample = _combine(runs_l, x1_l, ri_l, mod, lat_row(TM), ys)

    y_prompt, y_sample = y_prompt.reshape(nb, seq, D), y_sample.reshape(db, dseq, D)

    return (y_prompt, y_sample, jnp.stack(ckv_list, axis=1), jnp.stack(krope_list, axis=1),
            jnp.stack(rnn_list, axis=1))
```

```python
import functools
import math

import numpy as np
import jax
import jax.numpy as jnp
from jax import lax
from jax.experimental import pallas as pl
from jax.experimental.pallas import tpu as pltpu

D = 1024
DR = 1024
QL = 384
KVL = 256
NH = 8
NOPE = 64
ROPE = 32
QK = NOPE + ROPE
SCORE_SCALE = (QK ** -0.5) * math.log2(math.e)
VD = 64
O_G, O_Q, O_KV, O_KR, O_GATE = DR, 2 * DR, 2 * DR + QL, 2 * DR + QL + KVL, 2 * DR + QL + KVL + ROPE
HP = 128
DH = NH * HP
GRID_W = 64
ROPE_BASE = 10000.0
EPS = 1e-6
TINY = 1e-30
LRU_C = 8.0
LRU_BLOCK = 64
BDW = 256
CH = 1024
NG = 4
EPG = 4
NE = NG * EPG
DE = 512
LANES = 128
SUBLANES = 8
TM = 512
TS = 640
TMOE = 512
EXPERTS_PER_STEP = 2
XW = D + 3 * LANES
RUN_ALIGN = 16
VMEM_LIMIT = 52 * 1024 * 1024
BF = jnp.bfloat16
F32 = jnp.float32


def _cparams(sem):
    return pltpu.CompilerParams(dimension_semantics=sem, vmem_limit_bytes=VMEM_LIMIT)


def _dot(a, b):
    return jnp.dot(a, b, preferred_element_type=F32)


def _dot_nt(a, b):
    return lax.dot_general(a, b, (((1,), (1,)), ((), ())), preferred_element_type=F32)


def _rms(x, g, width):
    ms = jnp.sum(x * x, axis=-1, keepdims=True) * (1.0 / width)
    return x * lax.rsqrt(ms + EPS) * g


def _modulated_norm(x, g, scale, shift):
    return _rms(x, g * (1.0 + scale), D) + shift


def _mod_kernel(c_ref, w_ref, b_ref, o_ref):
    c = c_ref[...]
    s = c * jax.nn.sigmoid(c)
    o_ref[...] = _dot(s, w_ref[...]) + b_ref[...]


def _modulation(cond8, w_mod, b_mod):
    n = w_mod.shape[1]
    return pl.pallas_call(
        _mod_kernel,
        grid=(n // D,),
        in_specs=[
            pl.BlockSpec((SUBLANES, D), lambda j: (0, 0)),
            pl.BlockSpec((D, D), lambda j: (0, j)),
            pl.BlockSpec((1, D), lambda j: (0, j)),
        ],
        out_specs=pl.BlockSpec((SUBLANES, D), lambda j: (0, j)),
        out_shape=jax.ShapeDtypeStruct((SUBLANES, n), F32),
        compiler_params=_cparams(("arbitrary",)),
        name="modulation",
    )(cond8, w_mod, b_mod.reshape(1, n))


def _head_norm(xh, gain, cos=None, partner_scaled=None):
    ms = jnp.sum(xh * xh, axis=-1, keepdims=True) * (1.0 / QK)
    rs = lax.rsqrt(ms + EPS)
    y = xh * rs * gain
    if cos is None:
        return y
    return y * cos + partner_scaled * rs


def _keys_values(ckv, krp, wuk_ref, wuv_ref, gk, cos, partner_scaled, k_ref, v_ref):
    cb = ckv.astype(BF)
    kn = _dot(cb, wuk_ref[...])
    v_ref[...] = _dot(cb, wuv_ref[...]).astype(BF)
    for h in range(NH):
        kh = kn[:, h * HP:(h + 1) * HP] + krp
        k_ref[:, h * HP:(h + 1) * HP] = _head_norm(kh, gk, cos, partner_scaled).astype(BF)


def _proj_kernel(*refs, rope, emit_cache):
    it = iter(refs)
    x_ref, mod_ref, n1_ref = next(it), next(it), next(it)
    win_ref, wkvr_ref = next(it), next(it)
    wx_ref, wg_ref, wq_ref = win_ref.at[0:O_G], win_ref.at[O_G:O_Q], win_ref.at[O_Q:O_KV]
    qan_ref, kvan_ref, wuq_ref, gq_ref = next(it), next(it), next(it), next(it)
    wuk_ref, gk_ref, wuv_ref = next(it), next(it), next(it)
    if rope:
        wuqs_ref, gqs_ref, gks_ref, cos_ref, sins_ref = (next(it) for _ in range(5))
    h_ref, xr_ref, gg_ref, q_ref, k_ref, v_ref = (next(it) for _ in range(6))
    if emit_cache:
        ckv_ref, kro_ref = next(it), next(it)

    hb = _modulated_norm(x_ref[...], n1_ref[...], mod_ref[0, 1:2, :], mod_ref[0, 0:1, :]).astype(BF)
    h_ref[...] = hb
    xr_ref[...] = _dot_nt(hb, wx_ref[...]).astype(BF)
    gg_ref[...] = jax.nn.gelu(_dot_nt(hb, wg_ref[...])).astype(BF)

    qnb = _rms(_dot_nt(hb, wq_ref[...]), qan_ref[...], QL).astype(BF)
    q = _dot(qnb, wuq_ref[...])
    gq = gq_ref[...]
    cos = q_partner = q_pair_scale = None
    if rope:
        cos, sins = cos_ref[...], sins_ref[...]
        q_partner = _dot(qnb, wuqs_ref[...])
        q_pair_scale = gqs_ref[...] * sins
    for hd in range(NH):
        cols = slice(hd * HP, (hd + 1) * HP)
        partner = q_partner[:, cols] * q_pair_scale if rope else None
        q_ref[:, cols] = _head_norm(q[:, cols], gq, cos, partner).astype(BF)

    kvr = _dot_nt(hb, wkvr_ref[...])
    ckv = _rms(kvr[:, :KVL], kvan_ref[...], KVL)
    krp = kvr[:, KVL:KVL + HP]
    k_partner = kvr[:, KVL + HP:KVL + 2 * HP] * (gks_ref[...] * sins) if rope else None
    if emit_cache:
        ckv_ref[...] = ckv
        kro_ref[...] = krp[:, NOPE:NOPE + ROPE]
    _keys_values(ckv, krp, wuk_ref, wuv_ref, gk_ref[...], cos, k_partner, k_ref, v_ref)


def _resident(shape):
    return pl.BlockSpec(shape, lambda i: (0,) * len(shape), pipeline_mode=pl.Buffered(1))


def _projections(x, mod, mod_row, wts, rope_tabs, emit_cache):
    n = x.shape[0]
    rope = rope_tabs is not None
    tile = lambda w: pl.BlockSpec((TM, w), lambda i: (i, 0))
    in_specs = [
        tile(D),
        pl.BlockSpec((1, 6, D), lambda i: (mod_row(i), 0, 0)),
        _resident((1, D)),
        _resident(wts["w_in_t"].shape), _resident(wts["w_kvr"].shape),
        _resident((1, QL)), _resident((1, KVL)), _resident((QL, DH)), _resident((1, HP)),
        _resident((KVL, DH)), _resident((1, HP)), _resident((KVL, DH)),
    ]
    args = [x, mod, wts["n1"], wts["w_in_t"], wts["w_kvr"],
            wts["qan"], wts["kvan"], wts["w_uq"], wts["gq"], wts["w_uk"], wts["gk"], wts["w_uv"]]
    if rope:
        tiles_per_seq = rope_tabs[0].shape[0] // TM
        in_specs += [_resident((QL, DH)), _resident((1, HP)), _resident((1, HP))]
        in_specs += [pl.BlockSpec((TM, HP), lambda i: (i % tiles_per_seq, 0))] * 2
        args += [wts["w_uq_pair"], wts["gq_pair"], wts["gk_pair"]] + list(rope_tabs)
    out_specs = [tile(D), tile(DR), tile(DR), tile(DH), tile(DH), tile(DH)]
    out_shape = [jax.ShapeDtypeStruct((n, D), BF), jax.ShapeDtypeStruct((n, DR), BF),
                 jax.ShapeDtypeStruct((n, DR), BF), jax.ShapeDtypeStruct((n, DH), BF),
                 jax.ShapeDtypeStruct((n, DH), BF), jax.ShapeDtypeStruct((n, DH), BF)]
    if emit_cache:
        out_specs += [tile(KVL), tile(ROPE)]
        out_shape += [jax.ShapeDtypeStruct((n, KVL), F32), jax.ShapeDtypeStruct((n, ROPE), F32)]
    return pl.pallas_call(
        functools.partial(_proj_kernel, rope=rope, emit_cache=emit_cache),
        grid=(n // TM,),
        in_specs=in_specs,
        out_specs=out_specs,
        out_shape=out_shape,
        compiler_params=_cparams(("arbitrary",)),
        name="projections",
    )(*args)


def _cache_kv_kernel(ckv_ref, krp_ref, wuk_ref, gk_ref, wuv_ref, k_ref, v_ref):
    _keys_values(ckv_ref[0], krp_ref[0], wuk_ref, wuv_ref, gk_ref[...], None, None,
                 k_ref.at[0], v_ref.at[0])


def _cache_keys_values(ckv, krp, wts):
    b, s, _ = ckv.shape
    full = lambda shape: pl.BlockSpec(shape, lambda i: (0,) * len(shape))
    return pl.pallas_call(
        _cache_kv_kernel,
        grid=(b,),
        in_specs=[pl.BlockSpec((1, s, KVL), lambda i: (i, 0, 0)),
                  pl.BlockSpec((1, s, HP), lambda i: (i, 0, 0)),
                  full((KVL, DH)), full((1, HP)), full((KVL, DH))],
        out_specs=[pl.BlockSpec((1, s, DH), lambda i: (i, 0, 0))] * 2,
        out_shape=[jax.ShapeDtypeStruct((b, s, DH), BF)] * 2,
        compiler_params=_cparams(("arbitrary",)),
        name="cache_keys_values",
    )(ckv, krp, wts["w_uk"], wts["gk"], wts["w_uv"])


WIN = 256
SEG = WIN // 8


def _window_permutation():
    dst = np.arange(WIN)
    p = np.zeros((WIN, WIN), np.float32)
    p[dst, (dst % SUBLANES) * SEG + dst // SUBLANES] = 1.0
    return p


def _sigmoid(x):
    return 0.5 * jnp.tanh(0.5 * x) + 0.5


def _segment_pass(a_scr, b_scr, bases, inits, out_scr=None):
    def body(k, carry):
        new = []
        for d in range(2):
            h, p = carry[d]
            i = k if d == 0 else SEG - 1 - k
            rows = pl.ds(pl.multiple_of(bases[d] + i * SUBLANES, SUBLANES), SUBLANES)
            a = a_scr[d, rows, :]
            h = a * h + b_scr[d, rows, :]
            if out_scr is None:
                p = a * p
            else:
                out_scr[d, rows, :] = h
            new.append((h, p))
        return tuple(new)

    init = tuple((inits[d], jnp.ones_like(inits[d])) for d in range(2))
    return lax.fori_loop(0, SEG, body, init, unroll=4)


def _segment_entries(end, decay, carry_in, forward):
    order = range(SUBLANES) if forward else reversed(range(SUBLANES))
    rows = [None] * SUBLANES
    c = carry_in
    for s in order:
        rows[s] = c
        c = end[s:s + 1, :] + decay[s:s + 1, :] * c
    return jnp.concatenate(rows, axis=0), c


def _rglru_kernel(*refs, t, has_h0, emit_state):
    it = iter(refs)
    xr_ref, gg_ref, perm_ref, unperm_ref = (next(it) for _ in range(4))
    cw_ref, cb_ref, bd_ref, ba_ref, bx_ref, lam_ref = (next(it) for _ in range(6))
    h0_ref = next(it) if has_h0 else None
    y_ref = next(it)
    hf_ref = next(it) if emit_state else None
    a_scr, b_scr, h_scr = (next(it) for _ in range(3))

    n_win = t // WIN
    sub = lax.broadcasted_iota(jnp.int32, (SUBLANES, CH), 0)
    zero_row = jnp.zeros((1, CH), F32)
    edge = 2 * SUBLANES
    for w in range(n_win):
        lo, hi = w * WIN, (w + 1) * WIN
        xp = _dot(perm_ref[...], xr_ref[0, lo:hi, :])
        before = xr_ref[0, lo - edge:lo, :].astype(F32)[edge - 1:edge, :] if w > 0 else zero_row
        after = xr_ref[0, hi:hi + edge, :].astype(F32) if w < n_win - 1 else None
        after0 = after[0:1, :] if after is not None else zero_row
        after1 = after[1:2, :] if after is not None else zero_row
        tile_m1 = jnp.where(sub == 0, before, pltpu.roll(xp[WIN - SUBLANES:WIN, :], 1, 0))
        tile_p0 = jnp.where(sub == SUBLANES - 1, after0, pltpu.roll(xp[0:SUBLANES, :], SUBLANES - 1, 0))
        tile_p1 = jnp.where(sub == SUBLANES - 1, after1,
                            pltpu.roll(xp[SUBLANES:2 * SUBLANES, :], SUBLANES - 1, 0))
        xe = jnp.concatenate([tile_m1, xp, tile_p0, tile_p1], axis=0)
        xc = cb_ref[...]
        for tap in range(4):
            xc = xc + xe[tap * SUBLANES:tap * SUBLANES + WIN, :] * cw_ref[tap:tap + 1, :]
        for s in range(CH // BDW):
            cols = slice(s * BDW, (s + 1) * BDW)
            xs = xc[:, cols]
            xsb = xs.astype(BF)
            xh = 0.5 * xs
            for d in range(2):
                tr = jnp.tanh(_dot(xsb, bd_ref[2 * d, s]) + ba_ref[d:d + 1, cols])
                ti = jnp.tanh(_dot(xsb, bd_ref[2 * d + 1, s]) + bx_ref[d:d + 1, cols])
                nl = -lam_ref[d:d + 1, cols]
                softplus = jnp.maximum(nl, 0.0) + jnp.log(1.0 + jnp.exp(-jnp.abs(nl)))
                ch = (-0.5 * LRU_C) * softplus
                a = jnp.exp(tr * ch + ch)
                z = 1.0 - a * a
                root = z * lax.rsqrt(jnp.maximum(z, TINY))
                a_scr[d, lo:hi, cols] = a
                b_scr[d, lo:hi, cols] = root * (ti * xh + xh)

    if has_h0:
        carry = [h0_ref[0, 0:1, :], h0_ref[0, 1:2, :]]
    else:
        carry = [zero_row, zero_row]
    zeros = jnp.zeros((SUBLANES, CH), F32)
    for k in range(n_win):
        bases = (k * WIN, (n_win - 1 - k) * WIN)
        totals = _segment_pass(a_scr, b_scr, bases, (zeros, zeros))
        entries = []
        for d in range(2):
            entry, carry[d] = _segment_entries(totals[d][0], totals[d][1], carry[d], d == 0)
            entries.append(entry)
        _segment_pass(a_scr, b_scr, bases, entries, out_scr=h_scr)

    if emit_state:
        hf_ref[0, 0:1, :] = carry[0]
        hf_ref[0, 1:2, :] = carry[1]
    for w in range(n_win):
        lo, hi = w * WIN, (w + 1) * WIN
        gate = _dot(perm_ref[...], gg_ref[0, lo:hi, :])
        yp = ((h_scr[0, lo:hi, :] + h_scr[1, lo:hi, :]) * gate).astype(BF)
        y_ref[0, lo:hi, :] = _dot(unperm_ref[...], yp).astype(BF)


def _rglru(xr, gg, wts, h0, emit_state):
    b, t, _ = xr.shape
    nc = DR // CH
    has_h0 = h0 is not None
    chunk = lambda r: pl.BlockSpec((r, CH), lambda i, j: (0, j))
    seq = pl.BlockSpec((1, t, CH), lambda i, j: (i, 0, j))
    state = pl.BlockSpec((1, 2, CH), lambda i, j: (i, 0, j))
    window = pl.BlockSpec((WIN, WIN), lambda i, j: (0, 0))
    perm = _window_permutation()
    in_specs = [seq, seq, window, window, chunk(4), chunk(1),
                pl.BlockSpec((4, CH // BDW, BDW, BDW), lambda i, j: (0, j, 0, 0)),
                chunk(2), chunk(2), chunk(2)]
    args = [xr, gg, jnp.asarray(perm, BF), jnp.asarray(perm.T, BF),
            wts["conv_w"], wts["conv_b"], wts["bd"], wts["lru_ba"], wts["lru_bx"], wts["lru_lam"]]
    if has_h0:
        in_specs.append(state)
        args.append(h0)
    out_specs = [seq]
    out_shape = [jax.ShapeDtypeStruct((b, t, DR), BF)]
    if emit_state:
        out_specs.append(state)
        out_shape.append(jax.ShapeDtypeStruct((b, 2, DR), F32))
    res = pl.pallas_call(
        functools.partial(_rglru_kernel, t=t, has_h0=has_h0, emit_state=emit_state),
        grid=(b, nc),
        in_specs=in_specs,
        out_specs=out_specs,
        out_shape=out_shape,
        scratch_shapes=[pltpu.VMEM((2, t, CH), F32)] * 3,
        compiler_params=_cparams(("arbitrary", "arbitrary")),
        name="rglru",
    )(*args)
    return res if emit_state else (res[0], None)


def _attn_kernel(*refs, t, n_seqs, n_heads, has_ctx, q_block):
    it = iter(refs)
    q_ref, k_ref, v_ref = next(it), next(it), next(it)
    kc_ref = vc_ref = None
    if has_ctx:
        kc_ref, vc_ref = next(it), next(it)
    o_ref = next(it)
    for sq, hd in [(sq, hd) for sq in range(n_seqs) for hd in range(n_heads)]:
        cols = slice(hd * HP, (hd + 1) * HP)
        k = k_ref[sq, :, cols]
        v = v_ref[sq, :, cols]
        if has_ctx:
            kc = kc_ref[sq, :, cols]
            vc = vc_ref[sq, :, cols]
        for qb in range(t // q_block):
            rows = slice(qb * q_block, (qb + 1) * q_block)
            q = q_ref[sq, rows, cols]
            s = _dot_nt(q, k)
            m = jnp.max(s, axis=-1, keepdims=True)
            if has_ctx:
                sc = _dot_nt(q, kc)
                m = jnp.maximum(m, jnp.max(sc, axis=-1, keepdims=True))
            p = jnp.exp2(s - m)
            den = jnp.sum(p, axis=-1, keepdims=True)
            o = _dot(p.astype(BF), v)
            if has_ctx:
                pc = jnp.exp2(sc - m)
                den = den + jnp.sum(pc, axis=-1, keepdims=True)
                o = o + _dot(pc.astype(BF), vc)
            o_ref[sq, rows, cols] = (o / den).astype(BF)


def _attention(q, k, v, kc, vc, seqs_per_step, heads_per_step):
    b, t, _ = q.shape
    has_ctx = kc is not None
    w = heads_per_step * HP
    blk = lambda n: pl.BlockSpec((seqs_per_step, n, w), lambda i, j: (i, 0, j))
    in_specs = [blk(t), blk(t), blk(t)]
    args = [q, k, v]
    if has_ctx:
        in_specs += [blk(kc.shape[1])] * 2
        args += [kc, vc]
    return pl.pallas_call(
        functools.partial(_attn_kernel, t=t, n_seqs=seqs_per_step, n_heads=heads_per_step,
                          has_ctx=has_ctx, q_block=min(t, 256)),
        grid=(b // seqs_per_step, NH // heads_per_step),
        in_specs=in_specs,
        out_specs=blk(t),
        out_shape=jax.ShapeDtypeStruct((b, t, DH), BF),
        compiler_params=_cparams(("arbitrary", "arbitrary")),
        name="attention",
    )(*args)


def _route(logits):
    lane = lax.broadcasted_iota(jnp.int32, logits.shape, 1)
    lanef = lane.astype(F32)
    neg = -jnp.inf
    big = float(LANES)
    gl = jnp.where((lane >= NE) & (lane < NE + NG), logits, neg)
    gmax = jnp.max(gl, axis=-1, keepdims=True)
    gidx = jnp.min(jnp.where(gl == gmax, lanef, big), axis=-1, keepdims=True) - float(NE)
    gw = 1.0 / jnp.sum(jnp.exp(gl - gmax), axis=-1, keepdims=True)
    lo = gidx * float(EPG)
    el = jnp.where((lanef >= lo) & (lanef < lo + float(EPG)), logits, neg)
    v1 = jnp.max(el, axis=-1, keepdims=True)
    i1 = jnp.min(jnp.where(el == v1, lanef, big), axis=-1, keepdims=True)
    el2 = jnp.where(lanef == i1, neg, el)
    v2 = jnp.max(el2, axis=-1, keepdims=True)
    i2 = jnp.min(jnp.where(el2 == v2, lanef, big), axis=-1, keepdims=True)
    e2 = jnp.exp(v2 - v1)
    w1 = gw / (1.0 + e2)
    w2 = gw * e2 / (1.0 + e2)
    cmb = jnp.where(lanef == i1, w1, 0.0) + jnp.where(lanef == i2, w2, 0.0)
    return cmb, gidx


def _out_kernel(x_ref, h_ref, yr_ref, ya_ref, mod_ref, win_ref, wor_ref, wom_ref, wout_ref,
                n2_ref, rw_ref, rb_ref, x1_ref, hx_ref, rinfo_ref, tcnt_ref):
    x = x_ref[...]
    gl = _dot_nt(h_ref[...], win_ref[O_GATE:O_GATE + 2 * D, :])
    merged2 = ((jnp.tanh(0.5 * gl[:, :D]) + 1.0) * _dot(yr_ref[...], wor_ref[...])
               + (jnp.tanh(0.5 * gl[:, D:]) + 1.0) * _dot(ya_ref[...], wom_ref[...]))
    mix = _dot(merged2.astype(BF), wout_ref[...])
    x1 = x + mod_ref[0, 2:3, :] * mix
    x1_ref[...] = x1
    h2 = _modulated_norm(x1, n2_ref[...], mod_ref[0, 4:5, :], mod_ref[0, 3:4, :])
    h2_hi = h2.astype(BF)
    h2_lo = (h2 - h2_hi.astype(F32)).astype(BF)
    part = _dot(h2_hi, rw_ref[...])
    logits = part[:, :LANES] + part[:, LANES:] + _dot(h2_lo, rw_ref[:, :LANES]) + rb_ref[...]
    cmb, gidx = _route(logits)

    lanef = lax.broadcasted_iota(jnp.int32, cmb.shape, 1).astype(F32)
    ghot = jnp.where(lanef == gidx, 1.0, 0.0)
    r_i = lax.broadcasted_iota(jnp.int32, (TM, TM), 0)
    c_i = lax.broadcasted_iota(jnp.int32, (TM, TM), 1)
    tri = jnp.where(r_i > c_i, 1.0, 0.0).astype(BF)
    earlier_same = jnp.sum(_dot(tri, ghot.astype(BF)) * ghot, axis=-1, keepdims=True)
    counts = jnp.sum(ghot, axis=0, keepdims=True)
    padded = jnp.floor((counts + (RUN_ALIGN - 1.0)) * (1.0 / RUN_ALIGN)) * RUN_ALIGN
    lower_groups = jnp.sum(jnp.where(lanef < gidx, padded, 0.0), axis=-1, keepdims=True)
    lpos = lower_groups + earlier_same
    s_i = lax.broadcasted_iota(jnp.int32, (TM, TS), 1)
    to_sorted = jnp.where(s_i.astype(F32) == lpos, 1.0, 0.0).astype(BF)
    c1 = cmb.astype(BF)
    c2 = (cmb - c1.astype(F32)).astype(BF)
    c3 = (cmb - c1.astype(F32) - c2.astype(F32)).astype(BF)
    payload = jnp.concatenate([h2.astype(BF), c1, c2, c3], axis=1)
    srt = lax.dot_general(to_sorted, payload, (((0,), (0,)), ((), ())), preferred_element_type=F32)
    hx_ref[...] = srt.astype(BF)

    rinfo_ref[...] = jnp.where(lanef == 0.0, gidx, jnp.where(lanef == 1.0, lpos, 0.0))
    tcnt_ref[0] = jnp.broadcast_to(counts, (SUBLANES, LANES))


def _merge_out(x, h, yr, ya, mod, mod_row, wts):
    n = x.shape[0]
    n_tiles = n // TM
    tile = lambda w: pl.BlockSpec((TM, w), lambda i: (i, 0))
    return pl.pallas_call(
        _out_kernel,
        grid=(n_tiles,),
        in_specs=[tile(D), tile(D), tile(DR), tile(DH),
                  pl.BlockSpec((1, 6, D), lambda i: (mod_row(i), 0, 0)),
                  _resident(wts["w_in_t"].shape), _resident((DR, D)), _resident((DH, D)), _resident((D, D)),
                  _resident((1, D)), _resident((D, 2 * LANES)), _resident((1, LANES))],
        out_specs=[tile(D), pl.BlockSpec((TS, XW), lambda i: (i, 0)), tile(LANES),
                   pl.BlockSpec((1, SUBLANES, LANES), lambda i: (i, 0, 0))],
        out_shape=[jax.ShapeDtypeStruct((n, D), F32), jax.ShapeDtypeStruct((n_tiles * TS, XW), BF),
                   jax.ShapeDtypeStruct((n, LANES), F32),
                   jax.ShapeDtypeStruct((n_tiles, SUBLANES, LANES), F32)],
        compiler_params=_cparams(("arbitrary",)),
        name="merge_out",
    )(x, h, yr, ya, mod, wts["w_in_t"], wts["w_o_rnn"], wts["w_o_mla"], wts["w_out"],
      wts["n2"], wts["router_w"], wts["router_b"])


def _run_copies(local_ref, far_ref, len_ref, first, count, make_copy, action):
    for g in range(count):
        idx = first + g
        n = len_ref[idx]
        local0 = local_ref[idx] if local_ref is not None else 0
        far0 = far_ref[idx]
        for k in reversed(range(RUN_ALIGN.bit_length() - 1, TS.bit_length())):
            size = 1 << k
            done = (n >> (k + 1)) << (k + 1)

            @pl.when(((n >> k) & 1) == 1)
            def _():
                src = local0 if local_ref is None else pl.multiple_of(local0 + done, RUN_ALIGN)
                action(make_copy(src, pl.multiple_of(far0 + done, RUN_ALIGN), size))


ZERO_RUNS = 12


def _dispatch_kernel(local_ref, far_ref, len_ref, zfar_ref, zlen_ref, hxc_ref, hxl_ref, xs_ref,
                     zero_ref, sem, *, n_ctx_tiles):
    tile = pl.program_id(0)

    def copy_from(src_ref):
        def copy(src, dst, size):
            return pltpu.make_async_copy(src_ref.at[pl.ds(src, size)], xs_ref.at[pl.ds(dst, size)], sem)
        return copy

    def move(copy, *tables):
        _run_copies(*tables, copy, lambda c: c.start())
        _run_copies(*tables, copy, lambda c: c.wait())

    @pl.when(tile == 0)
    def _():
        zero_ref[...] = jnp.zeros_like(zero_ref)
        move(copy_from(zero_ref), None, zfar_ref, zlen_ref, 0, ZERO_RUNS)

    @pl.when(tile < n_ctx_tiles)
    def _():
        move(copy_from(hxc_ref), local_ref, far_ref, len_ref, tile * NG, NG)

    @pl.when(tile >= n_ctx_tiles)
    def _():
        move(copy_from(hxl_ref), local_ref, far_ref, len_ref, tile * NG, NG)


def _dispatch(tables, zero_tables, hx_c, hx_l, n_rows):
    n_c, n_l = hx_c.shape[0] // TS, hx_l.shape[0] // TS
    return pl.pallas_call(
        functools.partial(_dispatch_kernel, n_ctx_tiles=n_c),
        grid_spec=pltpu.PrefetchScalarGridSpec(
            num_scalar_prefetch=5,
            grid=(n_c + n_l,),
            in_specs=[pl.BlockSpec((TS, XW), lambda i, *_: (jnp.minimum(i, n_c - 1), 0)),
                      pl.BlockSpec((TS, XW), lambda i, *_: (jnp.maximum(i - n_c, 0), 0))],
            out_specs=pl.BlockSpec(memory_space=pl.ANY),
            scratch_shapes=[pltpu.VMEM((TMOE, XW), BF), pltpu.SemaphoreType.DMA(())],
        ),
        out_shape=jax.ShapeDtypeStruct((n_rows, XW), BF),
        compiler_params=_cparams(("arbitrary",)),
        name="dispatch",
    )(*tables, *zero_tables, hx_c, hx_l)


def _opens_group(j, tg):
    return (j == 0) | (tg[j] != tg[jnp.maximum(j - 1, 0)])


def _moe_kernel(tb_ref, tg_ref, nt_ref, xs_ref, w1_ref, w3_ref, w2_ref, o_ref,
                acc_ref, w1c_ref, w3c_ref, w2c_ref):
    j = pl.program_id(0)
    pair = pl.program_id(1)

    @pl.when((j < nt_ref[0]) & _opens_group(j, tg_ref))
    def _():
        w1c_ref[pair] = w1_ref[...].astype(BF)
        w3c_ref[pair] = w3_ref[...].astype(BF)
        w2c_ref[pair] = w2_ref[...].reshape(EXPERTS_PER_STEP * DE, D).astype(BF)

    @pl.when(j < nt_ref[0])
    def _():
        xt = xs_ref[:, :D]
        cmb = (xs_ref[:, D:D + LANES].astype(F32) + xs_ref[:, D + LANES:D + 2 * LANES].astype(F32)
               + xs_ref[:, D + 2 * LANES:].astype(F32))
        lane = lax.broadcasted_iota(jnp.int32, cmb.shape, 1)
        first = tg_ref[j] * EPG + pair * EXPERTS_PER_STEP
        hidden = []
        for u in range(EXPERTS_PER_STEP):
            a = _dot(xt, w1c_ref[pair, u])
            he = (a * _sigmoid(a)) * _dot(xt, w3c_ref[pair, u])
            ce = jnp.sum(jnp.where(lane == first + u, cmb, 0.0), axis=-1, keepdims=True)
            hidden.append((he * ce).astype(BF))
        y = _dot(jnp.concatenate(hidden, axis=1), w2c_ref[pair])

        @pl.when(pair == 0)
        def _():
            acc_ref[...] = y

        @pl.when(pair == EPG // EXPERTS_PER_STEP - 1)
        def _():
            o_ref[...] = (acc_ref[...] + y).astype(BF)

    @pl.when((j >= nt_ref[0]) & (pair == EPG // EXPERTS_PER_STEP - 1))
    def _():
        o_ref[...] = jnp.zeros_like(o_ref)


def _experts(tile_block, tile_group, n_tiles, xs, wts):
    m = xs.shape[0]
    steps = EPG // EXPERTS_PER_STEP
    assert steps == 2, "the kernel keeps one partial sum: first step stores it, second adds and writes"

    def w_idx(j, e, tb, tg, nt):
        needed = (j < nt[0]) & _opens_group(j, tg)
        return (tg[j] * steps + jnp.where(needed, e, steps - 1), 0, 0)

    return pl.pallas_call(
        _moe_kernel,
        grid_spec=pltpu.PrefetchScalarGridSpec(
            num_scalar_prefetch=3,
            grid=(m // TMOE, steps),
            in_specs=[pl.BlockSpec((TMOE, XW), lambda j, e, tb, tg, nt: (tb[j], 0)),
                      pl.BlockSpec((EXPERTS_PER_STEP, D, DE), w_idx),
                      pl.BlockSpec((EXPERTS_PER_STEP, D, DE), w_idx),
                      pl.BlockSpec((EXPERTS_PER_STEP, DE, D), w_idx)],
            out_specs=pl.BlockSpec((TMOE, D), lambda j, e, tb, tg, nt: (j, 0)),
            scratch_shapes=[pltpu.VMEM((TMOE, D), F32),
                            pltpu.VMEM((steps, EXPERTS_PER_STEP, D, DE), BF),
                            pltpu.VMEM((steps, EXPERTS_PER_STEP, D, DE), BF),
                            pltpu.VMEM((steps, EXPERTS_PER_STEP * DE, D), BF)],
        ),
        out_shape=jax.ShapeDtypeStruct((m, D), BF),
        compiler_params=_cparams(("arbitrary", "arbitrary")),
        name="experts",
    )(tile_block, tile_group, n_tiles, xs, wts["exp_w1"], wts["exp_w3"], wts["exp_w2"])


def _combine_kernel(local_ref, far_ref, len_ref, x1_ref, rinfo_ref, mod_ref, ys_ref, o_ref,
                    buf_ref, sem, *, n_steps):
    i = pl.program_id(0)

    def runs(step, slot, action):
        def copy(dst, src, size):
            return pltpu.make_async_copy(ys_ref.at[pl.ds(src, size)],
                                         buf_ref.at[slot, pl.ds(dst, size)], sem.at[slot])

        _run_copies(local_ref, far_ref, len_ref, step * NG, NG, copy, action)

    @pl.when(i == 0)
    def _():
        buf_ref[...] = jnp.zeros_like(buf_ref)
        runs(0, 0, lambda c: c.start())

    @pl.when(i + 1 < n_steps)
    def _():
        runs(i + 1, (i + 1) % 2, lambda c: c.start())

    slot = i % 2
    runs(i, slot, lambda c: c.wait())
    lpos = rinfo_ref[:, 1:2]
    s_i = lax.broadcasted_iota(jnp.int32, (TM, TS), 1)
    from_sorted = jnp.where(s_i.astype(F32) == lpos, 1.0, 0.0).astype(BF)
    moe = _dot(from_sorted, buf_ref[slot])
    o_ref[...] = x1_ref[...] + mod_ref[0, 5:6, :] * moe


def _combine(tables, x1, rinfo, mod, mod_row, ys):
    n = x1.shape[0]
    n_steps = n // TM
    return pl.pallas_call(
        functools.partial(_combine_kernel, n_steps=n_steps),
        grid_spec=pltpu.PrefetchScalarGridSpec(
            num_scalar_prefetch=3,
            grid=(n_steps,),
            in_specs=[pl.BlockSpec((TM, D), lambda i, *_: (i, 0)),
                      pl.BlockSpec((TM, LANES), lambda i, *_: (i, 0)),
                      pl.BlockSpec((1, 6, D), lambda i, *_: (mod_row(i), 0, 0)),
                      pl.BlockSpec(memory_space=pl.ANY)],
            out_specs=pl.BlockSpec((TM, D), lambda i, *_: (i, 0)),
            scratch_shapes=[pltpu.VMEM((2, TS, D), BF), pltpu.SemaphoreType.DMA((2,))],
        ),
        out_shape=jax.ShapeDtypeStruct((n, D), F32),
        compiler_params=_cparams(("arbitrary",)),
        name="combine",
    )(*tables, x1, rinfo, mod, ys)


def _run_lengths(tile_counts):
    counts = tile_counts[:, 0, :NG].astype(jnp.int32)
    return ((counts + RUN_ALIGN - 1) // RUN_ALIGN) * RUN_ALIGN


def _run_tables(lengths, first_far):
    local = jnp.cumsum(lengths, axis=1) - lengths
    far = first_far[None, :] + jnp.cumsum(lengths, axis=0) - lengths
    flat = lambda a: a.astype(jnp.int32).reshape(-1)
    return flat(local), flat(far), flat(lengths)


def _group_layout(counts, max_tiles):
    padded = ((counts + TMOE - 1) // TMOE) * TMOE
    ends = jnp.cumsum(padded)
    offsets = ends - padded
    n_tiles = (ends[-1] // TMOE).astype(jnp.int32)
    tile = jnp.minimum(jnp.arange(max_tiles, dtype=jnp.int32), jnp.maximum(n_tiles - 1, 0))
    tile_group = jnp.sum((tile[:, None] * TMOE >= ends[None, :]).astype(jnp.int32), axis=1)
    spare = ends[-1] + TMOE * jnp.arange(ZERO_RUNS - NG, dtype=jnp.int32)
    spare_len = jnp.where(spare < max_tiles * TMOE, TMOE, 0)
    zero_start = jnp.concatenate([offsets + counts, jnp.where(spare_len > 0, spare, 0)]).astype(jnp.int32)
    zero_len = jnp.concatenate([padded - counts, spare_len]).astype(jnp.int32)
    return offsets, tile, tile_group, n_tiles.reshape(1), (zero_start, zero_len)


def _pad_heads(w, perm=None, rotary_only=False):
    lead = w.shape[:-1]
    per = w.shape[-1] // NH
    w = w.reshape(lead + (NH, per))
    if perm is not None:
        nope = jnp.zeros_like(w[..., :NOPE]) if rotary_only else w[..., :NOPE]
        w = jnp.concatenate([nope, w[..., NOPE:][..., perm]], axis=-1)
    w = jnp.pad(w, [(0, 0)] * len(lead) + [(0, 0), (0, HP - per)])
    return w.reshape(lead + (NH * HP,))


def _pad_gain(g, perm, rotary_only=False):
    nope = jnp.zeros((NOPE,), F32) if rotary_only else g[:NOPE]
    g = jnp.concatenate([nope, g[NOPE:][perm], jnp.zeros((HP - QK,), F32)])
    return g.reshape(1, HP)


def _block_diag(w):
    per = BDW // LRU_BLOCK
    rows = w.reshape(DR // BDW, BDW, LRU_BLOCK)
    idx = np.arange(BDW) // LRU_BLOCK
    mask = jnp.asarray(idx[:, None] == idx[None, :], w.dtype)
    return jnp.concatenate([rows] * per, axis=-1) * mask


def _prepare_shared(l, p):
    w_in = lambda a, b: jnp.transpose(lax.slice(p["w_in"], (l, 0, a), (l + 1, D, b)).reshape(D, b - a))
    bd = jnp.stack([_block_diag(p["lru_wa"][l, 0]), _block_diag(p["lru_wx"][l, 0]),
                    _block_diag(p["lru_wa"][l, 1]), _block_diag(p["lru_wx"][l, 1])])
    bd = (0.5 * bd).astype(BF)
    wom = p["w_o_mla"][l].reshape(NH, VD, D)
    wom = jnp.pad(wom, ((0, 0), (0, HP - VD), (0, 0))).reshape(DH, D)
    router_w = jnp.concatenate([p["router_we"][l], p["router_wg"][l],
                                jnp.zeros((D, LANES - NE - NG), F32)], axis=1)
    router_b = jnp.concatenate([p["router_be"][l], p["router_bg"][l],
                                jnp.zeros((LANES - NE - NG,), F32)]).reshape(1, LANES)
    router_hi = router_w.astype(BF)
    router_lo = (router_w - router_hi.astype(F32)).astype(BF)
    router_w = jnp.concatenate([router_hi, router_lo], axis=1)
    return {
        "n1": p["norm1_g"][l].reshape(1, D), "n2": p["norm2_g"][l].reshape(1, D),
        "w_in_t": jnp.transpose(p["w_in"][l]).astype(BF),
        "w_kv": w_in(O_KV, O_KR), "w_kr": w_in(O_KR, O_GATE),
        "qan": p["q_a_norm"][l].reshape(1, QL), "kvan": p["kv_a_norm"][l].reshape(1, KVL),
        "w_uk": _pad_heads(p["w_uk"][l]).astype(BF),
        "w_uv": _pad_heads(p["w_uv"][l]).astype(BF),
        "conv_w": p["conv_w"][l], "conv_b": p["conv_b"][l].reshape(1, DR), "bd": bd,
        "lru_ba": 0.5 * p["lru_ba"][l], "lru_bx": 0.5 * p["lru_bx"][l], "lru_lam": p["lru_lam"][l],
        "w_o_rnn": p["w_o_rnn"][l].astype(BF), "w_o_mla": wom.astype(BF), "w_out": (0.5 * p["w_out"][l]).astype(BF),
        "router_w": router_w, "router_b": router_b,
        "exp_w1": p["exp_w1"][l], "exp_w3": p["exp_w3"][l], "exp_w2": p["exp_w2"][l],
    }


def _with_rope_order(l, p, shared, perm, rotary):
    w = dict(shared)
    zeros = lambda n: jnp.zeros((n, D), F32)
    rope_block = lambda order: [zeros(NOPE), shared["w_kr"][order, :], zeros(HP - QK)]
    kvr = [shared["w_kv"]] + rope_block(perm)
    w["w_uq"] = _pad_heads(p["w_uq"][l], perm).astype(BF)
    w["gq"] = _pad_gain(p["q_norm"][l], perm) * SCORE_SCALE
    w["gk"] = _pad_gain(p["k_norm"][l], perm)
    if rotary:
        pair = np.concatenate([perm[ROPE // 2:], perm[:ROPE // 2]])
        kvr += rope_block(pair)
        w["w_uq_pair"] = _pad_heads(p["w_uq"][l], pair, rotary_only=True).astype(BF)
        w["gq_pair"] = _pad_gain(p["q_norm"][l], pair, rotary_only=True) * SCORE_SCALE
        w["gk_pair"] = _pad_gain(p["k_norm"][l], pair, rotary_only=True)
    w["w_kvr"] = jnp.concatenate(kvr, axis=0).astype(BF)
    return w


def _rope_tables(n_tokens):
    rows = n_tokens // GRID_W
    row = np.repeat(np.arange(rows), GRID_W).astype(np.float32)
    col = np.tile(np.arange(GRID_W), rows).astype(np.float32)
    axis_dim = ROPE // 2
    inv = (np.float32(ROPE_BASE) ** (-np.arange(0, axis_dim, 2, dtype=np.float32) / axis_dim)).astype(np.float32)
    ang = np.concatenate([row[:, None] * inv, col[:, None] * inv], axis=-1).astype(np.float32)
    cos, sin = np.cos(ang), np.sin(ang)
    ones = lambda n: np.ones((n_tokens, n), np.float32)
    zeros = lambda n: np.zeros((n_tokens, n), np.float32)
    cos_t = np.concatenate([ones(NOPE), cos, cos, ones(HP - QK)], axis=1)
    sin_t = np.concatenate([zeros(NOPE), -sin, sin, zeros(HP - QK)], axis=1)
    return jnp.asarray(cos_t, F32), jnp.asarray(sin_t, F32)


def kernel(x_prompt, x_sample, cache_mla_ckv, cache_mla_krope, state_rglru, c, c_ctx, norm1_g, norm2_g, w_mod, b_mod, w_in, conv_w, conv_b, lru_wa, lru_ba, lru_wx, lru_bx, lru_lam, q_a_norm, kv_a_norm, w_uq, w_uk, w_uv, q_norm, k_norm, w_o_rnn, w_o_mla, w_out, router_wg, router_bg, router_we, router_be, exp_w1, exp_w3, exp_w2):
    p = dict(norm1_g=norm1_g, norm2_g=norm2_g, w_in=w_in, conv_w=conv_w, conv_b=conv_b,
             lru_wa=lru_wa, lru_ba=lru_ba, lru_wx=lru_wx, lru_bx=lru_bx, lru_lam=lru_lam,
             q_a_norm=q_a_norm, kv_a_norm=kv_a_norm, w_uq=w_uq, w_uk=w_uk, w_uv=w_uv,
             q_norm=q_norm, k_norm=k_norm, w_o_rnn=w_o_rnn, w_o_mla=w_o_mla, w_out=w_out,
             router_wg=router_wg, router_bg=router_bg, router_we=router_we, router_be=router_be,
             exp_w1=exp_w1, exp_w3=exp_w3, exp_w2=exp_w2)
    depth = w_in.shape[0]
    nb, seq, _ = x_prompt.shape
    db, dseq, _ = x_sample.shape
    ident = np.arange(ROPE)
    halves = np.concatenate([np.arange(0, ROPE, 2), np.arange(1, ROPE, 2)])
    rope_tabs = _rope_tables(dseq)
    cond8 = jnp.concatenate([c_ctx[None, :], c, jnp.zeros((SUBLANES - 1 - db, D), F32)], axis=0)
    ctx_row = lambda tile_rows: (lambda i: 0)
    lat_row = lambda tile_rows: (lambda i: (i * tile_rows) // dseq + 1)
    n_ctx, n_lat = nb * seq, db * dseq
    run_padding = ((n_ctx + n_lat) // TM) * NG * (RUN_ALIGN - 1)
    max_tiles = -(-(n_ctx + n_lat + run_padding) // TMOE) + NG
    assert max_tiles - (n_ctx + n_lat) // TMOE <= ZERO_RUNS - NG, "more unused expert tiles than zero runs"
    per_seq = lambda arrs, b, t: [a.reshape(b, t, a.shape[-1]) for a in arrs]
    flat = lambda a: a.reshape(-1, a.shape[-1])

    y_prompt, y_sample = x_prompt.reshape(n_ctx, D), x_sample.reshape(n_lat, D)
    ckv_list, krope_list, rnn_list = [], [], []
    for l in range(depth):
        shared = _prepare_shared(l, p)
        w_ctx = _with_rope_order(l, p, shared, ident, False)
        w_lat = _with_rope_order(l, p, shared, halves, True)
        mod = _modulation(cond8, w_mod[l], b_mod[l]).reshape(SUBLANES, 6, D)

        h, xr, gg, q, k, v, ckv, kro = _projections(y_prompt, mod, ctx_row(TM), w_ctx, None, True)
        xr, gg, q, k, v = per_seq([xr, gg, q, k, v], nb, seq)
        yr, h_fin = _rglru(xr, gg, shared, None, True)
        ya = _attention(q, k, v, None, None, 4, NH)
        x1_c, hx_c, ri_c, tc_c = _merge_out(y_prompt, h, flat(yr), flat(ya), mod, ctx_row(TM), shared)
        ckv_list.append(ckv.reshape(nb, seq, KVL))
        krope_list.append(kro.reshape(nb, seq, ROPE))
        rnn_list.append(h_fin)

        krp_cache = jnp.pad(cache_mla_krope[:, l][..., halves], ((0, 0), (0, 0), (NOPE, HP - QK)))
        kc, vc = _cache_keys_values(cache_mla_ckv[:, l], krp_cache, w_lat)
        h, xr, gg, q, k, v = _projections(y_sample, mod, lat_row(TM), w_lat, rope_tabs, False)
        xr, gg, q, k, v = per_seq([xr, gg, q, k, v], db, dseq)
        yr, _ = _rglru(xr, gg, shared, state_rglru[:, l], False)
        ya = _attention(q, k, v, kc, vc, 1, 4)
        x1_l, hx_l, ri_l, tc_l = _merge_out(y_sample, h, flat(yr), flat(ya), mod, lat_row(TM), shared)

        len_c, len_l = _run_lengths(tc_c), _run_lengths(tc_l)
        rows_c = jnp.sum(len_c, axis=0)
        offsets, tile_block, tile_group, n_tiles, zero_runs = _group_layout(
            rows_c + jnp.sum(len_l, axis=0), max_tiles)
        runs_c = _run_tables(len_c, offsets)
        runs_l = _run_tables(len_l, offsets + rows_c)
        runs = tuple(jnp.concatenate([a, b]) for a, b in zip(runs_c, runs_l))
        xs = _dispatch(runs, zero_runs, hx_c, hx_l, max_tiles * TMOE)
        ys = _experts(tile_block, tile_group, n_tiles, xs, shared)
        y_prompt = _combine(runs_c, x1_c, ri_c, mod, ctx_row(TM), ys)
        y_sample = _combine(runs_l, x1_l, ri_l, mod, lat_row(TM), ys)

    y_prompt, y_sample = y_prompt.reshape(nb, seq, D), y_sample.reshape(db, dseq, D)

    return (y_prompt, y_sample, jnp.stack(ckv_list, axis=1), jnp.stack(krope_list, axis=1),
            jnp.stack(rnn_list, axis=1))
```

```python
import functools
import math

import numpy as np
import jax
import jax.numpy as jnp
from jax import lax
from jax.experimental import pallas as pl
from jax.experimental.pallas import tpu as pltpu

D = 1024
DR = 1024
QL = 384
KVL = 256
NH = 8
NOPE = 64
ROPE = 32
QK = NOPE + ROPE
SCORE_SCALE = (QK ** -0.5) * math.log2(math.e)
VD = 64
O_G, O_Q, O_KV, O_KR, O_GATE = DR, 2 * DR, 2 * DR + QL, 2 * DR + QL + KVL, 2 * DR + QL + KVL + ROPE
HP = 128
DH = NH * HP
GRID_W = 64
ROPE_BASE = 10000.0
EPS = 1e-6
TINY = 1e-30
LRU_C = 8.0
LRU_BLOCK = 64
BDW = 256
CH = 1024
NG = 4
EPG = 4
NE = NG * EPG
DE = 512
LANES = 128
SUBLANES = 8
TM = 512
TS = 640
TMOE = 512
EXPERTS_PER_STEP = 2
XW = D + LANES
RUN_ALIGN = 16
VMEM_LIMIT = 52 * 1024 * 1024
BF = jnp.bfloat16
F32 = jnp.float32


def _cparams(sem):
    return pltpu.CompilerParams(dimension_semantics=sem, vmem_limit_bytes=VMEM_LIMIT)


def _dot(a, b):
    return jnp.dot(a, b, preferred_element_type=F32)


def _dot_nt(a, b):
    return lax.dot_general(a, b, (((1,), (1,)), ((), ())), preferred_element_type=F32)


def _rms(x, g, width):
    ms = jnp.sum(x * x, axis=-1, keepdims=True) * (1.0 / width)
    return x * lax.rsqrt(ms + EPS) * g


def _modulated_norm(x, g, scale, shift):
    return _rms(x, g * (1.0 + scale), D) + shift


def _mod_kernel(c_ref, w_ref, b_ref, o_ref):
    c = c_ref[...]
    s = c * jax.nn.sigmoid(c)
    o_ref[...] = _dot(s, w_ref[...]) + b_ref[...]


def _modulation(cond8, w_mod, b_mod):
    n = w_mod.shape[1]
    return pl.pallas_call(
        _mod_kernel,
        grid=(n // D,),
        in_specs=[
            pl.BlockSpec((SUBLANES, D), lambda j: (0, 0)),
            pl.BlockSpec((D, D), lambda j: (0, j)),
            pl.BlockSpec((1, D), lambda j: (0, j)),
        ],
        out_specs=pl.BlockSpec((SUBLANES, D), lambda j: (0, j)),
        out_shape=jax.ShapeDtypeStruct((SUBLANES, n), F32),
        compiler_params=_cparams(("arbitrary",)),
        name="modulation",
    )(cond8, w_mod, b_mod.reshape(1, n))


def _head_norm(xh, gain, cos=None, partner_scaled=None):
    ms = jnp.sum(xh * xh, axis=-1, keepdims=True) * (1.0 / QK)
    rs = lax.rsqrt(ms + EPS)
    y = xh * rs * gain
    if cos is None:
        return y
    return y * cos + partner_scaled * rs


def _keys_values(ckv, krp, wuk_ref, wuv_ref, gk, cos, partner_scaled, k_ref, v_ref):
    cb = ckv.astype(BF)
    kn = _dot(cb, wuk_ref[...])
    v_ref[...] = _dot(cb, wuv_ref[...]).astype(BF)
    for h in range(NH):
        kh = kn[:, h * HP:(h + 1) * HP] + krp
        k_ref[:, h * HP:(h + 1) * HP] = _head_norm(kh, gk, cos, partner_scaled).astype(BF)


def _proj_kernel(*refs, rope, emit_cache):
    it = iter(refs)
    x_ref, mod_ref, n1_ref = next(it), next(it), next(it)
    win_ref, wkvr_ref = next(it), next(it)
    wx_ref, wg_ref, wq_ref = win_ref.at[0:O_G], win_ref.at[O_G:O_Q], win_ref.at[O_Q:O_KV]
    qan_ref, kvan_ref, wuq_ref, gq_ref = next(it), next(it), next(it), next(it)
    wuk_ref, gk_ref, wuv_ref = next(it), next(it), next(it)
    if rope:
        wuqs_ref, gqs_ref, gks_ref, cos_ref, sins_ref = (next(it) for _ in range(5))
    h_ref, xr_ref, gg_ref, q_ref, k_ref, v_ref = (next(it) for _ in range(6))
    if emit_cache:
        ckv_ref, kro_ref = next(it), next(it)

    hb = _modulated_norm(x_ref[...], n1_ref[...], mod_ref[0, 1:2, :], mod_ref[0, 0:1, :]).astype(BF)
    h_ref[...] = hb
    xr_ref[...] = _dot_nt(hb, wx_ref[...]).astype(BF)
    gg_ref[...] = jax.nn.gelu(_dot_nt(hb, wg_ref[...])).astype(BF)

    qnb = _rms(_dot_nt(hb, wq_ref[...]), qan_ref[...], QL).astype(BF)
    q = _dot(qnb, wuq_ref[...])
    gq = gq_ref[...]
    cos = q_partner = q_pair_scale = None
    if rope:
        cos, sins = cos_ref[...], sins_ref[...]
        q_partner = _dot(qnb, wuqs_ref[...])
        q_pair_scale = gqs_ref[...] * sins
    for hd in range(NH):
        cols = slice(hd * HP, (hd + 1) * HP)
        partner = q_partner[:, cols] * q_pair_scale if rope else None
        q_ref[:, cols] = _head_norm(q[:, cols], gq, cos, partner).astype(BF)

    kvr = _dot_nt(hb, wkvr_ref[...])
    ckv = _rms(kvr[:, :KVL], kvan_ref[...], KVL)
    krp = kvr[:, KVL:KVL + HP]
    k_partner = kvr[:, KVL + HP:KVL + 2 * HP] * (gks_ref[...] * sins) if rope else None
    if emit_cache:
        ckv_ref[...] = ckv
        kro_ref[...] = krp[:, NOPE:NOPE + ROPE]
    _keys_values(ckv, krp, wuk_ref, wuv_ref, gk_ref[...], cos, k_partner, k_ref, v_ref)


def _resident(shape):
    return pl.BlockSpec(shape, lambda i: (0,) * len(shape), pipeline_mode=pl.Buffered(1))


def _projections(x, mod, mod_row, wts, rope_tabs, emit_cache):
    n = x.shape[0]
    rope = rope_tabs is not None
    tile = lambda w: pl.BlockSpec((TM, w), lambda i: (i, 0))
    in_specs = [
        tile(D),
        pl.BlockSpec((1, 6, D), lambda i: (mod_row(i), 0, 0)),
        _resident((1, D)),
        _resident(wts["w_in_t"].shape), _resident(wts["w_kvr"].shape),
        _resident((1, QL)), _resident((1, KVL)), _resident((QL, DH)), _resident((1, HP)),
        _resident((KVL, DH)), _resident((1, HP)), _resident((KVL, DH)),
    ]
    args = [x, mod, wts["n1"], wts["w_in_t"], wts["w_kvr"],
            wts["qan"], wts["kvan"], wts["w_uq"], wts["gq"], wts["w_uk"], wts["gk"], wts["w_uv"]]
    if rope:
        tiles_per_seq = rope_tabs[0].shape[0] // TM
        in_specs += [_resident((QL, DH)), _resident((1, HP)), _resident((1, HP))]
        in_specs += [pl.BlockSpec((TM, HP), lambda i: (i % tiles_per_seq, 0))] * 2
        args += [wts["w_uq_pair"], wts["gq_pair"], wts["gk_pair"]] + list(rope_tabs)
    out_specs = [tile(D), tile(DR), tile(DR), tile(DH), tile(DH), tile(DH)]
    out_shape = [jax.ShapeDtypeStruct((n, D), BF), jax.ShapeDtypeStruct((n, DR), BF),
                 jax.ShapeDtypeStruct((n, DR), BF), jax.ShapeDtypeStruct((n, DH), BF),
                 jax.ShapeDtypeStruct((n, DH), BF), jax.ShapeDtypeStruct((n, DH), BF)]
    if emit_cache:
        out_specs += [tile(KVL), tile(ROPE)]
        out_shape += [jax.ShapeDtypeStruct((n, KVL), F32), jax.ShapeDtypeStruct((n, ROPE), F32)]
    return pl.pallas_call(
        functools.partial(_proj_kernel, rope=rope, emit_cache=emit_cache),
        grid=(n // TM,),
        in_specs=in_specs,
        out_specs=out_specs,
        out_shape=out_shape,
        compiler_params=_cparams(("arbitrary",)),
        name="projections",
    )(*args)


def _cache_kv_kernel(ckv_ref, krp_ref, wuk_ref, gk_ref, wuv_ref, k_ref, v_ref):
    _keys_values(ckv_ref[0], krp_ref[0], wuk_ref, wuv_ref, gk_ref[...], None, None,
                 k_ref.at[0], v_ref.at[0])


def _cache_keys_values(ckv, krp, wts):
    b, s, _ = ckv.shape
    full = lambda shape: pl.BlockSpec(shape, lambda i: (0,) * len(shape))
    return pl.pallas_call(
        _cache_kv_kernel,
        grid=(b,),
        in_specs=[pl.BlockSpec((1, s, KVL), lambda i: (i, 0, 0)),
                  pl.BlockSpec((1, s, HP), lambda i: (i, 0, 0)),
                  full((KVL, DH)), full((1, HP)), full((KVL, DH))],
        out_specs=[pl.BlockSpec((1, s, DH), lambda i: (i, 0, 0))] * 2,
        out_shape=[jax.ShapeDtypeStruct((b, s, DH), BF)] * 2,
        compiler_params=_cparams(("arbitrary",)),
        name="cache_keys_values",
    )(ckv, krp, wts["w_uk"], wts["gk"], wts["w_uv"])


WIN = 256
SEG = WIN // 8


def _window_permutation():
    dst = np.arange(WIN)
    p = np.zeros((WIN, WIN), np.float32)
    p[dst, (dst % SUBLANES) * SEG + dst // SUBLANES] = 1.0
    return p


def _sigmoid(x):
    return 0.5 * jnp.tanh(0.5 * x) + 0.5


def _segment_pass(a_scr, b_scr, bases, inits, out_scr=None):
    def body(k, carry):
        new = []
        for d in range(2):
            h, p = carry[d]
            i = k if d == 0 else SEG - 1 - k
            rows = pl.ds(pl.multiple_of(bases[d] + i * SUBLANES, SUBLANES), SUBLANES)
            a = a_scr[d, rows, :]
            h = a * h + b_scr[d, rows, :]
            if out_scr is None:
                p = a * p
            else:
                out_scr[d, rows, :] = h
            new.append((h, p))
        return tuple(new)

    init = tuple((inits[d], jnp.ones_like(inits[d])) for d in range(2))
    return lax.fori_loop(0, SEG, body, init, unroll=4)


def _segment_entries(end, decay, carry_in, forward):
    order = range(SUBLANES) if forward else reversed(range(SUBLANES))
    rows = [None] * SUBLANES
    c = carry_in
    for s in order:
        rows[s] = c
        c = end[s:s + 1, :] + decay[s:s + 1, :] * c
    return jnp.concatenate(rows, axis=0), c


def _rglru_kernel(*refs, t, has_h0, emit_state):
    it = iter(refs)
    xr_ref, gg_ref, perm_ref, unperm_ref = (next(it) for _ in range(4))
    cw_ref, cb_ref, bd_ref, ba_ref, bx_ref, lam_ref = (next(it) for _ in range(6))
    h0_ref = next(it) if has_h0 else None
    y_ref = next(it)
    hf_ref = next(it) if emit_state else None
    a_scr, b_scr, h_scr = (next(it) for _ in range(3))

    n_win = t // WIN
    sub = lax.broadcasted_iota(jnp.int32, (SUBLANES, CH), 0)
    zero_row = jnp.zeros((1, CH), F32)
    edge = 2 * SUBLANES
    for w in range(n_win):
        lo, hi = w * WIN, (w + 1) * WIN
        xp = _dot(perm_ref[...], xr_ref[0, lo:hi, :])
        before = xr_ref[0, lo - edge:lo, :].astype(F32)[edge - 1:edge, :] if w > 0 else zero_row
        after = xr_ref[0, hi:hi + edge, :].astype(F32) if w < n_win - 1 else None
        after0 = after[0:1, :] if after is not None else zero_row
        after1 = after[1:2, :] if after is not None else zero_row
        tile_m1 = jnp.where(sub == 0, before, pltpu.roll(xp[WIN - SUBLANES:WIN, :], 1, 0))
        tile_p0 = jnp.where(sub == SUBLANES - 1, after0, pltpu.roll(xp[0:SUBLANES, :], SUBLANES - 1, 0))
        tile_p1 = jnp.where(sub == SUBLANES - 1, after1,
                            pltpu.roll(xp[SUBLANES:2 * SUBLANES, :], SUBLANES - 1, 0))
        xe = jnp.concatenate([tile_m1, xp, tile_p0, tile_p1], axis=0)
        xc = cb_ref[...]
        for tap in range(4):
            xc = xc + xe[tap * SUBLANES:tap * SUBLANES + WIN, :] * cw_ref[tap:tap + 1, :]
        for s in range(CH // BDW):
            cols = slice(s * BDW, (s + 1) * BDW)
            xs = xc[:, cols]
            xsb = xs.astype(BF)
            xh = 0.5 * xs
            for d in range(2):
                tr = jnp.tanh(_dot(xsb, bd_ref[2 * d, s]) + ba_ref[d:d + 1, cols])
                ti = jnp.tanh(_dot(xsb, bd_ref[2 * d + 1, s]) + bx_ref[d:d + 1, cols])
                nl = -lam_ref[d:d + 1, cols]
                softplus = jnp.maximum(nl, 0.0) + jnp.log(1.0 + jnp.exp(-jnp.abs(nl)))
                ch = (-0.5 * LRU_C) * softplus
                a = jnp.exp(tr * ch + ch)
                z = 1.0 - a * a
                root = z * lax.rsqrt(jnp.maximum(z, TINY))
                a_scr[d, lo:hi, cols] = a
                b_scr[d, lo:hi, cols] = root * (ti * xh + xh)

    if has_h0:
        carry = [h0_ref[0, 0:1, :], h0_ref[0, 1:2, :]]
    else:
        carry = [zero_row, zero_row]
    zeros = jnp.zeros((SUBLANES, CH), F32)
    for k in range(n_win):
        bases = (k * WIN, (n_win - 1 - k) * WIN)
        totals = _segment_pass(a_scr, b_scr, bases, (zeros, zeros))
        entries = []
        for d in range(2):
            entry, carry[d] = _segment_entries(totals[d][0], totals[d][1], carry[d], d == 0)
            entries.append(entry)
        _segment_pass(a_scr, b_scr, bases, entries, out_scr=h_scr)

    if emit_state:
        hf_ref[0, 0:1, :] = carry[0]
        hf_ref[0, 1:2, :] = carry[1]
    for w in range(n_win):
        lo, hi = w * WIN, (w + 1) * WIN
        gate = _dot(perm_ref[...], gg_ref[0, lo:hi, :])
        yp = ((h_scr[0, lo:hi, :] + h_scr[1, lo:hi, :]) * gate).astype(BF)
        y_ref[0, lo:hi, :] = _dot(unperm_ref[...], yp).astype(BF)


def _rglru(xr, gg, wts, h0, emit_state):
    b, t, _ = xr.shape
    nc = DR // CH
    has_h0 = h0 is not None
    chunk = lambda r: pl.BlockSpec((r, CH), lambda i, j: (0, j))
    seq = pl.BlockSpec((1, t, CH), lambda i, j: (i, 0, j))
    state = pl.BlockSpec((1, 2, CH), lambda i, j: (i, 0, j))
    window = pl.BlockSpec((WIN, WIN), lambda i, j: (0, 0))
    perm = _window_permutation()
    in_specs = [seq, seq, window, window, chunk(4), chunk(1),
                pl.BlockSpec((4, CH // BDW, BDW, BDW), lambda i, j: (0, j, 0, 0)),
                chunk(2), chunk(2), chunk(2)]
    args = [xr, gg, jnp.asarray(perm, BF), jnp.asarray(perm.T, BF),
            wts["conv_w"], wts["conv_b"], wts["bd"], wts["lru_ba"], wts["lru_bx"], wts["lru_lam"]]
    if has_h0:
        in_specs.append(state)
        args.append(h0)
    out_specs = [seq]
    out_shape = [jax.ShapeDtypeStruct((b, t, DR), BF)]
    if emit_state:
        out_specs.append(state)
        out_shape.append(jax.ShapeDtypeStruct((b, 2, DR), F32))
    res = pl.pallas_call(
        functools.partial(_rglru_kernel, t=t, has_h0=has_h0, emit_state=emit_state),
        grid=(b, nc),
        in_specs=in_specs,
        out_specs=out_specs,
        out_shape=out_shape,
        scratch_shapes=[pltpu.VMEM((2, t, CH), F32)] * 3,
        compiler_params=_cparams(("arbitrary", "arbitrary")),
        name="rglru",
    )(*args)
    return res if emit_state else (res[0], None)


def _attn_kernel(*refs, t, n_seqs, n_heads, has_ctx, q_block):
    it = iter(refs)
    q_ref, k_ref, v_ref = next(it), next(it), next(it)
    kc_ref = vc_ref = None
    if has_ctx:
        kc_ref, vc_ref = next(it), next(it)
    o_ref = next(it)
    for sq, hd in [(sq, hd) for sq in range(n_seqs) for hd in range(n_heads)]:
        cols = slice(hd * HP, (hd + 1) * HP)
        k = k_ref[sq, :, cols]
        v = v_ref[sq, :, cols]
        if has_ctx:
            kc = kc_ref[sq, :, cols]
            vc = vc_ref[sq, :, cols]
        for qb in range(t // q_block):
            rows = slice(qb * q_block, (qb + 1) * q_block)
            q = q_ref[sq, rows, cols]
            s = _dot_nt(q, k)
            m = jnp.max(s, axis=-1, keepdims=True)
            if has_ctx:
                sc = _dot_nt(q, kc)
                m = jnp.maximum(m, jnp.max(sc, axis=-1, keepdims=True))
            p = jnp.exp2(s - m)
            den = jnp.sum(p, axis=-1, keepdims=True)
            o = _dot(p.astype(BF), v)
            if has_ctx:
                pc = jnp.exp2(sc - m)
                den = den + jnp.sum(pc, axis=-1, keepdims=True)
                o = o + _dot(pc.astype(BF), vc)
            o_ref[sq, rows, cols] = (o / den).astype(BF)


def _attention(q, k, v, kc, vc, seqs_per_step, heads_per_step):
    b, t, _ = q.shape
    has_ctx = kc is not None
    w = heads_per_step * HP
    blk = lambda n: pl.BlockSpec((seqs_per_step, n, w), lambda i, j: (i, 0, j))
    in_specs = [blk(t), blk(t), blk(t)]
    args = [q, k, v]
    if has_ctx:
        in_specs += [blk(kc.shape[1])] * 2
        args += [kc, vc]
    return pl.pallas_call(
        functools.partial(_attn_kernel, t=t, n_seqs=seqs_per_step, n_heads=heads_per_step,
                          has_ctx=has_ctx, q_block=min(t, 256)),
        grid=(b // seqs_per_step, NH // heads_per_step),
        in_specs=in_specs,
        out_specs=blk(t),
        out_shape=jax.ShapeDtypeStruct((b, t, DH), BF),
        compiler_params=_cparams(("arbitrary", "arbitrary")),
        name="attention",
    )(*args)


def _route(logits):
    lane = lax.broadcasted_iota(jnp.int32, logits.shape, 1)
    lanef = lane.astype(F32)
    neg = -jnp.inf
    big = float(LANES)
    gl = jnp.where((lane >= NE) & (lane < NE + NG), logits, neg)
    gmax = jnp.max(gl, axis=-1, keepdims=True)
    gidx = jnp.min(jnp.where(gl == gmax, lanef, big), axis=-1, keepdims=True) - float(NE)
    gw = 1.0 / jnp.sum(jnp.exp(gl - gmax), axis=-1, keepdims=True)
    lo = gidx * float(EPG)
    el = jnp.where((lanef >= lo) & (lanef < lo + float(EPG)), logits, neg)
    v1 = jnp.max(el, axis=-1, keepdims=True)
    i1 = jnp.min(jnp.where(el == v1, lanef, big), axis=-1, keepdims=True)
    el2 = jnp.where(lanef == i1, neg, el)
    v2 = jnp.max(el2, axis=-1, keepdims=True)
    i2 = jnp.min(jnp.where(el2 == v2, lanef, big), axis=-1, keepdims=True)
    e2 = jnp.exp(v2 - v1)
    w1 = gw / (1.0 + e2)
    w2 = gw * e2 / (1.0 + e2)
    cmb = jnp.where(lanef == i1, w1, 0.0) + jnp.where(lanef == i2, w2, 0.0)
    return cmb, gidx


def _out_kernel(x_ref, h_ref, yr_ref, ya_ref, mod_ref, win_ref, wor_ref, wom_ref, wout_ref,
                n2_ref, rw_ref, rb_ref, x1_ref, hx_ref, rinfo_ref, tcnt_ref):
    x = x_ref[...]
    gl = _dot_nt(h_ref[...], win_ref[O_GATE:O_GATE + 2 * D, :])
    merged2 = ((jnp.tanh(0.5 * gl[:, :D]) + 1.0) * _dot(yr_ref[...], wor_ref[...])
               + (jnp.tanh(0.5 * gl[:, D:]) + 1.0) * _dot(ya_ref[...], wom_ref[...]))
    mix = _dot(merged2.astype(BF), wout_ref[...])
    x1 = x + mod_ref[0, 2:3, :] * mix
    x1_ref[...] = x1
    h2 = _modulated_norm(x1, n2_ref[...], mod_ref[0, 4:5, :], mod_ref[0, 3:4, :])
    h2_hi = h2.astype(BF)
    h2_lo = (h2 - h2_hi.astype(F32)).astype(BF)
    part = _dot(h2_hi, rw_ref[...])
    logits = part[:, :LANES] + part[:, LANES:] + _dot(h2_lo, rw_ref[:, :LANES]) + rb_ref[...]
    cmb, gidx = _route(logits)

    lanef = lax.broadcasted_iota(jnp.int32, cmb.shape, 1).astype(F32)
    ghot = jnp.where(lanef == gidx, 1.0, 0.0)
    r_i = lax.broadcasted_iota(jnp.int32, (TM, TM), 0)
    c_i = lax.broadcasted_iota(jnp.int32, (TM, TM), 1)
    tri = jnp.where(r_i > c_i, 1.0, 0.0).astype(BF)
    earlier_same = jnp.sum(_dot(tri, ghot.astype(BF)) * ghot, axis=-1, keepdims=True)
    counts = jnp.sum(ghot, axis=0, keepdims=True)
    padded = jnp.floor((counts + (RUN_ALIGN - 1.0)) * (1.0 / RUN_ALIGN)) * RUN_ALIGN
    lower_groups = jnp.sum(jnp.where(lanef < gidx, padded, 0.0), axis=-1, keepdims=True)
    lpos = lower_groups + earlier_same
    s_i = lax.broadcasted_iota(jnp.int32, (TM, TS), 1)
    to_sorted = jnp.where(s_i.astype(F32) == lpos, 1.0, 0.0).astype(BF)
    c1 = cmb.astype(BF).astype(F32)
    c2 = (cmb - c1).astype(BF).astype(F32)
    c3 = (cmb - c1 - c2).astype(BF).astype(F32)
    parts = c1 + pltpu.roll(c2, NE, 1) + pltpu.roll(c3, 2 * NE, 1)
    payload = jnp.concatenate([h2.astype(BF), parts.astype(BF)], axis=1)
    srt = lax.dot_general(to_sorted, payload, (((0,), (0,)), ((), ())), preferred_element_type=F32)
    hx_ref[...] = srt.astype(BF)

    rinfo_ref[...] = jnp.where(lanef == 0.0, gidx, jnp.where(lanef == 1.0, lpos, 0.0))
    tcnt_ref[0] = jnp.broadcast_to(counts, (SUBLANES, LANES))


def _merge_out(x, h, yr, ya, mod, mod_row, wts):
    n = x.shape[0]
    n_tiles = n // TM
    tile = lambda w: pl.BlockSpec((TM, w), lambda i: (i, 0))
    return pl.pallas_call(
        _out_kernel,
        grid=(n_tiles,),
        in_specs=[tile(D), tile(D), tile(DR), tile(DH),
                  pl.BlockSpec((1, 6, D), lambda i: (mod_row(i), 0, 0)),
                  _resident(wts["w_in_t"].shape), _resident((DR, D)), _resident((DH, D)), _resident((D, D)),
                  _resident((1, D)), _resident((D, 2 * LANES)), _resident((1, LANES))],
        out_specs=[tile(D), pl.BlockSpec((TS, XW), lambda i: (i, 0)), tile(LANES),
                   pl.BlockSpec((1, SUBLANES, LANES), lambda i: (i, 0, 0))],
        out_shape=[jax.ShapeDtypeStruct((n, D), F32), jax.ShapeDtypeStruct((n_tiles * TS, XW), BF),
                   jax.ShapeDtypeStruct((n, LANES), F32),
                   jax.ShapeDtypeStruct((n_tiles, SUBLANES, LANES), F32)],
        compiler_params=_cparams(("arbitrary",)),
        name="merge_out",
    )(x, h, yr, ya, mod, wts["w_in_t"], wts["w_o_rnn"], wts["w_o_mla"], wts["w_out"],
      wts["n2"], wts["router_w"], wts["router_b"])


def _run_copies(local_ref, far_ref, len_ref, first, count, make_copy, action):
    for g in range(count):
        idx = first + g
        n = len_ref[idx]
        local0 = local_ref[idx] if local_ref is not None else 0
        far0 = far_ref[idx]
        for k in reversed(range(RUN_ALIGN.bit_length() - 1, TS.bit_length())):
            size = 1 << k
            done = (n >> (k + 1)) << (k + 1)

            @pl.when(((n >> k) & 1) == 1)
            def _():
                src = local0 if local_ref is None else pl.multiple_of(local0 + done, RUN_ALIGN)
                action(make_copy(src, pl.multiple_of(far0 + done, RUN_ALIGN), size))


ZERO_RUNS = 12


def _dispatch_kernel(local_ref, far_ref, len_ref, zfar_ref, zlen_ref, hxc_ref, hxl_ref, xs_ref,
                     zero_ref, sem, *, n_ctx_tiles):
    tile = pl.program_id(0)

    def copy_from(src_ref):
        def copy(src, dst, size):
            return pltpu.make_async_copy(src_ref.at[pl.ds(src, size)], xs_ref.at[pl.ds(dst, size)], sem)
        return copy

    def move(copy, *tables):
        _run_copies(*tables, copy, lambda c: c.start())
        _run_copies(*tables, copy, lambda c: c.wait())

    @pl.when(tile == 0)
    def _():
        zero_ref[...] = jnp.zeros_like(zero_ref)
        move(copy_from(zero_ref), None, zfar_ref, zlen_ref, 0, ZERO_RUNS)

    @pl.when(tile < n_ctx_tiles)
    def _():
        move(copy_from(hxc_ref), local_ref, far_ref, len_ref, tile * NG, NG)

    @pl.when(tile >= n_ctx_tiles)
    def _():
        move(copy_from(hxl_ref), local_ref, far_ref, len_ref, tile * NG, NG)


def _dispatch(tables, zero_tables, hx_c, hx_l, n_rows):
    n_c, n_l = hx_c.shape[0] // TS, hx_l.shape[0] // TS
    return pl.pallas_call(
        functools.partial(_dispatch_kernel, n_ctx_tiles=n_c),
        grid_spec=pltpu.PrefetchScalarGridSpec(
            num_scalar_prefetch=5,
            grid=(n_c + n_l,),
            in_specs=[pl.BlockSpec((TS, XW), lambda i, *_: (jnp.minimum(i, n_c - 1), 0)),
                      pl.BlockSpec((TS, XW), lambda i, *_: (jnp.maximum(i - n_c, 0), 0))],
            out_specs=pl.BlockSpec(memory_space=pl.ANY),
            scratch_shapes=[pltpu.VMEM((TMOE, XW), BF), pltpu.SemaphoreType.DMA(())],
        ),
        out_shape=jax.ShapeDtypeStruct((n_rows, XW), BF),
        compiler_params=_cparams(("arbitrary",)),
        name="dispatch",
    )(*tables, *zero_tables, hx_c, hx_l)


def _opens_group(j, tg):
    return (j == 0) | (tg[j] != tg[jnp.maximum(j - 1, 0)])


def _moe_kernel(tb_ref, tg_ref, nt_ref, xs_ref, w1_ref, w3_ref, w2_ref, o_ref,
                acc_ref, w1c_ref, w3c_ref, w2c_ref):
    j = pl.program_id(0)
    pair = pl.program_id(1)

    @pl.when((j < nt_ref[0]) & _opens_group(j, tg_ref))
    def _():
        w1c_ref[pair] = w1_ref[...].astype(BF)
        w3c_ref[pair] = w3_ref[...].astype(BF)
        w2c_ref[pair] = w2_ref[...].reshape(EXPERTS_PER_STEP * DE, D).astype(BF)

    @pl.when(j < nt_ref[0])
    def _():
        xt = xs_ref[:, :D]
        cmb = xs_ref[:, D:].astype(F32)
        lane = lax.broadcasted_iota(jnp.int32, cmb.shape, 1) & (NE - 1)
        first = tg_ref[j] * EPG + pair * EXPERTS_PER_STEP
        hidden = []
        for u in range(EXPERTS_PER_STEP):
            a = _dot(xt, w1c_ref[pair, u])
            he = (a * _sigmoid(a)) * _dot(xt, w3c_ref[pair, u])
            ce = jnp.sum(jnp.where(lane == first + u, cmb, 0.0), axis=-1, keepdims=True)
            hidden.append((he * ce).astype(BF))
        y = _dot(jnp.concatenate(hidden, axis=1), w2c_ref[pair])

        @pl.when(pair == 0)
        def _():
            acc_ref[...] = y

        @pl.when(pair == EPG // EXPERTS_PER_STEP - 1)
        def _():
            o_ref[...] = (acc_ref[...] + y).astype(BF)

    @pl.when((j >= nt_ref[0]) & (pair == EPG // EXPERTS_PER_STEP - 1))
    def _():
        o_ref[...] = jnp.zeros_like(o_ref)


def _experts(tile_block, tile_group, n_tiles, xs, wts):
    m = xs.shape[0]
    steps = EPG // EXPERTS_PER_STEP
    assert steps == 2, "the kernel keeps one partial sum: first step stores it, second adds and writes"

    def w_idx(j, e, tb, tg, nt):
        needed = (j < nt[0]) & _opens_group(j, tg)
        return (tg[j] * steps + jnp.where(needed, e, steps - 1), 0, 0)

    return pl.pallas_call(
        _moe_kernel,
        grid_spec=pltpu.PrefetchScalarGridSpec(
            num_scalar_prefetch=3,
            grid=(m // TMOE, steps),
            in_specs=[pl.BlockSpec((TMOE, XW), lambda j, e, tb, tg, nt: (tb[j], 0)),
                      pl.BlockSpec((EXPERTS_PER_STEP, D, DE), w_idx),
                      pl.BlockSpec((EXPERTS_PER_STEP, D, DE), w_idx),
                      pl.BlockSpec((EXPERTS_PER_STEP, DE, D), w_idx)],
            out_specs=pl.BlockSpec((TMOE, D), lambda j, e, tb, tg, nt: (j, 0)),
            scratch_shapes=[pltpu.VMEM((TMOE, D), F32),
                            pltpu.VMEM((steps, EXPERTS_PER_STEP, D, DE), BF),
                            pltpu.VMEM((steps, EXPERTS_PER_STEP, D, DE), BF),
                            pltpu.VMEM((steps, EXPERTS_PER_STEP * DE, D), BF)],
        ),
        out_shape=jax.ShapeDtypeStruct((m, D), BF),
        compiler_params=_cparams(("arbitrary", "arbitrary")),
        name="experts",
    )(tile_block, tile_group, n_tiles, xs, wts["exp_w1"], wts["exp_w3"], wts["exp_w2"])


def _combine_kernel(local_ref, far_ref, len_ref, x1_ref, rinfo_ref, mod_ref, ys_ref, o_ref,
                    buf_ref, sem, *, n_steps):
    i = pl.program_id(0)

    def runs(step, slot, action):
        def copy(dst, src, size):
            return pltpu.make_async_copy(ys_ref.at[pl.ds(src, size)],
                                         buf_ref.at[slot, pl.ds(dst, size)], sem.at[slot])

        _run_copies(local_ref, far_ref, len_ref, step * NG, NG, copy, action)

    @pl.when(i == 0)
    def _():
        buf_ref[...] = jnp.zeros_like(buf_ref)
        runs(0, 0, lambda c: c.start())

    @pl.when(i + 1 < n_steps)
    def _():
        runs(i + 1, (i + 1) % 2, lambda c: c.start())

    slot = i % 2
    runs(i, slot, lambda c: c.wait())
    lpos = rinfo_ref[:, 1:2]
    s_i = lax.broadcasted_iota(jnp.int32, (TM, TS), 1)
    from_sorted = jnp.where(s_i.astype(F32) == lpos, 1.0, 0.0).astype(BF)
    moe = _dot(from_sorted, buf_ref[slot])
    o_ref[...] = x1_ref[...] + mod_ref[0, 5:6, :] * moe


def _combine(tables, x1, rinfo, mod, mod_row, ys):
    n = x1.shape[0]
    n_steps = n // TM
    return pl.pallas_call(
        functools.partial(_combine_kernel, n_steps=n_steps),
        grid_spec=pltpu.PrefetchScalarGridSpec(
            num_scalar_prefetch=3,
            grid=(n_steps,),
            in_specs=[pl.BlockSpec((TM, D), lambda i, *_: (i, 0)),
                      pl.BlockSpec((TM, LANES), lambda i, *_: (i, 0)),
                      pl.BlockSpec((1, 6, D), lambda i, *_: (mod_row(i), 0, 0)),
                      pl.BlockSpec(memory_space=pl.ANY)],
            out_specs=pl.BlockSpec((TM, D), lambda i, *_: (i, 0)),
            scratch_shapes=[pltpu.VMEM((2, TS, D), BF), pltpu.SemaphoreType.DMA((2,))],
        ),
        out_shape=jax.ShapeDtypeStruct((n, D), F32),
        compiler_params=_cparams(("arbitrary",)),
        name="combine",
    )(*tables, x1, rinfo, mod, ys)


def _run_lengths(tile_counts):
    counts = tile_counts[:, 0, :NG].astype(jnp.int32)
    return ((counts + RUN_ALIGN - 1) // RUN_ALIGN) * RUN_ALIGN


def _run_tables(lengths, first_far):
    local = jnp.cumsum(lengths, axis=1) - lengths
    far = first_far[None, :] + jnp.cumsum(lengths, axis=0) - lengths
    flat = lambda a: a.astype(jnp.int32).reshape(-1)
    return flat(local), flat(far), flat(lengths)


def _group_layout(counts, max_tiles):
    padded = ((counts + TMOE - 1) // TMOE) * TMOE
    ends = jnp.cumsum(padded)
    offsets = ends - padded
    n_tiles = (ends[-1] // TMOE).astype(jnp.int32)
    tile = jnp.minimum(jnp.arange(max_tiles, dtype=jnp.int32), jnp.maximum(n_tiles - 1, 0))
    tile_group = jnp.sum((tile[:, None] * TMOE >= ends[None, :]).astype(jnp.int32), axis=1)
    spare = ends[-1] + TMOE * jnp.arange(ZERO_RUNS - NG, dtype=jnp.int32)
    spare_len = jnp.where(spare < max_tiles * TMOE, TMOE, 0)
    zero_start = jnp.concatenate([offsets + counts, jnp.where(spare_len > 0, spare, 0)]).astype(jnp.int32)
    zero_len = jnp.concatenate([padded - counts, spare_len]).astype(jnp.int32)
    return offsets, tile, tile_group, n_tiles.reshape(1), (zero_start, zero_len)


def _pad_heads(w, perm=None, rotary_only=False):
    lead = w.shape[:-1]
    per = w.shape[-1] // NH
    w = w.reshape(lead + (NH, per))
    if perm is not None:
        nope = jnp.zeros_like(w[..., :NOPE]) if rotary_only else w[..., :NOPE]
        w = jnp.concatenate([nope, w[..., NOPE:][..., perm]], axis=-1)
    w = jnp.pad(w, [(0, 0)] * len(lead) + [(0, 0), (0, HP - per)])
    return w.reshape(lead + (NH * HP,))


def _pad_gain(g, perm, rotary_only=False):
    nope = jnp.zeros((NOPE,), F32) if rotary_only else g[:NOPE]
    g = jnp.concatenate([nope, g[NOPE:][perm], jnp.zeros((HP - QK,), F32)])
    return g.reshape(1, HP)


def _block_diag(w):
    per = BDW // LRU_BLOCK
    rows = w.reshape(DR // BDW, BDW, LRU_BLOCK)
    idx = np.arange(BDW) // LRU_BLOCK
    mask = jnp.asarray(idx[:, None] == idx[None, :], w.dtype)
    return jnp.concatenate([rows] * per, axis=-1) * mask


def _prepare_shared(l, p):
    w_in = lambda a, b: jnp.transpose(lax.slice(p["w_in"], (l, 0, a), (l + 1, D, b)).reshape(D, b - a))
    bd = jnp.stack([_block_diag(p["lru_wa"][l, 0]), _block_diag(p["lru_wx"][l, 0]),
                    _block_diag(p["lru_wa"][l, 1]), _block_diag(p["lru_wx"][l, 1])])
    bd = (0.5 * bd).astype(BF)
    wom = p["w_o_mla"][l].reshape(NH, VD, D)
    wom = jnp.pad(wom, ((0, 0), (0, HP - VD), (0, 0))).reshape(DH, D)
    router_w = jnp.concatenate([p["router_we"][l], p["router_wg"][l],
                                jnp.zeros((D, LANES - NE - NG), F32)], axis=1)
    router_b = jnp.concatenate([p["router_be"][l], p["router_bg"][l],
                                jnp.zeros((LANES - NE - NG,), F32)]).reshape(1, LANES)
    router_hi = router_w.astype(BF)
    router_lo = (router_w - router_hi.astype(F32)).astype(BF)
    router_w = jnp.concatenate([router_hi, router_lo], axis=1)
    return {
        "n1": p["norm1_g"][l].reshape(1, D), "n2": p["norm2_g"][l].reshape(1, D),
        "w_in_t": jnp.transpose(p["w_in"][l]).astype(BF),
        "w_kv": w_in(O_KV, O_KR), "w_kr": w_in(O_KR, O_GATE),
        "qan": p["q_a_norm"][l].reshape(1, QL), "kvan": p["kv_a_norm"][l].reshape(1, KVL),
        "w_uk": _pad_heads(p["w_uk"][l]).astype(BF),
        "w_uv": _pad_heads(p["w_uv"][l]).astype(BF),
        "conv_w": p["conv_w"][l], "conv_b": p["conv_b"][l].reshape(1, DR), "bd": bd,
        "lru_ba": 0.5 * p["lru_ba"][l], "lru_bx": 0.5 * p["lru_bx"][l], "lru_lam": p["lru_lam"][l],
        "w_o_rnn": p["w_o_rnn"][l].astype(BF), "w_o_mla": wom.astype(BF), "w_out": (0.5 * p["w_out"][l]).astype(BF),
        "router_w": router_w, "router_b": router_b,
        "exp_w1": p["exp_w1"][l], "exp_w3": p["exp_w3"][l], "exp_w2": p["exp_w2"][l],
    }


def _with_rope_order(l, p, shared, perm, rotary):
    w = dict(shared)
    zeros = lambda n: jnp.zeros((n, D), F32)
    rope_block = lambda order: [zeros(NOPE), shared["w_kr"][order, :], zeros(HP - QK)]
    kvr = [shared["w_kv"]] + rope_block(perm)
    w["w_uq"] = _pad_heads(p["w_uq"][l], perm).astype(BF)
    w["gq"] = _pad_gain(p["q_norm"][l], perm) * SCORE_SCALE
    w["gk"] = _pad_gain(p["k_norm"][l], perm)
    if rotary:
        pair = np.concatenate([perm[ROPE // 2:], perm[:ROPE // 2]])
        kvr += rope_block(pair)
        w["w_uq_pair"] = _pad_heads(p["w_uq"][l], pair, rotary_only=True).astype(BF)
        w["gq_pair"] = _pad_gain(p["q_norm"][l], pair, rotary_only=True) * SCORE_SCALE
        w["gk_pair"] = _pad_gain(p["k_norm"][l], pair, rotary_only=True)
    w["w_kvr"] = jnp.concatenate(kvr, axis=0).astype(BF)
    return w


def _rope_tables(n_tokens):
    rows = n_tokens // GRID_W
    row = np.repeat(np.arange(rows), GRID_W).astype(np.float32)
    col = np.tile(np.arange(GRID_W), rows).astype(np.float32)
    axis_dim = ROPE // 2
    inv = (np.float32(ROPE_BASE) ** (-np.arange(0, axis_dim, 2, dtype=np.float32) / axis_dim)).astype(np.float32)
    ang = np.concatenate([row[:, None] * inv, col[:, None] * inv], axis=-1).astype(np.float32)
    cos, sin = np.cos(ang), np.sin(ang)
    ones = lambda n: np.ones((n_tokens, n), np.float32)
    zeros = lambda n: np.zeros((n_tokens, n), np.float32)
    cos_t = np.concatenate([ones(NOPE), cos, cos, ones(HP - QK)], axis=1)
    sin_t = np.concatenate([zeros(NOPE), -sin, sin, zeros(HP - QK)], axis=1)
    return jnp.asarray(cos_t, F32), jnp.asarray(sin_t, F32)


def kernel(x_prompt, x_sample, cache_mla_ckv, cache_mla_krope, state_rglru, c, c_ctx, norm1_g, norm2_g, w_mod, b_mod, w_in, conv_w, conv_b, lru_wa, lru_ba, lru_wx, lru_bx, lru_lam, q_a_norm, kv_a_norm, w_uq, w_uk, w_uv, q_norm, k_norm, w_o_rnn, w_o_mla, w_out, router_wg, router_bg, router_we, router_be, exp_w1, exp_w3, exp_w2):
    p = dict(norm1_g=norm1_g, norm2_g=norm2_g, w_in=w_in, conv_w=conv_w, conv_b=conv_b,
             lru_wa=lru_wa, lru_ba=lru_ba, lru_wx=lru_wx, lru_bx=lru_bx, lru_lam=lru_lam,
             q_a_norm=q_a_norm, kv_a_norm=kv_a_norm, w_uq=w_uq, w_uk=w_uk, w_uv=w_uv,
             q_norm=q_norm, k_norm=k_norm, w_o_rnn=w_o_rnn, w_o_mla=w_o_mla, w_out=w_out,
             router_wg=router_wg, router_bg=router_bg, router_we=router_we, router_be=router_be,
             exp_w1=exp_w1, exp_w3=exp_w3, exp_w2=exp_w2)
    depth = w_in.shape[0]
    nb, seq, _ = x_prompt.shape
    db, dseq, _ = x_sample.shape
    ident = np.arange(ROPE)
    halves = np.concatenate([np.arange(0, ROPE, 2), np.arange(1, ROPE, 2)])
    rope_tabs = _rope_tables(dseq)
    cond8 = jnp.concatenate([c_ctx[None, :], c, jnp.zeros((SUBLANES - 1 - db, D), F32)], axis=0)
    ctx_row = lambda tile_rows: (lambda i: 0)
    lat_row = lambda tile_rows: (lambda i: (i * tile_rows) // dseq + 1)
    n_ctx, n_lat = nb * seq, db * dseq
    run_padding = ((n_ctx + n_lat) // TM) * NG * (RUN_ALIGN - 1)
    max_tiles = -(-(n_ctx + n_lat + run_padding) // TMOE) + NG
    assert max_tiles - (n_ctx + n_lat) // TMOE <= ZERO_RUNS - NG, "more unused expert tiles than zero runs"
    per_seq = lambda arrs, b, t: [a.reshape(b, t, a.shape[-1]) for a in arrs]
    flat = lambda a: a.reshape(-1, a.shape[-1])

    y_prompt, y_sample = x_prompt.reshape(n_ctx, D), x_sample.reshape(n_lat, D)
    ckv_list, krope_list, rnn_list = [], [], []
    for l in range(depth):
        shared = _prepare_shared(l, p)
        w_ctx = _with_rope_order(l, p, shared, ident, False)
        w_lat = _with_rope_order(l, p, shared, halves, True)
        mod = _modulation(cond8, w_mod[l], b_mod[l]).reshape(SUBLANES, 6, D)

        h, xr, gg, q, k, v, ckv, kro = _projections(y_prompt, mod, ctx_row(TM), w_ctx, None, True)
        xr, gg, q, k, v = per_seq([xr, gg, q, k, v], nb, seq)
        yr, h_fin = _rglru(xr, gg, shared, None, True)
        ya = _attention(q, k, v, None, None, 4, NH)
        x1_c, hx_c, ri_c, tc_c = _merge_out(y_prompt, h, flat(yr), flat(ya), mod, ctx_row(TM), shared)
        ckv_list.append(ckv.reshape(nb, seq, KVL))
        krope_list.append(kro.reshape(nb, seq, ROPE))
        rnn_list.append(h_fin)

        krp_cache = jnp.pad(cache_mla_krope[:, l][..., halves], ((0, 0), (0, 0), (NOPE, HP - QK)))
        kc, vc = _cache_keys_values(cache_mla_ckv[:, l], krp_cache, w_lat)
        h, xr, gg, q, k, v = _projections(y_sample, mod, lat_row(TM), w_lat, rope_tabs, False)
        xr, gg, q, k, v = per_seq([xr, gg, q, k, v], db, dseq)
        yr, _ = _rglru(xr, gg, shared, state_rglru[:, l], False)
        ya = _attention(q, k, v, kc, vc, 1, 4)
        x1_l, hx_l, ri_l, tc_l = _merge_out(y_sample, h, flat(yr), flat(ya), mod, lat_row(TM), shared)

        len_c, len_l = _run_lengths(tc_c), _run_lengths(tc_l)
        rows_c = jnp.sum(len_c, axis=0)
        offsets, tile_block, tile_group, n_tiles, zero_runs = _group_layout(
            rows_c + jnp.sum(len_l, axis=0), max_tiles)
        runs_c = _run_tables(len_c, offsets)
        runs_l = _run_tables(len_l, offsets + rows_c)
        runs = tuple(jnp.concatenate([a, b]) for a, b in zip(runs_c, runs_l))
        xs = _dispatch(runs, zero_runs, hx_c, hx_l, max_tiles * TMOE)
        ys = _experts(tile_block, tile_group, n_tiles, xs, shared)
        y_prompt = _combine(runs_c, x1_c, ri_c, mod, ctx_row(TM), ys)
        y_sample = _combine(runs_l, x1_l, ri_l, mod, lat_row(TM), ys)

    y_prompt, y_sample = y_prompt.reshape(nb, seq, D), y_sample.reshape(db, dseq, D)

    return (y_prompt, y_sample, jnp.stack(ckv_list, axis=1), jnp.stack(krope_list, axis=1),
            jnp.stack(rnn_list, axis=1))
```

```python
import functools
import math

import numpy as np
import jax
import jax.numpy as jnp
from jax import lax
from jax.experimental import pallas as pl
from jax.experimental.pallas import tpu as pltpu

D = 1024
DR = 1024
QL = 384
KVL = 256
NH = 8
NOPE = 64
ROPE = 32
QK = NOPE + ROPE
SCORE_SCALE = (QK ** -0.5) * math.log2(math.e)
VD = 64
O_G, O_Q, O_KV, O_KR, O_GATE = DR, 2 * DR, 2 * DR + QL, 2 * DR + QL + KVL, 2 * DR + QL + KVL + ROPE
HP = 128
DH = NH * HP
GRID_W = 64
ROPE_BASE = 10000.0
EPS = 1e-6
TINY = 1e-30
LRU_C = 8.0
LRU_BLOCK = 64
BDW = 256
CH = 1024
NG = 4
EPG = 4
NE = NG * EPG
DE = 512
LANES = 128
SUBLANES = 8
TM = 512
TS = 576
TMOE = 512
EXPERTS_PER_STEP = 2
XW = D + LANES
RUN_ALIGN = 16
VMEM_LIMIT = 52 * 1024 * 1024
BF = jnp.bfloat16
F32 = jnp.float32


def _cparams(sem):
    return pltpu.CompilerParams(dimension_semantics=sem, vmem_limit_bytes=VMEM_LIMIT)


def _dot(a, b):
    return jnp.dot(a, b, preferred_element_type=F32)


def _dot_nt(a, b):
    return lax.dot_general(a, b, (((1,), (1,)), ((), ())), preferred_element_type=F32)


def _rms(x, g, width):
    ms = jnp.sum(x * x, axis=-1, keepdims=True) * (1.0 / width)
    return x * lax.rsqrt(ms + EPS) * g


def _modulated_norm(x, g, scale, shift):
    return _rms(x, g * (1.0 + scale), D) + shift


def _mod_kernel(c_ref, w_ref, b_ref, o_ref):
    c = c_ref[...]
    s = c * jax.nn.sigmoid(c)
    o_ref[...] = _dot(s, w_ref[...]) + b_ref[...]


def _modulation(cond8, w_mod, b_mod):
    n = w_mod.shape[1]
    return pl.pallas_call(
        _mod_kernel,
        grid=(n // D,),
        in_specs=[
            pl.BlockSpec((SUBLANES, D), lambda j: (0, 0)),
            pl.BlockSpec((D, D), lambda j: (0, j)),
            pl.BlockSpec((1, D), lambda j: (0, j)),
        ],
        out_specs=pl.BlockSpec((SUBLANES, D), lambda j: (0, j)),
        out_shape=jax.ShapeDtypeStruct((SUBLANES, n), F32),
        compiler_params=_cparams(("arbitrary",)),
        name="modulation",
    )(cond8, w_mod, b_mod.reshape(1, n))


def _head_norm(xh, gain, cos=None, partner_scaled=None):
    ms = jnp.sum(xh * xh, axis=-1, keepdims=True) * (1.0 / QK)
    rs = lax.rsqrt(ms + EPS)
    y = xh * rs * gain
    if cos is None:
        return y
    return y * cos + partner_scaled * rs


def _keys_values(ckv, krp, wuk_ref, wuv_ref, gk, cos, partner_scaled, k_ref, v_ref):
    cb = ckv.astype(BF)
    kn = _dot(cb, wuk_ref[...])
    v_ref[...] = _dot(cb, wuv_ref[...]).astype(BF)
    for h in range(NH):
        kh = kn[:, h * HP:(h + 1) * HP] + krp
        k_ref[:, h * HP:(h + 1) * HP] = _head_norm(kh, gk, cos, partner_scaled).astype(BF)


def _proj_kernel(*refs, rope, emit_cache):
    it = iter(refs)
    x_ref, mod_ref, n1_ref = next(it), next(it), next(it)
    win_ref, wkvr_ref = next(it), next(it)
    wx_ref, wg_ref, wq_ref = win_ref.at[0:O_G], win_ref.at[O_G:O_Q], win_ref.at[O_Q:O_KV]
    qan_ref, kvan_ref, wuq_ref, gq_ref = next(it), next(it), next(it), next(it)
    wuk_ref, gk_ref, wuv_ref = next(it), next(it), next(it)
    if rope:
        wuqs_ref, gqs_ref, gks_ref, cos_ref, sins_ref = (next(it) for _ in range(5))
    h_ref, xr_ref, gg_ref, q_ref, k_ref, v_ref = (next(it) for _ in range(6))
    if emit_cache:
        ckv_ref, kro_ref = next(it), next(it)

    hb = _modulated_norm(x_ref[...], n1_ref[...], mod_ref[0, 1:2, :], mod_ref[0, 0:1, :]).astype(BF)
    h_ref[...] = hb
    xr_ref[...] = _dot_nt(hb, wx_ref[...]).astype(BF)
    gg_ref[...] = jax.nn.gelu(_dot_nt(hb, wg_ref[...])).astype(BF)

    qnb = _rms(_dot_nt(hb, wq_ref[...]), qan_ref[...], QL).astype(BF)
    q = _dot(qnb, wuq_ref[...])
    gq = gq_ref[...]
    cos = q_partner = q_pair_scale = None
    if rope:
        cos, sins = cos_ref[...], sins_ref[...]
        q_partner = _dot(qnb, wuqs_ref[...])
        q_pair_scale = gqs_ref[...] * sins
    for hd in range(NH):
        cols = slice(hd * HP, (hd + 1) * HP)
        partner = q_partner[:, cols] * q_pair_scale if rope else None
        q_ref[:, cols] = _head_norm(q[:, cols], gq, cos, partner).astype(BF)

    kvr = _dot_nt(hb, wkvr_ref[...])
    ckv = _rms(kvr[:, :KVL], kvan_ref[...], KVL)
    krp = kvr[:, KVL:KVL + HP]
    k_partner = kvr[:, KVL + HP:KVL + 2 * HP] * (gks_ref[...] * sins) if rope else None
    if emit_cache:
        ckv_ref[...] = ckv
        kro_ref[...] = krp[:, NOPE:NOPE + ROPE]
    _keys_values(ckv, krp, wuk_ref, wuv_ref, gk_ref[...], cos, k_partner, k_ref, v_ref)


def _resident(shape):
    return pl.BlockSpec(shape, lambda i: (0,) * len(shape), pipeline_mode=pl.Buffered(1))


def _projections(x, mod, mod_row, wts, rope_tabs, emit_cache):
    n = x.shape[0]
    rope = rope_tabs is not None
    tile = lambda w: pl.BlockSpec((TM, w), lambda i: (i, 0))
    in_specs = [
        tile(D),
        pl.BlockSpec((1, 6, D), lambda i: (mod_row(i), 0, 0)),
        _resident((1, D)),
        _resident(wts["w_in_t"].shape), _resident(wts["w_kvr"].shape),
        _resident((1, QL)), _resident((1, KVL)), _resident((QL, DH)), _resident((1, HP)),
        _resident((KVL, DH)), _resident((1, HP)), _resident((KVL, DH)),
    ]
    args = [x, mod, wts["n1"], wts["w_in_t"], wts["w_kvr"],
            wts["qan"], wts["kvan"], wts["w_uq"], wts["gq"], wts["w_uk"], wts["gk"], wts["w_uv"]]
    if rope:
        tiles_per_seq = rope_tabs[0].shape[0] // TM
        in_specs += [_resident((QL, DH)), _resident((1, HP)), _resident((1, HP))]
        in_specs += [pl.BlockSpec((TM, HP), lambda i: (i % tiles_per_seq, 0))] * 2
        args += [wts["w_uq_pair"], wts["gq_pair"], wts["gk_pair"]] + list(rope_tabs)
    out_specs = [tile(D), tile(DR), tile(DR), tile(DH), tile(DH), tile(DH)]
    out_shape = [jax.ShapeDtypeStruct((n, D), BF), jax.ShapeDtypeStruct((n, DR), BF),
                 jax.ShapeDtypeStruct((n, DR), BF), jax.ShapeDtypeStruct((n, DH), BF),
                 jax.ShapeDtypeStruct((n, DH), BF), jax.ShapeDtypeStruct((n, DH), BF)]
    if emit_cache:
        out_specs += [tile(KVL), tile(ROPE)]
        out_shape += [jax.ShapeDtypeStruct((n, KVL), F32), jax.ShapeDtypeStruct((n, ROPE), F32)]
    return pl.pallas_call(
        functools.partial(_proj_kernel, rope=rope, emit_cache=emit_cache),
        grid=(n // TM,),
        in_specs=in_specs,
        out_specs=out_specs,
        out_shape=out_shape,
        compiler_params=_cparams(("arbitrary",)),
        name="projections",
    )(*args)


def _cache_kv_kernel(ckv_ref, krp_ref, wuk_ref, gk_ref, wuv_ref, k_ref, v_ref):
    _keys_values(ckv_ref[0], krp_ref[0], wuk_ref, wuv_ref, gk_ref[...], None, None,
                 k_ref.at[0], v_ref.at[0])


def _cache_keys_values(ckv, krp, wts):
    b, s, _ = ckv.shape
    full = lambda shape: pl.BlockSpec(shape, lambda i: (0,) * len(shape))
    return pl.pallas_call(
        _cache_kv_kernel,
        grid=(b,),
        in_specs=[pl.BlockSpec((1, s, KVL), lambda i: (i, 0, 0)),
                  pl.BlockSpec((1, s, HP), lambda i: (i, 0, 0)),
                  full((KVL, DH)), full((1, HP)), full((KVL, DH))],
        out_specs=[pl.BlockSpec((1, s, DH), lambda i: (i, 0, 0))] * 2,
        out_shape=[jax.ShapeDtypeStruct((b, s, DH), BF)] * 2,
        compiler_params=_cparams(("arbitrary",)),
        name="cache_keys_values",
    )(ckv, krp, wts["w_uk"], wts["gk"], wts["w_uv"])


WIN = 256
SEG = WIN // 8


def _window_permutation():
    dst = np.arange(WIN)
    p = np.zeros((WIN, WIN), np.float32)
    p[dst, (dst % SUBLANES) * SEG + dst // SUBLANES] = 1.0
    return p


def _sigmoid(x):
    return 0.5 * jnp.tanh(0.5 * x) + 0.5


def _segment_pass(a_scr, b_scr, bases, inits, out_scr=None):
    def body(k, carry):
        new = []
        for d in range(2):
            h, p = carry[d]
            i = k if d == 0 else SEG - 1 - k
            rows = pl.ds(pl.multiple_of(bases[d] + i * SUBLANES, SUBLANES), SUBLANES)
            a = a_scr[d, rows, :]
            h = a * h + b_scr[d, rows, :]
            if out_scr is None:
                p = a * p
            else:
                out_scr[d, rows, :] = h
            new.append((h, p))
        return tuple(new)

    init = tuple((inits[d], jnp.ones_like(inits[d])) for d in range(2))
    return lax.fori_loop(0, SEG, body, init, unroll=4)


def _segment_entries(end, decay, carry_in, forward):
    order = range(SUBLANES) if forward else reversed(range(SUBLANES))
    rows = [None] * SUBLANES
    c = carry_in
    for s in order:
        rows[s] = c
        c = end[s:s + 1, :] + decay[s:s + 1, :] * c
    return jnp.concatenate(rows, axis=0), c


def _rglru_kernel(*refs, t, has_h0, emit_state):
    it = iter(refs)
    xr_ref, gg_ref, perm_ref, unperm_ref = (next(it) for _ in range(4))
    cw_ref, cb_ref, bd_ref, ba_ref, bx_ref, lam_ref = (next(it) for _ in range(6))
    h0_ref = next(it) if has_h0 else None
    y_ref = next(it)
    hf_ref = next(it) if emit_state else None
    a_scr, b_scr, h_scr = (next(it) for _ in range(3))

    n_win = t // WIN
    sub = lax.broadcasted_iota(jnp.int32, (SUBLANES, CH), 0)
    zero_row = jnp.zeros((1, CH), F32)
    edge = 2 * SUBLANES
    for w in range(n_win):
        lo, hi = w * WIN, (w + 1) * WIN
        xp = _dot(perm_ref[...], xr_ref[0, lo:hi, :])
        before = xr_ref[0, lo - edge:lo, :].astype(F32)[edge - 1:edge, :] if w > 0 else zero_row
        after = xr_ref[0, hi:hi + edge, :].astype(F32) if w < n_win - 1 else None
        after0 = after[0:1, :] if after is not None else zero_row
        after1 = after[1:2, :] if after is not None else zero_row
        tile_m1 = jnp.where(sub == 0, before, pltpu.roll(xp[WIN - SUBLANES:WIN, :], 1, 0))
        tile_p0 = jnp.where(sub == SUBLANES - 1, after0, pltpu.roll(xp[0:SUBLANES, :], SUBLANES - 1, 0))
        tile_p1 = jnp.where(sub == SUBLANES - 1, after1,
                            pltpu.roll(xp[SUBLANES:2 * SUBLANES, :], SUBLANES - 1, 0))
        xe = jnp.concatenate([tile_m1, xp, tile_p0, tile_p1], axis=0)
        xc = cb_ref[...]
        for tap in range(4):
            xc = xc + xe[tap * SUBLANES:tap * SUBLANES + WIN, :] * cw_ref[tap:tap + 1, :]
        for s in range(CH // BDW):
            cols = slice(s * BDW, (s + 1) * BDW)
            xs = xc[:, cols]
            xsb = xs.astype(BF)
            xh = 0.5 * xs
            for d in range(2):
                tr = jnp.tanh(_dot(xsb, bd_ref[2 * d, s]) + ba_ref[d:d + 1, cols])
                ti = jnp.tanh(_dot(xsb, bd_ref[2 * d + 1, s]) + bx_ref[d:d + 1, cols])
                nl = -lam_ref[d:d + 1, cols]
                softplus = jnp.maximum(nl, 0.0) + jnp.log(1.0 + jnp.exp(-jnp.abs(nl)))
                ch = (-0.5 * LRU_C) * softplus
                a = jnp.exp(tr * ch + ch)
                z = 1.0 - a * a
                root = z * lax.rsqrt(jnp.maximum(z, TINY))
                a_scr[d, lo:hi, cols] = a
                b_scr[d, lo:hi, cols] = root * (ti * xh + xh)

    if has_h0:
        carry = [h0_ref[0, 0:1, :], h0_ref[0, 1:2, :]]
    else:
        carry = [zero_row, zero_row]
    zeros = jnp.zeros((SUBLANES, CH), F32)
    for k in range(n_win):
        bases = (k * WIN, (n_win - 1 - k) * WIN)
        totals = _segment_pass(a_scr, b_scr, bases, (zeros, zeros))
        entries = []
        for d in range(2):
            entry, carry[d] = _segment_entries(totals[d][0], totals[d][1], carry[d], d == 0)
            entries.append(entry)
        _segment_pass(a_scr, b_scr, bases, entries, out_scr=h_scr)

    if emit_state:
        hf_ref[0, 0:1, :] = carry[0]
        hf_ref[0, 1:2, :] = carry[1]
    for w in range(n_win):
        lo, hi = w * WIN, (w + 1) * WIN
        gate = _dot(perm_ref[...], gg_ref[0, lo:hi, :])
        yp = ((h_scr[0, lo:hi, :] + h_scr[1, lo:hi, :]) * gate).astype(BF)
        y_ref[0, lo:hi, :] = _dot(unperm_ref[...], yp).astype(BF)


def _rglru(xr, gg, wts, h0, emit_state):
    b, t, _ = xr.shape
    nc = DR // CH
    has_h0 = h0 is not None
    chunk = lambda r: pl.BlockSpec((r, CH), lambda i, j: (0, j))
    seq = pl.BlockSpec((1, t, CH), lambda i, j: (i, 0, j))
    state = pl.BlockSpec((1, 2, CH), lambda i, j: (i, 0, j))
    window = pl.BlockSpec((WIN, WIN), lambda i, j: (0, 0))
    perm = _window_permutation()
    in_specs = [seq, seq, window, window, chunk(4), chunk(1),
                pl.BlockSpec((4, CH // BDW, BDW, BDW), lambda i, j: (0, j, 0, 0)),
                chunk(2), chunk(2), chunk(2)]
    args = [xr, gg, jnp.asarray(perm, BF), jnp.asarray(perm.T, BF),
            wts["conv_w"], wts["conv_b"], wts["bd"], wts["lru_ba"], wts["lru_bx"], wts["lru_lam"]]
    if has_h0:
        in_specs.append(state)
        args.append(h0)
    out_specs = [seq]
    out_shape = [jax.ShapeDtypeStruct((b, t, DR), BF)]
    if emit_state:
        out_specs.append(state)
        out_shape.append(jax.ShapeDtypeStruct((b, 2, DR), F32))
    res = pl.pallas_call(
        functools.partial(_rglru_kernel, t=t, has_h0=has_h0, emit_state=emit_state),
        grid=(b, nc),
        in_specs=in_specs,
        out_specs=out_specs,
        out_shape=out_shape,
        scratch_shapes=[pltpu.VMEM((2, t, CH), F32)] * 3,
        compiler_params=_cparams(("arbitrary", "arbitrary")),
        name="rglru",
    )(*args)
    return res if emit_state else (res[0], None)


def _attn_kernel(*refs, t, n_seqs, n_heads, has_ctx, q_block):
    it = iter(refs)
    q_ref, k_ref, v_ref = next(it), next(it), next(it)
    kc_ref = vc_ref = None
    if has_ctx:
        kc_ref, vc_ref = next(it), next(it)
    o_ref = next(it)
    for sq, hd in [(sq, hd) for sq in range(n_seqs) for hd in range(n_heads)]:
        cols = slice(hd * HP, (hd + 1) * HP)
        k = k_ref[sq, :, cols]
        v = v_ref[sq, :, cols]
        if has_ctx:
            kc = kc_ref[sq, :, cols]
            vc = vc_ref[sq, :, cols]
        for qb in range(t // q_block):
            rows = slice(qb * q_block, (qb + 1) * q_block)
            q = q_ref[sq, rows, cols]
            s = _dot_nt(q, k)
            m = jnp.max(s, axis=-1, keepdims=True)
            if has_ctx:
                sc = _dot_nt(q, kc)
                m = jnp.maximum(m, jnp.max(sc, axis=-1, keepdims=True))
            p = jnp.exp2(s - m)
            den = jnp.sum(p, axis=-1, keepdims=True)
            o = _dot(p.astype(BF), v)
            if has_ctx:
                pc = jnp.exp2(sc - m)
                den = den + jnp.sum(pc, axis=-1, keepdims=True)
                o = o + _dot(pc.astype(BF), vc)
            o_ref[sq, rows, cols] = (o / den).astype(BF)


def _attention(q, k, v, kc, vc, seqs_per_step, heads_per_step):
    b, t, _ = q.shape
    has_ctx = kc is not None
    w = heads_per_step * HP
    blk = lambda n: pl.BlockSpec((seqs_per_step, n, w), lambda i, j: (i, 0, j))
    in_specs = [blk(t), blk(t), blk(t)]
    args = [q, k, v]
    if has_ctx:
        in_specs += [blk(kc.shape[1])] * 2
        args += [kc, vc]
    return pl.pallas_call(
        functools.partial(_attn_kernel, t=t, n_seqs=seqs_per_step, n_heads=heads_per_step,
                          has_ctx=has_ctx, q_block=min(t, 256)),
        grid=(b // seqs_per_step, NH // heads_per_step),
        in_specs=in_specs,
        out_specs=blk(t),
        out_shape=jax.ShapeDtypeStruct((b, t, DH), BF),
        compiler_params=_cparams(("arbitrary", "arbitrary")),
        name="attention",
    )(*args)


def _route(logits):
    lane = lax.broadcasted_iota(jnp.int32, logits.shape, 1)
    lanef = lane.astype(F32)
    neg = -jnp.inf
    big = float(LANES)
    gl = jnp.where((lane >= NE) & (lane < NE + NG), logits, neg)
    gmax = jnp.max(gl, axis=-1, keepdims=True)
    gidx = jnp.min(jnp.where(gl == gmax, lanef, big), axis=-1, keepdims=True) - float(NE)
    gw = 1.0 / jnp.sum(jnp.exp(gl - gmax), axis=-1, keepdims=True)
    lo = gidx * float(EPG)
    el = jnp.where((lanef >= lo) & (lanef < lo + float(EPG)), logits, neg)
    v1 = jnp.max(el, axis=-1, keepdims=True)
    i1 = jnp.min(jnp.where(el == v1, lanef, big), axis=-1, keepdims=True)
    el2 = jnp.where(lanef == i1, neg, el)
    v2 = jnp.max(el2, axis=-1, keepdims=True)
    i2 = jnp.min(jnp.where(el2 == v2, lanef, big), axis=-1, keepdims=True)
    e2 = jnp.exp(v2 - v1)
    w1 = gw / (1.0 + e2)
    w2 = gw * e2 / (1.0 + e2)
    cmb = jnp.where(lanef == i1, w1, 0.0) + jnp.where(lanef == i2, w2, 0.0)
    return cmb, gidx


def _out_kernel(x_ref, h_ref, yr_ref, ya_ref, mod_ref, win_ref, wor_ref, wom_ref, wout_ref,
                n2_ref, rw_ref, rb_ref, x1_ref, hx_ref, rinfo_ref, tcnt_ref):
    x = x_ref[...]
    gl = _dot_nt(h_ref[...], win_ref[O_GATE:O_GATE + 2 * D, :])
    merged2 = ((jnp.tanh(0.5 * gl[:, :D]) + 1.0) * _dot(yr_ref[...], wor_ref[...])
               + (jnp.tanh(0.5 * gl[:, D:]) + 1.0) * _dot(ya_ref[...], wom_ref[...]))
    mix = _dot(merged2.astype(BF), wout_ref[...])
    x1 = x + mod_ref[0, 2:3, :] * mix
    x1_ref[...] = x1
    h2 = _modulated_norm(x1, n2_ref[...], mod_ref[0, 4:5, :], mod_ref[0, 3:4, :])
    h2_hi = h2.astype(BF)
    h2_lo = (h2 - h2_hi.astype(F32)).astype(BF)
    part = _dot(h2_hi, rw_ref[...])
    logits = part[:, :LANES] + part[:, LANES:] + _dot(h2_lo, rw_ref[:, :LANES]) + rb_ref[...]
    cmb, gidx = _route(logits)

    lanef = lax.broadcasted_iota(jnp.int32, cmb.shape, 1).astype(F32)
    ghot = jnp.where(lanef == gidx, 1.0, 0.0)
    r_i = lax.broadcasted_iota(jnp.int32, (TM, TM), 0)
    c_i = lax.broadcasted_iota(jnp.int32, (TM, TM), 1)
    tri = jnp.where(r_i > c_i, 1.0, 0.0).astype(BF)
    earlier_same = jnp.sum(_dot(tri, ghot.astype(BF)) * ghot, axis=-1, keepdims=True)
    counts = jnp.sum(ghot, axis=0, keepdims=True)
    padded = jnp.floor((counts + (RUN_ALIGN - 1.0)) * (1.0 / RUN_ALIGN)) * RUN_ALIGN
    lower_groups = jnp.sum(jnp.where(lanef < gidx, padded, 0.0), axis=-1, keepdims=True)
    lpos = lower_groups + earlier_same
    s_i = lax.broadcasted_iota(jnp.int32, (TM, TS), 1)
    to_sorted = jnp.where(s_i.astype(F32) == lpos, 1.0, 0.0).astype(BF)
    c1 = cmb.astype(BF).astype(F32)
    c2 = (cmb - c1).astype(BF).astype(F32)
    c3 = (cmb - c1 - c2).astype(BF).astype(F32)
    parts = c1 + pltpu.roll(c2, NE, 1) + pltpu.roll(c3, 2 * NE, 1)
    payload = jnp.concatenate([h2.astype(BF), parts.astype(BF)], axis=1)
    srt = lax.dot_general(to_sorted, payload, (((0,), (0,)), ((), ())), preferred_element_type=F32)
    hx_ref[...] = srt.astype(BF)

    rinfo_ref[...] = jnp.where(lanef == 0.0, gidx, jnp.where(lanef == 1.0, lpos, 0.0))
    tcnt_ref[0] = jnp.broadcast_to(counts, (SUBLANES, LANES))


def _merge_out(x, h, yr, ya, mod, mod_row, wts):
    n = x.shape[0]
    n_tiles = n // TM
    tile = lambda w: pl.BlockSpec((TM, w), lambda i: (i, 0))
    return pl.pallas_call(
        _out_kernel,
        grid=(n_tiles,),
        in_specs=[tile(D), tile(D), tile(DR), tile(DH),
                  pl.BlockSpec((1, 6, D), lambda i: (mod_row(i), 0, 0)),
                  _resident(wts["w_in_t"].shape), _resident((DR, D)), _resident((DH, D)), _resident((D, D)),
                  _resident((1, D)), _resident((D, 2 * LANES)), _resident((1, LANES))],
        out_specs=[tile(D), pl.BlockSpec((TS, XW), lambda i: (i, 0)), tile(LANES),
                   pl.BlockSpec((1, SUBLANES, LANES), lambda i: (i, 0, 0))],
        out_shape=[jax.ShapeDtypeStruct((n, D), F32), jax.ShapeDtypeStruct((n_tiles * TS, XW), BF),
                   jax.ShapeDtypeStruct((n, LANES), F32),
                   jax.ShapeDtypeStruct((n_tiles, SUBLANES, LANES), F32)],
        compiler_params=_cparams(("arbitrary",)),
        name="merge_out",
    )(x, h, yr, ya, mod, wts["w_in_t"], wts["w_o_rnn"], wts["w_o_mla"], wts["w_out"],
      wts["n2"], wts["router_w"], wts["router_b"])


def _run_copies(local_ref, far_ref, len_ref, first, count, make_copy, action):
    for g in range(count):
        idx = first + g
        n = len_ref[idx]
        local0 = local_ref[idx] if local_ref is not None else 0
        far0 = far_ref[idx]
        for k in reversed(range(RUN_ALIGN.bit_length() - 1, TS.bit_length())):
            size = 1 << k
            done = (n >> (k + 1)) << (k + 1)

            @pl.when(((n >> k) & 1) == 1)
            def _():
                src = local0 if local_ref is None else pl.multiple_of(local0 + done, RUN_ALIGN)
                action(make_copy(src, pl.multiple_of(far0 + done, RUN_ALIGN), size))


ZERO_RUNS = 12


def _dispatch_kernel(local_ref, far_ref, len_ref, zfar_ref, zlen_ref, hxc_ref, hxl_ref, xs_ref,
                     zero_ref, sem, *, n_ctx_tiles):
    tile = pl.program_id(0)

    def copy_from(src_ref):
        def copy(src, dst, size):
            return pltpu.make_async_copy(src_ref.at[pl.ds(src, size)], xs_ref.at[pl.ds(dst, size)], sem)
        return copy

    def move(copy, *tables):
        _run_copies(*tables, copy, lambda c: c.start())
        _run_copies(*tables, copy, lambda c: c.wait())

    @pl.when(tile == 0)
    def _():
        zero_ref[...] = jnp.zeros_like(zero_ref)
        move(copy_from(zero_ref), None, zfar_ref, zlen_ref, 0, ZERO_RUNS)

    @pl.when(tile < n_ctx_tiles)
    def _():
        move(copy_from(hxc_ref), local_ref, far_ref, len_ref, tile * NG, NG)

    @pl.when(tile >= n_ctx_tiles)
    def _():
        move(copy_from(hxl_ref), local_ref, far_ref, len_ref, tile * NG, NG)


def _dispatch(tables, zero_tables, hx_c, hx_l, n_rows):
    n_c, n_l = hx_c.shape[0] // TS, hx_l.shape[0] // TS
    return pl.pallas_call(
        functools.partial(_dispatch_kernel, n_ctx_tiles=n_c),
        grid_spec=pltpu.PrefetchScalarGridSpec(
            num_scalar_prefetch=5,
            grid=(n_c + n_l,),
            in_specs=[pl.BlockSpec((TS, XW), lambda i, *_: (jnp.minimum(i, n_c - 1), 0)),
                      pl.BlockSpec((TS, XW), lambda i, *_: (jnp.maximum(i - n_c, 0), 0))],
            out_specs=pl.BlockSpec(memory_space=pl.ANY),
            scratch_shapes=[pltpu.VMEM((TMOE, XW), BF), pltpu.SemaphoreType.DMA(())],
        ),
        out_shape=jax.ShapeDtypeStruct((n_rows, XW), BF),
        compiler_params=_cparams(("arbitrary",)),
        name="dispatch",
    )(*tables, *zero_tables, hx_c, hx_l)


def _opens_group(j, tg):
    return (j == 0) | (tg[j] != tg[jnp.maximum(j - 1, 0)])


def _moe_kernel(tb_ref, tg_ref, nt_ref, xs_ref, w1_ref, w3_ref, w2_ref, o_ref,
                acc_ref, w1c_ref, w3c_ref, w2c_ref):
    j = pl.program_id(0)
    pair = pl.program_id(1)

    @pl.when((j < nt_ref[0]) & _opens_group(j, tg_ref))
    def _():
        w1c_ref[pair] = w1_ref[...].astype(BF)
        w3c_ref[pair] = w3_ref[...].astype(BF)
        w2c_ref[pair] = w2_ref[...].reshape(EXPERTS_PER_STEP * DE, D).astype(BF)

    @pl.when(j < nt_ref[0])
    def _():
        xt = xs_ref[:, :D]
        cmb = xs_ref[:, D:].astype(F32)
        lane = lax.broadcasted_iota(jnp.int32, cmb.shape, 1) & (NE - 1)
        first = tg_ref[j] * EPG + pair * EXPERTS_PER_STEP
        hidden = []
        for u in range(EXPERTS_PER_STEP):
            a = _dot(xt, w1c_ref[pair, u])
            he = (a * _sigmoid(a)) * _dot(xt, w3c_ref[pair, u])
            ce = jnp.sum(jnp.where(lane == first + u, cmb, 0.0), axis=-1, keepdims=True)
            hidden.append((he * ce).astype(BF))
        y = _dot(jnp.concatenate(hidden, axis=1), w2c_ref[pair])

        @pl.when(pair == 0)
        def _():
            acc_ref[...] = y

        @pl.when(pair == EPG // EXPERTS_PER_STEP - 1)
        def _():
            o_ref[...] = (acc_ref[...] + y).astype(BF)

    @pl.when((j >= nt_ref[0]) & (pair == EPG // EXPERTS_PER_STEP - 1))
    def _():
        o_ref[...] = jnp.zeros_like(o_ref)


def _experts(tile_block, tile_group, n_tiles, xs, wts):
    m = xs.shape[0]
    steps = EPG // EXPERTS_PER_STEP
    assert steps == 2, "the kernel keeps one partial sum: first step stores it, second adds and writes"

    def w_idx(j, e, tb, tg, nt):
        needed = (j < nt[0]) & _opens_group(j, tg)
        return (tg[j] * steps + jnp.where(needed, e, steps - 1), 0, 0)

    return pl.pallas_call(
        _moe_kernel,
        grid_spec=pltpu.PrefetchScalarGridSpec(
            num_scalar_prefetch=3,
            grid=(m // TMOE, steps),
            in_specs=[pl.BlockSpec((TMOE, XW), lambda j, e, tb, tg, nt: (tb[j], 0)),
                      pl.BlockSpec((EXPERTS_PER_STEP, D, DE), w_idx),
                      pl.BlockSpec((EXPERTS_PER_STEP, D, DE), w_idx),
                      pl.BlockSpec((EXPERTS_PER_STEP, DE, D), w_idx)],
            out_specs=pl.BlockSpec((TMOE, D), lambda j, e, tb, tg, nt: (j, 0)),
            scratch_shapes=[pltpu.VMEM((TMOE, D), F32),
                            pltpu.VMEM((steps, EXPERTS_PER_STEP, D, DE), BF),
                            pltpu.VMEM((steps, EXPERTS_PER_STEP, D, DE), BF),
                            pltpu.VMEM((steps, EXPERTS_PER_STEP * DE, D), BF)],
        ),
        out_shape=jax.ShapeDtypeStruct((m, D), BF),
        compiler_params=_cparams(("arbitrary", "arbitrary")),
        name="experts",
    )(tile_block, tile_group, n_tiles, xs, wts["exp_w1"], wts["exp_w3"], wts["exp_w2"])


def _combine_kernel(local_ref, far_ref, len_ref, x1_ref, rinfo_ref, mod_ref, ys_ref, o_ref,
                    buf_ref, sem, *, n_steps):
    i = pl.program_id(0)

    def runs(step, slot, action):
        def copy(dst, src, size):
            return pltpu.make_async_copy(ys_ref.at[pl.ds(src, size)],
                                         buf_ref.at[slot, pl.ds(dst, size)], sem.at[slot])

        _run_copies(local_ref, far_ref, len_ref, step * NG, NG, copy, action)

    @pl.when(i == 0)
    def _():
        buf_ref[...] = jnp.zeros_like(buf_ref)
        runs(0, 0, lambda c: c.start())

    @pl.when(i + 1 < n_steps)
    def _():
        runs(i + 1, (i + 1) % 2, lambda c: c.start())

    slot = i % 2
    runs(i, slot, lambda c: c.wait())
    lpos = rinfo_ref[:, 1:2]
    s_i = lax.broadcasted_iota(jnp.int32, (TM, TS), 1)
    from_sorted = jnp.where(s_i.astype(F32) == lpos, 1.0, 0.0).astype(BF)
    moe = _dot(from_sorted, buf_ref[slot])
    o_ref[...] = x1_ref[...] + mod_ref[0, 5:6, :] * moe


def _combine(tables, x1, rinfo, mod, mod_row, ys):
    n = x1.shape[0]
    n_steps = n // TM
    return pl.pallas_call(
        functools.partial(_combine_kernel, n_steps=n_steps),
        grid_spec=pltpu.PrefetchScalarGridSpec(
            num_scalar_prefetch=3,
            grid=(n_steps,),
            in_specs=[pl.BlockSpec((TM, D), lambda i, *_: (i, 0)),
                      pl.BlockSpec((TM, LANES), lambda i, *_: (i, 0)),
                      pl.BlockSpec((1, 6, D), lambda i, *_: (mod_row(i), 0, 0)),
                      pl.BlockSpec(memory_space=pl.ANY)],
            out_specs=pl.BlockSpec((TM, D), lambda i, *_: (i, 0)),
            scratch_shapes=[pltpu.VMEM((2, TS, D), BF), pltpu.SemaphoreType.DMA((2,))],
        ),
        out_shape=jax.ShapeDtypeStruct((n, D), F32),
        compiler_params=_cparams(("arbitrary",)),
        name="combine",
    )(*tables, x1, rinfo, mod, ys)


def _run_lengths(tile_counts):
    counts = tile_counts[:, 0, :NG].astype(jnp.int32)
    return ((counts + RUN_ALIGN - 1) // RUN_ALIGN) * RUN_ALIGN


def _run_tables(lengths, first_far):
    local = jnp.cumsum(lengths, axis=1) - lengths
    far = first_far[None, :] + jnp.cumsum(lengths, axis=0) - lengths
    flat = lambda a: a.astype(jnp.int32).reshape(-1)
    return flat(local), flat(far), flat(lengths)


def _group_layout(counts, max_tiles):
    padded = ((counts + TMOE - 1) // TMOE) * TMOE
    ends = jnp.cumsum(padded)
    offsets = ends - padded
    n_tiles = (ends[-1] // TMOE).astype(jnp.int32)
    tile = jnp.minimum(jnp.arange(max_tiles, dtype=jnp.int32), jnp.maximum(n_tiles - 1, 0))
    tile_group = jnp.sum((tile[:, None] * TMOE >= ends[None, :]).astype(jnp.int32), axis=1)
    spare = ends[-1] + TMOE * jnp.arange(ZERO_RUNS - NG, dtype=jnp.int32)
    spare_len = jnp.where(spare < max_tiles * TMOE, TMOE, 0)
    zero_start = jnp.concatenate([offsets + counts, jnp.where(spare_len > 0, spare, 0)]).astype(jnp.int32)
    zero_len = jnp.concatenate([padded - counts, spare_len]).astype(jnp.int32)
    return offsets, tile, tile_group, n_tiles.reshape(1), (zero_start, zero_len)


def _pad_heads(w, perm=None, rotary_only=False):
    lead = w.shape[:-1]
    per = w.shape[-1] // NH
    w = w.reshape(lead + (NH, per))
    if perm is not None:
        nope = jnp.zeros_like(w[..., :NOPE]) if rotary_only else w[..., :NOPE]
        w = jnp.concatenate([nope, w[..., NOPE:][..., perm]], axis=-1)
    w = jnp.pad(w, [(0, 0)] * len(lead) + [(0, 0), (0, HP - per)])
    return w.reshape(lead + (NH * HP,))


def _pad_gain(g, perm, rotary_only=False):
    nope = jnp.zeros((NOPE,), F32) if rotary_only else g[:NOPE]
    g = jnp.concatenate([nope, g[NOPE:][perm], jnp.zeros((HP - QK,), F32)])
    return g.reshape(1, HP)


def _block_diag(w):
    per = BDW // LRU_BLOCK
    rows = w.reshape(DR // BDW, BDW, LRU_BLOCK)
    idx = np.arange(BDW) // LRU_BLOCK
    mask = jnp.asarray(idx[:, None] == idx[None, :], w.dtype)
    return jnp.concatenate([rows] * per, axis=-1) * mask


def _prepare_shared(l, p):
    w_in = lambda a, b: jnp.transpose(lax.slice(p["w_in"], (l, 0, a), (l + 1, D, b)).reshape(D, b - a))
    bd = jnp.stack([_block_diag(p["lru_wa"][l, 0]), _block_diag(p["lru_wx"][l, 0]),
                    _block_diag(p["lru_wa"][l, 1]), _block_diag(p["lru_wx"][l, 1])])
    bd = (0.5 * bd).astype(BF)
    wom = p["w_o_mla"][l].reshape(NH, VD, D)
    wom = jnp.pad(wom, ((0, 0), (0, HP - VD), (0, 0))).reshape(DH, D)
    router_w = jnp.concatenate([p["router_we"][l], p["router_wg"][l],
                                jnp.zeros((D, LANES - NE - NG), F32)], axis=1)
    router_b = jnp.concatenate([p["router_be"][l], p["router_bg"][l],
                                jnp.zeros((LANES - NE - NG,), F32)]).reshape(1, LANES)
    router_hi = router_w.astype(BF)
    router_lo = (router_w - router_hi.astype(F32)).astype(BF)
    router_w = jnp.concatenate([router_hi, router_lo], axis=1)
    return {
        "n1": p["norm1_g"][l].reshape(1, D), "n2": p["norm2_g"][l].reshape(1, D),
        "w_in_t": jnp.transpose(p["w_in"][l]).astype(BF),
        "w_kv": w_in(O_KV, O_KR), "w_kr": w_in(O_KR, O_GATE),
        "qan": p["q_a_norm"][l].reshape(1, QL), "kvan": p["kv_a_norm"][l].reshape(1, KVL),
        "w_uk": _pad_heads(p["w_uk"][l]).astype(BF),
        "w_uv": _pad_heads(p["w_uv"][l]).astype(BF),
        "conv_w": p["conv_w"][l], "conv_b": p["conv_b"][l].reshape(1, DR), "bd": bd,
        "lru_ba": 0.5 * p["lru_ba"][l], "lru_bx": 0.5 * p["lru_bx"][l], "lru_lam": p["lru_lam"][l],
        "w_o_rnn": p["w_o_rnn"][l].astype(BF), "w_o_mla": wom.astype(BF), "w_out": (0.5 * p["w_out"][l]).astype(BF),
        "router_w": router_w, "router_b": router_b,
        "exp_w1": p["exp_w1"][l], "exp_w3": p["exp_w3"][l], "exp_w2": p["exp_w2"][l],
    }


def _with_rope_order(l, p, shared, perm, rotary):
    w = dict(shared)
    zeros = lambda n: jnp.zeros((n, D), F32)
    rope_block = lambda order: [zeros(NOPE), shared["w_kr"][order, :], zeros(HP - QK)]
    kvr = [shared["w_kv"]] + rope_block(perm)
    w["w_uq"] = _pad_heads(p["w_uq"][l], perm).astype(BF)
    w["gq"] = _pad_gain(p["q_norm"][l], perm) * SCORE_SCALE
    w["gk"] = _pad_gain(p["k_norm"][l], perm)
    if rotary:
        pair = np.concatenate([perm[ROPE // 2:], perm[:ROPE // 2]])
        kvr += rope_block(pair)
        w["w_uq_pair"] = _pad_heads(p["w_uq"][l], pair, rotary_only=True).astype(BF)
        w["gq_pair"] = _pad_gain(p["q_norm"][l], pair, rotary_only=True) * SCORE_SCALE
        w["gk_pair"] = _pad_gain(p["k_norm"][l], pair, rotary_only=True)
    w["w_kvr"] = jnp.concatenate(kvr, axis=0).astype(BF)
    return w


def _rope_tables(n_tokens):
    rows = n_tokens // GRID_W
    row = np.repeat(np.arange(rows), GRID_W).astype(np.float32)
    col = np.tile(np.arange(GRID_W), rows).astype(np.float32)
    axis_dim = ROPE // 2
    inv = (np.float32(ROPE_BASE) ** (-np.arange(0, axis_dim, 2, dtype=np.float32) / axis_dim)).astype(np.float32)
    ang = np.concatenate([row[:, None] * inv, col[:, None] * inv], axis=-1).astype(np.float32)
    cos, sin = np.cos(ang), np.sin(ang)
    ones = lambda n: np.ones((n_tokens, n), np.float32)
    zeros = lambda n: np.zeros((n_tokens, n), np.float32)
    cos_t = np.concatenate([ones(NOPE), cos, cos, ones(HP - QK)], axis=1)
    sin_t = np.concatenate([zeros(NOPE), -sin, sin, zeros(HP - QK)], axis=1)
    return jnp.asarray(cos_t, F32), jnp.asarray(sin_t, F32)


def kernel(x_prompt, x_sample, cache_mla_ckv, cache_mla_krope, state_rglru, c, c_ctx, norm1_g, norm2_g, w_mod, b_mod, w_in, conv_w, conv_b, lru_wa, lru_ba, lru_wx, lru_bx, lru_lam, q_a_norm, kv_a_norm, w_uq, w_uk, w_uv, q_norm, k_norm, w_o_rnn, w_o_mla, w_out, router_wg, router_bg, router_we, router_be, exp_w1, exp_w3, exp_w2):
    p = dict(norm1_g=norm1_g, norm2_g=norm2_g, w_in=w_in, conv_w=conv_w, conv_b=conv_b,
             lru_wa=lru_wa, lru_ba=lru_ba, lru_wx=lru_wx, lru_bx=lru_bx, lru_lam=lru_lam,
             q_a_norm=q_a_norm, kv_a_norm=kv_a_norm, w_uq=w_uq, w_uk=w_uk, w_uv=w_uv,
             q_norm=q_norm, k_norm=k_norm, w_o_rnn=w_o_rnn, w_o_mla=w_o_mla, w_out=w_out,
             router_wg=router_wg, router_bg=router_bg, router_we=router_we, router_be=router_be,
             exp_w1=exp_w1, exp_w3=exp_w3, exp_w2=exp_w2)
    depth = w_in.shape[0]
    nb, seq, _ = x_prompt.shape
    db, dseq, _ = x_sample.shape
    ident = np.arange(ROPE)
    halves = np.concatenate([np.arange(0, ROPE, 2), np.arange(1, ROPE, 2)])
    rope_tabs = _rope_tables(dseq)
    cond8 = jnp.concatenate([c_ctx[None, :], c, jnp.zeros((SUBLANES - 1 - db, D), F32)], axis=0)
    ctx_row = lambda tile_rows: (lambda i: 0)
    lat_row = lambda tile_rows: (lambda i: (i * tile_rows) // dseq + 1)
    n_ctx, n_lat = nb * seq, db * dseq
    run_padding = ((n_ctx + n_lat) // TM) * NG * (RUN_ALIGN - 1)
    max_tiles = -(-(n_ctx + n_lat + run_padding) // TMOE) + NG
    assert max_tiles - (n_ctx + n_lat) // TMOE <= ZERO_RUNS - NG, "more unused expert tiles than zero runs"
    per_seq = lambda arrs, b, t: [a.reshape(b, t, a.shape[-1]) for a in arrs]
    flat = lambda a: a.reshape(-1, a.shape[-1])

    y_prompt, y_sample = x_prompt.reshape(n_ctx, D), x_sample.reshape(n_lat, D)
    ckv_list, krope_list, rnn_list = [], [], []
    for l in range(depth):
        shared = _prepare_shared(l, p)
        w_ctx = _with_rope_order(l, p, shared, ident, False)
        w_lat = _with_rope_order(l, p, shared, halves, True)
        mod = _modulation(cond8, w_mod[l], b_mod[l]).reshape(SUBLANES, 6, D)

        h, xr, gg, q, k, v, ckv, kro = _projections(y_prompt, mod, ctx_row(TM), w_ctx, None, True)
        xr, gg, q, k, v = per_seq([xr, gg, q, k, v], nb, seq)
        yr, h_fin = _rglru(xr, gg, shared, None, True)
        ya = _attention(q, k, v, None, None, 4, NH)
        x1_c, hx_c, ri_c, tc_c = _merge_out(y_prompt, h, flat(yr), flat(ya), mod, ctx_row(TM), shared)
        ckv_list.append(ckv.reshape(nb, seq, KVL))
        krope_list.append(kro.reshape(nb, seq, ROPE))
        rnn_list.append(h_fin)

        krp_cache = jnp.pad(cache_mla_krope[:, l][..., halves], ((0, 0), (0, 0), (NOPE, HP - QK)))
        kc, vc = _cache_keys_values(cache_mla_ckv[:, l], krp_cache, w_lat)
        h, xr, gg, q, k, v = _projections(y_sample, mod, lat_row(TM), w_lat, rope_tabs, False)
        xr, gg, q, k, v = per_seq([xr, gg, q, k, v], db, dseq)
        yr, _ = _rglru(xr, gg, shared, state_rglru[:, l], False)
        ya = _attention(q, k, v, kc, vc, 1, 4)
        x1_l, hx_l, ri_l, tc_l = _merge_out(y_sample, h, flat(yr), flat(ya), mod, lat_row(TM), shared)

        len_c, len_l = _run_lengths(tc_c), _run_lengths(tc_l)
        rows_c = jnp.sum(len_c, axis=0)
        offsets, tile_block, tile_group, n_tiles, zero_runs = _group_layout(
            rows_c + jnp.sum(len_l, axis=0), max_tiles)
        runs_c = _run_tables(len_c, offsets)
        runs_l = _run_tables(len_l, offsets + rows_c)
        runs = tuple(jnp.concatenate([a, b]) for a, b in zip(runs_c, runs_l))
        xs = _dispatch(runs, zero_runs, hx_c, hx_l, max_tiles * TMOE)
        ys = _experts(tile_block, tile_group, n_tiles, xs, shared)
        y_prompt = _combine(runs_c, x1_c, ri_c, mod, ctx_row(TM), ys)
        y_sample = _combine(runs_l, x1_l, ri_l, mod, lat_row(TM), ys)

    y_prompt, y_sample = y_prompt.reshape(nb, seq, D), y_sample.reshape(db, dseq, D)

    return (y_prompt, y_sample, jnp.stack(ckv_list, axis=1), jnp.stack(krope_list, axis=1),
            jnp.stack(rnn_list, axis=1))
```
